```python
import math
import jax, jax.numpy as jnp
from jax import lax
import numpy as np

D_MODEL = 1024
BATCH = 16
SEQ = 256
DEPTH = 2
DEC_BATCH = 2
DEC_SEQ = 4096
PAST_LEN = 512

GRID_W = 64
HEAD_DIM = 64
ROPE_FREQS = HEAD_DIM // 4
ROPE_BASE = 10000.0
EPS = 1e-6
NEG_INF = -1e30
Q_BLOCK = 128
WINDOW = 128
BAND = Q_BLOCK + 2 * WINDOW
A_WIDTH = D_MODEL // 2
B_WIDTH = D_MODEL // 2
POOL_WINDOWS = (2, 4, 8, 16)
N_POOL_GROUPS = len(POOL_WINDOWS)
POOL_GW = A_WIDTH // N_POOL_GROUPS
N_FFT_GROUPS = 4
FFT_GW = B_WIDTH // N_FFT_GROUPS
EVEN_WIDTH = A_WIDTH + B_WIDTH
C_HEADS = D_MODEL // 256
D_HEADS = D_MODEL // 128
D_KV_HEADS = 2
D_GROUP = D_HEADS // D_KV_HEADS
C_Q = C_HEADS * 2 * HEAD_DIM
C_K = C_HEADS * 2 * HEAD_DIM
C_V = C_HEADS * 2 * HEAD_DIM
D_Q = D_HEADS * HEAD_DIM
D_K = D_KV_HEADS * HEAD_DIM
D_V = D_KV_HEADS * HEAD_DIM
ODD_SPLITS = (C_Q, C_Q + C_K, C_Q + C_K + C_V, C_Q + C_K + C_V + D_Q, C_Q + C_K + C_V + D_Q + D_K)
ODD_IN = C_Q + C_K + C_V + D_Q + D_K + D_V
ODD_OUT = C_V + D_Q
D_FF = 4 * D_MODEL
N_EVEN = (DEPTH + 1) // 2
N_ODD = DEPTH // 2
ATTN_SCALE = HEAD_DIM ** -0.5

kernel_name = "hybrid_diffusion_prefix_trunk_step"


def _rms(x, g):
    xf = x.astype(jnp.float32)
    y = xf * lax.rsqrt(jnp.mean(xf * xf, axis=-1, keepdims=True) + EPS)
    return y.astype(x.dtype) * g


def _adaln(cond, w, b):
    m = jax.nn.silu(cond) @ w + b
    return jnp.split(m[:, None, :], 6, axis=-1)


def _channel_mixer(h, w1, w2):
    return jnp.square(jax.nn.relu(h @ w1)) @ w2


def _pool_mix(u, w_pool, scale):
    B, N, _ = u.shape
    ug = u.reshape(B, N, N_POOL_GROUPS, POOL_GW)
    cs = jnp.concatenate([jnp.zeros((B, 1, N_POOL_GROUPS, POOL_GW), jnp.float32),
                          jnp.cumsum(ug.astype(jnp.float32), axis=1)], axis=1)
    t = jnp.arange(N)
    outs = []
    for gi, w in enumerate(POOL_WINDOWS):
        lo = jnp.clip(t - w // 2, 0, N)
        hi = jnp.clip(t + w // 2, 0, N)
        csg = cs[:, :, gi]
        cnt = (hi - lo).astype(jnp.float32)
        outs.append((csg[:, hi] - csg[:, lo]) / cnt[None, :, None])
    pooled = jnp.stack(outs, axis=2).astype(u.dtype) - ug
    y = jnp.einsum('bngc,gcd->bngd', pooled, w_pool)
    return y.reshape(B, N, A_WIDTH) * scale


def _fourier_mix(u, w_fft):
    B, N, _ = u.shape
    uf = u.reshape(B, N, N_FFT_GROUPS, FFT_GW).astype(jnp.float32)
    f = jnp.fft.fft2(uf, axes=(1, 3)).real * ((N * FFT_GW) ** -0.5)
    y = jnp.einsum('bngc,gcd->bngd', f.astype(u.dtype), w_fft)
    return y.reshape(B, N, B_WIDTH)


def _even_mixer(h, w_in, w_pool, pool_scale, w_fft, w_out):
    u = h @ w_in
    a = _pool_mix(u[..., :A_WIDTH], w_pool, pool_scale)
    b = _fourier_mix(u[..., A_WIDTH:], w_fft)
    return jnp.concatenate([a, b], axis=-1) @ w_out


def _axial_rope(n):
    rows = n // GRID_W
    row = jnp.repeat(jnp.arange(rows), GRID_W).astype(jnp.float32)
    col = jnp.tile(jnp.arange(GRID_W), rows).astype(jnp.float32)
    freqs = ROPE_BASE ** (-jnp.arange(ROPE_FREQS, dtype=jnp.float32) / ROPE_FREQS)
    ang = jnp.stack([row[:, None] * freqs, col[:, None] * freqs], axis=1)
    ang = jnp.concatenate([ang, ang], axis=-1).reshape(n, HEAD_DIM)
    return jnp.cos(ang), jnp.sin(ang)


def _apply_rope(x, cos, sin):
    shp = x.shape
    xa = x.reshape(shp[:-1] + (2, 2, ROPE_FREQS))
    rot = jnp.stack([-xa[..., 1, :], xa[..., 0, :]], axis=-2).reshape(shp)
    bshape = (1, shp[1]) + (1,) * (x.ndim - 3) + (HEAD_DIM,)
    return x * cos.reshape(bshape).astype(x.dtype) + rot * sin.reshape(bshape).astype(x.dtype)


def _odd_qkv(h, w_in, c_qn, c_kn, d_qn, d_kn):
    B, N, _ = h.shape
    cq, ck, cv, dq, dk, dv = jnp.split(h @ w_in, ODD_SPLITS, axis=-1)
    cq = _rms(cq.reshape(B, N, C_HEADS, 2, HEAD_DIM), c_qn)
    ck = _rms(ck.reshape(B, N, C_HEADS, 2, HEAD_DIM), c_kn)
    cv = cv.reshape(B, N, C_HEADS, 2 * HEAD_DIM)
    dq = _rms(dq.reshape(B, N, D_KV_HEADS, D_GROUP, HEAD_DIM), d_qn)
    dk = _rms(dk.reshape(B, N, D_KV_HEADS, HEAD_DIM), d_kn)
    dv = dv.reshape(B, N, D_KV_HEADS, HEAD_DIM)
    return cq, ck, cv, dq, dk, dv


def _map_query_blocks(fn, q):
    B, N = q.shape[0], q.shape[1]
    nb = N // Q_BLOCK
    qb = jnp.moveaxis(q.reshape((B, nb, Q_BLOCK) + q.shape[2:]), 1, 0)
    out = lax.map(lambda args: fn(*args), (jnp.arange(nb), qb))
    out = jnp.moveaxis(out, 0, 1)
    return out.reshape((B, N) + out.shape[3:])


def _diff_lambda(lq1, lk1, lq2, lk2, lam_init):
    f = jnp.float32
    return (jnp.exp(jnp.sum(lq1.astype(f) * lk1.astype(f))) -
            jnp.exp(jnp.sum(lq2.astype(f) * lk2.astype(f))) + lam_init)


def _diff_attend_block(qb, k, v, lam):
    s = jnp.einsum('bqhcd,bkhcd->bhcqk', qb, k).astype(jnp.float32) * ATTN_SCALE
    p = jax.nn.softmax(s, axis=-1)
    p = p[:, :, 0] - lam * p[:, :, 1]
    return jnp.einsum('bhqk,bkhe->bqhe', p.astype(v.dtype), v)


def _sink_attend_block(qb, k, v, sink, mask):
    s = jnp.einsum('bqjgd,bkjd->bjgqk', qb, k).astype(jnp.float32) * ATTN_SCALE
    if mask is not None:
        s = jnp.where(mask, s, NEG_INF)
    sk = jnp.broadcast_to(sink.astype(jnp.float32)[None, :, :, None, None], s.shape[:-1] + (1,))
    p = jax.nn.softmax(jnp.concatenate([s, sk], axis=-1), axis=-1)[..., :-1]
    return jnp.einsum('bjgqk,bkjd->bqjgd', p.astype(v.dtype), v)


def _window_attention(q, k, v, ctx_k, ctx_v, sink):
    N = q.shape[1]
    L = ctx_k.shape[1]
    pad = ((0, 0), (WINDOW, WINDOW), (0, 0), (0, 0))
    kp = jnp.pad(k, pad)
    vp = jnp.pad(v, pad)
    qq = jnp.arange(Q_BLOCK)[:, None]
    kk = jnp.arange(BAND)[None, :]
    band_ok = jnp.abs(qq + WINDOW - kk) <= WINDOW
    ctx_ok = jnp.ones((Q_BLOCK, L), bool)

    def fn(b, qb):
        start = b * Q_BLOCK
        kb = lax.dynamic_slice_in_dim(kp, start, BAND, axis=1)
        vb = lax.dynamic_slice_in_dim(vp, start, BAND, axis=1)
        jpos = start - WINDOW + kk
        valid = band_ok & (jpos >= 0) & (jpos < N)
        mask = jnp.concatenate([ctx_ok, valid], axis=1)
        return _sink_attend_block(qb, jnp.concatenate([ctx_k, kb], axis=1),
                                  jnp.concatenate([ctx_v, vb], axis=1), sink, mask)

    return _map_query_blocks(fn, q)


def _odd_output(c_out, d_out, subln_g, lam_init, w_out):
    B, N = c_out.shape[0], c_out.shape[1]
    c = (_rms(c_out, subln_g) * (1.0 - lam_init)).reshape(B, N, C_V)
    d = d_out.reshape(B, N, D_Q)
    return jnp.concatenate([c, d], axis=-1) @ w_out


def _lam_init(layer):
    return 0.8 - 0.6 * math.exp(-0.3 * layer)


def setup_inputs(seed: int = 0) -> dict:
    key = jax.random.key(seed)
    ks = iter(jax.random.split(key, 40))
    f = jnp.float32

    def nrm(shape, s):
        return jax.random.normal(next(ks), shape, f) * s

    def gain(shape):
        return 1.0 + nrm(shape, 0.02)

    return {
        "x_prompt": nrm((BATCH, SEQ, D_MODEL), 1.0),
        "x_sample": nrm((DEC_BATCH, DEC_SEQ, D_MODEL), 1.0),
        "c": nrm((DEC_BATCH, D_MODEL), 1.0),
        "cache_c_k": nrm((DEC_BATCH, N_ODD, PAST_LEN, C_HEADS, 2, HEAD_DIM), 1.0),
        "cache_c_v": nrm((DEC_BATCH, N_ODD, PAST_LEN, C_HEADS, 2 * HEAD_DIM), 1.0),
        "cache_d_k": nrm((DEC_BATCH, N_ODD, PAST_LEN, D_KV_HEADS, HEAD_DIM), 1.0),
        "cache_d_v": nrm((DEC_BATCH, N_ODD, PAST_LEN, D_KV_HEADS, HEAD_DIM), 1.0),
        "c_ctx": nrm((D_MODEL,), 1.0),
        "norm1_g": gain((DEPTH, D_MODEL)),
        "norm2_g": gain((DEPTH, D_MODEL)),
        "w_ada": nrm((DEPTH, D_MODEL, 6 * D_MODEL), 0.5 * D_MODEL ** -0.5),
        "b_ada": nrm((DEPTH, 6 * D_MODEL), 0.01),
        "w_in_even": nrm((N_EVEN, D_MODEL, EVEN_WIDTH), D_MODEL ** -0.5),
        "w_pool": nrm((N_EVEN, N_POOL_GROUPS, POOL_GW, POOL_GW), POOL_GW ** -0.5),
        "pool_scale": gain((N_EVEN, A_WIDTH)),
        "w_fft": nrm((N_EVEN, N_FFT_GROUPS, FFT_GW, FFT_GW), FFT_GW ** -0.5),
        "w_out_even": nrm((N_EVEN, EVEN_WIDTH, D_MODEL), EVEN_WIDTH ** -0.5),
        "w_in_odd": nrm((N_ODD, D_MODEL, ODD_IN), D_MODEL ** -0.5),
        "c_qn_g": gain((N_ODD, HEAD_DIM)),
        "c_kn_g": gain((N_ODD, HEAD_DIM)),
        "lam_q1": nrm((N_ODD, HEAD_DIM), 0.1),
        "lam_k1": nrm((N_ODD, HEAD_DIM), 0.1),
        "lam_q2": nrm((N_ODD, HEAD_DIM), 0.1),
        "lam_k2": nrm((N_ODD, HEAD_DIM), 0.1),
        "c_subln_g": gain((N_ODD, 2 * HEAD_DIM)),
        "d_qn_g": gain((N_ODD, HEAD_DIM)),
        "d_kn_g": gain((N_ODD, HEAD_DIM)),
        "d_sink": nrm((N_ODD, D_HEADS), 0.5),
        "w_out_odd": nrm((N_ODD, ODD_OUT, D_MODEL), ODD_OUT ** -0.5),
        "w_mlp1": nrm((DEPTH, D_MODEL, D_FF), D_MODEL ** -0.5),
        "w_mlp2": nrm((DEPTH, D_FF, D_MODEL), D_FF ** -0.5),
    }


def reference(x_prompt, x_sample, c, cache_c_k, cache_c_v, cache_d_k, cache_d_v, c_ctx,
              norm1_g, norm2_g, w_ada, b_ada, w_in_even, w_pool, pool_scale, w_fft, w_out_even,
              w_in_odd, c_qn_g, c_kn_g, lam_q1, lam_k1, lam_q2, lam_k2, c_subln_g,
              d_qn_g, d_kn_g, d_sink, w_out_odd, w_mlp1, w_mlp2):
    x = x_prompt
    cond_ctx = c_ctx[None, :]
    st_ck, st_cv, st_dk, st_dv = [], [], [], []
    for l in range(DEPTH):
        sh1, sc1, g1, sh2, sc2, g2 = _adaln(cond_ctx, w_ada[l], b_ada[l])
        h = _rms(x, norm1_g[l]) * (1.0 + sc1) + sh1
        if l % 2 == 0:
            e = l // 2
            mix = _even_mixer(h, w_in_even[e], w_pool[e], pool_scale[e], w_fft[e], w_out_even[e])
        else:
            o = l // 2
            cq, ck, cv, dq, dk, dv = _odd_qkv(h, w_in_odd[o], c_qn_g[o], c_kn_g[o], d_qn_g[o], d_kn_g[o])
            lam = _diff_lambda(lam_q1[o], lam_k1[o], lam_q2[o], lam_k2[o], _lam_init(l))
            sink = d_sink[o].reshape(D_KV_HEADS, D_GROUP)
            c_out = _map_query_blocks(lambda b, qb: _diff_attend_block(qb, ck, cv, lam), cq)
            d_out = _map_query_blocks(lambda b, qb: _sink_attend_block(qb, dk, dv, sink, None), dq)
            mix = _odd_output(c_out, d_out, c_subln_g[o], _lam_init(l), w_out_odd[o])
            st_ck.append(ck)
            st_cv.append(cv)
            st_dk.append(dk)
            st_dv.append(dv)
        x = x + g1 * mix
        h = _rms(x, norm2_g[l]) * (1.0 + sc2) + sh2
        x = x + g2 * _channel_mixer(h, w_mlp1[l], w_mlp2[l])
    y_prompt = x
    new_c_k = jnp.stack(st_ck, axis=1)
    new_c_v = jnp.stack(st_cv, axis=1)
    new_d_k = jnp.stack(st_dk, axis=1)
    new_d_v = jnp.stack(st_dv, axis=1)

    x = x_sample
    n_lat = x.shape[1]
    cos, sin = _axial_rope(n_lat)
    for l in range(DEPTH):
        sh1, sc1, g1, sh2, sc2, g2 = _adaln(c, w_ada[l], b_ada[l])
        h = _rms(x, norm1_g[l]) * (1.0 + sc1) + sh1
        if l % 2 == 0:
            e = l // 2
            mix = _even_mixer(h, w_in_even[e], w_pool[e], pool_scale[e], w_fft[e], w_out_even[e])
        else:
            o = l // 2
            cq, ck, cv, dq, dk, dv = _odd_qkv(h, w_in_odd[o], c_qn_g[o], c_kn_g[o], d_qn_g[o], d_kn_g[o])
            cq = _apply_rope(cq, cos, sin)
            ck = _apply_rope(ck, cos, sin)
            dq = _apply_rope(dq, cos, sin)
            dk = _apply_rope(dk, cos, sin)
            lam = _diff_lambda(lam_q1[o], lam_k1[o], lam_q2[o], lam_k2[o], _lam_init(l))
            sink = d_sink[o].reshape(D_KV_HEADS, D_GROUP)
            k_all = jnp.concatenate([cache_c_k[:, o], ck], axis=1)
            v_all = jnp.concatenate([cache_c_v[:, o], cv], axis=1)
            c_out = _map_query_blocks(lambda b, qb: _diff_attend_block(qb, k_all, v_all, lam), cq)
            d_out = _window_attention(dq, dk, dv, cache_d_k[:, o], cache_d_v[:, o], sink)
            mix = _odd_output(c_out, d_out, c_subln_g[o], _lam_init(l), w_out_odd[o])
        x = x + g1 * mix
        h = _rms(x, norm2_g[l]) * (1.0 + sc2) + sh2
        x = x + g2 * _channel_mixer(h, w_mlp1[l], w_mlp2[l])
    y_sample = x
    return (y_prompt, y_sample, new_c_k, new_c_v, new_d_k, new_d_v)
```

```python
import functools
import math

import numpy as np
import jax
import jax.numpy as jnp
from jax import lax
from jax.experimental import pallas as pl
from jax.experimental.pallas import tpu as pltpu

F32 = jnp.float32
BF16 = jnp.bfloat16

D_MODEL = 1024
HEAD_DIM = 64
LANES = 128
SUBLANES = 8
GRID_W = 64
ROPE_FREQS = HEAD_DIM // 4
ROPE_BASE = 10000.0
EPS = 1e-6
NEG_INF = -1e30
WINDOW = 128
A_WIDTH = D_MODEL // 2
B_WIDTH = D_MODEL // 2
POOL_WINDOWS = (2, 4, 8, 16)
POOL_HALO = 8
N_GROUPS = 4
GROUP_W = A_WIDTH // N_GROUPS
C_HEADS = 4
D_HEADS = 8
D_KV_HEADS = 2
D_GROUP = D_HEADS // D_KV_HEADS
C_W = C_HEADS * 2 * HEAD_DIM
DQ_W = D_HEADS * HEAD_DIM
DKV_W = D_KV_HEADS * HEAD_DIM
DKV_DUP_W = 2 * DKV_W
D_FF = 4 * D_MODEL
ATTN_SCALE = HEAD_DIM ** -0.5
ODD_NORMED_W = C_W + C_W + DQ_W + DKV_DUP_W
ODD_W = ODD_NORMED_W + C_W + DKV_DUP_W
VMEM_LIMIT = 56 * 1024 * 1024


def _lam_init(layer):
    return 0.8 - 0.6 * math.exp(-0.3 * layer)


def _params(n_axes):
    return pltpu.CompilerParams(dimension_semantics=("arbitrary",) * n_axes,
                                vmem_limit_bytes=VMEM_LIMIT)


def _const_spec(shape):
    nd = len(shape)
    return pl.BlockSpec(shape, lambda *_: (0,) * nd, pipeline_mode=pl.Buffered(1))


def _mod_spec(layer, which, row_of_batch):
    return pl.BlockSpec((None, None, None, 1, D_MODEL),
                        lambda b, i: (layer, row_of_batch(b), which, 0, 0))


def _rms_mod(x, g, sc, sh):
    ms = jnp.mean(x * x, axis=-1, keepdims=True)
    return (x * lax.rsqrt(ms + EPS) * g) * (1.0 + sc) + sh


def _bf16_const(a):
    return jnp.asarray(a, F32).astype(BF16)


def _dot_nt(a, b):
    return lax.dot_general(a, b, (((1,), (1,)), ((), ())), preferred_element_type=F32)


def _dot(a, b):
    return jnp.dot(a, b, preferred_element_type=F32)


def _adaln_kernel(cond_ref, w_ref, b_ref, o_ref):
    cnd = cond_ref[...]
    s = (cnd * jax.nn.sigmoid(cnd)).astype(BF16)
    o_ref[...] = _dot(s, w_ref[...].astype(BF16)) + b_ref[...]


def _adaln_call(cond, w_ada, b_ada):
    depth = w_ada.shape[0]
    rows = cond.shape[0]
    tn = 1536
    return pl.pallas_call(
        _adaln_kernel,
        grid=(depth, 6 * D_MODEL // tn),
        in_specs=[
            pl.BlockSpec((rows, D_MODEL), lambda l, j: (0, 0)),
            pl.BlockSpec((None, D_MODEL, tn), lambda l, j: (l, 0, j)),
            pl.BlockSpec((None, 1, tn), lambda l, j: (l, 0, j)),
        ],
        out_specs=pl.BlockSpec((None, rows, tn), lambda l, j: (l, 0, j)),
        out_shape=jax.ShapeDtypeStruct((depth, rows, 6 * D_MODEL), F32),
        compiler_params=_params(2),
        name="adaln",
    )(cond, w_ada, b_ada.reshape(depth, 1, 6 * D_MODEL))


def _even_in_kernel(x_ref, xp_ref, xn_ref, sh_ref, sc_ref, g_ref, win_ref, wpool_ref,
                    pscale_ref, dftc_ref, wfft_ref, a_ref, z_ref, ext_ref, *, tm, n):
    i = pl.program_id(1)
    n_tiles = n // tm
    g, sc, sh = g_ref[...], sc_ref[...], sh_ref[...]
    h = _rms_mod(x_ref[...], g, sc, sh).astype(BF16)
    u = _dot(h, win_ref[...])
    xh = jnp.concatenate([xp_ref[...], xn_ref[...]], axis=0)
    hh = _rms_mod(xh, g, sc, sh).astype(BF16)
    uh = _dot(hh, win_ref[:, :A_WIDTH])
    has_prev = (i > 0).astype(F32)
    has_next = (i < n_tiles - 1).astype(F32)
    ext_ref[0:POOL_HALO, :] = uh[0:POOL_HALO] * has_prev
    ext_ref[POOL_HALO:POOL_HALO + tm, :] = u[:, :A_WIDTH]
    ext_ref[POOL_HALO + tm:, :] = uh[POOL_HALO:] * has_next

    t = i * tm + lax.broadcasted_iota(jnp.int32, (tm, 1), 0)
    for gi, w in enumerate(POOL_WINDOWS):
        cols = slice(gi * GROUP_W, (gi + 1) * GROUP_W)
        acc = None
        for j in range(-(w // 2), w // 2):
            v = ext_ref[pl.ds(POOL_HALO + j, tm), cols]
            acc = v if acc is None else acc + v
        lo = jnp.maximum(t - w // 2, 0)
        hi = jnp.minimum(t + w // 2, n)
        cnt = (hi - lo).astype(F32)
        pooled = acc / cnt - u[:, cols]
        y = _dot(pooled.astype(BF16), wpool_ref[gi]) * pscale_ref[:, cols]
        a_ref[:, cols] = y.astype(BF16)

    for gi in range(N_GROUPS):
        ub = u[:, A_WIDTH + gi * GROUP_W:A_WIDTH + (gi + 1) * GROUP_W].astype(BF16)
        cs = _dot(ub, dftc_ref[...])
        zr = _dot(cs[:, :GROUP_W].astype(BF16), wfft_ref[gi])
        zi = _dot(cs[:, GROUP_W:].astype(BF16), wfft_ref[gi])
        z_ref[:, gi * GROUP_W:(gi + 1) * GROUP_W] = zr.astype(BF16)
        z_ref[:, B_WIDTH + gi * GROUP_W:B_WIDTH + (gi + 1) * GROUP_W] = zi.astype(BF16)


def _even_in_call(x, mod, layer, row_of_batch, g, w_in, w_pool, pool_scale, dftc, w_fft, tm):
    bsz, n, _ = x.shape
    hb = tm // POOL_HALO
    kern = functools.partial(_even_in_kernel, tm=tm, n=n)
    return pl.pallas_call(
        kern,
        grid=(bsz, n // tm),
        in_specs=[
            pl.BlockSpec((None, tm, D_MODEL), lambda b, i: (b, i, 0)),
            pl.BlockSpec((None, POOL_HALO, D_MODEL),
                         lambda b, i: (b, jnp.maximum(i * hb - 1, 0), 0)),
            pl.BlockSpec((None, POOL_HALO, D_MODEL),
                         lambda b, i: (b, jnp.minimum((i + 1) * hb, n // POOL_HALO - 1), 0)),
            _mod_spec(layer, 0, row_of_batch),
            _mod_spec(layer, 1, row_of_batch),
            _const_spec((1, D_MODEL)),
            _const_spec((D_MODEL, D_MODEL)),
            _const_spec((N_GROUPS, GROUP_W, GROUP_W)),
            _const_spec((1, A_WIDTH)),
            _const_spec((GROUP_W, 2 * GROUP_W)),
            _const_spec((N_GROUPS, GROUP_W, GROUP_W)),
        ],
        out_specs=[
            pl.BlockSpec((None, tm, A_WIDTH), lambda b, i: (b, i, 0)),
            pl.BlockSpec((None, tm, 2 * B_WIDTH), lambda b, i: (b, i, 0)),
        ],
        out_shape=[
            jax.ShapeDtypeStruct((bsz, n, A_WIDTH), BF16),
            jax.ShapeDtypeStruct((bsz, n, 2 * B_WIDTH), BF16),
        ],
        scratch_shapes=[pltpu.VMEM((tm + 2 * POOL_HALO, A_WIDTH), F32)],
        compiler_params=_params(2),
        name="even_in",
    )(x, x, x, mod, mod, g, w_in, w_pool, pool_scale, dftc, w_fft)


def _dft_direct_kernel(z_ref, m_ref, y_ref, *, scale):
    z = z_ref[...]
    rhs = jnp.concatenate([z[:, :B_WIDTH], z[:, B_WIDTH:]], axis=0)
    y_ref[...] = (_dot(m_ref[...], rhs) * scale).astype(BF16)


def _dft_direct_call(z):
    bsz, n, _ = z.shape
    k = np.arange(n)
    ang = 2.0 * np.pi * np.outer(k, k) / n
    m = _bf16_const(np.concatenate([np.cos(ang), np.sin(ang)], axis=1))
    kern = functools.partial(_dft_direct_kernel, scale=float((n * GROUP_W) ** -0.5))
    return pl.pallas_call(
        kern,
        grid=(bsz,),
        in_specs=[pl.BlockSpec((None, n, 2 * B_WIDTH), lambda b: (b, 0, 0)),
                  _const_spec((n, 2 * n))],
        out_specs=pl.BlockSpec((None, n, B_WIDTH), lambda b: (b, 0, 0)),
        out_shape=jax.ShapeDtypeStruct((bsz, n, B_WIDTH), BF16),
        compiler_params=_params(1),
        name="dft_direct",
    )(z, m)


DFT_ROWS = 8
DFT_DIRECT_MAX = 256


def _dft_stage1_kernel(z_ref, m_ref, twr_ref, twi_ref, o_ref, *, na):
    for bi in range(DFT_ROWS):
        zb = z_ref[bi]
        rhs = jnp.concatenate([zb[:, :B_WIDTH], zb[:, B_WIDTH:]], axis=0)
        r = _dot(m_ref[...], rhs)
        br, bim = r[:na], r[na:]
        tr, ti = twr_ref[bi], twi_ref[bi]
        o_ref[bi, :, :B_WIDTH] = (br * tr - bim * ti).astype(BF16)
        o_ref[bi, :, B_WIDTH:] = (br * ti + bim * tr).astype(BF16)


def _dft_stage2_kernel(b_ref, m_ref, y_ref, *, scale):
    for di in range(DFT_ROWS):
        bb = b_ref[di]
        rhs = jnp.concatenate([bb[:, :B_WIDTH], bb[:, B_WIDTH:]], axis=0)
        y_ref[di] = (_dot(m_ref[...], rhs) * scale).astype(BF16)


def _dft_two_stage_call(z):
    bsz, n, _ = z.shape
    na = 1 << (int(math.log2(n)) // 2)
    nb = n // na
    ia = np.arange(na)
    ib = np.arange(nb)
    ang_a = 2.0 * np.pi * np.outer(ia, ia) / na
    fr, fi = np.cos(ang_a), -np.sin(ang_a)
    m1 = _bf16_const(np.block([[fr, -fi], [fi, fr]]))
    ang_t = 2.0 * np.pi * np.outer(ib, ia) / n
    twr = jnp.asarray(np.cos(ang_t)[:, :, None], F32)
    twi = jnp.asarray(-np.sin(ang_t)[:, :, None], F32)
    ang_b = 2.0 * np.pi * np.outer(ib, ib) / nb
    m2 = _bf16_const(np.concatenate([np.cos(ang_b), np.sin(ang_b)], axis=1))

    zt = z.reshape(bsz, na, nb, 2 * B_WIDTH).transpose(0, 2, 1, 3)
    s1 = pl.pallas_call(
        functools.partial(_dft_stage1_kernel, na=na),
        grid=(bsz, nb // DFT_ROWS),
        in_specs=[
            pl.BlockSpec((None, DFT_ROWS, na, 2 * B_WIDTH), lambda b, j: (b, j, 0, 0)),
            _const_spec((2 * na, 2 * na)),
            pl.BlockSpec((DFT_ROWS, na, 1), lambda b, j: (j, 0, 0)),
            pl.BlockSpec((DFT_ROWS, na, 1), lambda b, j: (j, 0, 0)),
        ],
        out_specs=pl.BlockSpec((None, DFT_ROWS, na, 2 * B_WIDTH), lambda b, j: (b, j, 0, 0)),
        out_shape=jax.ShapeDtypeStruct((bsz, nb, na, 2 * B_WIDTH), BF16),
        compiler_params=_params(2),
        name="dft_stage1",
    )(zt, m1, twr, twi)
    bt = s1.transpose(0, 2, 1, 3)
    yt = pl.pallas_call(
        functools.partial(_dft_stage2_kernel, scale=float((n * GROUP_W) ** -0.5)),
        grid=(bsz, na // DFT_ROWS),
        in_specs=[
            pl.BlockSpec((None, DFT_ROWS, nb, 2 * B_WIDTH), lambda b, j: (b, j, 0, 0)),
            _const_spec((nb, 2 * nb)),
        ],
        out_specs=pl.BlockSpec((None, DFT_ROWS, nb, B_WIDTH), lambda b, j: (b, j, 0, 0)),
        out_shape=jax.ShapeDtypeStruct((bsz, na, nb, B_WIDTH), BF16),
        compiler_params=_params(2),
        name="dft_stage2",
    )(bt, m2)
    return yt.transpose(0, 2, 1, 3).reshape(bsz, n, B_WIDTH)


FF_CHUNK = 1024


def _out_mlp_kernel(x_ref, p1_ref, p2_ref, g1_ref, sh_ref, sc_ref, g2_ref, ng_ref,
                    wout_ref, w1_ref, w2_ref, o_ref):
    half = p1_ref.shape[-1]
    mix = _dot(p1_ref[...], wout_ref[:half, :]) + _dot(p2_ref[...], wout_ref[half:, :])
    x1 = x_ref[...] + g1_ref[...] * mix
    h = _rms_mod(x1, ng_ref[...], sc_ref[...], sh_ref[...]).astype(BF16)
    acc = None
    for c in range(D_FF // FF_CHUNK):
        a = _dot(h, w1_ref[:, c * FF_CHUNK:(c + 1) * FF_CHUNK])
        a = jnp.square(jnp.maximum(a, 0.0)).astype(BF16)
        part = _dot(a, w2_ref[c * FF_CHUNK:(c + 1) * FF_CHUNK, :])
        acc = part if acc is None else acc + part
    o_ref[...] = x1 + g2_ref[...] * acc


def _out_mlp_call(x, p1, p2, mod, layer, row_of_batch, ng, w_out, w1, w2, tm):
    bsz, n, _ = x.shape
    half = p1.shape[-1]
    tok = lambda w: pl.BlockSpec((None, tm, w), lambda b, i: (b, i, 0))
    return pl.pallas_call(
        _out_mlp_kernel,
        grid=(bsz, n // tm),
        in_specs=[
            tok(D_MODEL), tok(half), tok(half),
            _mod_spec(layer, 2, row_of_batch),
            _mod_spec(layer, 3, row_of_batch),
            _mod_spec(layer, 4, row_of_batch),
            _mod_spec(layer, 5, row_of_batch),
            _const_spec((1, D_MODEL)),
            _const_spec((2 * half, D_MODEL)),
            _const_spec((D_MODEL, D_FF)),
            _const_spec((D_FF, D_MODEL)),
        ],
        out_specs=tok(D_MODEL),
        out_shape=jax.ShapeDtypeStruct((bsz, n, D_MODEL), F32),
        compiler_params=_params(2),
        name="out_mlp",
    )(x, p1, p2, mod, mod, mod, mod, ng, w_out, w1, w2)


def _odd_in_kernel(*refs, rope, emit_cache):
    x_ref, sh_ref, sc_ref, g_ref, win_ref, gain_ref, hm_ref = refs[:7]
    k = 7
    if rope:
        cos_ref, sin_ref = refs[k:k + 2]
        k += 2
    cq_ref, ck_ref, cvo_ref, dq_ref, dk_ref, dv_ref = refs[k:k + 6]
    k += 6
    if emit_cache:
        nck_ref, ncv_ref, ndk_ref, ndv_ref = refs[k:k + 4]

    h = _rms_mod(x_ref[...], g_ref[...], sc_ref[...], sh_ref[...]).astype(BF16)
    qkv = _dot(h, win_ref[...])
    tm = qkv.shape[0]
    lane = lax.broadcasted_iota(jnp.int32, (tm, LANES), 1)
    first_half = (lane % (2 * ROPE_FREQS)) < ROPE_FREQS

    normed = []
    for c in range(ODD_NORMED_W // LANES):
        v = qkv[:, c * LANES:(c + 1) * LANES]
        ms = _dot((v * v).astype(BF16), hm_ref[...])
        y = v * lax.rsqrt(ms + EPS) * gain_ref[:, c * LANES:(c + 1) * LANES]
        normed.append(y)

    def roped(y):
        if not rope:
            return y
        rot = jnp.where(first_half, -pltpu.roll(y, LANES - ROPE_FREQS, 1),
                        pltpu.roll(y, ROPE_FREQS, 1))
        return y * cos_ref[...] + rot * sin_ref[...]

    nq = C_W // LANES
    for c in range(nq):
        cq_ref[:, c * LANES:(c + 1) * LANES] = roped(normed[c]).astype(BF16)
    for c in range(nq):
        y = normed[nq + c]
        if emit_cache:
            nck_ref[:, c * LANES:(c + 1) * LANES] = y
        ck_ref[:, c * LANES:(c + 1) * LANES] = roped(y).astype(BF16)
    for c in range(DQ_W // LANES):
        dq_ref[:, c * LANES:(c + 1) * LANES] = roped(normed[2 * nq + c]).astype(BF16)
    dk_chunks = []
    for c in range(DKV_DUP_W // LANES):
        y = normed[2 * nq + DQ_W // LANES + c]
        dk_chunks.append(y)
        dk_ref[:, c * LANES:(c + 1) * LANES] = roped(y).astype(BF16)
    cv = qkv[:, ODD_NORMED_W:ODD_NORMED_W + C_W]
    cvo_ref[...] = cv.astype(BF16)
    dv = qkv[:, ODD_NORMED_W + C_W:]
    dv_ref[...] = dv.astype(BF16)
    if emit_cache:
        ncv_ref[...] = cv
        lo = lane < HEAD_DIM
        ndk_ref[...] = jnp.where(lo, dk_chunks[0], dk_chunks[1])
        ndv_ref[...] = jnp.where(lo, dv[:, :LANES], dv[:, LANES:])


def _odd_in_call(x, mod, layer, row_of_batch, g, w_in, gains, head_mean, rope_tabs, emit_cache, tm):
    bsz, n, _ = x.shape
    rope = rope_tabs is not None
    tok = lambda w: pl.BlockSpec((None, tm, w), lambda b, i: (b, i, 0))
    in_specs = [
        tok(D_MODEL),
        _mod_spec(layer, 0, row_of_batch),
        _mod_spec(layer, 1, row_of_batch),
        _const_spec((1, D_MODEL)),
        _const_spec((D_MODEL, ODD_W)),
        _const_spec((1, ODD_NORMED_W)),
        _const_spec((LANES, LANES)),
    ]
    args = [x, mod, mod, g, w_in, gains, head_mean]
    if rope:
        in_specs += [pl.BlockSpec((tm, LANES), lambda b, i: (i, 0))] * 2
        args += list(rope_tabs)
    widths = [C_W, C_W, C_W, DQ_W, DKV_DUP_W, DKV_DUP_W]
    out_specs = [tok(w) for w in widths]
    out_shape = [jax.ShapeDtypeStruct((bsz, n, w), BF16) for w in widths]
    if emit_cache:
        cw = [C_W, C_W, DKV_W, DKV_W]
        out_specs += [tok(w) for w in cw]
        out_shape += [jax.ShapeDtypeStruct((bsz, n, w), F32) for w in cw]
    return pl.pallas_call(
        functools.partial(_odd_in_kernel, rope=rope, emit_cache=emit_cache),
        grid=(bsz, n // tm),
        in_specs=in_specs,
        out_specs=out_specs,
        out_shape=out_shape,
        compiler_params=_params(2),
        name="odd_in",
    )(*args)


def _diff_attn_kernel(q_ref, k_ref, vt_ref, lamv_ref, g_ref, o_ref, *, lam_init):
    q = q_ref[...]
    tq = q.shape[0]
    lane = lax.broadcasted_iota(jnp.int32, q.shape, 1)
    zero = jnp.zeros_like(q)
    qq = jnp.concatenate([jnp.where(lane < HEAD_DIM, q, zero),
                          jnp.where(lane >= HEAD_DIM, q, zero)], axis=0)
    st = _dot_nt(k_ref[...], qq)
    m = jnp.max(st, axis=0, keepdims=True)
    e = jnp.exp(st - m)
    inv = 1.0 / jnp.sum(e, axis=0, keepdims=True)
    lv = lamv_ref[...]
    lam = (jnp.exp(jnp.sum(lv[0:1] * lv[1:2], keepdims=True))
           - jnp.exp(jnp.sum(lv[2:3] * lv[3:4], keepdims=True)) + lam_init)
    pt = (e[:, :tq] * inv[:, :tq] - e[:, tq:] * (inv[:, tq:] * lam)).astype(BF16)
    ot = _dot(vt_ref[...], pt)
    o = ot.T
    ms = jnp.mean(o * o, axis=-1, keepdims=True)
    y = o * lax.rsqrt(ms + EPS) * g_ref[...] * (1.0 - lam_init)
    o_ref[...] = y.astype(BF16)


def _diff_attn_call(q, k, vt, lamv, subln_g, lam_init, tq):
    bsz, n, _ = q.shape
    nk = k.shape[1]
    return pl.pallas_call(
        functools.partial(_diff_attn_kernel, lam_init=lam_init),
        grid=(bsz, C_HEADS, n // tq),
        in_specs=[
            pl.BlockSpec((None, tq, LANES), lambda b, h, i: (b, i, h)),
            pl.BlockSpec((None, nk, LANES), lambda b, h, i: (b, 0, h)),
            pl.BlockSpec((None, LANES, nk), lambda b, h, i: (b, h, 0)),
            _const_spec((4, HEAD_DIM)),
            _const_spec((1, LANES)),
        ],
        out_specs=pl.BlockSpec((None, tq, LANES), lambda b, h, i: (b, i, h)),
        out_shape=jax.ShapeDtypeStruct((bsz, n, C_W), BF16),
        compiler_params=_params(3),
        name="diff_attn",
    )(q, k, vt, lamv, subln_g)


def _sink_attn_kernel(*refs, windowed, n, n_ctx):
    if windowed:
        q_ref, ck_ref, k0_ref, k1_ref, k2_ref, cvt_ref, v0_ref, v1_ref, v2_ref, sink_ref, o_ref = refs
        kcat = jnp.concatenate([ck_ref[...], k0_ref[...], k1_ref[...], k2_ref[...]], axis=0)
        vtcat = jnp.concatenate([cvt_ref[...], v0_ref[...], v1_ref[...], v2_ref[...]], axis=1)
    else:
        q_ref, k_ref, vt_ref, sink_ref, o_ref = refs
        kcat = k_ref[...]
        vtcat = vt_ref[...]
    qd = q_ref[...]
    tq = qd.shape[0]
    lane = lax.broadcasted_iota(jnp.int32, (tq, LANES), 1)
    lo = lane < HEAD_DIM
    parts = []
    for c in range(2):
        ch = qd[:, c * LANES:(c + 1) * LANES]
        zero = jnp.zeros_like(ch)
        parts += [jnp.where(lo, ch, zero), jnp.where(lo, zero, ch)]
    qq = jnp.concatenate(parts, axis=0)
    st = _dot_nt(kcat, qq)
    if windowed:
        start = pl.program_id(2) * tq
        r = lax.broadcasted_iota(jnp.int32, st.shape, 0)
        qi = lax.broadcasted_iota(jnp.int32, st.shape, 1) % tq
        kk = r - n_ctx
        jpos = start - WINDOW + kk
        valid = (r < n_ctx) | ((jnp.abs(qi + WINDOW - kk) <= WINDOW) & (jpos >= 0) & (jpos < n))
        st = jnp.where(valid, st, NEG_INF)
    sink = sink_ref[...]
    m = jnp.maximum(jnp.max(st, axis=0, keepdims=True), sink)
    e = jnp.exp(st - m)
    inv = 1.0 / (jnp.sum(e, axis=0, keepdims=True) + jnp.exp(sink - m))
    pt = (e * inv).astype(BF16)
    ot = _dot(vtcat, pt)
    for c in range(2):
        o0 = ot[:, (2 * c) * tq:(2 * c + 1) * tq].T
        o1 = ot[:, (2 * c + 1) * tq:(2 * c + 2) * tq].T
        o_ref[:, c * LANES:(c + 1) * LANES] = jnp.where(lo, o0, o1).astype(BF16)


def _sink_attn_call(q, k, vt, sink_rows, ctx_k=None, ctx_vt=None):
    bsz, n, _ = q.shape
    windowed = ctx_k is not None
    tq = WINDOW if windowed else n
    nblk = n // tq
    q_spec = pl.BlockSpec((None, tq, 2 * LANES), lambda b, j, i: (b, i, j))
    sink_spec = pl.BlockSpec((None, 1, D_GROUP * tq), lambda b, j, i: (j, 0, 0))
    if windowed:
        n_ctx = ctx_k.shape[1]
        prev = lambda i: jnp.maximum(i - 1, 0)
        nxt = lambda i: jnp.minimum(i + 1, nblk - 1)
        in_specs = [
            q_spec,
            pl.BlockSpec((None, n_ctx, LANES), lambda b, j, i: (b, 0, j)),
            pl.BlockSpec((None, tq, LANES), lambda b, j, i: (b, prev(i), j)),
            pl.BlockSpec((None, tq, LANES), lambda b, j, i: (b, i, j)),
            pl.BlockSpec((None, tq, LANES), lambda b, j, i: (b, nxt(i), j)),
            pl.BlockSpec((None, LANES, n_ctx), lambda b, j, i: (b, j, 0)),
            pl.BlockSpec((None, LANES, tq), lambda b, j, i: (b, j, prev(i))),
            pl.BlockSpec((None, LANES, tq), lambda b, j, i: (b, j, i)),
            pl.BlockSpec((None, LANES, tq), lambda b, j, i: (b, j, nxt(i))),
            sink_spec,
        ]
        args = (q, ctx_k, k, k, k, ctx_vt, vt, vt, vt, sink_rows)
    else:
        n_ctx = 0
        in_specs = [
            q_spec,
            pl.BlockSpec((None, n, LANES), lambda b, j, i: (b, 0, j)),
            pl.BlockSpec((None, LANES, n), lambda b, j, i: (b, j, 0)),
            sink_spec,
        ]
        args = (q, k, vt, sink_rows)
    return pl.pallas_call(
        functools.partial(_sink_attn_kernel, windowed=windowed, n=n, n_ctx=n_ctx),
        grid=(bsz, D_KV_HEADS, nblk),
        in_specs=in_specs,
        out_specs=pl.BlockSpec((None, tq, 2 * LANES), lambda b, j, i: (b, i, j)),
        out_shape=jax.ShapeDtypeStruct((bsz, n, DQ_W), BF16),
        compiler_params=_params(3),
        name="sink_attn",
    )(*args)


def _dup_heads(a):
    lead = a.shape[:-1]
    a = a.reshape(lead + (D_KV_HEADS, 1, HEAD_DIM))
    return jnp.broadcast_to(a, lead + (D_KV_HEADS, 2, HEAD_DIM)).reshape(lead + (DKV_DUP_W,))


def _odd_weights(w_in_odd, c_qn_g, c_kn_g, d_qn_g, d_kn_g):
    cq, ck, cv, dq, dk, dv = jnp.split(
        w_in_odd, (C_W, 2 * C_W, 3 * C_W, 3 * C_W + DQ_W, 3 * C_W + DQ_W + DKV_W), axis=-1)
    w = jnp.concatenate([cq, ck, dq, _dup_heads(dk), cv, _dup_heads(dv)], axis=-1).astype(BF16)
    gains = jnp.concatenate([
        jnp.tile(c_qn_g * ATTN_SCALE, C_W // HEAD_DIM),
        jnp.tile(c_kn_g, C_W // HEAD_DIM),
        jnp.tile(d_qn_g * ATTN_SCALE, DQ_W // HEAD_DIM),
        jnp.tile(d_kn_g, DKV_DUP_W // HEAD_DIM),
    ])[None, :]
    return w, gains


def _rope_tables(n):
    rows = n // GRID_W
    row = jnp.repeat(jnp.arange(rows), GRID_W).astype(F32)
    col = jnp.tile(jnp.arange(GRID_W), rows).astype(F32)
    freqs = ROPE_BASE ** (-jnp.arange(ROPE_FREQS, dtype=F32) / ROPE_FREQS)
    ang = jnp.stack([row[:, None] * freqs, col[:, None] * freqs], axis=1)
    ang = jnp.concatenate([ang, ang], axis=-1).reshape(n, HEAD_DIM)
    ang = jnp.concatenate([ang, ang], axis=-1)
    return jnp.cos(ang), jnp.sin(ang)


def _token_tile(n):
    return min(n, 512)


def _run_group(x, mod, row_of_batch, wts, caches):
    bsz, n, _ = x.shape
    decode = caches is not None
    tm = _token_tile(n)
    depth = wts["norm1_g"].shape[0]
    new_cache = []
    for l in range(depth):
        g1n = wts["norm1_g"][l][None, :]
        g2n = wts["norm2_g"][l][None, :]
        if l % 2 == 0:
            e = l // 2
            a, z = _even_in_call(x, mod, l, row_of_batch, g1n, wts["w_in_even"][e], wts["w_pool"][e],
                                 wts["pool_scale"][e][None, :], wts["dftc"], wts["w_fft"][e], tm)
            y = _dft_two_stage_call(z) if n > DFT_DIRECT_MAX else _dft_direct_call(z)
            p1, p2, w_out = a, y, wts["w_out_even"][e]
        else:
            o = l // 2
            lam_init = _lam_init(l)
            rope_tabs = _rope_tables(n) if decode else None
            outs = _odd_in_call(x, mod, l, row_of_batch, g1n, wts["w_in_odd"][o], wts["odd_gains"][o],
                                wts["head_mean"], rope_tabs, not decode, tm)
            cq, ck, cv, dq, dkk, dvv = outs[:6]
            if decode:
                c_k, c_v, d_k, d_v = caches
                lc = c_k.shape[2]
                k_all = jnp.concatenate([c_k[:, o].reshape(bsz, lc, C_W).astype(BF16), ck], axis=1)
                v_all = jnp.concatenate([c_v[:, o].reshape(bsz, lc, C_W).astype(BF16), cv], axis=1)
                ctx_k = _dup_heads(d_k[:, o].reshape(bsz, lc, DKV_W)).astype(BF16)
                ctx_vt = _dup_heads(d_v[:, o].reshape(bsz, lc, DKV_W)).astype(BF16).transpose(0, 2, 1)
                tq = min(n, 256)
            else:
                new_cache.append(outs[6:])
                k_all, v_all = ck, cv
                ctx_k = ctx_vt = None
                tq = n
            c_out = _diff_attn_call(cq, k_all, v_all.transpose(0, 2, 1), wts["lamv"][o],
                                    wts["c_subln_g"][o][None, :], lam_init, tq)
            sink_tq = WINDOW if decode else n
            sink_rows = jnp.repeat(wts["d_sink"][o].reshape(D_KV_HEADS, D_GROUP), sink_tq,
                                   axis=-1)[:, None, :]
            d_out = _sink_attn_call(dq, dkk, dvv.transpose(0, 2, 1), sink_rows, ctx_k, ctx_vt)
            p1, p2, w_out = c_out, d_out, wts["w_out_odd"][o]
        x = _out_mlp_call(x, p1, p2, mod, l, row_of_batch, g2n, w_out,
                          wts["w_mlp1"][l], wts["w_mlp2"][l], tm)
    return x, new_cache


def kernel(x_prompt, x_sample, c, cache_c_k, cache_c_v, cache_d_k, cache_d_v, c_ctx,
           norm1_g, norm2_g, w_ada, b_ada, w_in_even, w_pool, pool_scale, w_fft, w_out_even,
           w_in_odd, c_qn_g, c_kn_g, lam_q1, lam_k1, lam_q2, lam_k2, c_subln_g,
           d_qn_g, d_kn_g, d_sink, w_out_odd, w_mlp1, w_mlp2):
    depth = norm1_g.shape[0]
    n_odd = w_in_odd.shape[0]
    dec_b = c.shape[0]

    rows = -(-(1 + dec_b) // SUBLANES) * SUBLANES
    cond = jnp.zeros((rows, D_MODEL), F32).at[0].set(c_ctx).at[1:1 + dec_b].set(c)
    mod = _adaln_call(cond, w_ada, b_ada).reshape(depth, rows, 6, 1, D_MODEL)

    ic = np.arange(GROUP_W)
    ang = 2.0 * np.pi * np.outer(ic, ic) / GROUP_W
    dftc = _bf16_const(np.concatenate([np.cos(ang), -np.sin(ang)], axis=1))
    head_mean = _bf16_const(np.kron(np.eye(LANES // HEAD_DIM), np.ones((HEAD_DIM, HEAD_DIM))) / HEAD_DIM)

    odd = [_odd_weights(w_in_odd[o], c_qn_g[o], c_kn_g[o], d_qn_g[o], d_kn_g[o]) for o in range(n_odd)]
    wts = {
        "norm1_g": norm1_g, "norm2_g": norm2_g,
        "w_in_even": w_in_even.astype(BF16), "w_pool": w_pool.astype(BF16), "pool_scale": pool_scale,
        "w_fft": w_fft.astype(BF16), "w_out_even": w_out_even.astype(BF16), "dftc": dftc,
        "w_in_odd": [w for w, _ in odd], "odd_gains": [g for _, g in odd], "head_mean": head_mean,
        "lamv": jnp.stack([lam_q1, lam_k1, lam_q2, lam_k2], axis=1),
        "c_subln_g": c_subln_g, "d_sink": d_sink, "w_out_odd": w_out_odd.astype(BF16),
        "w_mlp1": w_mlp1.astype(BF16), "w_mlp2": w_mlp2.astype(BF16),
    }

    y_prompt, new_cache = _run_group(x_prompt, mod, lambda b: 0, wts, None)
    y_sample, _ = _run_group(x_sample, mod, lambda b: 1 + b, wts,
                             (cache_c_k, cache_c_v, cache_d_k, cache_d_v))

    bsz, n, _ = x_prompt.shape
    new_c_k = jnp.stack([nc[0].reshape(bsz, n, C_HEADS, 2, HEAD_DIM) for nc in new_cache], axis=1)
    new_c_v = jnp.stack([nc[1].reshape(bsz, n, C_HEADS, 2 * HEAD_DIM) for nc in new_cache], axis=1)
    new_d_k = jnp.stack([nc[2].reshape(bsz, n, D_KV_HEADS, HEAD_DIM) for nc in new_cache], axis=1)
    new_d_v = jnp.stack([nc[3].reshape(bsz, n, D_KV_HEADS, HEAD_DIM) for nc in new_cache], axis=1)
    return (y_prompt, y_sample, new_c_k, new_c_v, new_d_k, new_d_v)
```

```python
import functools
import math

import numpy as np
import jax
import jax.numpy as jnp
from jax import lax
from jax.experimental import pallas as pl
from jax.experimental.pallas import tpu as pltpu

F32 = jnp.float32
BF16 = jnp.bfloat16

D_MODEL = 1024
HEAD_DIM = 64
LANES = 128
SUBLANES = 8
GRID_W = 64
ROPE_FREQS = HEAD_DIM // 4
ROPE_BASE = 10000.0
EPS = 1e-6
NEG_INF = -1e30
WINDOW = 128
A_WIDTH = D_MODEL // 2
B_WIDTH = D_MODEL // 2
POOL_WINDOWS = (2, 4, 8, 16)
POOL_HALO = 8
N_GROUPS = 4
GROUP_W = A_WIDTH // N_GROUPS
C_HEADS = 4
D_HEADS = 8
D_KV_HEADS = 2
D_GROUP = D_HEADS // D_KV_HEADS
C_W = C_HEADS * 2 * HEAD_DIM
DQ_W = D_HEADS * HEAD_DIM
DKV_W = D_KV_HEADS * HEAD_DIM
DKV_DUP_W = 2 * DKV_W
D_FF = 4 * D_MODEL
ATTN_SCALE = HEAD_DIM ** -0.5
LOG2E = math.log2(math.e)
V_ROWS = LANES + 16
ODD_NORMED_W = C_W + C_W + DQ_W + DKV_DUP_W
ODD_W = ODD_NORMED_W + C_W + DKV_DUP_W
VMEM_LIMIT = 56 * 1024 * 1024


def _lam_init(layer):
    return 0.8 - 0.6 * math.exp(-0.3 * layer)


def _params(n_axes):
    return pltpu.CompilerParams(dimension_semantics=("arbitrary",) * n_axes,
                                vmem_limit_bytes=VMEM_LIMIT)


def _const_spec(shape):
    nd = len(shape)
    return pl.BlockSpec(shape, lambda *_: (0,) * nd, pipeline_mode=pl.Buffered(1))


def _mod_spec(layer, which, row_of_batch):
    return pl.BlockSpec((None, None, None, 1, D_MODEL),
                        lambda b, i: (layer, row_of_batch(b), which, 0, 0))


def _rms_mod(x, g, sc, sh):
    ms = jnp.mean(x * x, axis=-1, keepdims=True)
    return (x * lax.rsqrt(ms + EPS) * g) * (1.0 + sc) + sh


def _bf16_const(a):
    return jnp.asarray(a, F32).astype(BF16)


def _dot_nt(a, b):
    return lax.dot_general(a, b, (((1,), (1,)), ((), ())), preferred_element_type=F32)


def _dot(a, b):
    return jnp.dot(a, b, preferred_element_type=F32)


def _adaln_kernel(cond_ref, w_ref, b_ref, o_ref):
    cnd = cond_ref[...]
    s = (cnd * jax.nn.sigmoid(cnd)).astype(BF16)
    o_ref[...] = _dot(s, w_ref[...].astype(BF16)) + b_ref[...]


def _adaln_call(cond, w_ada, b_ada):
    depth = w_ada.shape[0]
    rows = cond.shape[0]
    tn = 1536
    return pl.pallas_call(
        _adaln_kernel,
        grid=(depth, 6 * D_MODEL // tn),
        in_specs=[
            pl.BlockSpec((rows, D_MODEL), lambda l, j: (0, 0)),
            pl.BlockSpec((None, D_MODEL, tn), lambda l, j: (l, 0, j)),
            pl.BlockSpec((None, 1, tn), lambda l, j: (l, 0, j)),
        ],
        out_specs=pl.BlockSpec((None, rows, tn), lambda l, j: (l, 0, j)),
        out_shape=jax.ShapeDtypeStruct((depth, rows, 6 * D_MODEL), F32),
        compiler_params=_params(2),
        name="adaln",
    )(cond, w_ada, b_ada.reshape(depth, 1, 6 * D_MODEL))


def _even_in_kernel(x_ref, xp_ref, xn_ref, sh_ref, sc_ref, g_ref, win_ref, wpool_ref,
                    pscale_ref, dftc_ref, wfft_ref, a_ref, z_ref, ext_ref, *, tm, n):
    i = pl.program_id(1)
    n_tiles = n // tm
    g, sc, sh = g_ref[...], sc_ref[...], sh_ref[...]
    h = _rms_mod(x_ref[...], g, sc, sh).astype(BF16)
    u = _dot(h, win_ref[...])
    xh = jnp.concatenate([xp_ref[...], xn_ref[...]], axis=0)
    hh = _rms_mod(xh, g, sc, sh).astype(BF16)
    uh = _dot(hh, win_ref[:, :A_WIDTH])
    has_prev = (i > 0).astype(F32)
    has_next = (i < n_tiles - 1).astype(F32)
    ext_ref[0:POOL_HALO, :] = uh[0:POOL_HALO] * has_prev
    ext_ref[POOL_HALO:POOL_HALO + tm, :] = u[:, :A_WIDTH]
    ext_ref[POOL_HALO + tm:, :] = uh[POOL_HALO:] * has_next

    t = i * tm + lax.broadcasted_iota(jnp.int32, (tm, 1), 0)
    for gi, w in enumerate(POOL_WINDOWS):
        cols = slice(gi * GROUP_W, (gi + 1) * GROUP_W)
        acc = None
        for j in range(-(w // 2), w // 2):
            v = ext_ref[pl.ds(POOL_HALO + j, tm), cols]
            acc = v if acc is None else acc + v
        lo = jnp.maximum(t - w // 2, 0)
        hi = jnp.minimum(t + w // 2, n)
        cnt = (hi - lo).astype(F32)
        pooled = acc / cnt - u[:, cols]
        y = _dot(pooled.astype(BF16), wpool_ref[gi]) * pscale_ref[:, cols]
        a_ref[:, cols] = y.astype(BF16)

    for gi in range(N_GROUPS):
        ub = u[:, A_WIDTH + gi * GROUP_W:A_WIDTH + (gi + 1) * GROUP_W].astype(BF16)
        cs = _dot(ub, dftc_ref[...])
        zr = _dot(cs[:, :GROUP_W].astype(BF16), wfft_ref[gi])
        zi = _dot(cs[:, GROUP_W:].astype(BF16), wfft_ref[gi])
        z_ref[:, gi * GROUP_W:(gi + 1) * GROUP_W] = zr.astype(BF16)
        z_ref[:, B_WIDTH + gi * GROUP_W:B_WIDTH + (gi + 1) * GROUP_W] = zi.astype(BF16)


def _even_in_call(x, mod, layer, row_of_batch, g, w_in, w_pool, pool_scale, dftc, w_fft, tm):
    bsz, n, _ = x.shape
    hb = tm // POOL_HALO
    kern = functools.partial(_even_in_kernel, tm=tm, n=n)
    return pl.pallas_call(
        kern,
        grid=(bsz, n // tm),
        in_specs=[
            pl.BlockSpec((None, tm, D_MODEL), lambda b, i: (b, i, 0)),
            pl.BlockSpec((None, POOL_HALO, D_MODEL),
                         lambda b, i: (b, jnp.maximum(i * hb - 1, 0), 0)),
            pl.BlockSpec((None, POOL_HALO, D_MODEL),
                         lambda b, i: (b, jnp.minimum((i + 1) * hb, n // POOL_HALO - 1), 0)),
            _mod_spec(layer, 0, row_of_batch),
            _mod_spec(layer, 1, row_of_batch),
            _const_spec((1, D_MODEL)),
            _const_spec((D_MODEL, D_MODEL)),
            _const_spec((N_GROUPS, GROUP_W, GROUP_W)),
            _const_spec((1, A_WIDTH)),
            _const_spec((GROUP_W, 2 * GROUP_W)),
            _const_spec((N_GROUPS, GROUP_W, GROUP_W)),
        ],
        out_specs=[
            pl.BlockSpec((None, tm, A_WIDTH), lambda b, i: (b, i, 0)),
            pl.BlockSpec((None, tm, 2 * B_WIDTH), lambda b, i: (b, i, 0)),
        ],
        out_shape=[
            jax.ShapeDtypeStruct((bsz, n, A_WIDTH), BF16),
            jax.ShapeDtypeStruct((bsz, n, 2 * B_WIDTH), BF16),
        ],
        scratch_shapes=[pltpu.VMEM((tm + 2 * POOL_HALO, A_WIDTH), F32)],
        compiler_params=_params(2),
        name="even_in",
    )(x, x, x, mod, mod, g, w_in, w_pool, pool_scale, dftc, w_fft)


def _dft_direct_kernel(z_ref, m_ref, y_ref, *, scale):
    z = z_ref[...]
    rhs = jnp.concatenate([z[:, :B_WIDTH], z[:, B_WIDTH:]], axis=0)
    y_ref[...] = (_dot(m_ref[...], rhs) * scale).astype(BF16)


def _dft_direct_call(z):
    bsz, n, _ = z.shape
    k = np.arange(n)
    ang = 2.0 * np.pi * np.outer(k, k) / n
    m = _bf16_const(np.concatenate([np.cos(ang), np.sin(ang)], axis=1))
    kern = functools.partial(_dft_direct_kernel, scale=float((n * GROUP_W) ** -0.5))
    return pl.pallas_call(
        kern,
        grid=(bsz,),
        in_specs=[pl.BlockSpec((None, n, 2 * B_WIDTH), lambda b: (b, 0, 0)),
                  _const_spec((n, 2 * n))],
        out_specs=pl.BlockSpec((None, n, B_WIDTH), lambda b: (b, 0, 0)),
        out_shape=jax.ShapeDtypeStruct((bsz, n, B_WIDTH), BF16),
        compiler_params=_params(1),
        name="dft_direct",
    )(z, m)


DFT_ROWS = 8
DFT_DIRECT_MAX = 256


def _dft_stage1_kernel(z_ref, m_ref, twr_ref, twi_ref, o_ref, *, na):
    for bi in range(DFT_ROWS):
        zb = z_ref[bi]
        rhs = jnp.concatenate([zb[:, :B_WIDTH], zb[:, B_WIDTH:]], axis=0)
        r = _dot(m_ref[...], rhs)
        br, bim = r[:na], r[na:]
        tr, ti = twr_ref[bi], twi_ref[bi]
        o_ref[bi, :, :B_WIDTH] = (br * tr - bim * ti).astype(BF16)
        o_ref[bi, :, B_WIDTH:] = (br * ti + bim * tr).astype(BF16)


def _dft_stage2_kernel(b_ref, m_ref, y_ref, *, scale):
    for di in range(DFT_ROWS):
        bb = b_ref[di]
        rhs = jnp.concatenate([bb[:, :B_WIDTH], bb[:, B_WIDTH:]], axis=0)
        y_ref[di] = (_dot(m_ref[...], rhs) * scale).astype(BF16)


def _dft_two_stage_call(z):
    bsz, n, _ = z.shape
    na = 1 << (int(math.log2(n)) // 2)
    nb = n // na
    ia = np.arange(na)
    ib = np.arange(nb)
    ang_a = 2.0 * np.pi * np.outer(ia, ia) / na
    fr, fi = np.cos(ang_a), -np.sin(ang_a)
    m1 = _bf16_const(np.block([[fr, -fi], [fi, fr]]))
    ang_t = 2.0 * np.pi * np.outer(ib, ia) / n
    twr = jnp.asarray(np.cos(ang_t)[:, :, None], F32)
    twi = jnp.asarray(-np.sin(ang_t)[:, :, None], F32)
    ang_b = 2.0 * np.pi * np.outer(ib, ib) / nb
    m2 = _bf16_const(np.concatenate([np.cos(ang_b), np.sin(ang_b)], axis=1))

    zt = z.reshape(bsz, na, nb, 2 * B_WIDTH).transpose(0, 2, 1, 3)
    s1 = pl.pallas_call(
        functools.partial(_dft_stage1_kernel, na=na),
        grid=(bsz, nb // DFT_ROWS),
        in_specs=[
            pl.BlockSpec((None, DFT_ROWS, na, 2 * B_WIDTH), lambda b, j: (b, j, 0, 0)),
            _const_spec((2 * na, 2 * na)),
            pl.BlockSpec((DFT_ROWS, na, 1), lambda b, j: (j, 0, 0)),
            pl.BlockSpec((DFT_ROWS, na, 1), lambda b, j: (j, 0, 0)),
        ],
        out_specs=pl.BlockSpec((None, DFT_ROWS, na, 2 * B_WIDTH), lambda b, j: (b, j, 0, 0)),
        out_shape=jax.ShapeDtypeStruct((bsz, nb, na, 2 * B_WIDTH), BF16),
        compiler_params=_params(2),
        name="dft_stage1",
    )(zt, m1, twr, twi)
    bt = s1.transpose(0, 2, 1, 3)
    yt = pl.pallas_call(
        functools.partial(_dft_stage2_kernel, scale=float((n * GROUP_W) ** -0.5)),
        grid=(bsz, na // DFT_ROWS),
        in_specs=[
            pl.BlockSpec((None, DFT_ROWS, nb, 2 * B_WIDTH), lambda b, j: (b, j, 0, 0)),
            _const_spec((nb, 2 * nb)),
        ],
        out_specs=pl.BlockSpec((None, DFT_ROWS, nb, B_WIDTH), lambda b, j: (b, j, 0, 0)),
        out_shape=jax.ShapeDtypeStruct((bsz, na, nb, B_WIDTH), BF16),
        compiler_params=_params(2),
        name="dft_stage2",
    )(bt, m2)
    return yt.transpose(0, 2, 1, 3).reshape(bsz, n, B_WIDTH)


FF_CHUNK = 1024


def _out_mlp_kernel(x_ref, p1_ref, p2_ref, g1_ref, sh_ref, sc_ref, g2_ref, ng_ref,
                    wout_ref, w1_ref, w2_ref, o_ref):
    half = p1_ref.shape[-1]
    mix = _dot(p1_ref[...], wout_ref[:half, :]) + _dot(p2_ref[...], wout_ref[half:, :])
    x1 = x_ref[...] + g1_ref[...] * mix
    h = _rms_mod(x1, ng_ref[...], sc_ref[...], sh_ref[...]).astype(BF16)
    acc = None
    for c in range(D_FF // FF_CHUNK):
        a = _dot(h, w1_ref[:, c * FF_CHUNK:(c + 1) * FF_CHUNK])
        a = jnp.square(jnp.maximum(a, 0.0)).astype(BF16)
        part = _dot(a, w2_ref[c * FF_CHUNK:(c + 1) * FF_CHUNK, :])
        acc = part if acc is None else acc + part
    o_ref[...] = x1 + g2_ref[...] * acc


def _out_mlp_call(x, p1, p2, mod, layer, row_of_batch, ng, w_out, w1, w2, tm):
    bsz, n, _ = x.shape
    half = p1.shape[-1]
    tok = lambda w: pl.BlockSpec((None, tm, w), lambda b, i: (b, i, 0))
    return pl.pallas_call(
        _out_mlp_kernel,
        grid=(bsz, n // tm),
        in_specs=[
            tok(D_MODEL), tok(half), tok(half),
            _mod_spec(layer, 2, row_of_batch),
            _mod_spec(layer, 3, row_of_batch),
            _mod_spec(layer, 4, row_of_batch),
            _mod_spec(layer, 5, row_of_batch),
            _const_spec((1, D_MODEL)),
            _const_spec((2 * half, D_MODEL)),
            _const_spec((D_MODEL, D_FF)),
            _const_spec((D_FF, D_MODEL)),
        ],
        out_specs=tok(D_MODEL),
        out_shape=jax.ShapeDtypeStruct((bsz, n, D_MODEL), F32),
        compiler_params=_params(2),
        name="out_mlp",
    )(x, p1, p2, mod, mod, mod, mod, ng, w_out, w1, w2)


def _odd_in_kernel(*refs, rope, emit_cache):
    x_ref, sh_ref, sc_ref, g_ref, win_ref, gain_ref, hm_ref = refs[:7]
    k = 7
    if rope:
        cos_ref, sin_ref = refs[k:k + 2]
        k += 2
    cq_ref, ck_ref, cvo_ref, dq_ref, dk_ref, dv_ref = refs[k:k + 6]
    k += 6
    if emit_cache:
        nck_ref, ncv_ref, ndk_ref, ndv_ref = refs[k:k + 4]

    h = _rms_mod(x_ref[...], g_ref[...], sc_ref[...], sh_ref[...]).astype(BF16)
    qkv = _dot(h, win_ref[...])
    tm = qkv.shape[0]
    lane = lax.broadcasted_iota(jnp.int32, (tm, LANES), 1)
    first_half = (lane % (2 * ROPE_FREQS)) < ROPE_FREQS

    normed = []
    for c in range(ODD_NORMED_W // LANES):
        v = qkv[:, c * LANES:(c + 1) * LANES]
        ms = _dot((v * v).astype(BF16), hm_ref[...])
        y = v * lax.rsqrt(ms + EPS) * gain_ref[:, c * LANES:(c + 1) * LANES]
        normed.append(y)

    def roped(y):
        if not rope:
            return y
        rot = jnp.where(first_half, -pltpu.roll(y, LANES - ROPE_FREQS, 1),
                        pltpu.roll(y, ROPE_FREQS, 1))
        return y * cos_ref[...] + rot * sin_ref[...]

    nq = C_W // LANES
    for c in range(nq):
        cq_ref[:, c * LANES:(c + 1) * LANES] = roped(normed[c]).astype(BF16)
    for c in range(nq):
        y = normed[nq + c]
        if emit_cache:
            nck_ref[:, c * LANES:(c + 1) * LANES] = y
        ck_ref[:, c * LANES:(c + 1) * LANES] = roped(y).astype(BF16)
    for c in range(DQ_W // LANES):
        dq_ref[:, c * LANES:(c + 1) * LANES] = roped(normed[2 * nq + c]).astype(BF16)
    dk_chunks = []
    for c in range(DKV_DUP_W // LANES):
        y = normed[2 * nq + DQ_W // LANES + c]
        dk_chunks.append(y)
        dk_ref[:, c * LANES:(c + 1) * LANES] = roped(y).astype(BF16)
    cv = qkv[:, ODD_NORMED_W:ODD_NORMED_W + C_W]
    cvo_ref[...] = cv.astype(BF16)
    dv = qkv[:, ODD_NORMED_W + C_W:]
    dv_ref[...] = dv.astype(BF16)
    if emit_cache:
        ncv_ref[...] = cv
        lo = lane < HEAD_DIM
        ndk_ref[...] = jnp.where(lo, dk_chunks[0], dk_chunks[1])
        ndv_ref[...] = jnp.where(lo, dv[:, :LANES], dv[:, LANES:])


def _odd_in_call(x, mod, layer, row_of_batch, g, w_in, gains, head_mean, rope_tabs, emit_cache, tm):
    bsz, n, _ = x.shape
    rope = rope_tabs is not None
    tok = lambda w: pl.BlockSpec((None, tm, w), lambda b, i: (b, i, 0))
    in_specs = [
        tok(D_MODEL),
        _mod_spec(layer, 0, row_of_batch),
        _mod_spec(layer, 1, row_of_batch),
        _const_spec((1, D_MODEL)),
        _const_spec((D_MODEL, ODD_W)),
        _const_spec((1, ODD_NORMED_W)),
        _const_spec((LANES, LANES)),
    ]
    args = [x, mod, mod, g, w_in, gains, head_mean]
    if rope:
        in_specs += [pl.BlockSpec((tm, LANES), lambda b, i: (i, 0))] * 2
        args += list(rope_tabs)
    widths = [C_W, C_W, C_W, DQ_W, DKV_DUP_W, DKV_DUP_W]
    out_specs = [tok(w) for w in widths]
    out_shape = [jax.ShapeDtypeStruct((bsz, n, w), BF16) for w in widths]
    if emit_cache:
        cw = [C_W, C_W, DKV_W, DKV_W]
        out_specs += [tok(w) for w in cw]
        out_shape += [jax.ShapeDtypeStruct((bsz, n, w), F32) for w in cw]
    return pl.pallas_call(
        functools.partial(_odd_in_kernel, rope=rope, emit_cache=emit_cache),
        grid=(bsz, n // tm),
        in_specs=in_specs,
        out_specs=out_specs,
        out_shape=out_shape,
        compiler_params=_params(2),
        name="odd_in",
    )(*args)


def _key_chunk(nk):
    return next(c for c in (512, 256, 128) if nk % c == 0)


def _diff_attn_kernel(q_ref, k_ref, vt_ref, lamv_ref, g_ref, o_ref, *, lam_init):
    q = q_ref[...]
    tq = q.shape[0]
    nk = k_ref.shape[0]
    kc = _key_chunk(nk)
    lane = lax.broadcasted_iota(jnp.int32, q.shape, 1)
    zero = jnp.zeros_like(q)
    qq = jnp.concatenate([jnp.where(lane < HEAD_DIM, q, zero),
                          jnp.where(lane >= HEAD_DIM, q, zero)], axis=0)
    m = acc = None
    n_chunks = nk // kc
    s_next = _dot_nt(k_ref[0:kc, :], qq)
    for c in range(n_chunks):
        s = s_next
        if c + 1 < n_chunks:
            s_next = _dot_nt(k_ref[(c + 1) * kc:(c + 2) * kc, :], qq)
        cm = jnp.max(s, axis=0, keepdims=True)
        m_new = cm if m is None else jnp.maximum(m, cm)
        e = jnp.exp2(s - m_new).astype(BF16)
        pv = _dot(vt_ref[:, c * kc:(c + 1) * kc], e)
        acc = pv if acc is None else acc * jnp.exp2(m - m_new) + pv
        m = m_new
    inv = 1.0 / acc[LANES:LANES + 1, :]
    lv = lamv_ref[...]
    lam = (jnp.exp(jnp.sum(lv[0:1] * lv[1:2], keepdims=True))
           - jnp.exp(jnp.sum(lv[2:3] * lv[3:4], keepdims=True)) + lam_init)
    ot = acc[:LANES, :tq] * inv[:, :tq] - acc[:LANES, tq:] * (inv[:, tq:] * lam)
    o = ot.T
    ms = jnp.mean(o * o, axis=-1, keepdims=True)
    y = o * lax.rsqrt(ms + EPS) * g_ref[...] * (1.0 - lam_init)
    o_ref[...] = y.astype(BF16)


def _diff_attn_call(q, k, vt, lamv, subln_g, lam_init, tq):
    bsz, n, _ = q.shape
    nk = k.shape[1]
    return pl.pallas_call(
        functools.partial(_diff_attn_kernel, lam_init=lam_init),
        grid=(bsz, C_HEADS, n // tq),
        in_specs=[
            pl.BlockSpec((None, tq, LANES), lambda b, h, i: (b, i, h)),
            pl.BlockSpec((None, nk, LANES), lambda b, h, i: (b, 0, h)),
            pl.BlockSpec((None, None, V_ROWS, nk), lambda b, h, i: (b, h, 0, 0)),
            _const_spec((4, HEAD_DIM)),
            _const_spec((1, LANES)),
        ],
        out_specs=pl.BlockSpec((None, tq, LANES), lambda b, h, i: (b, i, h)),
        out_shape=jax.ShapeDtypeStruct((bsz, n, C_W), BF16),
        compiler_params=_params(3),
        name="diff_attn",
    )(q, k, vt, lamv, subln_g)


def _sink_attn_kernel(*refs, windowed):
    if windowed:
        (q_ref, ck_ref, k0_ref, k1_ref, k2_ref, cvt_ref, v0_ref, v1_ref, v2_ref,
         bias_ref, sink_ref, o_ref) = refs
    else:
        q_ref, k_ref, vt_ref, sink_ref, o_ref = refs
    qd = q_ref[...]
    tq = qd.shape[0]
    lane = lax.broadcasted_iota(jnp.int32, (tq, LANES), 1)
    lo = lane < HEAD_DIM
    parts = []
    for c in range(2):
        ch = qd[:, c * LANES:(c + 1) * LANES]
        zero = jnp.zeros_like(ch)
        parts += [jnp.where(lo, ch, zero), jnp.where(lo, zero, ch)]
    qq = jnp.concatenate(parts, axis=0)
    if windowed:
        kband = jnp.concatenate([k0_ref[...], k1_ref[...], k2_ref[...]], axis=0)
        vtband = jnp.concatenate([v0_ref[...], v1_ref[...], v2_ref[...]], axis=1)
        scores = [_dot_nt(ck_ref[...], qq), _dot_nt(kband, qq) + bias_ref[...]]
        values = [cvt_ref[...], vtband]
    else:
        scores = [_dot_nt(k_ref[...], qq)]
        values = [vt_ref[...]]
    sink = sink_ref[...]
    m = sink
    for s in scores:
        m = jnp.maximum(m, jnp.max(s, axis=0, keepdims=True))
    acc = None
    for s, vt in zip(scores, values):
        pv = _dot(vt, jnp.exp2(s - m).astype(BF16))
        acc = pv if acc is None else acc + pv
    inv = 1.0 / (acc[LANES:LANES + 1, :] + jnp.exp2(sink - m))
    ot = acc[:LANES, :] * inv
    for c in range(2):
        o0 = ot[:, (2 * c) * tq:(2 * c + 1) * tq].T
        o1 = ot[:, (2 * c + 1) * tq:(2 * c + 2) * tq].T
        o_ref[:, c * LANES:(c + 1) * LANES] = jnp.where(lo, o0, o1).astype(BF16)


def _window_bias(n, tq):
    nblk = n // tq
    kk = np.arange(3 * tq)[:, None]
    qi = np.arange(tq)[None, :]
    band_ok = np.abs(qi + WINDOW - kk) <= WINDOW
    out = []
    for blk in (0, 1, nblk - 1):
        jpos = blk * tq - WINDOW + kk
        ok = band_ok & (jpos >= 0) & (jpos < n)
        out.append(np.tile(np.where(ok, 0.0, NEG_INF), (1, D_GROUP)))
    return jnp.asarray(np.stack(out), F32)


def _sink_attn_call(q, k, vt, sink_rows, ctx_k=None, ctx_vt=None):
    bsz, n, _ = q.shape
    windowed = ctx_k is not None
    tq = WINDOW if windowed else n
    nblk = n // tq
    q_spec = pl.BlockSpec((None, tq, 2 * LANES), lambda b, j, i: (b, i, j))
    sink_spec = pl.BlockSpec((None, 1, D_GROUP * tq), lambda b, j, i: (j, 0, 0))
    if windowed:
        assert nblk >= 3, "first / interior / last mask variants need three query blocks"
        n_ctx = ctx_k.shape[1]
        prev = lambda i: jnp.maximum(i - 1, 0)
        nxt = lambda i: jnp.minimum(i + 1, nblk - 1)
        variant = lambda i: jnp.where(i == 0, 0, jnp.where(i == nblk - 1, 2, 1))
        in_specs = [
            q_spec,
            pl.BlockSpec((None, n_ctx, LANES), lambda b, j, i: (b, 0, j)),
            pl.BlockSpec((None, tq, LANES), lambda b, j, i: (b, prev(i), j)),
            pl.BlockSpec((None, tq, LANES), lambda b, j, i: (b, i, j)),
            pl.BlockSpec((None, tq, LANES), lambda b, j, i: (b, nxt(i), j)),
            pl.BlockSpec((None, None, V_ROWS, n_ctx), lambda b, j, i: (b, j, 0, 0)),
            pl.BlockSpec((None, None, V_ROWS, tq), lambda b, j, i: (b, j, 0, prev(i))),
            pl.BlockSpec((None, None, V_ROWS, tq), lambda b, j, i: (b, j, 0, i)),
            pl.BlockSpec((None, None, V_ROWS, tq), lambda b, j, i: (b, j, 0, nxt(i))),
            pl.BlockSpec((None, 3 * tq, D_GROUP * tq), lambda b, j, i: (variant(i), 0, 0)),
            sink_spec,
        ]
        args = (q, ctx_k, k, k, k, ctx_vt, vt, vt, vt, _window_bias(n, tq), sink_rows)
    else:
        in_specs = [
            q_spec,
            pl.BlockSpec((None, n, LANES), lambda b, j, i: (b, 0, j)),
            pl.BlockSpec((None, None, V_ROWS, n), lambda b, j, i: (b, j, 0, 0)),
            sink_spec,
        ]
        args = (q, k, vt, sink_rows)
    return pl.pallas_call(
        functools.partial(_sink_attn_kernel, windowed=windowed),
        grid=(bsz, D_KV_HEADS, nblk),
        in_specs=in_specs,
        out_specs=pl.BlockSpec((None, tq, 2 * LANES), lambda b, j, i: (b, i, j)),
        out_shape=jax.ShapeDtypeStruct((bsz, n, DQ_W), BF16),
        compiler_params=_params(3),
        name="sink_attn",
    )(*args)


def _dup_heads(a):
    lead = a.shape[:-1]
    a = a.reshape(lead + (D_KV_HEADS, 1, HEAD_DIM))
    return jnp.broadcast_to(a, lead + (D_KV_HEADS, 2, HEAD_DIM)).reshape(lead + (DKV_DUP_W,))


def _odd_weights(w_in_odd, c_qn_g, c_kn_g, d_qn_g, d_kn_g):
    cq, ck, cv, dq, dk, dv = jnp.split(
        w_in_odd, (C_W, 2 * C_W, 3 * C_W, 3 * C_W + DQ_W, 3 * C_W + DQ_W + DKV_W), axis=-1)
    w = jnp.concatenate([cq, ck, dq, _dup_heads(dk), cv, _dup_heads(dv)], axis=-1).astype(BF16)
    gains = jnp.concatenate([
        jnp.tile(c_qn_g * (ATTN_SCALE * LOG2E), C_W // HEAD_DIM),
        jnp.tile(c_kn_g, C_W // HEAD_DIM),
        jnp.tile(d_qn_g * (ATTN_SCALE * LOG2E), DQ_W // HEAD_DIM),
        jnp.tile(d_kn_g, DKV_DUP_W // HEAD_DIM),
    ])[None, :]
    return w, gains


def _value_rows(v):
    bsz, nk, w = v.shape
    vt = v.reshape(bsz, nk, w // LANES, LANES).transpose(0, 2, 3, 1)
    pad = jnp.zeros((bsz, w // LANES, V_ROWS - LANES, nk), v.dtype).at[:, :, 0, :].set(1)
    return jnp.concatenate([vt, pad], axis=2)


def _rope_tables(n):
    rows = n // GRID_W
    row = jnp.repeat(jnp.arange(rows), GRID_W).astype(F32)
    col = jnp.tile(jnp.arange(GRID_W), rows).astype(F32)
    freqs = ROPE_BASE ** (-jnp.arange(ROPE_FREQS, dtype=F32) / ROPE_FREQS)
    ang = jnp.stack([row[:, None] * freqs, col[:, None] * freqs], axis=1)
    ang = jnp.concatenate([ang, ang], axis=-1).reshape(n, HEAD_DIM)
    ang = jnp.concatenate([ang, ang], axis=-1)
    return jnp.cos(ang), jnp.sin(ang)


def _token_tile(n):
    return min(n, 512)


def _run_group(x, mod, row_of_batch, wts, caches):
    bsz, n, _ = x.shape
    decode = caches is not None
    tm = _token_tile(n)
    depth = wts["norm1_g"].shape[0]
    new_cache = []
    for l in range(depth):
        g1n = wts["norm1_g"][l][None, :]
        g2n = wts["norm2_g"][l][None, :]
        if l % 2 == 0:
            e = l // 2
            a, z = _even_in_call(x, mod, l, row_of_batch, g1n, wts["w_in_even"][e], wts["w_pool"][e],
                                 wts["pool_scale"][e][None, :], wts["dftc"], wts["w_fft"][e], tm)
            y = _dft_two_stage_call(z) if n > DFT_DIRECT_MAX else _dft_direct_call(z)
            p1, p2, w_out = a, y, wts["w_out_even"][e]
        else:
            o = l // 2
            lam_init = _lam_init(l)
            rope_tabs = _rope_tables(n) if decode else None
            outs = _odd_in_call(x, mod, l, row_of_batch, g1n, wts["w_in_odd"][o], wts["odd_gains"][o],
                                wts["head_mean"], rope_tabs, not decode, tm)
            cq, ck, cv, dq, dkk, dvv = outs[:6]
            if decode:
                c_k, c_v, d_k, d_v = caches
                lc = c_k.shape[2]
                k_all = jnp.concatenate([c_k[:, o].reshape(bsz, lc, C_W).astype(BF16), ck], axis=1)
                v_all = jnp.concatenate([c_v[:, o].reshape(bsz, lc, C_W).astype(BF16), cv], axis=1)
                ctx_k = _dup_heads(d_k[:, o].reshape(bsz, lc, DKV_W)).astype(BF16)
                ctx_vt = _value_rows(_dup_heads(d_v[:, o].reshape(bsz, lc, DKV_W)).astype(BF16))
                tq = min(n, 256)
            else:
                new_cache.append(outs[6:])
                k_all, v_all = ck, cv
                ctx_k = ctx_vt = None
                tq = n
            c_out = _diff_attn_call(cq, k_all, _value_rows(v_all), wts["lamv"][o],
                                    wts["c_subln_g"][o][None, :], lam_init, tq)
            sink_tq = WINDOW if decode else n
            sink_rows = jnp.repeat(wts["d_sink"][o].reshape(D_KV_HEADS, D_GROUP) * LOG2E, sink_tq,
                                   axis=-1)[:, None, :]
            d_out = _sink_attn_call(dq, dkk, _value_rows(dvv), sink_rows, ctx_k, ctx_vt)
            p1, p2, w_out = c_out, d_out, wts["w_out_odd"][o]
        x = _out_mlp_call(x, p1, p2, mod, l, row_of_batch, g2n, w_out,
                          wts["w_mlp1"][l], wts["w_mlp2"][l], tm)
    return x, new_cache


def kernel(x_prompt, x_sample, c, cache_c_k, cache_c_v, cache_d_k, cache_d_v, c_ctx,
           norm1_g, norm2_g, w_ada, b_ada, w_in_even, w_pool, pool_scale, w_fft, w_out_even,
           w_in_odd, c_qn_g, c_kn_g, lam_q1, lam_k1, lam_q2, lam_k2, c_subln_g,
           d_qn_g, d_kn_g, d_sink, w_out_odd, w_mlp1, w_mlp2):
    depth = norm1_g.shape[0]
    n_odd = w_in_odd.shape[0]
    dec_b = c.shape[0]

    rows = -(-(1 + dec_b) // SUBLANES) * SUBLANES
    cond = jnp.zeros((rows, D_MODEL), F32).at[0].set(c_ctx).at[1:1 + dec_b].set(c)
    mod = _adaln_call(cond, w_ada, b_ada).reshape(depth, rows, 6, 1, D_MODEL)

    ic = np.arange(GROUP_W)
    ang = 2.0 * np.pi * np.outer(ic, ic) / GROUP_W
    dftc = _bf16_const(np.concatenate([np.cos(ang), -np.sin(ang)], axis=1))
    head_mean = _bf16_const(np.kron(np.eye(LANES // HEAD_DIM), np.ones((HEAD_DIM, HEAD_DIM))) / HEAD_DIM)

    odd = [_odd_weights(w_in_odd[o], c_qn_g[o], c_kn_g[o], d_qn_g[o], d_kn_g[o]) for o in range(n_odd)]
    wts = {
        "norm1_g": norm1_g, "norm2_g": norm2_g,
        "w_in_even": w_in_even.astype(BF16), "w_pool": w_pool.astype(BF16), "pool_scale": pool_scale,
        "w_fft": w_fft.astype(BF16), "w_out_even": w_out_even.astype(BF16), "dftc": dftc,
        "w_in_odd": [w for w, _ in odd], "odd_gains": [g for _, g in odd], "head_mean": head_mean,
        "lamv": jnp.stack([lam_q1, lam_k1, lam_q2, lam_k2], axis=1),
        "c_subln_g": c_subln_g, "d_sink": d_sink, "w_out_odd": w_out_odd.astype(BF16),
        "w_mlp1": w_mlp1.astype(BF16), "w_mlp2": w_mlp2.astype(BF16),
    }

    y_prompt, new_cache = _run_group(x_prompt, mod, lambda b: 0, wts, None)
    y_sample, _ = _run_group(x_sample, mod, lambda b: 1 + b, wts,
                             (cache_c_k, cache_c_v, cache_d_k, cache_d_v))

    bsz, n, _ = x_prompt.shape
    new_c_k = jnp.stack([nc[0].reshape(bsz, n, C_HEADS, 2, HEAD_DIM) for nc in new_cache], axis=1)
    new_c_v = jnp.stack([nc[1].reshape(bsz, n, C_HEADS, 2 * HEAD_DIM) for nc in new_cache], axis=1)
    new_d_k = jnp.stack([nc[2].reshape(bsz, n, D_KV_HEADS, HEAD_DIM) for nc in new_cache], axis=1)
    new_d_v = jnp.stack([nc[3].reshape(bsz, n, D_KV_HEADS, HEAD_DIM) for nc in new_cache], axis=1)
    return (y_prompt, y_sample, new_c_k, new_c_v, new_d_k, new_d_v)
```

```python
import functools
import math

import numpy as np
import jax
import jax.numpy as jnp
from jax import lax
from jax.experimental import pallas as pl
from jax.experimental.pallas import tpu as pltpu

F32 = jnp.float32
BF16 = jnp.bfloat16

D_MODEL = 1024
HEAD_DIM = 64
LANES = 128
SUBLANES = 8
GRID_W = 64
ROPE_FREQS = HEAD_DIM // 4
ROPE_BASE = 10000.0
EPS = 1e-6
NEG_INF = -1e30
WINDOW = 128
A_WIDTH = D_MODEL // 2
B_WIDTH = D_MODEL // 2
POOL_WINDOWS = (2, 4, 8, 16)
POOL_HALO = 8
N_GROUPS = 4
GROUP_W = A_WIDTH // N_GROUPS
C_HEADS = 4
D_HEADS = 8
D_KV_HEADS = 2
D_GROUP = D_HEADS // D_KV_HEADS
C_W = C_HEADS * 2 * HEAD_DIM
DQ_W = D_HEADS * HEAD_DIM
DKV_W = D_KV_HEADS * HEAD_DIM
DKV_DUP_W = 2 * DKV_W
D_FF = 4 * D_MODEL
ATTN_SCALE = HEAD_DIM ** -0.5
LOG2E = math.log2(math.e)
V_ROWS = LANES + 16
ODD_NORMED_W = C_W + C_W + DQ_W + DKV_DUP_W
ODD_W = ODD_NORMED_W + C_W + DKV_DUP_W
VMEM_LIMIT = 56 * 1024 * 1024


def _lam_init(layer):
    return 0.8 - 0.6 * math.exp(-0.3 * layer)


def _params(n_axes):
    return pltpu.CompilerParams(dimension_semantics=("arbitrary",) * n_axes,
                                vmem_limit_bytes=VMEM_LIMIT)


def _const_spec(shape):
    nd = len(shape)
    return pl.BlockSpec(shape, lambda *_: (0,) * nd, pipeline_mode=pl.Buffered(1))


def _mod_spec(layer, which, row_of_batch):
    return pl.BlockSpec((None, None, None, 1, D_MODEL),
                        lambda b, i: (layer, row_of_batch(b), which, 0, 0))


def _rms_mod(x, g, sc, sh):
    ms = jnp.mean(x * x, axis=-1, keepdims=True)
    return (x * lax.rsqrt(ms + EPS) * g) * (1.0 + sc) + sh


def _bf16_const(a):
    return jnp.asarray(a, F32).astype(BF16)


def _dot_nt(a, b):
    return lax.dot_general(a, b, (((1,), (1,)), ((), ())), preferred_element_type=F32)


def _dot(a, b):
    return jnp.dot(a, b, preferred_element_type=F32)


def _adaln_kernel(cond_ref, w_ref, b_ref, o_ref):
    cnd = cond_ref[...]
    s = (cnd * jax.nn.sigmoid(cnd)).astype(BF16)
    o_ref[...] = _dot(s, w_ref[...].astype(BF16)) + b_ref[...]


def _adaln_call(cond, w_ada, b_ada):
    depth = w_ada.shape[0]
    rows = cond.shape[0]
    tn = 1536
    return pl.pallas_call(
        _adaln_kernel,
        grid=(depth, 6 * D_MODEL // tn),
        in_specs=[
            pl.BlockSpec((rows, D_MODEL), lambda l, j: (0, 0)),
            pl.BlockSpec((None, D_MODEL, tn), lambda l, j: (l, 0, j)),
            pl.BlockSpec((None, 1, tn), lambda l, j: (l, 0, j)),
        ],
        out_specs=pl.BlockSpec((None, rows, tn), lambda l, j: (l, 0, j)),
        out_shape=jax.ShapeDtypeStruct((depth, rows, 6 * D_MODEL), F32),
        compiler_params=_params(2),
        name="adaln",
    )(cond, w_ada, b_ada.reshape(depth, 1, 6 * D_MODEL))


def _even_in_kernel(x_ref, xp_ref, xn_ref, sh_ref, sc_ref, g_ref, win_ref, wpool_ref,
                    pscale_ref, dftc_ref, wfft_ref, a_ref, z_ref, ext_ref, *, tm, n):
    i = pl.program_id(1)
    n_tiles = n // tm
    g, sc, sh = g_ref[...], sc_ref[...], sh_ref[...]
    h = _rms_mod(x_ref[...], g, sc, sh).astype(BF16)
    u = _dot(h, win_ref[...])
    xh = jnp.concatenate([xp_ref[...], xn_ref[...]], axis=0)
    hh = _rms_mod(xh, g, sc, sh).astype(BF16)
    uh = _dot(hh, win_ref[:, :A_WIDTH])
    has_prev = (i > 0).astype(F32)
    has_next = (i < n_tiles - 1).astype(F32)
    ext_ref[0:POOL_HALO, :] = uh[0:POOL_HALO] * has_prev
    ext_ref[POOL_HALO:POOL_HALO + tm, :] = u[:, :A_WIDTH]
    ext_ref[POOL_HALO + tm:, :] = uh[POOL_HALO:] * has_next

    t = i * tm + lax.broadcasted_iota(jnp.int32, (tm, 1), 0)
    for gi, w in enumerate(POOL_WINDOWS):
        cols = slice(gi * GROUP_W, (gi + 1) * GROUP_W)
        acc = None
        for j in range(-(w // 2), w // 2):
            v = ext_ref[pl.ds(POOL_HALO + j, tm), cols]
            acc = v if acc is None else acc + v
        lo = jnp.maximum(t - w // 2, 0)
        hi = jnp.minimum(t + w // 2, n)
        cnt = (hi - lo).astype(F32)
        pooled = acc / cnt - u[:, cols]
        y = _dot(pooled.astype(BF16), wpool_ref[gi]) * pscale_ref[:, cols]
        a_ref[:, cols] = y.astype(BF16)

    for gi in range(N_GROUPS):
        ub = u[:, A_WIDTH + gi * GROUP_W:A_WIDTH + (gi + 1) * GROUP_W].astype(BF16)
        cs = _dot(ub, dftc_ref[...])
        zr = _dot(cs[:, :GROUP_W].astype(BF16), wfft_ref[gi])
        zi = _dot(cs[:, GROUP_W:].astype(BF16), wfft_ref[gi])
        z_ref[:, gi * GROUP_W:(gi + 1) * GROUP_W] = zr.astype(BF16)
        z_ref[:, B_WIDTH + gi * GROUP_W:B_WIDTH + (gi + 1) * GROUP_W] = zi.astype(BF16)


def _even_in_call(x, mod, layer, row_of_batch, g, w_in, w_pool, pool_scale, dftc, w_fft, tm):
    bsz, n, _ = x.shape
    hb = tm // POOL_HALO
    kern = functools.partial(_even_in_kernel, tm=tm, n=n)
    return pl.pallas_call(
        kern,
        grid=(bsz, n // tm),
        in_specs=[
            pl.BlockSpec((None, tm, D_MODEL), lambda b, i: (b, i, 0)),
            pl.BlockSpec((None, POOL_HALO, D_MODEL),
                         lambda b, i: (b, jnp.maximum(i * hb - 1, 0), 0)),
            pl.BlockSpec((None, POOL_HALO, D_MODEL),
                         lambda b, i: (b, jnp.minimum((i + 1) * hb, n // POOL_HALO - 1), 0)),
            _mod_spec(layer, 0, row_of_batch),
            _mod_spec(layer, 1, row_of_batch),
            _const_spec((1, D_MODEL)),
            _const_spec((D_MODEL, D_MODEL)),
            _const_spec((N_GROUPS, GROUP_W, GROUP_W)),
            _const_spec((1, A_WIDTH)),
            _const_spec((GROUP_W, 2 * GROUP_W)),
            _const_spec((N_GROUPS, GROUP_W, GROUP_W)),
        ],
        out_specs=[
            pl.BlockSpec((None, tm, A_WIDTH), lambda b, i: (b, i, 0)),
            pl.BlockSpec((None, tm, 2 * B_WIDTH), lambda b, i: (b, i, 0)),
        ],
        out_shape=[
            jax.ShapeDtypeStruct((bsz, n, A_WIDTH), BF16),
            jax.ShapeDtypeStruct((bsz, n, 2 * B_WIDTH), BF16),
        ],
        scratch_shapes=[pltpu.VMEM((tm + 2 * POOL_HALO, A_WIDTH), F32)],
        compiler_params=_params(2),
        name="even_in",
    )(x, x, x, mod, mod, g, w_in, w_pool, pool_scale, dftc, w_fft)


def _dft_direct_kernel(z_ref, m_ref, y_ref, *, scale):
    z = z_ref[...]
    rhs = jnp.concatenate([z[:, :B_WIDTH], z[:, B_WIDTH:]], axis=0)
    y_ref[...] = (_dot(m_ref[...], rhs) * scale).astype(BF16)


def _dft_direct_call(z):
    bsz, n, _ = z.shape
    k = np.arange(n)
    ang = 2.0 * np.pi * np.outer(k, k) / n
    m = _bf16_const(np.concatenate([np.cos(ang), np.sin(ang)], axis=1))
    kern = functools.partial(_dft_direct_kernel, scale=float((n * GROUP_W) ** -0.5))
    return pl.pallas_call(
        kern,
        grid=(bsz,),
        in_specs=[pl.BlockSpec((None, n, 2 * B_WIDTH), lambda b: (b, 0, 0)),
                  _const_spec((n, 2 * n))],
        out_specs=pl.BlockSpec((None, n, B_WIDTH), lambda b: (b, 0, 0)),
        out_shape=jax.ShapeDtypeStruct((bsz, n, B_WIDTH), BF16),
        compiler_params=_params(1),
        name="dft_direct",
    )(z, m)


DFT_ROWS = 8
DFT_DIRECT_MAX = 256


def _dft_stage1_kernel(z_ref, m_ref, twr_ref, twi_ref, o_ref, *, na):
    for bi in range(DFT_ROWS):
        zb = z_ref[bi]
        rhs = jnp.concatenate([zb[:, :B_WIDTH], zb[:, B_WIDTH:]], axis=0)
        r = _dot(m_ref[...], rhs)
        br, bim = r[:na], r[na:]
        tr, ti = twr_ref[bi], twi_ref[bi]
        o_ref[bi, :, :B_WIDTH] = (br * tr - bim * ti).astype(BF16)
        o_ref[bi, :, B_WIDTH:] = (br * ti + bim * tr).astype(BF16)


def _dft_stage2_kernel(b_ref, m_ref, y_ref, *, scale):
    for di in range(DFT_ROWS):
        bb = b_ref[di]
        rhs = jnp.concatenate([bb[:, :B_WIDTH], bb[:, B_WIDTH:]], axis=0)
        y_ref[di] = (_dot(m_ref[...], rhs) * scale).astype(BF16)


def _dft_two_stage_call(z):
    bsz, n, _ = z.shape
    na = 1 << (int(math.log2(n)) // 2)
    nb = n // na
    ia = np.arange(na)
    ib = np.arange(nb)
    ang_a = 2.0 * np.pi * np.outer(ia, ia) / na
    fr, fi = np.cos(ang_a), -np.sin(ang_a)
    m1 = _bf16_const(np.block([[fr, -fi], [fi, fr]]))
    ang_t = 2.0 * np.pi * np.outer(ib, ia) / n
    twr = jnp.asarray(np.cos(ang_t)[:, :, None], F32)
    twi = jnp.asarray(-np.sin(ang_t)[:, :, None], F32)
    ang_b = 2.0 * np.pi * np.outer(ib, ib) / nb
    m2 = _bf16_const(np.concatenate([np.cos(ang_b), np.sin(ang_b)], axis=1))

    zt = z.reshape(bsz, na, nb, 2 * B_WIDTH).transpose(0, 2, 1, 3)
    s1 = pl.pallas_call(
        functools.partial(_dft_stage1_kernel, na=na),
        grid=(bsz, nb // DFT_ROWS),
        in_specs=[
            pl.BlockSpec((None, DFT_ROWS, na, 2 * B_WIDTH), lambda b, j: (b, j, 0, 0)),
            _const_spec((2 * na, 2 * na)),
            pl.BlockSpec((DFT_ROWS, na, 1), lambda b, j: (j, 0, 0)),
            pl.BlockSpec((DFT_ROWS, na, 1), lambda b, j: (j, 0, 0)),
        ],
        out_specs=pl.BlockSpec((None, DFT_ROWS, na, 2 * B_WIDTH), lambda b, j: (b, j, 0, 0)),
        out_shape=jax.ShapeDtypeStruct((bsz, nb, na, 2 * B_WIDTH), BF16),
        compiler_params=_params(2),
        name="dft_stage1",
    )(zt, m1, twr, twi)
    bt = s1.transpose(0, 2, 1, 3)
    yt = pl.pallas_call(
        functools.partial(_dft_stage2_kernel, scale=float((n * GROUP_W) ** -0.5)),
        grid=(bsz, na // DFT_ROWS),
        in_specs=[
            pl.BlockSpec((None, DFT_ROWS, nb, 2 * B_WIDTH), lambda b, j: (b, j, 0, 0)),
            _const_spec((nb, 2 * nb)),
        ],
        out_specs=pl.BlockSpec((None, DFT_ROWS, nb, B_WIDTH), lambda b, j: (b, j, 0, 0)),
        out_shape=jax.ShapeDtypeStruct((bsz, na, nb, B_WIDTH), BF16),
        compiler_params=_params(2),
        name="dft_stage2",
    )(bt, m2)
    return yt.transpose(0, 2, 1, 3).reshape(bsz, n, B_WIDTH)


FF_CHUNK = 1024


def _out_mlp_kernel(x_ref, p1_ref, p2_ref, g1_ref, sh_ref, sc_ref, g2_ref, ng_ref,
                    wout_ref, w1_ref, w2_ref, o_ref):
    half = p1_ref.shape[-1]
    mix = _dot(p1_ref[...], wout_ref[:half, :]) + _dot(p2_ref[...], wout_ref[half:, :])
    x1 = x_ref[...] + g1_ref[...] * mix
    h = _rms_mod(x1, ng_ref[...], sc_ref[...], sh_ref[...]).astype(BF16)
    acc = None
    for c in range(D_FF // FF_CHUNK):
        a = _dot(h, w1_ref[:, c * FF_CHUNK:(c + 1) * FF_CHUNK])
        a = jnp.square(jnp.maximum(a, 0.0)).astype(BF16)
        part = _dot(a, w2_ref[c * FF_CHUNK:(c + 1) * FF_CHUNK, :])
        acc = part if acc is None else acc + part
    o_ref[...] = x1 + g2_ref[...] * acc


def _out_mlp_call(x, p1, p2, mod, layer, row_of_batch, ng, w_out, w1, w2, tm):
    bsz, n, _ = x.shape
    half = p1.shape[-1]
    tok = lambda w: pl.BlockSpec((None, tm, w), lambda b, i: (b, i, 0))
    return pl.pallas_call(
        _out_mlp_kernel,
        grid=(bsz, n // tm),
        in_specs=[
            tok(D_MODEL), tok(half), tok(half),
            _mod_spec(layer, 2, row_of_batch),
            _mod_spec(layer, 3, row_of_batch),
            _mod_spec(layer, 4, row_of_batch),
            _mod_spec(layer, 5, row_of_batch),
            _const_spec((1, D_MODEL)),
            _const_spec((2 * half, D_MODEL)),
            _const_spec((D_MODEL, D_FF)),
            _const_spec((D_FF, D_MODEL)),
        ],
        out_specs=tok(D_MODEL),
        out_shape=jax.ShapeDtypeStruct((bsz, n, D_MODEL), F32),
        compiler_params=_params(2),
        name="out_mlp",
    )(x, p1, p2, mod, mod, mod, mod, ng, w_out, w1, w2)


def _odd_in_kernel(*refs, rope, emit_cache):
    x_ref, sh_ref, sc_ref, g_ref, win_ref, gain_ref, hm_ref = refs[:7]
    k = 7
    if rope:
        cos_ref, sin_ref = refs[k:k + 2]
        k += 2
    cq_ref, ck_ref, cvo_ref, dq_ref, dk_ref, dv_ref = refs[k:k + 6]
    k += 6
    if emit_cache:
        nck_ref, ncv_ref, ndk_ref, ndv_ref = refs[k:k + 4]

    h = _rms_mod(x_ref[...], g_ref[...], sc_ref[...], sh_ref[...]).astype(BF16)
    qkv = _dot(h, win_ref[...])
    tm = qkv.shape[0]
    lane = lax.broadcasted_iota(jnp.int32, (tm, LANES), 1)
    first_half = (lane % (2 * ROPE_FREQS)) < ROPE_FREQS

    normed = []
    for c in range(ODD_NORMED_W // LANES):
        v = qkv[:, c * LANES:(c + 1) * LANES]
        ms = _dot((v * v).astype(BF16), hm_ref[...])
        y = v * lax.rsqrt(ms + EPS) * gain_ref[:, c * LANES:(c + 1) * LANES]
        normed.append(y)

    def roped(y):
        if not rope:
            return y
        rot = jnp.where(first_half, -pltpu.roll(y, LANES - ROPE_FREQS, 1),
                        pltpu.roll(y, ROPE_FREQS, 1))
        return y * cos_ref[...] + rot * sin_ref[...]

    nq = C_W // LANES
    for c in range(nq):
        cq_ref[:, c * LANES:(c + 1) * LANES] = roped(normed[c]).astype(BF16)
    for c in range(nq):
        y = normed[nq + c]
        if emit_cache:
            nck_ref[:, c * LANES:(c + 1) * LANES] = y
        ck_ref[:, c * LANES:(c + 1) * LANES] = roped(y).astype(BF16)
    for c in range(DQ_W // LANES):
        dq_ref[:, c * LANES:(c + 1) * LANES] = roped(normed[2 * nq + c]).astype(BF16)
    dk_chunks = []
    for c in range(DKV_DUP_W // LANES):
        y = normed[2 * nq + DQ_W // LANES + c]
        dk_chunks.append(y)
        dk_ref[:, c * LANES:(c + 1) * LANES] = roped(y).astype(BF16)
    cv = qkv[:, ODD_NORMED_W:ODD_NORMED_W + C_W]
    cvo_ref[...] = cv.astype(BF16)
    dv = qkv[:, ODD_NORMED_W + C_W:]
    dv_ref[...] = dv.astype(BF16)
    if emit_cache:
        ncv_ref[...] = cv
        lo = lane < HEAD_DIM
        ndk_ref[...] = jnp.where(lo, dk_chunks[0], dk_chunks[1])
        ndv_ref[...] = jnp.where(lo, dv[:, :LANES], dv[:, LANES:])


def _odd_in_call(x, mod, layer, row_of_batch, g, w_in, gains, head_mean, rope_tabs, emit_cache, tm):
    bsz, n, _ = x.shape
    rope = rope_tabs is not None
    tok = lambda w: pl.BlockSpec((None, tm, w), lambda b, i: (b, i, 0))
    in_specs = [
        tok(D_MODEL),
        _mod_spec(layer, 0, row_of_batch),
        _mod_spec(layer, 1, row_of_batch),
        _const_spec((1, D_MODEL)),
        _const_spec((D_MODEL, ODD_W)),
        _const_spec((1, ODD_NORMED_W)),
        _const_spec((LANES, LANES)),
    ]
    args = [x, mod, mod, g, w_in, gains, head_mean]
    if rope:
        in_specs += [pl.BlockSpec((tm, LANES), lambda b, i: (i, 0))] * 2
        args += list(rope_tabs)
    widths = [C_W, C_W, C_W, DQ_W, DKV_DUP_W, DKV_DUP_W]
    out_specs = [tok(w) for w in widths]
    out_shape = [jax.ShapeDtypeStruct((bsz, n, w), BF16) for w in widths]
    if emit_cache:
        cw = [C_W, C_W, DKV_W, DKV_W]
        out_specs += [tok(w) for w in cw]
        out_shape += [jax.ShapeDtypeStruct((bsz, n, w), F32) for w in cw]
    return pl.pallas_call(
        functools.partial(_odd_in_kernel, rope=rope, emit_cache=emit_cache),
        grid=(bsz, n // tm),
        in_specs=in_specs,
        out_specs=out_specs,
        out_shape=out_shape,
        compiler_params=_params(2),
        name="odd_in",
    )(*args)


QK_AHEAD = 3


def _key_chunk(nk):
    return next(c for c in (512, 256, 128) if nk % c == 0)


def _diff_attn_kernel(q_ref, k_ref, vt_ref, lamv_ref, g_ref, o_ref, *, lam_init, tq):
    nk = k_ref.shape[0]
    kc = _key_chunk(nk)
    n_chunks = nk // kc
    tiles = q_ref.shape[0] // tq
    lane = lax.broadcasted_iota(jnp.int32, (tq, LANES), 1)
    lv = lamv_ref[...]
    lam = (jnp.exp(jnp.sum(lv[0:1] * lv[1:2], keepdims=True))
           - jnp.exp(jnp.sum(lv[2:3] * lv[3:4], keepdims=True)) + lam_init)

    qqs = {}

    def score(item):
        t, c = item
        if t not in qqs:
            q = q_ref[t * tq:(t + 1) * tq, :]
            zero = jnp.zeros_like(q)
            qqs[t] = jnp.concatenate([jnp.where(lane < HEAD_DIM, q, zero),
                                      jnp.where(lane >= HEAD_DIM, q, zero)], axis=0)
        return _dot_nt(k_ref[c * kc:(c + 1) * kc, :], qqs[t])

    def finish(t, acc):
        inv = 1.0 / acc[LANES:LANES + 1, :]
        ot = acc[:LANES, :tq] * inv[:, :tq] - acc[:LANES, tq:] * (inv[:, tq:] * lam)
        o = ot.T
        ms = jnp.mean(o * o, axis=-1, keepdims=True)
        y = o * lax.rsqrt(ms + EPS) * g_ref[...] * (1.0 - lam_init)
        o_ref[t * tq:(t + 1) * tq, :] = y.astype(BF16)

    items = [(t, c) for t in range(tiles) for c in range(n_chunks)]
    ahead = [score(it) for it in items[:QK_AHEAD]]
    m = acc = None
    for idx, (t, c) in enumerate(items):
        s = ahead.pop(0)
        if idx + QK_AHEAD < len(items):
            ahead.append(score(items[idx + QK_AHEAD]))
        cm = jnp.max(s, axis=0, keepdims=True)
        m_new = cm if c == 0 else jnp.maximum(m, cm)
        e = jnp.exp2(s - m_new).astype(BF16)
        pv = _dot(vt_ref[:, c * kc:(c + 1) * kc], e)
        acc = pv if c == 0 else acc * jnp.exp2(m - m_new) + pv
        m = m_new
        if c == n_chunks - 1:
            finish(t, acc)


def _diff_attn_call(q, k, vt, lamv, subln_g, lam_init, tq, tiles):
    bsz, n, _ = q.shape
    nk = k.shape[1]
    return pl.pallas_call(
        functools.partial(_diff_attn_kernel, lam_init=lam_init, tq=tq),
        grid=(bsz, C_HEADS, n // (tq * tiles)),
        in_specs=[
            pl.BlockSpec((None, tiles * tq, LANES), lambda b, h, i: (b, i, h)),
            pl.BlockSpec((None, nk, LANES), lambda b, h, i: (b, 0, h)),
            pl.BlockSpec((None, None, V_ROWS, nk), lambda b, h, i: (b, h, 0, 0)),
            _const_spec((4, HEAD_DIM)),
            _const_spec((1, LANES)),
        ],
        out_specs=pl.BlockSpec((None, tiles * tq, LANES), lambda b, h, i: (b, i, h)),
        out_shape=jax.ShapeDtypeStruct((bsz, n, C_W), BF16),
        compiler_params=_params(3),
        name="diff_attn",
    )(q, k, vt, lamv, subln_g)


def _sink_attn_kernel(*refs, windowed, tq, nblk):
    if windowed:
        (q_ref, ck_ref, kp_ref, km_ref, kn_ref, cvt_ref, vp_ref, vm_ref, vn_ref,
         bias_ref, sink_ref, o_ref) = refs
        kband = jnp.concatenate([kp_ref[...], km_ref[...], kn_ref[...]], axis=0)
        vtband = jnp.concatenate([vp_ref[...], vm_ref[...], vn_ref[...]], axis=1)
    else:
        q_ref, k_ref, vt_ref, sink_ref, o_ref = refs
    subs = q_ref.shape[0] // tq
    lane = lax.broadcasted_iota(jnp.int32, (tq, LANES), 1)
    lo = lane < HEAD_DIM
    sink = sink_ref[...]

    def scores_of(sub):
        qd = q_ref[sub * tq:(sub + 1) * tq, :]
        parts = []
        for c in range(2):
            ch = qd[:, c * LANES:(c + 1) * LANES]
            zero = jnp.zeros_like(ch)
            parts += [jnp.where(lo, ch, zero), jnp.where(lo, zero, ch)]
        qq = jnp.concatenate(parts, axis=0)
        if not windowed:
            return [_dot_nt(k_ref[...], qq)]
        blk = pl.program_id(2) * subs + sub
        variant = jnp.where(blk == 0, 0, jnp.where(blk == nblk - 1, 2, 1))
        band = _dot_nt(kband[sub * tq:(sub + 3) * tq, :], qq) + bias_ref[variant]
        return [_dot_nt(ck_ref[...], qq), band]

    def finish(sub, scores):
        values = [cvt_ref[...], vtband[:, sub * tq:(sub + 3) * tq]] if windowed else [vt_ref[...]]
        m = sink
        for s in scores:
            m = jnp.maximum(m, jnp.max(s, axis=0, keepdims=True))
        acc = None
        for s, vt in zip(scores, values):
            pv = _dot(vt, jnp.exp2(s - m).astype(BF16))
            acc = pv if acc is None else acc + pv
        inv = 1.0 / (acc[LANES:LANES + 1, :] + jnp.exp2(sink - m))
        ot = acc[:LANES, :] * inv
        for c in range(2):
            o0 = ot[:, (2 * c) * tq:(2 * c + 1) * tq].T
            o1 = ot[:, (2 * c + 1) * tq:(2 * c + 2) * tq].T
            o_ref[sub * tq:(sub + 1) * tq, c * LANES:(c + 1) * LANES] = jnp.where(lo, o0, o1).astype(BF16)

    ahead = [scores_of(sub) for sub in range(min(QK_AHEAD, subs))]
    for sub in range(subs):
        cur = ahead.pop(0)
        if sub + QK_AHEAD < subs:
            ahead.append(scores_of(sub + QK_AHEAD))
        finish(sub, cur)


def _window_bias(n, tq):
    nblk = n // tq
    kk = np.arange(3 * tq)[:, None]
    qi = np.arange(tq)[None, :]
    band_ok = np.abs(qi + WINDOW - kk) <= WINDOW
    out = []
    for blk in (0, 1, nblk - 1):
        jpos = blk * tq - WINDOW + kk
        ok = band_ok & (jpos >= 0) & (jpos < n)
        out.append(np.tile(np.where(ok, 0.0, NEG_INF), (1, D_GROUP)))
    return jnp.asarray(np.stack(out), F32)


def _sink_attn_call(q, k, vt, sink_rows, ctx_k=None, ctx_vt=None):
    bsz, n, _ = q.shape
    windowed = ctx_k is not None
    tq = WINDOW if windowed else n
    nblk = n // tq
    subs = next(s for s in (4, 2, 1) if nblk % s == 0) if windowed else 1
    steps = nblk // subs
    q_spec = pl.BlockSpec((None, subs * tq, 2 * LANES), lambda b, j, i: (b, i, j))
    sink_spec = pl.BlockSpec((None, 1, D_GROUP * tq), lambda b, j, i: (j, 0, 0))
    if windowed:
        assert nblk >= 3, "first / interior / last mask variants need three query blocks"
        n_ctx = ctx_k.shape[1]
        prev = lambda i: jnp.maximum(i * subs - 1, 0)
        nxt = lambda i: jnp.minimum((i + 1) * subs, nblk - 1)
        in_specs = [
            q_spec,
            pl.BlockSpec((None, n_ctx, LANES), lambda b, j, i: (b, 0, j)),
            pl.BlockSpec((None, tq, LANES), lambda b, j, i: (b, prev(i), j)),
            pl.BlockSpec((None, subs * tq, LANES), lambda b, j, i: (b, i, j)),
            pl.BlockSpec((None, tq, LANES), lambda b, j, i: (b, nxt(i), j)),
            pl.BlockSpec((None, None, V_ROWS, n_ctx), lambda b, j, i: (b, j, 0, 0)),
            pl.BlockSpec((None, None, V_ROWS, tq), lambda b, j, i: (b, j, 0, prev(i))),
            pl.BlockSpec((None, None, V_ROWS, subs * tq), lambda b, j, i: (b, j, 0, i)),
            pl.BlockSpec((None, None, V_ROWS, tq), lambda b, j, i: (b, j, 0, nxt(i))),
            _const_spec((3, 3 * tq, D_GROUP * tq)),
            sink_spec,
        ]
        args = (q, ctx_k, k, k, k, ctx_vt, vt, vt, vt, _window_bias(n, tq), sink_rows)
    else:
        in_specs = [
            q_spec,
            pl.BlockSpec((None, n, LANES), lambda b, j, i: (b, 0, j)),
            pl.BlockSpec((None, None, V_ROWS, n), lambda b, j, i: (b, j, 0, 0)),
            sink_spec,
        ]
        args = (q, k, vt, sink_rows)
    return pl.pallas_call(
        functools.partial(_sink_attn_kernel, windowed=windowed, tq=tq, nblk=nblk),
        grid=(bsz, D_KV_HEADS, steps),
        in_specs=in_specs,
        out_specs=pl.BlockSpec((None, subs * tq, 2 * LANES), lambda b, j, i: (b, i, j)),
        out_shape=jax.ShapeDtypeStruct((bsz, n, DQ_W), BF16),
        compiler_params=_params(3),
        name="sink_attn",
    )(*args)


def _dup_heads(a):
    lead = a.shape[:-1]
    a = a.reshape(lead + (D_KV_HEADS, 1, HEAD_DIM))
    return jnp.broadcast_to(a, lead + (D_KV_HEADS, 2, HEAD_DIM)).reshape(lead + (DKV_DUP_W,))


def _odd_weights(w_in_odd, c_qn_g, c_kn_g, d_qn_g, d_kn_g):
    cq, ck, cv, dq, dk, dv = jnp.split(
        w_in_odd, (C_W, 2 * C_W, 3 * C_W, 3 * C_W + DQ_W, 3 * C_W + DQ_W + DKV_W), axis=-1)
    w = jnp.concatenate([cq, ck, dq, _dup_heads(dk), cv, _dup_heads(dv)], axis=-1).astype(BF16)
    gains = jnp.concatenate([
        jnp.tile(c_qn_g * (ATTN_SCALE * LOG2E), C_W // HEAD_DIM),
        jnp.tile(c_kn_g, C_W // HEAD_DIM),
        jnp.tile(d_qn_g * (ATTN_SCALE * LOG2E), DQ_W // HEAD_DIM),
        jnp.tile(d_kn_g, DKV_DUP_W // HEAD_DIM),
    ])[None, :]
    return w, gains


def _value_rows(v):
    bsz, nk, w = v.shape
    vt = v.reshape(bsz, nk, w // LANES, LANES).transpose(0, 2, 3, 1)
    pad = jnp.zeros((bsz, w // LANES, V_ROWS - LANES, nk), v.dtype).at[:, :, 0, :].set(1)
    return jnp.concatenate([vt, pad], axis=2)


def _rope_tables(n):
    rows = n // GRID_W
    row = jnp.repeat(jnp.arange(rows), GRID_W).astype(F32)
    col = jnp.tile(jnp.arange(GRID_W), rows).astype(F32)
    freqs = ROPE_BASE ** (-jnp.arange(ROPE_FREQS, dtype=F32) / ROPE_FREQS)
    ang = jnp.stack([row[:, None] * freqs, col[:, None] * freqs], axis=1)
    ang = jnp.concatenate([ang, ang], axis=-1).reshape(n, HEAD_DIM)
    ang = jnp.concatenate([ang, ang], axis=-1)
    return jnp.cos(ang), jnp.sin(ang)


def _token_tile(n):
    return min(n, 512)


def _run_group(x, mod, row_of_batch, wts, caches):
    bsz, n, _ = x.shape
    decode = caches is not None
    tm = _token_tile(n)
    depth = wts["norm1_g"].shape[0]
    new_cache = []
    for l in range(depth):
        g1n = wts["norm1_g"][l][None, :]
        g2n = wts["norm2_g"][l][None, :]
        if l % 2 == 0:
            e = l // 2
            a, z = _even_in_call(x, mod, l, row_of_batch, g1n, wts["w_in_even"][e], wts["w_pool"][e],
                                 wts["pool_scale"][e][None, :], wts["dftc"], wts["w_fft"][e], tm)
            y = _dft_two_stage_call(z) if n > DFT_DIRECT_MAX else _dft_direct_call(z)
            p1, p2, w_out = a, y, wts["w_out_even"][e]
        else:
            o = l // 2
            lam_init = _lam_init(l)
            rope_tabs = _rope_tables(n) if decode else None
            outs = _odd_in_call(x, mod, l, row_of_batch, g1n, wts["w_in_odd"][o], wts["odd_gains"][o],
                                wts["head_mean"], rope_tabs, not decode, tm)
            cq, ck, cv, dq, dkk, dvv = outs[:6]
            if decode:
                c_k, c_v, d_k, d_v = caches
                lc = c_k.shape[2]
                k_all = jnp.concatenate([c_k[:, o].reshape(bsz, lc, C_W).astype(BF16), ck], axis=1)
                v_all = jnp.concatenate([c_v[:, o].reshape(bsz, lc, C_W).astype(BF16), cv], axis=1)
                ctx_k = _dup_heads(d_k[:, o].reshape(bsz, lc, DKV_W)).astype(BF16)
                ctx_vt = _value_rows(_dup_heads(d_v[:, o].reshape(bsz, lc, DKV_W)).astype(BF16))
                tq = min(n, 256)
            else:
                new_cache.append(outs[6:])
                k_all, v_all = ck, cv
                ctx_k = ctx_vt = None
                tq = n
            tiles = 2 if n % (2 * tq) == 0 else 1
            c_out = _diff_attn_call(cq, k_all, _value_rows(v_all), wts["lamv"][o],
                                    wts["c_subln_g"][o][None, :], lam_init, tq, tiles)
            sink_tq = WINDOW if decode else n
            sink_rows = jnp.repeat(wts["d_sink"][o].reshape(D_KV_HEADS, D_GROUP) * LOG2E, sink_tq,
                                   axis=-1)[:, None, :]
            d_out = _sink_attn_call(dq, dkk, _value_rows(dvv), sink_rows, ctx_k, ctx_vt)
            p1, p2, w_out = c_out, d_out, wts["w_out_odd"][o]
        x = _out_mlp_call(x, p1, p2, mod, l, row_of_batch, g2n, w_out,
                          wts["w_mlp1"][l], wts["w_mlp2"][l], tm)
    return x, new_cache


def kernel(x_prompt, x_sample, c, cache_c_k, cache_c_v, cache_d_k, cache_d_v, c_ctx,
           norm1_g, norm2_g, w_ada, b_ada, w_in_even, w_pool, pool_scale, w_fft, w_out_even,
           w_in_odd, c_qn_g, c_kn_g, lam_q1, lam_k1, lam_q2, lam_k2, c_subln_g,
           d_qn_g, d_kn_g, d_sink, w_out_odd, w_mlp1, w_mlp2):
    depth = norm1_g.shape[0]
    n_odd = w_in_odd.shape[0]
    dec_b = c.shape[0]

    rows = -(-(1 + dec_b) // SUBLANES) * SUBLANES
    cond = jnp.zeros((rows, D_MODEL), F32).at[0].set(c_ctx).at[1:1 + dec_b].set(c)
    mod = _adaln_call(cond, w_ada, b_ada).reshape(depth, rows, 6, 1, D_MODEL)

    ic = np.arange(GROUP_W)
    ang = 2.0 * np.pi * np.outer(ic, ic) / GROUP_W
    dftc = _bf16_const(np.concatenate([np.cos(ang), -np.sin(ang)], axis=1))
    head_mean = _bf16_const(np.kron(np.eye(LANES // HEAD_DIM), np.ones((HEAD_DIM, HEAD_DIM))) / HEAD_DIM)

    odd = [_odd_weights(w_in_odd[o], c_qn_g[o], c_kn_g[o], d_qn_g[o], d_kn_g[o]) for o in range(n_odd)]
    wts = {
        "norm1_g": norm1_g, "norm2_g": norm2_g,
        "w_in_even": w_in_even.astype(BF16), "w_pool": w_pool.astype(BF16), "pool_scale": pool_scale,
        "w_fft": w_fft.astype(BF16), "w_out_even": w_out_even.astype(BF16), "dftc": dftc,
        "w_in_odd": [w for w, _ in odd], "odd_gains": [g for _, g in odd], "head_mean": head_mean,
        "lamv": jnp.stack([lam_q1, lam_k1, lam_q2, lam_k2], axis=1),
        "c_subln_g": c_subln_g, "d_sink": d_sink, "w_out_odd": w_out_odd.astype(BF16),
        "w_mlp1": w_mlp1.astype(BF16), "w_mlp2": w_mlp2.astype(BF16),
    }

    y_prompt, new_cache = _run_group(x_prompt, mod, lambda b: 0, wts, None)
    y_sample, _ = _run_group(x_sample, mod, lambda b: 1 + b, wts,
                             (cache_c_k, cache_c_v, cache_d_k, cache_d_v))

    bsz, n, _ = x_prompt.shape
    new_c_k = jnp.stack([nc[0].reshape(bsz, n, C_HEADS, 2, HEAD_DIM) for nc in new_cache], axis=1)
    new_c_v = jnp.stack([nc[1].reshape(bsz, n, C_HEADS, 2 * HEAD_DIM) for nc in new_cache], axis=1)
    new_d_k = jnp.stack([nc[2].reshape(bsz, n, D_KV_HEADS, HEAD_DIM) for nc in new_cache], axis=1)
    new_d_v = jnp.stack([nc[3].reshape(bsz, n, D_KV_HEADS, HEAD_DIM) for nc in new_cache], axis=1)
    return (y_prompt, y_sample, new_c_k, new_c_v, new_d_k, new_d_v)
```

```python
import functools
import math

import numpy as np
import jax
import jax.numpy as jnp
from jax import lax
from jax.experimental import pallas as pl
from jax.experimental.pallas import tpu as pltpu

F32 = jnp.float32
BF16 = jnp.bfloat16

D_MODEL = 1024
HEAD_DIM = 64
LANES = 128
SUBLANES = 8
GRID_W = 64
ROPE_FREQS = HEAD_DIM // 4
ROPE_BASE = 10000.0
EPS = 1e-6
NEG_INF = -1e30
WINDOW = 128
A_WIDTH = D_MODEL // 2
B_WIDTH = D_MODEL // 2
POOL_WINDOWS = (2, 4, 8, 16)
POOL_HALO = 8
N_GROUPS = 4
GROUP_W = A_WIDTH // N_GROUPS
C_HEADS = 4
D_HEADS = 8
D_KV_HEADS = 2
D_GROUP = D_HEADS // D_KV_HEADS
C_W = C_HEADS * 2 * HEAD_DIM
DQ_W = D_HEADS * HEAD_DIM
DKV_W = D_KV_HEADS * HEAD_DIM
DKV_DUP_W = 2 * DKV_W
D_FF = 4 * D_MODEL
ATTN_SCALE = HEAD_DIM ** -0.5
LOG2E = math.log2(math.e)
V_ROWS = LANES + 16
ODD_NORMED_W = C_W + C_W + DQ_W + DKV_DUP_W
ODD_W = ODD_NORMED_W + C_W + DKV_DUP_W
VMEM_LIMIT = 56 * 1024 * 1024


def _lam_init(layer):
    return 0.8 - 0.6 * math.exp(-0.3 * layer)


def _params(n_axes):
    return pltpu.CompilerParams(dimension_semantics=("arbitrary",) * n_axes,
                                vmem_limit_bytes=VMEM_LIMIT)


def _const_spec(shape):
    nd = len(shape)
    return pl.BlockSpec(shape, lambda *_: (0,) * nd, pipeline_mode=pl.Buffered(1))


def _mod_spec(layer, which, row_of_batch):
    return pl.BlockSpec((None, None, None, 1, D_MODEL),
                        lambda b, i: (layer, row_of_batch(b), which, 0, 0))


def _rms_mod(x, g, sc, sh):
    ms = jnp.mean(x * x, axis=-1, keepdims=True)
    return (x * lax.rsqrt(ms + EPS) * g) * (1.0 + sc) + sh


def _bf16_const(a):
    return jnp.asarray(a, F32).astype(BF16)


def _dot_nt(a, b):
    return lax.dot_general(a, b, (((1,), (1,)), ((), ())), preferred_element_type=F32)


def _dot(a, b):
    return jnp.dot(a, b, preferred_element_type=F32)


def _adaln_kernel(cond_ref, w_ref, b_ref, o_ref):
    cnd = cond_ref[...]
    s = (cnd * jax.nn.sigmoid(cnd)).astype(BF16)
    o_ref[...] = _dot(s, w_ref[...].astype(BF16)) + b_ref[...]


def _adaln_call(cond, w_ada, b_ada):
    depth = w_ada.shape[0]
    rows = cond.shape[0]
    tn = 1536
    return pl.pallas_call(
        _adaln_kernel,
        grid=(depth, 6 * D_MODEL // tn),
        in_specs=[
            pl.BlockSpec((rows, D_MODEL), lambda l, j: (0, 0)),
            pl.BlockSpec((None, D_MODEL, tn), lambda l, j: (l, 0, j)),
            pl.BlockSpec((None, 1, tn), lambda l, j: (l, 0, j)),
        ],
        out_specs=pl.BlockSpec((None, rows, tn), lambda l, j: (l, 0, j)),
        out_shape=jax.ShapeDtypeStruct((depth, rows, 6 * D_MODEL), F32),
        compiler_params=_params(2),
        name="adaln",
    )(cond, w_ada, b_ada.reshape(depth, 1, 6 * D_MODEL))


def _even_in_kernel(x_ref, xp_ref, xn_ref, sh_ref, sc_ref, g_ref, win_ref, wpool_ref,
                    pscale_ref, dftc_ref, wfft_ref, a_ref, z_ref, ext_ref, *, tm, n):
    i = pl.program_id(1)
    n_tiles = n // tm
    g, sc, sh = g_ref[...], sc_ref[...], sh_ref[...]
    h = _rms_mod(x_ref[...], g, sc, sh).astype(BF16)
    u = _dot(h, win_ref[...])
    xh = jnp.concatenate([xp_ref[...], xn_ref[...]], axis=0)
    hh = _rms_mod(xh, g, sc, sh).astype(BF16)
    uh = _dot(hh, win_ref[:, :A_WIDTH])
    has_prev = (i > 0).astype(F32)
    has_next = (i < n_tiles - 1).astype(F32)
    ext_ref[0:POOL_HALO, :] = uh[0:POOL_HALO] * has_prev
    ext_ref[POOL_HALO:POOL_HALO + tm, :] = u[:, :A_WIDTH]
    ext_ref[POOL_HALO + tm:, :] = uh[POOL_HALO:] * has_next

    t = i * tm + lax.broadcasted_iota(jnp.int32, (tm, 1), 0)
    for gi, w in enumerate(POOL_WINDOWS):
        cols = slice(gi * GROUP_W, (gi + 1) * GROUP_W)
        acc = None
        for j in range(-(w // 2), w // 2):
            v = ext_ref[pl.ds(POOL_HALO + j, tm), cols]
            acc = v if acc is None else acc + v
        lo = jnp.maximum(t - w // 2, 0)
        hi = jnp.minimum(t + w // 2, n)
        cnt = (hi - lo).astype(F32)
        pooled = acc / cnt - u[:, cols]
        y = _dot(pooled.astype(BF16), wpool_ref[gi]) * pscale_ref[:, cols]
        a_ref[:, cols] = y.astype(BF16)

    for gi in range(N_GROUPS):
        ub = u[:, A_WIDTH + gi * GROUP_W:A_WIDTH + (gi + 1) * GROUP_W].astype(BF16)
        cs = _dot(ub, dftc_ref[...])
        zr = _dot(cs[:, :GROUP_W].astype(BF16), wfft_ref[gi])
        zi = _dot(cs[:, GROUP_W:].astype(BF16), wfft_ref[gi])
        z_ref[:, gi * GROUP_W:(gi + 1) * GROUP_W] = zr.astype(BF16)
        z_ref[:, B_WIDTH + gi * GROUP_W:B_WIDTH + (gi + 1) * GROUP_W] = zi.astype(BF16)


def _even_in_call(x, mod, layer, row_of_batch, g, w_in, w_pool, pool_scale, dftc, w_fft, tm):
    bsz, n, _ = x.shape
    hb = tm // POOL_HALO
    kern = functools.partial(_even_in_kernel, tm=tm, n=n)
    return pl.pallas_call(
        kern,
        grid=(bsz, n // tm),
        in_specs=[
            pl.BlockSpec((None, tm, D_MODEL), lambda b, i: (b, i, 0)),
            pl.BlockSpec((None, POOL_HALO, D_MODEL),
                         lambda b, i: (b, jnp.maximum(i * hb - 1, 0), 0)),
            pl.BlockSpec((None, POOL_HALO, D_MODEL),
                         lambda b, i: (b, jnp.minimum((i + 1) * hb, n // POOL_HALO - 1), 0)),
            _mod_spec(layer, 0, row_of_batch),
            _mod_spec(layer, 1, row_of_batch),
            _const_spec((1, D_MODEL)),
            _const_spec((D_MODEL, D_MODEL)),
            _const_spec((N_GROUPS, GROUP_W, GROUP_W)),
            _const_spec((1, A_WIDTH)),
            _const_spec((GROUP_W, 2 * GROUP_W)),
            _const_spec((N_GROUPS, GROUP_W, GROUP_W)),
        ],
        out_specs=[
            pl.BlockSpec((None, tm, A_WIDTH), lambda b, i: (b, i, 0)),
            pl.BlockSpec((None, tm, 2 * B_WIDTH), lambda b, i: (b, i, 0)),
        ],
        out_shape=[
            jax.ShapeDtypeStruct((bsz, n, A_WIDTH), BF16),
            jax.ShapeDtypeStruct((bsz, n, 2 * B_WIDTH), BF16),
        ],
        scratch_shapes=[pltpu.VMEM((tm + 2 * POOL_HALO, A_WIDTH), F32)],
        compiler_params=_params(2),
        name="even_in",
    )(x, x, x, mod, mod, g, w_in, w_pool, pool_scale, dftc, w_fft)


def _dft_direct_kernel(z_ref, m_ref, y_ref, *, scale):
    z = z_ref[...]
    rhs = jnp.concatenate([z[:, :B_WIDTH], z[:, B_WIDTH:]], axis=0)
    y_ref[...] = (_dot(m_ref[...], rhs) * scale).astype(BF16)


def _dft_direct_call(z):
    bsz, n, _ = z.shape
    k = np.arange(n)
    ang = 2.0 * np.pi * np.outer(k, k) / n
    m = _bf16_const(np.concatenate([np.cos(ang), np.sin(ang)], axis=1))
    kern = functools.partial(_dft_direct_kernel, scale=float((n * GROUP_W) ** -0.5))
    return pl.pallas_call(
        kern,
        grid=(bsz,),
        in_specs=[pl.BlockSpec((None, n, 2 * B_WIDTH), lambda b: (b, 0, 0)),
                  _const_spec((n, 2 * n))],
        out_specs=pl.BlockSpec((None, n, B_WIDTH), lambda b: (b, 0, 0)),
        out_shape=jax.ShapeDtypeStruct((bsz, n, B_WIDTH), BF16),
        compiler_params=_params(1),
        name="dft_direct",
    )(z, m)


DFT_ROWS = 8
DFT_DIRECT_MAX = 256


def _dft_stage1_kernel(z_ref, m_ref, twr_ref, twi_ref, o_ref, *, na):
    for bi in range(DFT_ROWS):
        zb = z_ref[bi]
        rhs = jnp.concatenate([zb[:, :B_WIDTH], zb[:, B_WIDTH:]], axis=0)
        r = _dot(m_ref[...], rhs)
        br, bim = r[:na], r[na:]
        tr, ti = twr_ref[bi], twi_ref[bi]
        o_ref[bi, :, :B_WIDTH] = (br * tr - bim * ti).astype(BF16)
        o_ref[bi, :, B_WIDTH:] = (br * ti + bim * tr).astype(BF16)


def _dft_stage2_kernel(b_ref, m_ref, y_ref, *, scale):
    for di in range(DFT_ROWS):
        bb = b_ref[di]
        rhs = jnp.concatenate([bb[:, :B_WIDTH], bb[:, B_WIDTH:]], axis=0)
        y_ref[di] = (_dot(m_ref[...], rhs) * scale).astype(BF16)


def _dft_two_stage_call(z):
    bsz, n, _ = z.shape
    na = 1 << (int(math.log2(n)) // 2)
    nb = n // na
    ia = np.arange(na)
    ib = np.arange(nb)
    ang_a = 2.0 * np.pi * np.outer(ia, ia) / na
    fr, fi = np.cos(ang_a), -np.sin(ang_a)
    m1 = _bf16_const(np.block([[fr, -fi], [fi, fr]]))
    ang_t = 2.0 * np.pi * np.outer(ib, ia) / n
    twr = jnp.asarray(np.cos(ang_t)[:, :, None], F32)
    twi = jnp.asarray(-np.sin(ang_t)[:, :, None], F32)
    ang_b = 2.0 * np.pi * np.outer(ib, ib) / nb
    m2 = _bf16_const(np.concatenate([np.cos(ang_b), np.sin(ang_b)], axis=1))

    zt = z.reshape(bsz, na, nb, 2 * B_WIDTH).transpose(0, 2, 1, 3)
    s1 = pl.pallas_call(
        functools.partial(_dft_stage1_kernel, na=na),
        grid=(bsz, nb // DFT_ROWS),
        in_specs=[
            pl.BlockSpec((None, DFT_ROWS, na, 2 * B_WIDTH), lambda b, j: (b, j, 0, 0)),
            _const_spec((2 * na, 2 * na)),
            pl.BlockSpec((DFT_ROWS, na, 1), lambda b, j: (j, 0, 0)),
            pl.BlockSpec((DFT_ROWS, na, 1), lambda b, j: (j, 0, 0)),
        ],
        out_specs=pl.BlockSpec((None, DFT_ROWS, na, 2 * B_WIDTH), lambda b, j: (b, j, 0, 0)),
        out_shape=jax.ShapeDtypeStruct((bsz, nb, na, 2 * B_WIDTH), BF16),
        compiler_params=_params(2),
        name="dft_stage1",
    )(zt, m1, twr, twi)
    bt = s1.transpose(0, 2, 1, 3)
    yt = pl.pallas_call(
        functools.partial(_dft_stage2_kernel, scale=float((n * GROUP_W) ** -0.5)),
        grid=(bsz, na // DFT_ROWS),
        in_specs=[
            pl.BlockSpec((None, DFT_ROWS, nb, 2 * B_WIDTH), lambda b, j: (b, j, 0, 0)),
            _const_spec((nb, 2 * nb)),
        ],
        out_specs=pl.BlockSpec((None, DFT_ROWS, nb, B_WIDTH), lambda b, j: (b, j, 0, 0)),
        out_shape=jax.ShapeDtypeStruct((bsz, na, nb, B_WIDTH), BF16),
        compiler_params=_params(2),
        name="dft_stage2",
    )(bt, m2)
    return yt.transpose(0, 2, 1, 3).reshape(bsz, n, B_WIDTH)


FF_CHUNK = 1024


def _out_mlp_kernel(x_ref, p1_ref, p2_ref, g1_ref, sh_ref, sc_ref, g2_ref, ng_ref,
                    wout_ref, w1_ref, w2_ref, o_ref):
    half = p1_ref.shape[-1]
    mix = _dot(p1_ref[...], wout_ref[:half, :]) + _dot(p2_ref[...], wout_ref[half:, :])
    x1 = x_ref[...] + g1_ref[...] * mix
    h = _rms_mod(x1, ng_ref[...], sc_ref[...], sh_ref[...]).astype(BF16)
    acc = None
    for c in range(D_FF // FF_CHUNK):
        a = _dot(h, w1_ref[:, c * FF_CHUNK:(c + 1) * FF_CHUNK])
        a = jnp.square(jnp.maximum(a, 0.0)).astype(BF16)
        part = _dot(a, w2_ref[c * FF_CHUNK:(c + 1) * FF_CHUNK, :])
        acc = part if acc is None else acc + part
    o_ref[...] = x1 + g2_ref[...] * acc


def _out_mlp_call(x, p1, p2, mod, layer, row_of_batch, ng, w_out, w1, w2, tm):
    bsz, n, _ = x.shape
    half = p1.shape[-1]
    tok = lambda w: pl.BlockSpec((None, tm, w), lambda b, i: (b, i, 0))
    return pl.pallas_call(
        _out_mlp_kernel,
        grid=(bsz, n // tm),
        in_specs=[
            tok(D_MODEL), tok(half), tok(half),
            _mod_spec(layer, 2, row_of_batch),
            _mod_spec(layer, 3, row_of_batch),
            _mod_spec(layer, 4, row_of_batch),
            _mod_spec(layer, 5, row_of_batch),
            _const_spec((1, D_MODEL)),
            _const_spec((2 * half, D_MODEL)),
            _const_spec((D_MODEL, D_FF)),
            _const_spec((D_FF, D_MODEL)),
        ],
        out_specs=tok(D_MODEL),
        out_shape=jax.ShapeDtypeStruct((bsz, n, D_MODEL), F32),
        compiler_params=_params(2),
        name="out_mlp",
    )(x, p1, p2, mod, mod, mod, mod, ng, w_out, w1, w2)


def _odd_in_kernel(*refs, rope, emit_cache):
    x_ref, sh_ref, sc_ref, g_ref, win_ref, gain_ref, hm_ref = refs[:7]
    k = 7
    if rope:
        cos_ref, sin_ref = refs[k:k + 2]
        k += 2
    cq_ref, ck_ref, cvt_ref, dq_ref, dk_ref, dvt_ref = refs[k:k + 6]
    k += 6
    if emit_cache:
        nck_ref, ncv_ref, ndk_ref, ndv_ref = refs[k:k + 4]

    h = _rms_mod(x_ref[...], g_ref[...], sc_ref[...], sh_ref[...]).astype(BF16)
    qkv = _dot(h, win_ref[...])
    tm = qkv.shape[0]
    lane = lax.broadcasted_iota(jnp.int32, (tm, LANES), 1)
    first_half = (lane % (2 * ROPE_FREQS)) < ROPE_FREQS

    normed = []
    for c in range(ODD_NORMED_W // (2 * LANES)):
        v = qkv[:, c * 2 * LANES:(c + 1) * 2 * LANES]
        ms = _dot((v * v).astype(BF16), hm_ref[...])
        y = v * lax.rsqrt(ms + EPS) * gain_ref[:, c * 2 * LANES:(c + 1) * 2 * LANES]
        normed += [y[:, :LANES], y[:, LANES:]]

    def roped(y):
        if not rope:
            return y
        rot = jnp.where(first_half, -pltpu.roll(y, LANES - ROPE_FREQS, 1),
                        pltpu.roll(y, ROPE_FREQS, 1))
        return y * cos_ref[...] + rot * sin_ref[...]

    nq = C_W // LANES
    for c in range(nq):
        cq_ref[:, c * LANES:(c + 1) * LANES] = roped(normed[c]).astype(BF16)
    for c in range(nq):
        y = normed[nq + c]
        if emit_cache:
            nck_ref[:, c * LANES:(c + 1) * LANES] = y
        ck_ref[:, c * LANES:(c + 1) * LANES] = roped(y).astype(BF16)
    for c in range(DQ_W // LANES):
        dq_ref[:, c * LANES:(c + 1) * LANES] = roped(normed[2 * nq + c]).astype(BF16)
    dk_chunks = []
    for c in range(DKV_DUP_W // LANES):
        y = normed[2 * nq + DQ_W // LANES + c]
        dk_chunks.append(y)
        dk_ref[:, c * LANES:(c + 1) * LANES] = roped(y).astype(BF16)
    cv = qkv[:, ODD_NORMED_W:ODD_NORMED_W + C_W]
    dv = qkv[:, ODD_NORMED_W + C_W:]
    pad_row = lax.broadcasted_iota(jnp.int32, (V_ROWS - LANES, tm), 0)
    pad = jnp.where(pad_row == 0, 1.0, 0.0).astype(BF16)
    for v, vt_ref in ((cv, cvt_ref), (dv, dvt_ref)):
        for hh in range(vt_ref.shape[0]):
            vt_ref[hh, :LANES, :] = v[:, hh * LANES:(hh + 1) * LANES].T.astype(BF16)
            vt_ref[hh, LANES:, :] = pad
    if emit_cache:
        ncv_ref[...] = cv
        lo = lane < HEAD_DIM
        ndk_ref[...] = jnp.where(lo, dk_chunks[0], dk_chunks[1])
        ndv_ref[...] = jnp.where(lo, dv[:, :LANES], dv[:, LANES:])


def _odd_in_call(x, mod, layer, row_of_batch, g, w_in, gains, head_mean, rope_tabs, emit_cache, tm):
    bsz, n, _ = x.shape
    rope = rope_tabs is not None
    tok = lambda w: pl.BlockSpec((None, tm, w), lambda b, i: (b, i, 0))
    in_specs = [
        tok(D_MODEL),
        _mod_spec(layer, 0, row_of_batch),
        _mod_spec(layer, 1, row_of_batch),
        _const_spec((1, D_MODEL)),
        _const_spec((D_MODEL, ODD_W)),
        _const_spec((1, ODD_NORMED_W)),
        _const_spec((2 * LANES, 2 * LANES)),
    ]
    args = [x, mod, mod, g, w_in, gains, head_mean]
    if rope:
        in_specs += [pl.BlockSpec((tm, LANES), lambda b, i: (i, 0))] * 2
        args += list(rope_tabs)
    vt_spec = lambda heads: pl.BlockSpec((None, heads, V_ROWS, tm), lambda b, i: (b, 0, 0, i))
    vt_shape = lambda heads: jax.ShapeDtypeStruct((bsz, heads, V_ROWS, n), BF16)
    tok_shape = lambda w: jax.ShapeDtypeStruct((bsz, n, w), BF16)
    out_specs = [tok(C_W), tok(C_W), vt_spec(C_HEADS), tok(DQ_W), tok(DKV_DUP_W), vt_spec(D_KV_HEADS)]
    out_shape = [tok_shape(C_W), tok_shape(C_W), vt_shape(C_HEADS), tok_shape(DQ_W),
                 tok_shape(DKV_DUP_W), vt_shape(D_KV_HEADS)]
    if emit_cache:
        cw = [C_W, C_W, DKV_W, DKV_W]
        out_specs += [tok(w) for w in cw]
        out_shape += [jax.ShapeDtypeStruct((bsz, n, w), F32) for w in cw]
    return pl.pallas_call(
        functools.partial(_odd_in_kernel, rope=rope, emit_cache=emit_cache),
        grid=(bsz, n // tm),
        in_specs=in_specs,
        out_specs=out_specs,
        out_shape=out_shape,
        compiler_params=_params(2),
        name="odd_in",
    )(*args)


QK_AHEAD = 3


def _diff_attn_kernel(*refs, lam_init, tq, kc):
    n_src = (len(refs) - 4) // 2
    q_ref, (lamv_ref, g_ref, o_ref) = refs[0], refs[-3:]
    k_refs, vt_refs = refs[1:1 + n_src], refs[1 + n_src:1 + 2 * n_src]
    chunks = [(src, off) for src in range(n_src) for off in range(0, k_refs[src].shape[0], kc)]
    n_chunks = len(chunks)
    tiles = q_ref.shape[0] // tq
    lane = lax.broadcasted_iota(jnp.int32, (tq, LANES), 1)
    lv = lamv_ref[...]
    lam = (jnp.exp(jnp.sum(lv[0:1] * lv[1:2], keepdims=True))
           - jnp.exp(jnp.sum(lv[2:3] * lv[3:4], keepdims=True)) + lam_init)

    qqs = {}

    def score(item):
        t, c = item
        if t not in qqs:
            q = q_ref[t * tq:(t + 1) * tq, :]
            zero = jnp.zeros_like(q)
            qqs[t] = jnp.concatenate([jnp.where(lane < HEAD_DIM, q, zero),
                                      jnp.where(lane >= HEAD_DIM, q, zero)], axis=0)
        src, off = chunks[c]
        return _dot_nt(k_refs[src][off:off + kc, :], qqs[t])

    def finish(t, acc):
        inv = 1.0 / acc[LANES:LANES + 1, :]
        ot = acc[:LANES, :tq] * inv[:, :tq] - acc[:LANES, tq:] * (inv[:, tq:] * lam)
        o = ot.T
        ms = jnp.mean(o * o, axis=-1, keepdims=True)
        y = o * lax.rsqrt(ms + EPS) * g_ref[...] * (1.0 - lam_init)
        o_ref[t * tq:(t + 1) * tq, :] = y.astype(BF16)

    items = [(t, c) for t in range(tiles) for c in range(n_chunks)]
    ahead = [score(it) for it in items[:QK_AHEAD]]
    m = acc = None
    for idx, (t, c) in enumerate(items):
        s = ahead.pop(0)
        if idx + QK_AHEAD < len(items):
            ahead.append(score(items[idx + QK_AHEAD]))
        cm = jnp.max(s, axis=0, keepdims=True)
        m_new = cm if c == 0 else jnp.maximum(m, cm)
        e = jnp.exp2(s - m_new).astype(BF16)
        src, off = chunks[c]
        pv = _dot(vt_refs[src][:, off:off + kc], e)
        acc = pv if c == 0 else acc * jnp.exp2(m - m_new) + pv
        m = m_new
        if c == n_chunks - 1:
            finish(t, acc)


def _diff_attn_call(q, ks, vts, lamv, subln_g, lam_init, tq, tiles):
    bsz, n, _ = q.shape
    kc = next(c for c in (512, 256, 128) if all(k.shape[1] % c == 0 for k in ks))
    k_specs = [pl.BlockSpec((None, k.shape[1], LANES), lambda b, h, i: (b, 0, h)) for k in ks]
    vt_specs = [pl.BlockSpec((None, None, V_ROWS, k.shape[1]), lambda b, h, i: (b, h, 0, 0)) for k in ks]
    return pl.pallas_call(
        functools.partial(_diff_attn_kernel, lam_init=lam_init, tq=tq, kc=kc),
        grid=(bsz, C_HEADS, n // (tq * tiles)),
        in_specs=[pl.BlockSpec((None, tiles * tq, LANES), lambda b, h, i: (b, i, h))]
        + k_specs + vt_specs + [_const_spec((4, HEAD_DIM)), _const_spec((1, LANES))],
        out_specs=pl.BlockSpec((None, tiles * tq, LANES), lambda b, h, i: (b, i, h)),
        out_shape=jax.ShapeDtypeStruct((bsz, n, C_W), BF16),
        compiler_params=_params(3),
        name="diff_attn",
    )(q, *ks, *vts, lamv, subln_g)


def _sink_attn_kernel(*refs, windowed, tq, nblk):
    if windowed:
        (q_ref, ck_ref, kp_ref, km_ref, kn_ref, cvt_ref, vp_ref, vm_ref, vn_ref,
         bias_ref, sink_ref, o_ref) = refs
        kband = jnp.concatenate([kp_ref[...], km_ref[...], kn_ref[...]], axis=0)
        vtband = jnp.concatenate([vp_ref[...], vm_ref[...], vn_ref[...]], axis=1)
    else:
        q_ref, k_ref, vt_ref, sink_ref, o_ref = refs
    subs = q_ref.shape[0] // tq
    lane = lax.broadcasted_iota(jnp.int32, (tq, LANES), 1)
    lo = lane < HEAD_DIM
    sink = sink_ref[...]

    def scores_of(sub):
        qd = q_ref[sub * tq:(sub + 1) * tq, :]
        parts = []
        for c in range(2):
            ch = qd[:, c * LANES:(c + 1) * LANES]
            zero = jnp.zeros_like(ch)
            parts += [jnp.where(lo, ch, zero), jnp.where(lo, zero, ch)]
        qq = jnp.concatenate(parts, axis=0)
        if not windowed:
            return [_dot_nt(k_ref[...], qq)]
        blk = pl.program_id(2) * subs + sub
        variant = jnp.where(blk == 0, 0, jnp.where(blk == nblk - 1, 2, 1))
        band = _dot_nt(kband[sub * tq:(sub + 3) * tq, :], qq) + bias_ref[variant]
        return [_dot_nt(ck_ref[...], qq), band]

    def finish(sub, scores):
        values = [cvt_ref[...], vtband[:, sub * tq:(sub + 3) * tq]] if windowed else [vt_ref[...]]
        m = sink
        for s in scores:
            m = jnp.maximum(m, jnp.max(s, axis=0, keepdims=True))
        acc = None
        for s, vt in zip(scores, values):
            pv = _dot(vt, jnp.exp2(s - m).astype(BF16))
            acc = pv if acc is None else acc + pv
        inv = 1.0 / (acc[LANES:LANES + 1, :] + jnp.exp2(sink - m))
        ot = acc[:LANES, :] * inv
        for c in range(2):
            o0 = ot[:, (2 * c) * tq:(2 * c + 1) * tq].T
            o1 = ot[:, (2 * c + 1) * tq:(2 * c + 2) * tq].T
            o_ref[sub * tq:(sub + 1) * tq, c * LANES:(c + 1) * LANES] = jnp.where(lo, o0, o1).astype(BF16)

    ahead = [scores_of(sub) for sub in range(min(QK_AHEAD, subs))]
    for sub in range(subs):
        cur = ahead.pop(0)
        if sub + QK_AHEAD < subs:
            ahead.append(scores_of(sub + QK_AHEAD))
        finish(sub, cur)


def _window_bias(n, tq):
    nblk = n // tq
    kk = np.arange(3 * tq)[:, None]
    qi = np.arange(tq)[None, :]
    band_ok = np.abs(qi + WINDOW - kk) <= WINDOW
    out = []
    for blk in (0, 1, nblk - 1):
        jpos = blk * tq - WINDOW + kk
        ok = band_ok & (jpos >= 0) & (jpos < n)
        out.append(np.tile(np.where(ok, 0.0, NEG_INF), (1, D_GROUP)))
    return jnp.asarray(np.stack(out), F32)


def _sink_attn_call(q, k, vt, sink_rows, ctx_k=None, ctx_vt=None):
    bsz, n, _ = q.shape
    windowed = ctx_k is not None
    tq = WINDOW if windowed else n
    nblk = n // tq
    subs = next(s for s in (4, 2, 1) if nblk % s == 0) if windowed else 1
    steps = nblk // subs
    q_spec = pl.BlockSpec((None, subs * tq, 2 * LANES), lambda b, j, i: (b, i, j))
    sink_spec = pl.BlockSpec((None, 1, D_GROUP * tq), lambda b, j, i: (j, 0, 0))
    if windowed:
        assert nblk >= 3, "first / interior / last mask variants need three query blocks"
        n_ctx = ctx_k.shape[1]
        prev = lambda i: jnp.maximum(i * subs - 1, 0)
        nxt = lambda i: jnp.minimum((i + 1) * subs, nblk - 1)
        in_specs = [
            q_spec,
            pl.BlockSpec((None, n_ctx, LANES), lambda b, j, i: (b, 0, j)),
            pl.BlockSpec((None, tq, LANES), lambda b, j, i: (b, prev(i), j)),
            pl.BlockSpec((None, subs * tq, LANES), lambda b, j, i: (b, i, j)),
            pl.BlockSpec((None, tq, LANES), lambda b, j, i: (b, nxt(i), j)),
            pl.BlockSpec((None, None, V_ROWS, n_ctx), lambda b, j, i: (b, j, 0, 0)),
            pl.BlockSpec((None, None, V_ROWS, tq), lambda b, j, i: (b, j, 0, prev(i))),
            pl.BlockSpec((None, None, V_ROWS, subs * tq), lambda b, j, i: (b, j, 0, i)),
            pl.BlockSpec((None, None, V_ROWS, tq), lambda b, j, i: (b, j, 0, nxt(i))),
            _const_spec((3, 3 * tq, D_GROUP * tq)),
            sink_spec,
        ]
        args = (q, ctx_k, k, k, k, ctx_vt, vt, vt, vt, _window_bias(n, tq), sink_rows)
    else:
        in_specs = [
            q_spec,
            pl.BlockSpec((None, n, LANES), lambda b, j, i: (b, 0, j)),
            pl.BlockSpec((None, None, V_ROWS, n), lambda b, j, i: (b, j, 0, 0)),
            sink_spec,
        ]
        args = (q, k, vt, sink_rows)
    return pl.pallas_call(
        functools.partial(_sink_attn_kernel, windowed=windowed, tq=tq, nblk=nblk),
        grid=(bsz, D_KV_HEADS, steps),
        in_specs=in_specs,
        out_specs=pl.BlockSpec((None, subs * tq, 2 * LANES), lambda b, j, i: (b, i, j)),
        out_shape=jax.ShapeDtypeStruct((bsz, n, DQ_W), BF16),
        compiler_params=_params(3),
        name="sink_attn",
    )(*args)


def _dup_heads(a):
    lead = a.shape[:-1]
    a = a.reshape(lead + (D_KV_HEADS, 1, HEAD_DIM))
    return jnp.broadcast_to(a, lead + (D_KV_HEADS, 2, HEAD_DIM)).reshape(lead + (DKV_DUP_W,))


def _odd_weights(w_in_odd, c_qn_g, c_kn_g, d_qn_g, d_kn_g):
    cq, ck, cv, dq, dk, dv = jnp.split(
        w_in_odd, (C_W, 2 * C_W, 3 * C_W, 3 * C_W + DQ_W, 3 * C_W + DQ_W + DKV_W), axis=-1)
    w = jnp.concatenate([cq, ck, dq, _dup_heads(dk), cv, _dup_heads(dv)], axis=-1).astype(BF16)
    gains = jnp.concatenate([
        jnp.tile(c_qn_g * (ATTN_SCALE * LOG2E), C_W // HEAD_DIM),
        jnp.tile(c_kn_g, C_W // HEAD_DIM),
        jnp.tile(d_qn_g * (ATTN_SCALE * LOG2E), DQ_W // HEAD_DIM),
        jnp.tile(d_kn_g, DKV_DUP_W // HEAD_DIM),
    ])[None, :]
    return w, gains


def _value_rows(v):
    bsz, nk, w = v.shape
    vt = v.reshape(bsz, nk, w // LANES, LANES).transpose(0, 2, 3, 1)
    pad = jnp.zeros((bsz, w // LANES, V_ROWS - LANES, nk), v.dtype).at[:, :, 0, :].set(1)
    return jnp.concatenate([vt, pad], axis=2)


def _rope_tables(n):
    rows = n // GRID_W
    row = np.repeat(np.arange(rows), GRID_W).astype(np.float64)
    col = np.tile(np.arange(GRID_W), rows).astype(np.float64)
    freqs = ROPE_BASE ** (-np.arange(ROPE_FREQS, dtype=np.float64) / ROPE_FREQS)
    ang = np.stack([row[:, None] * freqs, col[:, None] * freqs], axis=1)
    ang = np.concatenate([ang, ang], axis=-1).reshape(n, HEAD_DIM)
    ang = np.concatenate([ang, ang], axis=-1)
    return jnp.asarray(np.cos(ang), F32), jnp.asarray(np.sin(ang), F32)


def _token_tile(n):
    return min(n, 512)


def _run_group(x, mod, row_of_batch, wts, caches):
    bsz, n, _ = x.shape
    decode = caches is not None
    tm = _token_tile(n)
    depth = wts["norm1_g"].shape[0]
    new_cache = []
    for l in range(depth):
        g1n = wts["norm1_g"][l][None, :]
        g2n = wts["norm2_g"][l][None, :]
        if l % 2 == 0:
            e = l // 2
            a, z = _even_in_call(x, mod, l, row_of_batch, g1n, wts["w_in_even"][e], wts["w_pool"][e],
                                 wts["pool_scale"][e][None, :], wts["dftc"], wts["w_fft"][e], tm)
            y = _dft_two_stage_call(z) if n > DFT_DIRECT_MAX else _dft_direct_call(z)
            p1, p2, w_out = a, y, wts["w_out_even"][e]
        else:
            o = l // 2
            lam_init = _lam_init(l)
            rope_tabs = _rope_tables(n) if decode else None
            outs = _odd_in_call(x, mod, l, row_of_batch, g1n, wts["w_in_odd"][o], wts["odd_gains"][o],
                                wts["head_mean"], rope_tabs, not decode, tm)
            cq, ck, cvt, dq, dkk, dvt = outs[:6]
            if decode:
                c_k, c_v, d_k, d_v = caches
                lc = c_k.shape[2]
                ks = [c_k[:, o].reshape(bsz, lc, C_W).astype(BF16), ck]
                vts = [_value_rows(c_v[:, o].reshape(bsz, lc, C_W).astype(BF16)), cvt]
                ctx_k = _dup_heads(d_k[:, o].reshape(bsz, lc, DKV_W)).astype(BF16)
                ctx_vt = _value_rows(_dup_heads(d_v[:, o].reshape(bsz, lc, DKV_W)).astype(BF16))
                tq = min(n, 256)
            else:
                new_cache.append(outs[6:])
                ks, vts = [ck], [cvt]
                ctx_k = ctx_vt = None
                tq = n
            tiles = 2 if n % (2 * tq) == 0 else 1
            c_out = _diff_attn_call(cq, ks, vts, wts["lamv"][o],
                                    wts["c_subln_g"][o][None, :], lam_init, tq, tiles)
            sink_tq = WINDOW if decode else n
            sink_rows = jnp.repeat(wts["d_sink"][o].reshape(D_KV_HEADS, D_GROUP) * LOG2E, sink_tq,
                                   axis=-1)[:, None, :]
            d_out = _sink_attn_call(dq, dkk, dvt, sink_rows, ctx_k, ctx_vt)
            p1, p2, w_out = c_out, d_out, wts["w_out_odd"][o]
        x = _out_mlp_call(x, p1, p2, mod, l, row_of_batch, g2n, w_out,
                          wts["w_mlp1"][l], wts["w_mlp2"][l], tm)
    return x, new_cache


def kernel(x_prompt, x_sample, c, cache_c_k, cache_c_v, cache_d_k, cache_d_v, c_ctx,
           norm1_g, norm2_g, w_ada, b_ada, w_in_even, w_pool, pool_scale, w_fft, w_out_even,
           w_in_odd, c_qn_g, c_kn_g, lam_q1, lam_k1, lam_q2, lam_k2, c_subln_g,
           d_qn_g, d_kn_g, d_sink, w_out_odd, w_mlp1, w_mlp2):
    depth = norm1_g.shape[0]
    n_odd = w_in_odd.shape[0]
    dec_b = c.shape[0]

    rows = -(-(1 + dec_b) // SUBLANES) * SUBLANES
    cond = jnp.zeros((rows, D_MODEL), F32).at[0].set(c_ctx).at[1:1 + dec_b].set(c)
    mod = _adaln_call(cond, w_ada, b_ada).reshape(depth, rows, 6, 1, D_MODEL)

    ic = np.arange(GROUP_W)
    ang = 2.0 * np.pi * np.outer(ic, ic) / GROUP_W
    dftc = _bf16_const(np.concatenate([np.cos(ang), -np.sin(ang)], axis=1))
    head_mean = _bf16_const(np.kron(np.eye(2 * LANES // HEAD_DIM), np.ones((HEAD_DIM, HEAD_DIM))) / HEAD_DIM)

    odd = [_odd_weights(w_in_odd[o], c_qn_g[o], c_kn_g[o], d_qn_g[o], d_kn_g[o]) for o in range(n_odd)]
    per_layer = lambda w: [w[i].astype(BF16) for i in range(w.shape[0])]
    wts = {
        "norm1_g": norm1_g, "norm2_g": norm2_g,
        "w_in_even": per_layer(w_in_even), "w_pool": per_layer(w_pool), "pool_scale": pool_scale,
        "w_fft": per_layer(w_fft), "w_out_even": per_layer(w_out_even), "dftc": dftc,
        "w_in_odd": [w for w, _ in odd], "odd_gains": [g for _, g in odd], "head_mean": head_mean,
        "lamv": jnp.stack([lam_q1, lam_k1, lam_q2, lam_k2], axis=1),
        "c_subln_g": c_subln_g, "d_sink": d_sink, "w_out_odd": per_layer(w_out_odd),
        "w_mlp1": per_layer(w_mlp1), "w_mlp2": per_layer(w_mlp2),
    }

    y_prompt, new_cache = _run_group(x_prompt, mod, lambda b: 0, wts, None)
    y_sample, _ = _run_group(x_sample, mod, lambda b: 1 + b, wts,
                             (cache_c_k, cache_c_v, cache_d_k, cache_d_v))

    bsz, n, _ = x_prompt.shape
    new_c_k = jnp.stack([nc[0].reshape(bsz, n, C_HEADS, 2, HEAD_DIM) for nc in new_cache], axis=1)
    new_c_v = jnp.stack([nc[1].reshape(bsz, n, C_HEADS, 2 * HEAD_DIM) for nc in new_cache], axis=1)
    new_d_k = jnp.stack([nc[2].reshape(bsz, n, D_KV_HEADS, HEAD_DIM) for nc in new_cache], axis=1)
    new_d_v = jnp.stack([nc[3].reshape(bsz, n, D_KV_HEADS, HEAD_DIM) for nc in new_cache], axis=1)
    return (y_prompt, y_sample, new_c_k, new_c_v, new_d_k, new_d_v)
```

```python
import functools
import math

import numpy as np
import jax
import jax.numpy as jnp
from jax import lax
from jax.experimental import pallas as pl
from jax.experimental.pallas import tpu as pltpu

F32 = jnp.float32
BF16 = jnp.bfloat16

D_MODEL = 1024
HEAD_DIM = 64
LANES = 128
SUBLANES = 8
GRID_W = 64
ROPE_FREQS = HEAD_DIM // 4
ROPE_BASE = 10000.0
EPS = 1e-6
NEG_INF = -1e30
WINDOW = 128
A_WIDTH = D_MODEL // 2
B_WIDTH = D_MODEL // 2
POOL_WINDOWS = (2, 4, 8, 16)
POOL_HALO = 8
N_GROUPS = 4
GROUP_W = A_WIDTH // N_GROUPS
C_HEADS = 4
D_HEADS = 8
D_KV_HEADS = 2
D_GROUP = D_HEADS // D_KV_HEADS
C_W = C_HEADS * 2 * HEAD_DIM
DQ_W = D_HEADS * HEAD_DIM
DKV_W = D_KV_HEADS * HEAD_DIM
DKV_DUP_W = 2 * DKV_W
D_FF = 4 * D_MODEL
ATTN_SCALE = HEAD_DIM ** -0.5
LOG2E = math.log2(math.e)
V_ROWS = LANES + 16
ODD_NORMED_W = C_W + C_W + DQ_W + DKV_DUP_W
ODD_W = ODD_NORMED_W + C_W + DKV_DUP_W
VMEM_LIMIT = 56 * 1024 * 1024


def _lam_init(layer):
    return 0.8 - 0.6 * math.exp(-0.3 * layer)


def _params(n_axes):
    return pltpu.CompilerParams(dimension_semantics=("arbitrary",) * n_axes,
                                vmem_limit_bytes=VMEM_LIMIT)


def _const_spec(shape):
    nd = len(shape)
    return pl.BlockSpec(shape, lambda *_: (0,) * nd, pipeline_mode=pl.Buffered(1))


def _stacked_spec(shape, idx):
    nd = len(shape)
    return pl.BlockSpec((None,) + tuple(shape), lambda *_: (idx,) + (0,) * nd, pipeline_mode=pl.Buffered(1))


def _mod_spec(layer, which, row_of_batch):
    return pl.BlockSpec((None, None, None, 1, D_MODEL),
                        lambda b, i: (layer, row_of_batch(b), which, 0, 0))


def _rms_mod(x, g, sc, sh):
    ms = jnp.mean(x * x, axis=-1, keepdims=True)
    return (x * lax.rsqrt(ms + EPS) * g) * (1.0 + sc) + sh


def _bf16_const(a):
    return jnp.asarray(a, F32).astype(BF16)


def _dot_nt(a, b):
    return lax.dot_general(a, b, (((1,), (1,)), ((), ())), preferred_element_type=F32)


def _dot(a, b):
    return jnp.dot(a, b, preferred_element_type=F32)


def _adaln_kernel(cond_ref, w_ref, b_ref, o_ref):
    cnd = cond_ref[...]
    s = (cnd * jax.nn.sigmoid(cnd)).astype(BF16)
    o_ref[...] = _dot(s, w_ref[...].astype(BF16)) + b_ref[...]


def _adaln_call(cond, w_ada, b_ada):
    depth = w_ada.shape[0]
    rows = cond.shape[0]
    tn = 1536
    return pl.pallas_call(
        _adaln_kernel,
        grid=(depth, 6 * D_MODEL // tn),
        in_specs=[
            pl.BlockSpec((rows, D_MODEL), lambda l, j: (0, 0)),
            pl.BlockSpec((None, D_MODEL, tn), lambda l, j: (l, 0, j)),
            pl.BlockSpec((None, 1, tn), lambda l, j: (l, 0, j)),
        ],
        out_specs=pl.BlockSpec((None, rows, tn), lambda l, j: (l, 0, j)),
        out_shape=jax.ShapeDtypeStruct((depth, rows, 6 * D_MODEL), F32),
        compiler_params=_params(2),
        name="adaln",
    )(cond, w_ada, b_ada.reshape(depth, 1, 6 * D_MODEL))


def _even_in_kernel(x_ref, xp_ref, xn_ref, sh_ref, sc_ref, g_ref, win_ref, wpool_ref,
                    pscale_ref, dftc_ref, wfft_ref, a_ref, z_ref, ext_ref, *, tm, n):
    i = pl.program_id(1)
    n_tiles = n // tm
    g, sc, sh = g_ref[...], sc_ref[...], sh_ref[...]
    h = _rms_mod(x_ref[...], g, sc, sh).astype(BF16)
    u = _dot(h, win_ref[...])
    xh = jnp.concatenate([xp_ref[...], xn_ref[...]], axis=0)
    hh = _rms_mod(xh, g, sc, sh).astype(BF16)
    uh = _dot(hh, win_ref[:, :A_WIDTH])
    has_prev = (i > 0).astype(F32)
    has_next = (i < n_tiles - 1).astype(F32)
    ext_ref[0:POOL_HALO, :] = uh[0:POOL_HALO] * has_prev
    ext_ref[POOL_HALO:POOL_HALO + tm, :] = u[:, :A_WIDTH]
    ext_ref[POOL_HALO + tm:, :] = uh[POOL_HALO:] * has_next

    t = i * tm + lax.broadcasted_iota(jnp.int32, (tm, 1), 0)
    for gi, w in enumerate(POOL_WINDOWS):
        cols = slice(gi * GROUP_W, (gi + 1) * GROUP_W)
        acc = None
        for j in range(-(w // 2), w // 2):
            v = ext_ref[pl.ds(POOL_HALO + j, tm), cols]
            acc = v if acc is None else acc + v
        lo = jnp.maximum(t - w // 2, 0)
        hi = jnp.minimum(t + w // 2, n)
        cnt = (hi - lo).astype(F32)
        pooled = acc / cnt - u[:, cols]
        y = _dot(pooled.astype(BF16), wpool_ref[gi]) * pscale_ref[:, cols]
        a_ref[:, cols] = y.astype(BF16)

    for gi in range(N_GROUPS):
        ub = u[:, A_WIDTH + gi * GROUP_W:A_WIDTH + (gi + 1) * GROUP_W].astype(BF16)
        cs = _dot(ub, dftc_ref[...])
        zr = _dot(cs[:, :GROUP_W].astype(BF16), wfft_ref[gi])
        zi = _dot(cs[:, GROUP_W:].astype(BF16), wfft_ref[gi])
        z_ref[:, gi * GROUP_W:(gi + 1) * GROUP_W] = zr.astype(BF16)
        z_ref[:, B_WIDTH + gi * GROUP_W:B_WIDTH + (gi + 1) * GROUP_W] = zi.astype(BF16)


def _even_in_call(x, mod, layer, row_of_batch, g, e, w_in, w_pool, pool_scale, dftc, w_fft, tm):
    bsz, n, _ = x.shape
    hb = tm // POOL_HALO
    kern = functools.partial(_even_in_kernel, tm=tm, n=n)
    return pl.pallas_call(
        kern,
        grid=(bsz, n // tm),
        in_specs=[
            pl.BlockSpec((None, tm, D_MODEL), lambda b, i: (b, i, 0)),
            pl.BlockSpec((None, POOL_HALO, D_MODEL),
                         lambda b, i: (b, jnp.maximum(i * hb - 1, 0), 0)),
            pl.BlockSpec((None, POOL_HALO, D_MODEL),
                         lambda b, i: (b, jnp.minimum((i + 1) * hb, n // POOL_HALO - 1), 0)),
            _mod_spec(layer, 0, row_of_batch),
            _mod_spec(layer, 1, row_of_batch),
            _const_spec((1, D_MODEL)),
            _stacked_spec((D_MODEL, D_MODEL), e),
            _stacked_spec((N_GROUPS, GROUP_W, GROUP_W), e),
            _const_spec((1, A_WIDTH)),
            _const_spec((GROUP_W, 2 * GROUP_W)),
            _stacked_spec((N_GROUPS, GROUP_W, GROUP_W), e),
        ],
        out_specs=[
            pl.BlockSpec((None, tm, A_WIDTH), lambda b, i: (b, i, 0)),
            pl.BlockSpec((None, tm, 2 * B_WIDTH), lambda b, i: (b, i, 0)),
        ],
        out_shape=[
            jax.ShapeDtypeStruct((bsz, n, A_WIDTH), BF16),
            jax.ShapeDtypeStruct((bsz, n, 2 * B_WIDTH), BF16),
        ],
        scratch_shapes=[pltpu.VMEM((tm + 2 * POOL_HALO, A_WIDTH), F32)],
        compiler_params=_params(2),
        name="even_in",
    )(x, x, x, mod, mod, g, w_in, w_pool, pool_scale, dftc, w_fft)


def _dft_direct_kernel(z_ref, m_ref, y_ref, *, scale):
    z = z_ref[...]
    rhs = jnp.concatenate([z[:, :B_WIDTH], z[:, B_WIDTH:]], axis=0)
    y_ref[...] = (_dot(m_ref[...], rhs) * scale).astype(BF16)


def _dft_direct_call(z):
    bsz, n, _ = z.shape
    k = np.arange(n)
    ang = 2.0 * np.pi * np.outer(k, k) / n
    m = _bf16_const(np.concatenate([np.cos(ang), np.sin(ang)], axis=1))
    kern = functools.partial(_dft_direct_kernel, scale=float((n * GROUP_W) ** -0.5))
    return pl.pallas_call(
        kern,
        grid=(bsz,),
        in_specs=[pl.BlockSpec((None, n, 2 * B_WIDTH), lambda b: (b, 0, 0)),
                  _const_spec((n, 2 * n))],
        out_specs=pl.BlockSpec((None, n, B_WIDTH), lambda b: (b, 0, 0)),
        out_shape=jax.ShapeDtypeStruct((bsz, n, B_WIDTH), BF16),
        compiler_params=_params(1),
        name="dft_direct",
    )(z, m)


DFT_ROWS = 8
DFT_DIRECT_MAX = 256


def _dft_stage1_kernel(z_ref, m_ref, twr_ref, twi_ref, o_ref, *, na):
    for bi in range(DFT_ROWS):
        zb = z_ref[bi]
        rhs = jnp.concatenate([zb[:, :B_WIDTH], zb[:, B_WIDTH:]], axis=0)
        r = _dot(m_ref[...], rhs)
        br, bim = r[:na], r[na:]
        tr, ti = twr_ref[bi], twi_ref[bi]
        o_ref[bi, :, :B_WIDTH] = (br * tr - bim * ti).astype(BF16)
        o_ref[bi, :, B_WIDTH:] = (br * ti + bim * tr).astype(BF16)


def _dft_stage2_kernel(b_ref, m_ref, y_ref, *, scale):
    for di in range(DFT_ROWS):
        bb = b_ref[di]
        rhs = jnp.concatenate([bb[:, :B_WIDTH], bb[:, B_WIDTH:]], axis=0)
        y_ref[di] = (_dot(m_ref[...], rhs) * scale).astype(BF16)


def _dft_two_stage_call(z):
    bsz, n, _ = z.shape
    na = 1 << (int(math.log2(n)) // 2)
    nb = n // na
    ia = np.arange(na)
    ib = np.arange(nb)
    ang_a = 2.0 * np.pi * np.outer(ia, ia) / na
    fr, fi = np.cos(ang_a), -np.sin(ang_a)
    m1 = _bf16_const(np.block([[fr, -fi], [fi, fr]]))
    ang_t = 2.0 * np.pi * np.outer(ib, ia) / n
    twr = jnp.asarray(np.cos(ang_t)[:, :, None], F32)
    twi = jnp.asarray(-np.sin(ang_t)[:, :, None], F32)
    ang_b = 2.0 * np.pi * np.outer(ib, ib) / nb
    m2 = _bf16_const(np.concatenate([np.cos(ang_b), np.sin(ang_b)], axis=1))

    zt = z.reshape(bsz, na, nb, 2 * B_WIDTH).transpose(0, 2, 1, 3)
    s1 = pl.pallas_call(
        functools.partial(_dft_stage1_kernel, na=na),
        grid=(bsz, nb // DFT_ROWS),
        in_specs=[
            pl.BlockSpec((None, DFT_ROWS, na, 2 * B_WIDTH), lambda b, j: (b, j, 0, 0)),
            _const_spec((2 * na, 2 * na)),
            pl.BlockSpec((DFT_ROWS, na, 1), lambda b, j: (j, 0, 0)),
            pl.BlockSpec((DFT_ROWS, na, 1), lambda b, j: (j, 0, 0)),
        ],
        out_specs=pl.BlockSpec((None, DFT_ROWS, na, 2 * B_WIDTH), lambda b, j: (b, j, 0, 0)),
        out_shape=jax.ShapeDtypeStruct((bsz, nb, na, 2 * B_WIDTH), BF16),
        compiler_params=_params(2),
        name="dft_stage1",
    )(zt, m1, twr, twi)
    bt = s1.transpose(0, 2, 1, 3)
    yt = pl.pallas_call(
        functools.partial(_dft_stage2_kernel, scale=float((n * GROUP_W) ** -0.5)),
        grid=(bsz, na // DFT_ROWS),
        in_specs=[
            pl.BlockSpec((None, DFT_ROWS, nb, 2 * B_WIDTH), lambda b, j: (b, j, 0, 0)),
            _const_spec((nb, 2 * nb)),
        ],
        out_specs=pl.BlockSpec((None, DFT_ROWS, nb, B_WIDTH), lambda b, j: (b, j, 0, 0)),
        out_shape=jax.ShapeDtypeStruct((bsz, na, nb, B_WIDTH), BF16),
        compiler_params=_params(2),
        name="dft_stage2",
    )(bt, m2)
    return yt.transpose(0, 2, 1, 3).reshape(bsz, n, B_WIDTH)


FF_CHUNK = 1024


def _out_mlp_kernel(x_ref, p1_ref, p2_ref, g1_ref, sh_ref, sc_ref, g2_ref, ng_ref,
                    wout_ref, w1_ref, w2_ref, o_ref):
    half = p1_ref.shape[-1]
    mix = _dot(p1_ref[...], wout_ref[:half, :]) + _dot(p2_ref[...], wout_ref[half:, :])
    x1 = x_ref[...] + g1_ref[...] * mix
    h = _rms_mod(x1, ng_ref[...], sc_ref[...], sh_ref[...]).astype(BF16)
    acc = None
    for c in range(D_FF // FF_CHUNK):
        a = _dot(h, w1_ref[:, c * FF_CHUNK:(c + 1) * FF_CHUNK])
        a = jnp.square(jnp.maximum(a, 0.0)).astype(BF16)
        part = _dot(a, w2_ref[c * FF_CHUNK:(c + 1) * FF_CHUNK, :])
        acc = part if acc is None else acc + part
    o_ref[...] = x1 + g2_ref[...] * acc


def _out_mlp_call(x, p1, p2, mod, layer, row_of_batch, ng, w_out, out_idx, w1, w2, tm):
    bsz, n, _ = x.shape
    half = p1.shape[-1]
    tok = lambda w: pl.BlockSpec((None, tm, w), lambda b, i: (b, i, 0))
    return pl.pallas_call(
        _out_mlp_kernel,
        grid=(bsz, n // tm),
        in_specs=[
            tok(D_MODEL), tok(half), tok(half),
            _mod_spec(layer, 2, row_of_batch),
            _mod_spec(layer, 3, row_of_batch),
            _mod_spec(layer, 4, row_of_batch),
            _mod_spec(layer, 5, row_of_batch),
            _const_spec((1, D_MODEL)),
            _stacked_spec((2 * half, D_MODEL), out_idx),
            _stacked_spec((D_MODEL, D_FF), layer),
            _stacked_spec((D_FF, D_MODEL), layer),
        ],
        out_specs=tok(D_MODEL),
        out_shape=jax.ShapeDtypeStruct((bsz, n, D_MODEL), F32),
        compiler_params=_params(2),
        name="out_mlp",
    )(x, p1, p2, mod, mod, mod, mod, ng, w_out, w1, w2)


def _odd_in_kernel(*refs, rope, emit_cache, tm):
    x_ref, sh_ref, sc_ref, g_ref, win_ref, gain_ref, hm_ref = refs[:7]
    k = 7
    if rope:
        cos_ref, sin_ref = refs[k:k + 2]
        k += 2
    cq_ref, ck_ref, cvt_ref, dq_ref, dk_ref, dvt_ref = refs[k:k + 6]
    k += 6
    if emit_cache:
        nck_ref, ncv_ref, ndk_ref, ndv_ref = refs[k:k + 4]
    subs = x_ref.shape[0] // tm
    lane = lax.broadcasted_iota(jnp.int32, (tm, LANES), 1)
    first_half = (lane % (2 * ROPE_FREQS)) < ROPE_FREQS
    lo = lane < HEAD_DIM
    pad_row = lax.broadcasted_iota(jnp.int32, (V_ROWS - LANES, tm), 0)
    pad = jnp.where(pad_row == 0, 1.0, 0.0).astype(BF16)

    def project(sub):
        x = x_ref[sub * tm:(sub + 1) * tm, :]
        h = _rms_mod(x, g_ref[...], sc_ref[...], sh_ref[...]).astype(BF16)
        return _dot(h, win_ref[...])

    def head_norms(qkv):
        normed = []
        for c in range(ODD_NORMED_W // (2 * LANES)):
            v = qkv[:, c * 2 * LANES:(c + 1) * 2 * LANES]
            ms = _dot((v * v).astype(BF16), hm_ref[...])
            y = v * lax.rsqrt(ms + EPS) * gain_ref[:, c * 2 * LANES:(c + 1) * 2 * LANES]
            normed += [y[:, :LANES], y[:, LANES:]]
        return normed

    def finish(sub, qkv, normed):
        rows = slice(sub * tm, (sub + 1) * tm)

        def roped(y):
            if not rope:
                return y
            rot = jnp.where(first_half, -pltpu.roll(y, LANES - ROPE_FREQS, 1),
                            pltpu.roll(y, ROPE_FREQS, 1))
            return y * cos_ref[rows, :] + rot * sin_ref[rows, :]

        nq = C_W // LANES
        for c in range(nq):
            cq_ref[rows, c * LANES:(c + 1) * LANES] = roped(normed[c]).astype(BF16)
        for c in range(nq):
            y = normed[nq + c]
            if emit_cache:
                nck_ref[rows, c * LANES:(c + 1) * LANES] = y
            ck_ref[rows, c * LANES:(c + 1) * LANES] = roped(y).astype(BF16)
        for c in range(DQ_W // LANES):
            dq_ref[rows, c * LANES:(c + 1) * LANES] = roped(normed[2 * nq + c]).astype(BF16)
        dk_chunks = []
        for c in range(DKV_DUP_W // LANES):
            y = normed[2 * nq + DQ_W // LANES + c]
            dk_chunks.append(y)
            dk_ref[rows, c * LANES:(c + 1) * LANES] = roped(y).astype(BF16)
        cv = qkv[:, ODD_NORMED_W:ODD_NORMED_W + C_W]
        dv = qkv[:, ODD_NORMED_W + C_W:]
        for v, vt_ref in ((cv, cvt_ref), (dv, dvt_ref)):
            for hh in range(vt_ref.shape[0]):
                vt_ref[hh, :LANES, rows] = v[:, hh * LANES:(hh + 1) * LANES].T.astype(BF16)
                vt_ref[hh, LANES:, rows] = pad
        if emit_cache:
            ncv_ref[rows, :] = cv
            ndk_ref[rows, :] = jnp.where(lo, dk_chunks[0], dk_chunks[1])
            ndv_ref[rows, :] = jnp.where(lo, dv[:, :LANES], dv[:, LANES:])

    qkv = project(0)
    for sub in range(subs):
        normed = head_norms(qkv)
        nxt = project(sub + 1) if sub + 1 < subs else None
        finish(sub, qkv, normed)
        qkv = nxt


def _odd_in_call(x, mod, layer, row_of_batch, g, w_in, gains, head_mean, rope_tabs, emit_cache, tm):
    bsz, n, _ = x.shape
    rope = rope_tabs is not None
    subs = 2 if n % (2 * tm) == 0 else 1
    ts = subs * tm
    tok = lambda w: pl.BlockSpec((None, ts, w), lambda b, i: (b, i, 0))
    in_specs = [
        tok(D_MODEL),
        _mod_spec(layer, 0, row_of_batch),
        _mod_spec(layer, 1, row_of_batch),
        _const_spec((1, D_MODEL)),
        _const_spec((D_MODEL, ODD_W)),
        _const_spec((1, ODD_NORMED_W)),
        _const_spec((2 * LANES, 2 * LANES)),
    ]
    args = [x, mod, mod, g, w_in, gains, head_mean]
    if rope:
        in_specs += [pl.BlockSpec((ts, LANES), lambda b, i: (i, 0))] * 2
        args += list(rope_tabs)
    vt_spec = lambda heads: pl.BlockSpec((None, heads, V_ROWS, ts), lambda b, i: (b, 0, 0, i))
    vt_shape = lambda heads: jax.ShapeDtypeStruct((bsz, heads, V_ROWS, n), BF16)
    tok_shape = lambda w: jax.ShapeDtypeStruct((bsz, n, w), BF16)
    out_specs = [tok(C_W), tok(C_W), vt_spec(C_HEADS), tok(DQ_W), tok(DKV_DUP_W), vt_spec(D_KV_HEADS)]
    out_shape = [tok_shape(C_W), tok_shape(C_W), vt_shape(C_HEADS), tok_shape(DQ_W),
                 tok_shape(DKV_DUP_W), vt_shape(D_KV_HEADS)]
    if emit_cache:
        cw = [C_W, C_W, DKV_W, DKV_W]
        out_specs += [tok(w) for w in cw]
        out_shape += [jax.ShapeDtypeStruct((bsz, n, w), F32) for w in cw]
    return pl.pallas_call(
        functools.partial(_odd_in_kernel, rope=rope, emit_cache=emit_cache, tm=tm),
        grid=(bsz, n // ts),
        in_specs=in_specs,
        out_specs=out_specs,
        out_shape=out_shape,
        compiler_params=_params(2),
        name="odd_in",
    )(*args)


QK_AHEAD = 3


def _diff_attn_kernel(*refs, lam_init, tq, kc):
    n_src = (len(refs) - 4) // 2
    q_ref, (lamv_ref, g_ref, o_ref) = refs[0], refs[-3:]
    k_refs, vt_refs = refs[1:1 + n_src], refs[1 + n_src:1 + 2 * n_src]
    chunks = [(src, off) for src in range(n_src) for off in range(0, k_refs[src].shape[0], kc)]
    n_chunks = len(chunks)
    tiles = q_ref.shape[0] // tq
    heads = q_ref.shape[1] // LANES
    lane = lax.broadcasted_iota(jnp.int32, (tq, LANES), 1)
    lv = lamv_ref[...]
    lam = (jnp.exp(jnp.sum(lv[0:1] * lv[1:2], keepdims=True))
           - jnp.exp(jnp.sum(lv[2:3] * lv[3:4], keepdims=True)) + lam_init)

    qqs = {}

    def score(item):
        h, t, c = item
        cols = slice(h * LANES, (h + 1) * LANES)
        if (h, t) not in qqs:
            q = q_ref[t * tq:(t + 1) * tq, cols]
            zero = jnp.zeros_like(q)
            qqs[h, t] = jnp.concatenate([jnp.where(lane < HEAD_DIM, q, zero),
                                         jnp.where(lane >= HEAD_DIM, q, zero)], axis=0)
        src, off = chunks[c]
        return _dot_nt(k_refs[src][off:off + kc, cols], qqs[h, t])

    def finish(h, t, acc):
        inv = 1.0 / acc[LANES:LANES + 1, :]
        ot = acc[:LANES, :tq] * inv[:, :tq] - acc[:LANES, tq:] * (inv[:, tq:] * lam)
        o = ot.T
        ms = jnp.mean(o * o, axis=-1, keepdims=True)
        y = o * lax.rsqrt(ms + EPS) * g_ref[...] * (1.0 - lam_init)
        o_ref[t * tq:(t + 1) * tq, h * LANES:(h + 1) * LANES] = y.astype(BF16)

    items = [(h, t, c) for h in range(heads) for t in range(tiles) for c in range(n_chunks)]
    ahead = [score(it) for it in items[:QK_AHEAD]]
    m = acc = None
    for idx, (h, t, c) in enumerate(items):
        s = ahead.pop(0)
        if idx + QK_AHEAD < len(items):
            ahead.append(score(items[idx + QK_AHEAD]))
        cm = jnp.max(s, axis=0, keepdims=True)
        m_new = cm if c == 0 else jnp.maximum(m, cm)
        src, off = chunks[c]
        e = jnp.exp2(s - m_new).astype(BF16)
        pv = _dot(vt_refs[src][h, :, off:off + kc], e)
        acc = pv if c == 0 else acc * jnp.exp2(m - m_new) + pv
        m = m_new
        if c == n_chunks - 1:
            finish(h, t, acc)


def _diff_attn_call(q, ks, vts, lamv, subln_g, lam_init, tq, tiles, heads, flat_values=False):
    bsz, n, _ = q.shape
    kc = next(c for c in (512, 256, 128) if all(k.shape[1] % c == 0 for k in ks))
    hw = heads * LANES
    k_specs = [pl.BlockSpec((None, k.shape[1], hw), lambda b, h, i: (b, 0, h)) for k in ks]
    vt_index = (lambda b, h, i: (0, h, 0, b)) if flat_values else (lambda b, h, i: (b, h, 0, 0))
    vt_specs = [pl.BlockSpec((None, heads, V_ROWS, k.shape[1]), vt_index) for k in ks]
    return pl.pallas_call(
        functools.partial(_diff_attn_kernel, lam_init=lam_init, tq=tq, kc=kc),
        grid=(bsz, C_HEADS // heads, n // (tq * tiles)),
        in_specs=[pl.BlockSpec((None, tiles * tq, hw), lambda b, h, i: (b, i, h))]
        + k_specs + vt_specs + [_const_spec((4, HEAD_DIM)), _const_spec((1, LANES))],
        out_specs=pl.BlockSpec((None, tiles * tq, hw), lambda b, h, i: (b, i, h)),
        out_shape=jax.ShapeDtypeStruct((bsz, n, C_W), BF16),
        compiler_params=_params(3),
        name="diff_attn",
    )(q, *ks, *vts, lamv, subln_g)


def _sink_attn_kernel(*refs, windowed, tq, nblk):
    if windowed:
        (q_ref, ck_ref, kp_ref, km_ref, kn_ref, cvt_ref, vp_ref, vm_ref, vn_ref,
         bias_ref, sink_ref, o_ref) = refs
        kband = jnp.concatenate([kp_ref[...], km_ref[...], kn_ref[...]], axis=0)
        vtband = jnp.concatenate([vp_ref[...], vm_ref[...], vn_ref[...]], axis=1)
    else:
        q_ref, k_ref, vt_ref, sink_ref, o_ref = refs
    subs = q_ref.shape[0] // tq
    groups = q_ref.shape[1] // (2 * LANES)
    lane = lax.broadcasted_iota(jnp.int32, (tq, LANES), 1)
    lo = lane < HEAD_DIM

    def scores_of(stream):
        g, sub = stream
        qd = q_ref[sub * tq:(sub + 1) * tq, g * 2 * LANES:(g + 1) * 2 * LANES]
        parts = []
        for c in range(2):
            ch = qd[:, c * LANES:(c + 1) * LANES]
            zero = jnp.zeros_like(ch)
            parts += [jnp.where(lo, ch, zero), jnp.where(lo, zero, ch)]
        qq = jnp.concatenate(parts, axis=0)
        if not windowed:
            return [_dot_nt(k_ref[:, g * LANES:(g + 1) * LANES], qq)]
        blk = pl.program_id(2) * subs + sub
        variant = jnp.where(blk == 0, 0, jnp.where(blk == nblk - 1, 2, 1))
        band = _dot_nt(kband[sub * tq:(sub + 3) * tq, :], qq) + bias_ref[variant]
        return [_dot_nt(ck_ref[...], qq), band]

    def finish(stream, scores):
        g, sub = stream
        values = [cvt_ref[...], vtband[:, sub * tq:(sub + 3) * tq]] if windowed else [vt_ref[g]]
        sink = sink_ref[g]
        m = sink
        for s in scores:
            m = jnp.maximum(m, jnp.max(s, axis=0, keepdims=True))
        acc = None
        for s, vt in zip(scores, values):
            pv = _dot(vt, jnp.exp2(s - m).astype(BF16))
            acc = pv if acc is None else acc + pv
        inv = 1.0 / (acc[LANES:LANES + 1, :] + jnp.exp2(sink - m))
        ot = acc[:LANES, :] * inv
        for c in range(2):
            o0 = ot[:, (2 * c) * tq:(2 * c + 1) * tq].T
            o1 = ot[:, (2 * c + 1) * tq:(2 * c + 2) * tq].T
            cols = slice((2 * g + c) * LANES, (2 * g + c + 1) * LANES)
            o_ref[sub * tq:(sub + 1) * tq, cols] = jnp.where(lo, o0, o1).astype(BF16)

    streams = [(g, sub) for g in range(groups) for sub in range(subs)]
    ahead = [scores_of(st) for st in streams[:QK_AHEAD]]
    for idx, st in enumerate(streams):
        cur = ahead.pop(0)
        if idx + QK_AHEAD < len(streams):
            ahead.append(scores_of(streams[idx + QK_AHEAD]))
        finish(st, cur)


def _window_bias(n, tq):
    nblk = n // tq
    kk = np.arange(3 * tq)[:, None]
    qi = np.arange(tq)[None, :]
    band_ok = np.abs(qi + WINDOW - kk) <= WINDOW
    out = []
    for blk in (0, 1, nblk - 1):
        jpos = blk * tq - WINDOW + kk
        ok = band_ok & (jpos >= 0) & (jpos < n)
        out.append(np.tile(np.where(ok, 0.0, NEG_INF), (1, D_GROUP)))
    return jnp.asarray(np.stack(out), F32)


def _sink_attn_call(q, k, vt, sink_rows, ctx_k=None, ctx_vt=None, flat_values=False):
    bsz, n, _ = q.shape
    windowed = ctx_k is not None
    tq = WINDOW if windowed else n
    nblk = n // tq
    subs = next(s for s in (4, 2, 1) if nblk % s == 0) if windowed else 1
    steps = nblk // subs
    groups = 1 if windowed else D_KV_HEADS
    q_spec = pl.BlockSpec((None, subs * tq, groups * 2 * LANES), lambda b, j, i: (b, i, j))
    sink_spec = pl.BlockSpec((groups, 1, D_GROUP * tq), lambda b, j, i: (j, 0, 0))
    if windowed:
        assert nblk >= 3, "first / interior / last mask variants need three query blocks"
        n_ctx = ctx_k.shape[1]
        prev = lambda i: jnp.maximum(i * subs - 1, 0)
        nxt = lambda i: jnp.minimum((i + 1) * subs, nblk - 1)
        in_specs = [
            q_spec,
            pl.BlockSpec((None, n_ctx, LANES), lambda b, j, i: (b, 0, j)),
            pl.BlockSpec((None, tq, LANES), lambda b, j, i: (b, prev(i), j)),
            pl.BlockSpec((None, subs * tq, LANES), lambda b, j, i: (b, i, j)),
            pl.BlockSpec((None, tq, LANES), lambda b, j, i: (b, nxt(i), j)),
            pl.BlockSpec((None, None, V_ROWS, n_ctx), lambda b, j, i: (b, j, 0, 0)),
            pl.BlockSpec((None, None, V_ROWS, tq), lambda b, j, i: (b, j, 0, prev(i))),
            pl.BlockSpec((None, None, V_ROWS, subs * tq), lambda b, j, i: (b, j, 0, i)),
            pl.BlockSpec((None, None, V_ROWS, tq), lambda b, j, i: (b, j, 0, nxt(i))),
            _const_spec((3, 3 * tq, D_GROUP * tq)),
            sink_spec,
        ]
        args = (q, ctx_k, k, k, k, ctx_vt, vt, vt, vt, _window_bias(n, tq), sink_rows)
    else:
        in_specs = [
            q_spec,
            pl.BlockSpec((None, n, groups * LANES), lambda b, j, i: (b, 0, j)),
            pl.BlockSpec((None, groups, V_ROWS, n),
                         (lambda b, j, i: (0, j, 0, b)) if flat_values else (lambda b, j, i: (b, j, 0, 0))),
            sink_spec,
        ]
        args = (q, k, vt, sink_rows)
    return pl.pallas_call(
        functools.partial(_sink_attn_kernel, windowed=windowed, tq=tq, nblk=nblk),
        grid=(bsz, D_KV_HEADS // groups, steps),
        in_specs=in_specs,
        out_specs=q_spec,
        out_shape=jax.ShapeDtypeStruct((bsz, n, DQ_W), BF16),
        compiler_params=_params(3),
        name="sink_attn",
    )(*args)


def _dup_heads(a):
    lead = a.shape[:-1]
    a = a.reshape(lead + (D_KV_HEADS, 1, HEAD_DIM))
    return jnp.broadcast_to(a, lead + (D_KV_HEADS, 2, HEAD_DIM)).reshape(lead + (DKV_DUP_W,))


def _odd_weights(w_in_odd, c_qn_g, c_kn_g, d_qn_g, d_kn_g):
    cq, ck, cv, dq, dk, dv = jnp.split(
        w_in_odd, (C_W, 2 * C_W, 3 * C_W, 3 * C_W + DQ_W, 3 * C_W + DQ_W + DKV_W), axis=-1)
    w = jnp.concatenate([cq, ck, dq, _dup_heads(dk), cv, _dup_heads(dv)], axis=-1).astype(BF16)
    gains = jnp.concatenate([
        jnp.tile(c_qn_g * (ATTN_SCALE * LOG2E), C_W // HEAD_DIM),
        jnp.tile(c_kn_g, C_W // HEAD_DIM),
        jnp.tile(d_qn_g * (ATTN_SCALE * LOG2E), DQ_W // HEAD_DIM),
        jnp.tile(d_kn_g, DKV_DUP_W // HEAD_DIM),
    ])[None, :]
    return w, gains


def _value_rows(v):
    bsz, nk, w = v.shape
    vt = v.reshape(bsz, nk, w // LANES, LANES).transpose(0, 2, 3, 1)
    pad = jnp.zeros((bsz, w // LANES, V_ROWS - LANES, nk), v.dtype).at[:, :, 0, :].set(1)
    return jnp.concatenate([vt, pad], axis=2)


def _rope_tables(n):
    rows = n // GRID_W
    row = np.repeat(np.arange(rows), GRID_W).astype(np.float64)
    col = np.tile(np.arange(GRID_W), rows).astype(np.float64)
    freqs = ROPE_BASE ** (-np.arange(ROPE_FREQS, dtype=np.float64) / ROPE_FREQS)
    ang = np.stack([row[:, None] * freqs, col[:, None] * freqs], axis=1)
    ang = np.concatenate([ang, ang], axis=-1).reshape(n, HEAD_DIM)
    ang = np.concatenate([ang, ang], axis=-1)
    return jnp.asarray(np.cos(ang), F32), jnp.asarray(np.sin(ang), F32)


def _token_tile(n):
    return min(n, 512)


def _run_group(x, mod, row_of_batch, wts, caches):
    bsz, n, _ = x.shape
    decode = caches is not None
    tm = _token_tile(n)
    depth = wts["norm1_g"].shape[0]
    new_cache = []
    for l in range(depth):
        g1n = wts["norm1_g"][l][None, :]
        g2n = wts["norm2_g"][l][None, :]
        if l % 2 == 0:
            e = l // 2
            a, z = _even_in_call(x, mod, l, row_of_batch, g1n, e, wts["w_in_even"], wts["w_pool"],
                                 wts["pool_scale"][e][None, :], wts["dftc"], wts["w_fft"], tm)
            y = _dft_two_stage_call(z) if n > DFT_DIRECT_MAX else _dft_direct_call(z)
            p1, p2, w_out, out_idx = a, y, wts["w_out_even"], e
        else:
            o = l // 2
            lam_init = _lam_init(l)
            rope_tabs = _rope_tables(n) if decode else None
            xo = x if decode else x.reshape(1, bsz * n, D_MODEL)
            outs = _odd_in_call(xo, mod, l, row_of_batch, g1n, wts["w_in_odd"][o], wts["odd_gains"][o],
                                wts["head_mean"], rope_tabs, not decode, _token_tile(xo.shape[1]))
            if not decode:
                outs = [a if a.ndim == 4 else a.reshape(bsz, n, a.shape[-1]) for a in outs]
            cq, ck, cvt, dq, dkk, dvt = outs[:6]
            if decode:
                c_k, c_v, d_k, d_v = caches
                lc = c_k.shape[2]
                ks = [c_k[:, o].reshape(bsz, lc, C_W).astype(BF16), ck]
                vts = [_value_rows(c_v[:, o].reshape(bsz, lc, C_W).astype(BF16)), cvt]
                ctx_k = _dup_heads(d_k[:, o].reshape(bsz, lc, DKV_W)).astype(BF16)
                ctx_vt = _value_rows(_dup_heads(d_v[:, o].reshape(bsz, lc, DKV_W)).astype(BF16))
                tq = min(n, 256)
            else:
                new_cache.append(outs[6:])
                ks, vts = [ck], [cvt]
                ctx_k = ctx_vt = None
                tq = n
            tiles = 2 if n % (2 * tq) == 0 else 1
            heads = 1 if decode else C_HEADS
            c_out = _diff_attn_call(cq, ks, vts, wts["lamv"][o],
                                    wts["c_subln_g"][o][None, :], lam_init, tq, tiles, heads, not decode)
            sink_tq = WINDOW if decode else n
            sink_rows = jnp.repeat(wts["d_sink"][o].reshape(D_KV_HEADS, D_GROUP) * LOG2E, sink_tq,
                                   axis=-1)[:, None, :]
            d_out = _sink_attn_call(dq, dkk, dvt, sink_rows, ctx_k, ctx_vt, not decode)
            p1, p2, w_out, out_idx = c_out, d_out, wts["w_out_odd"], o
        x = _out_mlp_call(x, p1, p2, mod, l, row_of_batch, g2n, w_out, out_idx,
                          wts["w_mlp1"], wts["w_mlp2"], tm)
    return x, new_cache


def kernel(x_prompt, x_sample, c, cache_c_k, cache_c_v, cache_d_k, cache_d_v, c_ctx,
           norm1_g, norm2_g, w_ada, b_ada, w_in_even, w_pool, pool_scale, w_fft, w_out_even,
           w_in_odd, c_qn_g, c_kn_g, lam_q1, lam_k1, lam_q2, lam_k2, c_subln_g,
           d_qn_g, d_kn_g, d_sink, w_out_odd, w_mlp1, w_mlp2):
    depth = norm1_g.shape[0]
    n_odd = w_in_odd.shape[0]
    dec_b = c.shape[0]

    rows = -(-(1 + dec_b) // SUBLANES) * SUBLANES
    cond = jnp.zeros((rows, D_MODEL), F32).at[0].set(c_ctx).at[1:1 + dec_b].set(c)
    mod = _adaln_call(cond, w_ada, b_ada).reshape(depth, rows, 6, 1, D_MODEL)

    ic = np.arange(GROUP_W)
    ang = 2.0 * np.pi * np.outer(ic, ic) / GROUP_W
    dftc = _bf16_const(np.concatenate([np.cos(ang), -np.sin(ang)], axis=1))
    head_mean = _bf16_const(np.kron(np.eye(2 * LANES // HEAD_DIM), np.ones((HEAD_DIM, HEAD_DIM))) / HEAD_DIM)

    odd = [_odd_weights(w_in_odd[o], c_qn_g[o], c_kn_g[o], d_qn_g[o], d_kn_g[o]) for o in range(n_odd)]
    bf16 = lambda w: w.astype(BF16)
    wts = {
        "norm1_g": norm1_g, "norm2_g": norm2_g,
        "w_in_even": bf16(w_in_even), "w_pool": bf16(w_pool), "pool_scale": pool_scale,
        "w_fft": bf16(w_fft), "w_out_even": bf16(w_out_even), "dftc": dftc,
        "w_in_odd": [w for w, _ in odd], "odd_gains": [g for _, g in odd], "head_mean": head_mean,
        "lamv": jnp.stack([lam_q1, lam_k1, lam_q2, lam_k2], axis=1),
        "c_subln_g": c_subln_g, "d_sink": d_sink, "w_out_odd": bf16(w_out_odd),
        "w_mlp1": bf16(w_mlp1), "w_mlp2": bf16(w_mlp2),
    }

    y_prompt, new_cache = _run_group(x_prompt, mod, lambda b: 0, wts, None)
    y_sample, _ = _run_group(x_sample, mod, lambda b: 1 + b, wts,
                             (cache_c_k, cache_c_v, cache_d_k, cache_d_v))

    bsz, n, _ = x_prompt.shape
    new_c_k = jnp.stack([nc[0].reshape(bsz, n, C_HEADS, 2, HEAD_DIM) for nc in new_cache], axis=1)
    new_c_v = jnp.stack([nc[1].reshape(bsz, n, C_HEADS, 2 * HEAD_DIM) for nc in new_cache], axis=1)
    new_d_k = jnp.stack([nc[2].reshape(bsz, n, D_KV_HEADS, HEAD_DIM) for nc in new_cache], axis=1)
    new_d_v = jnp.stack([nc[3].reshape(bsz, n, D_KV_HEADS, HEAD_DIM) for nc in new_cache], axis=1)
    return (y_prompt, y_sample, new_c_k, new_c_v, new_d_k, new_d_v)
```

```python
import functools
import math

import numpy as np
import jax
import jax.numpy as jnp
from jax import lax
from jax.experimental import pallas as pl
from jax.experimental.pallas import tpu as pltpu

F32 = jnp.float32
BF16 = jnp.bfloat16

D_MODEL = 1024
HEAD_DIM = 64
LANES = 128
SUBLANES = 8
GRID_W = 64
ROPE_FREQS = HEAD_DIM // 4
ROPE_BASE = 10000.0
EPS = 1e-6
NEG_INF = -1e30
WINDOW = 128
A_WIDTH = D_MODEL // 2
B_WIDTH = D_MODEL // 2
POOL_WINDOWS = (2, 4, 8, 16)
POOL_HALO = 8
N_GROUPS = 4
GROUP_W = A_WIDTH // N_GROUPS
C_HEADS = 4
D_HEADS = 8
D_KV_HEADS = 2
D_GROUP = D_HEADS // D_KV_HEADS
C_W = C_HEADS * 2 * HEAD_DIM
DQ_W = D_HEADS * HEAD_DIM
DKV_W = D_KV_HEADS * HEAD_DIM
DKV_DUP_W = 2 * DKV_W
D_FF = 4 * D_MODEL
ATTN_SCALE = HEAD_DIM ** -0.5
LOG2E = math.log2(math.e)
V_ROWS = LANES + 16
ODD_NORMED_W = C_W + C_W + DQ_W + DKV_DUP_W
ODD_W = ODD_NORMED_W + C_W + DKV_DUP_W
VMEM_LIMIT = 56 * 1024 * 1024


def _lam_init(layer):
    return 0.8 - 0.6 * math.exp(-0.3 * layer)


def _params(n_axes):
    return pltpu.CompilerParams(dimension_semantics=("arbitrary",) * n_axes,
                                vmem_limit_bytes=VMEM_LIMIT)


def _const_spec(shape):
    nd = len(shape)
    return pl.BlockSpec(shape, lambda *_: (0,) * nd, pipeline_mode=pl.Buffered(1))


def _stacked_spec(shape, idx):
    nd = len(shape)
    return pl.BlockSpec((None,) + tuple(shape), lambda *_: (idx,) + (0,) * nd, pipeline_mode=pl.Buffered(1))


def _mod_spec(layer, which, row_of_batch):
    return pl.BlockSpec((None, None, None, 1, D_MODEL),
                        lambda b, i: (layer, row_of_batch(b), which, 0, 0))


def _rms_mod(x, g, sc, sh):
    ms = jnp.mean(x * x, axis=-1, keepdims=True)
    return (x * lax.rsqrt(ms + EPS) * g) * (1.0 + sc) + sh


def _bf16_const(a):
    return jnp.asarray(a, F32).astype(BF16)


def _dot_nt(a, b):
    return lax.dot_general(a, b, (((1,), (1,)), ((), ())), preferred_element_type=F32)


def _dot(a, b):
    return jnp.dot(a, b, preferred_element_type=F32)


def _adaln_kernel(cond_ref, w_ref, b_ref, o_ref):
    cnd = cond_ref[...]
    s = (cnd * jax.nn.sigmoid(cnd)).astype(BF16)
    o_ref[...] = _dot(s, w_ref[...].astype(BF16)) + b_ref[...]


def _adaln_call(cond, w_ada, b_ada):
    depth = w_ada.shape[0]
    rows = cond.shape[0]
    tn = 1536
    return pl.pallas_call(
        _adaln_kernel,
        grid=(depth, 6 * D_MODEL // tn),
        in_specs=[
            pl.BlockSpec((rows, D_MODEL), lambda l, j: (0, 0)),
            pl.BlockSpec((None, D_MODEL, tn), lambda l, j: (l, 0, j)),
            pl.BlockSpec((None, 1, tn), lambda l, j: (l, 0, j)),
        ],
        out_specs=pl.BlockSpec((None, rows, tn), lambda l, j: (l, 0, j)),
        out_shape=jax.ShapeDtypeStruct((depth, rows, 6 * D_MODEL), F32),
        compiler_params=_params(2),
        name="adaln",
    )(cond, w_ada, b_ada.reshape(depth, 1, 6 * D_MODEL))


def _even_in_kernel(x_ref, xp_ref, xn_ref, sh_ref, sc_ref, g_ref, win_ref, wpool_ref,
                    pscale_ref, dftc_ref, wfft_ref, a_ref, z_ref, ext_ref, *, tm, n):
    i = pl.program_id(1)
    n_tiles = n // tm
    g, sc, sh = g_ref[...], sc_ref[...], sh_ref[...]
    h = _rms_mod(x_ref[...], g, sc, sh).astype(BF16)
    u = _dot(h, win_ref[...])
    xh = jnp.concatenate([xp_ref[...], xn_ref[...]], axis=0)
    hh = _rms_mod(xh, g, sc, sh).astype(BF16)
    uh = _dot(hh, win_ref[:, :A_WIDTH])
    has_prev = (i > 0).astype(F32)
    has_next = (i < n_tiles - 1).astype(F32)
    ext_ref[0:POOL_HALO, :] = uh[0:POOL_HALO] * has_prev
    ext_ref[POOL_HALO:POOL_HALO + tm, :] = u[:, :A_WIDTH]
    ext_ref[POOL_HALO + tm:, :] = uh[POOL_HALO:] * has_next

    t = i * tm + lax.broadcasted_iota(jnp.int32, (tm, 1), 0)
    for gi, w in enumerate(POOL_WINDOWS):
        cols = slice(gi * GROUP_W, (gi + 1) * GROUP_W)
        acc = None
        for j in range(-(w // 2), w // 2):
            v = ext_ref[pl.ds(POOL_HALO + j, tm), cols]
            acc = v if acc is None else acc + v
        lo = jnp.maximum(t - w // 2, 0)
        hi = jnp.minimum(t + w // 2, n)
        cnt = (hi - lo).astype(F32)
        pooled = acc / cnt - u[:, cols]
        y = _dot(pooled.astype(BF16), wpool_ref[gi]) * pscale_ref[:, cols]
        a_ref[:, cols] = y.astype(BF16)

    for gi in range(N_GROUPS):
        ub = u[:, A_WIDTH + gi * GROUP_W:A_WIDTH + (gi + 1) * GROUP_W].astype(BF16)
        cs = _dot(ub, dftc_ref[...])
        zr = _dot(cs[:, :GROUP_W].astype(BF16), wfft_ref[gi])
        zi = _dot(cs[:, GROUP_W:].astype(BF16), wfft_ref[gi])
        z_ref[:, gi * GROUP_W:(gi + 1) * GROUP_W] = zr.astype(BF16)
        z_ref[:, B_WIDTH + gi * GROUP_W:B_WIDTH + (gi + 1) * GROUP_W] = zi.astype(BF16)


def _even_in_call(x, mod, layer, row_of_batch, g, e, w_in, w_pool, pool_scale, dftc, w_fft, tm):
    bsz, n, _ = x.shape
    hb = tm // POOL_HALO
    kern = functools.partial(_even_in_kernel, tm=tm, n=n)
    return pl.pallas_call(
        kern,
        grid=(bsz, n // tm),
        in_specs=[
            pl.BlockSpec((None, tm, D_MODEL), lambda b, i: (b, i, 0)),
            pl.BlockSpec((None, POOL_HALO, D_MODEL),
                         lambda b, i: (b, jnp.maximum(i * hb - 1, 0), 0)),
            pl.BlockSpec((None, POOL_HALO, D_MODEL),
                         lambda b, i: (b, jnp.minimum((i + 1) * hb, n // POOL_HALO - 1), 0)),
            _mod_spec(layer, 0, row_of_batch),
            _mod_spec(layer, 1, row_of_batch),
            _const_spec((1, D_MODEL)),
            _stacked_spec((D_MODEL, D_MODEL), e),
            _stacked_spec((N_GROUPS, GROUP_W, GROUP_W), e),
            _const_spec((1, A_WIDTH)),
            _const_spec((GROUP_W, 2 * GROUP_W)),
            _stacked_spec((N_GROUPS, GROUP_W, GROUP_W), e),
        ],
        out_specs=[
            pl.BlockSpec((None, tm, A_WIDTH), lambda b, i: (b, i, 0)),
            pl.BlockSpec((None, tm, 2 * B_WIDTH), lambda b, i: (b, i, 0)),
        ],
        out_shape=[
            jax.ShapeDtypeStruct((bsz, n, A_WIDTH), BF16),
            jax.ShapeDtypeStruct((bsz, n, 2 * B_WIDTH), BF16),
        ],
        scratch_shapes=[pltpu.VMEM((tm + 2 * POOL_HALO, A_WIDTH), F32)],
        compiler_params=_params(2),
        name="even_in",
    )(x, x, x, mod, mod, g, w_in, w_pool, pool_scale, dftc, w_fft)


def _dft_direct_kernel(z_ref, m_ref, y_ref, *, scale):
    z = z_ref[...]
    rhs = jnp.concatenate([z[:, :B_WIDTH], z[:, B_WIDTH:]], axis=0)
    y_ref[...] = (_dot(m_ref[...], rhs) * scale).astype(BF16)


def _dft_direct_call(z):
    bsz, n, _ = z.shape
    k = np.arange(n)
    ang = 2.0 * np.pi * np.outer(k, k) / n
    m = _bf16_const(np.concatenate([np.cos(ang), np.sin(ang)], axis=1))
    kern = functools.partial(_dft_direct_kernel, scale=float((n * GROUP_W) ** -0.5))
    return pl.pallas_call(
        kern,
        grid=(bsz,),
        in_specs=[pl.BlockSpec((None, n, 2 * B_WIDTH), lambda b: (b, 0, 0)),
                  _const_spec((n, 2 * n))],
        out_specs=pl.BlockSpec((None, n, B_WIDTH), lambda b: (b, 0, 0)),
        out_shape=jax.ShapeDtypeStruct((bsz, n, B_WIDTH), BF16),
        compiler_params=_params(1),
        name="dft_direct",
    )(z, m)


DFT_ROWS = 16
DFT_DIRECT_MAX = 256


def _dft_stage1_kernel(z_ref, m_ref, twr_ref, twi_ref, o_ref, zs_ref, os_ref, *, na):
    nt = 2 * B_WIDTH // LANES
    zf = z_ref[...].astype(F32).reshape(na * DFT_ROWS, 2 * B_WIDTH)
    for k in range(nt):
        zs_ref[k] = zf[:, k * LANES:(k + 1) * LANES]
    for bi in range(DFT_ROWS):
        rows = pl.ds(bi, na, stride=DFT_ROWS)
        zb = [zs_ref[k, rows, :].astype(BF16) for k in range(nt)]
        rhs = jnp.concatenate([jnp.concatenate(zb[:nt // 2], axis=1),
                               jnp.concatenate(zb[nt // 2:], axis=1)], axis=0)
        r = _dot(m_ref[...], rhs)
        br, bim = r[:na], r[na:]
        tr, ti = twr_ref[bi], twi_ref[bi]
        out = (br * tr - bim * ti, br * ti + bim * tr)
        for k in range(nt):
            half, kk = divmod(k, nt // 2)
            os_ref[k, rows, :] = out[half][:, kk * LANES:(kk + 1) * LANES]
    of = jnp.concatenate([os_ref[k] for k in range(nt)], axis=1)
    o_ref[...] = of.reshape(na, DFT_ROWS, 2 * B_WIDTH).astype(BF16)


def _dft_stage2_kernel(b_ref, m_ref, y_ref, ys_ref, *, scale):
    nb = b_ref.shape[1]
    for di in range(DFT_ROWS):
        bb = b_ref[di]
        rhs = jnp.concatenate([bb[:, :B_WIDTH], bb[:, B_WIDTH:]], axis=0)
        y = _dot(m_ref[...], rhs) * scale
        for k in range(B_WIDTH // LANES):
            ys_ref[k, pl.ds(di, nb, stride=DFT_ROWS), :] = y[:, k * LANES:(k + 1) * LANES]
    yf = jnp.concatenate([ys_ref[k] for k in range(B_WIDTH // LANES)], axis=1)
    y_ref[...] = yf.reshape(nb, DFT_ROWS, B_WIDTH).astype(BF16)


def _dft_two_stage_call(z):
    bsz, n, _ = z.shape
    na = 1 << (int(math.log2(n)) // 2)
    nb = n // na
    ia = np.arange(na)
    ib = np.arange(nb)
    ang_a = 2.0 * np.pi * np.outer(ia, ia) / na
    fr, fi = np.cos(ang_a), -np.sin(ang_a)
    m1 = _bf16_const(np.block([[fr, -fi], [fi, fr]]))
    ang_t = 2.0 * np.pi * np.outer(ib, ia) / n
    twr = jnp.asarray(np.cos(ang_t)[:, :, None], F32)
    twi = jnp.asarray(-np.sin(ang_t)[:, :, None], F32)
    ang_b = 2.0 * np.pi * np.outer(ib, ib) / nb
    m2 = _bf16_const(np.concatenate([np.cos(ang_b), np.sin(ang_b)], axis=1))

    s1 = pl.pallas_call(
        functools.partial(_dft_stage1_kernel, na=na),
        grid=(bsz, nb // DFT_ROWS),
        in_specs=[
            pl.BlockSpec((None, na, DFT_ROWS, 2 * B_WIDTH), lambda b, j: (b, 0, j, 0)),
            _const_spec((2 * na, 2 * na)),
            pl.BlockSpec((DFT_ROWS, na, 1), lambda b, j: (j, 0, 0)),
            pl.BlockSpec((DFT_ROWS, na, 1), lambda b, j: (j, 0, 0)),
        ],
        out_specs=pl.BlockSpec((None, na, DFT_ROWS, 2 * B_WIDTH), lambda b, j: (b, 0, j, 0)),
        out_shape=jax.ShapeDtypeStruct((bsz, na, nb, 2 * B_WIDTH), BF16),
        scratch_shapes=[pltpu.VMEM((2 * B_WIDTH // LANES, na * DFT_ROWS, LANES), F32)] * 2,
        compiler_params=_params(2),
        name="dft_stage1",
    )(z.reshape(bsz, na, nb, 2 * B_WIDTH), m1, twr, twi)
    y = pl.pallas_call(
        functools.partial(_dft_stage2_kernel, scale=float((n * GROUP_W) ** -0.5)),
        grid=(bsz, na // DFT_ROWS),
        in_specs=[
            pl.BlockSpec((None, DFT_ROWS, nb, 2 * B_WIDTH), lambda b, j: (b, j, 0, 0)),
            _const_spec((nb, 2 * nb)),
        ],
        out_specs=pl.BlockSpec((None, nb, DFT_ROWS, B_WIDTH), lambda b, j: (b, 0, j, 0)),
        out_shape=jax.ShapeDtypeStruct((bsz, nb, na, B_WIDTH), BF16),
        scratch_shapes=[pltpu.VMEM((B_WIDTH // LANES, nb * DFT_ROWS, LANES), F32)],
        compiler_params=_params(2),
        name="dft_stage2",
    )(s1, m2)
    return y.reshape(bsz, n, B_WIDTH)


FF_CHUNK = 1024


def _out_mlp_kernel(x_ref, p1_ref, p2_ref, g1_ref, sh_ref, sc_ref, g2_ref, ng_ref,
                    wout_ref, w1_ref, w2_ref, o_ref):
    half = p1_ref.shape[-1]
    mix = _dot(p1_ref[...], wout_ref[:half, :]) + _dot(p2_ref[...], wout_ref[half:, :])
    x1 = x_ref[...] + g1_ref[...] * mix
    h = _rms_mod(x1, ng_ref[...], sc_ref[...], sh_ref[...]).astype(BF16)
    acc = None
    for c in range(D_FF // FF_CHUNK):
        a = _dot(h, w1_ref[:, c * FF_CHUNK:(c + 1) * FF_CHUNK])
        a = jnp.square(jnp.maximum(a, 0.0)).astype(BF16)
        part = _dot(a, w2_ref[c * FF_CHUNK:(c + 1) * FF_CHUNK, :])
        acc = part if acc is None else acc + part
    o_ref[...] = x1 + g2_ref[...] * acc


def _out_mlp_call(x, p1, p2, mod, layer, row_of_batch, ng, w_out, out_idx, w1, w2, tm):
    bsz, n, _ = x.shape
    half = p1.shape[-1]
    tok = lambda w: pl.BlockSpec((None, tm, w), lambda b, i: (b, i, 0))
    return pl.pallas_call(
        _out_mlp_kernel,
        grid=(bsz, n // tm),
        in_specs=[
            tok(D_MODEL), tok(half), tok(half),
            _mod_spec(layer, 2, row_of_batch),
            _mod_spec(layer, 3, row_of_batch),
            _mod_spec(layer, 4, row_of_batch),
            _mod_spec(layer, 5, row_of_batch),
            _const_spec((1, D_MODEL)),
            _stacked_spec((2 * half, D_MODEL), out_idx),
            _stacked_spec((D_MODEL, D_FF), layer),
            _stacked_spec((D_FF, D_MODEL), layer),
        ],
        out_specs=tok(D_MODEL),
        out_shape=jax.ShapeDtypeStruct((bsz, n, D_MODEL), F32),
        compiler_params=_params(2),
        name="out_mlp",
    )(x, p1, p2, mod, mod, mod, mod, ng, w_out, w1, w2)


def _odd_in_kernel(*refs, rope, emit_cache, tm):
    x_ref, sh_ref, sc_ref, g_ref, win_ref, gain_ref, hm_ref = refs[:7]
    k = 7
    if rope:
        cos_ref, sin_ref = refs[k:k + 2]
        k += 2
    cq_ref, ck_ref, cvt_ref, dq_ref, dk_ref, dvt_ref = refs[k:k + 6]
    k += 6
    if emit_cache:
        nck_ref, ncv_ref, ndk_ref, ndv_ref = refs[k:k + 4]
    subs = x_ref.shape[0] // tm
    lane = lax.broadcasted_iota(jnp.int32, (tm, LANES), 1)
    first_half = (lane % (2 * ROPE_FREQS)) < ROPE_FREQS
    pad_row = lax.broadcasted_iota(jnp.int32, (V_ROWS - LANES, tm), 0)
    pad = jnp.where(pad_row == 0, 1.0, 0.0).astype(BF16)

    def cache_rows(ref, sub, r):
        rpt = ref.shape[0] // x_ref.shape[0]
        return ref.at[pl.ds(sub * tm * rpt + r, tm, stride=rpt), :]

    def cache_cols(ref, sub, lead, mat):
        n_req = ref.shape[-1]
        for r in range(tm // n_req):
            ref[(sub * (tm // n_req) + r,) + lead] = mat[:, r * n_req:(r + 1) * n_req]

    def project(sub):
        x = x_ref[sub * tm:(sub + 1) * tm, :]
        h = _rms_mod(x, g_ref[...], sc_ref[...], sh_ref[...]).astype(BF16)
        return _dot(h, win_ref[...])

    def head_norms(qkv):
        normed = []
        for c in range(ODD_NORMED_W // (2 * LANES)):
            v = qkv[:, c * 2 * LANES:(c + 1) * 2 * LANES]
            ms = _dot((v * v).astype(BF16), hm_ref[...])
            y = v * lax.rsqrt(ms + EPS) * gain_ref[:, c * 2 * LANES:(c + 1) * 2 * LANES]
            normed += [y[:, :LANES], y[:, LANES:]]
        return normed

    def finish(sub, qkv, normed):
        rows = slice(sub * tm, (sub + 1) * tm)

        def roped(y):
            if not rope:
                return y
            rot = jnp.where(first_half, -pltpu.roll(y, LANES - ROPE_FREQS, 1),
                            pltpu.roll(y, ROPE_FREQS, 1))
            return y * cos_ref[rows, :] + rot * sin_ref[rows, :]

        nq = C_W // LANES
        for c in range(nq):
            cq_ref[rows, c * LANES:(c + 1) * LANES] = roped(normed[c]).astype(BF16)
        for c in range(nq):
            y = normed[nq + c]
            if emit_cache:
                yt = y.T
                for half in range(2):
                    cache_cols(nck_ref, sub, (c, half), yt[half * HEAD_DIM:(half + 1) * HEAD_DIM])
            ck_ref[rows, c * LANES:(c + 1) * LANES] = roped(y).astype(BF16)
        for c in range(DQ_W // LANES):
            dq_ref[rows, c * LANES:(c + 1) * LANES] = roped(normed[2 * nq + c]).astype(BF16)
        dk_chunks = []
        for c in range(DKV_DUP_W // LANES):
            y = normed[2 * nq + DQ_W // LANES + c]
            dk_chunks.append(y)
            dk_ref[rows, c * LANES:(c + 1) * LANES] = roped(y).astype(BF16)
        cv = qkv[:, ODD_NORMED_W:ODD_NORMED_W + C_W]
        dv = qkv[:, ODD_NORMED_W + C_W:]
        for v, vt_ref in ((cv, cvt_ref), (dv, dvt_ref)):
            for hh in range(vt_ref.shape[0]):
                vt_ref[hh, :LANES, rows] = v[:, hh * LANES:(hh + 1) * LANES].T.astype(BF16)
                vt_ref[hh, LANES:, rows] = pad
        if emit_cache:
            for hh in range(C_HEADS):
                cache_rows(ncv_ref, sub, hh)[...] = cv[:, hh * LANES:(hh + 1) * LANES]
            for j in range(D_KV_HEADS):
                cache_cols(ndk_ref, sub, (j,), dk_chunks[j].T[:HEAD_DIM])
                cache_cols(ndv_ref, sub, (j,), dv[:, j * LANES:(j + 1) * LANES].T[:HEAD_DIM])

    qkv = project(0)
    for sub in range(subs):
        normed = head_norms(qkv)
        nxt = project(sub + 1) if sub + 1 < subs else None
        finish(sub, qkv, normed)
        qkv = nxt


def _odd_in_call(x, mod, layer, row_of_batch, g, w_in, gains, head_mean, rope_tabs, tm, cache_n=None):
    emit_cache = cache_n is not None
    bsz, n, _ = x.shape
    rope = rope_tabs is not None
    subs = 2 if n % (2 * tm) == 0 else 1
    ts = subs * tm
    tok = lambda w: pl.BlockSpec((None, ts, w), lambda b, i: (b, i, 0))
    in_specs = [
        tok(D_MODEL),
        _mod_spec(layer, 0, row_of_batch),
        _mod_spec(layer, 1, row_of_batch),
        _const_spec((1, D_MODEL)),
        _const_spec((D_MODEL, ODD_W)),
        _const_spec((1, ODD_NORMED_W)),
        _const_spec((2 * LANES, 2 * LANES)),
    ]
    args = [x, mod, mod, g, w_in, gains, head_mean]
    if rope:
        in_specs += [pl.BlockSpec((ts, LANES), lambda b, i: (i, 0))] * 2
        args += list(rope_tabs)
    vt_spec = lambda heads: pl.BlockSpec((None, heads, V_ROWS, ts), lambda b, i: (b, 0, 0, i))
    vt_shape = lambda heads: jax.ShapeDtypeStruct((bsz, heads, V_ROWS, n), BF16)
    tok_shape = lambda w: jax.ShapeDtypeStruct((bsz, n, w), BF16)
    out_specs = [tok(C_W), tok(C_W), vt_spec(C_HEADS), tok(DQ_W), tok(DKV_DUP_W), vt_spec(D_KV_HEADS)]
    out_shape = [tok_shape(C_W), tok_shape(C_W), vt_shape(C_HEADS), tok_shape(DQ_W),
                 tok_shape(DKV_DUP_W), vt_shape(D_KV_HEADS)]
    if emit_cache:
        assert bsz == 1 and tm % cache_n == 0 and n % cache_n == 0
        reqs, rps = n // cache_n, ts // cache_n
        for lead in ((C_HEADS, 2), None, (D_KV_HEADS,), (D_KV_HEADS,)):
            if lead is None:
                out_specs.append(pl.BlockSpec((None, ts * C_HEADS, LANES), lambda b, i: (b, i, 0)))
                out_shape.append(jax.ShapeDtypeStruct((bsz, n * C_HEADS, LANES), F32))
            else:
                zeros = (0,) * (len(lead) + 2)
                out_specs.append(pl.BlockSpec((rps,) + lead + (HEAD_DIM, cache_n),
                                              lambda b, i, zeros=zeros: (i,) + zeros))
                out_shape.append(jax.ShapeDtypeStruct((reqs,) + lead + (HEAD_DIM, cache_n), F32))
    return pl.pallas_call(
        functools.partial(_odd_in_kernel, rope=rope, emit_cache=emit_cache, tm=tm),
        grid=(bsz, n // ts),
        in_specs=in_specs,
        out_specs=out_specs,
        out_shape=out_shape,
        compiler_params=_params(2),
        name="odd_in",
    )(*args)


QK_AHEAD = 3


def _diff_attn_kernel(*refs, lam_init, tq, kc):
    n_src = (len(refs) - 4) // 2
    q_ref, (lamv_ref, g_ref, o_ref) = refs[0], refs[-3:]
    k_refs, vt_refs = refs[1:1 + n_src], refs[1 + n_src:1 + 2 * n_src]
    chunks = [(src, off) for src in range(n_src) for off in range(0, k_refs[src].shape[0], kc)]
    n_chunks = len(chunks)
    tiles = q_ref.shape[0] // tq
    heads = q_ref.shape[1] // LANES
    lane = lax.broadcasted_iota(jnp.int32, (tq, LANES), 1)
    lv = lamv_ref[...]
    lam = (jnp.exp(jnp.sum(lv[0:1] * lv[1:2], keepdims=True))
           - jnp.exp(jnp.sum(lv[2:3] * lv[3:4], keepdims=True)) + lam_init)

    qqs = {}

    def score(item):
        h, t, c = item
        cols = slice(h * LANES, (h + 1) * LANES)
        if (h, t) not in qqs:
            q = q_ref[t * tq:(t + 1) * tq, cols]
            zero = jnp.zeros_like(q)
            qqs[h, t] = jnp.concatenate([jnp.where(lane < HEAD_DIM, q, zero),
                                         jnp.where(lane >= HEAD_DIM, q, zero)], axis=0)
        src, off = chunks[c]
        return _dot_nt(k_refs[src][off:off + kc, cols], qqs[h, t])

    def finish(h, t, acc):
        inv = 1.0 / acc[LANES:LANES + 1, :]
        ot = acc[:LANES, :tq] * inv[:, :tq] - acc[:LANES, tq:] * (inv[:, tq:] * lam)
        o = ot.T
        ms = jnp.mean(o * o, axis=-1, keepdims=True)
        y = o * lax.rsqrt(ms + EPS) * g_ref[...] * (1.0 - lam_init)
        o_ref[t * tq:(t + 1) * tq, h * LANES:(h + 1) * LANES] = y.astype(BF16)

    items = [(h, t, c) for h in range(heads) for t in range(tiles) for c in range(n_chunks)]
    ahead = [score(it) for it in items[:QK_AHEAD]]
    m = acc = None
    for idx, (h, t, c) in enumerate(items):
        s = ahead.pop(0)
        if idx + QK_AHEAD < len(items):
            ahead.append(score(items[idx + QK_AHEAD]))
        cm = jnp.max(s, axis=0, keepdims=True)
        m_new = cm if c == 0 else jnp.maximum(m, cm)
        src, off = chunks[c]
        e = jnp.exp2(s - m_new).astype(BF16)
        pv = _dot(vt_refs[src][h, :, off:off + kc], e)
        acc = pv if c == 0 else acc * jnp.exp2(m - m_new) + pv
        m = m_new
        if c == n_chunks - 1:
            finish(h, t, acc)


def _diff_attn_call(q, ks, vts, lamv, subln_g, lam_init, tq, tiles, heads, flat_values=False):
    bsz, n, _ = q.shape
    kc = next(c for c in (512, 256, 128) if all(k.shape[1] % c == 0 for k in ks))
    hw = heads * LANES
    k_specs = [pl.BlockSpec((None, k.shape[1], hw), lambda b, h, i: (b, 0, h)) for k in ks]
    vt_index = (lambda b, h, i: (0, h, 0, b)) if flat_values else (lambda b, h, i: (b, h, 0, 0))
    vt_specs = [pl.BlockSpec((None, heads, V_ROWS, k.shape[1]), vt_index) for k in ks]
    return pl.pallas_call(
        functools.partial(_diff_attn_kernel, lam_init=lam_init, tq=tq, kc=kc),
        grid=(bsz, C_HEADS // heads, n // (tq * tiles)),
        in_specs=[pl.BlockSpec((None, tiles * tq, hw), lambda b, h, i: (b, i, h))]
        + k_specs + vt_specs + [_const_spec((4, HEAD_DIM)), _const_spec((1, LANES))],
        out_specs=pl.BlockSpec((None, tiles * tq, hw), lambda b, h, i: (b, i, h)),
        out_shape=jax.ShapeDtypeStruct((bsz, n, C_W), BF16),
        compiler_params=_params(3),
        name="diff_attn",
    )(q, *ks, *vts, lamv, subln_g)


def _sink_attn_kernel(*refs, windowed, tq, nblk):
    if windowed:
        (q_ref, ck_ref, kp_ref, km_ref, kn_ref, cvt_ref, vp_ref, vm_ref, vn_ref,
         bias_ref, sink_ref, o_ref) = refs
        kband = jnp.concatenate([kp_ref[...], km_ref[...], kn_ref[...]], axis=0)
        vtband = jnp.concatenate([vp_ref[...], vm_ref[...], vn_ref[...]], axis=1)
    else:
        q_ref, k_ref, vt_ref, sink_ref, o_ref = refs
    subs = q_ref.shape[0] // tq
    groups = q_ref.shape[1] // (2 * LANES)
    lane = lax.broadcasted_iota(jnp.int32, (tq, LANES), 1)
    lo = lane < HEAD_DIM

    def scores_of(stream):
        g, sub = stream
        qd = q_ref[sub * tq:(sub + 1) * tq, g * 2 * LANES:(g + 1) * 2 * LANES]
        parts = []
        for c in range(2):
            ch = qd[:, c * LANES:(c + 1) * LANES]
            zero = jnp.zeros_like(ch)
            parts += [jnp.where(lo, ch, zero), jnp.where(lo, zero, ch)]
        qq = jnp.concatenate(parts, axis=0)
        if not windowed:
            return [_dot_nt(k_ref[:, g * LANES:(g + 1) * LANES], qq)]
        blk = pl.program_id(2) * subs + sub
        variant = jnp.where(blk == 0, 0, jnp.where(blk == nblk - 1, 2, 1))
        band = _dot_nt(kband[sub * tq:(sub + 3) * tq, :], qq) + bias_ref[variant]
        return [_dot_nt(ck_ref[...], qq), band]

    def finish(stream, scores):
        g, sub = stream
        values = [cvt_ref[...], vtband[:, sub * tq:(sub + 3) * tq]] if windowed else [vt_ref[g]]
        sink = sink_ref[g]
        m = sink
        for s in scores:
            m = jnp.maximum(m, jnp.max(s, axis=0, keepdims=True))
        acc = None
        for s, vt in zip(scores, values):
            pv = _dot(vt, jnp.exp2(s - m).astype(BF16))
            acc = pv if acc is None else acc + pv
        inv = 1.0 / (acc[LANES:LANES + 1, :] + jnp.exp2(sink - m))
        ot = acc[:LANES, :] * inv
        for c in range(2):
            o0 = ot[:, (2 * c) * tq:(2 * c + 1) * tq].T
            o1 = ot[:, (2 * c + 1) * tq:(2 * c + 2) * tq].T
            cols = slice((2 * g + c) * LANES, (2 * g + c + 1) * LANES)
            o_ref[sub * tq:(sub + 1) * tq, cols] = jnp.where(lo, o0, o1).astype(BF16)

    streams = [(g, sub) for g in range(groups) for sub in range(subs)]
    ahead = [scores_of(st) for st in streams[:QK_AHEAD]]
    for idx, st in enumerate(streams):
        cur = ahead.pop(0)
        if idx + QK_AHEAD < len(streams):
            ahead.append(scores_of(streams[idx + QK_AHEAD]))
        finish(st, cur)


def _window_bias(n, tq):
    nblk = n // tq
    kk = np.arange(3 * tq)[:, None]
    qi = np.arange(tq)[None, :]
    band_ok = np.abs(qi + WINDOW - kk) <= WINDOW
    out = []
    for blk in (0, 1, nblk - 1):
        jpos = blk * tq - WINDOW + kk
        ok = band_ok & (jpos >= 0) & (jpos < n)
        out.append(np.tile(np.where(ok, 0.0, NEG_INF), (1, D_GROUP)))
    return jnp.asarray(np.stack(out), F32)


def _sink_attn_call(q, k, vt, sink_rows, ctx_k=None, ctx_vt=None, flat_values=False):
    bsz, n, _ = q.shape
    windowed = ctx_k is not None
    tq = WINDOW if windowed else n
    nblk = n // tq
    subs = next(s for s in (4, 2, 1) if nblk % s == 0) if windowed else 1
    steps = nblk // subs
    groups = 1 if windowed else D_KV_HEADS
    q_spec = pl.BlockSpec((None, subs * tq, groups * 2 * LANES), lambda b, j, i: (b, i, j))
    sink_spec = pl.BlockSpec((groups, 1, D_GROUP * tq), lambda b, j, i: (j, 0, 0))
    if windowed:
        assert nblk >= 3, "first / interior / last mask variants need three query blocks"
        n_ctx = ctx_k.shape[1]
        prev = lambda i: jnp.maximum(i * subs - 1, 0)
        nxt = lambda i: jnp.minimum((i + 1) * subs, nblk - 1)
        in_specs = [
            q_spec,
            pl.BlockSpec((None, n_ctx, LANES), lambda b, j, i: (b, 0, j)),
            pl.BlockSpec((None, tq, LANES), lambda b, j, i: (b, prev(i), j)),
            pl.BlockSpec((None, subs * tq, LANES), lambda b, j, i: (b, i, j)),
            pl.BlockSpec((None, tq, LANES), lambda b, j, i: (b, nxt(i), j)),
            pl.BlockSpec((None, None, V_ROWS, n_ctx), lambda b, j, i: (b, j, 0, 0)),
            pl.BlockSpec((None, None, V_ROWS, tq), lambda b, j, i: (b, j, 0, prev(i))),
            pl.BlockSpec((None, None, V_ROWS, subs * tq), lambda b, j, i: (b, j, 0, i)),
            pl.BlockSpec((None, None, V_ROWS, tq), lambda b, j, i: (b, j, 0, nxt(i))),
            _const_spec((3, 3 * tq, D_GROUP * tq)),
            sink_spec,
        ]
        args = (q, ctx_k, k, k, k, ctx_vt, vt, vt, vt, _window_bias(n, tq), sink_rows)
    else:
        in_specs = [
            q_spec,
            pl.BlockSpec((None, n, groups * LANES), lambda b, j, i: (b, 0, j)),
            pl.BlockSpec((None, groups, V_ROWS, n),
                         (lambda b, j, i: (0, j, 0, b)) if flat_values else (lambda b, j, i: (b, j, 0, 0))),
            sink_spec,
        ]
        args = (q, k, vt, sink_rows)
    return pl.pallas_call(
        functools.partial(_sink_attn_kernel, windowed=windowed, tq=tq, nblk=nblk),
        grid=(bsz, D_KV_HEADS // groups, steps),
        in_specs=in_specs,
        out_specs=q_spec,
        out_shape=jax.ShapeDtypeStruct((bsz, n, DQ_W), BF16),
        compiler_params=_params(3),
        name="sink_attn",
    )(*args)


def _dup_heads(a):
    lead = a.shape[:-1]
    a = a.reshape(lead + (D_KV_HEADS, 1, HEAD_DIM))
    return jnp.broadcast_to(a, lead + (D_KV_HEADS, 2, HEAD_DIM)).reshape(lead + (DKV_DUP_W,))


def _odd_weights(w_in_odd, c_qn_g, c_kn_g, d_qn_g, d_kn_g):
    cq, ck, cv, dq, dk, dv = jnp.split(
        w_in_odd, (C_W, 2 * C_W, 3 * C_W, 3 * C_W + DQ_W, 3 * C_W + DQ_W + DKV_W), axis=-1)
    w = jnp.concatenate([cq, ck, dq, _dup_heads(dk), cv, _dup_heads(dv)], axis=-1).astype(BF16)
    gains = jnp.concatenate([
        jnp.tile(c_qn_g * (ATTN_SCALE * LOG2E), C_W // HEAD_DIM),
        jnp.tile(c_kn_g, C_W // HEAD_DIM),
        jnp.tile(d_qn_g * (ATTN_SCALE * LOG2E), DQ_W // HEAD_DIM),
        jnp.tile(d_kn_g, DKV_DUP_W // HEAD_DIM),
    ])[None, :]
    return w, gains


def _value_rows(v):
    bsz, nk, w = v.shape
    vt = v.reshape(bsz, nk, w // LANES, LANES).transpose(0, 2, 3, 1)
    pad = jnp.zeros((bsz, w // LANES, V_ROWS - LANES, nk), v.dtype).at[:, :, 0, :].set(1)
    return jnp.concatenate([vt, pad], axis=2)


def _rope_tables(n):
    rows = n // GRID_W
    row = np.repeat(np.arange(rows), GRID_W).astype(np.float64)
    col = np.tile(np.arange(GRID_W), rows).astype(np.float64)
    freqs = ROPE_BASE ** (-np.arange(ROPE_FREQS, dtype=np.float64) / ROPE_FREQS)
    ang = np.stack([row[:, None] * freqs, col[:, None] * freqs], axis=1)
    ang = np.concatenate([ang, ang], axis=-1).reshape(n, HEAD_DIM)
    ang = np.concatenate([ang, ang], axis=-1)
    return jnp.asarray(np.cos(ang), F32), jnp.asarray(np.sin(ang), F32)


def _token_tile(n):
    return min(n, 512)


def _run_group(x, mod, row_of_batch, wts, caches):
    bsz, n, _ = x.shape
    decode = caches is not None
    tm = _token_tile(n)
    depth = wts["norm1_g"].shape[0]
    new_cache = []
    for l in range(depth):
        g1n = wts["norm1_g"][l][None, :]
        g2n = wts["norm2_g"][l][None, :]
        if l % 2 == 0:
            e = l // 2
            a, z = _even_in_call(x, mod, l, row_of_batch, g1n, e, wts["w_in_even"], wts["w_pool"],
                                 wts["pool_scale"][e][None, :], wts["dftc"], wts["w_fft"], tm)
            y = _dft_two_stage_call(z) if n > DFT_DIRECT_MAX else _dft_direct_call(z)
            p1, p2, w_out, out_idx = a, y, wts["w_out_even"], e
        else:
            o = l // 2
            lam_init = _lam_init(l)
            rope_tabs = _rope_tables(n) if decode else None
            xo = x if decode else x.reshape(1, bsz * n, D_MODEL)
            outs = _odd_in_call(xo, mod, l, row_of_batch, g1n, wts["w_in_odd"][o], wts["odd_gains"][o],
                                wts["head_mean"], rope_tabs, _token_tile(xo.shape[1]),
                                None if decode else n)
            if not decode:
                outs = [a if a.ndim == 4 else a.reshape(bsz, n, a.shape[-1]) for a in outs[:6]] + list(outs[6:])
            cq, ck, cvt, dq, dkk, dvt = outs[:6]
            if decode:
                c_k, c_v, d_k, d_v = caches
                lc = c_k.shape[2]
                ks = [c_k[:, o].reshape(bsz, lc, C_W).astype(BF16), ck]
                vts = [_value_rows(c_v[:, o].reshape(bsz, lc, C_W).astype(BF16)), cvt]
                ctx_k = _dup_heads(d_k[:, o].reshape(bsz, lc, DKV_W)).astype(BF16)
                ctx_vt = _value_rows(_dup_heads(d_v[:, o].reshape(bsz, lc, DKV_W)).astype(BF16))
                tq = min(n, 256)
            else:
                new_cache.append(outs[6:])
                ks, vts = [ck], [cvt]
                ctx_k = ctx_vt = None
                tq = n
            tiles = 2 if n % (2 * tq) == 0 else 1
            heads = 1 if decode else C_HEADS
            c_out = _diff_attn_call(cq, ks, vts, wts["lamv"][o],
                                    wts["c_subln_g"][o][None, :], lam_init, tq, tiles, heads, not decode)
            sink_tq = WINDOW if decode else n
            sink_rows = jnp.repeat(wts["d_sink"][o].reshape(D_KV_HEADS, D_GROUP) * LOG2E, sink_tq,
                                   axis=-1)[:, None, :]
            d_out = _sink_attn_call(dq, dkk, dvt, sink_rows, ctx_k, ctx_vt, not decode)
            p1, p2, w_out, out_idx = c_out, d_out, wts["w_out_odd"], o
        x = _out_mlp_call(x, p1, p2, mod, l, row_of_batch, g2n, w_out, out_idx,
                          wts["w_mlp1"], wts["w_mlp2"], tm)
    return x, new_cache


def kernel(x_prompt, x_sample, c, cache_c_k, cache_c_v, cache_d_k, cache_d_v, c_ctx,
           norm1_g, norm2_g, w_ada, b_ada, w_in_even, w_pool, pool_scale, w_fft, w_out_even,
           w_in_odd, c_qn_g, c_kn_g, lam_q1, lam_k1, lam_q2, lam_k2, c_subln_g,
           d_qn_g, d_kn_g, d_sink, w_out_odd, w_mlp1, w_mlp2):
    depth = norm1_g.shape[0]
    n_odd = w_in_odd.shape[0]
    dec_b = c.shape[0]

    rows = -(-(1 + dec_b) // SUBLANES) * SUBLANES
    cond = jnp.zeros((rows, D_MODEL), F32).at[0].set(c_ctx).at[1:1 + dec_b].set(c)
    mod = _adaln_call(cond, w_ada, b_ada).reshape(depth, rows, 6, 1, D_MODEL)

    ic = np.arange(GROUP_W)
    ang = 2.0 * np.pi * np.outer(ic, ic) / GROUP_W
    dftc = _bf16_const(np.concatenate([np.cos(ang), -np.sin(ang)], axis=1))
    head_mean = _bf16_const(np.kron(np.eye(2 * LANES // HEAD_DIM), np.ones((HEAD_DIM, HEAD_DIM))) / HEAD_DIM)

    odd = [_odd_weights(w_in_odd[o], c_qn_g[o], c_kn_g[o], d_qn_g[o], d_kn_g[o]) for o in range(n_odd)]
    bf16 = lambda w: w.astype(BF16)
    wts = {
        "norm1_g": norm1_g, "norm2_g": norm2_g,
        "w_in_even": bf16(w_in_even), "w_pool": bf16(w_pool), "pool_scale": pool_scale,
        "w_fft": bf16(w_fft), "w_out_even": bf16(w_out_even), "dftc": dftc,
        "w_in_odd": [w for w, _ in odd], "odd_gains": [g for _, g in odd], "head_mean": head_mean,
        "lamv": jnp.stack([lam_q1, lam_k1, lam_q2, lam_k2], axis=1),
        "c_subln_g": c_subln_g, "d_sink": d_sink, "w_out_odd": bf16(w_out_odd),
        "w_mlp1": bf16(w_mlp1), "w_mlp2": bf16(w_mlp2),
    }

    y_prompt, new_cache = _run_group(x_prompt, mod, lambda b: 0, wts, None)
    y_sample, _ = _run_group(x_sample, mod, lambda b: 1 + b, wts,
                             (cache_c_k, cache_c_v, cache_d_k, cache_d_v))

    bsz, n, _ = x_prompt.shape
    new_c_k = jnp.stack([nc[0].transpose(0, 4, 1, 2, 3) for nc in new_cache], axis=1)
    new_c_v = jnp.stack([nc[1].reshape(bsz, n, C_HEADS, 2 * HEAD_DIM) for nc in new_cache], axis=1)
    new_d_k = jnp.stack([nc[2].transpose(0, 3, 1, 2) for nc in new_cache], axis=1)
    new_d_v = jnp.stack([nc[3].transpose(0, 3, 1, 2) for nc in new_cache], axis=1)
    return (y_prompt, y_sample, new_c_k, new_c_v, new_d_k, new_d_v)
```

```python
import functools
import math

import numpy as np
import jax
import jax.numpy as jnp
from jax import lax
from jax.experimental import pallas as pl
from jax.experimental.pallas import tpu as pltpu

F32 = jnp.float32
BF16 = jnp.bfloat16

D_MODEL = 1024
HEAD_DIM = 64
LANES = 128
SUBLANES = 8
GRID_W = 64
ROPE_FREQS = HEAD_DIM // 4
ROPE_BASE = 10000.0
EPS = 1e-6
NEG_INF = -1e30
WINDOW = 128
A_WIDTH = D_MODEL // 2
B_WIDTH = D_MODEL // 2
POOL_WINDOWS = (2, 4, 8, 16)
POOL_HALO = 8
N_GROUPS = 4
GROUP_W = A_WIDTH // N_GROUPS
C_HEADS = 4
D_HEADS = 8
D_KV_HEADS = 2
D_GROUP = D_HEADS // D_KV_HEADS
C_W = C_HEADS * 2 * HEAD_DIM
DQ_W = D_HEADS * HEAD_DIM
DKV_W = D_KV_HEADS * HEAD_DIM
DKV_DUP_W = 2 * DKV_W
D_FF = 4 * D_MODEL
ATTN_SCALE = HEAD_DIM ** -0.5
LOG2E = math.log2(math.e)
V_ROWS = LANES + 16
ODD_NORMED_W = C_W + C_W + DQ_W + DKV_DUP_W
ODD_W = ODD_NORMED_W + C_W + DKV_DUP_W
VMEM_LIMIT = 56 * 1024 * 1024


def _lam_init(layer):
    return 0.8 - 0.6 * math.exp(-0.3 * layer)


def _params(n_axes):
    return pltpu.CompilerParams(dimension_semantics=("arbitrary",) * n_axes,
                                vmem_limit_bytes=VMEM_LIMIT)


def _const_spec(shape):
    nd = len(shape)
    return pl.BlockSpec(shape, lambda *_: (0,) * nd, pipeline_mode=pl.Buffered(1))


def _stacked_spec(shape, idx):
    nd = len(shape)
    return pl.BlockSpec((None,) + tuple(shape), lambda *_: (idx,) + (0,) * nd, pipeline_mode=pl.Buffered(1))


def _mod_spec(layer, which, row_of_batch):
    return pl.BlockSpec((None, None, None, 1, D_MODEL),
                        lambda b, i: (layer, row_of_batch(b), which, 0, 0))


def _rms_mod(x, g, sc, sh):
    ms = jnp.mean(x * x, axis=-1, keepdims=True)
    return (x * lax.rsqrt(ms + EPS) * g) * (1.0 + sc) + sh


def _bf16_const(a):
    return jnp.asarray(a, F32).astype(BF16)


def _dot_nt(a, b):
    return lax.dot_general(a, b, (((1,), (1,)), ((), ())), preferred_element_type=F32)


def _dot(a, b):
    return jnp.dot(a, b, preferred_element_type=F32)


def _adaln_kernel(cond_ref, w_ref, b_ref, o_ref):
    cnd = cond_ref[...]
    s = (cnd * jax.nn.sigmoid(cnd)).astype(BF16)
    o_ref[...] = _dot(s, w_ref[...].astype(BF16)) + b_ref[...]


def _adaln_call(cond, w_ada, b_ada):
    depth = w_ada.shape[0]
    rows = cond.shape[0]
    tn = 1536
    return pl.pallas_call(
        _adaln_kernel,
        grid=(depth, 6 * D_MODEL // tn),
        in_specs=[
            pl.BlockSpec((rows, D_MODEL), lambda l, j: (0, 0)),
            pl.BlockSpec((None, D_MODEL, tn), lambda l, j: (l, 0, j)),
            pl.BlockSpec((None, 1, tn), lambda l, j: (l, 0, j)),
        ],
        out_specs=pl.BlockSpec((None, rows, tn), lambda l, j: (l, 0, j)),
        out_shape=jax.ShapeDtypeStruct((depth, rows, 6 * D_MODEL), F32),
        compiler_params=_params(2),
        name="adaln",
    )(cond, w_ada, b_ada.reshape(depth, 1, 6 * D_MODEL))


def _even_in_kernel(x_ref, xp_ref, xn_ref, sh_ref, sc_ref, g_ref, win_ref, wpool_ref,
                    pscale_ref, dftc_ref, wfft_ref, a_ref, z_ref, ext_ref, *, tm, n):
    i = pl.program_id(1)
    n_tiles = n // tm
    g, sc, sh = g_ref[...], sc_ref[...], sh_ref[...]
    h = _rms_mod(x_ref[...], g, sc, sh).astype(BF16)
    u = _dot(h, win_ref[...])
    xh = jnp.concatenate([xp_ref[...], xn_ref[...]], axis=0)
    hh = _rms_mod(xh, g, sc, sh).astype(BF16)
    uh = _dot(hh, win_ref[:, :A_WIDTH])
    has_prev = (i > 0).astype(F32)
    has_next = (i < n_tiles - 1).astype(F32)
    ext_ref[0:POOL_HALO, :] = uh[0:POOL_HALO] * has_prev
    ext_ref[POOL_HALO:POOL_HALO + tm, :] = u[:, :A_WIDTH]
    ext_ref[POOL_HALO + tm:, :] = uh[POOL_HALO:] * has_next

    t = i * tm + lax.broadcasted_iota(jnp.int32, (tm, 1), 0)
    for gi, w in enumerate(POOL_WINDOWS):
        cols = slice(gi * GROUP_W, (gi + 1) * GROUP_W)
        acc = None
        for j in range(-(w // 2), w // 2):
            v = ext_ref[pl.ds(POOL_HALO + j, tm), cols]
            acc = v if acc is None else acc + v
        lo = jnp.maximum(t - w // 2, 0)
        hi = jnp.minimum(t + w // 2, n)
        cnt = (hi - lo).astype(F32)
        pooled = acc / cnt - u[:, cols]
        y = _dot(pooled.astype(BF16), wpool_ref[gi]) * pscale_ref[:, cols]
        a_ref[:, cols] = y.astype(BF16)

    for gi in range(N_GROUPS):
        ub = u[:, A_WIDTH + gi * GROUP_W:A_WIDTH + (gi + 1) * GROUP_W].astype(BF16)
        cs = _dot(ub, dftc_ref[...])
        zr = _dot(cs[:, :GROUP_W].astype(BF16), wfft_ref[gi])
        zi = _dot(cs[:, GROUP_W:].astype(BF16), wfft_ref[gi])
        z_ref[:, gi * GROUP_W:(gi + 1) * GROUP_W] = zr.astype(BF16)
        z_ref[:, B_WIDTH + gi * GROUP_W:B_WIDTH + (gi + 1) * GROUP_W] = zi.astype(BF16)


def _even_in_call(x, mod, layer, row_of_batch, g, e, w_in, w_pool, pool_scale, dftc, w_fft, tm):
    bsz, n, _ = x.shape
    hb = tm // POOL_HALO
    kern = functools.partial(_even_in_kernel, tm=tm, n=n)
    return pl.pallas_call(
        kern,
        grid=(bsz, n // tm),
        in_specs=[
            pl.BlockSpec((None, tm, D_MODEL), lambda b, i: (b, i, 0)),
            pl.BlockSpec((None, POOL_HALO, D_MODEL),
                         lambda b, i: (b, jnp.maximum(i * hb - 1, 0), 0)),
            pl.BlockSpec((None, POOL_HALO, D_MODEL),
                         lambda b, i: (b, jnp.minimum((i + 1) * hb, n // POOL_HALO - 1), 0)),
            _mod_spec(layer, 0, row_of_batch),
            _mod_spec(layer, 1, row_of_batch),
            _const_spec((1, D_MODEL)),
            _stacked_spec((D_MODEL, D_MODEL), e),
            _stacked_spec((N_GROUPS, GROUP_W, GROUP_W), e),
            _const_spec((1, A_WIDTH)),
            _const_spec((GROUP_W, 2 * GROUP_W)),
            _stacked_spec((N_GROUPS, GROUP_W, GROUP_W), e),
        ],
        out_specs=[
            pl.BlockSpec((None, tm, A_WIDTH), lambda b, i: (b, i, 0)),
            pl.BlockSpec((None, tm, 2 * B_WIDTH), lambda b, i: (b, i, 0)),
        ],
        out_shape=[
            jax.ShapeDtypeStruct((bsz, n, A_WIDTH), BF16),
            jax.ShapeDtypeStruct((bsz, n, 2 * B_WIDTH), BF16),
        ],
        scratch_shapes=[pltpu.VMEM((tm + 2 * POOL_HALO, A_WIDTH), F32)],
        compiler_params=_params(2),
        name="even_in",
    )(x, x, x, mod, mod, g, w_in, w_pool, pool_scale, dftc, w_fft)


def _dft_direct_kernel(z_ref, m_ref, y_ref, *, scale):
    z = z_ref[...]
    rhs = jnp.concatenate([z[:, :B_WIDTH], z[:, B_WIDTH:]], axis=0)
    y_ref[...] = (_dot(m_ref[...], rhs) * scale).astype(BF16)


def _dft_direct_call(z):
    bsz, n, _ = z.shape
    k = np.arange(n)
    ang = 2.0 * np.pi * np.outer(k, k) / n
    m = _bf16_const(np.concatenate([np.cos(ang), np.sin(ang)], axis=1))
    kern = functools.partial(_dft_direct_kernel, scale=float((n * GROUP_W) ** -0.5))
    return pl.pallas_call(
        kern,
        grid=(bsz,),
        in_specs=[pl.BlockSpec((None, n, 2 * B_WIDTH), lambda b: (b, 0, 0)),
                  _const_spec((n, 2 * n))],
        out_specs=pl.BlockSpec((None, n, B_WIDTH), lambda b: (b, 0, 0)),
        out_shape=jax.ShapeDtypeStruct((bsz, n, B_WIDTH), BF16),
        compiler_params=_params(1),
        name="dft_direct",
    )(z, m)


DFT_ROWS = 16
DFT_DIRECT_MAX = 256
DFT_PITCH = 24


def _pitch_rows(x):
    g, _, w = x.shape
    pad = jnp.zeros((g, DFT_PITCH - DFT_ROWS, w), x.dtype)
    return jnp.concatenate([x, pad], axis=1).reshape(g * DFT_PITCH, w)


def _dft_stage1_kernel(z_ref, m_ref, twr_ref, twi_ref, o_ref, zs_ref, os_ref, *, na):
    nt = 2 * B_WIDTH // LANES

    @pl.when((pl.program_id(0) == 0) & (pl.program_id(1) == 0))
    def _():
        os_ref[...] = jnp.zeros_like(os_ref)

    zf = _pitch_rows(z_ref[...].astype(F32))
    for k in range(nt):
        zs_ref[k] = zf[:, k * LANES:(k + 1) * LANES]
    for bi in range(DFT_ROWS):
        rows = pl.ds(bi, na, stride=DFT_PITCH)
        zb = [zs_ref[k, rows, :].astype(BF16) for k in range(nt)]
        rhs = jnp.concatenate([jnp.concatenate(zb[:nt // 2], axis=1),
                               jnp.concatenate(zb[nt // 2:], axis=1)], axis=0)
        r = _dot(m_ref[...], rhs)
        br, bim = r[:na], r[na:]
        tr, ti = twr_ref[bi], twi_ref[bi]
        out = (br * tr - bim * ti, br * ti + bim * tr)
        for k in range(nt):
            half, kk = divmod(k, nt // 2)
            os_ref[k, rows, :] = out[half][:, kk * LANES:(kk + 1) * LANES]
    of = jnp.concatenate([os_ref[k] for k in range(nt)], axis=1)
    o_ref[...] = of.reshape(na, DFT_PITCH, 2 * B_WIDTH)[:, :DFT_ROWS].astype(BF16)


def _dft_stage2_kernel(b_ref, m_ref, y_ref, ys_ref, *, scale):
    nb = b_ref.shape[1]

    @pl.when((pl.program_id(0) == 0) & (pl.program_id(1) == 0))
    def _():
        ys_ref[...] = jnp.zeros_like(ys_ref)

    for di in range(DFT_ROWS):
        bb = b_ref[di]
        rhs = jnp.concatenate([bb[:, :B_WIDTH], bb[:, B_WIDTH:]], axis=0)
        y = _dot(m_ref[...], rhs) * scale
        for k in range(B_WIDTH // LANES):
            ys_ref[k, pl.ds(di, nb, stride=DFT_PITCH), :] = y[:, k * LANES:(k + 1) * LANES]
    yf = jnp.concatenate([ys_ref[k] for k in range(B_WIDTH // LANES)], axis=1)
    y_ref[...] = yf.reshape(nb, DFT_PITCH, B_WIDTH)[:, :DFT_ROWS].astype(BF16)


def _dft_two_stage_call(z):
    bsz, n, _ = z.shape
    na = 1 << (int(math.log2(n)) // 2)
    nb = n // na
    ia = np.arange(na)
    ib = np.arange(nb)
    ang_a = 2.0 * np.pi * np.outer(ia, ia) / na
    fr, fi = np.cos(ang_a), -np.sin(ang_a)
    m1 = _bf16_const(np.block([[fr, -fi], [fi, fr]]))
    ang_t = 2.0 * np.pi * np.outer(ib, ia) / n
    twr = jnp.asarray(np.cos(ang_t)[:, :, None], F32)
    twi = jnp.asarray(-np.sin(ang_t)[:, :, None], F32)
    ang_b = 2.0 * np.pi * np.outer(ib, ib) / nb
    m2 = _bf16_const(np.concatenate([np.cos(ang_b), np.sin(ang_b)], axis=1))

    s1 = pl.pallas_call(
        functools.partial(_dft_stage1_kernel, na=na),
        grid=(bsz, nb // DFT_ROWS),
        in_specs=[
            pl.BlockSpec((None, na, DFT_ROWS, 2 * B_WIDTH), lambda b, j: (b, 0, j, 0)),
            _const_spec((2 * na, 2 * na)),
            pl.BlockSpec((DFT_ROWS, na, 1), lambda b, j: (j, 0, 0)),
            pl.BlockSpec((DFT_ROWS, na, 1), lambda b, j: (j, 0, 0)),
        ],
        out_specs=pl.BlockSpec((None, na, DFT_ROWS, 2 * B_WIDTH), lambda b, j: (b, 0, j, 0)),
        out_shape=jax.ShapeDtypeStruct((bsz, na, nb, 2 * B_WIDTH), BF16),
        scratch_shapes=[pltpu.VMEM((2 * B_WIDTH // LANES, na * DFT_PITCH, LANES), F32)] * 2,
        compiler_params=_params(2),
        name="dft_stage1",
    )(z.reshape(bsz, na, nb, 2 * B_WIDTH), m1, twr, twi)
    y = pl.pallas_call(
        functools.partial(_dft_stage2_kernel, scale=float((n * GROUP_W) ** -0.5)),
        grid=(bsz, na // DFT_ROWS),
        in_specs=[
            pl.BlockSpec((None, DFT_ROWS, nb, 2 * B_WIDTH), lambda b, j: (b, j, 0, 0)),
            _const_spec((nb, 2 * nb)),
        ],
        out_specs=pl.BlockSpec((None, nb, DFT_ROWS, B_WIDTH), lambda b, j: (b, 0, j, 0)),
        out_shape=jax.ShapeDtypeStruct((bsz, nb, na, B_WIDTH), BF16),
        scratch_shapes=[pltpu.VMEM((B_WIDTH // LANES, nb * DFT_PITCH, LANES), F32)],
        compiler_params=_params(2),
        name="dft_stage2",
    )(s1, m2)
    return y.reshape(bsz, n, B_WIDTH)


FF_CHUNK = 1024


def _out_mlp_kernel(x_ref, p1_ref, p2_ref, g1_ref, sh_ref, sc_ref, g2_ref, ng_ref,
                    wout_ref, w1_ref, w2_ref, o_ref):
    half = p1_ref.shape[-1]
    mix = _dot(p1_ref[...], wout_ref[:half, :]) + _dot(p2_ref[...], wout_ref[half:, :])
    x1 = x_ref[...] + g1_ref[...] * mix
    h = _rms_mod(x1, ng_ref[...], sc_ref[...], sh_ref[...]).astype(BF16)
    acc = None
    for c in range(D_FF // FF_CHUNK):
        a = _dot(h, w1_ref[:, c * FF_CHUNK:(c + 1) * FF_CHUNK])
        a = jnp.square(jnp.maximum(a, 0.0)).astype(BF16)
        part = _dot(a, w2_ref[c * FF_CHUNK:(c + 1) * FF_CHUNK, :])
        acc = part if acc is None else acc + part
    o_ref[...] = x1 + g2_ref[...] * acc


def _out_mlp_call(x, p1, p2, mod, layer, row_of_batch, ng, w_out, out_idx, w1, w2, tm):
    bsz, n, _ = x.shape
    half = p1.shape[-1]
    tok = lambda w: pl.BlockSpec((None, tm, w), lambda b, i: (b, i, 0))
    return pl.pallas_call(
        _out_mlp_kernel,
        grid=(bsz, n // tm),
        in_specs=[
            tok(D_MODEL), tok(half), tok(half),
            _mod_spec(layer, 2, row_of_batch),
            _mod_spec(layer, 3, row_of_batch),
            _mod_spec(layer, 4, row_of_batch),
            _mod_spec(layer, 5, row_of_batch),
            _const_spec((1, D_MODEL)),
            _stacked_spec((2 * half, D_MODEL), out_idx),
            _stacked_spec((D_MODEL, D_FF), layer),
            _stacked_spec((D_FF, D_MODEL), layer),
        ],
        out_specs=tok(D_MODEL),
        out_shape=jax.ShapeDtypeStruct((bsz, n, D_MODEL), F32),
        compiler_params=_params(2),
        name="out_mlp",
    )(x, p1, p2, mod, mod, mod, mod, ng, w_out, w1, w2)


def _odd_in_kernel(*refs, rope, emit_cache, tm):
    x_ref, sh_ref, sc_ref, g_ref, win_ref, gain_ref, hm_ref = refs[:7]
    k = 7
    if rope:
        cos_ref, sin_ref = refs[k:k + 2]
        k += 2
    cq_ref, ck_ref, cvt_ref, dq_ref, dk_ref, dvt_ref = refs[k:k + 6]
    k += 6
    if emit_cache:
        nck_ref, ncv_ref, ndk_ref, ndv_ref = refs[k:k + 4]
    subs = x_ref.shape[0] // tm
    lane = lax.broadcasted_iota(jnp.int32, (tm, LANES), 1)
    first_half = (lane % (2 * ROPE_FREQS)) < ROPE_FREQS
    pad_row = lax.broadcasted_iota(jnp.int32, (V_ROWS - LANES, tm), 0)
    pad = jnp.where(pad_row == 0, 1.0, 0.0).astype(BF16)

    def cache_rows(ref, sub, r):
        rpt = ref.shape[0] // x_ref.shape[0]
        return ref.at[pl.ds(sub * tm * rpt + r, tm, stride=rpt), :]

    def cache_cols(ref, sub, lead, mat):
        n_req = ref.shape[-1]
        for r in range(tm // n_req):
            ref[(sub * (tm // n_req) + r,) + lead] = mat[:, r * n_req:(r + 1) * n_req]

    def project(sub):
        x = x_ref[sub * tm:(sub + 1) * tm, :]
        h = _rms_mod(x, g_ref[...], sc_ref[...], sh_ref[...]).astype(BF16)
        return _dot(h, win_ref[...])

    def head_norms(qkv):
        normed = []
        for c in range(ODD_NORMED_W // (2 * LANES)):
            v = qkv[:, c * 2 * LANES:(c + 1) * 2 * LANES]
            ms = _dot((v * v).astype(BF16), hm_ref[...])
            y = v * lax.rsqrt(ms + EPS) * gain_ref[:, c * 2 * LANES:(c + 1) * 2 * LANES]
            normed += [y[:, :LANES], y[:, LANES:]]
        return normed

    def finish(sub, qkv, normed):
        rows = slice(sub * tm, (sub + 1) * tm)

        def roped(y):
            if not rope:
                return y
            rot = jnp.where(first_half, -pltpu.roll(y, LANES - ROPE_FREQS, 1),
                            pltpu.roll(y, ROPE_FREQS, 1))
            return y * cos_ref[rows, :] + rot * sin_ref[rows, :]

        nq = C_W // LANES
        for c in range(nq):
            cq_ref[rows, c * LANES:(c + 1) * LANES] = roped(normed[c]).astype(BF16)
        for c in range(nq):
            y = normed[nq + c]
            if emit_cache:
                yt = y.T
                for half in range(2):
                    cache_cols(nck_ref, sub, (c, half), yt[half * HEAD_DIM:(half + 1) * HEAD_DIM])
            ck_ref[rows, c * LANES:(c + 1) * LANES] = roped(y).astype(BF16)
        for c in range(DQ_W // LANES):
            dq_ref[rows, c * LANES:(c + 1) * LANES] = roped(normed[2 * nq + c]).astype(BF16)
        dk_chunks = []
        for c in range(DKV_DUP_W // LANES):
            y = normed[2 * nq + DQ_W // LANES + c]
            dk_chunks.append(y)
            dk_ref[rows, c * LANES:(c + 1) * LANES] = roped(y).astype(BF16)
        cv = qkv[:, ODD_NORMED_W:ODD_NORMED_W + C_W]
        dv = qkv[:, ODD_NORMED_W + C_W:]
        for v, vt_ref in ((cv, cvt_ref), (dv, dvt_ref)):
            for hh in range(vt_ref.shape[0]):
                vt_ref[hh, :LANES, rows] = v[:, hh * LANES:(hh + 1) * LANES].T.astype(BF16)
                vt_ref[hh, LANES:, rows] = pad
        if emit_cache:
            for hh in range(C_HEADS):
                cache_rows(ncv_ref, sub, hh)[...] = cv[:, hh * LANES:(hh + 1) * LANES]
            for j in range(D_KV_HEADS):
                cache_cols(ndk_ref, sub, (j,), dk_chunks[j].T[:HEAD_DIM])
                cache_cols(ndv_ref, sub, (j,), dv[:, j * LANES:(j + 1) * LANES].T[:HEAD_DIM])

    qkv = project(0)
    for sub in range(subs):
        normed = head_norms(qkv)
        nxt = project(sub + 1) if sub + 1 < subs else None
        finish(sub, qkv, normed)
        qkv = nxt


def _odd_in_call(x, mod, layer, row_of_batch, g, w_in, gains, head_mean, rope_tabs, tm, cache_n=None):
    emit_cache = cache_n is not None
    bsz, n, _ = x.shape
    rope = rope_tabs is not None
    subs = 2 if n % (2 * tm) == 0 else 1
    ts = subs * tm
    tok = lambda w: pl.BlockSpec((None, ts, w), lambda b, i: (b, i, 0))
    in_specs = [
        tok(D_MODEL),
        _mod_spec(layer, 0, row_of_batch),
        _mod_spec(layer, 1, row_of_batch),
        _const_spec((1, D_MODEL)),
        _const_spec((D_MODEL, ODD_W)),
        _const_spec((1, ODD_NORMED_W)),
        _const_spec((2 * LANES, 2 * LANES)),
    ]
    args = [x, mod, mod, g, w_in, gains, head_mean]
    if rope:
        in_specs += [pl.BlockSpec((ts, LANES), lambda b, i: (i, 0))] * 2
        args += list(rope_tabs)
    vt_spec = lambda heads: pl.BlockSpec((None, heads, V_ROWS, ts), lambda b, i: (b, 0, 0, i))
    vt_shape = lambda heads: jax.ShapeDtypeStruct((bsz, heads, V_ROWS, n), BF16)
    tok_shape = lambda w: jax.ShapeDtypeStruct((bsz, n, w), BF16)
    out_specs = [tok(C_W), tok(C_W), vt_spec(C_HEADS), tok(DQ_W), tok(DKV_DUP_W), vt_spec(D_KV_HEADS)]
    out_shape = [tok_shape(C_W), tok_shape(C_W), vt_shape(C_HEADS), tok_shape(DQ_W),
                 tok_shape(DKV_DUP_W), vt_shape(D_KV_HEADS)]
    if emit_cache:
        assert bsz == 1 and tm % cache_n == 0 and n % cache_n == 0
        reqs, rps = n // cache_n, ts // cache_n
        for lead in ((C_HEADS, 2), None, (D_KV_HEADS,), (D_KV_HEADS,)):
            if lead is None:
                out_specs.append(pl.BlockSpec((None, ts * C_HEADS, LANES), lambda b, i: (b, i, 0)))
                out_shape.append(jax.ShapeDtypeStruct((bsz, n * C_HEADS, LANES), F32))
            else:
                zeros = (0,) * (len(lead) + 2)
                out_specs.append(pl.BlockSpec((rps,) + lead + (HEAD_DIM, cache_n),
                                              lambda b, i, zeros=zeros: (i,) + zeros))
                out_shape.append(jax.ShapeDtypeStruct((reqs,) + lead + (HEAD_DIM, cache_n), F32))
    return pl.pallas_call(
        functools.partial(_odd_in_kernel, rope=rope, emit_cache=emit_cache, tm=tm),
        grid=(bsz, n // ts),
        in_specs=in_specs,
        out_specs=out_specs,
        out_shape=out_shape,
        compiler_params=_params(2),
        name="odd_in",
    )(*args)


QK_AHEAD = 3


def _diff_attn_kernel(*refs, lam_init, tq, kc):
    n_src = (len(refs) - 4) // 2
    q_ref, (lamv_ref, g_ref, o_ref) = refs[0], refs[-3:]
    k_refs, vt_refs = refs[1:1 + n_src], refs[1 + n_src:1 + 2 * n_src]
    chunks = [(src, off) for src in range(n_src) for off in range(0, k_refs[src].shape[0], kc)]
    n_chunks = len(chunks)
    tiles = q_ref.shape[0] // tq
    heads = q_ref.shape[1] // LANES
    lane = lax.broadcasted_iota(jnp.int32, (tq, LANES), 1)
    lv = lamv_ref[...]
    lam = (jnp.exp(jnp.sum(lv[0:1] * lv[1:2], keepdims=True))
           - jnp.exp(jnp.sum(lv[2:3] * lv[3:4], keepdims=True)) + lam_init)

    qqs = {}

    def score(item):
        h, t, c = item
        cols = slice(h * LANES, (h + 1) * LANES)
        if (h, t) not in qqs:
            q = q_ref[t * tq:(t + 1) * tq, cols]
            zero = jnp.zeros_like(q)
            qqs[h, t] = jnp.concatenate([jnp.where(lane < HEAD_DIM, q, zero),
                                         jnp.where(lane >= HEAD_DIM, q, zero)], axis=0)
        src, off = chunks[c]
        return _dot_nt(k_refs[src][off:off + kc, cols], qqs[h, t])

    def finish(h, t, acc, l):
        inv = 1.0 / l
        ot = acc[:, :tq] * inv[:, :tq] - acc[:, tq:] * (inv[:, tq:] * lam)
        o = ot.T
        ms = jnp.mean(o * o, axis=-1, keepdims=True)
        y = o * lax.rsqrt(ms + EPS) * g_ref[...] * (1.0 - lam_init)
        o_ref[t * tq:(t + 1) * tq, h * LANES:(h + 1) * LANES] = y.astype(BF16)

    items = [(h, t, c) for h in range(heads) for t in range(tiles) for c in range(n_chunks)]
    ahead = [score(it) for it in items[:QK_AHEAD]]
    m = acc = None
    for idx, (h, t, c) in enumerate(items):
        s = ahead.pop(0)
        if idx + QK_AHEAD < len(items):
            ahead.append(score(items[idx + QK_AHEAD]))
        cm = jnp.max(s, axis=0, keepdims=True)
        m_new = cm if c == 0 else jnp.maximum(m, cm)
        src, off = chunks[c]
        e = jnp.exp2(s - m_new)
        lsum = jnp.sum(e, axis=0, keepdims=True)
        pv = _dot(vt_refs[src][h, :LANES, off:off + kc], e.astype(BF16))
        if c == 0:
            acc, l = pv, lsum
        else:
            alpha = jnp.exp2(m - m_new)
            acc, l = acc * alpha + pv, l * alpha + lsum
        m = m_new
        if c == n_chunks - 1:
            finish(h, t, acc, l)


def _diff_attn_call(q, ks, vts, lamv, subln_g, lam_init, tq, tiles, heads, flat_values=False):
    bsz, n, _ = q.shape
    kc = next(c for c in (512, 256, 128) if all(k.shape[1] % c == 0 for k in ks))
    hw = heads * LANES
    k_specs = [pl.BlockSpec((None, k.shape[1], hw), lambda b, h, i: (b, 0, h)) for k in ks]
    vt_index = (lambda b, h, i: (0, h, 0, b)) if flat_values else (lambda b, h, i: (b, h, 0, 0))
    vt_specs = [pl.BlockSpec((None, heads, V_ROWS, k.shape[1]), vt_index) for k in ks]
    return pl.pallas_call(
        functools.partial(_diff_attn_kernel, lam_init=lam_init, tq=tq, kc=kc),
        grid=(bsz, C_HEADS // heads, n // (tq * tiles)),
        in_specs=[pl.BlockSpec((None, tiles * tq, hw), lambda b, h, i: (b, i, h))]
        + k_specs + vt_specs + [_const_spec((4, HEAD_DIM)), _const_spec((1, LANES))],
        out_specs=pl.BlockSpec((None, tiles * tq, hw), lambda b, h, i: (b, i, h)),
        out_shape=jax.ShapeDtypeStruct((bsz, n, C_W), BF16),
        compiler_params=_params(3),
        name="diff_attn",
    )(q, *ks, *vts, lamv, subln_g)


def _sink_attn_kernel(*refs, windowed, tq, nblk):
    if windowed:
        (q_ref, ck_ref, kp_ref, km_ref, kn_ref, cvt_ref, vp_ref, vm_ref, vn_ref,
         bias_ref, sink_ref, o_ref) = refs
        kband = jnp.concatenate([kp_ref[...], km_ref[...], kn_ref[...]], axis=0)
        vtband = jnp.concatenate([vp_ref[...], vm_ref[...], vn_ref[...]], axis=1)
    else:
        q_ref, k_ref, vt_ref, sink_ref, o_ref = refs
    subs = q_ref.shape[0] // tq
    groups = q_ref.shape[1] // (2 * LANES)
    lane = lax.broadcasted_iota(jnp.int32, (tq, LANES), 1)
    lo = lane < HEAD_DIM

    def scores_of(stream):
        g, sub = stream
        qd = q_ref[sub * tq:(sub + 1) * tq, g * 2 * LANES:(g + 1) * 2 * LANES]
        parts = []
        for c in range(2):
            ch = qd[:, c * LANES:(c + 1) * LANES]
            zero = jnp.zeros_like(ch)
            parts += [jnp.where(lo, ch, zero), jnp.where(lo, zero, ch)]
        qq = jnp.concatenate(parts, axis=0)
        if not windowed:
            return [_dot_nt(k_ref[:, g * LANES:(g + 1) * LANES], qq)]
        blk = pl.program_id(2) * subs + sub
        variant = jnp.where(blk == 0, 0, jnp.where(blk == nblk - 1, 2, 1))
        band = _dot_nt(kband[sub * tq:(sub + 3) * tq, :], qq) + bias_ref[variant]
        return [_dot_nt(ck_ref[...], qq), band]

    def finish(stream, scores):
        g, sub = stream
        values = [cvt_ref[...], vtband[:, sub * tq:(sub + 3) * tq]] if windowed else [vt_ref[g]]
        sink = sink_ref[g]
        m = sink
        for s in scores:
            m = jnp.maximum(m, jnp.max(s, axis=0, keepdims=True))
        acc = None
        for s, vt in zip(scores, values):
            pv = _dot(vt, jnp.exp2(s - m).astype(BF16))
            acc = pv if acc is None else acc + pv
        inv = 1.0 / (acc[LANES:LANES + 1, :] + jnp.exp2(sink - m))
        ot = acc[:LANES, :] * inv
        for c in range(2):
            o0 = ot[:, (2 * c) * tq:(2 * c + 1) * tq].T
            o1 = ot[:, (2 * c + 1) * tq:(2 * c + 2) * tq].T
            cols = slice((2 * g + c) * LANES, (2 * g + c + 1) * LANES)
            o_ref[sub * tq:(sub + 1) * tq, cols] = jnp.where(lo, o0, o1).astype(BF16)

    streams = [(g, sub) for g in range(groups) for sub in range(subs)]
    ahead = [scores_of(st) for st in streams[:QK_AHEAD]]
    for idx, st in enumerate(streams):
        cur = ahead.pop(0)
        if idx + QK_AHEAD < len(streams):
            ahead.append(scores_of(streams[idx + QK_AHEAD]))
        finish(st, cur)


def _window_bias(n, tq):
    nblk = n // tq
    kk = np.arange(3 * tq)[:, None]
    qi = np.arange(tq)[None, :]
    band_ok = np.abs(qi + WINDOW - kk) <= WINDOW
    out = []
    for blk in (0, 1, nblk - 1):
        jpos = blk * tq - WINDOW + kk
        ok = band_ok & (jpos >= 0) & (jpos < n)
        out.append(np.tile(np.where(ok, 0.0, NEG_INF), (1, D_GROUP)))
    return jnp.asarray(np.stack(out), F32)


def _sink_attn_call(q, k, vt, sink_rows, ctx_k=None, ctx_vt=None, flat_values=False):
    bsz, n, _ = q.shape
    windowed = ctx_k is not None
    tq = WINDOW if windowed else n
    nblk = n // tq
    subs = next(s for s in (4, 2, 1) if nblk % s == 0) if windowed else 1
    steps = nblk // subs
    groups = 1 if windowed else D_KV_HEADS
    q_spec = pl.BlockSpec((None, subs * tq, groups * 2 * LANES), lambda b, j, i: (b, i, j))
    sink_spec = pl.BlockSpec((groups, 1, D_GROUP * tq), lambda b, j, i: (j, 0, 0))
    if windowed:
        assert nblk >= 3, "first / interior / last mask variants need three query blocks"
        n_ctx = ctx_k.shape[1]
        prev = lambda i: jnp.maximum(i * subs - 1, 0)
        nxt = lambda i: jnp.minimum((i + 1) * subs, nblk - 1)
        in_specs = [
            q_spec,
            pl.BlockSpec((None, n_ctx, LANES), lambda b, j, i: (b, 0, j)),
            pl.BlockSpec((None, tq, LANES), lambda b, j, i: (b, prev(i), j)),
            pl.BlockSpec((None, subs * tq, LANES), lambda b, j, i: (b, i, j)),
            pl.BlockSpec((None, tq, LANES), lambda b, j, i: (b, nxt(i), j)),
            pl.BlockSpec((None, None, V_ROWS, n_ctx), lambda b, j, i: (b, j, 0, 0)),
            pl.BlockSpec((None, None, V_ROWS, tq), lambda b, j, i: (b, j, 0, prev(i))),
            pl.BlockSpec((None, None, V_ROWS, subs * tq), lambda b, j, i: (b, j, 0, i)),
            pl.BlockSpec((None, None, V_ROWS, tq), lambda b, j, i: (b, j, 0, nxt(i))),
            _const_spec((3, 3 * tq, D_GROUP * tq)),
            sink_spec,
        ]
        args = (q, ctx_k, k, k, k, ctx_vt, vt, vt, vt, _window_bias(n, tq), sink_rows)
    else:
        in_specs = [
            q_spec,
            pl.BlockSpec((None, n, groups * LANES), lambda b, j, i: (b, 0, j)),
            pl.BlockSpec((None, groups, V_ROWS, n),
                         (lambda b, j, i: (0, j, 0, b)) if flat_values else (lambda b, j, i: (b, j, 0, 0))),
            sink_spec,
        ]
        args = (q, k, vt, sink_rows)
    return pl.pallas_call(
        functools.partial(_sink_attn_kernel, windowed=windowed, tq=tq, nblk=nblk),
        grid=(bsz, D_KV_HEADS // groups, steps),
        in_specs=in_specs,
        out_specs=q_spec,
        out_shape=jax.ShapeDtypeStruct((bsz, n, DQ_W), BF16),
        compiler_params=_params(3),
        name="sink_attn",
    )(*args)


def _dup_heads(a):
    lead = a.shape[:-1]
    a = a.reshape(lead + (D_KV_HEADS, 1, HEAD_DIM))
    return jnp.broadcast_to(a, lead + (D_KV_HEADS, 2, HEAD_DIM)).reshape(lead + (DKV_DUP_W,))


def _odd_weights(w_in_odd, c_qn_g, c_kn_g, d_qn_g, d_kn_g):
    cq, ck, cv, dq, dk, dv = jnp.split(
        w_in_odd, (C_W, 2 * C_W, 3 * C_W, 3 * C_W + DQ_W, 3 * C_W + DQ_W + DKV_W), axis=-1)
    w = jnp.concatenate([cq, ck, dq, _dup_heads(dk), cv, _dup_heads(dv)], axis=-1).astype(BF16)
    gains = jnp.concatenate([
        jnp.tile(c_qn_g * (ATTN_SCALE * LOG2E), C_W // HEAD_DIM),
        jnp.tile(c_kn_g, C_W // HEAD_DIM),
        jnp.tile(d_qn_g * (ATTN_SCALE * LOG2E), DQ_W // HEAD_DIM),
        jnp.tile(d_kn_g, DKV_DUP_W // HEAD_DIM),
    ])[None, :]
    return w, gains


def _value_rows(v):
    bsz, nk, w = v.shape
    vt = v.reshape(bsz, nk, w // LANES, LANES).transpose(0, 2, 3, 1)
    pad = jnp.zeros((bsz, w // LANES, V_ROWS - LANES, nk), v.dtype).at[:, :, 0, :].set(1)
    return jnp.concatenate([vt, pad], axis=2)


def _rope_tables(n):
    rows = n // GRID_W
    row = np.repeat(np.arange(rows), GRID_W).astype(np.float64)
    col = np.tile(np.arange(GRID_W), rows).astype(np.float64)
    freqs = ROPE_BASE ** (-np.arange(ROPE_FREQS, dtype=np.float64) / ROPE_FREQS)
    ang = np.stack([row[:, None] * freqs, col[:, None] * freqs], axis=1)
    ang = np.concatenate([ang, ang], axis=-1).reshape(n, HEAD_DIM)
    ang = np.concatenate([ang, ang], axis=-1)
    return jnp.asarray(np.cos(ang), F32), jnp.asarray(np.sin(ang), F32)


def _token_tile(n):
    return min(n, 512)


def _run_group(x, mod, row_of_batch, wts, caches):
    bsz, n, _ = x.shape
    decode = caches is not None
    tm = _token_tile(n)
    depth = wts["norm1_g"].shape[0]
    new_cache = []
    for l in range(depth):
        g1n = wts["norm1_g"][l][None, :]
        g2n = wts["norm2_g"][l][None, :]
        if l % 2 == 0:
            e = l // 2
            a, z = _even_in_call(x, mod, l, row_of_batch, g1n, e, wts["w_in_even"], wts["w_pool"],
                                 wts["pool_scale"][e][None, :], wts["dftc"], wts["w_fft"], tm)
            y = _dft_two_stage_call(z) if n > DFT_DIRECT_MAX else _dft_direct_call(z)
            p1, p2, w_out, out_idx = a, y, wts["w_out_even"], e
        else:
            o = l // 2
            lam_init = _lam_init(l)
            rope_tabs = _rope_tables(n) if decode else None
            xo = x if decode else x.reshape(1, bsz * n, D_MODEL)
            outs = _odd_in_call(xo, mod, l, row_of_batch, g1n, wts["w_in_odd"][o], wts["odd_gains"][o],
                                wts["head_mean"], rope_tabs, _token_tile(xo.shape[1]),
                                None if decode else n)
            if not decode:
                outs = [a if a.ndim == 4 else a.reshape(bsz, n, a.shape[-1]) for a in outs[:6]] + list(outs[6:])
            cq, ck, cvt, dq, dkk, dvt = outs[:6]
            if decode:
                c_k, c_v, d_k, d_v = caches
                lc = c_k.shape[2]
                ks = [c_k[:, o].reshape(bsz, lc, C_W).astype(BF16), ck]
                vts = [_value_rows(c_v[:, o].reshape(bsz, lc, C_W).astype(BF16)), cvt]
                ctx_k = _dup_heads(d_k[:, o].reshape(bsz, lc, DKV_W)).astype(BF16)
                ctx_vt = _value_rows(_dup_heads(d_v[:, o].reshape(bsz, lc, DKV_W)).astype(BF16))
                tq = min(n, 256)
            else:
                new_cache.append(outs[6:])
                ks, vts = [ck], [cvt]
                ctx_k = ctx_vt = None
                tq = n
            tiles = 2 if n % (2 * tq) == 0 else 1
            heads = 1 if decode else C_HEADS
            c_out = _diff_attn_call(cq, ks, vts, wts["lamv"][o],
                                    wts["c_subln_g"][o][None, :], lam_init, tq, tiles, heads, not decode)
            sink_tq = WINDOW if decode else n
            sink_rows = jnp.repeat(wts["d_sink"][o].reshape(D_KV_HEADS, D_GROUP) * LOG2E, sink_tq,
                                   axis=-1)[:, None, :]
            d_out = _sink_attn_call(dq, dkk, dvt, sink_rows, ctx_k, ctx_vt, not decode)
            p1, p2, w_out, out_idx = c_out, d_out, wts["w_out_odd"], o
        flat = (lambda a: a) if decode else (lambda a: a.reshape(1, bsz * n, a.shape[-1]))
        x = _out_mlp_call(flat(x), flat(p1), flat(p2), mod, l, row_of_batch, g2n, w_out, out_idx,
                          wts["w_mlp1"], wts["w_mlp2"],
                          tm if decode else _token_tile(bsz * n)).reshape(bsz, n, D_MODEL)
    return x, new_cache


def kernel(x_prompt, x_sample, c, cache_c_k, cache_c_v, cache_d_k, cache_d_v, c_ctx,
           norm1_g, norm2_g, w_ada, b_ada, w_in_even, w_pool, pool_scale, w_fft, w_out_even,
           w_in_odd, c_qn_g, c_kn_g, lam_q1, lam_k1, lam_q2, lam_k2, c_subln_g,
           d_qn_g, d_kn_g, d_sink, w_out_odd, w_mlp1, w_mlp2):
    depth = norm1_g.shape[0]
    n_odd = w_in_odd.shape[0]
    dec_b = c.shape[0]

    rows = -(-(1 + dec_b) // SUBLANES) * SUBLANES
    cond = jnp.zeros((rows, D_MODEL), F32).at[0].set(c_ctx).at[1:1 + dec_b].set(c)
    mod = _adaln_call(cond, w_ada, b_ada).reshape(depth, rows, 6, 1, D_MODEL)

    ic = np.arange(GROUP_W)
    ang = 2.0 * np.pi * np.outer(ic, ic) / GROUP_W
    dftc = _bf16_const(np.concatenate([np.cos(ang), -np.sin(ang)], axis=1))
    head_mean = _bf16_const(np.kron(np.eye(2 * LANES // HEAD_DIM), np.ones((HEAD_DIM, HEAD_DIM))) / HEAD_DIM)

    odd = [_odd_weights(w_in_odd[o], c_qn_g[o], c_kn_g[o], d_qn_g[o], d_kn_g[o]) for o in range(n_odd)]
    bf16 = lambda w: w.astype(BF16)
    wts = {
        "norm1_g": norm1_g, "norm2_g": norm2_g,
        "w_in_even": bf16(w_in_even), "w_pool": bf16(w_pool), "pool_scale": pool_scale,
        "w_fft": bf16(w_fft), "w_out_even": bf16(w_out_even), "dftc": dftc,
        "w_in_odd": [w for w, _ in odd], "odd_gains": [g for _, g in odd], "head_mean": head_mean,
        "lamv": jnp.stack([lam_q1, lam_k1, lam_q2, lam_k2], axis=1),
        "c_subln_g": c_subln_g, "d_sink": d_sink, "w_out_odd": bf16(w_out_odd),
        "w_mlp1": bf16(w_mlp1), "w_mlp2": bf16(w_mlp2),
    }

    y_prompt, new_cache = _run_group(x_prompt, mod, lambda b: 0, wts, None)
    y_sample, _ = _run_group(x_sample, mod, lambda b: 1 + b, wts,
                             (cache_c_k, cache_c_v, cache_d_k, cache_d_v))

    bsz, n, _ = x_prompt.shape
    new_c_k = jnp.stack([nc[0].transpose(0, 4, 1, 2, 3) for nc in new_cache], axis=1)
    new_c_v = jnp.stack([nc[1].reshape(bsz, n, C_HEADS, 2 * HEAD_DIM) for nc in new_cache], axis=1)
    new_d_k = jnp.stack([nc[2].transpose(0, 3, 1, 2) for nc in new_cache], axis=1)
    new_d_v = jnp.stack([nc[3].transpose(0, 3, 1, 2) for nc in new_cache], axis=1)
    return (y_prompt, y_sample, new_c_k, new_c_v, new_d_k, new_d_v)
```

```python
import functools
import math

import numpy as np
import jax
import jax.numpy as jnp
from jax import lax
from jax.experimental import pallas as pl
from jax.experimental.pallas import tpu as pltpu

F32 = jnp.float32
BF16 = jnp.bfloat16

D_MODEL = 1024
HEAD_DIM = 64
LANES = 128
SUBLANES = 8
GRID_W = 64
ROPE_FREQS = HEAD_DIM // 4
ROPE_BASE = 10000.0
EPS = 1e-6
NEG_INF = -1e30
WINDOW = 128
A_WIDTH = D_MODEL // 2
B_WIDTH = D_MODEL // 2
POOL_WINDOWS = (2, 4, 8, 16)
POOL_HALO = 8
N_GROUPS = 4
GROUP_W = A_WIDTH // N_GROUPS
C_HEADS = 4
D_HEADS = 8
D_KV_HEADS = 2
D_GROUP = D_HEADS // D_KV_HEADS
C_W = C_HEADS * 2 * HEAD_DIM
DQ_W = D_HEADS * HEAD_DIM
DKV_W = D_KV_HEADS * HEAD_DIM
DKV_DUP_W = 2 * DKV_W
D_FF = 4 * D_MODEL
ATTN_SCALE = HEAD_DIM ** -0.5
LOG2E = math.log2(math.e)
V_ROWS = LANES + 16
ODD_NORMED_W = C_W + C_W + DQ_W + DKV_DUP_W
ODD_W = ODD_NORMED_W + C_W + DKV_DUP_W
VMEM_LIMIT = 56 * 1024 * 1024


def _lam_init(layer):
    return 0.8 - 0.6 * math.exp(-0.3 * layer)


def _params(n_axes):
    return pltpu.CompilerParams(dimension_semantics=("arbitrary",) * n_axes,
                                vmem_limit_bytes=VMEM_LIMIT)


def _const_spec(shape):
    nd = len(shape)
    return pl.BlockSpec(shape, lambda *_: (0,) * nd, pipeline_mode=pl.Buffered(1))


def _stacked_spec(shape, idx):
    nd = len(shape)
    return pl.BlockSpec((None,) + tuple(shape), lambda *_: (idx,) + (0,) * nd, pipeline_mode=pl.Buffered(1))


def _mod_spec(layer, which, row_of_batch):
    return pl.BlockSpec((None, None, None, 1, D_MODEL),
                        lambda b, i: (layer, row_of_batch(b), which, 0, 0))


def _rms_mod(x, g, sc, sh):
    ms = jnp.mean(x * x, axis=-1, keepdims=True)
    return (x * lax.rsqrt(ms + EPS) * g) * (1.0 + sc) + sh


def _bf16_const(a):
    return jnp.asarray(a, F32).astype(BF16)


def _dot_nt(a, b):
    return lax.dot_general(a, b, (((1,), (1,)), ((), ())), preferred_element_type=F32)


def _dot(a, b):
    return jnp.dot(a, b, preferred_element_type=F32)


def _adaln_kernel(cond_ref, w_ref, b_ref, o_ref):
    cnd = cond_ref[...]
    s = (cnd * jax.nn.sigmoid(cnd)).astype(BF16)
    o_ref[...] = _dot(s, w_ref[...].astype(BF16)) + b_ref[...]


def _adaln_call(cond, w_ada, b_ada):
    depth = w_ada.shape[0]
    rows = cond.shape[0]
    tn = 1536
    return pl.pallas_call(
        _adaln_kernel,
        grid=(depth, 6 * D_MODEL // tn),
        in_specs=[
            pl.BlockSpec((rows, D_MODEL), lambda l, j: (0, 0)),
            pl.BlockSpec((None, D_MODEL, tn), lambda l, j: (l, 0, j)),
            pl.BlockSpec((None, 1, tn), lambda l, j: (l, 0, j)),
        ],
        out_specs=pl.BlockSpec((None, rows, tn), lambda l, j: (l, 0, j)),
        out_shape=jax.ShapeDtypeStruct((depth, rows, 6 * D_MODEL), F32),
        compiler_params=_params(2),
        name="adaln",
    )(cond, w_ada, b_ada.reshape(depth, 1, 6 * D_MODEL))


def _even_in_kernel(x_ref, xp_ref, xn_ref, sh_ref, sc_ref, g_ref, win_ref, wpool_ref,
                    pscale_ref, dftc_ref, wfft_ref, a_ref, z_ref, *, tm, n):
    i = pl.program_id(1)
    n_tiles = n // tm
    g, sc, sh = g_ref[...], sc_ref[...], sh_ref[...]
    seqs = x_ref.shape[0]
    h = _rms_mod(x_ref[...].reshape(seqs * tm, D_MODEL), g, sc, sh).astype(BF16)
    u = _dot(h, win_ref[...])
    xh = jnp.concatenate([xp_ref[...], xn_ref[...]], axis=0)
    hh = _rms_mod(xh, g, sc, sh).astype(BF16)
    uh = _dot(hh, win_ref[:, :A_WIDTH])
    has_prev = (i > 0).astype(F32)
    has_next = (i < n_tiles - 1).astype(F32)
    rows = tm + 2 * POOL_HALO

    for gi in range(N_GROUPS):
        ub = u[:, A_WIDTH + gi * GROUP_W:A_WIDTH + (gi + 1) * GROUP_W].astype(BF16)
        cs = _dot(ub, dftc_ref[...])
        zr = _dot(cs[:, :GROUP_W].astype(BF16), wfft_ref[gi])
        zi = _dot(cs[:, GROUP_W:].astype(BF16), wfft_ref[gi])
        for q in range(seqs):
            z_ref[q, :, gi * GROUP_W:(gi + 1) * GROUP_W] = zr[q * tm:(q + 1) * tm].astype(BF16)
            z_ref[q, :, B_WIDTH + gi * GROUP_W:B_WIDTH + (gi + 1) * GROUP_W] = zi[q * tm:(q + 1) * tm].astype(BF16)

    t = i * tm + lax.broadcasted_iota(jnp.int32, (tm, 1), 0)
    for q, (gi, w) in ((q, gw) for q in range(seqs) for gw in enumerate(POOL_WINDOWS)):
        cols = slice(gi * GROUP_W, (gi + 1) * GROUP_W)
        uq = u[q * tm:(q + 1) * tm, cols]
        ext = jnp.concatenate([uh[0:POOL_HALO, cols] * has_prev, uq,
                               uh[POOL_HALO:, cols] * has_next], axis=0)
        s = ext
        span = 1
        while span < w:
            s = s + pltpu.roll(s, span, 0)
            span *= 2
        shift = w // 2 - 1
        s = pltpu.roll(s, rows - shift, 0) if shift else s
        acc = s[POOL_HALO:POOL_HALO + tm]
        lo = jnp.maximum(t - w // 2, 0)
        hi = jnp.minimum(t + w // 2, n)
        cnt = (hi - lo).astype(F32)
        pooled = acc / cnt - uq
        y = _dot(pooled.astype(BF16), wpool_ref[gi]) * pscale_ref[:, cols]
        a_ref[q, :, cols] = y.astype(BF16)


def _even_in_call(x, mod, layer, row_of_batch, g, e, w_in, w_pool, pool_scale, dftc, w_fft, tm, seqs):
    bsz, n, _ = x.shape
    assert seqs == 1 or tm == n
    hb = tm // POOL_HALO
    kern = functools.partial(_even_in_kernel, tm=tm, n=n)
    return pl.pallas_call(
        kern,
        grid=(bsz // seqs, n // tm),
        in_specs=[
            pl.BlockSpec((seqs, tm, D_MODEL), lambda b, i: (b, i, 0)),
            pl.BlockSpec((None, POOL_HALO, D_MODEL),
                         lambda b, i: (b * seqs, jnp.maximum(i * hb - 1, 0), 0)),
            pl.BlockSpec((None, POOL_HALO, D_MODEL),
                         lambda b, i: (b * seqs, jnp.minimum((i + 1) * hb, n // POOL_HALO - 1), 0)),
            _mod_spec(layer, 0, row_of_batch),
            _mod_spec(layer, 1, row_of_batch),
            _const_spec((1, D_MODEL)),
            _stacked_spec((D_MODEL, D_MODEL), e),
            _stacked_spec((N_GROUPS, GROUP_W, GROUP_W), e),
            _const_spec((1, A_WIDTH)),
            _const_spec((GROUP_W, 2 * GROUP_W)),
            _stacked_spec((N_GROUPS, GROUP_W, GROUP_W), e),
        ],
        out_specs=[
            pl.BlockSpec((seqs, tm, A_WIDTH), lambda b, i: (b, i, 0)),
            pl.BlockSpec((seqs, tm, 2 * B_WIDTH), lambda b, i: (b, i, 0)),
        ],
        out_shape=[
            jax.ShapeDtypeStruct((bsz, n, A_WIDTH), BF16),
            jax.ShapeDtypeStruct((bsz, n, 2 * B_WIDTH), BF16),
        ],
        compiler_params=_params(2),
        name="even_in",
    )(x, x, x, mod, mod, g, w_in, w_pool, pool_scale, dftc, w_fft)


def _dft_direct_kernel(z_ref, m_ref, y_ref, *, scale):
    z = z_ref[...]
    rhs = jnp.concatenate([z[:, :B_WIDTH], z[:, B_WIDTH:]], axis=0)
    y_ref[...] = (_dot(m_ref[...], rhs) * scale).astype(BF16)


def _dft_direct_call(z):
    bsz, n, _ = z.shape
    k = np.arange(n)
    ang = 2.0 * np.pi * np.outer(k, k) / n
    m = _bf16_const(np.concatenate([np.cos(ang), np.sin(ang)], axis=1))
    kern = functools.partial(_dft_direct_kernel, scale=float((n * GROUP_W) ** -0.5))
    return pl.pallas_call(
        kern,
        grid=(bsz,),
        in_specs=[pl.BlockSpec((None, n, 2 * B_WIDTH), lambda b: (b, 0, 0)),
                  _const_spec((n, 2 * n))],
        out_specs=pl.BlockSpec((None, n, B_WIDTH), lambda b: (b, 0, 0)),
        out_shape=jax.ShapeDtypeStruct((bsz, n, B_WIDTH), BF16),
        compiler_params=_params(1),
        name="dft_direct",
    )(z, m)


DFT_ROWS = 16
DFT_DIRECT_MAX = 256
DFT_PITCH = 24


def _pitch_rows(x):
    g, _, w = x.shape
    pad = jnp.zeros((g, DFT_PITCH - DFT_ROWS, w), x.dtype)
    return jnp.concatenate([x, pad], axis=1).reshape(g * DFT_PITCH, w)


def _dft_stage1_kernel(z_ref, m_ref, twr_ref, twi_ref, o_ref, zs_ref, os_ref, *, na):
    nt = 2 * B_WIDTH // LANES

    @pl.when((pl.program_id(0) == 0) & (pl.program_id(1) == 0))
    def _():
        os_ref[...] = jnp.zeros_like(os_ref)

    zf = _pitch_rows(z_ref[...].astype(F32))
    for k in range(nt):
        zs_ref[k] = zf[:, k * LANES:(k + 1) * LANES]
    for bi in range(DFT_ROWS):
        rows = pl.ds(bi, na, stride=DFT_PITCH)
        zb = [zs_ref[k, rows, :].astype(BF16) for k in range(nt)]
        rhs = jnp.concatenate([jnp.concatenate(zb[:nt // 2], axis=1),
                               jnp.concatenate(zb[nt // 2:], axis=1)], axis=0)
        r = _dot(m_ref[...], rhs)
        br, bim = r[:na], r[na:]
        tr, ti = twr_ref[bi], twi_ref[bi]
        out = (br * tr - bim * ti, br * ti + bim * tr)
        for k in range(nt):
            half, kk = divmod(k, nt // 2)
            os_ref[k, rows, :] = out[half][:, kk * LANES:(kk + 1) * LANES]
    of = jnp.concatenate([os_ref[k] for k in range(nt)], axis=1)
    o_ref[...] = of.reshape(na, DFT_PITCH, 2 * B_WIDTH)[:, :DFT_ROWS].astype(BF16)


def _dft_stage2_kernel(b_ref, m_ref, y_ref, ys_ref, *, scale):
    nb = b_ref.shape[1]

    @pl.when((pl.program_id(0) == 0) & (pl.program_id(1) == 0))
    def _():
        ys_ref[...] = jnp.zeros_like(ys_ref)

    for di in range(DFT_ROWS):
        bb = b_ref[di]
        rhs = jnp.concatenate([bb[:, :B_WIDTH], bb[:, B_WIDTH:]], axis=0)
        y = _dot(m_ref[...], rhs) * scale
        for k in range(B_WIDTH // LANES):
            ys_ref[k, pl.ds(di, nb, stride=DFT_PITCH), :] = y[:, k * LANES:(k + 1) * LANES]
    yf = jnp.concatenate([ys_ref[k] for k in range(B_WIDTH // LANES)], axis=1)
    y_ref[...] = yf.reshape(nb, DFT_PITCH, B_WIDTH)[:, :DFT_ROWS].astype(BF16)


def _dft_two_stage_call(z):
    bsz, n, _ = z.shape
    na = 1 << (int(math.log2(n)) // 2)
    nb = n // na
    ia = np.arange(na)
    ib = np.arange(nb)
    ang_a = 2.0 * np.pi * np.outer(ia, ia) / na
    fr, fi = np.cos(ang_a), -np.sin(ang_a)
    m1 = _bf16_const(np.block([[fr, -fi], [fi, fr]]))
    ang_t = 2.0 * np.pi * np.outer(ib, ia) / n
    twr = jnp.asarray(np.cos(ang_t)[:, :, None], F32)
    twi = jnp.asarray(-np.sin(ang_t)[:, :, None], F32)
    ang_b = 2.0 * np.pi * np.outer(ib, ib) / nb
    m2 = _bf16_const(np.concatenate([np.cos(ang_b), np.sin(ang_b)], axis=1))

    s1 = pl.pallas_call(
        functools.partial(_dft_stage1_kernel, na=na),
        grid=(bsz, nb // DFT_ROWS),
        in_specs=[
            pl.BlockSpec((None, na, DFT_ROWS, 2 * B_WIDTH), lambda b, j: (b, 0, j, 0)),
            _const_spec((2 * na, 2 * na)),
            pl.BlockSpec((DFT_ROWS, na, 1), lambda b, j: (j, 0, 0)),
            pl.BlockSpec((DFT_ROWS, na, 1), lambda b, j: (j, 0, 0)),
        ],
        out_specs=pl.BlockSpec((None, na, DFT_ROWS, 2 * B_WIDTH), lambda b, j: (b, 0, j, 0)),
        out_shape=jax.ShapeDtypeStruct((bsz, na, nb, 2 * B_WIDTH), BF16),
        scratch_shapes=[pltpu.VMEM((2 * B_WIDTH // LANES, na * DFT_PITCH, LANES), F32)] * 2,
        compiler_params=_params(2),
        name="dft_stage1",
    )(z.reshape(bsz, na, nb, 2 * B_WIDTH), m1, twr, twi)
    y = pl.pallas_call(
        functools.partial(_dft_stage2_kernel, scale=float((n * GROUP_W) ** -0.5)),
        grid=(bsz, na // DFT_ROWS),
        in_specs=[
            pl.BlockSpec((None, DFT_ROWS, nb, 2 * B_WIDTH), lambda b, j: (b, j, 0, 0)),
            _const_spec((nb, 2 * nb)),
        ],
        out_specs=pl.BlockSpec((None, nb, DFT_ROWS, B_WIDTH), lambda b, j: (b, 0, j, 0)),
        out_shape=jax.ShapeDtypeStruct((bsz, nb, na, B_WIDTH), BF16),
        scratch_shapes=[pltpu.VMEM((B_WIDTH // LANES, nb * DFT_PITCH, LANES), F32)],
        compiler_params=_params(2),
        name="dft_stage2",
    )(s1, m2)
    return y.reshape(bsz, n, B_WIDTH)


FF_CHUNK = 1024


def _out_mlp_kernel(x_ref, p1_ref, p2_ref, g1_ref, sh_ref, sc_ref, g2_ref, ng_ref,
                    wout_ref, w1_ref, w2_ref, o_ref):
    half = p1_ref.shape[-1]
    mix = _dot(p1_ref[...], wout_ref[:half, :]) + _dot(p2_ref[...], wout_ref[half:, :])
    x1 = x_ref[...] + g1_ref[...] * mix
    h = _rms_mod(x1, ng_ref[...], sc_ref[...], sh_ref[...]).astype(BF16)
    acc = None
    for c in range(D_FF // FF_CHUNK):
        a = _dot(h, w1_ref[:, c * FF_CHUNK:(c + 1) * FF_CHUNK])
        a = jnp.square(jnp.maximum(a, 0.0)).astype(BF16)
        part = _dot(a, w2_ref[c * FF_CHUNK:(c + 1) * FF_CHUNK, :])
        acc = part if acc is None else acc + part
    o_ref[...] = x1 + g2_ref[...] * acc


def _out_mlp_call(x, p1, p2, mod, layer, row_of_batch, ng, w_out, out_idx, w1, w2, tm):
    bsz, n, _ = x.shape
    half = p1.shape[-1]
    tok = lambda w: pl.BlockSpec((None, tm, w), lambda b, i: (b, i, 0))
    return pl.pallas_call(
        _out_mlp_kernel,
        grid=(bsz, n // tm),
        in_specs=[
            tok(D_MODEL), tok(half), tok(half),
            _mod_spec(layer, 2, row_of_batch),
            _mod_spec(layer, 3, row_of_batch),
            _mod_spec(layer, 4, row_of_batch),
            _mod_spec(layer, 5, row_of_batch),
            _const_spec((1, D_MODEL)),
            _stacked_spec((2 * half, D_MODEL), out_idx),
            _stacked_spec((D_MODEL, D_FF), layer),
            _stacked_spec((D_FF, D_MODEL), layer),
        ],
        out_specs=tok(D_MODEL),
        out_shape=jax.ShapeDtypeStruct((bsz, n, D_MODEL), F32),
        compiler_params=_params(2),
        name="out_mlp",
    )(x, p1, p2, mod, mod, mod, mod, ng, w_out, w1, w2)


def _odd_in_kernel(*refs, rope, emit_cache, tm):
    x_ref, sh_ref, sc_ref, g_ref, win_ref, gain_ref, hm_ref = refs[:7]
    k = 7
    if rope:
        cos_ref, sin_ref = refs[k:k + 2]
        k += 2
    cq_ref, ck_ref, cvt_ref, dq_ref, dk_ref, dvt_ref = refs[k:k + 6]
    k += 6
    if emit_cache:
        nck_ref, ncv_ref, ndk_ref, ndv_ref = refs[k:k + 4]
    subs = x_ref.shape[0] // tm
    lane = lax.broadcasted_iota(jnp.int32, (tm, LANES), 1)
    first_half = (lane % (2 * ROPE_FREQS)) < ROPE_FREQS
    pad_row = lax.broadcasted_iota(jnp.int32, (V_ROWS - LANES, tm), 0)
    pad = jnp.where(pad_row == 0, 1.0, 0.0).astype(BF16)

    def cache_rows(ref, sub, r):
        rpt = ref.shape[0] // x_ref.shape[0]
        return ref.at[pl.ds(sub * tm * rpt + r, tm, stride=rpt), :]

    def cache_cols(ref, sub, lead, mat):
        n_req = ref.shape[-1]
        for r in range(tm // n_req):
            ref[(sub * (tm // n_req) + r,) + lead] = mat[:, r * n_req:(r + 1) * n_req]

    def project(sub):
        x = x_ref[sub * tm:(sub + 1) * tm, :]
        h = _rms_mod(x, g_ref[...], sc_ref[...], sh_ref[...]).astype(BF16)
        return _dot(h, win_ref[...])

    def head_norms(qkv):
        normed = []
        for c in range(ODD_NORMED_W // (2 * LANES)):
            v = qkv[:, c * 2 * LANES:(c + 1) * 2 * LANES]
            ms = _dot((v * v).astype(BF16), hm_ref[...])
            y = v * lax.rsqrt(ms + EPS) * gain_ref[:, c * 2 * LANES:(c + 1) * 2 * LANES]
            normed += [y[:, :LANES], y[:, LANES:]]
        return normed

    def finish(sub, qkv, normed):
        rows = slice(sub * tm, (sub + 1) * tm)

        def roped(y):
            if not rope:
                return y
            rot = jnp.where(first_half, -pltpu.roll(y, LANES - ROPE_FREQS, 1),
                            pltpu.roll(y, ROPE_FREQS, 1))
            return y * cos_ref[rows, :] + rot * sin_ref[rows, :]

        nq = C_W // LANES
        for c in range(nq):
            cq_ref[rows, c * LANES:(c + 1) * LANES] = roped(normed[c]).astype(BF16)
        for c in range(nq):
            y = normed[nq + c]
            if emit_cache:
                yt = y.T
                for half in range(2):
                    cache_cols(nck_ref, sub, (c, half), yt[half * HEAD_DIM:(half + 1) * HEAD_DIM])
            ck_ref[rows, c * LANES:(c + 1) * LANES] = roped(y).astype(BF16)
        for c in range(DQ_W // LANES):
            dq_ref[rows, c * LANES:(c + 1) * LANES] = roped(normed[2 * nq + c]).astype(BF16)
        dk_chunks = []
        for c in range(DKV_DUP_W // LANES):
            y = normed[2 * nq + DQ_W // LANES + c]
            dk_chunks.append(y)
            dk_ref[rows, c * LANES:(c + 1) * LANES] = roped(y).astype(BF16)
        cv = qkv[:, ODD_NORMED_W:ODD_NORMED_W + C_W]
        dv = qkv[:, ODD_NORMED_W + C_W:]
        for v, vt_ref in ((cv, cvt_ref), (dv, dvt_ref)):
            for hh in range(vt_ref.shape[0]):
                vt_ref[hh, :LANES, rows] = v[:, hh * LANES:(hh + 1) * LANES].T.astype(BF16)
                vt_ref[hh, LANES:, rows] = pad
        if emit_cache:
            for hh in range(C_HEADS):
                cache_rows(ncv_ref, sub, hh)[...] = cv[:, hh * LANES:(hh + 1) * LANES]
            for j in range(D_KV_HEADS):
                cache_cols(ndk_ref, sub, (j,), dk_chunks[j].T[:HEAD_DIM])
                cache_cols(ndv_ref, sub, (j,), dv[:, j * LANES:(j + 1) * LANES].T[:HEAD_DIM])

    qkv = project(0)
    for sub in range(subs):
        normed = head_norms(qkv)
        nxt = project(sub + 1) if sub + 1 < subs else None
        finish(sub, qkv, normed)
        qkv = nxt


def _odd_in_call(x, mod, layer, row_of_batch, g, w_in, gains, head_mean, rope_tabs, tm, cache_n=None):
    emit_cache = cache_n is not None
    bsz, n, _ = x.shape
    rope = rope_tabs is not None
    subs = 2 if n % (2 * tm) == 0 else 1
    ts = subs * tm
    tok = lambda w: pl.BlockSpec((None, ts, w), lambda b, i: (b, i, 0))
    in_specs = [
        tok(D_MODEL),
        _mod_spec(layer, 0, row_of_batch),
        _mod_spec(layer, 1, row_of_batch),
        _const_spec((1, D_MODEL)),
        _const_spec((D_MODEL, ODD_W)),
        _const_spec((1, ODD_NORMED_W)),
        _const_spec((2 * LANES, 2 * LANES)),
    ]
    args = [x, mod, mod, g, w_in, gains, head_mean]
    if rope:
        in_specs += [pl.BlockSpec((ts, LANES), lambda b, i: (i, 0))] * 2
        args += list(rope_tabs)
    vt_spec = lambda heads: pl.BlockSpec((None, heads, V_ROWS, ts), lambda b, i: (b, 0, 0, i))
    vt_shape = lambda heads: jax.ShapeDtypeStruct((bsz, heads, V_ROWS, n), BF16)
    tok_shape = lambda w: jax.ShapeDtypeStruct((bsz, n, w), BF16)
    out_specs = [tok(C_W), tok(C_W), vt_spec(C_HEADS), tok(DQ_W), tok(DKV_DUP_W), vt_spec(D_KV_HEADS)]
    out_shape = [tok_shape(C_W), tok_shape(C_W), vt_shape(C_HEADS), tok_shape(DQ_W),
                 tok_shape(DKV_DUP_W), vt_shape(D_KV_HEADS)]
    if emit_cache:
        assert bsz == 1 and tm % cache_n == 0 and n % cache_n == 0
        reqs, rps = n // cache_n, ts // cache_n
        for lead in ((C_HEADS, 2), None, (D_KV_HEADS,), (D_KV_HEADS,)):
            if lead is None:
                out_specs.append(pl.BlockSpec((None, ts * C_HEADS, LANES), lambda b, i: (b, i, 0)))
                out_shape.append(jax.ShapeDtypeStruct((bsz, n * C_HEADS, LANES), F32))
            else:
                zeros = (0,) * (len(lead) + 2)
                out_specs.append(pl.BlockSpec((rps,) + lead + (HEAD_DIM, cache_n),
                                              lambda b, i, zeros=zeros: (i,) + zeros))
                out_shape.append(jax.ShapeDtypeStruct((reqs,) + lead + (HEAD_DIM, cache_n), F32))
    return pl.pallas_call(
        functools.partial(_odd_in_kernel, rope=rope, emit_cache=emit_cache, tm=tm),
        grid=(bsz, n // ts),
        in_specs=in_specs,
        out_specs=out_specs,
        out_shape=out_shape,
        compiler_params=_params(2),
        name="odd_in",
    )(*args)


QK_AHEAD = 3


def _diff_attn_kernel(*refs, lam_init, tq, kc):
    n_src = (len(refs) - 4) // 2
    q_ref, (lamv_ref, g_ref, o_ref) = refs[0], refs[-3:]
    k_refs, vt_refs = refs[1:1 + n_src], refs[1 + n_src:1 + 2 * n_src]
    chunks = [(src, off) for src in range(n_src) for off in range(0, k_refs[src].shape[0], kc)]
    n_chunks = len(chunks)
    tiles = q_ref.shape[0] // tq
    heads = q_ref.shape[1] // LANES
    lane = lax.broadcasted_iota(jnp.int32, (tq, LANES), 1)
    lv = lamv_ref[...]
    lam = (jnp.exp(jnp.sum(lv[0:1] * lv[1:2], keepdims=True))
           - jnp.exp(jnp.sum(lv[2:3] * lv[3:4], keepdims=True)) + lam_init)

    qqs = {}

    def score(item):
        h, t, c = item
        cols = slice(h * LANES, (h + 1) * LANES)
        if (h, t) not in qqs:
            q = q_ref[t * tq:(t + 1) * tq, cols]
            zero = jnp.zeros_like(q)
            qqs[h, t] = jnp.concatenate([jnp.where(lane < HEAD_DIM, q, zero),
                                         jnp.where(lane >= HEAD_DIM, q, zero)], axis=0)
        src, off = chunks[c]
        return _dot_nt(k_refs[src][off:off + kc, cols], qqs[h, t])

    def finish(h, t, acc, l):
        inv = 1.0 / l
        ot = acc[:, :tq] * inv[:, :tq] - acc[:, tq:] * (inv[:, tq:] * lam)
        o = ot.T
        ms = jnp.mean(o * o, axis=-1, keepdims=True)
        y = o * lax.rsqrt(ms + EPS) * g_ref[...] * (1.0 - lam_init)
        o_ref[t * tq:(t + 1) * tq, h * LANES:(h + 1) * LANES] = y.astype(BF16)

    items = [(h, t, c) for h in range(heads) for t in range(tiles) for c in range(n_chunks)]
    ahead = [score(it) for it in items[:QK_AHEAD]]
    m = acc = None
    for idx, (h, t, c) in enumerate(items):
        s = ahead.pop(0)
        if idx + QK_AHEAD < len(items):
            ahead.append(score(items[idx + QK_AHEAD]))
        cm = jnp.max(s, axis=0, keepdims=True)
        m_new = cm if c == 0 else jnp.maximum(m, cm)
        src, off = chunks[c]
        e = jnp.exp2(s - m_new)
        lsum = jnp.sum(e, axis=0, keepdims=True)
        pv = _dot(vt_refs[src][h, :LANES, off:off + kc], e.astype(BF16))
        if c == 0:
            acc, l = pv, lsum
        else:
            alpha = jnp.exp2(m - m_new)
            acc, l = acc * alpha + pv, l * alpha + lsum
        m = m_new
        if c == n_chunks - 1:
            finish(h, t, acc, l)


def _diff_attn_call(q, ks, vts, lamv, subln_g, lam_init, tq, tiles, heads, flat_values=False):
    bsz, n, _ = q.shape
    kc = next(c for c in (512, 256, 128) if all(k.shape[1] % c == 0 for k in ks))
    hw = heads * LANES
    k_specs = [pl.BlockSpec((None, k.shape[1], hw), lambda b, h, i: (b, 0, h)) for k in ks]
    vt_index = (lambda b, h, i: (0, h, 0, b)) if flat_values else (lambda b, h, i: (b, h, 0, 0))
    vt_specs = [pl.BlockSpec((None, heads, V_ROWS, k.shape[1]), vt_index) for k in ks]
    return pl.pallas_call(
        functools.partial(_diff_attn_kernel, lam_init=lam_init, tq=tq, kc=kc),
        grid=(bsz, C_HEADS // heads, n // (tq * tiles)),
        in_specs=[pl.BlockSpec((None, tiles * tq, hw), lambda b, h, i: (b, i, h))]
        + k_specs + vt_specs + [_const_spec((4, HEAD_DIM)), _const_spec((1, LANES))],
        out_specs=pl.BlockSpec((None, tiles * tq, hw), lambda b, h, i: (b, i, h)),
        out_shape=jax.ShapeDtypeStruct((bsz, n, C_W), BF16),
        compiler_params=_params(3),
        name="diff_attn",
    )(q, *ks, *vts, lamv, subln_g)


def _sink_attn_kernel(*refs, windowed, tq, nblk):
    if windowed:
        (q_ref, ck_ref, kp_ref, km_ref, kn_ref, cvt_ref, vp_ref, vm_ref, vn_ref,
         bias_ref, sink_ref, o_ref) = refs
        kband = jnp.concatenate([kp_ref[...], km_ref[...], kn_ref[...]], axis=0)
        vtband = jnp.concatenate([vp_ref[...], vm_ref[...], vn_ref[...]], axis=1)
    else:
        q_ref, k_ref, vt_ref, sink_ref, o_ref = refs
    subs = q_ref.shape[0] // tq
    groups = q_ref.shape[1] // (2 * LANES)
    lane = lax.broadcasted_iota(jnp.int32, (tq, LANES), 1)
    lo = lane < HEAD_DIM

    def scores_of(stream):
        g, sub = stream
        qd = q_ref[sub * tq:(sub + 1) * tq, g * 2 * LANES:(g + 1) * 2 * LANES]
        parts = []
        for c in range(2):
            ch = qd[:, c * LANES:(c + 1) * LANES]
            zero = jnp.zeros_like(ch)
            parts += [jnp.where(lo, ch, zero), jnp.where(lo, zero, ch)]
        qq = jnp.concatenate(parts, axis=0)
        if not windowed:
            return [_dot_nt(k_ref[:, g * LANES:(g + 1) * LANES], qq)]
        blk = pl.program_id(2) * subs + sub
        variant = jnp.where(blk == 0, 0, jnp.where(blk == nblk - 1, 2, 1))
        band = _dot_nt(kband[sub * tq:(sub + 3) * tq, :], qq) + bias_ref[variant]
        return [_dot_nt(ck_ref[...], qq), band]

    def finish(stream, scores):
        g, sub = stream
        values = [cvt_ref[...], vtband[:, sub * tq:(sub + 3) * tq]] if windowed else [vt_ref[g]]
        sink = sink_ref[g]
        m = sink
        for s in scores:
            m = jnp.maximum(m, jnp.max(s, axis=0, keepdims=True))
        acc = None
        for s, vt in zip(scores, values):
            pv = _dot(vt, jnp.exp2(s - m).astype(BF16))
            acc = pv if acc is None else acc + pv
        inv = 1.0 / (acc[LANES:LANES + 1, :] + jnp.exp2(sink - m))
        ot = acc[:LANES, :] * inv
        for c in range(2):
            o0 = ot[:, (2 * c) * tq:(2 * c + 1) * tq].T
            o1 = ot[:, (2 * c + 1) * tq:(2 * c + 2) * tq].T
            cols = slice((2 * g + c) * LANES, (2 * g + c + 1) * LANES)
            o_ref[sub * tq:(sub + 1) * tq, cols] = jnp.where(lo, o0, o1).astype(BF16)

    streams = [(g, sub) for g in range(groups) for sub in range(subs)]
    ahead = [scores_of(st) for st in streams[:QK_AHEAD]]
    for idx, st in enumerate(streams):
        cur = ahead.pop(0)
        if idx + QK_AHEAD < len(streams):
            ahead.append(scores_of(streams[idx + QK_AHEAD]))
        finish(st, cur)


def _window_bias(n, tq):
    nblk = n // tq
    kk = np.arange(3 * tq)[:, None]
    qi = np.arange(tq)[None, :]
    band_ok = np.abs(qi + WINDOW - kk) <= WINDOW
    out = []
    for blk in (0, 1, nblk - 1):
        jpos = blk * tq - WINDOW + kk
        ok = band_ok & (jpos >= 0) & (jpos < n)
        out.append(np.tile(np.where(ok, 0.0, NEG_INF), (1, D_GROUP)))
    return jnp.asarray(np.stack(out), F32)


def _sink_attn_call(q, k, vt, sink_rows, ctx_k=None, ctx_vt=None, flat_values=False):
    bsz, n, _ = q.shape
    windowed = ctx_k is not None
    tq = WINDOW if windowed else n
    nblk = n // tq
    subs = next(s for s in (8, 4, 2, 1) if nblk % s == 0) if windowed else 1
    steps = nblk // subs
    groups = 1 if windowed else D_KV_HEADS
    q_spec = pl.BlockSpec((None, subs * tq, groups * 2 * LANES), lambda b, j, i: (b, i, j))
    sink_spec = pl.BlockSpec((groups, 1, D_GROUP * tq), lambda b, j, i: (j, 0, 0))
    if windowed:
        assert nblk >= 3, "first / interior / last mask variants need three query blocks"
        n_ctx = ctx_k.shape[1]
        prev = lambda i: jnp.maximum(i * subs - 1, 0)
        nxt = lambda i: jnp.minimum((i + 1) * subs, nblk - 1)
        in_specs = [
            q_spec,
            pl.BlockSpec((None, n_ctx, LANES), lambda b, j, i: (b, 0, j)),
            pl.BlockSpec((None, tq, LANES), lambda b, j, i: (b, prev(i), j)),
            pl.BlockSpec((None, subs * tq, LANES), lambda b, j, i: (b, i, j)),
            pl.BlockSpec((None, tq, LANES), lambda b, j, i: (b, nxt(i), j)),
            pl.BlockSpec((None, None, V_ROWS, n_ctx), lambda b, j, i: (b, j, 0, 0)),
            pl.BlockSpec((None, None, V_ROWS, tq), lambda b, j, i: (b, j, 0, prev(i))),
            pl.BlockSpec((None, None, V_ROWS, subs * tq), lambda b, j, i: (b, j, 0, i)),
            pl.BlockSpec((None, None, V_ROWS, tq), lambda b, j, i: (b, j, 0, nxt(i))),
            _const_spec((3, 3 * tq, D_GROUP * tq)),
            sink_spec,
        ]
        args = (q, ctx_k, k, k, k, ctx_vt, vt, vt, vt, _window_bias(n, tq), sink_rows)
    else:
        in_specs = [
            q_spec,
            pl.BlockSpec((None, n, groups * LANES), lambda b, j, i: (b, 0, j)),
            pl.BlockSpec((None, groups, V_ROWS, n),
                         (lambda b, j, i: (0, j, 0, b)) if flat_values else (lambda b, j, i: (b, j, 0, 0))),
            sink_spec,
        ]
        args = (q, k, vt, sink_rows)
    return pl.pallas_call(
        functools.partial(_sink_attn_kernel, windowed=windowed, tq=tq, nblk=nblk),
        grid=(bsz, D_KV_HEADS // groups, steps),
        in_specs=in_specs,
        out_specs=q_spec,
        out_shape=jax.ShapeDtypeStruct((bsz, n, DQ_W), BF16),
        compiler_params=_params(3),
        name="sink_attn",
    )(*args)


def _dup_heads(a):
    lead = a.shape[:-1]
    a = a.reshape(lead + (D_KV_HEADS, 1, HEAD_DIM))
    return jnp.broadcast_to(a, lead + (D_KV_HEADS, 2, HEAD_DIM)).reshape(lead + (DKV_DUP_W,))


def _odd_weights(w_in_odd, c_qn_g, c_kn_g, d_qn_g, d_kn_g):
    cq, ck, cv, dq, dk, dv = jnp.split(
        w_in_odd, (C_W, 2 * C_W, 3 * C_W, 3 * C_W + DQ_W, 3 * C_W + DQ_W + DKV_W), axis=-1)
    w = jnp.concatenate([cq, ck, dq, _dup_heads(dk), cv, _dup_heads(dv)], axis=-1).astype(BF16)
    gains = jnp.concatenate([
        jnp.tile(c_qn_g * (ATTN_SCALE * LOG2E), C_W // HEAD_DIM),
        jnp.tile(c_kn_g, C_W // HEAD_DIM),
        jnp.tile(d_qn_g * (ATTN_SCALE * LOG2E), DQ_W // HEAD_DIM),
        jnp.tile(d_kn_g, DKV_DUP_W // HEAD_DIM),
    ])[None, :]
    return w, gains


def _value_rows(v):
    bsz, nk, w = v.shape
    vt = v.reshape(bsz, nk, w // LANES, LANES).transpose(0, 2, 3, 1)
    pad = jnp.zeros((bsz, w // LANES, V_ROWS - LANES, nk), v.dtype).at[:, :, 0, :].set(1)
    return jnp.concatenate([vt, pad], axis=2)


def _rope_tables(n):
    rows = n // GRID_W
    row = np.repeat(np.arange(rows), GRID_W).astype(np.float64)
    col = np.tile(np.arange(GRID_W), rows).astype(np.float64)
    freqs = ROPE_BASE ** (-np.arange(ROPE_FREQS, dtype=np.float64) / ROPE_FREQS)
    ang = np.stack([row[:, None] * freqs, col[:, None] * freqs], axis=1)
    ang = np.concatenate([ang, ang], axis=-1).reshape(n, HEAD_DIM)
    ang = np.concatenate([ang, ang], axis=-1)
    return jnp.asarray(np.cos(ang), F32), jnp.asarray(np.sin(ang), F32)


def _token_tile(n):
    return min(n, 512)


def _run_group(x, mod, row_of_batch, wts, caches):
    bsz, n, _ = x.shape
    decode = caches is not None
    tm = _token_tile(n)
    depth = wts["norm1_g"].shape[0]
    new_cache = []
    for l in range(depth):
        g1n = wts["norm1_g"][l][None, :]
        g2n = wts["norm2_g"][l][None, :]
        if l % 2 == 0:
            e = l // 2
            a, z = _even_in_call(x, mod, l, row_of_batch, g1n, e, wts["w_in_even"], wts["w_pool"],
                                 wts["pool_scale"][e][None, :], wts["dftc"], wts["w_fft"], min(n, 1024),
                                 1 if decode or n > 1024 else next(q for q in (4, 2, 1) if bsz % q == 0))
            y = _dft_two_stage_call(z) if n > DFT_DIRECT_MAX else _dft_direct_call(z)
            p1, p2, w_out, out_idx = a, y, wts["w_out_even"], e
        else:
            o = l // 2
            lam_init = _lam_init(l)
            rope_tabs = _rope_tables(n) if decode else None
            xo = x if decode else x.reshape(1, bsz * n, D_MODEL)
            outs = _odd_in_call(xo, mod, l, row_of_batch, g1n, wts["w_in_odd"][o], wts["odd_gains"][o],
                                wts["head_mean"], rope_tabs, _token_tile(xo.shape[1]),
                                None if decode else n)
            if not decode:
                outs = [a if a.ndim == 4 else a.reshape(bsz, n, a.shape[-1]) for a in outs[:6]] + list(outs[6:])
            cq, ck, cvt, dq, dkk, dvt = outs[:6]
            if decode:
                c_k, c_v, d_k, d_v = caches
                lc = c_k.shape[2]
                ks = [c_k[:, o].reshape(bsz, lc, C_W).astype(BF16), ck]
                vts = [_value_rows(c_v[:, o].reshape(bsz, lc, C_W).astype(BF16)), cvt]
                ctx_k = _dup_heads(d_k[:, o].reshape(bsz, lc, DKV_W)).astype(BF16)
                ctx_vt = _value_rows(_dup_heads(d_v[:, o].reshape(bsz, lc, DKV_W)).astype(BF16))
                tq = min(n, 256)
            else:
                new_cache.append(outs[6:])
                ks, vts = [ck], [cvt]
                ctx_k = ctx_vt = None
                tq = n
            tiles = 2 if n % (2 * tq) == 0 else 1
            heads = 1 if decode else C_HEADS
            c_out = _diff_attn_call(cq, ks, vts, wts["lamv"][o],
                                    wts["c_subln_g"][o][None, :], lam_init, tq, tiles, heads, not decode)
            sink_tq = WINDOW if decode else n
            sink_rows = jnp.repeat(wts["d_sink"][o].reshape(D_KV_HEADS, D_GROUP) * LOG2E, sink_tq,
                                   axis=-1)[:, None, :]
            d_out = _sink_attn_call(dq, dkk, dvt, sink_rows, ctx_k, ctx_vt, not decode)
            p1, p2, w_out, out_idx = c_out, d_out, wts["w_out_odd"], o
        flat = (lambda a: a) if decode else (lambda a: a.reshape(1, bsz * n, a.shape[-1]))
        x = _out_mlp_call(flat(x), flat(p1), flat(p2), mod, l, row_of_batch, g2n, w_out, out_idx,
                          wts["w_mlp1"], wts["w_mlp2"],
                          tm if decode else _token_tile(bsz * n)).reshape(bsz, n, D_MODEL)
    return x, new_cache


def kernel(x_prompt, x_sample, c, cache_c_k, cache_c_v, cache_d_k, cache_d_v, c_ctx,
           norm1_g, norm2_g, w_ada, b_ada, w_in_even, w_pool, pool_scale, w_fft, w_out_even,
           w_in_odd, c_qn_g, c_kn_g, lam_q1, lam_k1, lam_q2, lam_k2, c_subln_g,
           d_qn_g, d_kn_g, d_sink, w_out_odd, w_mlp1, w_mlp2):
    depth = norm1_g.shape[0]
    n_odd = w_in_odd.shape[0]
    dec_b = c.shape[0]

    rows = -(-(1 + dec_b) // SUBLANES) * SUBLANES
    cond = jnp.zeros((rows, D_MODEL), F32).at[0].set(c_ctx).at[1:1 + dec_b].set(c)
    mod = _adaln_call(cond, w_ada, b_ada).reshape(depth, rows, 6, 1, D_MODEL)

    ic = np.arange(GROUP_W)
    ang = 2.0 * np.pi * np.outer(ic, ic) / GROUP_W
    dftc = _bf16_const(np.concatenate([np.cos(ang), -np.sin(ang)], axis=1))
    head_mean = _bf16_const(np.kron(np.eye(2 * LANES // HEAD_DIM), np.ones((HEAD_DIM, HEAD_DIM))) / HEAD_DIM)

    odd = [_odd_weights(w_in_odd[o], c_qn_g[o], c_kn_g[o], d_qn_g[o], d_kn_g[o]) for o in range(n_odd)]
    bf16 = lambda w: w.astype(BF16)
    wts = {
        "norm1_g": norm1_g, "norm2_g": norm2_g,
        "w_in_even": bf16(w_in_even), "w_pool": bf16(w_pool), "pool_scale": pool_scale,
        "w_fft": bf16(w_fft), "w_out_even": bf16(w_out_even), "dftc": dftc,
        "w_in_odd": [w for w, _ in odd], "odd_gains": [g for _, g in odd], "head_mean": head_mean,
        "lamv": jnp.stack([lam_q1, lam_k1, lam_q2, lam_k2], axis=1),
        "c_subln_g": c_subln_g, "d_sink": d_sink, "w_out_odd": bf16(w_out_odd),
        "w_mlp1": bf16(w_mlp1), "w_mlp2": bf16(w_mlp2),
    }

    y_prompt, new_cache = _run_group(x_prompt, mod, lambda b: 0, wts, None)
    y_sample, _ = _run_group(x_sample, mod, lambda b: 1 + b, wts,
                             (cache_c_k, cache_c_v, cache_d_k, cache_d_v))

    bsz, n, _ = x_prompt.shape
    new_c_k = jnp.stack([nc[0].transpose(0, 4, 1, 2, 3) for nc in new_cache], axis=1)
    new_c_v = jnp.stack([nc[1].reshape(bsz, n, C_HEADS, 2 * HEAD_DIM) for nc in new_cache], axis=1)
    new_d_k = jnp.stack([nc[2].transpose(0, 3, 1, 2) for nc in new_cache], axis=1)
    new_d_v = jnp.stack([nc[3].transpose(0, 3, 1, 2) for nc in new_cache], axis=1)
    return (y_prompt, y_sample, new_c_k, new_c_v, new_d_k, new_d_v)
```

```python
import functools
import math

import numpy as np
import jax
import jax.numpy as jnp
from jax import lax
from jax.experimental import pallas as pl
from jax.experimental.pallas import tpu as pltpu

F32 = jnp.float32
BF16 = jnp.bfloat16

D_MODEL = 1024
HEAD_DIM = 64
LANES = 128
SUBLANES = 8
GRID_W = 64
ROPE_FREQS = HEAD_DIM // 4
ROPE_BASE = 10000.0
EPS = 1e-6
NEG_INF = -1e30
WINDOW = 128
A_WIDTH = D_MODEL // 2
B_WIDTH = D_MODEL // 2
POOL_WINDOWS = (2, 4, 8, 16)
POOL_HALO = 8
N_GROUPS = 4
GROUP_W = A_WIDTH // N_GROUPS
C_HEADS = 4
D_HEADS = 8
D_KV_HEADS = 2
D_GROUP = D_HEADS // D_KV_HEADS
C_W = C_HEADS * 2 * HEAD_DIM
DQ_W = D_HEADS * HEAD_DIM
DKV_W = D_KV_HEADS * HEAD_DIM
DKV_DUP_W = 2 * DKV_W
D_FF = 4 * D_MODEL
ATTN_SCALE = HEAD_DIM ** -0.5
LOG2E = math.log2(math.e)
V_ROWS = LANES + 16
ODD_NORMED_W = C_W + C_W + DQ_W + DKV_DUP_W
ODD_W = ODD_NORMED_W + C_W + DKV_DUP_W
VMEM_LIMIT = 56 * 1024 * 1024


def _lam_init(layer):
    return 0.8 - 0.6 * math.exp(-0.3 * layer)


def _params(n_axes):
    return pltpu.CompilerParams(dimension_semantics=("arbitrary",) * n_axes,
                                vmem_limit_bytes=VMEM_LIMIT)


def _const_spec(shape):
    nd = len(shape)
    return pl.BlockSpec(shape, lambda *_: (0,) * nd, pipeline_mode=pl.Buffered(1))


def _stacked_spec(shape, idx):
    nd = len(shape)
    return pl.BlockSpec((None,) + tuple(shape), lambda *_: (idx,) + (0,) * nd, pipeline_mode=pl.Buffered(1))


def _mod_spec(layer, which, row_of_batch):
    return pl.BlockSpec((None, None, None, 1, D_MODEL),
                        lambda b, i: (layer, row_of_batch(b), which, 0, 0))


def _rms_mod(x, g, sc, sh):
    ms = jnp.mean(x * x, axis=-1, keepdims=True)
    return (x * lax.rsqrt(ms + EPS) * g) * (1.0 + sc) + sh


def _bf16_const(a):
    return jnp.asarray(a, F32).astype(BF16)


def _dot_nt(a, b):
    return lax.dot_general(a, b, (((1,), (1,)), ((), ())), preferred_element_type=F32)


def _dot(a, b):
    return jnp.dot(a, b, preferred_element_type=F32)


def _adaln_kernel(cond_ref, w_ref, b_ref, o_ref):
    cnd = cond_ref[...]
    s = (cnd * jax.nn.sigmoid(cnd)).astype(BF16)
    o_ref[...] = _dot(s, w_ref[...].astype(BF16)) + b_ref[...]


def _adaln_call(cond, w_ada, b_ada):
    depth = w_ada.shape[0]
    rows = cond.shape[0]
    tn = 1536
    return pl.pallas_call(
        _adaln_kernel,
        grid=(depth, 6 * D_MODEL // tn),
        in_specs=[
            pl.BlockSpec((rows, D_MODEL), lambda l, j: (0, 0)),
            pl.BlockSpec((None, D_MODEL, tn), lambda l, j: (l, 0, j)),
            pl.BlockSpec((None, 1, tn), lambda l, j: (l, 0, j)),
        ],
        out_specs=pl.BlockSpec((None, rows, tn), lambda l, j: (l, 0, j)),
        out_shape=jax.ShapeDtypeStruct((depth, rows, 6 * D_MODEL), F32),
        compiler_params=_params(2),
        name="adaln",
    )(cond, w_ada, b_ada.reshape(depth, 1, 6 * D_MODEL))


def _even_in_kernel(x_ref, xp_ref, xn_ref, sh_ref, sc_ref, g_ref, win_ref, wpool_ref,
                    pscale_ref, dftc_ref, wfft_ref, a_ref, z_ref, *, tm, n):
    i = pl.program_id(1)
    n_tiles = n // tm
    g, sc, sh = g_ref[...], sc_ref[...], sh_ref[...]
    seqs = x_ref.shape[0]
    h = _rms_mod(x_ref[...].reshape(seqs * tm, D_MODEL), g, sc, sh).astype(BF16)
    u = _dot(h, win_ref[...])
    xh = jnp.concatenate([xp_ref[...], xn_ref[...]], axis=0)
    hh = _rms_mod(xh, g, sc, sh).astype(BF16)
    uh = _dot(hh, win_ref[:, :A_WIDTH])
    has_prev = (i > 0).astype(F32)
    has_next = (i < n_tiles - 1).astype(F32)
    rows = tm + 2 * POOL_HALO

    for gi in range(N_GROUPS):
        ub = u[:, A_WIDTH + gi * GROUP_W:A_WIDTH + (gi + 1) * GROUP_W].astype(BF16)
        cs = _dot(ub, dftc_ref[...])
        zr = _dot(cs[:, :GROUP_W].astype(BF16), wfft_ref[gi])
        zi = _dot(cs[:, GROUP_W:].astype(BF16), wfft_ref[gi])
        for q in range(seqs):
            z_ref[q, :, gi * GROUP_W:(gi + 1) * GROUP_W] = zr[q * tm:(q + 1) * tm].astype(BF16)
            z_ref[q, :, B_WIDTH + gi * GROUP_W:B_WIDTH + (gi + 1) * GROUP_W] = zi[q * tm:(q + 1) * tm].astype(BF16)

    t = i * tm + lax.broadcasted_iota(jnp.int32, (tm, 1), 0)
    for q, (gi, w) in ((q, gw) for q in range(seqs) for gw in enumerate(POOL_WINDOWS)):
        cols = slice(gi * GROUP_W, (gi + 1) * GROUP_W)
        uq = u[q * tm:(q + 1) * tm, cols]
        ext = jnp.concatenate([uh[0:POOL_HALO, cols] * has_prev, uq,
                               uh[POOL_HALO:, cols] * has_next], axis=0)
        s = ext
        span = 1
        while span < w:
            s = s + pltpu.roll(s, span, 0)
            span *= 2
        shift = w // 2 - 1
        s = pltpu.roll(s, rows - shift, 0) if shift else s
        acc = s[POOL_HALO:POOL_HALO + tm]
        lo = jnp.maximum(t - w // 2, 0)
        hi = jnp.minimum(t + w // 2, n)
        cnt = (hi - lo).astype(F32)
        pooled = acc / cnt - uq
        y = _dot(pooled.astype(BF16), wpool_ref[gi]) * pscale_ref[:, cols]
        a_ref[q, :, cols] = y.astype(BF16)


def _even_in_call(x, mod, layer, row_of_batch, g, e, w_in, w_pool, pool_scale, dftc, w_fft, tm, seqs):
    bsz, n, _ = x.shape
    assert seqs == 1 or tm == n
    hb = tm // POOL_HALO
    kern = functools.partial(_even_in_kernel, tm=tm, n=n)
    return pl.pallas_call(
        kern,
        grid=(bsz // seqs, n // tm),
        in_specs=[
            pl.BlockSpec((seqs, tm, D_MODEL), lambda b, i: (b, i, 0)),
            pl.BlockSpec((None, POOL_HALO, D_MODEL),
                         lambda b, i: (b * seqs, jnp.maximum(i * hb - 1, 0), 0)),
            pl.BlockSpec((None, POOL_HALO, D_MODEL),
                         lambda b, i: (b * seqs, jnp.minimum((i + 1) * hb, n // POOL_HALO - 1), 0)),
            _mod_spec(layer, 0, row_of_batch),
            _mod_spec(layer, 1, row_of_batch),
            _const_spec((1, D_MODEL)),
            _stacked_spec((D_MODEL, D_MODEL), e),
            _stacked_spec((N_GROUPS, GROUP_W, GROUP_W), e),
            _const_spec((1, A_WIDTH)),
            _const_spec((GROUP_W, 2 * GROUP_W)),
            _stacked_spec((N_GROUPS, GROUP_W, GROUP_W), e),
        ],
        out_specs=[
            pl.BlockSpec((seqs, tm, A_WIDTH), lambda b, i: (b, i, 0)),
            pl.BlockSpec((seqs, tm, 2 * B_WIDTH), lambda b, i: (b, i, 0)),
        ],
        out_shape=[
            jax.ShapeDtypeStruct((bsz, n, A_WIDTH), BF16),
            jax.ShapeDtypeStruct((bsz, n, 2 * B_WIDTH), BF16),
        ],
        compiler_params=_params(2),
        name="even_in",
    )(x, x, x, mod, mod, g, w_in, w_pool, pool_scale, dftc, w_fft)


def _dft_direct_kernel(z_ref, m_ref, y_ref, *, scale):
    for q in range(z_ref.shape[0]):
        z = z_ref[q]
        rhs = jnp.concatenate([z[:, :B_WIDTH], z[:, B_WIDTH:]], axis=0)
        y_ref[q] = (_dot(m_ref[...], rhs) * scale).astype(BF16)


def _dft_direct_call(z):
    bsz, n, _ = z.shape
    k = np.arange(n)
    ang = 2.0 * np.pi * np.outer(k, k) / n
    m = _bf16_const(np.concatenate([np.cos(ang), np.sin(ang)], axis=1))
    kern = functools.partial(_dft_direct_kernel, scale=float((n * GROUP_W) ** -0.5))
    seqs = next(q for q in (4, 2, 1) if bsz % q == 0)
    return pl.pallas_call(
        kern,
        grid=(bsz // seqs,),
        in_specs=[pl.BlockSpec((seqs, n, 2 * B_WIDTH), lambda b: (b, 0, 0)),
                  _const_spec((n, 2 * n))],
        out_specs=pl.BlockSpec((seqs, n, B_WIDTH), lambda b: (b, 0, 0)),
        out_shape=jax.ShapeDtypeStruct((bsz, n, B_WIDTH), BF16),
        compiler_params=_params(1),
        name="dft_direct",
    )(z, m)


DFT_ROWS = 16
DFT_DIRECT_MAX = 256
DFT_PITCH = 24


def _pitch_rows(x):
    g, _, w = x.shape
    pad = jnp.zeros((g, DFT_PITCH - DFT_ROWS, w), x.dtype)
    return jnp.concatenate([x, pad], axis=1).reshape(g * DFT_PITCH, w)


def _dft_stage1_kernel(z_ref, m_ref, twr_ref, twi_ref, o_ref, zs_ref, os_ref, *, na):
    nt = 2 * B_WIDTH // LANES

    @pl.when((pl.program_id(0) == 0) & (pl.program_id(1) == 0))
    def _():
        os_ref[...] = jnp.zeros_like(os_ref)

    zf = _pitch_rows(z_ref[...].astype(F32))
    for k in range(nt):
        zs_ref[k] = zf[:, k * LANES:(k + 1) * LANES]
    for bi in range(DFT_ROWS):
        rows = pl.ds(bi, na, stride=DFT_PITCH)
        zb = [zs_ref[k, rows, :].astype(BF16) for k in range(nt)]
        rhs = jnp.concatenate([jnp.concatenate(zb[:nt // 2], axis=1),
                               jnp.concatenate(zb[nt // 2:], axis=1)], axis=0)
        r = _dot(m_ref[...], rhs)
        br, bim = r[:na], r[na:]
        tr, ti = twr_ref[bi], twi_ref[bi]
        out = (br * tr - bim * ti, br * ti + bim * tr)
        for k in range(nt):
            half, kk = divmod(k, nt // 2)
            os_ref[k, rows, :] = out[half][:, kk * LANES:(kk + 1) * LANES]
    of = jnp.concatenate([os_ref[k] for k in range(nt)], axis=1)
    o_ref[...] = of.reshape(na, DFT_PITCH, 2 * B_WIDTH)[:, :DFT_ROWS].astype(BF16)


def _dft_stage2_kernel(b_ref, m_ref, y_ref, ys_ref, *, scale):
    nb = b_ref.shape[1]

    @pl.when((pl.program_id(0) == 0) & (pl.program_id(1) == 0))
    def _():
        ys_ref[...] = jnp.zeros_like(ys_ref)

    for di in range(DFT_ROWS):
        bb = b_ref[di]
        rhs = jnp.concatenate([bb[:, :B_WIDTH], bb[:, B_WIDTH:]], axis=0)
        y = _dot(m_ref[...], rhs) * scale
        for k in range(B_WIDTH // LANES):
            ys_ref[k, pl.ds(di, nb, stride=DFT_PITCH), :] = y[:, k * LANES:(k + 1) * LANES]
    yf = jnp.concatenate([ys_ref[k] for k in range(B_WIDTH // LANES)], axis=1)
    y_ref[...] = yf.reshape(nb, DFT_PITCH, B_WIDTH)[:, :DFT_ROWS].astype(BF16)


def _dft_two_stage_call(z):
    bsz, n, _ = z.shape
    na = 1 << (int(math.log2(n)) // 2)
    nb = n // na
    ia = np.arange(na)
    ib = np.arange(nb)
    ang_a = 2.0 * np.pi * np.outer(ia, ia) / na
    fr, fi = np.cos(ang_a), -np.sin(ang_a)
    m1 = _bf16_const(np.block([[fr, -fi], [fi, fr]]))
    ang_t = 2.0 * np.pi * np.outer(ib, ia) / n
    twr = jnp.asarray(np.cos(ang_t)[:, :, None], F32)
    twi = jnp.asarray(-np.sin(ang_t)[:, :, None], F32)
    ang_b = 2.0 * np.pi * np.outer(ib, ib) / nb
    m2 = _bf16_const(np.concatenate([np.cos(ang_b), np.sin(ang_b)], axis=1))

    s1 = pl.pallas_call(
        functools.partial(_dft_stage1_kernel, na=na),
        grid=(bsz, nb // DFT_ROWS),
        in_specs=[
            pl.BlockSpec((None, na, DFT_ROWS, 2 * B_WIDTH), lambda b, j: (b, 0, j, 0)),
            _const_spec((2 * na, 2 * na)),
            pl.BlockSpec((DFT_ROWS, na, 1), lambda b, j: (j, 0, 0)),
            pl.BlockSpec((DFT_ROWS, na, 1), lambda b, j: (j, 0, 0)),
        ],
        out_specs=pl.BlockSpec((None, na, DFT_ROWS, 2 * B_WIDTH), lambda b, j: (b, 0, j, 0)),
        out_shape=jax.ShapeDtypeStruct((bsz, na, nb, 2 * B_WIDTH), BF16),
        scratch_shapes=[pltpu.VMEM((2 * B_WIDTH // LANES, na * DFT_PITCH, LANES), F32)] * 2,
        compiler_params=_params(2),
        name="dft_stage1",
    )(z.reshape(bsz, na, nb, 2 * B_WIDTH), m1, twr, twi)
    y = pl.pallas_call(
        functools.partial(_dft_stage2_kernel, scale=float((n * GROUP_W) ** -0.5)),
        grid=(bsz, na // DFT_ROWS),
        in_specs=[
            pl.BlockSpec((None, DFT_ROWS, nb, 2 * B_WIDTH), lambda b, j: (b, j, 0, 0)),
            _const_spec((nb, 2 * nb)),
        ],
        out_specs=pl.BlockSpec((None, nb, DFT_ROWS, B_WIDTH), lambda b, j: (b, 0, j, 0)),
        out_shape=jax.ShapeDtypeStruct((bsz, nb, na, B_WIDTH), BF16),
        scratch_shapes=[pltpu.VMEM((B_WIDTH // LANES, nb * DFT_PITCH, LANES), F32)],
        compiler_params=_params(2),
        name="dft_stage2",
    )(s1, m2)
    return y.reshape(bsz, n, B_WIDTH)


FF_CHUNK = 1024


def _out_mlp_kernel(x_ref, p1_ref, p2_ref, g1_ref, sh_ref, sc_ref, g2_ref, ng_ref,
                    wout_ref, w1_ref, w2_ref, o_ref):
    half = p1_ref.shape[-1]
    mix = _dot(p1_ref[...], wout_ref[:half, :]) + _dot(p2_ref[...], wout_ref[half:, :])
    x1 = x_ref[...] + g1_ref[...] * mix
    h = _rms_mod(x1, ng_ref[...], sc_ref[...], sh_ref[...]).astype(BF16)
    acc = None
    for c in range(D_FF // FF_CHUNK):
        a = _dot(h, w1_ref[:, c * FF_CHUNK:(c + 1) * FF_CHUNK])
        a = jnp.square(jnp.maximum(a, 0.0)).astype(BF16)
        part = _dot(a, w2_ref[c * FF_CHUNK:(c + 1) * FF_CHUNK, :])
        acc = part if acc is None else acc + part
    o_ref[...] = x1 + g2_ref[...] * acc


def _out_mlp_call(x, p1, p2, mod, layer, row_of_batch, ng, w_out, out_idx, w1, w2, tm):
    bsz, n, _ = x.shape
    half = p1.shape[-1]
    tok = lambda w: pl.BlockSpec((None, tm, w), lambda b, i: (b, i, 0))
    return pl.pallas_call(
        _out_mlp_kernel,
        grid=(bsz, n // tm),
        in_specs=[
            tok(D_MODEL), tok(half), tok(half),
            _mod_spec(layer, 2, row_of_batch),
            _mod_spec(layer, 3, row_of_batch),
            _mod_spec(layer, 4, row_of_batch),
            _mod_spec(layer, 5, row_of_batch),
            _const_spec((1, D_MODEL)),
            _stacked_spec((2 * half, D_MODEL), out_idx),
            _stacked_spec((D_MODEL, D_FF), layer),
            _stacked_spec((D_FF, D_MODEL), layer),
        ],
        out_specs=tok(D_MODEL),
        out_shape=jax.ShapeDtypeStruct((bsz, n, D_MODEL), F32),
        compiler_params=_params(2),
        name="out_mlp",
    )(x, p1, p2, mod, mod, mod, mod, ng, w_out, w1, w2)


def _odd_in_kernel(*refs, rope, emit_cache, tm):
    x_ref, sh_ref, sc_ref, g_ref, win_ref, gain_ref, hm_ref = refs[:7]
    k = 7
    if rope:
        cos_ref, sin_ref = refs[k:k + 2]
        k += 2
    cq_ref, ck_ref, cvt_ref, dq_ref, dk_ref, dvt_ref = refs[k:k + 6]
    k += 6
    if emit_cache:
        nck_ref, ncv_ref, ndk_ref, ndv_ref = refs[k:k + 4]
    subs = x_ref.shape[0] // tm
    lane = lax.broadcasted_iota(jnp.int32, (tm, LANES), 1)
    first_half = (lane % (2 * ROPE_FREQS)) < ROPE_FREQS
    pad_row = lax.broadcasted_iota(jnp.int32, (V_ROWS - LANES, tm), 0)
    pad = jnp.where(pad_row == 0, 1.0, 0.0).astype(BF16)

    def cache_rows(ref, sub, r):
        rpt = ref.shape[0] // x_ref.shape[0]
        return ref.at[pl.ds(sub * tm * rpt + r, tm, stride=rpt), :]

    def cache_cols(ref, sub, lead, mat):
        n_req = ref.shape[-1]
        for r in range(tm // n_req):
            ref[(sub * (tm // n_req) + r,) + lead] = mat[:, r * n_req:(r + 1) * n_req]

    groups = ((0, 2 * C_W), (2 * C_W, ODD_NORMED_W), (ODD_NORMED_W, ODD_W))
    hidden = {}

    def project(item):
        sub, grp = item
        if sub not in hidden:
            x = x_ref[sub * tm:(sub + 1) * tm, :]
            hidden[sub] = _rms_mod(x, g_ref[...], sc_ref[...], sh_ref[...]).astype(BF16)
        lo, hi = groups[grp]
        return _dot(hidden[sub], win_ref[:, lo:hi])

    def head_norms(p, lo):
        normed = []
        for c in range(p.shape[1] // (2 * LANES)):
            v = p[:, c * 2 * LANES:(c + 1) * 2 * LANES]
            ms = _dot((v * v).astype(BF16), hm_ref[...])
            y = v * lax.rsqrt(ms + EPS) * gain_ref[:, lo + c * 2 * LANES:lo + (c + 1) * 2 * LANES]
            normed += [y[:, :LANES], y[:, LANES:]]
        return normed

    def finish(item, p, normed):
        sub, grp = item
        rows = slice(sub * tm, (sub + 1) * tm)

        def roped(y):
            if not rope:
                return y
            rot = jnp.where(first_half, -pltpu.roll(y, LANES - ROPE_FREQS, 1),
                            pltpu.roll(y, ROPE_FREQS, 1))
            return y * cos_ref[rows, :] + rot * sin_ref[rows, :]

        def put(ref, chunks):
            for c, y in enumerate(chunks):
                ref[rows, c * LANES:(c + 1) * LANES] = roped(y).astype(BF16)

        nq = C_W // LANES
        if grp == 0:
            put(cq_ref, normed[:nq])
            put(ck_ref, normed[nq:])
            if emit_cache:
                for c, y in enumerate(normed[nq:]):
                    yt = y.T
                    for half in range(2):
                        cache_cols(nck_ref, sub, (c, half), yt[half * HEAD_DIM:(half + 1) * HEAD_DIM])
        elif grp == 1:
            put(dq_ref, normed[:nq])
            put(dk_ref, normed[nq:])
            if emit_cache:
                for j, y in enumerate(normed[nq:]):
                    cache_cols(ndk_ref, sub, (j,), y.T[:HEAD_DIM])
        else:
            cv, dv = p[:, :C_W], p[:, C_W:]
            for v, vt_ref in ((cv, cvt_ref), (dv, dvt_ref)):
                for hh in range(vt_ref.shape[0]):
                    vt = v[:, hh * LANES:(hh + 1) * LANES].T
                    vt_ref[hh, :LANES, rows] = vt.astype(BF16)
                    vt_ref[hh, LANES:, rows] = pad
                    if emit_cache and vt_ref is dvt_ref:
                        cache_cols(ndv_ref, sub, (hh,), vt[:HEAD_DIM])
            if emit_cache:
                for hh in range(C_HEADS):
                    cache_rows(ncv_ref, sub, hh)[...] = cv[:, hh * LANES:(hh + 1) * LANES]

    items = [(sub, grp) for sub in range(subs) for grp in range(len(groups))]
    p = project(items[0])
    for idx, item in enumerate(items):
        normed = head_norms(p, groups[item[1]][0]) if item[1] < 2 else None
        nxt = project(items[idx + 1]) if idx + 1 < len(items) else None
        finish(item, p, normed)
        p = nxt


def _odd_in_call(x, mod, layer, row_of_batch, g, w_in, gains, head_mean, rope_tabs, tm, cache_n=None):
    emit_cache = cache_n is not None
    bsz, n, _ = x.shape
    rope = rope_tabs is not None
    subs = 2 if n % (2 * tm) == 0 else 1
    ts = subs * tm
    tok = lambda w: pl.BlockSpec((None, ts, w), lambda b, i: (b, i, 0))
    in_specs = [
        tok(D_MODEL),
        _mod_spec(layer, 0, row_of_batch),
        _mod_spec(layer, 1, row_of_batch),
        _const_spec((1, D_MODEL)),
        _const_spec((D_MODEL, ODD_W)),
        _const_spec((1, ODD_NORMED_W)),
        _const_spec((2 * LANES, 2 * LANES)),
    ]
    args = [x, mod, mod, g, w_in, gains, head_mean]
    if rope:
        in_specs += [pl.BlockSpec((ts, LANES), lambda b, i: (i, 0))] * 2
        args += list(rope_tabs)
    vt_spec = lambda heads: pl.BlockSpec((None, heads, V_ROWS, ts), lambda b, i: (b, 0, 0, i))
    vt_shape = lambda heads: jax.ShapeDtypeStruct((bsz, heads, V_ROWS, n), BF16)
    tok_shape = lambda w: jax.ShapeDtypeStruct((bsz, n, w), BF16)
    out_specs = [tok(C_W), tok(C_W), vt_spec(C_HEADS), tok(DQ_W), tok(DKV_DUP_W), vt_spec(D_KV_HEADS)]
    out_shape = [tok_shape(C_W), tok_shape(C_W), vt_shape(C_HEADS), tok_shape(DQ_W),
                 tok_shape(DKV_DUP_W), vt_shape(D_KV_HEADS)]
    if emit_cache:
        assert bsz == 1 and tm % cache_n == 0 and n % cache_n == 0
        reqs, rps = n // cache_n, ts // cache_n
        for lead in ((C_HEADS, 2), None, (D_KV_HEADS,), (D_KV_HEADS,)):
            if lead is None:
                out_specs.append(pl.BlockSpec((None, ts * C_HEADS, LANES), lambda b, i: (b, i, 0)))
                out_shape.append(jax.ShapeDtypeStruct((bsz, n * C_HEADS, LANES), F32))
            else:
                zeros = (0,) * (len(lead) + 2)
                out_specs.append(pl.BlockSpec((rps,) + lead + (HEAD_DIM, cache_n),
                                              lambda b, i, zeros=zeros: (i,) + zeros))
                out_shape.append(jax.ShapeDtypeStruct((reqs,) + lead + (HEAD_DIM, cache_n), F32))
    return pl.pallas_call(
        functools.partial(_odd_in_kernel, rope=rope, emit_cache=emit_cache, tm=tm),
        grid=(bsz, n // ts),
        in_specs=in_specs,
        out_specs=out_specs,
        out_shape=out_shape,
        compiler_params=_params(2),
        name="odd_in",
    )(*args)


QK_AHEAD = 3


def _diff_attn_kernel(*refs, lam_init, tq, kc):
    n_src = (len(refs) - 4) // 2
    q_ref, (lamv_ref, g_ref, o_ref) = refs[0], refs[-3:]
    k_refs, vt_refs = refs[1:1 + n_src], refs[1 + n_src:1 + 2 * n_src]
    chunks = [(src, off) for src in range(n_src) for off in range(0, k_refs[src].shape[0], kc)]
    n_chunks = len(chunks)
    tiles = q_ref.shape[0] // tq
    heads = q_ref.shape[1] // LANES
    lane = lax.broadcasted_iota(jnp.int32, (tq, LANES), 1)
    lv = lamv_ref[...]
    lam = (jnp.exp(jnp.sum(lv[0:1] * lv[1:2], keepdims=True))
           - jnp.exp(jnp.sum(lv[2:3] * lv[3:4], keepdims=True)) + lam_init)

    qqs = {}

    def score(item):
        h, t, c = item
        cols = slice(h * LANES, (h + 1) * LANES)
        if (h, t) not in qqs:
            q = q_ref[t * tq:(t + 1) * tq, cols]
            zero = jnp.zeros_like(q)
            qqs[h, t] = jnp.concatenate([jnp.where(lane < HEAD_DIM, q, zero),
                                         jnp.where(lane >= HEAD_DIM, q, zero)], axis=0)
        src, off = chunks[c]
        return _dot_nt(k_refs[src][off:off + kc, cols], qqs[h, t])

    def finish(h, t, acc, l):
        inv = 1.0 / l
        ot = acc[:, :tq] * inv[:, :tq] - acc[:, tq:] * (inv[:, tq:] * lam)
        o = ot.T
        ms = jnp.mean(o * o, axis=-1, keepdims=True)
        y = o * lax.rsqrt(ms + EPS) * g_ref[...] * (1.0 - lam_init)
        o_ref[t * tq:(t + 1) * tq, h * LANES:(h + 1) * LANES] = y.astype(BF16)

    items = [(h, t, c) for h in range(heads) for t in range(tiles) for c in range(n_chunks)]
    ahead = [score(it) for it in items[:QK_AHEAD]]
    m = acc = None
    for idx, (h, t, c) in enumerate(items):
        s = ahead.pop(0)
        if idx + QK_AHEAD < len(items):
            ahead.append(score(items[idx + QK_AHEAD]))
        cm = jnp.max(s, axis=0, keepdims=True)
        m_new = cm if c == 0 else jnp.maximum(m, cm)
        src, off = chunks[c]
        e = jnp.exp2(s - m_new)
        lsum = jnp.sum(e, axis=0, keepdims=True)
        pv = _dot(vt_refs[src][h, :LANES, off:off + kc], e.astype(BF16))
        if c == 0:
            acc, l = pv, lsum
        else:
            alpha = jnp.exp2(m - m_new)
            acc, l = acc * alpha + pv, l * alpha + lsum
        m = m_new
        if c == n_chunks - 1:
            finish(h, t, acc, l)


def _diff_attn_call(q, ks, vts, lamv, subln_g, lam_init, tq, tiles, heads, flat_values=False):
    bsz, n, _ = q.shape
    kc = next(c for c in (512, 256, 128) if all(k.shape[1] % c == 0 for k in ks))
    hw = heads * LANES
    k_specs = [pl.BlockSpec((None, k.shape[1], hw), lambda b, h, i: (b, 0, h)) for k in ks]
    vt_index = (lambda b, h, i: (0, h, 0, b)) if flat_values else (lambda b, h, i: (b, h, 0, 0))
    vt_specs = [pl.BlockSpec((None, heads, V_ROWS, k.shape[1]), vt_index) for k in ks]
    return pl.pallas_call(
        functools.partial(_diff_attn_kernel, lam_init=lam_init, tq=tq, kc=kc),
        grid=(bsz, C_HEADS // heads, n // (tq * tiles)),
        in_specs=[pl.BlockSpec((None, tiles * tq, hw), lambda b, h, i: (b, i, h))]
        + k_specs + vt_specs + [_const_spec((4, HEAD_DIM)), _const_spec((1, LANES))],
        out_specs=pl.BlockSpec((None, tiles * tq, hw), lambda b, h, i: (b, i, h)),
        out_shape=jax.ShapeDtypeStruct((bsz, n, C_W), BF16),
        compiler_params=_params(3),
        name="diff_attn",
    )(q, *ks, *vts, lamv, subln_g)


def _sink_attn_kernel(*refs, windowed, tq, nblk):
    if windowed:
        (q_ref, ck_ref, kp_ref, km_ref, kn_ref, cvt_ref, vp_ref, vm_ref, vn_ref,
         bias_ref, sink_ref, o_ref) = refs
        kband = jnp.concatenate([kp_ref[...], km_ref[...], kn_ref[...]], axis=0)
        vtband = jnp.concatenate([vp_ref[...], vm_ref[...], vn_ref[...]], axis=1)
    else:
        q_ref, k_ref, vt_ref, sink_ref, o_ref = refs
    subs = q_ref.shape[0] // tq
    groups = q_ref.shape[1] // (2 * LANES)
    lane = lax.broadcasted_iota(jnp.int32, (tq, LANES), 1)
    lo = lane < HEAD_DIM

    def scores_of(stream):
        g, sub = stream
        qd = q_ref[sub * tq:(sub + 1) * tq, g * 2 * LANES:(g + 1) * 2 * LANES]
        parts = []
        for c in range(2):
            ch = qd[:, c * LANES:(c + 1) * LANES]
            zero = jnp.zeros_like(ch)
            parts += [jnp.where(lo, ch, zero), jnp.where(lo, zero, ch)]
        qq = jnp.concatenate(parts, axis=0)
        if not windowed:
            return [_dot_nt(k_ref[:, g * LANES:(g + 1) * LANES], qq)]
        blk = pl.program_id(2) * subs + sub
        variant = jnp.where(blk == 0, 0, jnp.where(blk == nblk - 1, 2, 1))
        band = _dot_nt(kband[sub * tq:(sub + 3) * tq, :], qq) + bias_ref[variant]
        return [_dot_nt(ck_ref[...], qq), band]

    def finish(stream, scores):
        g, sub = stream
        values = [cvt_ref[...], vtband[:, sub * tq:(sub + 3) * tq]] if windowed else [vt_ref[g]]
        sink = sink_ref[g]
        m = sink
        for s in scores:
            m = jnp.maximum(m, jnp.max(s, axis=0, keepdims=True))
        acc = None
        for s, vt in zip(scores, values):
            pv = _dot(vt, jnp.exp2(s - m).astype(BF16))
            acc = pv if acc is None else acc + pv
        inv = 1.0 / (acc[LANES:LANES + 1, :] + jnp.exp2(sink - m))
        ot = acc[:LANES, :] * inv
        for c in range(2):
            o0 = ot[:, (2 * c) * tq:(2 * c + 1) * tq].T
            o1 = ot[:, (2 * c + 1) * tq:(2 * c + 2) * tq].T
            cols = slice((2 * g + c) * LANES, (2 * g + c + 1) * LANES)
            o_ref[sub * tq:(sub + 1) * tq, cols] = jnp.where(lo, o0, o1).astype(BF16)

    streams = [(g, sub) for g in range(groups) for sub in range(subs)]
    ahead = [scores_of(st) for st in streams[:QK_AHEAD]]
    for idx, st in enumerate(streams):
        cur = ahead.pop(0)
        if idx + QK_AHEAD < len(streams):
            ahead.append(scores_of(streams[idx + QK_AHEAD]))
        finish(st, cur)


def _window_bias(n, tq):
    nblk = n // tq
    kk = np.arange(3 * tq)[:, None]
    qi = np.arange(tq)[None, :]
    band_ok = np.abs(qi + WINDOW - kk) <= WINDOW
    out = []
    for blk in (0, 1, nblk - 1):
        jpos = blk * tq - WINDOW + kk
        ok = band_ok & (jpos >= 0) & (jpos < n)
        out.append(np.tile(np.where(ok, 0.0, NEG_INF), (1, D_GROUP)))
    return jnp.asarray(np.stack(out), F32)


def _sink_attn_call(q, k, vt, sink_rows, ctx_k=None, ctx_vt=None, flat_values=False):
    bsz, n, _ = q.shape
    windowed = ctx_k is not None
    tq = WINDOW if windowed else n
    nblk = n // tq
    subs = next(s for s in (8, 4, 2, 1) if nblk % s == 0) if windowed else 1
    steps = nblk // subs
    groups = 1 if windowed else D_KV_HEADS
    q_spec = pl.BlockSpec((None, subs * tq, groups * 2 * LANES), lambda b, j, i: (b, i, j))
    sink_spec = pl.BlockSpec((groups, 1, D_GROUP * tq), lambda b, j, i: (j, 0, 0))
    if windowed:
        assert nblk >= 3, "first / interior / last mask variants need three query blocks"
        n_ctx = ctx_k.shape[1]
        prev = lambda i: jnp.maximum(i * subs - 1, 0)
        nxt = lambda i: jnp.minimum((i + 1) * subs, nblk - 1)
        in_specs = [
            q_spec,
            pl.BlockSpec((None, n_ctx, LANES), lambda b, j, i: (b, 0, j)),
            pl.BlockSpec((None, tq, LANES), lambda b, j, i: (b, prev(i), j)),
            pl.BlockSpec((None, subs * tq, LANES), lambda b, j, i: (b, i, j)),
            pl.BlockSpec((None, tq, LANES), lambda b, j, i: (b, nxt(i), j)),
            pl.BlockSpec((None, None, V_ROWS, n_ctx), lambda b, j, i: (b, j, 0, 0)),
            pl.BlockSpec((None, None, V_ROWS, tq), lambda b, j, i: (b, j, 0, prev(i))),
            pl.BlockSpec((None, None, V_ROWS, subs * tq), lambda b, j, i: (b, j, 0, i)),
            pl.BlockSpec((None, None, V_ROWS, tq), lambda b, j, i: (b, j, 0, nxt(i))),
            _const_spec((3, 3 * tq, D_GROUP * tq)),
            sink_spec,
        ]
        args = (q, ctx_k, k, k, k, ctx_vt, vt, vt, vt, _window_bias(n, tq), sink_rows)
    else:
        in_specs = [
            q_spec,
            pl.BlockSpec((None, n, groups * LANES), lambda b, j, i: (b, 0, j)),
            pl.BlockSpec((None, groups, V_ROWS, n),
                         (lambda b, j, i: (0, j, 0, b)) if flat_values else (lambda b, j, i: (b, j, 0, 0))),
            sink_spec,
        ]
        args = (q, k, vt, sink_rows)
    return pl.pallas_call(
        functools.partial(_sink_attn_kernel, windowed=windowed, tq=tq, nblk=nblk),
        grid=(bsz, D_KV_HEADS // groups, steps),
        in_specs=in_specs,
        out_specs=q_spec,
        out_shape=jax.ShapeDtypeStruct((bsz, n, DQ_W), BF16),
        compiler_params=_params(3),
        name="sink_attn",
    )(*args)


def _dup_heads(a):
    lead = a.shape[:-1]
    a = a.reshape(lead + (D_KV_HEADS, 1, HEAD_DIM))
    return jnp.broadcast_to(a, lead + (D_KV_HEADS, 2, HEAD_DIM)).reshape(lead + (DKV_DUP_W,))


def _odd_weights(w_in_odd, c_qn_g, c_kn_g, d_qn_g, d_kn_g):
    cq, ck, cv, dq, dk, dv = jnp.split(
        w_in_odd, (C_W, 2 * C_W, 3 * C_W, 3 * C_W + DQ_W, 3 * C_W + DQ_W + DKV_W), axis=-1)
    w = jnp.concatenate([cq, ck, dq, _dup_heads(dk), cv, _dup_heads(dv)], axis=-1).astype(BF16)
    gains = jnp.concatenate([
        jnp.tile(c_qn_g * (ATTN_SCALE * LOG2E), C_W // HEAD_DIM),
        jnp.tile(c_kn_g, C_W // HEAD_DIM),
        jnp.tile(d_qn_g * (ATTN_SCALE * LOG2E), DQ_W // HEAD_DIM),
        jnp.tile(d_kn_g, DKV_DUP_W // HEAD_DIM),
    ])[None, :]
    return w, gains


def _value_rows(v):
    bsz, nk, w = v.shape
    vt = v.reshape(bsz, nk, w // LANES, LANES).transpose(0, 2, 3, 1)
    pad = jnp.zeros((bsz, w // LANES, V_ROWS - LANES, nk), v.dtype).at[:, :, 0, :].set(1)
    return jnp.concatenate([vt, pad], axis=2)


def _rope_tables(n):
    rows = n // GRID_W
    row = np.repeat(np.arange(rows), GRID_W).astype(np.float64)
    col = np.tile(np.arange(GRID_W), rows).astype(np.float64)
    freqs = ROPE_BASE ** (-np.arange(ROPE_FREQS, dtype=np.float64) / ROPE_FREQS)
    ang = np.stack([row[:, None] * freqs, col[:, None] * freqs], axis=1)
    ang = np.concatenate([ang, ang], axis=-1).reshape(n, HEAD_DIM)
    ang = np.concatenate([ang, ang], axis=-1)
    return jnp.asarray(np.cos(ang), F32), jnp.asarray(np.sin(ang), F32)


MLP_TILE = 1024


def _token_tile(n):
    return min(n, 512)


def _run_group(x, mod, row_of_batch, wts, caches):
    bsz, n, _ = x.shape
    decode = caches is not None
    tm = _token_tile(n)
    depth = wts["norm1_g"].shape[0]
    new_cache = []
    for l in range(depth):
        g1n = wts["norm1_g"][l][None, :]
        g2n = wts["norm2_g"][l][None, :]
        if l % 2 == 0:
            e = l // 2
            a, z = _even_in_call(x, mod, l, row_of_batch, g1n, e, wts["w_in_even"], wts["w_pool"],
                                 wts["pool_scale"][e][None, :], wts["dftc"], wts["w_fft"], min(n, 1024),
                                 1 if decode or n > 1024 else next(q for q in (4, 2, 1) if bsz % q == 0))
            y = _dft_two_stage_call(z) if n > DFT_DIRECT_MAX else _dft_direct_call(z)
            p1, p2, w_out, out_idx = a, y, wts["w_out_even"], e
        else:
            o = l // 2
            lam_init = _lam_init(l)
            rope_tabs = _rope_tables(n) if decode else None
            xo = x if decode else x.reshape(1, bsz * n, D_MODEL)
            outs = _odd_in_call(xo, mod, l, row_of_batch, g1n, wts["w_in_odd"][o], wts["odd_gains"][o],
                                wts["head_mean"], rope_tabs, _token_tile(xo.shape[1]),
                                None if decode else n)
            if not decode:
                outs = [a if a.ndim == 4 else a.reshape(bsz, n, a.shape[-1]) for a in outs[:6]] + list(outs[6:])
            cq, ck, cvt, dq, dkk, dvt = outs[:6]
            if decode:
                c_k, c_v, d_k, d_v = caches
                lc = c_k.shape[2]
                ks = [c_k[:, o].reshape(bsz, lc, C_W).astype(BF16), ck]
                vts = [_value_rows(c_v[:, o].reshape(bsz, lc, C_W).astype(BF16)), cvt]
                ctx_k = _dup_heads(d_k[:, o].reshape(bsz, lc, DKV_W)).astype(BF16)
                ctx_vt = _value_rows(_dup_heads(d_v[:, o].reshape(bsz, lc, DKV_W)).astype(BF16))
                tq = min(n, 256)
            else:
                new_cache.append(outs[6:])
                ks, vts = [ck], [cvt]
                ctx_k = ctx_vt = None
                tq = n
            tiles = 2 if n % (2 * tq) == 0 else 1
            heads = 1 if decode else C_HEADS
            c_out = _diff_attn_call(cq, ks, vts, wts["lamv"][o],
                                    wts["c_subln_g"][o][None, :], lam_init, tq, tiles, heads, not decode)
            sink_tq = WINDOW if decode else n
            sink_rows = jnp.repeat(wts["d_sink"][o].reshape(D_KV_HEADS, D_GROUP) * LOG2E, sink_tq,
                                   axis=-1)[:, None, :]
            d_out = _sink_attn_call(dq, dkk, dvt, sink_rows, ctx_k, ctx_vt, not decode)
            p1, p2, w_out, out_idx = c_out, d_out, wts["w_out_odd"], o
        flat = (lambda a: a) if decode else (lambda a: a.reshape(1, bsz * n, a.shape[-1]))
        x = _out_mlp_call(flat(x), flat(p1), flat(p2), mod, l, row_of_batch, g2n, w_out, out_idx,
                          wts["w_mlp1"], wts["w_mlp2"],
                          min(n if decode else bsz * n, MLP_TILE)).reshape(bsz, n, D_MODEL)
    return x, new_cache


def kernel(x_prompt, x_sample, c, cache_c_k, cache_c_v, cache_d_k, cache_d_v, c_ctx,
           norm1_g, norm2_g, w_ada, b_ada, w_in_even, w_pool, pool_scale, w_fft, w_out_even,
           w_in_odd, c_qn_g, c_kn_g, lam_q1, lam_k1, lam_q2, lam_k2, c_subln_g,
           d_qn_g, d_kn_g, d_sink, w_out_odd, w_mlp1, w_mlp2):
    depth = norm1_g.shape[0]
    n_odd = w_in_odd.shape[0]
    dec_b = c.shape[0]

    rows = -(-(1 + dec_b) // SUBLANES) * SUBLANES
    cond = jnp.zeros((rows, D_MODEL), F32).at[0].set(c_ctx).at[1:1 + dec_b].set(c)
    mod = _adaln_call(cond, w_ada, b_ada).reshape(depth, rows, 6, 1, D_MODEL)

    ic = np.arange(GROUP_W)
    ang = 2.0 * np.pi * np.outer(ic, ic) / GROUP_W
    dftc = _bf16_const(np.concatenate([np.cos(ang), -np.sin(ang)], axis=1))
    head_mean = _bf16_const(np.kron(np.eye(2 * LANES // HEAD_DIM), np.ones((HEAD_DIM, HEAD_DIM))) / HEAD_DIM)

    odd = [_odd_weights(w_in_odd[o], c_qn_g[o], c_kn_g[o], d_qn_g[o], d_kn_g[o]) for o in range(n_odd)]
    bf16 = lambda w: w.astype(BF16)
    wts = {
        "norm1_g": norm1_g, "norm2_g": norm2_g,
        "w_in_even": bf16(w_in_even), "w_pool": bf16(w_pool), "pool_scale": pool_scale,
        "w_fft": bf16(w_fft), "w_out_even": bf16(w_out_even), "dftc": dftc,
        "w_in_odd": [w for w, _ in odd], "odd_gains": [g for _, g in odd], "head_mean": head_mean,
        "lamv": jnp.stack([lam_q1, lam_k1, lam_q2, lam_k2], axis=1),
        "c_subln_g": c_subln_g, "d_sink": d_sink, "w_out_odd": bf16(w_out_odd),
        "w_mlp1": bf16(w_mlp1), "w_mlp2": bf16(w_mlp2),
    }

    y_prompt, new_cache = _run_group(x_prompt, mod, lambda b: 0, wts, None)
    y_sample, _ = _run_group(x_sample, mod, lambda b: 1 + b, wts,
                             (cache_c_k, cache_c_v, cache_d_k, cache_d_v))

    bsz, n, _ = x_prompt.shape
    new_c_k = jnp.stack([nc[0].transpose(0, 4, 1, 2, 3) for nc in new_cache], axis=1)
    new_c_v = jnp.stack([nc[1].reshape(bsz, n, C_HEADS, 2 * HEAD_DIM) for nc in new_cache], axis=1)
    new_d_k = jnp.stack([nc[2].transpose(0, 3, 1, 2) for nc in new_cache], axis=1)
    new_d_v = jnp.stack([nc[3].transpose(0, 3, 1, 2) for nc in new_cache], axis=1)
    return (y_prompt, y_sample, new_c_k, new_c_v, new_d_k, new_d_v)
```

```python
import functools
import math

import numpy as np
import jax
import jax.numpy as jnp
from jax import lax
from jax.experimental import pallas as pl
from jax.experimental.pallas import tpu as pltpu

F32 = jnp.float32
BF16 = jnp.bfloat16

D_MODEL = 1024
HEAD_DIM = 64
LANES = 128
SUBLANES = 8
GRID_W = 64
ROPE_FREQS = HEAD_DIM // 4
ROPE_BASE = 10000.0
EPS = 1e-6
NEG_INF = -1e30
WINDOW = 128
A_WIDTH = D_MODEL // 2
B_WIDTH = D_MODEL // 2
POOL_WINDOWS = (2, 4, 8, 16)
POOL_HALO = 8
N_GROUPS = 4
GROUP_W = A_WIDTH // N_GROUPS
C_HEADS = 4
D_HEADS = 8
D_KV_HEADS = 2
D_GROUP = D_HEADS // D_KV_HEADS
C_W = C_HEADS * 2 * HEAD_DIM
DQ_W = D_HEADS * HEAD_DIM
DKV_W = D_KV_HEADS * HEAD_DIM
DKV_DUP_W = 2 * DKV_W
D_FF = 4 * D_MODEL
ATTN_SCALE = HEAD_DIM ** -0.5
LOG2E = math.log2(math.e)
V_ROWS = LANES + 16
ODD_NORMED_W = C_W + C_W + DQ_W + DKV_DUP_W
ODD_W = ODD_NORMED_W + C_W + DKV_DUP_W
VMEM_LIMIT = 56 * 1024 * 1024


def _lam_init(layer):
    return 0.8 - 0.6 * math.exp(-0.3 * layer)


def _params(n_axes):
    return pltpu.CompilerParams(dimension_semantics=("arbitrary",) * n_axes,
                                vmem_limit_bytes=VMEM_LIMIT)


def _const_spec(shape):
    nd = len(shape)
    return pl.BlockSpec(shape, lambda *_: (0,) * nd, pipeline_mode=pl.Buffered(1))


def _stacked_spec(shape, idx):
    nd = len(shape)
    return pl.BlockSpec((None,) + tuple(shape), lambda *_: (idx,) + (0,) * nd, pipeline_mode=pl.Buffered(1))


def _mod_spec(layer, which, row_of_batch):
    return pl.BlockSpec((None, None, None, 1, D_MODEL),
                        lambda b, i: (layer, row_of_batch(b), which, 0, 0))


def _rms_mod(x, g, sc, sh):
    ms = jnp.mean(x * x, axis=-1, keepdims=True)
    return (x * lax.rsqrt(ms + EPS) * g) * (1.0 + sc) + sh


def _bf16_const(a):
    return jnp.asarray(a, F32).astype(BF16)


def _dot_nt(a, b):
    return lax.dot_general(a, b, (((1,), (1,)), ((), ())), preferred_element_type=F32)


def _dot(a, b):
    return jnp.dot(a, b, preferred_element_type=F32)


def _adaln_kernel(cond_ref, w_ref, b_ref, o_ref):
    cnd = cond_ref[...]
    s = (cnd * jax.nn.sigmoid(cnd)).astype(BF16)
    o_ref[...] = _dot(s, w_ref[...].astype(BF16)) + b_ref[...]


def _adaln_call(cond, w_ada, b_ada):
    depth = w_ada.shape[0]
    rows = cond.shape[0]
    tn = 1536
    return pl.pallas_call(
        _adaln_kernel,
        grid=(depth, 6 * D_MODEL // tn),
        in_specs=[
            pl.BlockSpec((rows, D_MODEL), lambda l, j: (0, 0)),
            pl.BlockSpec((None, D_MODEL, tn), lambda l, j: (l, 0, j)),
            pl.BlockSpec((None, 1, tn), lambda l, j: (l, 0, j)),
        ],
        out_specs=pl.BlockSpec((None, rows, tn), lambda l, j: (l, 0, j)),
        out_shape=jax.ShapeDtypeStruct((depth, rows, 6 * D_MODEL), F32),
        compiler_params=_params(2),
        name="adaln",
    )(cond, w_ada, b_ada.reshape(depth, 1, 6 * D_MODEL))


def _even_in_kernel(x_ref, xp_ref, xn_ref, sh_ref, sc_ref, g_ref, win_ref, wpool_ref,
                    pscale_ref, dftc_ref, wfft_ref, a_ref, z_ref, *, tm, n):
    i = pl.program_id(1)
    n_tiles = n // tm
    g, sc, sh = g_ref[...], sc_ref[...], sh_ref[...]
    seqs = x_ref.shape[0]
    h = _rms_mod(x_ref[...].reshape(seqs * tm, D_MODEL), g, sc, sh).astype(BF16)
    u = _dot(h, win_ref[...])
    xh = jnp.concatenate([xp_ref[...], xn_ref[...]], axis=0)
    hh = _rms_mod(xh, g, sc, sh).astype(BF16)
    uh = _dot(hh, win_ref[:, :A_WIDTH])
    has_prev = (i > 0).astype(F32)
    has_next = (i < n_tiles - 1).astype(F32)
    rows = tm + 2 * POOL_HALO

    for gi in range(N_GROUPS):
        ub = u[:, A_WIDTH + gi * GROUP_W:A_WIDTH + (gi + 1) * GROUP_W].astype(BF16)
        cs = _dot(ub, dftc_ref[...])
        zr = _dot(cs[:, :GROUP_W].astype(BF16), wfft_ref[gi])
        zi = _dot(cs[:, GROUP_W:].astype(BF16), wfft_ref[gi])
        for q in range(seqs):
            z_ref[q, :, gi * GROUP_W:(gi + 1) * GROUP_W] = zr[q * tm:(q + 1) * tm].astype(BF16)
            z_ref[q, :, B_WIDTH + gi * GROUP_W:B_WIDTH + (gi + 1) * GROUP_W] = zi[q * tm:(q + 1) * tm].astype(BF16)

    t = i * tm + lax.broadcasted_iota(jnp.int32, (tm, 1), 0)
    for q, (gi, w) in ((q, gw) for q in range(seqs) for gw in enumerate(POOL_WINDOWS)):
        cols = slice(gi * GROUP_W, (gi + 1) * GROUP_W)
        uq = u[q * tm:(q + 1) * tm, cols]
        ext = jnp.concatenate([uh[0:POOL_HALO, cols] * has_prev, uq,
                               uh[POOL_HALO:, cols] * has_next], axis=0)
        s = ext
        span = 1
        while span < w:
            s = s + pltpu.roll(s, span, 0)
            span *= 2
        shift = w // 2 - 1
        s = pltpu.roll(s, rows - shift, 0) if shift else s
        acc = s[POOL_HALO:POOL_HALO + tm]
        lo = jnp.maximum(t - w // 2, 0)
        hi = jnp.minimum(t + w // 2, n)
        cnt = (hi - lo).astype(F32)
        pooled = acc / cnt - uq
        y = _dot(pooled.astype(BF16), wpool_ref[gi]) * pscale_ref[:, cols]
        a_ref[q, :, cols] = y.astype(BF16)


def _even_in_call(x, mod, layer, row_of_batch, g, e, w_in, w_pool, pool_scale, dftc, w_fft, tm, seqs):
    bsz, n, _ = x.shape
    assert seqs == 1 or tm == n
    hb = tm // POOL_HALO
    kern = functools.partial(_even_in_kernel, tm=tm, n=n)
    return pl.pallas_call(
        kern,
        grid=(bsz // seqs, n // tm),
        in_specs=[
            pl.BlockSpec((seqs, tm, D_MODEL), lambda b, i: (b, i, 0)),
            pl.BlockSpec((None, POOL_HALO, D_MODEL),
                         lambda b, i: (b * seqs, jnp.maximum(i * hb - 1, 0), 0)),
            pl.BlockSpec((None, POOL_HALO, D_MODEL),
                         lambda b, i: (b * seqs, jnp.minimum((i + 1) * hb, n // POOL_HALO - 1), 0)),
            _mod_spec(layer, 0, row_of_batch),
            _mod_spec(layer, 1, row_of_batch),
            _const_spec((1, D_MODEL)),
            _stacked_spec((D_MODEL, D_MODEL), e),
            _stacked_spec((N_GROUPS, GROUP_W, GROUP_W), e),
            _const_spec((1, A_WIDTH)),
            _const_spec((GROUP_W, 2 * GROUP_W)),
            _stacked_spec((N_GROUPS, GROUP_W, GROUP_W), e),
        ],
        out_specs=[
            pl.BlockSpec((seqs, tm, A_WIDTH), lambda b, i: (b, i, 0)),
            pl.BlockSpec((seqs, tm, 2 * B_WIDTH), lambda b, i: (b, i, 0)),
        ],
        out_shape=[
            jax.ShapeDtypeStruct((bsz, n, A_WIDTH), BF16),
            jax.ShapeDtypeStruct((bsz, n, 2 * B_WIDTH), BF16),
        ],
        compiler_params=_params(2),
        name="even_in",
    )(x, x, x, mod, mod, g, w_in, w_pool, pool_scale, dftc, w_fft)


def _dft_direct_kernel(z_ref, m_ref, y_ref, *, scale):
    for q in range(z_ref.shape[0]):
        z = z_ref[q]
        rhs = jnp.concatenate([z[:, :B_WIDTH], z[:, B_WIDTH:]], axis=0)
        y_ref[q] = (_dot(m_ref[...], rhs) * scale).astype(BF16)


def _dft_direct_call(z):
    bsz, n, _ = z.shape
    k = np.arange(n)
    ang = 2.0 * np.pi * np.outer(k, k) / n
    m = _bf16_const(np.concatenate([np.cos(ang), np.sin(ang)], axis=1))
    kern = functools.partial(_dft_direct_kernel, scale=float((n * GROUP_W) ** -0.5))
    seqs = next(q for q in (4, 2, 1) if bsz % q == 0)
    return pl.pallas_call(
        kern,
        grid=(bsz // seqs,),
        in_specs=[pl.BlockSpec((seqs, n, 2 * B_WIDTH), lambda b: (b, 0, 0)),
                  _const_spec((n, 2 * n))],
        out_specs=pl.BlockSpec((seqs, n, B_WIDTH), lambda b: (b, 0, 0)),
        out_shape=jax.ShapeDtypeStruct((bsz, n, B_WIDTH), BF16),
        compiler_params=_params(1),
        name="dft_direct",
    )(z, m)


DFT_ROWS = 16
DFT_DIRECT_MAX = 256
DFT_PITCH = 24


def _pitch_rows(x):
    g, _, w = x.shape
    pad = jnp.zeros((g, DFT_PITCH - DFT_ROWS, w), x.dtype)
    return jnp.concatenate([x, pad], axis=1).reshape(g * DFT_PITCH, w)


def _dft_stage1_kernel(z_ref, m_ref, twr_ref, twi_ref, o_ref, zs_ref, os_ref, *, na):
    nt = 2 * B_WIDTH // LANES

    @pl.when((pl.program_id(0) == 0) & (pl.program_id(1) == 0))
    def _():
        os_ref[...] = jnp.zeros_like(os_ref)

    zf = _pitch_rows(z_ref[...].astype(F32))
    for k in range(nt):
        zs_ref[k] = zf[:, k * LANES:(k + 1) * LANES]
    for bi in range(DFT_ROWS):
        rows = pl.ds(bi, na, stride=DFT_PITCH)
        zb = [zs_ref[k, rows, :].astype(BF16) for k in range(nt)]
        rhs = jnp.concatenate([jnp.concatenate(zb[:nt // 2], axis=1),
                               jnp.concatenate(zb[nt // 2:], axis=1)], axis=0)
        r = _dot(m_ref[...], rhs)
        br, bim = r[:na], r[na:]
        tr, ti = twr_ref[bi], twi_ref[bi]
        out = (br * tr - bim * ti, br * ti + bim * tr)
        for k in range(nt):
            half, kk = divmod(k, nt // 2)
            os_ref[k, rows, :] = out[half][:, kk * LANES:(kk + 1) * LANES]
    of = jnp.concatenate([os_ref[k] for k in range(nt)], axis=1)
    o_ref[...] = of.reshape(na, DFT_PITCH, 2 * B_WIDTH)[:, :DFT_ROWS].astype(BF16)


def _dft_stage2_kernel(b_ref, m_ref, y_ref, ys_ref, *, scale):
    nb = b_ref.shape[1]

    @pl.when((pl.program_id(0) == 0) & (pl.program_id(1) == 0))
    def _():
        ys_ref[...] = jnp.zeros_like(ys_ref)

    for di in range(DFT_ROWS):
        bb = b_ref[di]
        rhs = jnp.concatenate([bb[:, :B_WIDTH], bb[:, B_WIDTH:]], axis=0)
        y = _dot(m_ref[...], rhs) * scale
        for k in range(B_WIDTH // LANES):
            ys_ref[k, pl.ds(di, nb, stride=DFT_PITCH), :] = y[:, k * LANES:(k + 1) * LANES]
    yf = jnp.concatenate([ys_ref[k] for k in range(B_WIDTH // LANES)], axis=1)
    y_ref[...] = yf.reshape(nb, DFT_PITCH, B_WIDTH)[:, :DFT_ROWS].astype(BF16)


def _dft_two_stage_call(z):
    bsz, n, _ = z.shape
    na = 1 << (int(math.log2(n)) // 2)
    nb = n // na
    ia = np.arange(na)
    ib = np.arange(nb)
    ang_a = 2.0 * np.pi * np.outer(ia, ia) / na
    fr, fi = np.cos(ang_a), -np.sin(ang_a)
    m1 = _bf16_const(np.block([[fr, -fi], [fi, fr]]))
    ang_t = 2.0 * np.pi * np.outer(ib, ia) / n
    twr = jnp.asarray(np.cos(ang_t)[:, :, None], F32)
    twi = jnp.asarray(-np.sin(ang_t)[:, :, None], F32)
    ang_b = 2.0 * np.pi * np.outer(ib, ib) / nb
    m2 = _bf16_const(np.concatenate([np.cos(ang_b), np.sin(ang_b)], axis=1))

    s1 = pl.pallas_call(
        functools.partial(_dft_stage1_kernel, na=na),
        grid=(bsz, nb // DFT_ROWS),
        in_specs=[
            pl.BlockSpec((None, na, DFT_ROWS, 2 * B_WIDTH), lambda b, j: (b, 0, j, 0)),
            _const_spec((2 * na, 2 * na)),
            pl.BlockSpec((DFT_ROWS, na, 1), lambda b, j: (j, 0, 0)),
            pl.BlockSpec((DFT_ROWS, na, 1), lambda b, j: (j, 0, 0)),
        ],
        out_specs=pl.BlockSpec((None, na, DFT_ROWS, 2 * B_WIDTH), lambda b, j: (b, 0, j, 0)),
        out_shape=jax.ShapeDtypeStruct((bsz, na, nb, 2 * B_WIDTH), BF16),
        scratch_shapes=[pltpu.VMEM((2 * B_WIDTH // LANES, na * DFT_PITCH, LANES), F32)] * 2,
        compiler_params=_params(2),
        name="dft_stage1",
    )(z.reshape(bsz, na, nb, 2 * B_WIDTH), m1, twr, twi)
    y = pl.pallas_call(
        functools.partial(_dft_stage2_kernel, scale=float((n * GROUP_W) ** -0.5)),
        grid=(bsz, na // DFT_ROWS),
        in_specs=[
            pl.BlockSpec((None, DFT_ROWS, nb, 2 * B_WIDTH), lambda b, j: (b, j, 0, 0)),
            _const_spec((nb, 2 * nb)),
        ],
        out_specs=pl.BlockSpec((None, nb, DFT_ROWS, B_WIDTH), lambda b, j: (b, 0, j, 0)),
        out_shape=jax.ShapeDtypeStruct((bsz, nb, na, B_WIDTH), BF16),
        scratch_shapes=[pltpu.VMEM((B_WIDTH // LANES, nb * DFT_PITCH, LANES), F32)],
        compiler_params=_params(2),
        name="dft_stage2",
    )(s1, m2)
    return y.reshape(bsz, n, B_WIDTH)


FF_CHUNK = 1024


def _out_mlp_kernel(xa_ref, p1a_ref, p2a_ref, xb_ref, p1b_ref, p2b_ref, g1_ref, sh_ref, sc_ref, g2_ref,
                    ng_ref, wout_ref, w1_ref, w2_ref, oa_ref, ob_ref, *, steps_a):
    def body(x_ref, p1_ref, p2_ref, o_ref):
        half = p1_ref.shape[-1]
        mix = _dot(p1_ref[...], wout_ref[:half, :]) + _dot(p2_ref[...], wout_ref[half:, :])
        x1 = x_ref[...] + g1_ref[...] * mix
        h = _rms_mod(x1, ng_ref[...], sc_ref[...], sh_ref[...]).astype(BF16)
        acc = None
        for c in range(D_FF // FF_CHUNK):
            a = _dot(h, w1_ref[:, c * FF_CHUNK:(c + 1) * FF_CHUNK])
            a = jnp.square(jnp.maximum(a, 0.0)).astype(BF16)
            part = _dot(a, w2_ref[c * FF_CHUNK:(c + 1) * FF_CHUNK, :])
            acc = part if acc is None else acc + part
        o_ref[...] = x1 + g2_ref[...] * acc

    @pl.when(pl.program_id(0) < steps_a)
    def _():
        body(xa_ref, p1a_ref, p2a_ref, oa_ref)

    @pl.when(pl.program_id(0) >= steps_a)
    def _():
        body(xb_ref, p1b_ref, p2b_ref, ob_ref)


def _out_mlp_call(set_a, set_b, mod, layer, ng, w_out, out_idx, w1, w2):
    (xa, p1a, p2a, row_a), (xb, p1b, p2b, row_b) = set_a, set_b
    half = p1a.shape[-1]
    tm = min(xa.shape[1], xb.shape[1], MLP_TILE)
    tiles_a, tiles_b = xa.shape[1] // tm, xb.shape[1] // tm
    steps_a, steps_b = xa.shape[0] * tiles_a, xb.shape[0] * tiles_b

    def pos_a(s):
        sa = jnp.minimum(s, steps_a - 1)
        return sa // tiles_a, sa % tiles_a

    def pos_b(s):
        sb = jnp.maximum(s - steps_a, 0)
        return sb // tiles_b, sb % tiles_b

    tok_a = lambda w: pl.BlockSpec((None, tm, w), lambda s: pos_a(s) + (0,))
    tok_b = lambda w: pl.BlockSpec((None, tm, w), lambda s: pos_b(s) + (0,))
    row = lambda s: jnp.where(s < steps_a, row_a + pos_a(s)[0], row_b + pos_b(s)[0])
    mod_spec = lambda which: pl.BlockSpec((None, None, None, 1, D_MODEL),
                                          lambda s: (layer, row(s), which, 0, 0))
    return pl.pallas_call(
        functools.partial(_out_mlp_kernel, steps_a=steps_a),
        grid=(steps_a + steps_b,),
        in_specs=[
            tok_a(D_MODEL), tok_a(half), tok_a(half), tok_b(D_MODEL), tok_b(half), tok_b(half),
            mod_spec(2), mod_spec(3), mod_spec(4), mod_spec(5),
            _const_spec((1, D_MODEL)),
            _stacked_spec((2 * half, D_MODEL), out_idx),
            _stacked_spec((D_MODEL, D_FF), layer),
            _stacked_spec((D_FF, D_MODEL), layer),
        ],
        out_specs=[tok_a(D_MODEL), tok_b(D_MODEL)],
        out_shape=[jax.ShapeDtypeStruct(xa.shape, F32), jax.ShapeDtypeStruct(xb.shape, F32)],
        compiler_params=_params(1),
        name="out_mlp",
    )(xa, p1a, p2a, xb, p1b, p2b, mod, mod, mod, mod, ng, w_out, w1, w2)


def _odd_in_kernel(*refs, rope, emit_cache, tm):
    x_ref, sh_ref, sc_ref, g_ref, win_ref, gain_ref, hm_ref = refs[:7]
    k = 7
    if rope:
        cos_ref, sin_ref = refs[k:k + 2]
        k += 2
    cq_ref, ck_ref, cvt_ref, dq_ref, dk_ref, dvt_ref = refs[k:k + 6]
    k += 6
    if emit_cache:
        nck_ref, ncv_ref, ndk_ref, ndv_ref = refs[k:k + 4]
    subs = x_ref.shape[0] // tm
    lane = lax.broadcasted_iota(jnp.int32, (tm, LANES), 1)
    first_half = (lane % (2 * ROPE_FREQS)) < ROPE_FREQS
    pad_row = lax.broadcasted_iota(jnp.int32, (V_ROWS - LANES, tm), 0)
    pad = jnp.where(pad_row == 0, 1.0, 0.0).astype(BF16)

    def cache_rows(ref, sub, r):
        rpt = ref.shape[0] // x_ref.shape[0]
        return ref.at[pl.ds(sub * tm * rpt + r, tm, stride=rpt), :]

    def cache_cols(ref, sub, lead, mat):
        n_req = ref.shape[-1]
        for r in range(tm // n_req):
            ref[(sub * (tm // n_req) + r,) + lead] = mat[:, r * n_req:(r + 1) * n_req]

    groups = ((0, 2 * C_W), (2 * C_W, ODD_NORMED_W), (ODD_NORMED_W, ODD_W))
    hidden = {}

    def project(item):
        sub, grp = item
        if sub not in hidden:
            x = x_ref[sub * tm:(sub + 1) * tm, :]
            hidden[sub] = _rms_mod(x, g_ref[...], sc_ref[...], sh_ref[...]).astype(BF16)
        lo, hi = groups[grp]
        return _dot(hidden[sub], win_ref[:, lo:hi])

    def head_norms(p, lo):
        normed = []
        for c in range(p.shape[1] // (2 * LANES)):
            v = p[:, c * 2 * LANES:(c + 1) * 2 * LANES]
            ms = _dot((v * v).astype(BF16), hm_ref[...])
            y = v * lax.rsqrt(ms + EPS) * gain_ref[:, lo + c * 2 * LANES:lo + (c + 1) * 2 * LANES]
            normed += [y[:, :LANES], y[:, LANES:]]
        return normed

    def finish(item, p, normed):
        sub, grp = item
        rows = slice(sub * tm, (sub + 1) * tm)

        def roped(y):
            if not rope:
                return y
            rot = jnp.where(first_half, -pltpu.roll(y, LANES - ROPE_FREQS, 1),
                            pltpu.roll(y, ROPE_FREQS, 1))
            return y * cos_ref[rows, :] + rot * sin_ref[rows, :]

        def put(ref, chunks):
            for c, y in enumerate(chunks):
                ref[rows, c * LANES:(c + 1) * LANES] = roped(y).astype(BF16)

        nq = C_W // LANES
        if grp == 0:
            put(cq_ref, normed[:nq])
            put(ck_ref, normed[nq:])
            if emit_cache:
                for c, y in enumerate(normed[nq:]):
                    yt = y.T
                    for half in range(2):
                        cache_cols(nck_ref, sub, (c, half), yt[half * HEAD_DIM:(half + 1) * HEAD_DIM])
        elif grp == 1:
            put(dq_ref, normed[:nq])
            put(dk_ref, normed[nq:])
            if emit_cache:
                for j, y in enumerate(normed[nq:]):
                    cache_cols(ndk_ref, sub, (j,), y.T[:HEAD_DIM])
        else:
            cv, dv = p[:, :C_W], p[:, C_W:]
            for v, vt_ref in ((cv, cvt_ref), (dv, dvt_ref)):
                for hh in range(vt_ref.shape[0]):
                    vt = v[:, hh * LANES:(hh + 1) * LANES].T
                    vt_ref[hh, :LANES, rows] = vt.astype(BF16)
                    vt_ref[hh, LANES:, rows] = pad
                    if emit_cache and vt_ref is dvt_ref:
                        cache_cols(ndv_ref, sub, (hh,), vt[:HEAD_DIM])
            if emit_cache:
                for hh in range(C_HEADS):
                    cache_rows(ncv_ref, sub, hh)[...] = cv[:, hh * LANES:(hh + 1) * LANES]

    items = [(sub, grp) for sub in range(subs) for grp in range(len(groups))]
    p = project(items[0])
    for idx, item in enumerate(items):
        normed = head_norms(p, groups[item[1]][0]) if item[1] < 2 else None
        nxt = project(items[idx + 1]) if idx + 1 < len(items) else None
        finish(item, p, normed)
        p = nxt


def _odd_in_call(x, mod, layer, row_of_batch, g, w_in, gains, head_mean, rope_tabs, tm, cache_n=None):
    emit_cache = cache_n is not None
    bsz, n, _ = x.shape
    rope = rope_tabs is not None
    subs = 2 if n % (2 * tm) == 0 else 1
    ts = subs * tm
    tok = lambda w: pl.BlockSpec((None, ts, w), lambda b, i: (b, i, 0))
    in_specs = [
        tok(D_MODEL),
        _mod_spec(layer, 0, row_of_batch),
        _mod_spec(layer, 1, row_of_batch),
        _const_spec((1, D_MODEL)),
        _const_spec((D_MODEL, ODD_W)),
        _const_spec((1, ODD_NORMED_W)),
        _const_spec((2 * LANES, 2 * LANES)),
    ]
    args = [x, mod, mod, g, w_in, gains, head_mean]
    if rope:
        in_specs += [pl.BlockSpec((ts, LANES), lambda b, i: (i, 0))] * 2
        args += list(rope_tabs)
    vt_spec = lambda heads: pl.BlockSpec((None, heads, V_ROWS, ts), lambda b, i: (b, 0, 0, i))
    vt_shape = lambda heads: jax.ShapeDtypeStruct((bsz, heads, V_ROWS, n), BF16)
    tok_shape = lambda w: jax.ShapeDtypeStruct((bsz, n, w), BF16)
    out_specs = [tok(C_W), tok(C_W), vt_spec(C_HEADS), tok(DQ_W), tok(DKV_DUP_W), vt_spec(D_KV_HEADS)]
    out_shape = [tok_shape(C_W), tok_shape(C_W), vt_shape(C_HEADS), tok_shape(DQ_W),
                 tok_shape(DKV_DUP_W), vt_shape(D_KV_HEADS)]
    if emit_cache:
        assert bsz == 1 and tm % cache_n == 0 and n % cache_n == 0
        reqs, rps = n // cache_n, ts // cache_n
        for lead in ((C_HEADS, 2), None, (D_KV_HEADS,), (D_KV_HEADS,)):
            if lead is None:
                out_specs.append(pl.BlockSpec((None, ts * C_HEADS, LANES), lambda b, i: (b, i, 0)))
                out_shape.append(jax.ShapeDtypeStruct((bsz, n * C_HEADS, LANES), F32))
            else:
                zeros = (0,) * (len(lead) + 2)
                out_specs.append(pl.BlockSpec((rps,) + lead + (HEAD_DIM, cache_n),
                                              lambda b, i, zeros=zeros: (i,) + zeros))
                out_shape.append(jax.ShapeDtypeStruct((reqs,) + lead + (HEAD_DIM, cache_n), F32))
    return pl.pallas_call(
        functools.partial(_odd_in_kernel, rope=rope, emit_cache=emit_cache, tm=tm),
        grid=(bsz, n // ts),
        in_specs=in_specs,
        out_specs=out_specs,
        out_shape=out_shape,
        compiler_params=_params(2),
        name="odd_in",
    )(*args)


QK_AHEAD = 3


def _diff_attn_kernel(*refs, lam_init, tq, kc):
    n_src = (len(refs) - 4) // 2
    q_ref, (lamv_ref, g_ref, o_ref) = refs[0], refs[-3:]
    k_refs, vt_refs = refs[1:1 + n_src], refs[1 + n_src:1 + 2 * n_src]
    chunks = [(src, off) for src in range(n_src) for off in range(0, k_refs[src].shape[0], kc)]
    n_chunks = len(chunks)
    tiles = q_ref.shape[0] // tq
    heads = q_ref.shape[1] // LANES
    lane = lax.broadcasted_iota(jnp.int32, (tq, LANES), 1)
    lv = lamv_ref[...]
    lam = (jnp.exp(jnp.sum(lv[0:1] * lv[1:2], keepdims=True))
           - jnp.exp(jnp.sum(lv[2:3] * lv[3:4], keepdims=True)) + lam_init)

    qqs = {}

    def score(item):
        h, t, c = item
        cols = slice(h * LANES, (h + 1) * LANES)
        if (h, t) not in qqs:
            q = q_ref[t * tq:(t + 1) * tq, cols]
            zero = jnp.zeros_like(q)
            qqs[h, t] = jnp.concatenate([jnp.where(lane < HEAD_DIM, q, zero),
                                         jnp.where(lane >= HEAD_DIM, q, zero)], axis=0)
        src, off = chunks[c]
        return _dot_nt(k_refs[src][off:off + kc, cols], qqs[h, t])

    def finish(h, t, acc, l):
        inv = 1.0 / l
        ot = acc[:, :tq] * inv[:, :tq] - acc[:, tq:] * (inv[:, tq:] * lam)
        o = ot.T
        ms = jnp.mean(o * o, axis=-1, keepdims=True)
        y = o * lax.rsqrt(ms + EPS) * g_ref[...] * (1.0 - lam_init)
        o_ref[t * tq:(t + 1) * tq, h * LANES:(h + 1) * LANES] = y.astype(BF16)

    items = [(h, t, c) for h in range(heads) for t in range(tiles) for c in range(n_chunks)]
    ahead = [score(it) for it in items[:QK_AHEAD]]
    m = acc = None
    for idx, (h, t, c) in enumerate(items):
        s = ahead.pop(0)
        if idx + QK_AHEAD < len(items):
            ahead.append(score(items[idx + QK_AHEAD]))
        cm = jnp.max(s, axis=0, keepdims=True)
        m_new = cm if c == 0 else jnp.maximum(m, cm)
        src, off = chunks[c]
        e = jnp.exp2(s - m_new)
        lsum = jnp.sum(e, axis=0, keepdims=True)
        pv = _dot(vt_refs[src][h, :LANES, off:off + kc], e.astype(BF16))
        if c == 0:
            acc, l = pv, lsum
        else:
            alpha = jnp.exp2(m - m_new)
            acc, l = acc * alpha + pv, l * alpha + lsum
        m = m_new
        if c == n_chunks - 1:
            finish(h, t, acc, l)


def _diff_attn_call(q, ks, vts, lamv, subln_g, lam_init, tq, tiles, heads, flat_values=False):
    bsz, n, _ = q.shape
    kc = next(c for c in (512, 256, 128) if all(k.shape[1] % c == 0 for k in ks))
    hw = heads * LANES
    k_specs = [pl.BlockSpec((None, k.shape[1], hw), lambda b, h, i: (b, 0, h)) for k in ks]
    vt_index = (lambda b, h, i: (0, h, 0, b)) if flat_values else (lambda b, h, i: (b, h, 0, 0))
    vt_specs = [pl.BlockSpec((None, heads, V_ROWS, k.shape[1]), vt_index) for k in ks]
    return pl.pallas_call(
        functools.partial(_diff_attn_kernel, lam_init=lam_init, tq=tq, kc=kc),
        grid=(bsz, C_HEADS // heads, n // (tq * tiles)),
        in_specs=[pl.BlockSpec((None, tiles * tq, hw), lambda b, h, i: (b, i, h))]
        + k_specs + vt_specs + [_const_spec((4, HEAD_DIM)), _const_spec((1, LANES))],
        out_specs=pl.BlockSpec((None, tiles * tq, hw), lambda b, h, i: (b, i, h)),
        out_shape=jax.ShapeDtypeStruct((bsz, n, C_W), BF16),
        compiler_params=_params(3),
        name="diff_attn",
    )(q, *ks, *vts, lamv, subln_g)


def _sink_attn_kernel(*refs, windowed, tq, nblk):
    if windowed:
        (q_ref, ck_ref, kp_ref, km_ref, kn_ref, cvt_ref, vp_ref, vm_ref, vn_ref,
         bias_ref, sink_ref, o_ref) = refs
        kband = jnp.concatenate([kp_ref[...], km_ref[...], kn_ref[...]], axis=0)
        vtband = jnp.concatenate([vp_ref[...], vm_ref[...], vn_ref[...]], axis=1)
    else:
        q_ref, k_ref, vt_ref, sink_ref, o_ref = refs
    subs = q_ref.shape[0] // tq
    groups = q_ref.shape[1] // (2 * LANES)
    lane = lax.broadcasted_iota(jnp.int32, (tq, LANES), 1)
    lo = lane < HEAD_DIM

    def scores_of(stream):
        g, sub = stream
        qd = q_ref[sub * tq:(sub + 1) * tq, g * 2 * LANES:(g + 1) * 2 * LANES]
        parts = []
        for c in range(2):
            ch = qd[:, c * LANES:(c + 1) * LANES]
            zero = jnp.zeros_like(ch)
            parts += [jnp.where(lo, ch, zero), jnp.where(lo, zero, ch)]
        qq = jnp.concatenate(parts, axis=0)
        if not windowed:
            return [_dot_nt(k_ref[:, g * LANES:(g + 1) * LANES], qq)]
        blk = pl.program_id(2) * subs + sub
        variant = jnp.where(blk == 0, 0, jnp.where(blk == nblk - 1, 2, 1))
        band = _dot_nt(kband[sub * tq:(sub + 3) * tq, :], qq) + bias_ref[variant]
        return [_dot_nt(ck_ref[...], qq), band]

    def finish(stream, scores):
        g, sub = stream
        values = [cvt_ref[...], vtband[:, sub * tq:(sub + 3) * tq]] if windowed else [vt_ref[g]]
        sink = sink_ref[g]
        m = sink
        for s in scores:
            m = jnp.maximum(m, jnp.max(s, axis=0, keepdims=True))
        acc = None
        for s, vt in zip(scores, values):
            pv = _dot(vt, jnp.exp2(s - m).astype(BF16))
            acc = pv if acc is None else acc + pv
        inv = 1.0 / (acc[LANES:LANES + 1, :] + jnp.exp2(sink - m))
        ot = acc[:LANES, :] * inv
        for c in range(2):
            o0 = ot[:, (2 * c) * tq:(2 * c + 1) * tq].T
            o1 = ot[:, (2 * c + 1) * tq:(2 * c + 2) * tq].T
            cols = slice((2 * g + c) * LANES, (2 * g + c + 1) * LANES)
            o_ref[sub * tq:(sub + 1) * tq, cols] = jnp.where(lo, o0, o1).astype(BF16)

    streams = [(g, sub) for g in range(groups) for sub in range(subs)]
    ahead = [scores_of(st) for st in streams[:QK_AHEAD]]
    for idx, st in enumerate(streams):
        cur = ahead.pop(0)
        if idx + QK_AHEAD < len(streams):
            ahead.append(scores_of(streams[idx + QK_AHEAD]))
        finish(st, cur)


def _window_bias(n, tq):
    nblk = n // tq
    kk = np.arange(3 * tq)[:, None]
    qi = np.arange(tq)[None, :]
    band_ok = np.abs(qi + WINDOW - kk) <= WINDOW
    out = []
    for blk in (0, 1, nblk - 1):
        jpos = blk * tq - WINDOW + kk
        ok = band_ok & (jpos >= 0) & (jpos < n)
        out.append(np.tile(np.where(ok, 0.0, NEG_INF), (1, D_GROUP)))
    return jnp.asarray(np.stack(out), F32)


def _sink_attn_call(q, k, vt, sink_rows, ctx_k=None, ctx_vt=None, flat_values=False):
    bsz, n, _ = q.shape
    windowed = ctx_k is not None
    tq = WINDOW if windowed else n
    nblk = n // tq
    subs = next(s for s in (8, 4, 2, 1) if nblk % s == 0) if windowed else 1
    steps = nblk // subs
    groups = 1 if windowed else D_KV_HEADS
    q_spec = pl.BlockSpec((None, subs * tq, groups * 2 * LANES), lambda b, j, i: (b, i, j))
    sink_spec = pl.BlockSpec((groups, 1, D_GROUP * tq), lambda b, j, i: (j, 0, 0))
    if windowed:
        assert nblk >= 3, "first / interior / last mask variants need three query blocks"
        n_ctx = ctx_k.shape[1]
        prev = lambda i: jnp.maximum(i * subs - 1, 0)
        nxt = lambda i: jnp.minimum((i + 1) * subs, nblk - 1)
        in_specs = [
            q_spec,
            pl.BlockSpec((None, n_ctx, LANES), lambda b, j, i: (b, 0, j)),
            pl.BlockSpec((None, tq, LANES), lambda b, j, i: (b, prev(i), j)),
            pl.BlockSpec((None, subs * tq, LANES), lambda b, j, i: (b, i, j)),
            pl.BlockSpec((None, tq, LANES), lambda b, j, i: (b, nxt(i), j)),
            pl.BlockSpec((None, None, V_ROWS, n_ctx), lambda b, j, i: (b, j, 0, 0)),
            pl.BlockSpec((None, None, V_ROWS, tq), lambda b, j, i: (b, j, 0, prev(i))),
            pl.BlockSpec((None, None, V_ROWS, subs * tq), lambda b, j, i: (b, j, 0, i)),
            pl.BlockSpec((None, None, V_ROWS, tq), lambda b, j, i: (b, j, 0, nxt(i))),
            _const_spec((3, 3 * tq, D_GROUP * tq)),
            sink_spec,
        ]
        args = (q, ctx_k, k, k, k, ctx_vt, vt, vt, vt, _window_bias(n, tq), sink_rows)
    else:
        in_specs = [
            q_spec,
            pl.BlockSpec((None, n, groups * LANES), lambda b, j, i: (b, 0, j)),
            pl.BlockSpec((None, groups, V_ROWS, n),
                         (lambda b, j, i: (0, j, 0, b)) if flat_values else (lambda b, j, i: (b, j, 0, 0))),
            sink_spec,
        ]
        args = (q, k, vt, sink_rows)
    return pl.pallas_call(
        functools.partial(_sink_attn_kernel, windowed=windowed, tq=tq, nblk=nblk),
        grid=(bsz, D_KV_HEADS // groups, steps),
        in_specs=in_specs,
        out_specs=q_spec,
        out_shape=jax.ShapeDtypeStruct((bsz, n, DQ_W), BF16),
        compiler_params=_params(3),
        name="sink_attn",
    )(*args)


def _dup_heads(a):
    lead = a.shape[:-1]
    a = a.reshape(lead + (D_KV_HEADS, 1, HEAD_DIM))
    return jnp.broadcast_to(a, lead + (D_KV_HEADS, 2, HEAD_DIM)).reshape(lead + (DKV_DUP_W,))


def _odd_weights(w_in_odd, c_qn_g, c_kn_g, d_qn_g, d_kn_g):
    cq, ck, cv, dq, dk, dv = jnp.split(
        w_in_odd, (C_W, 2 * C_W, 3 * C_W, 3 * C_W + DQ_W, 3 * C_W + DQ_W + DKV_W), axis=-1)
    w = jnp.concatenate([cq, ck, dq, _dup_heads(dk), cv, _dup_heads(dv)], axis=-1).astype(BF16)
    gains = jnp.concatenate([
        jnp.tile(c_qn_g * (ATTN_SCALE * LOG2E), C_W // HEAD_DIM),
        jnp.tile(c_kn_g, C_W // HEAD_DIM),
        jnp.tile(d_qn_g * (ATTN_SCALE * LOG2E), DQ_W // HEAD_DIM),
        jnp.tile(d_kn_g, DKV_DUP_W // HEAD_DIM),
    ])[None, :]
    return w, gains


def _value_rows(v):
    bsz, nk, w = v.shape
    vt = v.reshape(bsz, nk, w // LANES, LANES).transpose(0, 2, 3, 1)
    pad = jnp.zeros((bsz, w // LANES, V_ROWS - LANES, nk), v.dtype).at[:, :, 0, :].set(1)
    return jnp.concatenate([vt, pad], axis=2)


def _rope_tables(n):
    rows = n // GRID_W
    row = np.repeat(np.arange(rows), GRID_W).astype(np.float64)
    col = np.tile(np.arange(GRID_W), rows).astype(np.float64)
    freqs = ROPE_BASE ** (-np.arange(ROPE_FREQS, dtype=np.float64) / ROPE_FREQS)
    ang = np.stack([row[:, None] * freqs, col[:, None] * freqs], axis=1)
    ang = np.concatenate([ang, ang], axis=-1).reshape(n, HEAD_DIM)
    ang = np.concatenate([ang, ang], axis=-1)
    return jnp.asarray(np.cos(ang), F32), jnp.asarray(np.sin(ang), F32)


MLP_TILE = 512


def _token_tile(n):
    return min(n, 512)


def _mixer(x, l, mod, row_of_batch, wts, caches, new_cache):
    bsz, n, _ = x.shape
    decode = caches is not None
    g1n = wts["norm1_g"][l][None, :]
    if l % 2 == 0:
        e = l // 2
        a, z = _even_in_call(x, mod, l, row_of_batch, g1n, e, wts["w_in_even"], wts["w_pool"],
                             wts["pool_scale"][e][None, :], wts["dftc"], wts["w_fft"], min(n, 1024),
                             1 if decode or n > 1024 else next(q for q in (4, 2, 1) if bsz % q == 0))
        y = _dft_two_stage_call(z) if n > DFT_DIRECT_MAX else _dft_direct_call(z)
        return a, y
    o = l // 2
    lam_init = _lam_init(l)
    rope_tabs = _rope_tables(n) if decode else None
    xo = x if decode else x.reshape(1, bsz * n, D_MODEL)
    outs = _odd_in_call(xo, mod, l, row_of_batch, g1n, wts["w_in_odd"][o], wts["odd_gains"][o],
                        wts["head_mean"], rope_tabs, _token_tile(xo.shape[1]),
                        None if decode else n)
    if not decode:
        outs = [a if a.ndim == 4 else a.reshape(bsz, n, a.shape[-1]) for a in outs[:6]] + list(outs[6:])
    cq, ck, cvt, dq, dkk, dvt = outs[:6]
    if decode:
        c_k, c_v, d_k, d_v = caches
        lc = c_k.shape[2]
        ks = [c_k[:, o].reshape(bsz, lc, C_W).astype(BF16), ck]
        vts = [_value_rows(c_v[:, o].reshape(bsz, lc, C_W).astype(BF16)), cvt]
        ctx_k = _dup_heads(d_k[:, o].reshape(bsz, lc, DKV_W)).astype(BF16)
        ctx_vt = _value_rows(_dup_heads(d_v[:, o].reshape(bsz, lc, DKV_W)).astype(BF16))
        tq = min(n, 256)
    else:
        new_cache.append(outs[6:])
        ks, vts = [ck], [cvt]
        ctx_k = ctx_vt = None
        tq = n
    tiles = 2 if n % (2 * tq) == 0 else 1
    heads = 1 if decode else C_HEADS
    c_out = _diff_attn_call(cq, ks, vts, wts["lamv"][o],
                            wts["c_subln_g"][o][None, :], lam_init, tq, tiles, heads, not decode)
    sink_tq = WINDOW if decode else n
    sink_rows = jnp.repeat(wts["d_sink"][o].reshape(D_KV_HEADS, D_GROUP) * LOG2E, sink_tq,
                           axis=-1)[:, None, :]
    d_out = _sink_attn_call(dq, dkk, dvt, sink_rows, ctx_k, ctx_vt, not decode)
    return c_out, d_out


def kernel(x_prompt, x_sample, c, cache_c_k, cache_c_v, cache_d_k, cache_d_v, c_ctx,
           norm1_g, norm2_g, w_ada, b_ada, w_in_even, w_pool, pool_scale, w_fft, w_out_even,
           w_in_odd, c_qn_g, c_kn_g, lam_q1, lam_k1, lam_q2, lam_k2, c_subln_g,
           d_qn_g, d_kn_g, d_sink, w_out_odd, w_mlp1, w_mlp2):
    depth = norm1_g.shape[0]
    n_odd = w_in_odd.shape[0]
    dec_b = c.shape[0]

    rows = -(-(1 + dec_b) // SUBLANES) * SUBLANES
    cond = jnp.zeros((rows, D_MODEL), F32).at[0].set(c_ctx).at[1:1 + dec_b].set(c)
    mod = _adaln_call(cond, w_ada, b_ada).reshape(depth, rows, 6, 1, D_MODEL)

    ic = np.arange(GROUP_W)
    ang = 2.0 * np.pi * np.outer(ic, ic) / GROUP_W
    dftc = _bf16_const(np.concatenate([np.cos(ang), -np.sin(ang)], axis=1))
    head_mean = _bf16_const(np.kron(np.eye(2 * LANES // HEAD_DIM), np.ones((HEAD_DIM, HEAD_DIM))) / HEAD_DIM)

    odd = [_odd_weights(w_in_odd[o], c_qn_g[o], c_kn_g[o], d_qn_g[o], d_kn_g[o]) for o in range(n_odd)]
    bf16 = lambda w: w.astype(BF16)
    wts = {
        "norm1_g": norm1_g, "norm2_g": norm2_g,
        "w_in_even": bf16(w_in_even), "w_pool": bf16(w_pool), "pool_scale": pool_scale,
        "w_fft": bf16(w_fft), "w_out_even": bf16(w_out_even), "dftc": dftc,
        "w_in_odd": [w for w, _ in odd], "odd_gains": [g for _, g in odd], "head_mean": head_mean,
        "lamv": jnp.stack([lam_q1, lam_k1, lam_q2, lam_k2], axis=1),
        "c_subln_g": c_subln_g, "d_sink": d_sink, "w_out_odd": bf16(w_out_odd),
        "w_mlp1": bf16(w_mlp1), "w_mlp2": bf16(w_mlp2),
    }

    bsz, n, _ = x_prompt.shape
    caches = (cache_c_k, cache_c_v, cache_d_k, cache_d_v)
    new_cache = []
    xp, xs = x_prompt, x_sample
    for l in range(depth):
        pp = _mixer(xp, l, mod, lambda b: 0, wts, None, new_cache)
        ps = _mixer(xs, l, mod, lambda b: 1 + b, wts, caches, new_cache)
        w_out, out_idx = (wts["w_out_even"], l // 2) if l % 2 == 0 else (wts["w_out_odd"], l // 2)
        flat = lambda a: a.reshape(1, bsz * n, a.shape[-1])
        xp, xs = _out_mlp_call((flat(xp), flat(pp[0]), flat(pp[1]), 0), (xs, ps[0], ps[1], 1),
                               mod, l, wts["norm2_g"][l][None, :], w_out, out_idx,
                               wts["w_mlp1"], wts["w_mlp2"])
        xp = xp.reshape(bsz, n, D_MODEL)
    y_prompt, y_sample = xp, xs
    new_c_k = jnp.stack([nc[0].transpose(0, 4, 1, 2, 3) for nc in new_cache], axis=1)
    new_c_v = jnp.stack([nc[1].reshape(bsz, n, C_HEADS, 2 * HEAD_DIM) for nc in new_cache], axis=1)
    new_d_k = jnp.stack([nc[2].transpose(0, 3, 1, 2) for nc in new_cache], axis=1)
    new_d_v = jnp.stack([nc[3].transpose(0, 3, 1, 2) for nc in new_cache], axis=1)
    return (y_prompt, y_sample, new_c_k, new_c_v, new_d_k, new_d_v)
```

```python
import functools
import math

import numpy as np
import jax
import jax.numpy as jnp
from jax import lax
from jax.experimental import pallas as pl
from jax.experimental.pallas import tpu as pltpu

F32 = jnp.float32
BF16 = jnp.bfloat16

D_MODEL = 1024
HEAD_DIM = 64
LANES = 128
SUBLANES = 8
GRID_W = 64
ROPE_FREQS = HEAD_DIM // 4
ROPE_BASE = 10000.0
EPS = 1e-6
NEG_INF = -1e30
WINDOW = 128
A_WIDTH = D_MODEL // 2
B_WIDTH = D_MODEL // 2
POOL_WINDOWS = (2, 4, 8, 16)
POOL_HALO = 8
N_GROUPS = 4
GROUP_W = A_WIDTH // N_GROUPS
C_HEADS = 4
D_HEADS = 8
D_KV_HEADS = 2
D_GROUP = D_HEADS // D_KV_HEADS
C_W = C_HEADS * 2 * HEAD_DIM
DQ_W = D_HEADS * HEAD_DIM
DKV_W = D_KV_HEADS * HEAD_DIM
DKV_DUP_W = 2 * DKV_W
D_FF = 4 * D_MODEL
ATTN_SCALE = HEAD_DIM ** -0.5
LOG2E = math.log2(math.e)
V_ROWS = LANES + 16
ODD_NORMED_W = C_W + C_W + DQ_W + DKV_DUP_W
ODD_W = ODD_NORMED_W + C_W + DKV_DUP_W
VMEM_LIMIT = 56 * 1024 * 1024


def _lam_init(layer):
    return 0.8 - 0.6 * math.exp(-0.3 * layer)


def _params(n_axes):
    return pltpu.CompilerParams(dimension_semantics=("arbitrary",) * n_axes,
                                vmem_limit_bytes=VMEM_LIMIT)


def _const_spec(shape):
    nd = len(shape)
    return pl.BlockSpec(shape, lambda *_: (0,) * nd, pipeline_mode=pl.Buffered(1))


def _stacked_spec(shape, idx):
    nd = len(shape)
    return pl.BlockSpec((None,) + tuple(shape), lambda *_: (idx,) + (0,) * nd, pipeline_mode=pl.Buffered(1))


def _mod_spec(layer, which, row_of_batch):
    return pl.BlockSpec((None, None, None, 1, D_MODEL),
                        lambda b, i: (layer, row_of_batch(b), which, 0, 0))


def _rms_mod(x, g, sc, sh):
    ms = jnp.mean(x * x, axis=-1, keepdims=True)
    return (x * lax.rsqrt(ms + EPS) * g) * (1.0 + sc) + sh


def _bf16_const(a):
    return jnp.asarray(a, F32).astype(BF16)


def _dot_nt(a, b):
    return lax.dot_general(a, b, (((1,), (1,)), ((), ())), preferred_element_type=F32)


def _dot(a, b):
    return jnp.dot(a, b, preferred_element_type=F32)


def _adaln_kernel(cond_ref, w_ref, b_ref, o_ref):
    cnd = cond_ref[...]
    s = (cnd * jax.nn.sigmoid(cnd)).astype(BF16)
    o_ref[...] = _dot(s, w_ref[...].astype(BF16)) + b_ref[...]


def _adaln_call(cond, w_ada, b_ada):
    depth = w_ada.shape[0]
    rows = cond.shape[0]
    tn = 1536
    return pl.pallas_call(
        _adaln_kernel,
        grid=(depth, 6 * D_MODEL // tn),
        in_specs=[
            pl.BlockSpec((rows, D_MODEL), lambda l, j: (0, 0)),
            pl.BlockSpec((None, D_MODEL, tn), lambda l, j: (l, 0, j)),
            pl.BlockSpec((None, 1, tn), lambda l, j: (l, 0, j)),
        ],
        out_specs=pl.BlockSpec((None, rows, tn), lambda l, j: (l, 0, j)),
        out_shape=jax.ShapeDtypeStruct((depth, rows, 6 * D_MODEL), F32),
        compiler_params=_params(2),
        name="adaln",
    )(cond, w_ada, b_ada.reshape(depth, 1, 6 * D_MODEL))


def _even_in_kernel(x_ref, xp_ref, xn_ref, sh_ref, sc_ref, g_ref, win_ref, wpool_ref,
                    pscale_ref, dftc_ref, wfft_ref, a_ref, z_ref, *, tm, n):
    i = pl.program_id(1)
    n_tiles = n // tm
    g, sc, sh = g_ref[...], sc_ref[...], sh_ref[...]
    seqs = x_ref.shape[0]
    h = _rms_mod(x_ref[...].reshape(seqs * tm, D_MODEL), g, sc, sh).astype(BF16)
    u = _dot(h, win_ref[...])
    xh = jnp.concatenate([xp_ref[...], xn_ref[...]], axis=0)
    hh = _rms_mod(xh, g, sc, sh).astype(BF16)
    uh = _dot(hh, win_ref[:, :A_WIDTH])
    has_prev = (i > 0).astype(F32)
    has_next = (i < n_tiles - 1).astype(F32)
    rows = tm + 2 * POOL_HALO

    for gi in range(N_GROUPS):
        ub = u[:, A_WIDTH + gi * GROUP_W:A_WIDTH + (gi + 1) * GROUP_W].astype(BF16)
        cs = _dot(ub, dftc_ref[...])
        zr = _dot(cs[:, :GROUP_W].astype(BF16), wfft_ref[gi])
        zi = _dot(cs[:, GROUP_W:].astype(BF16), wfft_ref[gi])
        for q in range(seqs):
            z_ref[q, :, gi * GROUP_W:(gi + 1) * GROUP_W] = zr[q * tm:(q + 1) * tm].astype(BF16)
            z_ref[q, :, B_WIDTH + gi * GROUP_W:B_WIDTH + (gi + 1) * GROUP_W] = zi[q * tm:(q + 1) * tm].astype(BF16)

    t = i * tm + lax.broadcasted_iota(jnp.int32, (tm, 1), 0)
    for q, (gi, w) in ((q, gw) for q in range(seqs) for gw in enumerate(POOL_WINDOWS)):
        cols = slice(gi * GROUP_W, (gi + 1) * GROUP_W)
        uq = u[q * tm:(q + 1) * tm, cols]
        ext = jnp.concatenate([uh[0:POOL_HALO, cols] * has_prev, uq,
                               uh[POOL_HALO:, cols] * has_next], axis=0)
        s = ext
        span = 1
        while span < w:
            s = s + pltpu.roll(s, span, 0)
            span *= 2
        shift = w // 2 - 1
        s = pltpu.roll(s, rows - shift, 0) if shift else s
        acc = s[POOL_HALO:POOL_HALO + tm]
        lo = jnp.maximum(t - w // 2, 0)
        hi = jnp.minimum(t + w // 2, n)
        cnt = (hi - lo).astype(F32)
        pooled = acc / cnt - uq
        y = _dot(pooled.astype(BF16), wpool_ref[gi]) * pscale_ref[:, cols]
        a_ref[q, :, cols] = y.astype(BF16)


def _even_in_call(x, mod, layer, row_of_batch, g, e, w_in, w_pool, pool_scale, dftc, w_fft, tm, seqs):
    bsz, n, _ = x.shape
    assert seqs == 1 or tm == n
    hb = tm // POOL_HALO
    kern = functools.partial(_even_in_kernel, tm=tm, n=n)
    return pl.pallas_call(
        kern,
        grid=(bsz // seqs, n // tm),
        in_specs=[
            pl.BlockSpec((seqs, tm, D_MODEL), lambda b, i: (b, i, 0)),
            pl.BlockSpec((None, POOL_HALO, D_MODEL),
                         lambda b, i: (b * seqs, jnp.maximum(i * hb - 1, 0), 0)),
            pl.BlockSpec((None, POOL_HALO, D_MODEL),
                         lambda b, i: (b * seqs, jnp.minimum((i + 1) * hb, n // POOL_HALO - 1), 0)),
            _mod_spec(layer, 0, row_of_batch),
            _mod_spec(layer, 1, row_of_batch),
            _const_spec((1, D_MODEL)),
            _stacked_spec((D_MODEL, D_MODEL), e),
            _stacked_spec((N_GROUPS, GROUP_W, GROUP_W), e),
            _const_spec((1, A_WIDTH)),
            _const_spec((GROUP_W, 2 * GROUP_W)),
            _stacked_spec((N_GROUPS, GROUP_W, GROUP_W), e),
        ],
        out_specs=[
            pl.BlockSpec((seqs, tm, A_WIDTH), lambda b, i: (b, i, 0)),
            pl.BlockSpec((seqs, tm, 2 * B_WIDTH), lambda b, i: (b, i, 0)),
        ],
        out_shape=[
            jax.ShapeDtypeStruct((bsz, n, A_WIDTH), BF16),
            jax.ShapeDtypeStruct((bsz, n, 2 * B_WIDTH), BF16),
        ],
        compiler_params=_params(2),
        name="even_in",
    )(x, x, x, mod, mod, g, w_in, w_pool, pool_scale, dftc, w_fft)


def _dft_direct_kernel(z_ref, m_ref, y_ref, *, scale):
    for q in range(z_ref.shape[0]):
        z = z_ref[q]
        rhs = jnp.concatenate([z[:, :B_WIDTH], z[:, B_WIDTH:]], axis=0)
        y_ref[q] = (_dot(m_ref[...], rhs) * scale).astype(BF16)


def _dft_direct_call(z):
    bsz, n, _ = z.shape
    k = np.arange(n)
    ang = 2.0 * np.pi * np.outer(k, k) / n
    m = _bf16_const(np.concatenate([np.cos(ang), np.sin(ang)], axis=1))
    kern = functools.partial(_dft_direct_kernel, scale=float((n * GROUP_W) ** -0.5))
    seqs = next(q for q in (4, 2, 1) if bsz % q == 0)
    return pl.pallas_call(
        kern,
        grid=(bsz // seqs,),
        in_specs=[pl.BlockSpec((seqs, n, 2 * B_WIDTH), lambda b: (b, 0, 0)),
                  _const_spec((n, 2 * n))],
        out_specs=pl.BlockSpec((seqs, n, B_WIDTH), lambda b: (b, 0, 0)),
        out_shape=jax.ShapeDtypeStruct((bsz, n, B_WIDTH), BF16),
        compiler_params=_params(1),
        name="dft_direct",
    )(z, m)


DFT_ROWS = 16
DFT_DIRECT_MAX = 256
DFT_PITCH = 24


def _pitch_rows(x):
    g, _, w = x.shape
    pad = jnp.zeros((g, DFT_PITCH - DFT_ROWS, w), x.dtype)
    return jnp.concatenate([x, pad], axis=1).reshape(g * DFT_PITCH, w)


def _dft_stage1_kernel(z_ref, m_ref, twr_ref, twi_ref, o_ref, zs_ref, os_ref, *, na):
    nt = 2 * B_WIDTH // LANES

    @pl.when((pl.program_id(0) == 0) & (pl.program_id(1) == 0))
    def _():
        os_ref[...] = jnp.zeros_like(os_ref)

    zf = _pitch_rows(z_ref[...].astype(F32))
    for k in range(nt):
        zs_ref[k] = zf[:, k * LANES:(k + 1) * LANES]
    for bi in range(DFT_ROWS):
        rows = pl.ds(bi, na, stride=DFT_PITCH)
        zb = [zs_ref[k, rows, :].astype(BF16) for k in range(nt)]
        rhs = jnp.concatenate([jnp.concatenate(zb[:nt // 2], axis=1),
                               jnp.concatenate(zb[nt // 2:], axis=1)], axis=0)
        r = _dot(m_ref[...], rhs)
        br, bim = r[:na], r[na:]
        tr, ti = twr_ref[bi], twi_ref[bi]
        out = (br * tr - bim * ti, br * ti + bim * tr)
        for k in range(nt):
            half, kk = divmod(k, nt // 2)
            os_ref[k, rows, :] = out[half][:, kk * LANES:(kk + 1) * LANES]
    of = jnp.concatenate([os_ref[k] for k in range(nt)], axis=1)
    o_ref[...] = of.reshape(na, DFT_PITCH, 2 * B_WIDTH)[:, :DFT_ROWS].astype(BF16)


def _dft_stage2_kernel(b_ref, m_ref, y_ref, ys_ref, *, scale):
    nb = b_ref.shape[1]

    @pl.when((pl.program_id(0) == 0) & (pl.program_id(1) == 0))
    def _():
        ys_ref[...] = jnp.zeros_like(ys_ref)

    for di in range(DFT_ROWS):
        bb = b_ref[di]
        rhs = jnp.concatenate([bb[:, :B_WIDTH], bb[:, B_WIDTH:]], axis=0)
        y = _dot(m_ref[...], rhs) * scale
        for k in range(B_WIDTH // LANES):
            ys_ref[k, pl.ds(di, nb, stride=DFT_PITCH), :] = y[:, k * LANES:(k + 1) * LANES]
    yf = jnp.concatenate([ys_ref[k] for k in range(B_WIDTH // LANES)], axis=1)
    y_ref[...] = yf.reshape(nb, DFT_PITCH, B_WIDTH)[:, :DFT_ROWS].astype(BF16)


def _dft_two_stage_call(z):
    bsz, n, _ = z.shape
    na = 1 << (int(math.log2(n)) // 2)
    nb = n // na
    ia = np.arange(na)
    ib = np.arange(nb)
    ang_a = 2.0 * np.pi * np.outer(ia, ia) / na
    fr, fi = np.cos(ang_a), -np.sin(ang_a)
    m1 = _bf16_const(np.block([[fr, -fi], [fi, fr]]))
    ang_t = 2.0 * np.pi * np.outer(ib, ia) / n
    twr = jnp.asarray(np.cos(ang_t)[:, :, None], F32)
    twi = jnp.asarray(-np.sin(ang_t)[:, :, None], F32)
    ang_b = 2.0 * np.pi * np.outer(ib, ib) / nb
    m2 = _bf16_const(np.concatenate([np.cos(ang_b), np.sin(ang_b)], axis=1))

    s1 = pl.pallas_call(
        functools.partial(_dft_stage1_kernel, na=na),
        grid=(bsz, nb // DFT_ROWS),
        in_specs=[
            pl.BlockSpec((None, na, DFT_ROWS, 2 * B_WIDTH), lambda b, j: (b, 0, j, 0)),
            _const_spec((2 * na, 2 * na)),
            pl.BlockSpec((DFT_ROWS, na, 1), lambda b, j: (j, 0, 0)),
            pl.BlockSpec((DFT_ROWS, na, 1), lambda b, j: (j, 0, 0)),
        ],
        out_specs=pl.BlockSpec((None, na, DFT_ROWS, 2 * B_WIDTH), lambda b, j: (b, 0, j, 0)),
        out_shape=jax.ShapeDtypeStruct((bsz, na, nb, 2 * B_WIDTH), BF16),
        scratch_shapes=[pltpu.VMEM((2 * B_WIDTH // LANES, na * DFT_PITCH, LANES), F32)] * 2,
        compiler_params=_params(2),
        name="dft_stage1",
    )(z.reshape(bsz, na, nb, 2 * B_WIDTH), m1, twr, twi)
    y = pl.pallas_call(
        functools.partial(_dft_stage2_kernel, scale=float((n * GROUP_W) ** -0.5)),
        grid=(bsz, na // DFT_ROWS),
        in_specs=[
            pl.BlockSpec((None, DFT_ROWS, nb, 2 * B_WIDTH), lambda b, j: (b, j, 0, 0)),
            _const_spec((nb, 2 * nb)),
        ],
        out_specs=pl.BlockSpec((None, nb, DFT_ROWS, B_WIDTH), lambda b, j: (b, 0, j, 0)),
        out_shape=jax.ShapeDtypeStruct((bsz, nb, na, B_WIDTH), BF16),
        scratch_shapes=[pltpu.VMEM((B_WIDTH // LANES, nb * DFT_PITCH, LANES), F32)],
        compiler_params=_params(2),
        name="dft_stage2",
    )(s1, m2)
    return y.reshape(bsz, n, B_WIDTH)


FF_CHUNK = 1024


def _out_mlp_kernel(xa_ref, p1a_ref, p2a_ref, xb_ref, p1b_ref, p2b_ref, g1_ref, sh_ref, sc_ref, g2_ref,
                    ng_ref, wout_ref, w1_ref, w2_ref, oa_ref, ob_ref, *, steps_a):
    def body(x_ref, p1_ref, p2_ref, o_ref):
        half = p1_ref.shape[-1]
        mix = _dot(p1_ref[...], wout_ref[:half, :]) + _dot(p2_ref[...], wout_ref[half:, :])
        x1 = x_ref[...] + g1_ref[...] * mix
        h = _rms_mod(x1, ng_ref[...], sc_ref[...], sh_ref[...]).astype(BF16)
        acc = None
        for c in range(D_FF // FF_CHUNK):
            a = _dot(h, w1_ref[:, c * FF_CHUNK:(c + 1) * FF_CHUNK])
            a = jnp.square(jnp.maximum(a, 0.0)).astype(BF16)
            part = _dot(a, w2_ref[c * FF_CHUNK:(c + 1) * FF_CHUNK, :])
            acc = part if acc is None else acc + part
        o_ref[...] = x1 + g2_ref[...] * acc

    @pl.when(pl.program_id(0) < steps_a)
    def _():
        body(xa_ref, p1a_ref, p2a_ref, oa_ref)

    @pl.when(pl.program_id(0) >= steps_a)
    def _():
        body(xb_ref, p1b_ref, p2b_ref, ob_ref)


def _out_mlp_call(set_a, set_b, mod, layer, ng, w_out, out_idx, w1, w2):
    (xa, p1a, p2a, row_a), (xb, p1b, p2b, row_b) = set_a, set_b
    half = p1a.shape[-1]
    tm = min(xa.shape[1], xb.shape[1], MLP_TILE)
    tiles_a, tiles_b = xa.shape[1] // tm, xb.shape[1] // tm
    steps_a, steps_b = xa.shape[0] * tiles_a, xb.shape[0] * tiles_b

    def pos_a(s):
        sa = jnp.minimum(s, steps_a - 1)
        return sa // tiles_a, sa % tiles_a

    def pos_b(s):
        sb = jnp.maximum(s - steps_a, 0)
        return sb // tiles_b, sb % tiles_b

    tok_a = lambda w: pl.BlockSpec((None, tm, w), lambda s: pos_a(s) + (0,))
    tok_b = lambda w: pl.BlockSpec((None, tm, w), lambda s: pos_b(s) + (0,))
    row = lambda s: jnp.where(s < steps_a, row_a + pos_a(s)[0], row_b + pos_b(s)[0])
    mod_spec = lambda which: pl.BlockSpec((None, None, None, 1, D_MODEL),
                                          lambda s: (layer, row(s), which, 0, 0))
    return pl.pallas_call(
        functools.partial(_out_mlp_kernel, steps_a=steps_a),
        grid=(steps_a + steps_b,),
        in_specs=[
            tok_a(D_MODEL), tok_a(half), tok_a(half), tok_b(D_MODEL), tok_b(half), tok_b(half),
            mod_spec(2), mod_spec(3), mod_spec(4), mod_spec(5),
            _const_spec((1, D_MODEL)),
            _stacked_spec((2 * half, D_MODEL), out_idx),
            _stacked_spec((D_MODEL, D_FF), layer),
            _stacked_spec((D_FF, D_MODEL), layer),
        ],
        out_specs=[tok_a(D_MODEL), tok_b(D_MODEL)],
        out_shape=[jax.ShapeDtypeStruct(xa.shape, F32), jax.ShapeDtypeStruct(xb.shape, F32)],
        compiler_params=_params(1),
        name="out_mlp",
    )(xa, p1a, p2a, xb, p1b, p2b, mod, mod, mod, mod, ng, w_out, w1, w2)


def _odd_in_kernel(*refs, rope, emit_cache, tm):
    x_ref, sh_ref, sc_ref, g_ref, win_ref, gain_ref, hm_ref = refs[:7]
    k = 7
    if rope:
        cos_ref, sin_ref = refs[k:k + 2]
        k += 2
    cq_ref, ck_ref, cvt_ref, dq_ref, dk_ref, dvt_ref = refs[k:k + 6]
    k += 6
    if emit_cache:
        nck_ref, ncv_ref, ndk_ref, ndv_ref = refs[k:k + 4]
    subs = x_ref.shape[0] // tm
    lane = lax.broadcasted_iota(jnp.int32, (tm, LANES), 1)
    first_half = (lane % (2 * ROPE_FREQS)) < ROPE_FREQS
    pad_row = lax.broadcasted_iota(jnp.int32, (V_ROWS - LANES, tm), 0)
    pad = jnp.where(pad_row == 0, 1.0, 0.0).astype(BF16)

    def cache_rows(ref, sub, r):
        rpt = ref.shape[0] // x_ref.shape[0]
        return ref.at[pl.ds(sub * tm * rpt + r, tm, stride=rpt), :]

    def cache_cols(ref, sub, lead, mat):
        n_req = ref.shape[-1]
        for r in range(tm // n_req):
            ref[(sub * (tm // n_req) + r,) + lead] = mat[:, r * n_req:(r + 1) * n_req]

    groups = ((0, 2 * C_W), (2 * C_W, ODD_NORMED_W), (ODD_NORMED_W, ODD_W))
    hidden = {}

    def project(item):
        sub, grp = item
        if sub not in hidden:
            x = x_ref[sub * tm:(sub + 1) * tm, :]
            hidden[sub] = _rms_mod(x, g_ref[...], sc_ref[...], sh_ref[...]).astype(BF16)
        lo, hi = groups[grp]
        return _dot(hidden[sub], win_ref[:, lo:hi])

    def head_norms(p, lo):
        normed = []
        for c in range(p.shape[1] // (2 * LANES)):
            v = p[:, c * 2 * LANES:(c + 1) * 2 * LANES]
            ms = _dot((v * v).astype(BF16), hm_ref[...])
            y = v * lax.rsqrt(ms + EPS) * gain_ref[:, lo + c * 2 * LANES:lo + (c + 1) * 2 * LANES]
            normed += [y[:, :LANES], y[:, LANES:]]
        return normed

    def finish(item, p, normed):
        sub, grp = item
        rows = slice(sub * tm, (sub + 1) * tm)

        def roped(y):
            if not rope:
                return y
            rot = jnp.where(first_half, -pltpu.roll(y, LANES - ROPE_FREQS, 1),
                            pltpu.roll(y, ROPE_FREQS, 1))
            return y * cos_ref[rows, :] + rot * sin_ref[rows, :]

        def put(ref, chunks):
            for c, y in enumerate(chunks):
                ref[rows, c * LANES:(c + 1) * LANES] = roped(y).astype(BF16)

        nq = C_W // LANES
        if grp == 0:
            put(cq_ref, normed[:nq])
            put(ck_ref, normed[nq:])
            if emit_cache:
                for c, y in enumerate(normed[nq:]):
                    yt = y.T
                    for half in range(2):
                        cache_cols(nck_ref, sub, (c, half), yt[half * HEAD_DIM:(half + 1) * HEAD_DIM])
        elif grp == 1:
            put(dq_ref, normed[:nq])
            put(dk_ref, normed[nq:])
            if emit_cache:
                for j, y in enumerate(normed[nq:]):
                    cache_cols(ndk_ref, sub, (j,), y.T[:HEAD_DIM])
        else:
            cv, dv = p[:, :C_W], p[:, C_W:]
            for v, vt_ref in ((cv, cvt_ref), (dv, dvt_ref)):
                for hh in range(vt_ref.shape[0]):
                    vt = v[:, hh * LANES:(hh + 1) * LANES].T
                    vt_ref[hh, :LANES, rows] = vt.astype(BF16)
                    vt_ref[hh, LANES:, rows] = pad
                    if emit_cache and vt_ref is dvt_ref:
                        cache_cols(ndv_ref, sub, (hh,), vt[:HEAD_DIM])
            if emit_cache:
                for hh in range(C_HEADS):
                    cache_rows(ncv_ref, sub, hh)[...] = cv[:, hh * LANES:(hh + 1) * LANES]

    items = [(sub, grp) for sub in range(subs) for grp in range(len(groups))]
    p = project(items[0])
    for idx, item in enumerate(items):
        normed = head_norms(p, groups[item[1]][0]) if item[1] < 2 else None
        nxt = project(items[idx + 1]) if idx + 1 < len(items) else None
        finish(item, p, normed)
        p = nxt


def _odd_in_call(x, mod, layer, row_of_batch, g, w_in, gains, head_mean, rope_tabs, tm, cache_n=None):
    emit_cache = cache_n is not None
    bsz, n, _ = x.shape
    rope = rope_tabs is not None
    subs = 2 if n % (2 * tm) == 0 else 1
    ts = subs * tm
    tok = lambda w: pl.BlockSpec((None, ts, w), lambda b, i: (b, i, 0))
    in_specs = [
        tok(D_MODEL),
        _mod_spec(layer, 0, row_of_batch),
        _mod_spec(layer, 1, row_of_batch),
        _const_spec((1, D_MODEL)),
        _const_spec((D_MODEL, ODD_W)),
        _const_spec((1, ODD_NORMED_W)),
        _const_spec((2 * LANES, 2 * LANES)),
    ]
    args = [x, mod, mod, g, w_in, gains, head_mean]
    if rope:
        in_specs += [pl.BlockSpec((ts, LANES), lambda b, i: (i, 0))] * 2
        args += list(rope_tabs)
    vt_spec = lambda heads: pl.BlockSpec((None, heads, V_ROWS, ts), lambda b, i: (b, 0, 0, i))
    vt_shape = lambda heads: jax.ShapeDtypeStruct((bsz, heads, V_ROWS, n), BF16)
    tok_shape = lambda w: jax.ShapeDtypeStruct((bsz, n, w), BF16)
    out_specs = [tok(C_W), tok(C_W), vt_spec(C_HEADS), tok(DQ_W), tok(DKV_DUP_W), vt_spec(D_KV_HEADS)]
    out_shape = [tok_shape(C_W), tok_shape(C_W), vt_shape(C_HEADS), tok_shape(DQ_W),
                 tok_shape(DKV_DUP_W), vt_shape(D_KV_HEADS)]
    if emit_cache:
        assert bsz == 1 and tm % cache_n == 0 and n % cache_n == 0
        reqs, rps = n // cache_n, ts // cache_n
        for lead in ((C_HEADS, 2), None, (D_KV_HEADS,), (D_KV_HEADS,)):
            if lead is None:
                out_specs.append(pl.BlockSpec((None, ts * C_HEADS, LANES), lambda b, i: (b, i, 0)))
                out_shape.append(jax.ShapeDtypeStruct((bsz, n * C_HEADS, LANES), F32))
            else:
                zeros = (0,) * (len(lead) + 2)
                out_specs.append(pl.BlockSpec((rps,) + lead + (HEAD_DIM, cache_n),
                                              lambda b, i, zeros=zeros: (i,) + zeros))
                out_shape.append(jax.ShapeDtypeStruct((reqs,) + lead + (HEAD_DIM, cache_n), F32))
    return pl.pallas_call(
        functools.partial(_odd_in_kernel, rope=rope, emit_cache=emit_cache, tm=tm),
        grid=(bsz, n // ts),
        in_specs=in_specs,
        out_specs=out_specs,
        out_shape=out_shape,
        compiler_params=_params(2),
        name="odd_in",
    )(*args)


QK_AHEAD = 3


def _diff_attn_kernel(*refs, lam_init, tq, kc):
    n_src = (len(refs) - 4) // 2
    q_ref, (lamv_ref, g_ref, o_ref) = refs[0], refs[-3:]
    k_refs, vt_refs = refs[1:1 + n_src], refs[1 + n_src:1 + 2 * n_src]
    chunks = [(src, off) for src in range(n_src) for off in range(0, k_refs[src].shape[0], kc)]
    n_chunks = len(chunks)
    tiles = q_ref.shape[0] // tq
    heads = q_ref.shape[1] // LANES
    lane = lax.broadcasted_iota(jnp.int32, (tq, LANES), 1)
    lv = lamv_ref[...]
    lam = (jnp.exp(jnp.sum(lv[0:1] * lv[1:2], keepdims=True))
           - jnp.exp(jnp.sum(lv[2:3] * lv[3:4], keepdims=True)) + lam_init)

    qqs = {}

    def score(item):
        h, t, c = item
        cols = slice(h * LANES, (h + 1) * LANES)
        if (h, t) not in qqs:
            q = q_ref[t * tq:(t + 1) * tq, cols]
            zero = jnp.zeros_like(q)
            qqs[h, t] = jnp.concatenate([jnp.where(lane < HEAD_DIM, q, zero),
                                         jnp.where(lane >= HEAD_DIM, q, zero)], axis=0)
        src, off = chunks[c]
        return _dot_nt(k_refs[src][off:off + kc, cols], qqs[h, t])

    def finish(h, t, acc, l):
        inv = 1.0 / l
        ot = acc[:, :tq] * inv[:, :tq] - acc[:, tq:] * (inv[:, tq:] * lam)
        o = ot.T
        ms = jnp.mean(o * o, axis=-1, keepdims=True)
        y = o * lax.rsqrt(ms + EPS) * g_ref[...] * (1.0 - lam_init)
        o_ref[t * tq:(t + 1) * tq, h * LANES:(h + 1) * LANES] = y.astype(BF16)

    items = [(h, t, c) for h in range(heads) for t in range(tiles) for c in range(n_chunks)]
    ahead = [score(it) for it in items[:QK_AHEAD]]
    m = acc = None
    for idx, (h, t, c) in enumerate(items):
        s = ahead.pop(0)
        if idx + QK_AHEAD < len(items):
            ahead.append(score(items[idx + QK_AHEAD]))
        cm = jnp.max(s, axis=0, keepdims=True)
        m_new = cm if c == 0 else jnp.maximum(m, cm)
        src, off = chunks[c]
        e = jnp.exp2(s - m_new)
        lsum = jnp.sum(e, axis=0, keepdims=True)
        pv = _dot(vt_refs[src][h, :LANES, off:off + kc], e.astype(BF16))
        if c == 0:
            acc, l = pv, lsum
        else:
            alpha = jnp.exp2(m - m_new)
            acc, l = acc * alpha + pv, l * alpha + lsum
        m = m_new
        if c == n_chunks - 1:
            finish(h, t, acc, l)


def _diff_attn_call(q, ks, vts, lamv, subln_g, lam_init, tq, tiles, heads, flat_values=False):
    bsz, n, _ = q.shape
    kc = next(c for c in (512, 256, 128) if all(k.shape[1] % c == 0 for k in ks))
    hw = heads * LANES
    k_specs = [pl.BlockSpec((None, k.shape[1], hw), lambda b, h, i: (b, 0, h)) for k in ks]
    vt_index = (lambda b, h, i: (0, h, 0, b)) if flat_values else (lambda b, h, i: (b, h, 0, 0))
    vt_specs = [pl.BlockSpec((None, heads, V_ROWS, k.shape[1]), vt_index) for k in ks]
    return pl.pallas_call(
        functools.partial(_diff_attn_kernel, lam_init=lam_init, tq=tq, kc=kc),
        grid=(bsz, C_HEADS // heads, n // (tq * tiles)),
        in_specs=[pl.BlockSpec((None, tiles * tq, hw), lambda b, h, i: (b, i, h))]
        + k_specs + vt_specs + [_const_spec((4, HEAD_DIM)), _const_spec((1, LANES))],
        out_specs=pl.BlockSpec((None, tiles * tq, hw), lambda b, h, i: (b, i, h)),
        out_shape=jax.ShapeDtypeStruct((bsz, n, C_W), BF16),
        compiler_params=_params(3),
        name="diff_attn",
    )(q, *ks, *vts, lamv, subln_g)


def _sink_attn_kernel(*refs, windowed, tq, nblk):
    if windowed:
        (q_ref, ck_ref, kp_ref, km_ref, kn_ref, cvt_ref, vp_ref, vm_ref, vn_ref,
         bias_ref, sink_ref, o_ref) = refs
        kband = jnp.concatenate([kp_ref[...], km_ref[...], kn_ref[...]], axis=0)
        vtband = jnp.concatenate([vp_ref[...], vm_ref[...], vn_ref[...]], axis=1)
    else:
        q_ref, k_ref, vt_ref, sink_ref, o_ref = refs
    subs = q_ref.shape[0] // tq
    groups = q_ref.shape[1] // (2 * LANES)
    lane = lax.broadcasted_iota(jnp.int32, (tq, LANES), 1)
    lo = lane < HEAD_DIM

    def scores_of(stream):
        g, sub = stream
        qd = q_ref[sub * tq:(sub + 1) * tq, g * 2 * LANES:(g + 1) * 2 * LANES]
        parts = []
        for c in range(2):
            ch = qd[:, c * LANES:(c + 1) * LANES]
            zero = jnp.zeros_like(ch)
            parts += [jnp.where(lo, ch, zero), jnp.where(lo, zero, ch)]
        qq = jnp.concatenate(parts, axis=0)
        if not windowed:
            return [_dot_nt(k_ref[:, g * LANES:(g + 1) * LANES], qq)]
        blk = pl.program_id(2) * subs + sub
        variant = jnp.where(blk == 0, 0, jnp.where(blk == nblk - 1, 2, 1))
        band = _dot_nt(kband[sub * tq:(sub + 3) * tq, :], qq) + bias_ref[variant]
        return [_dot_nt(ck_ref[...], qq), band]

    def finish(stream, scores):
        g, sub = stream
        values = [cvt_ref[...], vtband[:, sub * tq:(sub + 3) * tq]] if windowed else [vt_ref[g]]
        sink = sink_ref[g]
        m = sink
        for s in scores:
            m = jnp.maximum(m, jnp.max(s, axis=0, keepdims=True))
        acc = None
        for s, vt in zip(scores, values):
            pv = _dot(vt, jnp.exp2(s - m).astype(BF16))
            acc = pv if acc is None else acc + pv
        inv = 1.0 / (acc[LANES:LANES + 1, :] + jnp.exp2(sink - m))
        ot = acc[:LANES, :] * inv
        for c in range(2):
            o0 = ot[:, (2 * c) * tq:(2 * c + 1) * tq].T
            o1 = ot[:, (2 * c + 1) * tq:(2 * c + 2) * tq].T
            cols = slice((2 * g + c) * LANES, (2 * g + c + 1) * LANES)
            o_ref[sub * tq:(sub + 1) * tq, cols] = jnp.where(lo, o0, o1).astype(BF16)

    streams = [(g, sub) for g in range(groups) for sub in range(subs)]
    ahead = [scores_of(st) for st in streams[:QK_AHEAD]]
    for idx, st in enumerate(streams):
        cur = ahead.pop(0)
        if idx + QK_AHEAD < len(streams):
            ahead.append(scores_of(streams[idx + QK_AHEAD]))
        finish(st, cur)


def _window_bias(n, tq):
    nblk = n // tq
    kk = np.arange(3 * tq)[:, None]
    qi = np.arange(tq)[None, :]
    band_ok = np.abs(qi + WINDOW - kk) <= WINDOW
    out = []
    for blk in (0, 1, nblk - 1):
        jpos = blk * tq - WINDOW + kk
        ok = band_ok & (jpos >= 0) & (jpos < n)
        out.append(np.tile(np.where(ok, 0.0, NEG_INF), (1, D_GROUP)))
    return jnp.asarray(np.stack(out), F32)


def _sink_attn_call(q, k, vt, sink_rows, ctx_k=None, ctx_vt=None, flat_values=False):
    bsz, n, _ = q.shape
    windowed = ctx_k is not None
    tq = WINDOW if windowed else n
    nblk = n // tq
    subs = next(s for s in (16, 8, 4, 2, 1) if nblk % s == 0) if windowed else 1
    steps = nblk // subs
    groups = 1 if windowed else D_KV_HEADS
    q_spec = pl.BlockSpec((None, subs * tq, groups * 2 * LANES), lambda b, j, i: (b, i, j))
    sink_spec = pl.BlockSpec((groups, 1, D_GROUP * tq), lambda b, j, i: (j, 0, 0))
    if windowed:
        assert nblk >= 3, "first / interior / last mask variants need three query blocks"
        n_ctx = ctx_k.shape[1]
        prev = lambda i: jnp.maximum(i * subs - 1, 0)
        nxt = lambda i: jnp.minimum((i + 1) * subs, nblk - 1)
        in_specs = [
            q_spec,
            pl.BlockSpec((None, n_ctx, LANES), lambda b, j, i: (b, 0, j)),
            pl.BlockSpec((None, tq, LANES), lambda b, j, i: (b, prev(i), j)),
            pl.BlockSpec((None, subs * tq, LANES), lambda b, j, i: (b, i, j)),
            pl.BlockSpec((None, tq, LANES), lambda b, j, i: (b, nxt(i), j)),
            pl.BlockSpec((None, None, V_ROWS, n_ctx), lambda b, j, i: (b, j, 0, 0)),
            pl.BlockSpec((None, None, V_ROWS, tq), lambda b, j, i: (b, j, 0, prev(i))),
            pl.BlockSpec((None, None, V_ROWS, subs * tq), lambda b, j, i: (b, j, 0, i)),
            pl.BlockSpec((None, None, V_ROWS, tq), lambda b, j, i: (b, j, 0, nxt(i))),
            _const_spec((3, 3 * tq, D_GROUP * tq)),
            sink_spec,
        ]
        args = (q, ctx_k, k, k, k, ctx_vt, vt, vt, vt, _window_bias(n, tq), sink_rows)
    else:
        in_specs = [
            q_spec,
            pl.BlockSpec((None, n, groups * LANES), lambda b, j, i: (b, 0, j)),
            pl.BlockSpec((None, groups, V_ROWS, n),
                         (lambda b, j, i: (0, j, 0, b)) if flat_values else (lambda b, j, i: (b, j, 0, 0))),
            sink_spec,
        ]
        args = (q, k, vt, sink_rows)
    return pl.pallas_call(
        functools.partial(_sink_attn_kernel, windowed=windowed, tq=tq, nblk=nblk),
        grid=(bsz, D_KV_HEADS // groups, steps),
        in_specs=in_specs,
        out_specs=q_spec,
        out_shape=jax.ShapeDtypeStruct((bsz, n, DQ_W), BF16),
        compiler_params=_params(3),
        name="sink_attn",
    )(*args)


def _dup_heads(a):
    lead = a.shape[:-1]
    a = a.reshape(lead + (D_KV_HEADS, 1, HEAD_DIM))
    return jnp.broadcast_to(a, lead + (D_KV_HEADS, 2, HEAD_DIM)).reshape(lead + (DKV_DUP_W,))


def _odd_weights(w_in_odd, c_qn_g, c_kn_g, d_qn_g, d_kn_g):
    cq, ck, cv, dq, dk, dv = jnp.split(
        w_in_odd, (C_W, 2 * C_W, 3 * C_W, 3 * C_W + DQ_W, 3 * C_W + DQ_W + DKV_W), axis=-1)
    w = jnp.concatenate([cq, ck, dq, _dup_heads(dk), cv, _dup_heads(dv)], axis=-1).astype(BF16)
    gains = jnp.concatenate([
        jnp.tile(c_qn_g * (ATTN_SCALE * LOG2E), C_W // HEAD_DIM),
        jnp.tile(c_kn_g, C_W // HEAD_DIM),
        jnp.tile(d_qn_g * (ATTN_SCALE * LOG2E), DQ_W // HEAD_DIM),
        jnp.tile(d_kn_g, DKV_DUP_W // HEAD_DIM),
    ])[None, :]
    return w, gains


def _value_rows(v):
    bsz, nk, w = v.shape
    vt = v.reshape(bsz, nk, w // LANES, LANES).transpose(0, 2, 3, 1)
    pad = jnp.zeros((bsz, w // LANES, V_ROWS - LANES, nk), v.dtype).at[:, :, 0, :].set(1)
    return jnp.concatenate([vt, pad], axis=2)


def _rope_tables(n):
    rows = n // GRID_W
    row = np.repeat(np.arange(rows), GRID_W).astype(np.float64)
    col = np.tile(np.arange(GRID_W), rows).astype(np.float64)
    freqs = ROPE_BASE ** (-np.arange(ROPE_FREQS, dtype=np.float64) / ROPE_FREQS)
    ang = np.stack([row[:, None] * freqs, col[:, None] * freqs], axis=1)
    ang = np.concatenate([ang, ang], axis=-1).reshape(n, HEAD_DIM)
    ang = np.concatenate([ang, ang], axis=-1)
    return jnp.asarray(np.cos(ang), F32), jnp.asarray(np.sin(ang), F32)


MLP_TILE = 512


def _token_tile(n):
    return min(n, 512)


def _mixer(x, l, mod, row_of_batch, wts, caches, new_cache):
    bsz, n, _ = x.shape
    decode = caches is not None
    g1n = wts["norm1_g"][l][None, :]
    if l % 2 == 0:
        e = l // 2
        a, z = _even_in_call(x, mod, l, row_of_batch, g1n, e, wts["w_in_even"], wts["w_pool"],
                             wts["pool_scale"][e][None, :], wts["dftc"], wts["w_fft"], min(n, 1024),
                             1 if decode or n > 1024 else next(q for q in (4, 2, 1) if bsz % q == 0))
        y = _dft_two_stage_call(z) if n > DFT_DIRECT_MAX else _dft_direct_call(z)
        return a, y
    o = l // 2
    lam_init = _lam_init(l)
    rope_tabs = _rope_tables(n) if decode else None
    xo = x if decode else x.reshape(1, bsz * n, D_MODEL)
    outs = _odd_in_call(xo, mod, l, row_of_batch, g1n, wts["w_in_odd"][o], wts["odd_gains"][o],
                        wts["head_mean"], rope_tabs, _token_tile(xo.shape[1]),
                        None if decode else n)
    if not decode:
        outs = [a if a.ndim == 4 else a.reshape(bsz, n, a.shape[-1]) for a in outs[:6]] + list(outs[6:])
    cq, ck, cvt, dq, dkk, dvt = outs[:6]
    if decode:
        c_k, c_v, d_k, d_v = caches
        lc = c_k.shape[2]
        ks = [c_k[:, o].reshape(bsz, lc, C_W).astype(BF16), ck]
        vts = [_value_rows(c_v[:, o].reshape(bsz, lc, C_W).astype(BF16)), cvt]
        ctx_k = _dup_heads(d_k[:, o].reshape(bsz, lc, DKV_W)).astype(BF16)
        ctx_vt = _value_rows(_dup_heads(d_v[:, o].reshape(bsz, lc, DKV_W)).astype(BF16))
        tq = min(n, 256)
    else:
        new_cache.append(outs[6:])
        ks, vts = [ck], [cvt]
        ctx_k = ctx_vt = None
        tq = n
    tiles = next(t for t in (4, 2, 1) if n % (t * tq) == 0)
    heads = 1 if decode else C_HEADS
    c_out = _diff_attn_call(cq, ks, vts, wts["lamv"][o],
                            wts["c_subln_g"][o][None, :], lam_init, tq, tiles, heads, not decode)
    sink_tq = WINDOW if decode else n
    sink_rows = jnp.repeat(wts["d_sink"][o].reshape(D_KV_HEADS, D_GROUP) * LOG2E, sink_tq,
                           axis=-1)[:, None, :]
    d_out = _sink_attn_call(dq, dkk, dvt, sink_rows, ctx_k, ctx_vt, not decode)
    return c_out, d_out


def kernel(x_prompt, x_sample, c, cache_c_k, cache_c_v, cache_d_k, cache_d_v, c_ctx,
           norm1_g, norm2_g, w_ada, b_ada, w_in_even, w_pool, pool_scale, w_fft, w_out_even,
           w_in_odd, c_qn_g, c_kn_g, lam_q1, lam_k1, lam_q2, lam_k2, c_subln_g,
           d_qn_g, d_kn_g, d_sink, w_out_odd, w_mlp1, w_mlp2):
    depth = norm1_g.shape[0]
    n_odd = w_in_odd.shape[0]
    dec_b = c.shape[0]

    rows = -(-(1 + dec_b) // SUBLANES) * SUBLANES
    cond = jnp.zeros((rows, D_MODEL), F32).at[0].set(c_ctx).at[1:1 + dec_b].set(c)
    mod = _adaln_call(cond, w_ada, b_ada).reshape(depth, rows, 6, 1, D_MODEL)

    ic = np.arange(GROUP_W)
    ang = 2.0 * np.pi * np.outer(ic, ic) / GROUP_W
    dftc = _bf16_const(np.concatenate([np.cos(ang), -np.sin(ang)], axis=1))
    head_mean = _bf16_const(np.kron(np.eye(2 * LANES // HEAD_DIM), np.ones((HEAD_DIM, HEAD_DIM))) / HEAD_DIM)

    odd = [_odd_weights(w_in_odd[o], c_qn_g[o], c_kn_g[o], d_qn_g[o], d_kn_g[o]) for o in range(n_odd)]
    bf16 = lambda w: w.astype(BF16)
    wts = {
        "norm1_g": norm1_g, "norm2_g": norm2_g,
        "w_in_even": bf16(w_in_even), "w_pool": bf16(w_pool), "pool_scale": pool_scale,
        "w_fft": bf16(w_fft), "w_out_even": bf16(w_out_even), "dftc": dftc,
        "w_in_odd": [w for w, _ in odd], "odd_gains": [g for _, g in odd], "head_mean": head_mean,
        "lamv": jnp.stack([lam_q1, lam_k1, lam_q2, lam_k2], axis=1),
        "c_subln_g": c_subln_g, "d_sink": d_sink, "w_out_odd": bf16(w_out_odd),
        "w_mlp1": bf16(w_mlp1), "w_mlp2": bf16(w_mlp2),
    }

    bsz, n, _ = x_prompt.shape
    caches = (cache_c_k, cache_c_v, cache_d_k, cache_d_v)
    new_cache = []
    xp, xs = x_prompt, x_sample
    for l in range(depth):
        pp = _mixer(xp, l, mod, lambda b: 0, wts, None, new_cache)
        ps = _mixer(xs, l, mod, lambda b: 1 + b, wts, caches, new_cache)
        w_out, out_idx = (wts["w_out_even"], l // 2) if l % 2 == 0 else (wts["w_out_odd"], l // 2)
        flat = lambda a: a.reshape(1, bsz * n, a.shape[-1])
        xp, xs = _out_mlp_call((flat(xp), flat(pp[0]), flat(pp[1]), 0), (xs, ps[0], ps[1], 1),
                               mod, l, wts["norm2_g"][l][None, :], w_out, out_idx,
                               wts["w_mlp1"], wts["w_mlp2"])
        xp = xp.reshape(bsz, n, D_MODEL)
    y_prompt, y_sample = xp, xs
    new_c_k = jnp.stack([nc[0].transpose(0, 4, 1, 2, 3) for nc in new_cache], axis=1)
    new_c_v = jnp.stack([nc[1].reshape(bsz, n, C_HEADS, 2 * HEAD_DIM) for nc in new_cache], axis=1)
    new_d_k = jnp.stack([nc[2].transpose(0, 3, 1, 2) for nc in new_cache], axis=1)
    new_d_v = jnp.stack([nc[3].transpose(0, 3, 1, 2) for nc in new_cache], axis=1)
    return (y_prompt, y_sample, new_c_k, new_c_v, new_d_k, new_d_v)
```

```python
import functools
import math

import numpy as np
import jax
import jax.numpy as jnp
from jax import lax
from jax.experimental import pallas as pl
from jax.experimental.pallas import tpu as pltpu

F32 = jnp.float32
BF16 = jnp.bfloat16

D_MODEL = 1024
HEAD_DIM = 64
LANES = 128
SUBLANES = 8
GRID_W = 64
ROPE_FREQS = HEAD_DIM // 4
ROPE_BASE = 10000.0
EPS = 1e-6
NEG_INF = -1e30
WINDOW = 128
A_WIDTH = D_MODEL // 2
B_WIDTH = D_MODEL // 2
POOL_WINDOWS = (2, 4, 8, 16)
POOL_HALO = 8
N_GROUPS = 4
GROUP_W = A_WIDTH // N_GROUPS
C_HEADS = 4
D_HEADS = 8
D_KV_HEADS = 2
D_GROUP = D_HEADS // D_KV_HEADS
C_W = C_HEADS * 2 * HEAD_DIM
DQ_W = D_HEADS * HEAD_DIM
DKV_W = D_KV_HEADS * HEAD_DIM
DKV_DUP_W = 2 * DKV_W
D_FF = 4 * D_MODEL
ATTN_SCALE = HEAD_DIM ** -0.5
LOG2E = math.log2(math.e)
V_ROWS = LANES + 16
ODD_NORMED_W = C_W + C_W + DQ_W + DKV_DUP_W
ODD_W = ODD_NORMED_W + C_W + DKV_DUP_W
VMEM_LIMIT = 56 * 1024 * 1024


def _lam_init(layer):
    return 0.8 - 0.6 * math.exp(-0.3 * layer)


def _params(n_axes):
    return pltpu.CompilerParams(dimension_semantics=("arbitrary",) * n_axes,
                                vmem_limit_bytes=VMEM_LIMIT)


def _const_spec(shape):
    nd = len(shape)
    return pl.BlockSpec(shape, lambda *_: (0,) * nd, pipeline_mode=pl.Buffered(1))


def _stacked_spec(shape, idx):
    nd = len(shape)
    return pl.BlockSpec((None,) + tuple(shape), lambda *_: (idx,) + (0,) * nd, pipeline_mode=pl.Buffered(1))


def _mod_spec(layer, which, row_of_batch):
    return pl.BlockSpec((None, None, None, 1, D_MODEL),
                        lambda b, i: (layer, row_of_batch(b), which, 0, 0))


def _rms_mod(x, g, sc, sh):
    ms = jnp.mean(x * x, axis=-1, keepdims=True)
    return (x * lax.rsqrt(ms + EPS) * g) * (1.0 + sc) + sh


def _bf16_const(a):
    return jnp.asarray(a, F32).astype(BF16)


def _dot_nt(a, b):
    return lax.dot_general(a, b, (((1,), (1,)), ((), ())), preferred_element_type=F32)


def _dot(a, b):
    return jnp.dot(a, b, preferred_element_type=F32)


def _adaln_kernel(cond_ref, w_ref, b_ref, o_ref):
    cnd = cond_ref[...]
    s = (cnd * jax.nn.sigmoid(cnd)).astype(BF16)
    o_ref[...] = _dot(s, w_ref[...].astype(BF16)) + b_ref[...]


def _adaln_call(cond, w_ada, b_ada):
    depth = w_ada.shape[0]
    rows = cond.shape[0]
    tn = 3 * D_MODEL
    return pl.pallas_call(
        _adaln_kernel,
        grid=(depth, 6 * D_MODEL // tn),
        in_specs=[
            pl.BlockSpec((rows, D_MODEL), lambda l, j: (0, 0)),
            pl.BlockSpec((None, D_MODEL, tn), lambda l, j: (l, 0, j)),
            pl.BlockSpec((None, 1, tn), lambda l, j: (l, 0, j)),
        ],
        out_specs=pl.BlockSpec((None, rows, tn), lambda l, j: (l, 0, j)),
        out_shape=jax.ShapeDtypeStruct((depth, rows, 6 * D_MODEL), F32),
        compiler_params=_params(2),
        name="adaln",
    )(cond, w_ada, b_ada.reshape(depth, 1, 6 * D_MODEL))


def _even_in_kernel(x_ref, xp_ref, xn_ref, sh_ref, sc_ref, g_ref, win_ref, wpool_ref,
                    pscale_ref, dftc_ref, wfft_ref, a_ref, z_ref, *, tm, n):
    i = pl.program_id(1)
    n_tiles = n // tm
    g, sc, sh = g_ref[...], sc_ref[...], sh_ref[...]
    seqs = x_ref.shape[0]
    h = _rms_mod(x_ref[...].reshape(seqs * tm, D_MODEL), g, sc, sh).astype(BF16)
    u = _dot(h, win_ref[...])
    xh = jnp.concatenate([xp_ref[...], xn_ref[...]], axis=0)
    hh = _rms_mod(xh, g, sc, sh).astype(BF16)
    uh = _dot(hh, win_ref[:, :A_WIDTH])
    has_prev = (i > 0).astype(F32)
    has_next = (i < n_tiles - 1).astype(F32)
    rows = tm + 2 * POOL_HALO

    for gi in range(N_GROUPS):
        ub = u[:, A_WIDTH + gi * GROUP_W:A_WIDTH + (gi + 1) * GROUP_W].astype(BF16)
        cs = _dot(ub, dftc_ref[...])
        zr = _dot(cs[:, :GROUP_W].astype(BF16), wfft_ref[gi])
        zi = _dot(cs[:, GROUP_W:].astype(BF16), wfft_ref[gi])
        for q in range(seqs):
            z_ref[q, :, gi * GROUP_W:(gi + 1) * GROUP_W] = zr[q * tm:(q + 1) * tm].astype(BF16)
            z_ref[q, :, B_WIDTH + gi * GROUP_W:B_WIDTH + (gi + 1) * GROUP_W] = zi[q * tm:(q + 1) * tm].astype(BF16)

    t = i * tm + lax.broadcasted_iota(jnp.int32, (tm, 1), 0)
    for q, (gi, w) in ((q, gw) for q in range(seqs) for gw in enumerate(POOL_WINDOWS)):
        cols = slice(gi * GROUP_W, (gi + 1) * GROUP_W)
        uq = u[q * tm:(q + 1) * tm, cols]
        ext = jnp.concatenate([uh[0:POOL_HALO, cols] * has_prev, uq,
                               uh[POOL_HALO:, cols] * has_next], axis=0)
        s = ext
        span = 1
        while span < w:
            s = s + pltpu.roll(s, span, 0)
            span *= 2
        shift = w // 2 - 1
        s = pltpu.roll(s, rows - shift, 0) if shift else s
        acc = s[POOL_HALO:POOL_HALO + tm]
        lo = jnp.maximum(t - w // 2, 0)
        hi = jnp.minimum(t + w // 2, n)
        cnt = (hi - lo).astype(F32)
        pooled = acc / cnt - uq
        y = _dot(pooled.astype(BF16), wpool_ref[gi]) * pscale_ref[:, cols]
        a_ref[q, :, cols] = y.astype(BF16)


def _even_in_call(x, mod, layer, row_of_batch, g, e, w_in, w_pool, pool_scale, dftc, w_fft, tm, seqs):
    bsz, n, _ = x.shape
    assert seqs == 1 or tm == n
    hb = tm // POOL_HALO
    kern = functools.partial(_even_in_kernel, tm=tm, n=n)
    return pl.pallas_call(
        kern,
        grid=(bsz // seqs, n // tm),
        in_specs=[
            pl.BlockSpec((seqs, tm, D_MODEL), lambda b, i: (b, i, 0)),
            pl.BlockSpec((None, POOL_HALO, D_MODEL),
                         lambda b, i: (b * seqs, jnp.maximum(i * hb - 1, 0), 0)),
            pl.BlockSpec((None, POOL_HALO, D_MODEL),
                         lambda b, i: (b * seqs, jnp.minimum((i + 1) * hb, n // POOL_HALO - 1), 0)),
            _mod_spec(layer, 0, row_of_batch),
            _mod_spec(layer, 1, row_of_batch),
            _const_spec((1, D_MODEL)),
            _stacked_spec((D_MODEL, D_MODEL), e),
            _stacked_spec((N_GROUPS, GROUP_W, GROUP_W), e),
            _const_spec((1, A_WIDTH)),
            _const_spec((GROUP_W, 2 * GROUP_W)),
            _stacked_spec((N_GROUPS, GROUP_W, GROUP_W), e),
        ],
        out_specs=[
            pl.BlockSpec((seqs, tm, A_WIDTH), lambda b, i: (b, i, 0)),
            pl.BlockSpec((seqs, tm, 2 * B_WIDTH), lambda b, i: (b, i, 0)),
        ],
        out_shape=[
            jax.ShapeDtypeStruct((bsz, n, A_WIDTH), BF16),
            jax.ShapeDtypeStruct((bsz, n, 2 * B_WIDTH), BF16),
        ],
        compiler_params=_params(2),
        name="even_in",
    )(x, x, x, mod, mod, g, w_in, w_pool, pool_scale, dftc, w_fft)


def _dft_direct_kernel(z_ref, m_ref, y_ref, *, scale):
    for q in range(z_ref.shape[0]):
        z = z_ref[q]
        rhs = jnp.concatenate([z[:, :B_WIDTH], z[:, B_WIDTH:]], axis=0)
        y_ref[q] = (_dot(m_ref[...], rhs) * scale).astype(BF16)


def _dft_direct_call(z):
    bsz, n, _ = z.shape
    k = np.arange(n)
    ang = 2.0 * np.pi * np.outer(k, k) / n
    m = _bf16_const(np.concatenate([np.cos(ang), np.sin(ang)], axis=1))
    kern = functools.partial(_dft_direct_kernel, scale=float((n * GROUP_W) ** -0.5))
    seqs = next(q for q in (4, 2, 1) if bsz % q == 0)
    return pl.pallas_call(
        kern,
        grid=(bsz // seqs,),
        in_specs=[pl.BlockSpec((seqs, n, 2 * B_WIDTH), lambda b: (b, 0, 0)),
                  _const_spec((n, 2 * n))],
        out_specs=pl.BlockSpec((seqs, n, B_WIDTH), lambda b: (b, 0, 0)),
        out_shape=jax.ShapeDtypeStruct((bsz, n, B_WIDTH), BF16),
        compiler_params=_params(1),
        name="dft_direct",
    )(z, m)


DFT_ROWS = 16
DFT_DIRECT_MAX = 256
DFT_PITCH = 24


def _pitch_rows(x):
    g, _, w = x.shape
    pad = jnp.zeros((g, DFT_PITCH - DFT_ROWS, w), x.dtype)
    return jnp.concatenate([x, pad], axis=1).reshape(g * DFT_PITCH, w)


def _dft_stage1_kernel(z_ref, m_ref, twr_ref, twi_ref, o_ref, zs_ref, os_ref, *, na):
    nt = 2 * B_WIDTH // LANES

    @pl.when((pl.program_id(0) == 0) & (pl.program_id(1) == 0))
    def _():
        os_ref[...] = jnp.zeros_like(os_ref)

    zf = _pitch_rows(z_ref[...].astype(F32))
    for k in range(nt):
        zs_ref[k] = zf[:, k * LANES:(k + 1) * LANES]
    for bi in range(DFT_ROWS):
        rows = pl.ds(bi, na, stride=DFT_PITCH)
        zb = [zs_ref[k, rows, :].astype(BF16) for k in range(nt)]
        rhs = jnp.concatenate([jnp.concatenate(zb[:nt // 2], axis=1),
                               jnp.concatenate(zb[nt // 2:], axis=1)], axis=0)
        r = _dot(m_ref[...], rhs)
        br, bim = r[:na], r[na:]
        tr, ti = twr_ref[bi], twi_ref[bi]
        out = (br * tr - bim * ti, br * ti + bim * tr)
        for k in range(nt):
            half, kk = divmod(k, nt // 2)
            os_ref[k, rows, :] = out[half][:, kk * LANES:(kk + 1) * LANES]
    of = jnp.concatenate([os_ref[k] for k in range(nt)], axis=1)
    o_ref[...] = of.reshape(na, DFT_PITCH, 2 * B_WIDTH)[:, :DFT_ROWS].astype(BF16)


def _dft_stage2_kernel(b_ref, m_ref, y_ref, ys_ref, *, scale):
    nb = b_ref.shape[1]

    @pl.when((pl.program_id(0) == 0) & (pl.program_id(1) == 0))
    def _():
        ys_ref[...] = jnp.zeros_like(ys_ref)

    for di in range(DFT_ROWS):
        bb = b_ref[di]
        rhs = jnp.concatenate([bb[:, :B_WIDTH], bb[:, B_WIDTH:]], axis=0)
        y = _dot(m_ref[...], rhs) * scale
        for k in range(B_WIDTH // LANES):
            ys_ref[k, pl.ds(di, nb, stride=DFT_PITCH), :] = y[:, k * LANES:(k + 1) * LANES]
    yf = jnp.concatenate([ys_ref[k] for k in range(B_WIDTH // LANES)], axis=1)
    y_ref[...] = yf.reshape(nb, DFT_PITCH, B_WIDTH)[:, :DFT_ROWS].astype(BF16)


def _dft_two_stage_call(z):
    bsz, n, _ = z.shape
    na = 1 << (int(math.log2(n)) // 2)
    nb = n // na
    ia = np.arange(na)
    ib = np.arange(nb)
    ang_a = 2.0 * np.pi * np.outer(ia, ia) / na
    fr, fi = np.cos(ang_a), -np.sin(ang_a)
    m1 = _bf16_const(np.block([[fr, -fi], [fi, fr]]))
    ang_t = 2.0 * np.pi * np.outer(ib, ia) / n
    twr = jnp.asarray(np.cos(ang_t)[:, :, None], F32)
    twi = jnp.asarray(-np.sin(ang_t)[:, :, None], F32)
    ang_b = 2.0 * np.pi * np.outer(ib, ib) / nb
    m2 = _bf16_const(np.concatenate([np.cos(ang_b), np.sin(ang_b)], axis=1))

    s1 = pl.pallas_call(
        functools.partial(_dft_stage1_kernel, na=na),
        grid=(bsz, nb // DFT_ROWS),
        in_specs=[
            pl.BlockSpec((None, na, DFT_ROWS, 2 * B_WIDTH), lambda b, j: (b, 0, j, 0)),
            _const_spec((2 * na, 2 * na)),
            pl.BlockSpec((DFT_ROWS, na, 1), lambda b, j: (j, 0, 0)),
            pl.BlockSpec((DFT_ROWS, na, 1), lambda b, j: (j, 0, 0)),
        ],
        out_specs=pl.BlockSpec((None, na, DFT_ROWS, 2 * B_WIDTH), lambda b, j: (b, 0, j, 0)),
        out_shape=jax.ShapeDtypeStruct((bsz, na, nb, 2 * B_WIDTH), BF16),
        scratch_shapes=[pltpu.VMEM((2 * B_WIDTH // LANES, na * DFT_PITCH, LANES), F32)] * 2,
        compiler_params=_params(2),
        name="dft_stage1",
    )(z.reshape(bsz, na, nb, 2 * B_WIDTH), m1, twr, twi)
    y = pl.pallas_call(
        functools.partial(_dft_stage2_kernel, scale=float((n * GROUP_W) ** -0.5)),
        grid=(bsz, na // DFT_ROWS),
        in_specs=[
            pl.BlockSpec((None, DFT_ROWS, nb, 2 * B_WIDTH), lambda b, j: (b, j, 0, 0)),
            _const_spec((nb, 2 * nb)),
        ],
        out_specs=pl.BlockSpec((None, nb, DFT_ROWS, B_WIDTH), lambda b, j: (b, 0, j, 0)),
        out_shape=jax.ShapeDtypeStruct((bsz, nb, na, B_WIDTH), BF16),
        scratch_shapes=[pltpu.VMEM((B_WIDTH // LANES, nb * DFT_PITCH, LANES), F32)],
        compiler_params=_params(2),
        name="dft_stage2",
    )(s1, m2)
    return y.reshape(bsz, n, B_WIDTH)


FF_CHUNK = 1024


def _out_mlp_kernel(xa_ref, p1a_ref, p2a_ref, xb_ref, p1b_ref, p2b_ref, g1_ref, sh_ref, sc_ref, g2_ref,
                    ng_ref, wout_ref, w1_ref, w2_ref, oa_ref, ob_ref, *, steps_a):
    def body(x_ref, p1_ref, p2_ref, o_ref):
        half = p1_ref.shape[-1]
        mix = _dot(p1_ref[...], wout_ref[:half, :]) + _dot(p2_ref[...], wout_ref[half:, :])
        x1 = x_ref[...] + g1_ref[...] * mix
        h = _rms_mod(x1, ng_ref[...], sc_ref[...], sh_ref[...]).astype(BF16)
        acc = None
        for c in range(D_FF // FF_CHUNK):
            a = _dot(h, w1_ref[:, c * FF_CHUNK:(c + 1) * FF_CHUNK])
            a = jnp.square(jnp.maximum(a, 0.0)).astype(BF16)
            part = _dot(a, w2_ref[c * FF_CHUNK:(c + 1) * FF_CHUNK, :])
            acc = part if acc is None else acc + part
        o_ref[...] = x1 + g2_ref[...] * acc

    @pl.when(pl.program_id(0) < steps_a)
    def _():
        body(xa_ref, p1a_ref, p2a_ref, oa_ref)

    @pl.when(pl.program_id(0) >= steps_a)
    def _():
        body(xb_ref, p1b_ref, p2b_ref, ob_ref)


def _out_mlp_call(set_a, set_b, mod, layer, ng, w_out, out_idx, w1, w2):
    (xa, p1a, p2a, row_a), (xb, p1b, p2b, row_b) = set_a, set_b
    half = p1a.shape[-1]
    tm = min(xa.shape[1], xb.shape[1], MLP_TILE)
    tiles_a, tiles_b = xa.shape[1] // tm, xb.shape[1] // tm
    steps_a, steps_b = xa.shape[0] * tiles_a, xb.shape[0] * tiles_b

    def pos_a(s):
        sa = jnp.minimum(s, steps_a - 1)
        return sa // tiles_a, sa % tiles_a

    def pos_b(s):
        sb = jnp.maximum(s - steps_a, 0)
        return sb // tiles_b, sb % tiles_b

    tok_a = lambda w: pl.BlockSpec((None, tm, w), lambda s: pos_a(s) + (0,))
    tok_b = lambda w: pl.BlockSpec((None, tm, w), lambda s: pos_b(s) + (0,))
    row = lambda s: jnp.where(s < steps_a, row_a + pos_a(s)[0], row_b + pos_b(s)[0])
    mod_spec = lambda which: pl.BlockSpec((None, None, None, 1, D_MODEL),
                                          lambda s: (layer, row(s), which, 0, 0))
    return pl.pallas_call(
        functools.partial(_out_mlp_kernel, steps_a=steps_a),
        grid=(steps_a + steps_b,),
        in_specs=[
            tok_a(D_MODEL), tok_a(half), tok_a(half), tok_b(D_MODEL), tok_b(half), tok_b(half),
            mod_spec(2), mod_spec(3), mod_spec(4), mod_spec(5),
            _const_spec((1, D_MODEL)),
            _stacked_spec((2 * half, D_MODEL), out_idx),
            _stacked_spec((D_MODEL, D_FF), layer),
            _stacked_spec((D_FF, D_MODEL), layer),
        ],
        out_specs=[tok_a(D_MODEL), tok_b(D_MODEL)],
        out_shape=[jax.ShapeDtypeStruct(xa.shape, F32), jax.ShapeDtypeStruct(xb.shape, F32)],
        compiler_params=_params(1),
        name="out_mlp",
    )(xa, p1a, p2a, xb, p1b, p2b, mod, mod, mod, mod, ng, w_out, w1, w2)


def _odd_in_kernel(*refs, rope, emit_cache, tm):
    x_ref, sh_ref, sc_ref, g_ref, win_ref, gain_ref, hm_ref = refs[:7]
    k = 7
    if rope:
        cos_ref, sin_ref = refs[k:k + 2]
        k += 2
    cq_ref, ck_ref, cvt_ref, dq_ref, dk_ref, dvt_ref = refs[k:k + 6]
    k += 6
    if emit_cache:
        nck_ref, ncv_ref, ndk_ref, ndv_ref = refs[k:k + 4]
    subs = x_ref.shape[0] // tm
    lane = lax.broadcasted_iota(jnp.int32, (tm, LANES), 1)
    first_half = (lane % (2 * ROPE_FREQS)) < ROPE_FREQS
    pad_row = lax.broadcasted_iota(jnp.int32, (V_ROWS - LANES, tm), 0)
    pad = jnp.where(pad_row == 0, 1.0, 0.0).astype(BF16)

    def cache_rows(ref, sub, r):
        rpt = ref.shape[0] // x_ref.shape[0]
        return ref.at[pl.ds(sub * tm * rpt + r, tm, stride=rpt), :]

    def cache_cols(ref, sub, lead, mat):
        n_req = ref.shape[-1]
        for r in range(tm // n_req):
            ref[(sub * (tm // n_req) + r,) + lead] = mat[:, r * n_req:(r + 1) * n_req]

    groups = ((0, 2 * C_W), (2 * C_W, ODD_NORMED_W), (ODD_NORMED_W, ODD_W))
    hidden = {}

    def project(item):
        sub, grp = item
        if sub not in hidden:
            x = x_ref[sub * tm:(sub + 1) * tm, :]
            hidden[sub] = _rms_mod(x, g_ref[...], sc_ref[...], sh_ref[...]).astype(BF16)
        lo, hi = groups[grp]
        return _dot(hidden[sub], win_ref[:, lo:hi])

    def head_norms(p, lo):
        normed = []
        for c in range(p.shape[1] // (2 * LANES)):
            v = p[:, c * 2 * LANES:(c + 1) * 2 * LANES]
            ms = _dot((v * v).astype(BF16), hm_ref[...])
            y = v * lax.rsqrt(ms + EPS) * gain_ref[:, lo + c * 2 * LANES:lo + (c + 1) * 2 * LANES]
            normed += [y[:, :LANES], y[:, LANES:]]
        return normed

    def finish(item, p, normed):
        sub, grp = item
        rows = slice(sub * tm, (sub + 1) * tm)

        def roped(y):
            if not rope:
                return y
            rot = jnp.where(first_half, -pltpu.roll(y, LANES - ROPE_FREQS, 1),
                            pltpu.roll(y, ROPE_FREQS, 1))
            return y * cos_ref[rows, :] + rot * sin_ref[rows, :]

        def put(ref, chunks):
            for c, y in enumerate(chunks):
                ref[rows, c * LANES:(c + 1) * LANES] = roped(y).astype(BF16)

        nq = C_W // LANES
        if grp == 0:
            put(cq_ref, normed[:nq])
            put(ck_ref, normed[nq:])
            if emit_cache:
                for c, y in enumerate(normed[nq:]):
                    yt = y.T
                    for half in range(2):
                        cache_cols(nck_ref, sub, (c, half), yt[half * HEAD_DIM:(half + 1) * HEAD_DIM])
        elif grp == 1:
            put(dq_ref, normed[:nq])
            put(dk_ref, normed[nq:])
            if emit_cache:
                for j, y in enumerate(normed[nq:]):
                    cache_cols(ndk_ref, sub, (j,), y.T[:HEAD_DIM])
        else:
            cv, dv = p[:, :C_W], p[:, C_W:]
            for v, vt_ref in ((cv, cvt_ref), (dv, dvt_ref)):
                for hh in range(vt_ref.shape[0]):
                    vt = v[:, hh * LANES:(hh + 1) * LANES].T
                    vt_ref[hh, :LANES, rows] = vt.astype(BF16)
                    vt_ref[hh, LANES:, rows] = pad
                    if emit_cache and vt_ref is dvt_ref:
                        cache_cols(ndv_ref, sub, (hh,), vt[:HEAD_DIM])
            if emit_cache:
                for hh in range(C_HEADS):
                    cache_rows(ncv_ref, sub, hh)[...] = cv[:, hh * LANES:(hh + 1) * LANES]

    items = [(sub, grp) for sub in range(subs) for grp in range(len(groups))]
    p = project(items[0])
    for idx, item in enumerate(items):
        normed = head_norms(p, groups[item[1]][0]) if item[1] < 2 else None
        nxt = project(items[idx + 1]) if idx + 1 < len(items) else None
        finish(item, p, normed)
        p = nxt


def _odd_in_call(x, mod, layer, row_of_batch, g, w_in, gains, head_mean, rope_tabs, tm, cache_n=None):
    emit_cache = cache_n is not None
    bsz, n, _ = x.shape
    rope = rope_tabs is not None
    subs = 2 if n % (2 * tm) == 0 else 1
    ts = subs * tm
    tok = lambda w: pl.BlockSpec((None, ts, w), lambda b, i: (b, i, 0))
    in_specs = [
        tok(D_MODEL),
        _mod_spec(layer, 0, row_of_batch),
        _mod_spec(layer, 1, row_of_batch),
        _const_spec((1, D_MODEL)),
        _const_spec((D_MODEL, ODD_W)),
        _const_spec((1, ODD_NORMED_W)),
        _const_spec((2 * LANES, 2 * LANES)),
    ]
    args = [x, mod, mod, g, w_in, gains, head_mean]
    if rope:
        in_specs += [pl.BlockSpec((ts, LANES), lambda b, i: (i, 0))] * 2
        args += list(rope_tabs)
    vt_spec = lambda heads: pl.BlockSpec((None, heads, V_ROWS, ts), lambda b, i: (b, 0, 0, i))
    vt_shape = lambda heads: jax.ShapeDtypeStruct((bsz, heads, V_ROWS, n), BF16)
    tok_shape = lambda w: jax.ShapeDtypeStruct((bsz, n, w), BF16)
    out_specs = [tok(C_W), tok(C_W), vt_spec(C_HEADS), tok(DQ_W), tok(DKV_DUP_W), vt_spec(D_KV_HEADS)]
    out_shape = [tok_shape(C_W), tok_shape(C_W), vt_shape(C_HEADS), tok_shape(DQ_W),
                 tok_shape(DKV_DUP_W), vt_shape(D_KV_HEADS)]
    if emit_cache:
        assert bsz == 1 and tm % cache_n == 0 and n % cache_n == 0
        reqs, rps = n // cache_n, ts // cache_n
        for lead in ((C_HEADS, 2), None, (D_KV_HEADS,), (D_KV_HEADS,)):
            if lead is None:
                out_specs.append(pl.BlockSpec((None, ts * C_HEADS, LANES), lambda b, i: (b, i, 0)))
                out_shape.append(jax.ShapeDtypeStruct((bsz, n * C_HEADS, LANES), F32))
            else:
                zeros = (0,) * (len(lead) + 2)
                out_specs.append(pl.BlockSpec((rps,) + lead + (HEAD_DIM, cache_n),
                                              lambda b, i, zeros=zeros: (i,) + zeros))
                out_shape.append(jax.ShapeDtypeStruct((reqs,) + lead + (HEAD_DIM, cache_n), F32))
    return pl.pallas_call(
        functools.partial(_odd_in_kernel, rope=rope, emit_cache=emit_cache, tm=tm),
        grid=(bsz, n // ts),
        in_specs=in_specs,
        out_specs=out_specs,
        out_shape=out_shape,
        compiler_params=_params(2),
        name="odd_in",
    )(*args)


QK_AHEAD = 3


def _diff_attn_kernel(*refs, lam_init, tq, kc):
    n_src = (len(refs) - 4) // 2
    q_ref, (lamv_ref, g_ref, o_ref) = refs[0], refs[-3:]
    k_refs, vt_refs = refs[1:1 + n_src], refs[1 + n_src:1 + 2 * n_src]
    chunks = [(src, off) for src in range(n_src) for off in range(0, k_refs[src].shape[1], kc)]
    n_chunks = len(chunks)
    reqs = q_ref.shape[0]
    tiles = q_ref.shape[1] // tq
    heads = q_ref.shape[2] // LANES
    lane = lax.broadcasted_iota(jnp.int32, (tq, LANES), 1)
    lv = lamv_ref[...]
    lam = (jnp.exp(jnp.sum(lv[0:1] * lv[1:2], keepdims=True))
           - jnp.exp(jnp.sum(lv[2:3] * lv[3:4], keepdims=True)) + lam_init)

    qqs = {}

    def score(item):
        r, h, t, c = item
        cols = slice(h * LANES, (h + 1) * LANES)
        if (r, h, t) not in qqs:
            q = q_ref[r, t * tq:(t + 1) * tq, cols]
            zero = jnp.zeros_like(q)
            qqs[r, h, t] = jnp.concatenate([jnp.where(lane < HEAD_DIM, q, zero),
                                            jnp.where(lane >= HEAD_DIM, q, zero)], axis=0)
        src, off = chunks[c]
        return _dot_nt(k_refs[src][r, off:off + kc, cols], qqs[r, h, t])

    def finish(r, h, t, acc, l):
        inv = 1.0 / l
        ot = acc[:, :tq] * inv[:, :tq] - acc[:, tq:] * (inv[:, tq:] * lam)
        o = ot.T
        ms = jnp.mean(o * o, axis=-1, keepdims=True)
        y = o * lax.rsqrt(ms + EPS) * g_ref[...] * (1.0 - lam_init)
        o_ref[r, t * tq:(t + 1) * tq, h * LANES:(h + 1) * LANES] = y.astype(BF16)

    items = [(r, h, t, c) for r in range(reqs) for h in range(heads) for t in range(tiles)
             for c in range(n_chunks)]
    ahead = [score(it) for it in items[:QK_AHEAD]]
    m = acc = None
    for idx, (r, h, t, c) in enumerate(items):
        s = ahead.pop(0)
        if idx + QK_AHEAD < len(items):
            ahead.append(score(items[idx + QK_AHEAD]))
        cm = jnp.max(s, axis=0, keepdims=True)
        m_new = cm if c == 0 else jnp.maximum(m, cm)
        src, off = chunks[c]
        e = jnp.exp2(s - m_new)
        lsum = jnp.sum(e, axis=0, keepdims=True)
        lane0 = r * k_refs[src].shape[1] + off
        pv = _dot(vt_refs[src][h, :LANES, lane0:lane0 + kc], e.astype(BF16))
        if c == 0:
            acc, l = pv, lsum
        else:
            alpha = jnp.exp2(m - m_new)
            acc, l = acc * alpha + pv, l * alpha + lsum
        m = m_new
        if c == n_chunks - 1:
            finish(r, h, t, acc, l)


def _diff_attn_call(q, ks, vts, lamv, subln_g, lam_init, tq, tiles, heads, reqs=1, flat_values=False):
    bsz, n, _ = q.shape
    assert reqs == 1 or flat_values
    kc = next(c for c in (512, 256, 128) if all(k.shape[1] % c == 0 for k in ks))
    hw = heads * LANES
    k_specs = [pl.BlockSpec((reqs, k.shape[1], hw), lambda b, h, i: (b, 0, h)) for k in ks]
    vt_index = (lambda b, h, i: (0, h, 0, b)) if flat_values else (lambda b, h, i: (b, h, 0, 0))
    vt_specs = [pl.BlockSpec((None, heads, V_ROWS, reqs * k.shape[1]), vt_index) for k in ks]
    return pl.pallas_call(
        functools.partial(_diff_attn_kernel, lam_init=lam_init, tq=tq, kc=kc),
        grid=(bsz // reqs, C_HEADS // heads, n // (tq * tiles)),
        in_specs=[pl.BlockSpec((reqs, tiles * tq, hw), lambda b, h, i: (b, i, h))]
        + k_specs + vt_specs + [_const_spec((4, HEAD_DIM)), _const_spec((1, LANES))],
        out_specs=pl.BlockSpec((reqs, tiles * tq, hw), lambda b, h, i: (b, i, h)),
        out_shape=jax.ShapeDtypeStruct((bsz, n, C_W), BF16),
        compiler_params=_params(3),
        name="diff_attn",
    )(q, *ks, *vts, lamv, subln_g)


def _sink_attn_kernel(*refs, windowed, tq, nblk):
    if windowed:
        (q_ref, ck_ref, kp_ref, km_ref, kn_ref, cvt_ref, vp_ref, vm_ref, vn_ref,
         bias_ref, sink_ref, o_ref) = refs
        kband = jnp.concatenate([kp_ref[...], km_ref[...], kn_ref[...]], axis=0)
        vtband = jnp.concatenate([vp_ref[...], vm_ref[...], vn_ref[...]], axis=1)
    else:
        q_ref, k_ref, vt_ref, sink_ref, o_ref = refs
    reqs = q_ref.shape[0]
    subs = q_ref.shape[1] // tq
    groups = q_ref.shape[2] // (2 * LANES)
    lane = lax.broadcasted_iota(jnp.int32, (tq, LANES), 1)
    lo = lane < HEAD_DIM

    def scores_of(stream):
        r, g, sub = stream
        qd = q_ref[r, sub * tq:(sub + 1) * tq, g * 2 * LANES:(g + 1) * 2 * LANES]
        parts = []
        for c in range(2):
            ch = qd[:, c * LANES:(c + 1) * LANES]
            zero = jnp.zeros_like(ch)
            parts += [jnp.where(lo, ch, zero), jnp.where(lo, zero, ch)]
        qq = jnp.concatenate(parts, axis=0)
        if not windowed:
            return [_dot_nt(k_ref[r, :, g * LANES:(g + 1) * LANES], qq)]
        blk = pl.program_id(2) * subs + sub
        variant = jnp.where(blk == 0, 0, jnp.where(blk == nblk - 1, 2, 1))
        band = _dot_nt(kband[sub * tq:(sub + 3) * tq, :], qq) + bias_ref[variant]
        return [_dot_nt(ck_ref[...], qq), band]

    def finish(stream, scores):
        r, g, sub = stream
        values = ([cvt_ref[...], vtband[:, sub * tq:(sub + 3) * tq]] if windowed
                  else [vt_ref[g, :, r * tq:(r + 1) * tq]])
        sink = sink_ref[g]
        m = sink
        for s in scores:
            m = jnp.maximum(m, jnp.max(s, axis=0, keepdims=True))
        acc = None
        for s, vt in zip(scores, values):
            pv = _dot(vt, jnp.exp2(s - m).astype(BF16))
            acc = pv if acc is None else acc + pv
        inv = 1.0 / (acc[LANES:LANES + 1, :] + jnp.exp2(sink - m))
        ot = acc[:LANES, :] * inv
        for c in range(2):
            o0 = ot[:, (2 * c) * tq:(2 * c + 1) * tq].T
            o1 = ot[:, (2 * c + 1) * tq:(2 * c + 2) * tq].T
            cols = slice((2 * g + c) * LANES, (2 * g + c + 1) * LANES)
            o_ref[r, sub * tq:(sub + 1) * tq, cols] = jnp.where(lo, o0, o1).astype(BF16)

    streams = [(r, g, sub) for r in range(reqs) for g in range(groups) for sub in range(subs)]
    ahead = [scores_of(st) for st in streams[:QK_AHEAD]]
    for idx, st in enumerate(streams):
        cur = ahead.pop(0)
        if idx + QK_AHEAD < len(streams):
            ahead.append(scores_of(streams[idx + QK_AHEAD]))
        finish(st, cur)


def _window_bias(n, tq):
    nblk = n // tq
    kk = np.arange(3 * tq)[:, None]
    qi = np.arange(tq)[None, :]
    band_ok = np.abs(qi + WINDOW - kk) <= WINDOW
    out = []
    for blk in (0, 1, nblk - 1):
        jpos = blk * tq - WINDOW + kk
        ok = band_ok & (jpos >= 0) & (jpos < n)
        out.append(np.tile(np.where(ok, 0.0, NEG_INF), (1, D_GROUP)))
    return jnp.asarray(np.stack(out), F32)


def _sink_attn_call(q, k, vt, sink_rows, ctx_k=None, ctx_vt=None, reqs=1, flat_values=False):
    bsz, n, _ = q.shape
    windowed = ctx_k is not None
    assert reqs == 1 or (flat_values and not windowed)
    tq = WINDOW if windowed else n
    nblk = n // tq
    subs = next(s for s in (16, 8, 4, 2, 1) if nblk % s == 0) if windowed else 1
    steps = nblk // subs
    groups = 1 if windowed else D_KV_HEADS
    q_spec = pl.BlockSpec((reqs, subs * tq, groups * 2 * LANES), lambda b, j, i: (b, i, j))
    sink_spec = pl.BlockSpec((groups, 1, D_GROUP * tq), lambda b, j, i: (j, 0, 0))
    if windowed:
        assert nblk >= 3, "first / interior / last mask variants need three query blocks"
        n_ctx = ctx_k.shape[1]
        prev = lambda i: jnp.maximum(i * subs - 1, 0)
        nxt = lambda i: jnp.minimum((i + 1) * subs, nblk - 1)
        in_specs = [
            q_spec,
            pl.BlockSpec((None, n_ctx, LANES), lambda b, j, i: (b, 0, j)),
            pl.BlockSpec((None, tq, LANES), lambda b, j, i: (b, prev(i), j)),
            pl.BlockSpec((None, subs * tq, LANES), lambda b, j, i: (b, i, j)),
            pl.BlockSpec((None, tq, LANES), lambda b, j, i: (b, nxt(i), j)),
            pl.BlockSpec((None, None, V_ROWS, n_ctx), lambda b, j, i: (b, j, 0, 0)),
            pl.BlockSpec((None, None, V_ROWS, tq), lambda b, j, i: (b, j, 0, prev(i))),
            pl.BlockSpec((None, None, V_ROWS, subs * tq), lambda b, j, i: (b, j, 0, i)),
            pl.BlockSpec((None, None, V_ROWS, tq), lambda b, j, i: (b, j, 0, nxt(i))),
            _const_spec((3, 3 * tq, D_GROUP * tq)),
            sink_spec,
        ]
        args = (q, ctx_k, k, k, k, ctx_vt, vt, vt, vt, _window_bias(n, tq), sink_rows)
    else:
        in_specs = [
            q_spec,
            pl.BlockSpec((reqs, n, groups * LANES), lambda b, j, i: (b, 0, j)),
            pl.BlockSpec((None, groups, V_ROWS, reqs * n),
                         (lambda b, j, i: (0, j, 0, b)) if flat_values else (lambda b, j, i: (b, j, 0, 0))),
            sink_spec,
        ]
        args = (q, k, vt, sink_rows)
    return pl.pallas_call(
        functools.partial(_sink_attn_kernel, windowed=windowed, tq=tq, nblk=nblk),
        grid=(bsz // reqs, D_KV_HEADS // groups, steps),
        in_specs=in_specs,
        out_specs=q_spec,
        out_shape=jax.ShapeDtypeStruct((bsz, n, DQ_W), BF16),
        compiler_params=_params(3),
        name="sink_attn",
    )(*args)


def _dup_heads(a):
    lead = a.shape[:-1]
    a = a.reshape(lead + (D_KV_HEADS, 1, HEAD_DIM))
    return jnp.broadcast_to(a, lead + (D_KV_HEADS, 2, HEAD_DIM)).reshape(lead + (DKV_DUP_W,))


def _odd_weights(w_in_odd, c_qn_g, c_kn_g, d_qn_g, d_kn_g):
    cq, ck, cv, dq, dk, dv = jnp.split(
        w_in_odd, (C_W, 2 * C_W, 3 * C_W, 3 * C_W + DQ_W, 3 * C_W + DQ_W + DKV_W), axis=-1)
    w = jnp.concatenate([cq, ck, dq, _dup_heads(dk), cv, _dup_heads(dv)], axis=-1).astype(BF16)
    gains = jnp.concatenate([
        jnp.tile(c_qn_g * (ATTN_SCALE * LOG2E), C_W // HEAD_DIM),
        jnp.tile(c_kn_g, C_W // HEAD_DIM),
        jnp.tile(d_qn_g * (ATTN_SCALE * LOG2E), DQ_W // HEAD_DIM),
        jnp.tile(d_kn_g, DKV_DUP_W // HEAD_DIM),
    ])[None, :]
    return w, gains


def _value_rows(v):
    bsz, nk, w = v.shape
    vt = v.reshape(bsz, nk, w // LANES, LANES).transpose(0, 2, 3, 1)
    pad = jnp.zeros((bsz, w // LANES, V_ROWS - LANES, nk), v.dtype).at[:, :, 0, :].set(1)
    return jnp.concatenate([vt, pad], axis=2)


def _rope_tables(n):
    rows = n // GRID_W
    row = np.repeat(np.arange(rows), GRID_W).astype(np.float64)
    col = np.tile(np.arange(GRID_W), rows).astype(np.float64)
    freqs = ROPE_BASE ** (-np.arange(ROPE_FREQS, dtype=np.float64) / ROPE_FREQS)
    ang = np.stack([row[:, None] * freqs, col[:, None] * freqs], axis=1)
    ang = np.concatenate([ang, ang], axis=-1).reshape(n, HEAD_DIM)
    ang = np.concatenate([ang, ang], axis=-1)
    return jnp.asarray(np.cos(ang), F32), jnp.asarray(np.sin(ang), F32)


MLP_TILE = 512


def _token_tile(n):
    return min(n, 512)


def _mixer(x, l, mod, row_of_batch, wts, caches, new_cache):
    bsz, n, _ = x.shape
    decode = caches is not None
    g1n = wts["norm1_g"][l][None, :]
    if l % 2 == 0:
        e = l // 2
        a, z = _even_in_call(x, mod, l, row_of_batch, g1n, e, wts["w_in_even"], wts["w_pool"],
                             wts["pool_scale"][e][None, :], wts["dftc"], wts["w_fft"], min(n, 1024),
                             1 if decode or n > 1024 else next(q for q in (4, 2, 1) if bsz % q == 0))
        y = _dft_two_stage_call(z) if n > DFT_DIRECT_MAX else _dft_direct_call(z)
        return a, y
    o = l // 2
    lam_init = _lam_init(l)
    rope_tabs = _rope_tables(n) if decode else None
    xo = x if decode else x.reshape(1, bsz * n, D_MODEL)
    outs = _odd_in_call(xo, mod, l, row_of_batch, g1n, wts["w_in_odd"][o], wts["odd_gains"][o],
                        wts["head_mean"], rope_tabs, _token_tile(xo.shape[1]),
                        None if decode else n)
    if not decode:
        outs = [a if a.ndim == 4 else a.reshape(bsz, n, a.shape[-1]) for a in outs[:6]] + list(outs[6:])
    cq, ck, cvt, dq, dkk, dvt = outs[:6]
    if decode:
        c_k, c_v, d_k, d_v = caches
        lc = c_k.shape[2]
        ks = [c_k[:, o].reshape(bsz, lc, C_W).astype(BF16), ck]
        vts = [_value_rows(c_v[:, o].reshape(bsz, lc, C_W).astype(BF16)), cvt]
        ctx_k = _dup_heads(d_k[:, o].reshape(bsz, lc, DKV_W)).astype(BF16)
        ctx_vt = _value_rows(_dup_heads(d_v[:, o].reshape(bsz, lc, DKV_W)).astype(BF16))
        tq = min(n, 256)
    else:
        new_cache.append(outs[6:])
        ks, vts = [ck], [cvt]
        ctx_k = ctx_vt = None
        tq = n
    tiles = next(t for t in (4, 2, 1) if n % (t * tq) == 0)
    heads = 1 if decode else C_HEADS
    c_out = _diff_attn_call(cq, ks, vts, wts["lamv"][o],
                            wts["c_subln_g"][o][None, :], lam_init, tq, tiles, heads,
                            1 if decode else next(r for r in (4, 2, 1) if bsz % r == 0), not decode)
    sink_tq = WINDOW if decode else n
    sink_rows = jnp.repeat(wts["d_sink"][o].reshape(D_KV_HEADS, D_GROUP) * LOG2E, sink_tq,
                           axis=-1)[:, None, :]
    d_out = _sink_attn_call(dq, dkk, dvt, sink_rows, ctx_k, ctx_vt,
                            1 if decode else next(r for r in (4, 2, 1) if bsz % r == 0), not decode)
    return c_out, d_out


def kernel(x_prompt, x_sample, c, cache_c_k, cache_c_v, cache_d_k, cache_d_v, c_ctx,
           norm1_g, norm2_g, w_ada, b_ada, w_in_even, w_pool, pool_scale, w_fft, w_out_even,
           w_in_odd, c_qn_g, c_kn_g, lam_q1, lam_k1, lam_q2, lam_k2, c_subln_g,
           d_qn_g, d_kn_g, d_sink, w_out_odd, w_mlp1, w_mlp2):
    depth = norm1_g.shape[0]
    n_odd = w_in_odd.shape[0]
    dec_b = c.shape[0]

    rows = -(-(1 + dec_b) // SUBLANES) * SUBLANES
    cond = jnp.concatenate([c_ctx[None, :], c, jnp.zeros((rows - 1 - dec_b, D_MODEL), F32)], axis=0)
    mod = _adaln_call(cond, w_ada, b_ada).reshape(depth, rows, 6, 1, D_MODEL)

    ic = np.arange(GROUP_W)
    ang = 2.0 * np.pi * np.outer(ic, ic) / GROUP_W
    dftc = _bf16_const(np.concatenate([np.cos(ang), -np.sin(ang)], axis=1))
    head_mean = _bf16_const(np.kron(np.eye(2 * LANES // HEAD_DIM), np.ones((HEAD_DIM, HEAD_DIM))) / HEAD_DIM)

    odd = [_odd_weights(w_in_odd[o], c_qn_g[o], c_kn_g[o], d_qn_g[o], d_kn_g[o]) for o in range(n_odd)]
    bf16 = lambda w: w.astype(BF16)
    wts = {
        "norm1_g": norm1_g, "norm2_g": norm2_g,
        "w_in_even": bf16(w_in_even), "w_pool": bf16(w_pool), "pool_scale": pool_scale,
        "w_fft": bf16(w_fft), "w_out_even": bf16(w_out_even), "dftc": dftc,
        "w_in_odd": [w for w, _ in odd], "odd_gains": [g for _, g in odd], "head_mean": head_mean,
        "lamv": jnp.stack([lam_q1, lam_k1, lam_q2, lam_k2], axis=1),
        "c_subln_g": c_subln_g, "d_sink": d_sink, "w_out_odd": bf16(w_out_odd),
        "w_mlp1": bf16(w_mlp1), "w_mlp2": bf16(w_mlp2),
    }

    bsz, n, _ = x_prompt.shape
    caches = (cache_c_k, cache_c_v, cache_d_k, cache_d_v)
    new_cache = []
    xp, xs = x_prompt, x_sample
    for l in range(depth):
        pp = _mixer(xp, l, mod, lambda b: 0, wts, None, new_cache)
        ps = _mixer(xs, l, mod, lambda b: 1 + b, wts, caches, new_cache)
        w_out, out_idx = (wts["w_out_even"], l // 2) if l % 2 == 0 else (wts["w_out_odd"], l // 2)
        flat = lambda a: a.reshape(1, bsz * n, a.shape[-1])
        xp, xs = _out_mlp_call((flat(xp), flat(pp[0]), flat(pp[1]), 0), (xs, ps[0], ps[1], 1),
                               mod, l, wts["norm2_g"][l][None, :], w_out, out_idx,
                               wts["w_mlp1"], wts["w_mlp2"])
        xp = xp.reshape(bsz, n, D_MODEL)
    y_prompt, y_sample = xp, xs
    new_c_k = jnp.stack([nc[0].transpose(0, 4, 1, 2, 3) for nc in new_cache], axis=1)
    new_c_v = jnp.stack([nc[1].reshape(bsz, n, C_HEADS, 2 * HEAD_DIM) for nc in new_cache], axis=1)
    new_d_k = jnp.stack([nc[2].transpose(0, 3, 1, 2) for nc in new_cache], axis=1)
    new_d_v = jnp.stack([nc[3].transpose(0, 3, 1, 2) for nc in new_cache], axis=1)
    return (y_prompt, y_sample, new_c_k, new_c_v, new_d_k, new_d_v)
```

```python
import functools
import math

import numpy as np
import jax
import jax.numpy as jnp
from jax import lax
from jax.experimental import pallas as pl
from jax.experimental.pallas import tpu as pltpu

F32 = jnp.float32
BF16 = jnp.bfloat16

D_MODEL = 1024
HEAD_DIM = 64
LANES = 128
SUBLANES = 8
GRID_W = 64
ROPE_FREQS = HEAD_DIM // 4
ROPE_BASE = 10000.0
EPS = 1e-6
NEG_INF = -1e30
WINDOW = 128
A_WIDTH = D_MODEL // 2
B_WIDTH = D_MODEL // 2
POOL_WINDOWS = (2, 4, 8, 16)
POOL_HALO = 8
N_GROUPS = 4
GROUP_W = A_WIDTH // N_GROUPS
C_HEADS = 4
D_HEADS = 8
D_KV_HEADS = 2
D_GROUP = D_HEADS // D_KV_HEADS
C_W = C_HEADS * 2 * HEAD_DIM
DQ_W = D_HEADS * HEAD_DIM
DKV_W = D_KV_HEADS * HEAD_DIM
DKV_DUP_W = 2 * DKV_W
D_FF = 4 * D_MODEL
ATTN_SCALE = HEAD_DIM ** -0.5
LOG2E = math.log2(math.e)
V_ROWS = LANES + 16
ODD_NORMED_W = C_W + C_W + DQ_W + DKV_DUP_W
ODD_W = ODD_NORMED_W + C_W + DKV_DUP_W
VMEM_LIMIT = 56 * 1024 * 1024


def _lam_init(layer):
    return 0.8 - 0.6 * math.exp(-0.3 * layer)


def _params(n_axes):
    return pltpu.CompilerParams(dimension_semantics=("arbitrary",) * n_axes,
                                vmem_limit_bytes=VMEM_LIMIT)


def _const_spec(shape):
    nd = len(shape)
    return pl.BlockSpec(shape, lambda *_: (0,) * nd, pipeline_mode=pl.Buffered(1))


def _stacked_spec(shape, idx):
    nd = len(shape)
    return pl.BlockSpec((None,) + tuple(shape), lambda *_: (idx,) + (0,) * nd, pipeline_mode=pl.Buffered(1))


def _mod_spec(layer, which, row_of_batch):
    return pl.BlockSpec((None, None, None, 1, D_MODEL),
                        lambda b, i: (layer, row_of_batch(b), which, 0, 0))


def _rms_mod(x, g, sc, sh):
    ms = jnp.mean(x * x, axis=-1, keepdims=True)
    return (x * lax.rsqrt(ms + EPS) * g) * (1.0 + sc) + sh


def _bf16_const(a):
    return jnp.asarray(a, F32).astype(BF16)


def _dot_nt(a, b):
    return lax.dot_general(a, b, (((1,), (1,)), ((), ())), preferred_element_type=F32)


def _dot(a, b):
    return jnp.dot(a, b, preferred_element_type=F32)


def _adaln_kernel(cond_ref, w_ref, b_ref, o_ref):
    cnd = cond_ref[...]
    s = (cnd * jax.nn.sigmoid(cnd)).astype(BF16)
    o_ref[...] = _dot(s, w_ref[...].astype(BF16)) + b_ref[...]


def _adaln_call(cond, w_ada, b_ada):
    depth = w_ada.shape[0]
    rows = cond.shape[0]
    tn = 3 * D_MODEL
    return pl.pallas_call(
        _adaln_kernel,
        grid=(depth, 6 * D_MODEL // tn),
        in_specs=[
            pl.BlockSpec((rows, D_MODEL), lambda l, j: (0, 0)),
            pl.BlockSpec((None, D_MODEL, tn), lambda l, j: (l, 0, j)),
            pl.BlockSpec((None, 1, tn), lambda l, j: (l, 0, j)),
        ],
        out_specs=pl.BlockSpec((None, rows, tn), lambda l, j: (l, 0, j)),
        out_shape=jax.ShapeDtypeStruct((depth, rows, 6 * D_MODEL), F32),
        compiler_params=_params(2),
        name="adaln",
    )(cond, w_ada, b_ada.reshape(depth, 1, 6 * D_MODEL))


def _even_in_kernel(x_ref, xp_ref, xn_ref, sh_ref, sc_ref, g_ref, win_ref, wpool_ref,
                    pscale_ref, dftc_ref, wfft_ref, a_ref, z_ref, *, tm, n):
    i = pl.program_id(1)
    n_tiles = n // tm
    g, sc, sh = g_ref[...], sc_ref[...], sh_ref[...]
    seqs = x_ref.shape[0]
    h = _rms_mod(x_ref[...].reshape(seqs * tm, D_MODEL), g, sc, sh).astype(BF16)
    w_in = win_ref[...].astype(BF16)
    u = _dot(h, w_in)
    xh = jnp.concatenate([xp_ref[...], xn_ref[...]], axis=0)
    hh = _rms_mod(xh, g, sc, sh).astype(BF16)
    uh = _dot(hh, w_in[:, :A_WIDTH])
    has_prev = (i > 0).astype(F32)
    has_next = (i < n_tiles - 1).astype(F32)
    rows = tm + 2 * POOL_HALO

    for gi in range(N_GROUPS):
        ub = u[:, A_WIDTH + gi * GROUP_W:A_WIDTH + (gi + 1) * GROUP_W].astype(BF16)
        cs = _dot(ub, dftc_ref[...])
        w_fft = wfft_ref[gi].astype(BF16)
        zr = _dot(cs[:, :GROUP_W].astype(BF16), w_fft)
        zi = _dot(cs[:, GROUP_W:].astype(BF16), w_fft)
        for q in range(seqs):
            z_ref[q, :, gi * GROUP_W:(gi + 1) * GROUP_W] = zr[q * tm:(q + 1) * tm].astype(BF16)
            z_ref[q, :, B_WIDTH + gi * GROUP_W:B_WIDTH + (gi + 1) * GROUP_W] = zi[q * tm:(q + 1) * tm].astype(BF16)

    t = i * tm + lax.broadcasted_iota(jnp.int32, (tm, 1), 0)
    for q, (gi, w) in ((q, gw) for q in range(seqs) for gw in enumerate(POOL_WINDOWS)):
        cols = slice(gi * GROUP_W, (gi + 1) * GROUP_W)
        uq = u[q * tm:(q + 1) * tm, cols]
        ext = jnp.concatenate([uh[0:POOL_HALO, cols] * has_prev, uq,
                               uh[POOL_HALO:, cols] * has_next], axis=0)
        s = ext
        span = 1
        while span < w:
            s = s + pltpu.roll(s, span, 0)
            span *= 2
        shift = w // 2 - 1
        s = pltpu.roll(s, rows - shift, 0) if shift else s
        acc = s[POOL_HALO:POOL_HALO + tm]
        lo = jnp.maximum(t - w // 2, 0)
        hi = jnp.minimum(t + w // 2, n)
        cnt = (hi - lo).astype(F32)
        pooled = acc / cnt - uq
        y = _dot(pooled.astype(BF16), wpool_ref[gi].astype(BF16)) * pscale_ref[:, cols]
        a_ref[q, :, cols] = y.astype(BF16)


def _even_in_call(x, mod, layer, row_of_batch, g, e, w_in, w_pool, pool_scale, dftc, w_fft, tm, seqs):
    bsz, n, _ = x.shape
    assert seqs == 1 or tm == n
    hb = tm // POOL_HALO
    kern = functools.partial(_even_in_kernel, tm=tm, n=n)
    return pl.pallas_call(
        kern,
        grid=(bsz // seqs, n // tm),
        in_specs=[
            pl.BlockSpec((seqs, tm, D_MODEL), lambda b, i: (b, i, 0)),
            pl.BlockSpec((None, POOL_HALO, D_MODEL),
                         lambda b, i: (b * seqs, jnp.maximum(i * hb - 1, 0), 0)),
            pl.BlockSpec((None, POOL_HALO, D_MODEL),
                         lambda b, i: (b * seqs, jnp.minimum((i + 1) * hb, n // POOL_HALO - 1), 0)),
            _mod_spec(layer, 0, row_of_batch),
            _mod_spec(layer, 1, row_of_batch),
            _const_spec((1, D_MODEL)),
            _stacked_spec((D_MODEL, D_MODEL), e),
            _stacked_spec((N_GROUPS, GROUP_W, GROUP_W), e),
            _const_spec((1, A_WIDTH)),
            _const_spec((GROUP_W, 2 * GROUP_W)),
            _stacked_spec((N_GROUPS, GROUP_W, GROUP_W), e),
        ],
        out_specs=[
            pl.BlockSpec((seqs, tm, A_WIDTH), lambda b, i: (b, i, 0)),
            pl.BlockSpec((seqs, tm, 2 * B_WIDTH), lambda b, i: (b, i, 0)),
        ],
        out_shape=[
            jax.ShapeDtypeStruct((bsz, n, A_WIDTH), BF16),
            jax.ShapeDtypeStruct((bsz, n, 2 * B_WIDTH), BF16),
        ],
        compiler_params=_params(2),
        name="even_in",
    )(x, x, x, mod, mod, g, w_in, w_pool, pool_scale, dftc, w_fft)


def _dft_direct_kernel(z_ref, m_ref, y_ref, *, scale):
    for q in range(z_ref.shape[0]):
        z = z_ref[q]
        rhs = jnp.concatenate([z[:, :B_WIDTH], z[:, B_WIDTH:]], axis=0)
        y_ref[q] = (_dot(m_ref[...], rhs) * scale).astype(BF16)


def _dft_direct_call(z):
    bsz, n, _ = z.shape
    k = np.arange(n)
    ang = 2.0 * np.pi * np.outer(k, k) / n
    m = _bf16_const(np.concatenate([np.cos(ang), np.sin(ang)], axis=1))
    kern = functools.partial(_dft_direct_kernel, scale=float((n * GROUP_W) ** -0.5))
    seqs = next(q for q in (4, 2, 1) if bsz % q == 0)
    return pl.pallas_call(
        kern,
        grid=(bsz // seqs,),
        in_specs=[pl.BlockSpec((seqs, n, 2 * B_WIDTH), lambda b: (b, 0, 0)),
                  _const_spec((n, 2 * n))],
        out_specs=pl.BlockSpec((seqs, n, B_WIDTH), lambda b: (b, 0, 0)),
        out_shape=jax.ShapeDtypeStruct((bsz, n, B_WIDTH), BF16),
        compiler_params=_params(1),
        name="dft_direct",
    )(z, m)


DFT_ROWS = 16
DFT_DIRECT_MAX = 256
DFT_PITCH = 24


def _pitch_rows(x):
    g, _, w = x.shape
    pad = jnp.zeros((g, DFT_PITCH - DFT_ROWS, w), x.dtype)
    return jnp.concatenate([x, pad], axis=1).reshape(g * DFT_PITCH, w)


def _dft_stage1_kernel(z_ref, m_ref, twr_ref, twi_ref, o_ref, zs_ref, os_ref, *, na):
    nt = 2 * B_WIDTH // LANES

    @pl.when((pl.program_id(0) == 0) & (pl.program_id(1) == 0))
    def _():
        os_ref[...] = jnp.zeros_like(os_ref)

    zf = _pitch_rows(z_ref[...].astype(F32))
    for k in range(nt):
        zs_ref[k] = zf[:, k * LANES:(k + 1) * LANES]
    for bi in range(DFT_ROWS):
        rows = pl.ds(bi, na, stride=DFT_PITCH)
        zb = [zs_ref[k, rows, :].astype(BF16) for k in range(nt)]
        rhs = jnp.concatenate([jnp.concatenate(zb[:nt // 2], axis=1),
                               jnp.concatenate(zb[nt // 2:], axis=1)], axis=0)
        r = _dot(m_ref[...], rhs)
        br, bim = r[:na], r[na:]
        tr, ti = twr_ref[bi], twi_ref[bi]
        out = (br * tr - bim * ti, br * ti + bim * tr)
        for k in range(nt):
            half, kk = divmod(k, nt // 2)
            os_ref[k, rows, :] = out[half][:, kk * LANES:(kk + 1) * LANES]
    of = jnp.concatenate([os_ref[k] for k in range(nt)], axis=1)
    o_ref[...] = of.reshape(na, DFT_PITCH, 2 * B_WIDTH)[:, :DFT_ROWS].astype(BF16)


def _dft_stage2_kernel(b_ref, m_ref, y_ref, ys_ref, *, scale):
    nb = b_ref.shape[1]

    @pl.when((pl.program_id(0) == 0) & (pl.program_id(1) == 0))
    def _():
        ys_ref[...] = jnp.zeros_like(ys_ref)

    for di in range(DFT_ROWS):
        bb = b_ref[di]
        rhs = jnp.concatenate([bb[:, :B_WIDTH], bb[:, B_WIDTH:]], axis=0)
        y = _dot(m_ref[...], rhs) * scale
        for k in range(B_WIDTH // LANES):
            ys_ref[k, pl.ds(di, nb, stride=DFT_PITCH), :] = y[:, k * LANES:(k + 1) * LANES]
    yf = jnp.concatenate([ys_ref[k] for k in range(B_WIDTH // LANES)], axis=1)
    y_ref[...] = yf.reshape(nb, DFT_PITCH, B_WIDTH)[:, :DFT_ROWS].astype(BF16)


def _dft_two_stage_call(z):
    bsz, n, _ = z.shape
    na = 1 << (int(math.log2(n)) // 2)
    nb = n // na
    ia = np.arange(na)
    ib = np.arange(nb)
    ang_a = 2.0 * np.pi * np.outer(ia, ia) / na
    fr, fi = np.cos(ang_a), -np.sin(ang_a)
    m1 = _bf16_const(np.block([[fr, -fi], [fi, fr]]))
    ang_t = 2.0 * np.pi * np.outer(ib, ia) / n
    twr = jnp.asarray(np.cos(ang_t)[:, :, None], F32)
    twi = jnp.asarray(-np.sin(ang_t)[:, :, None], F32)
    ang_b = 2.0 * np.pi * np.outer(ib, ib) / nb
    m2 = _bf16_const(np.concatenate([np.cos(ang_b), np.sin(ang_b)], axis=1))

    s1 = pl.pallas_call(
        functools.partial(_dft_stage1_kernel, na=na),
        grid=(bsz, nb // DFT_ROWS),
        in_specs=[
            pl.BlockSpec((None, na, DFT_ROWS, 2 * B_WIDTH), lambda b, j: (b, 0, j, 0)),
            _const_spec((2 * na, 2 * na)),
            pl.BlockSpec((DFT_ROWS, na, 1), lambda b, j: (j, 0, 0)),
            pl.BlockSpec((DFT_ROWS, na, 1), lambda b, j: (j, 0, 0)),
        ],
        out_specs=pl.BlockSpec((None, na, DFT_ROWS, 2 * B_WIDTH), lambda b, j: (b, 0, j, 0)),
        out_shape=jax.ShapeDtypeStruct((bsz, na, nb, 2 * B_WIDTH), BF16),
        scratch_shapes=[pltpu.VMEM((2 * B_WIDTH // LANES, na * DFT_PITCH, LANES), F32)] * 2,
        compiler_params=_params(2),
        name="dft_stage1",
    )(z.reshape(bsz, na, nb, 2 * B_WIDTH), m1, twr, twi)
    y = pl.pallas_call(
        functools.partial(_dft_stage2_kernel, scale=float((n * GROUP_W) ** -0.5)),
        grid=(bsz, na // DFT_ROWS),
        in_specs=[
            pl.BlockSpec((None, DFT_ROWS, nb, 2 * B_WIDTH), lambda b, j: (b, j, 0, 0)),
            _const_spec((nb, 2 * nb)),
        ],
        out_specs=pl.BlockSpec((None, nb, DFT_ROWS, B_WIDTH), lambda b, j: (b, 0, j, 0)),
        out_shape=jax.ShapeDtypeStruct((bsz, nb, na, B_WIDTH), BF16),
        scratch_shapes=[pltpu.VMEM((B_WIDTH // LANES, nb * DFT_PITCH, LANES), F32)],
        compiler_params=_params(2),
        name="dft_stage2",
    )(s1, m2)
    return y.reshape(bsz, n, B_WIDTH)


FF_CHUNK = 1024


def _out_mlp_kernel(xa_ref, p1a_ref, p2a_ref, xb_ref, p1b_ref, p2b_ref, g1_ref, sh_ref, sc_ref, g2_ref,
                    ng_ref, wout_ref, w1_ref, w2_ref, oa_ref, ob_ref, *, steps_a):
    def body(x_ref, p1_ref, p2_ref, o_ref):
        half = p1_ref.shape[-1]
        mix = (_dot(p1_ref[...], wout_ref[:half, :].astype(BF16))
               + _dot(p2_ref[...], wout_ref[half:, :].astype(BF16)))
        x1 = x_ref[...] + g1_ref[...] * mix
        h = _rms_mod(x1, ng_ref[...], sc_ref[...], sh_ref[...]).astype(BF16)
        acc = None
        for c in range(D_FF // FF_CHUNK):
            a = _dot(h, w1_ref[:, c * FF_CHUNK:(c + 1) * FF_CHUNK])
            a = jnp.square(jnp.maximum(a, 0.0)).astype(BF16)
            part = _dot(a, w2_ref[c * FF_CHUNK:(c + 1) * FF_CHUNK, :])
            acc = part if acc is None else acc + part
        o_ref[...] = x1 + g2_ref[...] * acc

    @pl.when(pl.program_id(0) < steps_a)
    def _():
        body(xa_ref, p1a_ref, p2a_ref, oa_ref)

    @pl.when(pl.program_id(0) >= steps_a)
    def _():
        body(xb_ref, p1b_ref, p2b_ref, ob_ref)


def _out_mlp_call(set_a, set_b, mod, layer, ng, w_out, out_idx, w1, w2):
    (xa, p1a, p2a, row_a), (xb, p1b, p2b, row_b) = set_a, set_b
    half = p1a.shape[-1]
    tm = min(xa.shape[1], xb.shape[1], MLP_TILE)
    tiles_a, tiles_b = xa.shape[1] // tm, xb.shape[1] // tm
    steps_a, steps_b = xa.shape[0] * tiles_a, xb.shape[0] * tiles_b

    def pos_a(s):
        sa = jnp.minimum(s, steps_a - 1)
        return sa // tiles_a, sa % tiles_a

    def pos_b(s):
        sb = jnp.maximum(s - steps_a, 0)
        return sb // tiles_b, sb % tiles_b

    tok_a = lambda w: pl.BlockSpec((None, tm, w), lambda s: pos_a(s) + (0,))
    tok_b = lambda w: pl.BlockSpec((None, tm, w), lambda s: pos_b(s) + (0,))
    row = lambda s: jnp.where(s < steps_a, row_a + pos_a(s)[0], row_b + pos_b(s)[0])
    mod_spec = lambda which: pl.BlockSpec((None, None, None, 1, D_MODEL),
                                          lambda s: (layer, row(s), which, 0, 0))
    return pl.pallas_call(
        functools.partial(_out_mlp_kernel, steps_a=steps_a),
        grid=(steps_a + steps_b,),
        in_specs=[
            tok_a(D_MODEL), tok_a(half), tok_a(half), tok_b(D_MODEL), tok_b(half), tok_b(half),
            mod_spec(2), mod_spec(3), mod_spec(4), mod_spec(5),
            _const_spec((1, D_MODEL)),
            _stacked_spec((2 * half, D_MODEL), out_idx),
            _stacked_spec((D_MODEL, D_FF), layer),
            _stacked_spec((D_FF, D_MODEL), layer),
        ],
        out_specs=[tok_a(D_MODEL), tok_b(D_MODEL)],
        out_shape=[jax.ShapeDtypeStruct(xa.shape, F32), jax.ShapeDtypeStruct(xb.shape, F32)],
        compiler_params=_params(1),
        name="out_mlp",
    )(xa, p1a, p2a, xb, p1b, p2b, mod, mod, mod, mod, ng, w_out, w1, w2)


def _odd_in_kernel(*refs, rope, emit_cache, tm):
    x_ref, sh_ref, sc_ref, g_ref, win_ref, gain_ref, hm_ref = refs[:7]
    k = 7
    if rope:
        cos_ref, sin_ref = refs[k:k + 2]
        k += 2
    cq_ref, ck_ref, cvt_ref, dq_ref, dk_ref, dvt_ref = refs[k:k + 6]
    k += 6
    if emit_cache:
        nck_ref, ncv_ref, ndk_ref, ndv_ref = refs[k:k + 4]
    subs = x_ref.shape[0] // tm
    lane = lax.broadcasted_iota(jnp.int32, (tm, LANES), 1)
    first_half = (lane % (2 * ROPE_FREQS)) < ROPE_FREQS
    pad_row = lax.broadcasted_iota(jnp.int32, (V_ROWS - LANES, tm), 0)
    pad = jnp.where(pad_row == 0, 1.0, 0.0).astype(BF16)

    def cache_rows(ref, sub, r):
        rpt = ref.shape[0] // x_ref.shape[0]
        return ref.at[pl.ds(sub * tm * rpt + r, tm, stride=rpt), :]

    def cache_cols(ref, sub, lead, mat):
        n_req = ref.shape[-1]
        for r in range(tm // n_req):
            ref[(sub * (tm // n_req) + r,) + lead] = mat[:, r * n_req:(r + 1) * n_req]

    groups = ((0, 2 * C_W), (2 * C_W, ODD_NORMED_W), (ODD_NORMED_W, ODD_W))
    hidden = {}

    def project(item):
        sub, grp = item
        if sub not in hidden:
            x = x_ref[sub * tm:(sub + 1) * tm, :]
            hidden[sub] = _rms_mod(x, g_ref[...], sc_ref[...], sh_ref[...]).astype(BF16)
        lo, hi = groups[grp]
        return _dot(hidden[sub], win_ref[:, lo:hi])

    def head_norms(p, lo):
        normed = []
        for c in range(p.shape[1] // (2 * LANES)):
            v = p[:, c * 2 * LANES:(c + 1) * 2 * LANES]
            ms = _dot((v * v).astype(BF16), hm_ref[...])
            y = v * lax.rsqrt(ms + EPS) * gain_ref[:, lo + c * 2 * LANES:lo + (c + 1) * 2 * LANES]
            normed += [y[:, :LANES], y[:, LANES:]]
        return normed

    def finish(item, p, normed):
        sub, grp = item
        rows = slice(sub * tm, (sub + 1) * tm)

        def roped(y):
            if not rope:
                return y
            rot = jnp.where(first_half, -pltpu.roll(y, LANES - ROPE_FREQS, 1),
                            pltpu.roll(y, ROPE_FREQS, 1))
            return y * cos_ref[rows, :] + rot * sin_ref[rows, :]

        def put(ref, chunks):
            for c, y in enumerate(chunks):
                ref[rows, c * LANES:(c + 1) * LANES] = roped(y).astype(BF16)

        nq = C_W // LANES
        if grp == 0:
            put(cq_ref, normed[:nq])
            put(ck_ref, normed[nq:])
            if emit_cache:
                for c, y in enumerate(normed[nq:]):
                    yt = y.T
                    for half in range(2):
                        cache_cols(nck_ref, sub, (c, half), yt[half * HEAD_DIM:(half + 1) * HEAD_DIM])
        elif grp == 1:
            put(dq_ref, normed[:nq])
            put(dk_ref, normed[nq:])
            if emit_cache:
                for j, y in enumerate(normed[nq:]):
                    cache_cols(ndk_ref, sub, (j,), y.T[:HEAD_DIM])
        else:
            cv, dv = p[:, :C_W], p[:, C_W:]
            for v, vt_ref in ((cv, cvt_ref), (dv, dvt_ref)):
                for hh in range(vt_ref.shape[0]):
                    vt = v[:, hh * LANES:(hh + 1) * LANES].T
                    vt_ref[hh, :LANES, rows] = vt.astype(BF16)
                    vt_ref[hh, LANES:, rows] = pad
                    if emit_cache and vt_ref is dvt_ref:
                        cache_cols(ndv_ref, sub, (hh,), vt[:HEAD_DIM])
            if emit_cache:
                for hh in range(C_HEADS):
                    cache_rows(ncv_ref, sub, hh)[...] = cv[:, hh * LANES:(hh + 1) * LANES]

    items = [(sub, grp) for sub in range(subs) for grp in range(len(groups))]
    p = project(items[0])
    for idx, item in enumerate(items):
        normed = head_norms(p, groups[item[1]][0]) if item[1] < 2 else None
        nxt = project(items[idx + 1]) if idx + 1 < len(items) else None
        finish(item, p, normed)
        p = nxt


def _odd_in_call(x, mod, layer, row_of_batch, g, w_in, gains, head_mean, rope_tabs, tm, cache_n=None):
    emit_cache = cache_n is not None
    bsz, n, _ = x.shape
    rope = rope_tabs is not None
    subs = 2 if n % (2 * tm) == 0 else 1
    ts = subs * tm
    tok = lambda w: pl.BlockSpec((None, ts, w), lambda b, i: (b, i, 0))
    in_specs = [
        tok(D_MODEL),
        _mod_spec(layer, 0, row_of_batch),
        _mod_spec(layer, 1, row_of_batch),
        _const_spec((1, D_MODEL)),
        _const_spec((D_MODEL, ODD_W)),
        _const_spec((1, ODD_NORMED_W)),
        _const_spec((2 * LANES, 2 * LANES)),
    ]
    args = [x, mod, mod, g, w_in, gains, head_mean]
    if rope:
        in_specs += [pl.BlockSpec((ts, LANES), lambda b, i: (i, 0))] * 2
        args += list(rope_tabs)
    vt_spec = lambda heads: pl.BlockSpec((None, heads, V_ROWS, ts), lambda b, i: (b, 0, 0, i))
    vt_shape = lambda heads: jax.ShapeDtypeStruct((bsz, heads, V_ROWS, n), BF16)
    tok_shape = lambda w: jax.ShapeDtypeStruct((bsz, n, w), BF16)
    out_specs = [tok(C_W), tok(C_W), vt_spec(C_HEADS), tok(DQ_W), tok(DKV_DUP_W), vt_spec(D_KV_HEADS)]
    out_shape = [tok_shape(C_W), tok_shape(C_W), vt_shape(C_HEADS), tok_shape(DQ_W),
                 tok_shape(DKV_DUP_W), vt_shape(D_KV_HEADS)]
    if emit_cache:
        assert bsz == 1 and tm % cache_n == 0 and n % cache_n == 0
        reqs, rps = n // cache_n, ts // cache_n
        for lead in ((C_HEADS, 2), None, (D_KV_HEADS,), (D_KV_HEADS,)):
            if lead is None:
                out_specs.append(pl.BlockSpec((None, ts * C_HEADS, LANES), lambda b, i: (b, i, 0)))
                out_shape.append(jax.ShapeDtypeStruct((bsz, n * C_HEADS, LANES), F32))
            else:
                zeros = (0,) * (len(lead) + 2)
                out_specs.append(pl.BlockSpec((rps,) + lead + (HEAD_DIM, cache_n),
                                              lambda b, i, zeros=zeros: (i,) + zeros))
                out_shape.append(jax.ShapeDtypeStruct((reqs,) + lead + (HEAD_DIM, cache_n), F32))
    return pl.pallas_call(
        functools.partial(_odd_in_kernel, rope=rope, emit_cache=emit_cache, tm=tm),
        grid=(bsz, n // ts),
        in_specs=in_specs,
        out_specs=out_specs,
        out_shape=out_shape,
        compiler_params=_params(2),
        name="odd_in",
    )(*args)


QK_AHEAD = 3


def _diff_attn_kernel(*refs, lam_init, tq, kc):
    n_src = (len(refs) - 4) // 2
    q_ref, (lamv_ref, g_ref, o_ref) = refs[0], refs[-3:]
    k_refs, vt_refs = refs[1:1 + n_src], refs[1 + n_src:1 + 2 * n_src]
    chunks = [(src, off) for src in range(n_src) for off in range(0, k_refs[src].shape[1], kc)]
    n_chunks = len(chunks)
    reqs = q_ref.shape[0]
    tiles = q_ref.shape[1] // tq
    heads = q_ref.shape[2] // LANES
    lane = lax.broadcasted_iota(jnp.int32, (tq, LANES), 1)
    lv = lamv_ref[...]
    lam = (jnp.exp(jnp.sum(lv[0:1] * lv[1:2], keepdims=True))
           - jnp.exp(jnp.sum(lv[2:3] * lv[3:4], keepdims=True)) + lam_init)

    qqs = {}

    def score(item):
        r, h, t, c = item
        cols = slice(h * LANES, (h + 1) * LANES)
        if (r, h, t) not in qqs:
            q = q_ref[r, t * tq:(t + 1) * tq, cols]
            zero = jnp.zeros_like(q)
            qqs[r, h, t] = jnp.concatenate([jnp.where(lane < HEAD_DIM, q, zero),
                                            jnp.where(lane >= HEAD_DIM, q, zero)], axis=0)
        src, off = chunks[c]
        return _dot_nt(k_refs[src][r, off:off + kc, cols], qqs[r, h, t])

    def finish(r, h, t, acc, l):
        inv = 1.0 / l
        ot = acc[:, :tq] * inv[:, :tq] - acc[:, tq:] * (inv[:, tq:] * lam)
        o = ot.T
        ms = jnp.mean(o * o, axis=-1, keepdims=True)
        y = o * lax.rsqrt(ms + EPS) * g_ref[...] * (1.0 - lam_init)
        o_ref[r, t * tq:(t + 1) * tq, h * LANES:(h + 1) * LANES] = y.astype(BF16)

    items = [(r, h, t, c) for r in range(reqs) for h in range(heads) for t in range(tiles)
             for c in range(n_chunks)]
    ahead = [score(it) for it in items[:QK_AHEAD]]
    m = acc = None
    for idx, (r, h, t, c) in enumerate(items):
        s = ahead.pop(0)
        if idx + QK_AHEAD < len(items):
            ahead.append(score(items[idx + QK_AHEAD]))
        cm = jnp.max(s, axis=0, keepdims=True)
        m_new = cm if c == 0 else jnp.maximum(m, cm)
        src, off = chunks[c]
        e = jnp.exp2(s - m_new)
        lsum = jnp.sum(e, axis=0, keepdims=True)
        lane0 = r * k_refs[src].shape[1] + off
        pv = _dot(vt_refs[src][h, :LANES, lane0:lane0 + kc], e.astype(BF16))
        if c == 0:
            acc, l = pv, lsum
        else:
            alpha = jnp.exp2(m - m_new)
            acc, l = acc * alpha + pv, l * alpha + lsum
        m = m_new
        if c == n_chunks - 1:
            finish(r, h, t, acc, l)


def _diff_attn_call(q, ks, vts, lamv, subln_g, lam_init, tq, tiles, heads, reqs=1, flat_values=False):
    bsz, n, _ = q.shape
    assert reqs == 1 or flat_values
    kc = next(c for c in (512, 256, 128) if all(k.shape[1] % c == 0 for k in ks))
    hw = heads * LANES
    k_specs = [pl.BlockSpec((reqs, k.shape[1], hw), lambda b, h, i: (b, 0, h)) for k in ks]
    vt_index = (lambda b, h, i: (0, h, 0, b)) if flat_values else (lambda b, h, i: (b, h, 0, 0))
    vt_specs = [pl.BlockSpec((None, heads, V_ROWS, reqs * k.shape[1]), vt_index) for k in ks]
    return pl.pallas_call(
        functools.partial(_diff_attn_kernel, lam_init=lam_init, tq=tq, kc=kc),
        grid=(bsz // reqs, C_HEADS // heads, n // (tq * tiles)),
        in_specs=[pl.BlockSpec((reqs, tiles * tq, hw), lambda b, h, i: (b, i, h))]
        + k_specs + vt_specs + [_const_spec((4, HEAD_DIM)), _const_spec((1, LANES))],
        out_specs=pl.BlockSpec((reqs, tiles * tq, hw), lambda b, h, i: (b, i, h)),
        out_shape=jax.ShapeDtypeStruct((bsz, n, C_W), BF16),
        compiler_params=_params(3),
        name="diff_attn",
    )(q, *ks, *vts, lamv, subln_g)


def _sink_attn_kernel(*refs, windowed, tq, nblk):
    if windowed:
        (q_ref, ck_ref, kp_ref, km_ref, kn_ref, cvt_ref, vp_ref, vm_ref, vn_ref,
         bias_ref, sink_ref, o_ref) = refs
        kband = jnp.concatenate([kp_ref[...], km_ref[...], kn_ref[...]], axis=0)
        vtband = jnp.concatenate([vp_ref[...], vm_ref[...], vn_ref[...]], axis=1)
    else:
        q_ref, k_ref, vt_ref, sink_ref, o_ref = refs
    reqs = q_ref.shape[0]
    subs = q_ref.shape[1] // tq
    groups = q_ref.shape[2] // (2 * LANES)
    lane = lax.broadcasted_iota(jnp.int32, (tq, LANES), 1)
    lo = lane < HEAD_DIM

    def scores_of(stream):
        r, g, sub = stream
        qd = q_ref[r, sub * tq:(sub + 1) * tq, g * 2 * LANES:(g + 1) * 2 * LANES]
        parts = []
        for c in range(2):
            ch = qd[:, c * LANES:(c + 1) * LANES]
            zero = jnp.zeros_like(ch)
            parts += [jnp.where(lo, ch, zero), jnp.where(lo, zero, ch)]
        qq = jnp.concatenate(parts, axis=0)
        if not windowed:
            return [_dot_nt(k_ref[r, :, g * LANES:(g + 1) * LANES], qq)]
        blk = pl.program_id(2) * subs + sub
        variant = jnp.where(blk == 0, 0, jnp.where(blk == nblk - 1, 2, 1))
        band = _dot_nt(kband[sub * tq:(sub + 3) * tq, :], qq) + bias_ref[variant]
        return [_dot_nt(ck_ref[...], qq), band]

    def finish(stream, scores):
        r, g, sub = stream
        values = ([cvt_ref[...], vtband[:, sub * tq:(sub + 3) * tq]] if windowed
                  else [vt_ref[g, :, r * tq:(r + 1) * tq]])
        sink = sink_ref[g]
        m = sink
        for s in scores:
            m = jnp.maximum(m, jnp.max(s, axis=0, keepdims=True))
        acc = None
        for s, vt in zip(scores, values):
            pv = _dot(vt, jnp.exp2(s - m).astype(BF16))
            acc = pv if acc is None else acc + pv
        inv = 1.0 / (acc[LANES:LANES + 1, :] + jnp.exp2(sink - m))
        ot = acc[:LANES, :] * inv
        for c in range(2):
            o0 = ot[:, (2 * c) * tq:(2 * c + 1) * tq].T
            o1 = ot[:, (2 * c + 1) * tq:(2 * c + 2) * tq].T
            cols = slice((2 * g + c) * LANES, (2 * g + c + 1) * LANES)
            o_ref[r, sub * tq:(sub + 1) * tq, cols] = jnp.where(lo, o0, o1).astype(BF16)

    streams = [(r, g, sub) for r in range(reqs) for g in range(groups) for sub in range(subs)]
    ahead = [scores_of(st) for st in streams[:QK_AHEAD]]
    for idx, st in enumerate(streams):
        cur = ahead.pop(0)
        if idx + QK_AHEAD < len(streams):
            ahead.append(scores_of(streams[idx + QK_AHEAD]))
        finish(st, cur)


def _window_bias(n, tq):
    nblk = n // tq
    kk = np.arange(3 * tq)[:, None]
    qi = np.arange(tq)[None, :]
    band_ok = np.abs(qi + WINDOW - kk) <= WINDOW
    out = []
    for blk in (0, 1, nblk - 1):
        jpos = blk * tq - WINDOW + kk
        ok = band_ok & (jpos >= 0) & (jpos < n)
        out.append(np.tile(np.where(ok, 0.0, NEG_INF), (1, D_GROUP)))
    return jnp.asarray(np.stack(out), F32)


def _sink_attn_call(q, k, vt, sink_rows, ctx_k=None, ctx_vt=None, reqs=1, flat_values=False):
    bsz, n, _ = q.shape
    windowed = ctx_k is not None
    assert reqs == 1 or (flat_values and not windowed)
    tq = WINDOW if windowed else n
    nblk = n // tq
    subs = next(s for s in (16, 8, 4, 2, 1) if nblk % s == 0) if windowed else 1
    steps = nblk // subs
    groups = 1 if windowed else D_KV_HEADS
    q_spec = pl.BlockSpec((reqs, subs * tq, groups * 2 * LANES), lambda b, j, i: (b, i, j))
    sink_spec = pl.BlockSpec((groups, 1, D_GROUP * tq), lambda b, j, i: (j, 0, 0))
    if windowed:
        assert nblk >= 3, "first / interior / last mask variants need three query blocks"
        n_ctx = ctx_k.shape[1]
        prev = lambda i: jnp.maximum(i * subs - 1, 0)
        nxt = lambda i: jnp.minimum((i + 1) * subs, nblk - 1)
        in_specs = [
            q_spec,
            pl.BlockSpec((None, n_ctx, LANES), lambda b, j, i: (b, 0, j)),
            pl.BlockSpec((None, tq, LANES), lambda b, j, i: (b, prev(i), j)),
            pl.BlockSpec((None, subs * tq, LANES), lambda b, j, i: (b, i, j)),
            pl.BlockSpec((None, tq, LANES), lambda b, j, i: (b, nxt(i), j)),
            pl.BlockSpec((None, None, V_ROWS, n_ctx), lambda b, j, i: (b, j, 0, 0)),
            pl.BlockSpec((None, None, V_ROWS, tq), lambda b, j, i: (b, j, 0, prev(i))),
            pl.BlockSpec((None, None, V_ROWS, subs * tq), lambda b, j, i: (b, j, 0, i)),
            pl.BlockSpec((None, None, V_ROWS, tq), lambda b, j, i: (b, j, 0, nxt(i))),
            _const_spec((3, 3 * tq, D_GROUP * tq)),
            sink_spec,
        ]
        args = (q, ctx_k, k, k, k, ctx_vt, vt, vt, vt, _window_bias(n, tq), sink_rows)
    else:
        in_specs = [
            q_spec,
            pl.BlockSpec((reqs, n, groups * LANES), lambda b, j, i: (b, 0, j)),
            pl.BlockSpec((None, groups, V_ROWS, reqs * n),
                         (lambda b, j, i: (0, j, 0, b)) if flat_values else (lambda b, j, i: (b, j, 0, 0))),
            sink_spec,
        ]
        args = (q, k, vt, sink_rows)
    return pl.pallas_call(
        functools.partial(_sink_attn_kernel, windowed=windowed, tq=tq, nblk=nblk),
        grid=(bsz // reqs, D_KV_HEADS // groups, steps),
        in_specs=in_specs,
        out_specs=q_spec,
        out_shape=jax.ShapeDtypeStruct((bsz, n, DQ_W), BF16),
        compiler_params=_params(3),
        name="sink_attn",
    )(*args)


def _dup_heads(a):
    lead = a.shape[:-1]
    a = a.reshape(lead + (D_KV_HEADS, 1, HEAD_DIM))
    return jnp.broadcast_to(a, lead + (D_KV_HEADS, 2, HEAD_DIM)).reshape(lead + (DKV_DUP_W,))


def _odd_weights(w_in_odd, c_qn_g, c_kn_g, d_qn_g, d_kn_g):
    cq, ck, cv, dq, dk, dv = jnp.split(
        w_in_odd, (C_W, 2 * C_W, 3 * C_W, 3 * C_W + DQ_W, 3 * C_W + DQ_W + DKV_W), axis=-1)
    w = jnp.concatenate([cq, ck, dq, _dup_heads(dk), cv, _dup_heads(dv)], axis=-1).astype(BF16)
    gains = jnp.concatenate([
        jnp.tile(c_qn_g * (ATTN_SCALE * LOG2E), C_W // HEAD_DIM),
        jnp.tile(c_kn_g, C_W // HEAD_DIM),
        jnp.tile(d_qn_g * (ATTN_SCALE * LOG2E), DQ_W // HEAD_DIM),
        jnp.tile(d_kn_g, DKV_DUP_W // HEAD_DIM),
    ])[None, :]
    return w, gains


def _value_rows(v):
    bsz, nk, w = v.shape
    vt = v.reshape(bsz, nk, w // LANES, LANES).transpose(0, 2, 3, 1)
    pad = jnp.zeros((bsz, w // LANES, V_ROWS - LANES, nk), v.dtype).at[:, :, 0, :].set(1)
    return jnp.concatenate([vt, pad], axis=2)


def _rope_tables(n):
    rows = n // GRID_W
    row = np.repeat(np.arange(rows), GRID_W).astype(np.float64)
    col = np.tile(np.arange(GRID_W), rows).astype(np.float64)
    freqs = ROPE_BASE ** (-np.arange(ROPE_FREQS, dtype=np.float64) / ROPE_FREQS)
    ang = np.stack([row[:, None] * freqs, col[:, None] * freqs], axis=1)
    ang = np.concatenate([ang, ang], axis=-1).reshape(n, HEAD_DIM)
    ang = np.concatenate([ang, ang], axis=-1)
    return jnp.asarray(np.cos(ang), F32), jnp.asarray(np.sin(ang), F32)


MLP_TILE = 512


def _token_tile(n):
    return min(n, 512)


def _mixer(x, l, mod, row_of_batch, wts, caches, new_cache):
    bsz, n, _ = x.shape
    decode = caches is not None
    g1n = wts["norm1_g"][l][None, :]
    if l % 2 == 0:
        e = l // 2
        a, z = _even_in_call(x, mod, l, row_of_batch, g1n, e, wts["w_in_even"], wts["w_pool"],
                             wts["pool_scale"][e][None, :], wts["dftc"], wts["w_fft"], min(n, 1024),
                             1 if decode or n > 1024 else next(q for q in (4, 2, 1) if bsz % q == 0))
        y = _dft_two_stage_call(z) if n > DFT_DIRECT_MAX else _dft_direct_call(z)
        return a, y
    o = l // 2
    lam_init = _lam_init(l)
    rope_tabs = _rope_tables(n) if decode else None
    xo = x if decode else x.reshape(1, bsz * n, D_MODEL)
    outs = _odd_in_call(xo, mod, l, row_of_batch, g1n, wts["w_in_odd"][o], wts["odd_gains"][o],
                        wts["head_mean"], rope_tabs, _token_tile(xo.shape[1]),
                        None if decode else n)
    if not decode:
        outs = [a if a.ndim == 4 else a.reshape(bsz, n, a.shape[-1]) for a in outs[:6]] + list(outs[6:])
    cq, ck, cvt, dq, dkk, dvt = outs[:6]
    if decode:
        c_k, c_v, d_k, d_v = caches
        lc = c_k.shape[2]
        ks = [c_k[:, o].reshape(bsz, lc, C_W).astype(BF16), ck]
        vts = [_value_rows(c_v[:, o].reshape(bsz, lc, C_W).astype(BF16)), cvt]
        ctx_k = _dup_heads(d_k[:, o].reshape(bsz, lc, DKV_W)).astype(BF16)
        ctx_vt = _value_rows(_dup_heads(d_v[:, o].reshape(bsz, lc, DKV_W)).astype(BF16))
        tq = min(n, 256)
    else:
        new_cache.append(outs[6:])
        ks, vts = [ck], [cvt]
        ctx_k = ctx_vt = None
        tq = n
    tiles = next(t for t in (4, 2, 1) if n % (t * tq) == 0)
    heads = 1 if decode else C_HEADS
    c_out = _diff_attn_call(cq, ks, vts, wts["lamv"][o],
                            wts["c_subln_g"][o][None, :], lam_init, tq, tiles, heads,
                            1 if decode else next(r for r in (4, 2, 1) if bsz % r == 0), not decode)
    sink_tq = WINDOW if decode else n
    sink_rows = jnp.repeat(wts["d_sink"][o].reshape(D_KV_HEADS, D_GROUP) * LOG2E, sink_tq,
                           axis=-1)[:, None, :]
    d_out = _sink_attn_call(dq, dkk, dvt, sink_rows, ctx_k, ctx_vt,
                            1 if decode else next(r for r in (4, 2, 1) if bsz % r == 0), not decode)
    return c_out, d_out


def kernel(x_prompt, x_sample, c, cache_c_k, cache_c_v, cache_d_k, cache_d_v, c_ctx,
           norm1_g, norm2_g, w_ada, b_ada, w_in_even, w_pool, pool_scale, w_fft, w_out_even,
           w_in_odd, c_qn_g, c_kn_g, lam_q1, lam_k1, lam_q2, lam_k2, c_subln_g,
           d_qn_g, d_kn_g, d_sink, w_out_odd, w_mlp1, w_mlp2):
    depth = norm1_g.shape[0]
    n_odd = w_in_odd.shape[0]
    dec_b = c.shape[0]

    rows = -(-(1 + dec_b) // SUBLANES) * SUBLANES
    cond = jnp.concatenate([c_ctx[None, :], c, jnp.zeros((rows - 1 - dec_b, D_MODEL), F32)], axis=0)
    mod = _adaln_call(cond, w_ada, b_ada).reshape(depth, rows, 6, 1, D_MODEL)

    ic = np.arange(GROUP_W)
    ang = 2.0 * np.pi * np.outer(ic, ic) / GROUP_W
    dftc = _bf16_const(np.concatenate([np.cos(ang), -np.sin(ang)], axis=1))
    head_mean = _bf16_const(np.kron(np.eye(2 * LANES // HEAD_DIM), np.ones((HEAD_DIM, HEAD_DIM))) / HEAD_DIM)

    odd = [_odd_weights(w_in_odd[o], c_qn_g[o], c_kn_g[o], d_qn_g[o], d_kn_g[o]) for o in range(n_odd)]
    bf16 = lambda w: w.astype(BF16)
    wts = {
        "norm1_g": norm1_g, "norm2_g": norm2_g,
        "w_in_even": w_in_even, "w_pool": w_pool, "pool_scale": pool_scale,
        "w_fft": w_fft, "w_out_even": w_out_even, "dftc": dftc,
        "w_in_odd": [w for w, _ in odd], "odd_gains": [g for _, g in odd], "head_mean": head_mean,
        "lamv": jnp.stack([lam_q1, lam_k1, lam_q2, lam_k2], axis=1),
        "c_subln_g": c_subln_g, "d_sink": d_sink, "w_out_odd": w_out_odd,
        "w_mlp1": bf16(w_mlp1), "w_mlp2": bf16(w_mlp2),
    }

    bsz, n, _ = x_prompt.shape
    caches = (cache_c_k, cache_c_v, cache_d_k, cache_d_v)
    new_cache = []
    xp, xs = x_prompt, x_sample
    for l in range(depth):
        pp = _mixer(xp, l, mod, lambda b: 0, wts, None, new_cache)
        ps = _mixer(xs, l, mod, lambda b: 1 + b, wts, caches, new_cache)
        w_out, out_idx = (wts["w_out_even"], l // 2) if l % 2 == 0 else (wts["w_out_odd"], l // 2)
        flat = lambda a: a.reshape(1, bsz * n, a.shape[-1])
        xp, xs = _out_mlp_call((flat(xp), flat(pp[0]), flat(pp[1]), 0), (xs, ps[0], ps[1], 1),
                               mod, l, wts["norm2_g"][l][None, :], w_out, out_idx,
                               wts["w_mlp1"], wts["w_mlp2"])
        xp = xp.reshape(bsz, n, D_MODEL)
    y_prompt, y_sample = xp, xs
    new_c_k = jnp.stack([nc[0].transpose(0, 4, 1, 2, 3) for nc in new_cache], axis=1)
    new_c_v = jnp.stack([nc[1].reshape(bsz, n, C_HEADS, 2 * HEAD_DIM) for nc in new_cache], axis=1)
    new_d_k = jnp.stack([nc[2].transpose(0, 3, 1, 2) for nc in new_cache], axis=1)
    new_d_v = jnp.stack([nc[3].transpose(0, 3, 1, 2) for nc in new_cache], axis=1)
    return (y_prompt, y_sample, new_c_k, new_c_v, new_d_k, new_d_v)
```

```python
import functools
import math

import numpy as np
import jax
import jax.numpy as jnp
from jax import lax
from jax.experimental import pallas as pl
from jax.experimental.pallas import tpu as pltpu

F32 = jnp.float32
BF16 = jnp.bfloat16

D_MODEL = 1024
HEAD_DIM = 64
LANES = 128
SUBLANES = 8
GRID_W = 64
ROPE_FREQS = HEAD_DIM // 4
ROPE_BASE = 10000.0
EPS = 1e-6
NEG_INF = -1e30
WINDOW = 128
A_WIDTH = D_MODEL // 2
B_WIDTH = D_MODEL // 2
POOL_WINDOWS = (2, 4, 8, 16)
POOL_HALO = 8
N_GROUPS = 4
GROUP_W = A_WIDTH // N_GROUPS
C_HEADS = 4
D_HEADS = 8
D_KV_HEADS = 2
D_GROUP = D_HEADS // D_KV_HEADS
C_W = C_HEADS * 2 * HEAD_DIM
DQ_W = D_HEADS * HEAD_DIM
DKV_W = D_KV_HEADS * HEAD_DIM
DKV_DUP_W = 2 * DKV_W
D_FF = 4 * D_MODEL
ATTN_SCALE = HEAD_DIM ** -0.5
LOG2E = math.log2(math.e)
V_ROWS = LANES + 16
ODD_NORMED_W = C_W + C_W + DQ_W + DKV_DUP_W
ODD_W = ODD_NORMED_W + C_W + DKV_DUP_W
VMEM_LIMIT = 56 * 1024 * 1024


def _lam_init(layer):
    return 0.8 - 0.6 * math.exp(-0.3 * layer)


def _params(n_axes):
    return pltpu.CompilerParams(dimension_semantics=("arbitrary",) * n_axes,
                                vmem_limit_bytes=VMEM_LIMIT)


def _const_spec(shape):
    nd = len(shape)
    return pl.BlockSpec(shape, lambda *_: (0,) * nd, pipeline_mode=pl.Buffered(1))


def _stacked_spec(shape, idx):
    nd = len(shape)
    return pl.BlockSpec((None,) + tuple(shape), lambda *_: (idx,) + (0,) * nd, pipeline_mode=pl.Buffered(1))


def _side_cast_specs(stacks, layer, n_steps, step_of):
    specs, shapes = [], []
    for w in stacks:
        axis = 2 if w.shape[2] >= w.shape[1] else 1
        nb = 1
        while nb * 2 <= n_steps and w.shape[axis] % (nb * 2 * 2 * LANES) == 0:
            nb *= 2
        block = [None, w.shape[1], w.shape[2]]
        block[axis] //= nb

        def index(*grid, axis=axis, nb=nb):
            blk = jnp.minimum(step_of(*grid), nb - 1)
            return (layer, 0, blk) if axis == 2 else (layer, blk, 0)

        specs.append((pl.BlockSpec(tuple(block), index),
                      pl.BlockSpec(tuple(block[1:]), lambda *grid, index=index: index(*grid)[1:])))
        shapes.append(jax.ShapeDtypeStruct(w.shape[1:], BF16))
    return specs, shapes


def _mod_spec(layer, which, row_of_batch):
    return pl.BlockSpec((None, None, None, 1, D_MODEL),
                        lambda b, i: (layer, row_of_batch(b), which, 0, 0))


def _rms_mod(x, g, sc, sh):
    ms = jnp.mean(x * x, axis=-1, keepdims=True)
    return (x * lax.rsqrt(ms + EPS) * g) * (1.0 + sc) + sh


def _bf16_const(a):
    return jnp.asarray(a, F32).astype(BF16)


def _dot_nt(a, b):
    return lax.dot_general(a, b, (((1,), (1,)), ((), ())), preferred_element_type=F32)


def _dot(a, b):
    return jnp.dot(a, b, preferred_element_type=F32)


def _adaln_kernel(cond_ref, w_ref, b_ref, o_ref):
    cnd = cond_ref[...]
    s = (cnd * jax.nn.sigmoid(cnd)).astype(BF16)
    o_ref[...] = _dot(s, w_ref[...].astype(BF16)) + b_ref[...]


def _adaln_call(cond, w_ada, b_ada):
    depth = w_ada.shape[0]
    rows = cond.shape[0]
    tn = 3 * D_MODEL
    return pl.pallas_call(
        _adaln_kernel,
        grid=(depth, 6 * D_MODEL // tn),
        in_specs=[
            pl.BlockSpec((rows, D_MODEL), lambda l, j: (0, 0)),
            pl.BlockSpec((None, D_MODEL, tn), lambda l, j: (l, 0, j)),
            pl.BlockSpec((None, 1, tn), lambda l, j: (l, 0, j)),
        ],
        out_specs=pl.BlockSpec((None, rows, tn), lambda l, j: (l, 0, j)),
        out_shape=jax.ShapeDtypeStruct((depth, rows, 6 * D_MODEL), F32),
        compiler_params=_params(2),
        name="adaln",
    )(cond, w_ada, b_ada.reshape(depth, 1, 6 * D_MODEL))


def _even_in_kernel(*refs, tm, n, n_cast):
    (x_ref, xp_ref, xn_ref, sh_ref, sc_ref, g_ref, win_ref, wpool_ref,
     pscale_ref, dftc_ref, wfft_ref) = refs[:11]
    a_ref, z_ref = refs[11 + n_cast:13 + n_cast]
    for w_ref, o_ref in zip(refs[11:11 + n_cast], refs[13 + n_cast:]):
        o_ref[...] = w_ref[...].astype(BF16)
    i = pl.program_id(1)
    n_tiles = n // tm
    g, sc, sh = g_ref[...], sc_ref[...], sh_ref[...]
    seqs = x_ref.shape[0]
    h = _rms_mod(x_ref[...].reshape(seqs * tm, D_MODEL), g, sc, sh).astype(BF16)
    w_in = win_ref[...].astype(BF16)
    u = _dot(h, w_in)
    xh = jnp.concatenate([xp_ref[...], xn_ref[...]], axis=0)
    hh = _rms_mod(xh, g, sc, sh).astype(BF16)
    uh = _dot(hh, w_in[:, :A_WIDTH])
    has_prev = (i > 0).astype(F32)
    has_next = (i < n_tiles - 1).astype(F32)
    rows = tm + 2 * POOL_HALO

    for gi in range(N_GROUPS):
        ub = u[:, A_WIDTH + gi * GROUP_W:A_WIDTH + (gi + 1) * GROUP_W].astype(BF16)
        cs = _dot(ub, dftc_ref[...])
        w_fft = wfft_ref[gi].astype(BF16)
        zr = _dot(cs[:, :GROUP_W].astype(BF16), w_fft)
        zi = _dot(cs[:, GROUP_W:].astype(BF16), w_fft)
        for q in range(seqs):
            z_ref[q, :, gi * GROUP_W:(gi + 1) * GROUP_W] = zr[q * tm:(q + 1) * tm].astype(BF16)
            z_ref[q, :, B_WIDTH + gi * GROUP_W:B_WIDTH + (gi + 1) * GROUP_W] = zi[q * tm:(q + 1) * tm].astype(BF16)

    t = i * tm + lax.broadcasted_iota(jnp.int32, (tm, 1), 0)
    for q, (gi, w) in ((q, gw) for q in range(seqs) for gw in enumerate(POOL_WINDOWS)):
        cols = slice(gi * GROUP_W, (gi + 1) * GROUP_W)
        uq = u[q * tm:(q + 1) * tm, cols]
        ext = jnp.concatenate([uh[0:POOL_HALO, cols] * has_prev, uq,
                               uh[POOL_HALO:, cols] * has_next], axis=0)
        s = ext
        span = 1
        while span < w:
            s = s + pltpu.roll(s, span, 0)
            span *= 2
        shift = w // 2 - 1
        s = pltpu.roll(s, rows - shift, 0) if shift else s
        acc = s[POOL_HALO:POOL_HALO + tm]
        lo = jnp.maximum(t - w // 2, 0)
        hi = jnp.minimum(t + w // 2, n)
        cnt = (hi - lo).astype(F32)
        pooled = acc / cnt - uq
        y = _dot(pooled.astype(BF16), wpool_ref[gi].astype(BF16)) * pscale_ref[:, cols]
        a_ref[q, :, cols] = y.astype(BF16)


def _even_in_call(x, mod, layer, row_of_batch, g, e, w_in, w_pool, pool_scale, dftc, w_fft, tm, seqs,
                  side_cast=None):
    bsz, n, _ = x.shape
    assert seqs == 1 or tm == n
    hb = tm // POOL_HALO
    tiles = n // tm
    stacks, cast_layer = side_cast if side_cast else ((), 0)
    cast_specs, cast_shapes = _side_cast_specs(stacks, cast_layer, (bsz // seqs) * tiles,
                                               lambda b, i: b * tiles + i)
    kern = functools.partial(_even_in_kernel, tm=tm, n=n, n_cast=len(stacks))
    return pl.pallas_call(
        kern,
        grid=(bsz // seqs, n // tm),
        in_specs=[
            pl.BlockSpec((seqs, tm, D_MODEL), lambda b, i: (b, i, 0)),
            pl.BlockSpec((None, POOL_HALO, D_MODEL),
                         lambda b, i: (b * seqs, jnp.maximum(i * hb - 1, 0), 0)),
            pl.BlockSpec((None, POOL_HALO, D_MODEL),
                         lambda b, i: (b * seqs, jnp.minimum((i + 1) * hb, n // POOL_HALO - 1), 0)),
            _mod_spec(layer, 0, row_of_batch),
            _mod_spec(layer, 1, row_of_batch),
            _const_spec((1, D_MODEL)),
            _stacked_spec((D_MODEL, D_MODEL), e),
            _stacked_spec((N_GROUPS, GROUP_W, GROUP_W), e),
            _const_spec((1, A_WIDTH)),
            _const_spec((GROUP_W, 2 * GROUP_W)),
            _stacked_spec((N_GROUPS, GROUP_W, GROUP_W), e),
        ] + [i_spec for i_spec, _ in cast_specs],
        out_specs=[
            pl.BlockSpec((seqs, tm, A_WIDTH), lambda b, i: (b, i, 0)),
            pl.BlockSpec((seqs, tm, 2 * B_WIDTH), lambda b, i: (b, i, 0)),
        ] + [o_spec for _, o_spec in cast_specs],
        out_shape=[
            jax.ShapeDtypeStruct((bsz, n, A_WIDTH), BF16),
            jax.ShapeDtypeStruct((bsz, n, 2 * B_WIDTH), BF16),
        ] + cast_shapes,
        compiler_params=_params(2),
        name="even_in",
    )(x, x, x, mod, mod, g, w_in, w_pool, pool_scale, dftc, w_fft, *stacks)


def _dft_direct_kernel(z_ref, m_ref, y_ref, *, scale):
    for q in range(z_ref.shape[0]):
        z = z_ref[q]
        rhs = jnp.concatenate([z[:, :B_WIDTH], z[:, B_WIDTH:]], axis=0)
        y_ref[q] = (_dot(m_ref[...], rhs) * scale).astype(BF16)


def _dft_direct_call(z):
    bsz, n, _ = z.shape
    k = np.arange(n)
    ang = 2.0 * np.pi * np.outer(k, k) / n
    m = _bf16_const(np.concatenate([np.cos(ang), np.sin(ang)], axis=1))
    kern = functools.partial(_dft_direct_kernel, scale=float((n * GROUP_W) ** -0.5))
    seqs = next(q for q in (4, 2, 1) if bsz % q == 0)
    return pl.pallas_call(
        kern,
        grid=(bsz // seqs,),
        in_specs=[pl.BlockSpec((seqs, n, 2 * B_WIDTH), lambda b: (b, 0, 0)),
                  _const_spec((n, 2 * n))],
        out_specs=pl.BlockSpec((seqs, n, B_WIDTH), lambda b: (b, 0, 0)),
        out_shape=jax.ShapeDtypeStruct((bsz, n, B_WIDTH), BF16),
        compiler_params=_params(1),
        name="dft_direct",
    )(z, m)


DFT_ROWS = 16
DFT_DIRECT_MAX = 256
DFT_PITCH = 24


def _pitch_rows(x):
    g, _, w = x.shape
    pad = jnp.zeros((g, DFT_PITCH - DFT_ROWS, w), x.dtype)
    return jnp.concatenate([x, pad], axis=1).reshape(g * DFT_PITCH, w)


def _dft_stage1_kernel(z_ref, m_ref, twr_ref, twi_ref, o_ref, zs_ref, os_ref, *, na):
    nt = 2 * B_WIDTH // LANES

    @pl.when((pl.program_id(0) == 0) & (pl.program_id(1) == 0))
    def _():
        os_ref[...] = jnp.zeros_like(os_ref)

    zf = _pitch_rows(z_ref[...].astype(F32))
    for k in range(nt):
        zs_ref[k] = zf[:, k * LANES:(k + 1) * LANES]
    for bi in range(DFT_ROWS):
        rows = pl.ds(bi, na, stride=DFT_PITCH)
        zb = [zs_ref[k, rows, :].astype(BF16) for k in range(nt)]
        rhs = jnp.concatenate([jnp.concatenate(zb[:nt // 2], axis=1),
                               jnp.concatenate(zb[nt // 2:], axis=1)], axis=0)
        r = _dot(m_ref[...], rhs)
        br, bim = r[:na], r[na:]
        tr, ti = twr_ref[bi], twi_ref[bi]
        out = (br * tr - bim * ti, br * ti + bim * tr)
        for k in range(nt):
            half, kk = divmod(k, nt // 2)
            os_ref[k, rows, :] = out[half][:, kk * LANES:(kk + 1) * LANES]
    of = jnp.concatenate([os_ref[k] for k in range(nt)], axis=1)
    o_ref[...] = of.reshape(na, DFT_PITCH, 2 * B_WIDTH)[:, :DFT_ROWS].astype(BF16)


def _dft_stage2_kernel(b_ref, m_ref, y_ref, ys_ref, *, scale):
    nb = b_ref.shape[1]

    @pl.when((pl.program_id(0) == 0) & (pl.program_id(1) == 0))
    def _():
        ys_ref[...] = jnp.zeros_like(ys_ref)

    for di in range(DFT_ROWS):
        bb = b_ref[di]
        rhs = jnp.concatenate([bb[:, :B_WIDTH], bb[:, B_WIDTH:]], axis=0)
        y = _dot(m_ref[...], rhs) * scale
        for k in range(B_WIDTH // LANES):
            ys_ref[k, pl.ds(di, nb, stride=DFT_PITCH), :] = y[:, k * LANES:(k + 1) * LANES]
    yf = jnp.concatenate([ys_ref[k] for k in range(B_WIDTH // LANES)], axis=1)
    y_ref[...] = yf.reshape(nb, DFT_PITCH, B_WIDTH)[:, :DFT_ROWS].astype(BF16)


def _dft_two_stage_call(z):
    bsz, n, _ = z.shape
    na = 1 << (int(math.log2(n)) // 2)
    nb = n // na
    ia = np.arange(na)
    ib = np.arange(nb)
    ang_a = 2.0 * np.pi * np.outer(ia, ia) / na
    fr, fi = np.cos(ang_a), -np.sin(ang_a)
    m1 = _bf16_const(np.block([[fr, -fi], [fi, fr]]))
    ang_t = 2.0 * np.pi * np.outer(ib, ia) / n
    twr = jnp.asarray(np.cos(ang_t)[:, :, None], F32)
    twi = jnp.asarray(-np.sin(ang_t)[:, :, None], F32)
    ang_b = 2.0 * np.pi * np.outer(ib, ib) / nb
    m2 = _bf16_const(np.concatenate([np.cos(ang_b), np.sin(ang_b)], axis=1))

    s1 = pl.pallas_call(
        functools.partial(_dft_stage1_kernel, na=na),
        grid=(bsz, nb // DFT_ROWS),
        in_specs=[
            pl.BlockSpec((None, na, DFT_ROWS, 2 * B_WIDTH), lambda b, j: (b, 0, j, 0)),
            _const_spec((2 * na, 2 * na)),
            pl.BlockSpec((DFT_ROWS, na, 1), lambda b, j: (j, 0, 0)),
            pl.BlockSpec((DFT_ROWS, na, 1), lambda b, j: (j, 0, 0)),
        ],
        out_specs=pl.BlockSpec((None, na, DFT_ROWS, 2 * B_WIDTH), lambda b, j: (b, 0, j, 0)),
        out_shape=jax.ShapeDtypeStruct((bsz, na, nb, 2 * B_WIDTH), BF16),
        scratch_shapes=[pltpu.VMEM((2 * B_WIDTH // LANES, na * DFT_PITCH, LANES), F32)] * 2,
        compiler_params=_params(2),
        name="dft_stage1",
    )(z.reshape(bsz, na, nb, 2 * B_WIDTH), m1, twr, twi)
    y = pl.pallas_call(
        functools.partial(_dft_stage2_kernel, scale=float((n * GROUP_W) ** -0.5)),
        grid=(bsz, na // DFT_ROWS),
        in_specs=[
            pl.BlockSpec((None, DFT_ROWS, nb, 2 * B_WIDTH), lambda b, j: (b, j, 0, 0)),
            _const_spec((nb, 2 * nb)),
        ],
        out_specs=pl.BlockSpec((None, nb, DFT_ROWS, B_WIDTH), lambda b, j: (b, 0, j, 0)),
        out_shape=jax.ShapeDtypeStruct((bsz, nb, na, B_WIDTH), BF16),
        scratch_shapes=[pltpu.VMEM((B_WIDTH // LANES, nb * DFT_PITCH, LANES), F32)],
        compiler_params=_params(2),
        name="dft_stage2",
    )(s1, m2)
    return y.reshape(bsz, n, B_WIDTH)


FF_CHUNK = 1024


def _out_mlp_kernel(*refs, steps_a, n_cast):
    (xa_ref, p1a_ref, p2a_ref, xb_ref, p1b_ref, p2b_ref, g1_ref, sh_ref, sc_ref, g2_ref,
     ng_ref, wout_ref, w1_ref, w2_ref) = refs[:14]
    oa_ref, ob_ref = refs[14 + n_cast:16 + n_cast]
    for w_ref, o_ref in zip(refs[14:14 + n_cast], refs[16 + n_cast:]):
        o_ref[...] = w_ref[...].astype(BF16)

    def body(x_ref, p1_ref, p2_ref, o_ref):
        half = p1_ref.shape[-1]
        mix = (_dot(p1_ref[...], wout_ref[:half, :].astype(BF16))
               + _dot(p2_ref[...], wout_ref[half:, :].astype(BF16)))
        x1 = x_ref[...] + g1_ref[...] * mix
        h = _rms_mod(x1, ng_ref[...], sc_ref[...], sh_ref[...]).astype(BF16)
        acc = None
        for c in range(D_FF // FF_CHUNK):
            a = _dot(h, w1_ref[:, c * FF_CHUNK:(c + 1) * FF_CHUNK])
            a = jnp.square(jnp.maximum(a, 0.0)).astype(BF16)
            part = _dot(a, w2_ref[c * FF_CHUNK:(c + 1) * FF_CHUNK, :])
            acc = part if acc is None else acc + part
        o_ref[...] = x1 + g2_ref[...] * acc

    @pl.when(pl.program_id(0) < steps_a)
    def _():
        body(xa_ref, p1a_ref, p2a_ref, oa_ref)

    @pl.when(pl.program_id(0) >= steps_a)
    def _():
        body(xb_ref, p1b_ref, p2b_ref, ob_ref)


def _out_mlp_call(set_a, set_b, mod, layer, ng, w_out, out_idx, w1, w2, side_cast=None):
    (xa, p1a, p2a, row_a), (xb, p1b, p2b, row_b) = set_a, set_b
    half = p1a.shape[-1]
    tm = min(xa.shape[1], xb.shape[1], MLP_TILE)
    tiles_a, tiles_b = xa.shape[1] // tm, xb.shape[1] // tm
    steps_a, steps_b = xa.shape[0] * tiles_a, xb.shape[0] * tiles_b

    def pos_a(s):
        sa = jnp.minimum(s, steps_a - 1)
        return sa // tiles_a, sa % tiles_a

    def pos_b(s):
        sb = jnp.maximum(s - steps_a, 0)
        return sb // tiles_b, sb % tiles_b

    tok_a = lambda w: pl.BlockSpec((None, tm, w), lambda s: pos_a(s) + (0,))
    tok_b = lambda w: pl.BlockSpec((None, tm, w), lambda s: pos_b(s) + (0,))
    row = lambda s: jnp.where(s < steps_a, row_a + pos_a(s)[0], row_b + pos_b(s)[0])
    mod_spec = lambda which: pl.BlockSpec((None, None, None, 1, D_MODEL),
                                          lambda s: (layer, row(s), which, 0, 0))
    stacks, cast_layer = side_cast if side_cast else ((), 0)
    cast_specs, cast_shapes = _side_cast_specs(stacks, cast_layer, steps_a + steps_b, lambda s: s)
    return pl.pallas_call(
        functools.partial(_out_mlp_kernel, steps_a=steps_a, n_cast=len(stacks)),
        grid=(steps_a + steps_b,),
        in_specs=[
            tok_a(D_MODEL), tok_a(half), tok_a(half), tok_b(D_MODEL), tok_b(half), tok_b(half),
            mod_spec(2), mod_spec(3), mod_spec(4), mod_spec(5),
            _const_spec((1, D_MODEL)),
            _stacked_spec((2 * half, D_MODEL), out_idx),
            _const_spec((D_MODEL, D_FF)),
            _const_spec((D_FF, D_MODEL)),
        ] + [i_spec for i_spec, _ in cast_specs],
        out_specs=[tok_a(D_MODEL), tok_b(D_MODEL)] + [o_spec for _, o_spec in cast_specs],
        out_shape=[jax.ShapeDtypeStruct(xa.shape, F32), jax.ShapeDtypeStruct(xb.shape, F32)] + cast_shapes,
        compiler_params=_params(1),
        name="out_mlp",
    )(xa, p1a, p2a, xb, p1b, p2b, mod, mod, mod, mod, ng, w_out, w1, w2, *stacks)


def _odd_in_kernel(*refs, rope, emit_cache, tm):
    x_ref, sh_ref, sc_ref, g_ref, win_ref, gain_ref, hm_ref = refs[:7]
    k = 7
    if rope:
        cos_ref, sin_ref = refs[k:k + 2]
        k += 2
    cq_ref, ck_ref, cvt_ref, dq_ref, dk_ref, dvt_ref = refs[k:k + 6]
    k += 6
    if emit_cache:
        nck_ref, ncv_ref, ndk_ref, ndv_ref = refs[k:k + 4]
    subs = x_ref.shape[0] // tm
    lane = lax.broadcasted_iota(jnp.int32, (tm, LANES), 1)
    first_half = (lane % (2 * ROPE_FREQS)) < ROPE_FREQS
    pad_row = lax.broadcasted_iota(jnp.int32, (V_ROWS - LANES, tm), 0)
    pad = jnp.where(pad_row == 0, 1.0, 0.0).astype(BF16)

    def cache_rows(ref, sub, r):
        rpt = ref.shape[0] // x_ref.shape[0]
        return ref.at[pl.ds(sub * tm * rpt + r, tm, stride=rpt), :]

    def cache_cols(ref, sub, lead, mat):
        n_req = ref.shape[-1]
        for r in range(tm // n_req):
            ref[(sub * (tm // n_req) + r,) + lead] = mat[:, r * n_req:(r + 1) * n_req]

    groups = ((0, 2 * C_W), (2 * C_W, ODD_NORMED_W), (ODD_NORMED_W, ODD_W))
    hidden = {}

    def project(item):
        sub, grp = item
        if sub not in hidden:
            x = x_ref[sub * tm:(sub + 1) * tm, :]
            hidden[sub] = _rms_mod(x, g_ref[...], sc_ref[...], sh_ref[...]).astype(BF16)
        lo, hi = groups[grp]
        return _dot(hidden[sub], win_ref[:, lo:hi])

    def head_norms(p, lo):
        normed = []
        for c in range(p.shape[1] // (2 * LANES)):
            v = p[:, c * 2 * LANES:(c + 1) * 2 * LANES]
            ms = _dot((v * v).astype(BF16), hm_ref[...])
            y = v * lax.rsqrt(ms + EPS) * gain_ref[:, lo + c * 2 * LANES:lo + (c + 1) * 2 * LANES]
            normed += [y[:, :LANES], y[:, LANES:]]
        return normed

    def finish(item, p, normed):
        sub, grp = item
        rows = slice(sub * tm, (sub + 1) * tm)

        def roped(y):
            if not rope:
                return y
            rot = jnp.where(first_half, -pltpu.roll(y, LANES - ROPE_FREQS, 1),
                            pltpu.roll(y, ROPE_FREQS, 1))
            return y * cos_ref[rows, :] + rot * sin_ref[rows, :]

        def put(ref, chunks):
            for c, y in enumerate(chunks):
                ref[rows, c * LANES:(c + 1) * LANES] = roped(y).astype(BF16)

        nq = C_W // LANES
        if grp == 0:
            put(cq_ref, normed[:nq])
            put(ck_ref, normed[nq:])
            if emit_cache:
                for c, y in enumerate(normed[nq:]):
                    yt = y.T
                    for half in range(2):
                        cache_cols(nck_ref, sub, (c, half), yt[half * HEAD_DIM:(half + 1) * HEAD_DIM])
        elif grp == 1:
            put(dq_ref, normed[:nq])
            put(dk_ref, normed[nq:])
            if emit_cache:
                for j, y in enumerate(normed[nq:]):
                    cache_cols(ndk_ref, sub, (j,), y.T[:HEAD_DIM])
        else:
            cv, dv = p[:, :C_W], p[:, C_W:]
            for v, vt_ref in ((cv, cvt_ref), (dv, dvt_ref)):
                for hh in range(vt_ref.shape[0]):
                    vt = v[:, hh * LANES:(hh + 1) * LANES].T
                    vt_ref[hh, :LANES, rows] = vt.astype(BF16)
                    vt_ref[hh, LANES:, rows] = pad
                    if emit_cache and vt_ref is dvt_ref:
                        cache_cols(ndv_ref, sub, (hh,), vt[:HEAD_DIM])
            if emit_cache:
                for hh in range(C_HEADS):
                    cache_rows(ncv_ref, sub, hh)[...] = cv[:, hh * LANES:(hh + 1) * LANES]

    items = [(sub, grp) for sub in range(subs) for grp in range(len(groups))]
    p = project(items[0])
    for idx, item in enumerate(items):
        normed = head_norms(p, groups[item[1]][0]) if item[1] < 2 else None
        nxt = project(items[idx + 1]) if idx + 1 < len(items) else None
        finish(item, p, normed)
        p = nxt


def _odd_in_call(x, mod, layer, row_of_batch, g, w_in, gains, head_mean, rope_tabs, tm, cache_n=None):
    emit_cache = cache_n is not None
    bsz, n, _ = x.shape
    rope = rope_tabs is not None
    subs = 2 if n % (2 * tm) == 0 else 1
    ts = subs * tm
    tok = lambda w: pl.BlockSpec((None, ts, w), lambda b, i: (b, i, 0))
    in_specs = [
        tok(D_MODEL),
        _mod_spec(layer, 0, row_of_batch),
        _mod_spec(layer, 1, row_of_batch),
        _const_spec((1, D_MODEL)),
        _const_spec((D_MODEL, ODD_W)),
        _const_spec((1, ODD_NORMED_W)),
        _const_spec((2 * LANES, 2 * LANES)),
    ]
    args = [x, mod, mod, g, w_in, gains, head_mean]
    if rope:
        in_specs += [pl.BlockSpec((ts, LANES), lambda b, i: (i, 0))] * 2
        args += list(rope_tabs)
    vt_spec = lambda heads: pl.BlockSpec((None, heads, V_ROWS, ts), lambda b, i: (b, 0, 0, i))
    vt_shape = lambda heads: jax.ShapeDtypeStruct((bsz, heads, V_ROWS, n), BF16)
    tok_shape = lambda w: jax.ShapeDtypeStruct((bsz, n, w), BF16)
    out_specs = [tok(C_W), tok(C_W), vt_spec(C_HEADS), tok(DQ_W), tok(DKV_DUP_W), vt_spec(D_KV_HEADS)]
    out_shape = [tok_shape(C_W), tok_shape(C_W), vt_shape(C_HEADS), tok_shape(DQ_W),
                 tok_shape(DKV_DUP_W), vt_shape(D_KV_HEADS)]
    if emit_cache:
        assert bsz == 1 and tm % cache_n == 0 and n % cache_n == 0
        reqs, rps = n // cache_n, ts // cache_n
        for lead in ((C_HEADS, 2), None, (D_KV_HEADS,), (D_KV_HEADS,)):
            if lead is None:
                out_specs.append(pl.BlockSpec((None, ts * C_HEADS, LANES), lambda b, i: (b, i, 0)))
                out_shape.append(jax.ShapeDtypeStruct((bsz, n * C_HEADS, LANES), F32))
            else:
                zeros = (0,) * (len(lead) + 2)
                out_specs.append(pl.BlockSpec((rps,) + lead + (HEAD_DIM, cache_n),
                                              lambda b, i, zeros=zeros: (i,) + zeros))
                out_shape.append(jax.ShapeDtypeStruct((reqs,) + lead + (HEAD_DIM, cache_n), F32))
    return pl.pallas_call(
        functools.partial(_odd_in_kernel, rope=rope, emit_cache=emit_cache, tm=tm),
        grid=(bsz, n // ts),
        in_specs=in_specs,
        out_specs=out_specs,
        out_shape=out_shape,
        compiler_params=_params(2),
        name="odd_in",
    )(*args)


QK_AHEAD = 3


def _diff_attn_kernel(*refs, lam_init, tq, kc):
    n_src = (len(refs) - 4) // 2
    q_ref, (lamv_ref, g_ref, o_ref) = refs[0], refs[-3:]
    k_refs, vt_refs = refs[1:1 + n_src], refs[1 + n_src:1 + 2 * n_src]
    chunks = [(src, off) for src in range(n_src) for off in range(0, k_refs[src].shape[1], kc)]
    n_chunks = len(chunks)
    reqs = q_ref.shape[0]
    tiles = q_ref.shape[1] // tq
    heads = q_ref.shape[2] // LANES
    lane = lax.broadcasted_iota(jnp.int32, (tq, LANES), 1)
    lv = lamv_ref[...]
    lam = (jnp.exp(jnp.sum(lv[0:1] * lv[1:2], keepdims=True))
           - jnp.exp(jnp.sum(lv[2:3] * lv[3:4], keepdims=True)) + lam_init)

    qqs = {}

    def score(item):
        r, h, t, c = item
        cols = slice(h * LANES, (h + 1) * LANES)
        if (r, h, t) not in qqs:
            q = q_ref[r, t * tq:(t + 1) * tq, cols]
            zero = jnp.zeros_like(q)
            qqs[r, h, t] = jnp.concatenate([jnp.where(lane < HEAD_DIM, q, zero),
                                            jnp.where(lane >= HEAD_DIM, q, zero)], axis=0)
        src, off = chunks[c]
        return _dot_nt(k_refs[src][r, off:off + kc, cols], qqs[r, h, t])

    def finish(r, h, t, acc, l):
        inv = 1.0 / l
        ot = acc[:, :tq] * inv[:, :tq] - acc[:, tq:] * (inv[:, tq:] * lam)
        o = ot.T
        ms = jnp.mean(o * o, axis=-1, keepdims=True)
        y = o * lax.rsqrt(ms + EPS) * g_ref[...] * (1.0 - lam_init)
        o_ref[r, t * tq:(t + 1) * tq, h * LANES:(h + 1) * LANES] = y.astype(BF16)

    items = [(r, h, t, c) for r in range(reqs) for h in range(heads) for t in range(tiles)
             for c in range(n_chunks)]
    ahead = [score(it) for it in items[:QK_AHEAD]]
    m = acc = None
    for idx, (r, h, t, c) in enumerate(items):
        s = ahead.pop(0)
        if idx + QK_AHEAD < len(items):
            ahead.append(score(items[idx + QK_AHEAD]))
        cm = jnp.max(s, axis=0, keepdims=True)
        m_new = cm if c == 0 else jnp.maximum(m, cm)
        src, off = chunks[c]
        e = jnp.exp2(s - m_new)
        lsum = jnp.sum(e, axis=0, keepdims=True)
        lane0 = r * k_refs[src].shape[1] + off
        pv = _dot(vt_refs[src][h, :LANES, lane0:lane0 + kc], e.astype(BF16))
        if c == 0:
            acc, l = pv, lsum
        else:
            alpha = jnp.exp2(m - m_new)
            acc, l = acc * alpha + pv, l * alpha + lsum
        m = m_new
        if c == n_chunks - 1:
            finish(r, h, t, acc, l)


def _diff_attn_call(q, ks, vts, lamv, subln_g, lam_init, tq, tiles, heads, reqs=1, flat_values=False):
    bsz, n, _ = q.shape
    assert reqs == 1 or flat_values
    kc = next(c for c in (512, 256, 128) if all(k.shape[1] % c == 0 for k in ks))
    hw = heads * LANES
    k_specs = [pl.BlockSpec((reqs, k.shape[1], hw), lambda b, h, i: (b, 0, h)) for k in ks]
    vt_index = (lambda b, h, i: (0, h, 0, b)) if flat_values else (lambda b, h, i: (b, h, 0, 0))
    vt_specs = [pl.BlockSpec((None, heads, V_ROWS, reqs * k.shape[1]), vt_index) for k in ks]
    return pl.pallas_call(
        functools.partial(_diff_attn_kernel, lam_init=lam_init, tq=tq, kc=kc),
        grid=(bsz // reqs, C_HEADS // heads, n // (tq * tiles)),
        in_specs=[pl.BlockSpec((reqs, tiles * tq, hw), lambda b, h, i: (b, i, h))]
        + k_specs + vt_specs + [_const_spec((4, HEAD_DIM)), _const_spec((1, LANES))],
        out_specs=pl.BlockSpec((reqs, tiles * tq, hw), lambda b, h, i: (b, i, h)),
        out_shape=jax.ShapeDtypeStruct((bsz, n, C_W), BF16),
        compiler_params=_params(3),
        name="diff_attn",
    )(q, *ks, *vts, lamv, subln_g)


def _sink_attn_kernel(*refs, windowed, tq, nblk):
    if windowed:
        (q_ref, ck_ref, kp_ref, km_ref, kn_ref, cvt_ref, vp_ref, vm_ref, vn_ref,
         bias_ref, sink_ref, o_ref) = refs
        kband = jnp.concatenate([kp_ref[...], km_ref[...], kn_ref[...]], axis=0)
        vtband = jnp.concatenate([vp_ref[...], vm_ref[...], vn_ref[...]], axis=1)
    else:
        q_ref, k_ref, vt_ref, sink_ref, o_ref = refs
    reqs = q_ref.shape[0]
    subs = q_ref.shape[1] // tq
    groups = q_ref.shape[2] // (2 * LANES)
    lane = lax.broadcasted_iota(jnp.int32, (tq, LANES), 1)
    lo = lane < HEAD_DIM

    def scores_of(stream):
        r, g, sub = stream
        qd = q_ref[r, sub * tq:(sub + 1) * tq, g * 2 * LANES:(g + 1) * 2 * LANES]
        parts = []
        for c in range(2):
            ch = qd[:, c * LANES:(c + 1) * LANES]
            zero = jnp.zeros_like(ch)
            parts += [jnp.where(lo, ch, zero), jnp.where(lo, zero, ch)]
        qq = jnp.concatenate(parts, axis=0)
        if not windowed:
            return [_dot_nt(k_ref[r, :, g * LANES:(g + 1) * LANES], qq)]
        blk = pl.program_id(2) * subs + sub
        variant = jnp.where(blk == 0, 0, jnp.where(blk == nblk - 1, 2, 1))
        band = _dot_nt(kband[sub * tq:(sub + 3) * tq, :], qq) + bias_ref[variant]
        return [_dot_nt(ck_ref[...], qq), band]

    def finish(stream, scores):
        r, g, sub = stream
        values = ([cvt_ref[...], vtband[:, sub * tq:(sub + 3) * tq]] if windowed
                  else [vt_ref[g, :, r * tq:(r + 1) * tq]])
        sink = sink_ref[g]
        m = sink
        for s in scores:
            m = jnp.maximum(m, jnp.max(s, axis=0, keepdims=True))
        acc = None
        for s, vt in zip(scores, values):
            pv = _dot(vt, jnp.exp2(s - m).astype(BF16))
            acc = pv if acc is None else acc + pv
        inv = 1.0 / (acc[LANES:LANES + 1, :] + jnp.exp2(sink - m))
        ot = acc[:LANES, :] * inv
        for c in range(2):
            o0 = ot[:, (2 * c) * tq:(2 * c + 1) * tq].T
            o1 = ot[:, (2 * c + 1) * tq:(2 * c + 2) * tq].T
            cols = slice((2 * g + c) * LANES, (2 * g + c + 1) * LANES)
            o_ref[r, sub * tq:(sub + 1) * tq, cols] = jnp.where(lo, o0, o1).astype(BF16)

    streams = [(r, g, sub) for r in range(reqs) for g in range(groups) for sub in range(subs)]
    ahead = [scores_of(st) for st in streams[:QK_AHEAD]]
    for idx, st in enumerate(streams):
        cur = ahead.pop(0)
        if idx + QK_AHEAD < len(streams):
            ahead.append(scores_of(streams[idx + QK_AHEAD]))
        finish(st, cur)


def _window_bias(n, tq):
    nblk = n // tq
    kk = np.arange(3 * tq)[:, None]
    qi = np.arange(tq)[None, :]
    band_ok = np.abs(qi + WINDOW - kk) <= WINDOW
    out = []
    for blk in (0, 1, nblk - 1):
        jpos = blk * tq - WINDOW + kk
        ok = band_ok & (jpos >= 0) & (jpos < n)
        out.append(np.tile(np.where(ok, 0.0, NEG_INF), (1, D_GROUP)))
    return jnp.asarray(np.stack(out), F32)


def _sink_attn_call(q, k, vt, sink_rows, ctx_k=None, ctx_vt=None, reqs=1, flat_values=False):
    bsz, n, _ = q.shape
    windowed = ctx_k is not None
    assert reqs == 1 or (flat_values and not windowed)
    tq = WINDOW if windowed else n
    nblk = n // tq
    subs = next(s for s in (16, 8, 4, 2, 1) if nblk % s == 0) if windowed else 1
    steps = nblk // subs
    groups = 1 if windowed else D_KV_HEADS
    q_spec = pl.BlockSpec((reqs, subs * tq, groups * 2 * LANES), lambda b, j, i: (b, i, j))
    sink_spec = pl.BlockSpec((groups, 1, D_GROUP * tq), lambda b, j, i: (j, 0, 0))
    if windowed:
        assert nblk >= 3, "first / interior / last mask variants need three query blocks"
        n_ctx = ctx_k.shape[1]
        prev = lambda i: jnp.maximum(i * subs - 1, 0)
        nxt = lambda i: jnp.minimum((i + 1) * subs, nblk - 1)
        in_specs = [
            q_spec,
            pl.BlockSpec((None, n_ctx, LANES), lambda b, j, i: (b, 0, j)),
            pl.BlockSpec((None, tq, LANES), lambda b, j, i: (b, prev(i), j)),
            pl.BlockSpec((None, subs * tq, LANES), lambda b, j, i: (b, i, j)),
            pl.BlockSpec((None, tq, LANES), lambda b, j, i: (b, nxt(i), j)),
            pl.BlockSpec((None, None, V_ROWS, n_ctx), lambda b, j, i: (b, j, 0, 0)),
            pl.BlockSpec((None, None, V_ROWS, tq), lambda b, j, i: (b, j, 0, prev(i))),
            pl.BlockSpec((None, None, V_ROWS, subs * tq), lambda b, j, i: (b, j, 0, i)),
            pl.BlockSpec((None, None, V_ROWS, tq), lambda b, j, i: (b, j, 0, nxt(i))),
            _const_spec((3, 3 * tq, D_GROUP * tq)),
            sink_spec,
        ]
        args = (q, ctx_k, k, k, k, ctx_vt, vt, vt, vt, _window_bias(n, tq), sink_rows)
    else:
        in_specs = [
            q_spec,
            pl.BlockSpec((reqs, n, groups * LANES), lambda b, j, i: (b, 0, j)),
            pl.BlockSpec((None, groups, V_ROWS, reqs * n),
                         (lambda b, j, i: (0, j, 0, b)) if flat_values else (lambda b, j, i: (b, j, 0, 0))),
            sink_spec,
        ]
        args = (q, k, vt, sink_rows)
    return pl.pallas_call(
        functools.partial(_sink_attn_kernel, windowed=windowed, tq=tq, nblk=nblk),
        grid=(bsz // reqs, D_KV_HEADS // groups, steps),
        in_specs=in_specs,
        out_specs=q_spec,
        out_shape=jax.ShapeDtypeStruct((bsz, n, DQ_W), BF16),
        compiler_params=_params(3),
        name="sink_attn",
    )(*args)


def _dup_heads(a):
    lead = a.shape[:-1]
    a = a.reshape(lead + (D_KV_HEADS, 1, HEAD_DIM))
    return jnp.broadcast_to(a, lead + (D_KV_HEADS, 2, HEAD_DIM)).reshape(lead + (DKV_DUP_W,))


def _odd_weights(w_in_odd, c_qn_g, c_kn_g, d_qn_g, d_kn_g):
    cq, ck, cv, dq, dk, dv = jnp.split(
        w_in_odd, (C_W, 2 * C_W, 3 * C_W, 3 * C_W + DQ_W, 3 * C_W + DQ_W + DKV_W), axis=-1)
    w = jnp.concatenate([cq, ck, dq, _dup_heads(dk), cv, _dup_heads(dv)], axis=-1).astype(BF16)
    gains = jnp.concatenate([
        jnp.tile(c_qn_g * (ATTN_SCALE * LOG2E), C_W // HEAD_DIM),
        jnp.tile(c_kn_g, C_W // HEAD_DIM),
        jnp.tile(d_qn_g * (ATTN_SCALE * LOG2E), DQ_W // HEAD_DIM),
        jnp.tile(d_kn_g, DKV_DUP_W // HEAD_DIM),
    ])[None, :]
    return w, gains


def _value_rows(v):
    bsz, nk, w = v.shape
    vt = v.reshape(bsz, nk, w // LANES, LANES).transpose(0, 2, 3, 1)
    pad = jnp.zeros((bsz, w // LANES, V_ROWS - LANES, nk), v.dtype).at[:, :, 0, :].set(1)
    return jnp.concatenate([vt, pad], axis=2)


def _rope_tables(n):
    rows = n // GRID_W
    row = np.repeat(np.arange(rows), GRID_W).astype(np.float64)
    col = np.tile(np.arange(GRID_W), rows).astype(np.float64)
    freqs = ROPE_BASE ** (-np.arange(ROPE_FREQS, dtype=np.float64) / ROPE_FREQS)
    ang = np.stack([row[:, None] * freqs, col[:, None] * freqs], axis=1)
    ang = np.concatenate([ang, ang], axis=-1).reshape(n, HEAD_DIM)
    ang = np.concatenate([ang, ang], axis=-1)
    return jnp.asarray(np.cos(ang), F32), jnp.asarray(np.sin(ang), F32)


MLP_TILE = 512


def _token_tile(n):
    return min(n, 512)


def _mixer(x, l, mod, row_of_batch, wts, caches, new_cache, side_cast=None):
    bsz, n, _ = x.shape
    decode = caches is not None
    g1n = wts["norm1_g"][l][None, :]
    if l % 2 == 0:
        e = l // 2
        a, z, *copies = _even_in_call(x, mod, l, row_of_batch, g1n, e, wts["w_in_even"], wts["w_pool"],
                                      wts["pool_scale"][e][None, :], wts["dftc"], wts["w_fft"], min(n, 1024),
                                      1 if decode or n > 1024 else next(q for q in (4, 2, 1) if bsz % q == 0),
                                      side_cast)
        y = _dft_two_stage_call(z) if n > DFT_DIRECT_MAX else _dft_direct_call(z)
        return (a, y) + tuple(copies)
    o = l // 2
    lam_init = _lam_init(l)
    rope_tabs = _rope_tables(n) if decode else None
    xo = x if decode else x.reshape(1, bsz * n, D_MODEL)
    outs = _odd_in_call(xo, mod, l, row_of_batch, g1n, wts["w_in_odd"][o], wts["odd_gains"][o],
                        wts["head_mean"], rope_tabs, _token_tile(xo.shape[1]),
                        None if decode else n)
    if not decode:
        outs = [a if a.ndim == 4 else a.reshape(bsz, n, a.shape[-1]) for a in outs[:6]] + list(outs[6:])
    cq, ck, cvt, dq, dkk, dvt = outs[:6]
    if decode:
        c_k, c_v, d_k, d_v = caches
        lc = c_k.shape[2]
        ks = [c_k[:, o].reshape(bsz, lc, C_W).astype(BF16), ck]
        vts = [_value_rows(c_v[:, o].reshape(bsz, lc, C_W).astype(BF16)), cvt]
        ctx_k = _dup_heads(d_k[:, o].reshape(bsz, lc, DKV_W)).astype(BF16)
        ctx_vt = _value_rows(_dup_heads(d_v[:, o].reshape(bsz, lc, DKV_W)).astype(BF16))
        tq = min(n, 256)
    else:
        new_cache.append(outs[6:])
        ks, vts = [ck], [cvt]
        ctx_k = ctx_vt = None
        tq = n
    tiles = next(t for t in (4, 2, 1) if n % (t * tq) == 0)
    heads = 1 if decode else C_HEADS
    c_out = _diff_attn_call(cq, ks, vts, wts["lamv"][o],
                            wts["c_subln_g"][o][None, :], lam_init, tq, tiles, heads,
                            1 if decode else next(r for r in (4, 2, 1) if bsz % r == 0), not decode)
    sink_tq = WINDOW if decode else n
    sink_rows = jnp.repeat(wts["d_sink"][o].reshape(D_KV_HEADS, D_GROUP) * LOG2E, sink_tq,
                           axis=-1)[:, None, :]
    d_out = _sink_attn_call(dq, dkk, dvt, sink_rows, ctx_k, ctx_vt,
                            1 if decode else next(r for r in (4, 2, 1) if bsz % r == 0), not decode)
    return c_out, d_out


def kernel(x_prompt, x_sample, c, cache_c_k, cache_c_v, cache_d_k, cache_d_v, c_ctx,
           norm1_g, norm2_g, w_ada, b_ada, w_in_even, w_pool, pool_scale, w_fft, w_out_even,
           w_in_odd, c_qn_g, c_kn_g, lam_q1, lam_k1, lam_q2, lam_k2, c_subln_g,
           d_qn_g, d_kn_g, d_sink, w_out_odd, w_mlp1, w_mlp2):
    depth = norm1_g.shape[0]
    n_odd = w_in_odd.shape[0]
    dec_b = c.shape[0]

    rows = -(-(1 + dec_b) // SUBLANES) * SUBLANES
    cond = jnp.concatenate([c_ctx[None, :], c, jnp.zeros((rows - 1 - dec_b, D_MODEL), F32)], axis=0)
    mod = _adaln_call(cond, w_ada, b_ada).reshape(depth, rows, 6, 1, D_MODEL)

    ic = np.arange(GROUP_W)
    ang = 2.0 * np.pi * np.outer(ic, ic) / GROUP_W
    dftc = _bf16_const(np.concatenate([np.cos(ang), -np.sin(ang)], axis=1))
    head_mean = _bf16_const(np.kron(np.eye(2 * LANES // HEAD_DIM), np.ones((HEAD_DIM, HEAD_DIM))) / HEAD_DIM)

    odd = [_odd_weights(w_in_odd[o], c_qn_g[o], c_kn_g[o], d_qn_g[o], d_kn_g[o]) for o in range(n_odd)]
    wts = {
        "norm1_g": norm1_g, "norm2_g": norm2_g,
        "w_in_even": w_in_even, "w_pool": w_pool, "pool_scale": pool_scale,
        "w_fft": w_fft, "w_out_even": w_out_even, "dftc": dftc,
        "w_in_odd": [w for w, _ in odd], "odd_gains": [g for _, g in odd], "head_mean": head_mean,
        "lamv": jnp.stack([lam_q1, lam_k1, lam_q2, lam_k2], axis=1),
        "c_subln_g": c_subln_g, "d_sink": d_sink, "w_out_odd": w_out_odd,
    }

    bsz, n, _ = x_prompt.shape
    caches = (cache_c_k, cache_c_v, cache_d_k, cache_d_v)
    new_cache = []
    xp, xs = x_prompt, x_sample
    mlp_stacks = (w_mlp1, w_mlp2)
    mlp_w = None
    for l in range(depth):
        pp = _mixer(xp, l, mod, lambda b: 0, wts, None, new_cache)
        ps = _mixer(xs, l, mod, lambda b: 1 + b, wts, caches, new_cache,
                    (mlp_stacks, 0) if l == 0 else None)
        if l == 0:
            mlp_w = ps[2:]
        w_out, out_idx = (wts["w_out_even"], l // 2) if l % 2 == 0 else (wts["w_out_odd"], l // 2)
        flat = lambda a: a.reshape(1, bsz * n, a.shape[-1])
        xp, xs, *mlp_next = _out_mlp_call((flat(xp), flat(pp[0]), flat(pp[1]), 0), (xs, ps[0], ps[1], 1),
                                          mod, l, wts["norm2_g"][l][None, :], w_out, out_idx, *mlp_w,
                                          (mlp_stacks, l + 1) if l + 1 < depth else None)
        mlp_w = mlp_next
        xp = xp.reshape(bsz, n, D_MODEL)
    y_prompt, y_sample = xp, xs
    new_c_k = jnp.stack([nc[0].transpose(0, 4, 1, 2, 3) for nc in new_cache], axis=1)
    new_c_v = jnp.stack([nc[1].reshape(bsz, n, C_HEADS, 2 * HEAD_DIM) for nc in new_cache], axis=1)
    new_d_k = jnp.stack([nc[2].transpose(0, 3, 1, 2) for nc in new_cache], axis=1)
    new_d_v = jnp.stack([nc[3].transpose(0, 3, 1, 2) for nc in new_cache], axis=1)
    return (y_prompt, y_sample, new_c_k, new_c_v, new_d_k, new_d_v)
```

```python
import functools
import math

import numpy as np
import jax
import jax.numpy as jnp
from jax import lax
from jax.experimental import pallas as pl
from jax.experimental.pallas import tpu as pltpu

F32 = jnp.float32
BF16 = jnp.bfloat16

D_MODEL = 1024
HEAD_DIM = 64
LANES = 128
SUBLANES = 8
GRID_W = 64
ROPE_FREQS = HEAD_DIM // 4
ROPE_BASE = 10000.0
EPS = 1e-6
NEG_INF = -1e30
WINDOW = 128
A_WIDTH = D_MODEL // 2
B_WIDTH = D_MODEL // 2
POOL_WINDOWS = (2, 4, 8, 16)
POOL_HALO = 8
N_GROUPS = 4
GROUP_W = A_WIDTH // N_GROUPS
C_HEADS = 4
D_HEADS = 8
D_KV_HEADS = 2
D_GROUP = D_HEADS // D_KV_HEADS
C_W = C_HEADS * 2 * HEAD_DIM
DQ_W = D_HEADS * HEAD_DIM
DKV_W = D_KV_HEADS * HEAD_DIM
DKV_DUP_W = 2 * DKV_W
D_FF = 4 * D_MODEL
ATTN_SCALE = HEAD_DIM ** -0.5
LOG2E = math.log2(math.e)
V_ROWS = LANES + 16
ODD_CV0 = 2 * C_W
ODD_DQ0 = 3 * C_W
ODD_DK0 = ODD_DQ0 + DQ_W
ODD_DV0 = ODD_DK0 + DKV_W
ODD_W = ODD_DV0 + DKV_W
ODD_NORMED_W = 2 * C_W + DQ_W + DKV_W
VMEM_LIMIT = 56 * 1024 * 1024


def _lam_init(layer):
    return 0.8 - 0.6 * math.exp(-0.3 * layer)


def _params(n_axes):
    return pltpu.CompilerParams(dimension_semantics=("arbitrary",) * n_axes,
                                vmem_limit_bytes=VMEM_LIMIT)


def _const_spec(shape):
    nd = len(shape)
    return pl.BlockSpec(shape, lambda *_: (0,) * nd, pipeline_mode=pl.Buffered(1))


def _stacked_spec(shape, idx):
    nd = len(shape)
    return pl.BlockSpec((None,) + tuple(shape), lambda *_: (idx,) + (0,) * nd, pipeline_mode=pl.Buffered(1))


def _side_cast_specs(stacks, layer, n_steps, step_of):
    specs, shapes = [], []
    for w in stacks:
        axis = 2 if w.shape[2] >= w.shape[1] else 1
        nb = 1
        while nb * 2 <= n_steps and w.shape[axis] % (nb * 2 * 2 * LANES) == 0:
            nb *= 2
        block = [None, w.shape[1], w.shape[2]]
        block[axis] //= nb

        def index(*grid, axis=axis, nb=nb):
            blk = jnp.minimum(step_of(*grid), nb - 1)
            return (layer, 0, blk) if axis == 2 else (layer, blk, 0)

        specs.append((pl.BlockSpec(tuple(block), index),
                      pl.BlockSpec(tuple(block[1:]), lambda *grid, index=index: index(*grid)[1:])))
        shapes.append(jax.ShapeDtypeStruct(w.shape[1:], BF16))
    return specs, shapes


def _mod_spec(layer, which, row_of_batch):
    return pl.BlockSpec((None, None, None, 1, D_MODEL),
                        lambda b, i: (layer, row_of_batch(b), which, 0, 0))


def _rms_mod(x, g, sc, sh):
    ms = jnp.mean(x * x, axis=-1, keepdims=True)
    return (x * lax.rsqrt(ms + EPS) * g) * (1.0 + sc) + sh


def _bf16_const(a):
    return jnp.asarray(a, F32).astype(BF16)


def _dot_nt(a, b):
    return lax.dot_general(a, b, (((1,), (1,)), ((), ())), preferred_element_type=F32)


def _dot(a, b):
    return jnp.dot(a, b, preferred_element_type=F32)


def _adaln_kernel(cond_ref, w_ref, b_ref, o_ref):
    cnd = cond_ref[...]
    s = (cnd * jax.nn.sigmoid(cnd)).astype(BF16)
    o_ref[...] = _dot(s, w_ref[...].astype(BF16)) + b_ref[...]


def _adaln_call(cond, w_ada, b_ada):
    depth = w_ada.shape[0]
    rows = cond.shape[0]
    tn = 3 * D_MODEL
    return pl.pallas_call(
        _adaln_kernel,
        grid=(depth, 6 * D_MODEL // tn),
        in_specs=[
            pl.BlockSpec((rows, D_MODEL), lambda l, j: (0, 0)),
            pl.BlockSpec((None, D_MODEL, tn), lambda l, j: (l, 0, j)),
            pl.BlockSpec((None, 1, tn), lambda l, j: (l, 0, j)),
        ],
        out_specs=pl.BlockSpec((None, rows, tn), lambda l, j: (l, 0, j)),
        out_shape=jax.ShapeDtypeStruct((depth, rows, 6 * D_MODEL), F32),
        compiler_params=_params(2),
        name="adaln",
    )(cond, w_ada, b_ada.reshape(depth, 1, 6 * D_MODEL))


def _even_in_kernel(*refs, tm, n, n_cast):
    (x_ref, xp_ref, xn_ref, sh_ref, sc_ref, g_ref, win_ref, wpool_ref,
     pscale_ref, dftc_ref, wfft_ref) = refs[:11]
    a_ref, z_ref = refs[11 + n_cast:13 + n_cast]
    for w_ref, o_ref in zip(refs[11:11 + n_cast], refs[13 + n_cast:]):
        o_ref[...] = w_ref[...].astype(BF16)
    i = pl.program_id(1)
    n_tiles = n // tm
    g, sc, sh = g_ref[...], sc_ref[...], sh_ref[...]
    seqs = x_ref.shape[0]
    h = _rms_mod(x_ref[...].reshape(seqs * tm, D_MODEL), g, sc, sh).astype(BF16)
    w_in = win_ref[...].astype(BF16)
    u = _dot(h, w_in)
    xh = jnp.concatenate([xp_ref[...], xn_ref[...]], axis=0)
    hh = _rms_mod(xh, g, sc, sh).astype(BF16)
    uh = _dot(hh, w_in[:, :A_WIDTH])
    has_prev = (i > 0).astype(F32)
    has_next = (i < n_tiles - 1).astype(F32)
    rows = tm + 2 * POOL_HALO

    for gi in range(N_GROUPS):
        ub = u[:, A_WIDTH + gi * GROUP_W:A_WIDTH + (gi + 1) * GROUP_W].astype(BF16)
        cs = _dot(ub, dftc_ref[...])
        w_fft = wfft_ref[gi].astype(BF16)
        zr = _dot(cs[:, :GROUP_W].astype(BF16), w_fft)
        zi = _dot(cs[:, GROUP_W:].astype(BF16), w_fft)
        for q in range(seqs):
            z_ref[q, :, gi * GROUP_W:(gi + 1) * GROUP_W] = zr[q * tm:(q + 1) * tm].astype(BF16)
            z_ref[q, :, B_WIDTH + gi * GROUP_W:B_WIDTH + (gi + 1) * GROUP_W] = zi[q * tm:(q + 1) * tm].astype(BF16)

    t = i * tm + lax.broadcasted_iota(jnp.int32, (tm, 1), 0)
    for q, (gi, w) in ((q, gw) for q in range(seqs) for gw in enumerate(POOL_WINDOWS)):
        cols = slice(gi * GROUP_W, (gi + 1) * GROUP_W)
        uq = u[q * tm:(q + 1) * tm, cols]
        ext = jnp.concatenate([uh[0:POOL_HALO, cols] * has_prev, uq,
                               uh[POOL_HALO:, cols] * has_next], axis=0)
        s = ext
        span = 1
        while span < w:
            s = s + pltpu.roll(s, span, 0)
            span *= 2
        shift = w // 2 - 1
        s = pltpu.roll(s, rows - shift, 0) if shift else s
        acc = s[POOL_HALO:POOL_HALO + tm]
        lo = jnp.maximum(t - w // 2, 0)
        hi = jnp.minimum(t + w // 2, n)
        cnt = (hi - lo).astype(F32)
        pooled = acc / cnt - uq
        y = _dot(pooled.astype(BF16), wpool_ref[gi].astype(BF16)) * pscale_ref[:, cols]
        a_ref[q, :, cols] = y.astype(BF16)


def _even_in_call(x, mod, layer, row_of_batch, g, e, w_in, w_pool, pool_scale, dftc, w_fft, tm, seqs,
                  side_cast=None):
    bsz, n, _ = x.shape
    assert seqs == 1 or tm == n
    hb = tm // POOL_HALO
    tiles = n // tm
    stacks, cast_layer = side_cast if side_cast else ((), 0)
    cast_specs, cast_shapes = _side_cast_specs(stacks, cast_layer, (bsz // seqs) * tiles,
                                               lambda b, i: b * tiles + i)
    kern = functools.partial(_even_in_kernel, tm=tm, n=n, n_cast=len(stacks))
    return pl.pallas_call(
        kern,
        grid=(bsz // seqs, n // tm),
        in_specs=[
            pl.BlockSpec((seqs, tm, D_MODEL), lambda b, i: (b, i, 0)),
            pl.BlockSpec((None, POOL_HALO, D_MODEL),
                         lambda b, i: (b * seqs, jnp.maximum(i * hb - 1, 0), 0)),
            pl.BlockSpec((None, POOL_HALO, D_MODEL),
                         lambda b, i: (b * seqs, jnp.minimum((i + 1) * hb, n // POOL_HALO - 1), 0)),
            _mod_spec(layer, 0, row_of_batch),
            _mod_spec(layer, 1, row_of_batch),
            _const_spec((1, D_MODEL)),
            _stacked_spec((D_MODEL, D_MODEL), e),
            _stacked_spec((N_GROUPS, GROUP_W, GROUP_W), e),
            _const_spec((1, A_WIDTH)),
            _const_spec((GROUP_W, 2 * GROUP_W)),
            _stacked_spec((N_GROUPS, GROUP_W, GROUP_W), e),
        ] + [i_spec for i_spec, _ in cast_specs],
        out_specs=[
            pl.BlockSpec((seqs, tm, A_WIDTH), lambda b, i: (b, i, 0)),
            pl.BlockSpec((seqs, tm, 2 * B_WIDTH), lambda b, i: (b, i, 0)),
        ] + [o_spec for _, o_spec in cast_specs],
        out_shape=[
            jax.ShapeDtypeStruct((bsz, n, A_WIDTH), BF16),
            jax.ShapeDtypeStruct((bsz, n, 2 * B_WIDTH), BF16),
        ] + cast_shapes,
        compiler_params=_params(2),
        name="even_in",
    )(x, x, x, mod, mod, g, w_in, w_pool, pool_scale, dftc, w_fft, *stacks)


def _dft_direct_kernel(z_ref, m_ref, y_ref, *, scale):
    for q in range(z_ref.shape[0]):
        z = z_ref[q]
        rhs = jnp.concatenate([z[:, :B_WIDTH], z[:, B_WIDTH:]], axis=0)
        y_ref[q] = (_dot(m_ref[...], rhs) * scale).astype(BF16)


def _dft_direct_call(z):
    bsz, n, _ = z.shape
    k = np.arange(n)
    ang = 2.0 * np.pi * np.outer(k, k) / n
    m = _bf16_const(np.concatenate([np.cos(ang), np.sin(ang)], axis=1))
    kern = functools.partial(_dft_direct_kernel, scale=float((n * GROUP_W) ** -0.5))
    seqs = next(q for q in (4, 2, 1) if bsz % q == 0)
    return pl.pallas_call(
        kern,
        grid=(bsz // seqs,),
        in_specs=[pl.BlockSpec((seqs, n, 2 * B_WIDTH), lambda b: (b, 0, 0)),
                  _const_spec((n, 2 * n))],
        out_specs=pl.BlockSpec((seqs, n, B_WIDTH), lambda b: (b, 0, 0)),
        out_shape=jax.ShapeDtypeStruct((bsz, n, B_WIDTH), BF16),
        compiler_params=_params(1),
        name="dft_direct",
    )(z, m)


DFT_ROWS = 16
DFT_DIRECT_MAX = 256
DFT_PITCH = 24


def _pitch_rows(x):
    g, _, w = x.shape
    pad = jnp.zeros((g, DFT_PITCH - DFT_ROWS, w), x.dtype)
    return jnp.concatenate([x, pad], axis=1).reshape(g * DFT_PITCH, w)


def _dft_stage1_kernel(z_ref, m_ref, twr_ref, twi_ref, o_ref, zs_ref, os_ref, *, na):
    nt = 2 * B_WIDTH // LANES

    @pl.when((pl.program_id(0) == 0) & (pl.program_id(1) == 0))
    def _():
        os_ref[...] = jnp.zeros_like(os_ref)

    zf = _pitch_rows(z_ref[...].astype(F32))
    for k in range(nt):
        zs_ref[k] = zf[:, k * LANES:(k + 1) * LANES]
    for bi in range(DFT_ROWS):
        rows = pl.ds(bi, na, stride=DFT_PITCH)
        zb = [zs_ref[k, rows, :].astype(BF16) for k in range(nt)]
        rhs = jnp.concatenate([jnp.concatenate(zb[:nt // 2], axis=1),
                               jnp.concatenate(zb[nt // 2:], axis=1)], axis=0)
        r = _dot(m_ref[...], rhs)
        br, bim = r[:na], r[na:]
        tr, ti = twr_ref[bi], twi_ref[bi]
        out = (br * tr - bim * ti, br * ti + bim * tr)
        for k in range(nt):
            half, kk = divmod(k, nt // 2)
            os_ref[k, rows, :] = out[half][:, kk * LANES:(kk + 1) * LANES]
    of = jnp.concatenate([os_ref[k] for k in range(nt)], axis=1)
    o_ref[...] = of.reshape(na, DFT_PITCH, 2 * B_WIDTH)[:, :DFT_ROWS].astype(BF16)


def _dft_stage2_kernel(b_ref, m_ref, y_ref, ys_ref, *, scale):
    nb = b_ref.shape[1]

    @pl.when((pl.program_id(0) == 0) & (pl.program_id(1) == 0))
    def _():
        ys_ref[...] = jnp.zeros_like(ys_ref)

    for di in range(DFT_ROWS):
        bb = b_ref[di]
        rhs = jnp.concatenate([bb[:, :B_WIDTH], bb[:, B_WIDTH:]], axis=0)
        y = _dot(m_ref[...], rhs) * scale
        for k in range(B_WIDTH // LANES):
            ys_ref[k, pl.ds(di, nb, stride=DFT_PITCH), :] = y[:, k * LANES:(k + 1) * LANES]
    yf = jnp.concatenate([ys_ref[k] for k in range(B_WIDTH // LANES)], axis=1)
    y_ref[...] = yf.reshape(nb, DFT_PITCH, B_WIDTH)[:, :DFT_ROWS].astype(BF16)


def _dft_two_stage_call(z):
    bsz, n, _ = z.shape
    na = 1 << (int(math.log2(n)) // 2)
    nb = n // na
    ia = np.arange(na)
    ib = np.arange(nb)
    ang_a = 2.0 * np.pi * np.outer(ia, ia) / na
    fr, fi = np.cos(ang_a), -np.sin(ang_a)
    m1 = _bf16_const(np.block([[fr, -fi], [fi, fr]]))
    ang_t = 2.0 * np.pi * np.outer(ib, ia) / n
    twr = jnp.asarray(np.cos(ang_t)[:, :, None], F32)
    twi = jnp.asarray(-np.sin(ang_t)[:, :, None], F32)
    ang_b = 2.0 * np.pi * np.outer(ib, ib) / nb
    m2 = _bf16_const(np.concatenate([np.cos(ang_b), np.sin(ang_b)], axis=1))

    s1 = pl.pallas_call(
        functools.partial(_dft_stage1_kernel, na=na),
        grid=(bsz, nb // DFT_ROWS),
        in_specs=[
            pl.BlockSpec((None, na, DFT_ROWS, 2 * B_WIDTH), lambda b, j: (b, 0, j, 0)),
            _const_spec((2 * na, 2 * na)),
            pl.BlockSpec((DFT_ROWS, na, 1), lambda b, j: (j, 0, 0)),
            pl.BlockSpec((DFT_ROWS, na, 1), lambda b, j: (j, 0, 0)),
        ],
        out_specs=pl.BlockSpec((None, na, DFT_ROWS, 2 * B_WIDTH), lambda b, j: (b, 0, j, 0)),
        out_shape=jax.ShapeDtypeStruct((bsz, na, nb, 2 * B_WIDTH), BF16),
        scratch_shapes=[pltpu.VMEM((2 * B_WIDTH // LANES, na * DFT_PITCH, LANES), F32)] * 2,
        compiler_params=_params(2),
        name="dft_stage1",
    )(z.reshape(bsz, na, nb, 2 * B_WIDTH), m1, twr, twi)
    y = pl.pallas_call(
        functools.partial(_dft_stage2_kernel, scale=float((n * GROUP_W) ** -0.5)),
        grid=(bsz, na // DFT_ROWS),
        in_specs=[
            pl.BlockSpec((None, DFT_ROWS, nb, 2 * B_WIDTH), lambda b, j: (b, j, 0, 0)),
            _const_spec((nb, 2 * nb)),
        ],
        out_specs=pl.BlockSpec((None, nb, DFT_ROWS, B_WIDTH), lambda b, j: (b, 0, j, 0)),
        out_shape=jax.ShapeDtypeStruct((bsz, nb, na, B_WIDTH), BF16),
        scratch_shapes=[pltpu.VMEM((B_WIDTH // LANES, nb * DFT_PITCH, LANES), F32)],
        compiler_params=_params(2),
        name="dft_stage2",
    )(s1, m2)
    return y.reshape(bsz, n, B_WIDTH)


FF_CHUNK = 1024


def _out_mlp_kernel(*refs, steps_a, n_cast):
    (xa_ref, p1a_ref, p2a_ref, xb_ref, p1b_ref, p2b_ref, g1_ref, sh_ref, sc_ref, g2_ref,
     ng_ref, wout_ref, w1_ref, w2_ref) = refs[:14]
    oa_ref, ob_ref = refs[14 + n_cast:16 + n_cast]
    for w_ref, o_ref in zip(refs[14:14 + n_cast], refs[16 + n_cast:]):
        o_ref[...] = w_ref[...].astype(BF16)

    def body(x_ref, p1_ref, p2_ref, o_ref):
        half = p1_ref.shape[-1]
        mix = (_dot(p1_ref[...], wout_ref[:half, :].astype(BF16))
               + _dot(p2_ref[...], wout_ref[half:, :].astype(BF16)))
        x1 = x_ref[...] + g1_ref[...] * mix
        h = _rms_mod(x1, ng_ref[...], sc_ref[...], sh_ref[...]).astype(BF16)
        acc = None
        for c in range(D_FF // FF_CHUNK):
            a = _dot(h, w1_ref[:, c * FF_CHUNK:(c + 1) * FF_CHUNK])
            a = jnp.square(jnp.maximum(a, 0.0)).astype(BF16)
            part = _dot(a, w2_ref[c * FF_CHUNK:(c + 1) * FF_CHUNK, :])
            acc = part if acc is None else acc + part
        o_ref[...] = x1 + g2_ref[...] * acc

    @pl.when(pl.program_id(0) < steps_a)
    def _():
        body(xa_ref, p1a_ref, p2a_ref, oa_ref)

    @pl.when(pl.program_id(0) >= steps_a)
    def _():
        body(xb_ref, p1b_ref, p2b_ref, ob_ref)


def _out_mlp_call(set_a, set_b, mod, layer, ng, w_out, out_idx, w1, w2, side_cast=None):
    (xa, p1a, p2a, row_a), (xb, p1b, p2b, row_b) = set_a, set_b
    half = p1a.shape[-1]
    tm = min(xa.shape[1], xb.shape[1], MLP_TILE)
    tiles_a, tiles_b = xa.shape[1] // tm, xb.shape[1] // tm
    steps_a, steps_b = xa.shape[0] * tiles_a, xb.shape[0] * tiles_b

    def pos_a(s):
        sa = jnp.minimum(s, steps_a - 1)
        return sa // tiles_a, sa % tiles_a

    def pos_b(s):
        sb = jnp.maximum(s - steps_a, 0)
        return sb // tiles_b, sb % tiles_b

    tok_a = lambda w: pl.BlockSpec((None, tm, w), lambda s: pos_a(s) + (0,))
    tok_b = lambda w: pl.BlockSpec((None, tm, w), lambda s: pos_b(s) + (0,))
    row = lambda s: jnp.where(s < steps_a, row_a + pos_a(s)[0], row_b + pos_b(s)[0])
    mod_spec = lambda which: pl.BlockSpec((None, None, None, 1, D_MODEL),
                                          lambda s: (layer, row(s), which, 0, 0))
    stacks, cast_layer = side_cast if side_cast else ((), 0)
    cast_specs, cast_shapes = _side_cast_specs(stacks, cast_layer, steps_a + steps_b, lambda s: s)
    return pl.pallas_call(
        functools.partial(_out_mlp_kernel, steps_a=steps_a, n_cast=len(stacks)),
        grid=(steps_a + steps_b,),
        in_specs=[
            tok_a(D_MODEL), tok_a(half), tok_a(half), tok_b(D_MODEL), tok_b(half), tok_b(half),
            mod_spec(2), mod_spec(3), mod_spec(4), mod_spec(5),
            _const_spec((1, D_MODEL)),
            _stacked_spec((2 * half, D_MODEL), out_idx),
            _const_spec((D_MODEL, D_FF)),
            _const_spec((D_FF, D_MODEL)),
        ] + [i_spec for i_spec, _ in cast_specs],
        out_specs=[tok_a(D_MODEL), tok_b(D_MODEL)] + [o_spec for _, o_spec in cast_specs],
        out_shape=[jax.ShapeDtypeStruct(xa.shape, F32), jax.ShapeDtypeStruct(xb.shape, F32)] + cast_shapes,
        compiler_params=_params(1),
        name="out_mlp",
    )(xa, p1a, p2a, xb, p1b, p2b, mod, mod, mod, mod, ng, w_out, w1, w2, *stacks)


def _odd_in_kernel(*refs, rope, emit_cache, tm):
    x_ref, sh_ref, sc_ref, g_ref, win_ref, gain_ref, hm_ref = refs[:7]
    k = 7
    if rope:
        cos_ref, sin_ref = refs[k:k + 2]
        k += 2
    cq_ref, ck_ref, cvt_ref, dq_ref, dk_ref, dvt_ref = refs[k:k + 6]
    k += 6
    if emit_cache:
        nck_ref, ncv_ref, ndk_ref, ndv_ref = refs[k:k + 4]
    subs = x_ref.shape[0] // tm
    lane = lax.broadcasted_iota(jnp.int32, (tm, LANES), 1)
    first_half = (lane % (2 * ROPE_FREQS)) < ROPE_FREQS
    lo = lane < HEAD_DIM
    pad_row = lax.broadcasted_iota(jnp.int32, (V_ROWS - LANES, tm), 0)
    pad = jnp.where(pad_row == 0, 1.0, 0.0).astype(BF16)

    def cache_rows(ref, sub, r):
        rpt = ref.shape[0] // x_ref.shape[0]
        return ref.at[pl.ds(sub * tm * rpt + r, tm, stride=rpt), :]

    def cache_cols(ref, sub, lead, mat):
        n_req = ref.shape[-1]
        for r in range(tm // n_req):
            ref[(sub * (tm // n_req) + r,) + lead] = mat[:, r * n_req:(r + 1) * n_req]

    groups = ((0, ODD_CV0), (ODD_DQ0, ODD_W), (ODD_CV0, ODD_DQ0))
    gain_lo = (0, ODD_CV0)
    normed_w = (ODD_CV0, DQ_W + DKV_W)
    hidden = {}

    def project(item):
        sub, grp = item
        if sub not in hidden:
            x = x_ref[sub * tm:(sub + 1) * tm, :]
            hidden[sub] = _rms_mod(x, g_ref[...], sc_ref[...], sh_ref[...]).astype(BF16)
        lo, hi = groups[grp]
        return _dot(hidden[sub], win_ref[:, lo:hi].astype(BF16))

    def head_norms(p, lo):
        normed = []
        for c0 in range(0, p.shape[1], 2 * LANES):
            wd = min(2 * LANES, p.shape[1] - c0)
            v = p[:, c0:c0 + wd]
            ms = _dot((v * v).astype(BF16), hm_ref[:wd, :wd])
            y = v * lax.rsqrt(ms + EPS) * gain_ref[:, lo + c0:lo + c0 + wd]
            normed += [y[:, k:k + LANES] for k in range(0, wd, LANES)]
        return normed

    def finish(item, p, normed):
        sub, grp = item
        rows = slice(sub * tm, (sub + 1) * tm)

        def roped(y):
            if not rope:
                return y
            rot = jnp.where(first_half, -pltpu.roll(y, LANES - ROPE_FREQS, 1),
                            pltpu.roll(y, ROPE_FREQS, 1))
            return y * cos_ref[rows, :] + rot * sin_ref[rows, :]

        def put(ref, chunks):
            for c, y in enumerate(chunks):
                ref[rows, c * LANES:(c + 1) * LANES] = roped(y).astype(BF16)

        nq = C_W // LANES
        if grp == 0:
            put(cq_ref, normed[:nq])
            put(ck_ref, normed[nq:])
            if emit_cache:
                for c, y in enumerate(normed[nq:]):
                    yt = y.T
                    for half in range(2):
                        cache_cols(nck_ref, sub, (c, half), yt[half * HEAD_DIM:(half + 1) * HEAD_DIM])
        elif grp == 1:
            put(dq_ref, normed[:nq])
            dk = normed[nq]
            y = roped(dk)
            swapped = pltpu.roll(y, HEAD_DIM, 1)
            dk_ref[rows, :LANES] = jnp.where(lo, y, swapped).astype(BF16)
            dk_ref[rows, LANES:] = jnp.where(lo, swapped, y).astype(BF16)
            if emit_cache:
                dkt = dk.T
                for j in range(D_KV_HEADS):
                    cache_cols(ndk_ref, sub, (j,), dkt[j * HEAD_DIM:(j + 1) * HEAD_DIM])
            dvt = p[:, normed_w[1]:].T
            for j in range(D_KV_HEADS):
                vj = dvt[j * HEAD_DIM:(j + 1) * HEAD_DIM]
                dvt_ref[j, :LANES, rows] = jnp.concatenate([vj, vj], axis=0).astype(BF16)
                dvt_ref[j, LANES:, rows] = pad
                if emit_cache:
                    cache_cols(ndv_ref, sub, (j,), vj)
        else:
            cv = p
            for hh in range(C_HEADS):
                cvt_ref[hh, :LANES, rows] = cv[:, hh * LANES:(hh + 1) * LANES].T.astype(BF16)
                cvt_ref[hh, LANES:, rows] = pad
            if emit_cache:
                for hh in range(C_HEADS):
                    cache_rows(ncv_ref, sub, hh)[...] = cv[:, hh * LANES:(hh + 1) * LANES]

    items = [(sub, grp) for sub in range(subs) for grp in range(len(groups))]
    p = project(items[0])
    for idx, item in enumerate(items):
        normed = head_norms(p[:, :normed_w[item[1]]], gain_lo[item[1]]) if item[1] < 2 else None
        nxt = project(items[idx + 1]) if idx + 1 < len(items) else None
        finish(item, p, normed)
        p = nxt


def _odd_in_call(x, mod, layer, row_of_batch, g, w_in, odd_idx, gains, head_mean, rope_tabs, tm, cache_n=None):
    emit_cache = cache_n is not None
    bsz, n, _ = x.shape
    rope = rope_tabs is not None
    subs = 2 if n % (2 * tm) == 0 else 1
    ts = subs * tm
    tok = lambda w: pl.BlockSpec((None, ts, w), lambda b, i: (b, i, 0))
    in_specs = [
        tok(D_MODEL),
        _mod_spec(layer, 0, row_of_batch),
        _mod_spec(layer, 1, row_of_batch),
        _const_spec((1, D_MODEL)),
        _stacked_spec((D_MODEL, ODD_W), odd_idx),
        _const_spec((1, ODD_NORMED_W)),
        _const_spec((2 * LANES, 2 * LANES)),
    ]
    args = [x, mod, mod, g, w_in, gains, head_mean]
    if rope:
        in_specs += [pl.BlockSpec((ts, LANES), lambda b, i: (i, 0))] * 2
        args += list(rope_tabs)
    vt_spec = lambda heads: pl.BlockSpec((None, heads, V_ROWS, ts), lambda b, i: (b, 0, 0, i))
    vt_shape = lambda heads: jax.ShapeDtypeStruct((bsz, heads, V_ROWS, n), BF16)
    tok_shape = lambda w: jax.ShapeDtypeStruct((bsz, n, w), BF16)
    out_specs = [tok(C_W), tok(C_W), vt_spec(C_HEADS), tok(DQ_W), tok(DKV_DUP_W), vt_spec(D_KV_HEADS)]
    out_shape = [tok_shape(C_W), tok_shape(C_W), vt_shape(C_HEADS), tok_shape(DQ_W),
                 tok_shape(DKV_DUP_W), vt_shape(D_KV_HEADS)]
    if emit_cache:
        assert bsz == 1 and tm % cache_n == 0 and n % cache_n == 0
        reqs, rps = n // cache_n, ts // cache_n
        for lead in ((C_HEADS, 2), None, (D_KV_HEADS,), (D_KV_HEADS,)):
            if lead is None:
                out_specs.append(pl.BlockSpec((None, ts * C_HEADS, LANES), lambda b, i: (b, i, 0)))
                out_shape.append(jax.ShapeDtypeStruct((bsz, n * C_HEADS, LANES), F32))
            else:
                zeros = (0,) * (len(lead) + 2)
                out_specs.append(pl.BlockSpec((rps,) + lead + (HEAD_DIM, cache_n),
                                              lambda b, i, zeros=zeros: (i,) + zeros))
                out_shape.append(jax.ShapeDtypeStruct((reqs,) + lead + (HEAD_DIM, cache_n), F32))
    return pl.pallas_call(
        functools.partial(_odd_in_kernel, rope=rope, emit_cache=emit_cache, tm=tm),
        grid=(bsz, n // ts),
        in_specs=in_specs,
        out_specs=out_specs,
        out_shape=out_shape,
        compiler_params=_params(2),
        name="odd_in",
    )(*args)


QK_AHEAD = 3
SINK_AHEAD = 2


def _diff_attn_kernel(*refs, lam_init, tq, kc):
    n_src = (len(refs) - 4) // 2
    q_ref, (lamv_ref, g_ref, o_ref) = refs[0], refs[-3:]
    k_refs, vt_refs = refs[1:1 + n_src], refs[1 + n_src:1 + 2 * n_src]
    chunks = [(src, off) for src in range(n_src) for off in range(0, k_refs[src].shape[1], kc)]
    n_chunks = len(chunks)
    reqs = q_ref.shape[0]
    tiles = q_ref.shape[1] // tq
    heads = q_ref.shape[2] // LANES
    lane = lax.broadcasted_iota(jnp.int32, (tq, LANES), 1)
    lv = lamv_ref[...]
    lam = (jnp.exp(jnp.sum(lv[0:1] * lv[1:2], keepdims=True))
           - jnp.exp(jnp.sum(lv[2:3] * lv[3:4], keepdims=True)) + lam_init)

    qqs = {}

    def score(item):
        r, h, t, c = item
        cols = slice(h * LANES, (h + 1) * LANES)
        if (r, h, t) not in qqs:
            q = q_ref[r, t * tq:(t + 1) * tq, cols]
            zero = jnp.zeros_like(q)
            qqs[r, h, t] = jnp.concatenate([jnp.where(lane < HEAD_DIM, q, zero),
                                            jnp.where(lane >= HEAD_DIM, q, zero)], axis=0)
        src, off = chunks[c]
        return _dot_nt(k_refs[src][r, off:off + kc, cols], qqs[r, h, t])

    def finish(r, h, t, acc, l):
        inv = 1.0 / l
        ot = acc[:, :tq] * inv[:, :tq] - acc[:, tq:] * (inv[:, tq:] * lam)
        o = ot.T
        ms = jnp.mean(o * o, axis=-1, keepdims=True)
        y = o * lax.rsqrt(ms + EPS) * g_ref[...] * (1.0 - lam_init)
        o_ref[r, t * tq:(t + 1) * tq, h * LANES:(h + 1) * LANES] = y.astype(BF16)

    items = [(r, h, t, c) for r in range(reqs) for h in range(heads) for t in range(tiles)
             for c in range(n_chunks)]
    ahead = [score(it) for it in items[:QK_AHEAD]]
    m = acc = None
    for idx, (r, h, t, c) in enumerate(items):
        s = ahead.pop(0)
        if idx + QK_AHEAD < len(items):
            ahead.append(score(items[idx + QK_AHEAD]))
        cm = jnp.max(s, axis=0, keepdims=True)
        m_new = cm if c == 0 else jnp.maximum(m, cm)
        src, off = chunks[c]
        e = jnp.exp2(s - m_new)
        lsum = jnp.sum(e, axis=0, keepdims=True)
        lane0 = r * k_refs[src].shape[1] + off
        pv = _dot(vt_refs[src][h, :LANES, lane0:lane0 + kc], e.astype(BF16))
        if c == 0:
            acc, l = pv, lsum
        else:
            alpha = jnp.exp2(m - m_new)
            acc, l = acc * alpha + pv, l * alpha + lsum
        m = m_new
        if c == n_chunks - 1:
            finish(r, h, t, acc, l)


def _diff_attn_call(q, ks, vts, lamv, subln_g, lam_init, tq, tiles, heads, reqs=1, flat_values=False):
    bsz, n, _ = q.shape
    assert reqs == 1 or flat_values
    kc = next(c for c in (512, 256, 128) if all(k.shape[1] % c == 0 for k in ks))
    hw = heads * LANES
    k_specs = [pl.BlockSpec((reqs, k.shape[1], hw), lambda b, h, i: (b, 0, h)) for k in ks]
    vt_index = (lambda b, h, i: (0, h, 0, b)) if flat_values else (lambda b, h, i: (b, h, 0, 0))
    vt_specs = [pl.BlockSpec((None, heads, V_ROWS, reqs * k.shape[1]), vt_index) for k in ks]
    return pl.pallas_call(
        functools.partial(_diff_attn_kernel, lam_init=lam_init, tq=tq, kc=kc),
        grid=(bsz // reqs, C_HEADS // heads, n // (tq * tiles)),
        in_specs=[pl.BlockSpec((reqs, tiles * tq, hw), lambda b, h, i: (b, i, h))]
        + k_specs + vt_specs + [_const_spec((4, HEAD_DIM)), _const_spec((1, LANES))],
        out_specs=pl.BlockSpec((reqs, tiles * tq, hw), lambda b, h, i: (b, i, h)),
        out_shape=jax.ShapeDtypeStruct((bsz, n, C_W), BF16),
        compiler_params=_params(3),
        name="diff_attn",
    )(q, *ks, *vts, lamv, subln_g)


def _sink_attn_kernel(*refs, windowed, tq, nblk):
    if windowed:
        (q_ref, ck_ref, kp_ref, km_ref, kn_ref, cvt_ref, vp_ref, vm_ref, vn_ref,
         bias_ref, sink_ref, o_ref) = refs
        kband = jnp.concatenate([kp_ref[...], km_ref[...], kn_ref[...]], axis=0)
        vtband = jnp.concatenate([vp_ref[...], vm_ref[...], vn_ref[...]], axis=1)
    else:
        q_ref, k_ref, vt_ref, sink_ref, o_ref = refs
    reqs = q_ref.shape[0]
    subs = q_ref.shape[1] // tq
    groups = q_ref.shape[2] // (2 * LANES)
    lane = lax.broadcasted_iota(jnp.int32, (tq, LANES), 1)
    lo = lane < HEAD_DIM

    def scores_of(stream):
        r, g, sub = stream
        qd = q_ref[r, sub * tq:(sub + 1) * tq, g * 2 * LANES:(g + 1) * 2 * LANES]
        parts = []
        for c in range(2):
            ch = qd[:, c * LANES:(c + 1) * LANES]
            zero = jnp.zeros_like(ch)
            parts += [jnp.where(lo, ch, zero), jnp.where(lo, zero, ch)]
        qq = jnp.concatenate(parts, axis=0)
        if not windowed:
            return [_dot_nt(k_ref[r, :, g * LANES:(g + 1) * LANES], qq)]
        blk = pl.program_id(2) * subs + sub
        variant = jnp.where(blk == 0, 0, jnp.where(blk == nblk - 1, 2, 1))
        band = _dot_nt(kband[sub * tq:(sub + 3) * tq, :], qq) + bias_ref[variant]
        return [_dot_nt(ck_ref[...], qq), band]

    def finish(stream, scores):
        r, g, sub = stream
        values = ([cvt_ref[...], vtband[:, sub * tq:(sub + 3) * tq]] if windowed
                  else [vt_ref[g, :, r * tq:(r + 1) * tq]])
        sink = sink_ref[g]
        m = sink
        for s in scores:
            m = jnp.maximum(m, jnp.max(s, axis=0, keepdims=True))
        acc = None
        for s, vt in zip(scores, values):
            pv = _dot(vt, jnp.exp2(s - m).astype(BF16))
            acc = pv if acc is None else acc + pv
        inv = 1.0 / (acc[LANES:LANES + 1, :] + jnp.exp2(sink - m))
        ot = acc[:LANES, :] * inv
        for c in range(2):
            o0 = ot[:, (2 * c) * tq:(2 * c + 1) * tq].T
            o1 = ot[:, (2 * c + 1) * tq:(2 * c + 2) * tq].T
            cols = slice((2 * g + c) * LANES, (2 * g + c + 1) * LANES)
            o_ref[r, sub * tq:(sub + 1) * tq, cols] = jnp.where(lo, o0, o1).astype(BF16)

    streams = [(r, g, sub) for r in range(reqs) for g in range(groups) for sub in range(subs)]
    ahead = [scores_of(st) for st in streams[:SINK_AHEAD]]
    for idx, st in enumerate(streams):
        cur = ahead.pop(0)
        if idx + SINK_AHEAD < len(streams):
            ahead.append(scores_of(streams[idx + SINK_AHEAD]))
        finish(st, cur)


def _window_bias(n, tq):
    nblk = n // tq
    kk = np.arange(3 * tq)[:, None]
    qi = np.arange(tq)[None, :]
    band_ok = np.abs(qi + WINDOW - kk) <= WINDOW
    out = []
    for blk in (0, 1, nblk - 1):
        jpos = blk * tq - WINDOW + kk
        ok = band_ok & (jpos >= 0) & (jpos < n)
        out.append(np.tile(np.where(ok, 0.0, NEG_INF), (1, D_GROUP)))
    return jnp.asarray(np.stack(out), F32)


def _sink_attn_call(q, k, vt, sink_rows, ctx_k=None, ctx_vt=None, reqs=1, flat_values=False):
    bsz, n, _ = q.shape
    windowed = ctx_k is not None
    assert reqs == 1 or (flat_values and not windowed)
    tq = WINDOW if windowed else n
    nblk = n // tq
    subs = next(s for s in (16, 8, 4, 2, 1) if nblk % s == 0) if windowed else 1
    steps = nblk // subs
    groups = 1 if windowed else D_KV_HEADS
    q_spec = pl.BlockSpec((reqs, subs * tq, groups * 2 * LANES), lambda b, j, i: (b, i, j))
    sink_spec = pl.BlockSpec((groups, 1, D_GROUP * tq), lambda b, j, i: (j, 0, 0))
    if windowed:
        assert nblk >= 3, "first / interior / last mask variants need three query blocks"
        n_ctx = ctx_k.shape[1]
        prev = lambda i: jnp.maximum(i * subs - 1, 0)
        nxt = lambda i: jnp.minimum((i + 1) * subs, nblk - 1)
        in_specs = [
            q_spec,
            pl.BlockSpec((None, n_ctx, LANES), lambda b, j, i: (b, 0, j)),
            pl.BlockSpec((None, tq, LANES), lambda b, j, i: (b, prev(i), j)),
            pl.BlockSpec((None, subs * tq, LANES), lambda b, j, i: (b, i, j)),
            pl.BlockSpec((None, tq, LANES), lambda b, j, i: (b, nxt(i), j)),
            pl.BlockSpec((None, None, V_ROWS, n_ctx), lambda b, j, i: (b, j, 0, 0)),
            pl.BlockSpec((None, None, V_ROWS, tq), lambda b, j, i: (b, j, 0, prev(i))),
            pl.BlockSpec((None, None, V_ROWS, subs * tq), lambda b, j, i: (b, j, 0, i)),
            pl.BlockSpec((None, None, V_ROWS, tq), lambda b, j, i: (b, j, 0, nxt(i))),
            _const_spec((3, 3 * tq, D_GROUP * tq)),
            sink_spec,
        ]
        args = (q, ctx_k, k, k, k, ctx_vt, vt, vt, vt, _window_bias(n, tq), sink_rows)
    else:
        in_specs = [
            q_spec,
            pl.BlockSpec((reqs, n, groups * LANES), lambda b, j, i: (b, 0, j)),
            pl.BlockSpec((None, groups, V_ROWS, reqs * n),
                         (lambda b, j, i: (0, j, 0, b)) if flat_values else (lambda b, j, i: (b, j, 0, 0))),
            sink_spec,
        ]
        args = (q, k, vt, sink_rows)
    return pl.pallas_call(
        functools.partial(_sink_attn_kernel, windowed=windowed, tq=tq, nblk=nblk),
        grid=(bsz // reqs, D_KV_HEADS // groups, steps),
        in_specs=in_specs,
        out_specs=q_spec,
        out_shape=jax.ShapeDtypeStruct((bsz, n, DQ_W), BF16),
        compiler_params=_params(3),
        name="sink_attn",
    )(*args)


def _dup_heads(a):
    lead = a.shape[:-1]
    a = a.reshape(lead + (D_KV_HEADS, 1, HEAD_DIM))
    return jnp.broadcast_to(a, lead + (D_KV_HEADS, 2, HEAD_DIM)).reshape(lead + (DKV_DUP_W,))


def _odd_gains(c_qn_g, c_kn_g, d_qn_g, d_kn_g):
    return jnp.concatenate([
        jnp.tile(c_qn_g * (ATTN_SCALE * LOG2E), C_W // HEAD_DIM),
        jnp.tile(c_kn_g, C_W // HEAD_DIM),
        jnp.tile(d_qn_g * (ATTN_SCALE * LOG2E), DQ_W // HEAD_DIM),
        jnp.tile(d_kn_g, DKV_W // HEAD_DIM),
    ])[None, :]


def _value_rows(v):
    bsz, nk, w = v.shape
    vt = v.reshape(bsz, nk, w // LANES, LANES).transpose(0, 2, 3, 1)
    pad = jnp.zeros((bsz, w // LANES, V_ROWS - LANES, nk), v.dtype).at[:, :, 0, :].set(1)
    return jnp.concatenate([vt, pad], axis=2)


def _rope_tables(n):
    rows = n // GRID_W
    row = np.repeat(np.arange(rows), GRID_W).astype(np.float64)
    col = np.tile(np.arange(GRID_W), rows).astype(np.float64)
    freqs = ROPE_BASE ** (-np.arange(ROPE_FREQS, dtype=np.float64) / ROPE_FREQS)
    ang = np.stack([row[:, None] * freqs, col[:, None] * freqs], axis=1)
    ang = np.concatenate([ang, ang], axis=-1).reshape(n, HEAD_DIM)
    ang = np.concatenate([ang, ang], axis=-1)
    return jnp.asarray(np.cos(ang), F32), jnp.asarray(np.sin(ang), F32)


MLP_TILE = 512


def _token_tile(n):
    return min(n, 512)


def _mixer(x, l, mod, row_of_batch, wts, caches, new_cache, side_cast=None):
    bsz, n, _ = x.shape
    decode = caches is not None
    g1n = wts["norm1_g"][l][None, :]
    if l % 2 == 0:
        e = l // 2
        a, z, *copies = _even_in_call(x, mod, l, row_of_batch, g1n, e, wts["w_in_even"], wts["w_pool"],
                                      wts["pool_scale"][e][None, :], wts["dftc"], wts["w_fft"], min(n, 1024),
                                      1 if decode or n > 1024 else next(q for q in (4, 2, 1) if bsz % q == 0),
                                      side_cast)
        y = _dft_two_stage_call(z) if n > DFT_DIRECT_MAX else _dft_direct_call(z)
        return (a, y) + tuple(copies)
    o = l // 2
    lam_init = _lam_init(l)
    rope_tabs = _rope_tables(n) if decode else None
    xo = x if decode else x.reshape(1, bsz * n, D_MODEL)
    outs = _odd_in_call(xo, mod, l, row_of_batch, g1n, wts["w_in_odd"], o, wts["odd_gains"][o],
                        wts["head_mean"], rope_tabs, _token_tile(xo.shape[1]),
                        None if decode else n)
    if not decode:
        outs = [a if a.ndim == 4 else a.reshape(bsz, n, a.shape[-1]) for a in outs[:6]] + list(outs[6:])
    cq, ck, cvt, dq, dkk, dvt = outs[:6]
    if decode:
        c_k, c_v, d_k, d_v = caches
        lc = c_k.shape[2]
        ks = [c_k[:, o].reshape(bsz, lc, C_W).astype(BF16), ck]
        vts = [_value_rows(c_v[:, o].reshape(bsz, lc, C_W).astype(BF16)), cvt]
        ctx_k = _dup_heads(d_k[:, o].reshape(bsz, lc, DKV_W)).astype(BF16)
        ctx_vt = _value_rows(_dup_heads(d_v[:, o].reshape(bsz, lc, DKV_W)).astype(BF16))
        tq = min(n, 256)
    else:
        new_cache.append(outs[6:])
        ks, vts = [ck], [cvt]
        ctx_k = ctx_vt = None
        tq = n
    tiles = next(t for t in (4, 2, 1) if n % (t * tq) == 0)
    heads = 1 if decode else C_HEADS
    c_out = _diff_attn_call(cq, ks, vts, wts["lamv"][o],
                            wts["c_subln_g"][o][None, :], lam_init, tq, tiles, heads,
                            1 if decode else next(r for r in (4, 2, 1) if bsz % r == 0), not decode)
    sink_tq = WINDOW if decode else n
    sink_rows = jnp.repeat(wts["d_sink"][o].reshape(D_KV_HEADS, D_GROUP) * LOG2E, sink_tq,
                           axis=-1)[:, None, :]
    d_out = _sink_attn_call(dq, dkk, dvt, sink_rows, ctx_k, ctx_vt,
                            1 if decode else next(r for r in (4, 2, 1) if bsz % r == 0), not decode)
    return c_out, d_out


def kernel(x_prompt, x_sample, c, cache_c_k, cache_c_v, cache_d_k, cache_d_v, c_ctx,
           norm1_g, norm2_g, w_ada, b_ada, w_in_even, w_pool, pool_scale, w_fft, w_out_even,
           w_in_odd, c_qn_g, c_kn_g, lam_q1, lam_k1, lam_q2, lam_k2, c_subln_g,
           d_qn_g, d_kn_g, d_sink, w_out_odd, w_mlp1, w_mlp2):
    depth = norm1_g.shape[0]
    n_odd = w_in_odd.shape[0]
    dec_b = c.shape[0]

    rows = -(-(1 + dec_b) // SUBLANES) * SUBLANES
    cond = jnp.concatenate([c_ctx[None, :], c, jnp.zeros((rows - 1 - dec_b, D_MODEL), F32)], axis=0)
    mod = _adaln_call(cond, w_ada, b_ada).reshape(depth, rows, 6, 1, D_MODEL)

    ic = np.arange(GROUP_W)
    ang = 2.0 * np.pi * np.outer(ic, ic) / GROUP_W
    dftc = _bf16_const(np.concatenate([np.cos(ang), -np.sin(ang)], axis=1))
    head_mean = _bf16_const(np.kron(np.eye(2 * LANES // HEAD_DIM), np.ones((HEAD_DIM, HEAD_DIM))) / HEAD_DIM)

    odd_gains = [_odd_gains(c_qn_g[o], c_kn_g[o], d_qn_g[o], d_kn_g[o]) for o in range(n_odd)]
    wts = {
        "norm1_g": norm1_g, "norm2_g": norm2_g,
        "w_in_even": w_in_even, "w_pool": w_pool, "pool_scale": pool_scale,
        "w_fft": w_fft, "w_out_even": w_out_even, "dftc": dftc,
        "w_in_odd": w_in_odd, "odd_gains": odd_gains, "head_mean": head_mean,
        "lamv": jnp.stack([lam_q1, lam_k1, lam_q2, lam_k2], axis=1),
        "c_subln_g": c_subln_g, "d_sink": d_sink, "w_out_odd": w_out_odd,
    }

    bsz, n, _ = x_prompt.shape
    caches = (cache_c_k, cache_c_v, cache_d_k, cache_d_v)
    new_cache = []
    xp, xs = x_prompt, x_sample
    mlp_stacks = (w_mlp1, w_mlp2)
    mlp_w = None
    for l in range(depth):
        pp = _mixer(xp, l, mod, lambda b: 0, wts, None, new_cache)
        ps = _mixer(xs, l, mod, lambda b: 1 + b, wts, caches, new_cache,
                    (mlp_stacks, 0) if l == 0 else None)
        if l == 0:
            mlp_w = ps[2:]
        w_out, out_idx = (wts["w_out_even"], l // 2) if l % 2 == 0 else (wts["w_out_odd"], l // 2)
        flat = lambda a: a.reshape(1, bsz * n, a.shape[-1])
        xp, xs, *mlp_next = _out_mlp_call((flat(xp), flat(pp[0]), flat(pp[1]), 0), (xs, ps[0], ps[1], 1),
                                          mod, l, wts["norm2_g"][l][None, :], w_out, out_idx, *mlp_w,
                                          (mlp_stacks, l + 1) if l + 1 < depth else None)
        mlp_w = mlp_next
        xp = xp.reshape(bsz, n, D_MODEL)
    y_prompt, y_sample = xp, xs
    new_c_k = jnp.stack([nc[0].transpose(0, 4, 1, 2, 3) for nc in new_cache], axis=1)
    new_c_v = jnp.stack([nc[1].reshape(bsz, n, C_HEADS, 2 * HEAD_DIM) for nc in new_cache], axis=1)
    new_d_k = jnp.stack([nc[2].transpose(0, 3, 1, 2) for nc in new_cache], axis=1)
    new_d_v = jnp.stack([nc[3].transpose(0, 3, 1, 2) for nc in new_cache], axis=1)
    return (y_prompt, y_sample, new_c_k, new_c_v, new_d_k, new_d_v)
```

```python
import functools
import math

import numpy as np
import jax
import jax.numpy as jnp
from jax import lax
from jax.experimental import pallas as pl
from jax.experimental.pallas import tpu as pltpu

F32 = jnp.float32
BF16 = jnp.bfloat16

D_MODEL = 1024
HEAD_DIM = 64
LANES = 128
SUBLANES = 8
GRID_W = 64
ROPE_FREQS = HEAD_DIM // 4
ROPE_BASE = 10000.0
EPS = 1e-6
NEG_INF = -1e30
WINDOW = 128
A_WIDTH = D_MODEL // 2
B_WIDTH = D_MODEL // 2
POOL_WINDOWS = (2, 4, 8, 16)
POOL_HALO = 8
N_GROUPS = 4
GROUP_W = A_WIDTH // N_GROUPS
C_HEADS = 4
D_HEADS = 8
D_KV_HEADS = 2
D_GROUP = D_HEADS // D_KV_HEADS
C_W = C_HEADS * 2 * HEAD_DIM
DQ_W = D_HEADS * HEAD_DIM
DKV_W = D_KV_HEADS * HEAD_DIM
DKV_DUP_W = 2 * DKV_W
D_FF = 4 * D_MODEL
ATTN_SCALE = HEAD_DIM ** -0.5
LOG2E = math.log2(math.e)
V_ROWS = LANES + 16
ODD_CV0 = 2 * C_W
ODD_DQ0 = 3 * C_W
ODD_DK0 = ODD_DQ0 + DQ_W
ODD_DV0 = ODD_DK0 + DKV_W
ODD_W = ODD_DV0 + DKV_W
ODD_NORMED_W = 2 * C_W + DQ_W + DKV_W
VMEM_LIMIT = 56 * 1024 * 1024


def _lam_init(layer):
    return 0.8 - 0.6 * math.exp(-0.3 * layer)


def _params(n_axes):
    return pltpu.CompilerParams(dimension_semantics=("arbitrary",) * n_axes,
                                vmem_limit_bytes=VMEM_LIMIT)


def _const_spec(shape):
    nd = len(shape)
    return pl.BlockSpec(shape, lambda *_: (0,) * nd, pipeline_mode=pl.Buffered(1))


def _stacked_spec(shape, idx):
    nd = len(shape)
    return pl.BlockSpec((None,) + tuple(shape), lambda *_: (idx,) + (0,) * nd, pipeline_mode=pl.Buffered(1))


def _side_cast_specs(stacks, layer, n_steps, step_of):
    specs, shapes = [], []
    for w in stacks:
        axis = 2 if w.shape[2] >= w.shape[1] else 1
        nb = 1
        while nb * 2 <= n_steps and w.shape[axis] % (nb * 2 * 2 * LANES) == 0:
            nb *= 2
        block = [None, w.shape[1], w.shape[2]]
        block[axis] //= nb

        def index(*grid, axis=axis, nb=nb):
            blk = jnp.minimum(step_of(*grid), nb - 1)
            return (layer, 0, blk) if axis == 2 else (layer, blk, 0)

        specs.append((pl.BlockSpec(tuple(block), index),
                      pl.BlockSpec(tuple(block[1:]), lambda *grid, index=index: index(*grid)[1:])))
        shapes.append(jax.ShapeDtypeStruct(w.shape[1:], BF16))
    return specs, shapes


def _mod_spec(layer, which, row_of_batch):
    return pl.BlockSpec((None, None, None, 1, D_MODEL),
                        lambda b, i: (layer, row_of_batch(b), which, 0, 0))


def _rms_mod(x, g, sc, sh):
    ms = jnp.mean(x * x, axis=-1, keepdims=True)
    return (x * lax.rsqrt(ms + EPS) * g) * (1.0 + sc) + sh


def _bf16_const(a):
    return jnp.asarray(a, F32).astype(BF16)


def _dot_nt(a, b):
    return lax.dot_general(a, b, (((1,), (1,)), ((), ())), preferred_element_type=F32)


def _dot(a, b):
    return jnp.dot(a, b, preferred_element_type=F32)


def _adaln_kernel(cond_ref, w_ref, b_ref, o_ref):
    cnd = cond_ref[...]
    s = (cnd * jax.nn.sigmoid(cnd)).astype(BF16)
    o_ref[...] = _dot(s, w_ref[...].astype(BF16)) + b_ref[...]


def _adaln_call(cond, w_ada, b_ada):
    depth = w_ada.shape[0]
    rows = cond.shape[0]
    tn = 3 * D_MODEL
    return pl.pallas_call(
        _adaln_kernel,
        grid=(depth, 6 * D_MODEL // tn),
        in_specs=[
            pl.BlockSpec((rows, D_MODEL), lambda l, j: (0, 0)),
            pl.BlockSpec((None, D_MODEL, tn), lambda l, j: (l, 0, j)),
            pl.BlockSpec((None, 1, tn), lambda l, j: (l, 0, j)),
        ],
        out_specs=pl.BlockSpec((None, rows, tn), lambda l, j: (l, 0, j)),
        out_shape=jax.ShapeDtypeStruct((depth, rows, 6 * D_MODEL), F32),
        compiler_params=_params(2),
        name="adaln",
    )(cond, w_ada, b_ada.reshape(depth, 1, 6 * D_MODEL))


def _even_in_kernel(*refs, tm, n, n_cast, dft_scale):
    n_in = 11 if dft_scale is None else 12
    (x_ref, xp_ref, xn_ref, sh_ref, sc_ref, g_ref, win_ref, wpool_ref,
     pscale_ref, dftc_ref, wfft_ref) = refs[:11]
    a_ref, z_ref = refs[n_in + n_cast:n_in + 2 + n_cast]
    for w_ref, o_ref in zip(refs[n_in:n_in + n_cast], refs[n_in + 2 + n_cast:]):
        o_ref[...] = w_ref[...].astype(BF16)
    i = pl.program_id(1)
    n_tiles = n // tm
    g, sc, sh = g_ref[...], sc_ref[...], sh_ref[...]
    seqs = x_ref.shape[0]
    h = _rms_mod(x_ref[...].reshape(seqs * tm, D_MODEL), g, sc, sh).astype(BF16)
    w_in = win_ref[...].astype(BF16)
    u = _dot(h, w_in)
    xh = jnp.concatenate([xp_ref[...], xn_ref[...]], axis=0)
    hh = _rms_mod(xh, g, sc, sh).astype(BF16)
    uh = _dot(hh, w_in[:, :A_WIDTH])
    has_prev = (i > 0).astype(F32)
    has_next = (i < n_tiles - 1).astype(F32)
    rows = tm + 2 * POOL_HALO

    planes = []
    for gi in range(N_GROUPS):
        ub = u[:, A_WIDTH + gi * GROUP_W:A_WIDTH + (gi + 1) * GROUP_W].astype(BF16)
        cs = _dot(ub, dftc_ref[...])
        w_fft = wfft_ref[gi].astype(BF16)
        zr = _dot(cs[:, :GROUP_W].astype(BF16), w_fft)
        zi = _dot(cs[:, GROUP_W:].astype(BF16), w_fft)
        planes.append((zr.astype(BF16), zi.astype(BF16)))
    for q in range(seqs):
        sl = slice(q * tm, (q + 1) * tm)
        zr = jnp.concatenate([p[0][sl] for p in planes], axis=1)
        zi = jnp.concatenate([p[1][sl] for p in planes], axis=1)
        if dft_scale is None:
            z_ref[q, :, :B_WIDTH] = zr
            z_ref[q, :, B_WIDTH:] = zi
        else:
            y = _dot(refs[11][...], jnp.concatenate([zr, zi], axis=0)) * dft_scale
            z_ref[q] = y.astype(BF16)

    t = i * tm + lax.broadcasted_iota(jnp.int32, (tm, 1), 0)
    for q, (gi, w) in ((q, gw) for q in range(seqs) for gw in enumerate(POOL_WINDOWS)):
        cols = slice(gi * GROUP_W, (gi + 1) * GROUP_W)
        uq = u[q * tm:(q + 1) * tm, cols]
        ext = jnp.concatenate([uh[0:POOL_HALO, cols] * has_prev, uq,
                               uh[POOL_HALO:, cols] * has_next], axis=0)
        s = ext
        span = 1
        while span < w:
            s = s + pltpu.roll(s, span, 0)
            span *= 2
        shift = w // 2 - 1
        s = pltpu.roll(s, rows - shift, 0) if shift else s
        acc = s[POOL_HALO:POOL_HALO + tm]
        lo = jnp.maximum(t - w // 2, 0)
        hi = jnp.minimum(t + w // 2, n)
        cnt = (hi - lo).astype(F32)
        pooled = acc / cnt - uq
        y = _dot(pooled.astype(BF16), wpool_ref[gi].astype(BF16)) * pscale_ref[:, cols]
        a_ref[q, :, cols] = y.astype(BF16)


def _even_in_call(x, mod, layer, row_of_batch, g, e, w_in, w_pool, pool_scale, dftc, w_fft, tm, seqs,
                  side_cast=None):
    bsz, n, _ = x.shape
    assert seqs == 1 or tm == n
    hb = tm // POOL_HALO
    tiles = n // tm
    stacks, cast_layer = side_cast if side_cast else ((), 0)
    cast_specs, cast_shapes = _side_cast_specs(stacks, cast_layer, (bsz // seqs) * tiles,
                                               lambda b, i: b * tiles + i)
    fused = tm == n
    if fused:
        k = np.arange(n)
        ang = 2.0 * np.pi * np.outer(k, k) / n
        dft_args = [_bf16_const(np.concatenate([np.cos(ang), np.sin(ang)], axis=1))]
        dft_specs = [_const_spec((n, 2 * n))]
    else:
        dft_args, dft_specs = [], []
    kern = functools.partial(_even_in_kernel, tm=tm, n=n, n_cast=len(stacks),
                             dft_scale=float((n * GROUP_W) ** -0.5) if fused else None)
    second_w = B_WIDTH if fused else 2 * B_WIDTH
    return pl.pallas_call(
        kern,
        grid=(bsz // seqs, n // tm),
        in_specs=[
            pl.BlockSpec((seqs, tm, D_MODEL), lambda b, i: (b, i, 0)),
            pl.BlockSpec((None, POOL_HALO, D_MODEL),
                         lambda b, i: (b * seqs, jnp.maximum(i * hb - 1, 0), 0)),
            pl.BlockSpec((None, POOL_HALO, D_MODEL),
                         lambda b, i: (b * seqs, jnp.minimum((i + 1) * hb, n // POOL_HALO - 1), 0)),
            _mod_spec(layer, 0, row_of_batch),
            _mod_spec(layer, 1, row_of_batch),
            _const_spec((1, D_MODEL)),
            _stacked_spec((D_MODEL, D_MODEL), e),
            _stacked_spec((N_GROUPS, GROUP_W, GROUP_W), e),
            _const_spec((1, A_WIDTH)),
            _const_spec((GROUP_W, 2 * GROUP_W)),
            _stacked_spec((N_GROUPS, GROUP_W, GROUP_W), e),
        ] + dft_specs + [i_spec for i_spec, _ in cast_specs],
        out_specs=[
            pl.BlockSpec((seqs, tm, A_WIDTH), lambda b, i: (b, i, 0)),
            pl.BlockSpec((seqs, tm, second_w), lambda b, i: (b, i, 0)),
        ] + [o_spec for _, o_spec in cast_specs],
        out_shape=[
            jax.ShapeDtypeStruct((bsz, n, A_WIDTH), BF16),
            jax.ShapeDtypeStruct((bsz, n, second_w), BF16),
        ] + cast_shapes,
        compiler_params=_params(2),
        name="even_in",
    )(x, x, x, mod, mod, g, w_in, w_pool, pool_scale, dftc, w_fft, *dft_args, *stacks)


DFT_ROWS = 16
DFT_PITCH = 24


def _pitch_rows(x):
    g, _, w = x.shape
    pad = jnp.zeros((g, DFT_PITCH - DFT_ROWS, w), x.dtype)
    return jnp.concatenate([x, pad], axis=1).reshape(g * DFT_PITCH, w)


def _dft_stage1_kernel(z_ref, m_ref, twr_ref, twi_ref, o_ref, zs_ref, os_ref, *, na):
    nt = 2 * B_WIDTH // LANES

    @pl.when((pl.program_id(0) == 0) & (pl.program_id(1) == 0))
    def _():
        os_ref[...] = jnp.zeros_like(os_ref)

    zf = _pitch_rows(z_ref[...].astype(F32))
    for k in range(nt):
        zs_ref[k] = zf[:, k * LANES:(k + 1) * LANES]
    for bi in range(DFT_ROWS):
        rows = pl.ds(bi, na, stride=DFT_PITCH)
        zb = [zs_ref[k, rows, :].astype(BF16) for k in range(nt)]
        rhs = jnp.concatenate([jnp.concatenate(zb[:nt // 2], axis=1),
                               jnp.concatenate(zb[nt // 2:], axis=1)], axis=0)
        r = _dot(m_ref[...], rhs)
        br, bim = r[:na], r[na:]
        tr, ti = twr_ref[bi], twi_ref[bi]
        out = (br * tr - bim * ti, br * ti + bim * tr)
        for k in range(nt):
            half, kk = divmod(k, nt // 2)
            os_ref[k, rows, :] = out[half][:, kk * LANES:(kk + 1) * LANES]
    of = jnp.concatenate([os_ref[k] for k in range(nt)], axis=1)
    o_ref[...] = of.reshape(na, DFT_PITCH, 2 * B_WIDTH)[:, :DFT_ROWS].astype(BF16)


def _dft_stage2_kernel(b_ref, m_ref, y_ref, ys_ref, *, scale):
    nb = b_ref.shape[1]

    @pl.when((pl.program_id(0) == 0) & (pl.program_id(1) == 0))
    def _():
        ys_ref[...] = jnp.zeros_like(ys_ref)

    for di in range(DFT_ROWS):
        bb = b_ref[di]
        rhs = jnp.concatenate([bb[:, :B_WIDTH], bb[:, B_WIDTH:]], axis=0)
        y = _dot(m_ref[...], rhs) * scale
        for k in range(B_WIDTH // LANES):
            ys_ref[k, pl.ds(di, nb, stride=DFT_PITCH), :] = y[:, k * LANES:(k + 1) * LANES]
    yf = jnp.concatenate([ys_ref[k] for k in range(B_WIDTH // LANES)], axis=1)
    y_ref[...] = yf.reshape(nb, DFT_PITCH, B_WIDTH)[:, :DFT_ROWS].astype(BF16)


def _dft_two_stage_call(z):
    bsz, n, _ = z.shape
    na = 1 << (int(math.log2(n)) // 2)
    nb = n // na
    ia = np.arange(na)
    ib = np.arange(nb)
    ang_a = 2.0 * np.pi * np.outer(ia, ia) / na
    fr, fi = np.cos(ang_a), -np.sin(ang_a)
    m1 = _bf16_const(np.block([[fr, -fi], [fi, fr]]))
    ang_t = 2.0 * np.pi * np.outer(ib, ia) / n
    twr = jnp.asarray(np.cos(ang_t)[:, :, None], F32)
    twi = jnp.asarray(-np.sin(ang_t)[:, :, None], F32)
    ang_b = 2.0 * np.pi * np.outer(ib, ib) / nb
    m2 = _bf16_const(np.concatenate([np.cos(ang_b), np.sin(ang_b)], axis=1))

    s1 = pl.pallas_call(
        functools.partial(_dft_stage1_kernel, na=na),
        grid=(bsz, nb // DFT_ROWS),
        in_specs=[
            pl.BlockSpec((None, na, DFT_ROWS, 2 * B_WIDTH), lambda b, j: (b, 0, j, 0)),
            _const_spec((2 * na, 2 * na)),
            pl.BlockSpec((DFT_ROWS, na, 1), lambda b, j: (j, 0, 0)),
            pl.BlockSpec((DFT_ROWS, na, 1), lambda b, j: (j, 0, 0)),
        ],
        out_specs=pl.BlockSpec((None, na, DFT_ROWS, 2 * B_WIDTH), lambda b, j: (b, 0, j, 0)),
        out_shape=jax.ShapeDtypeStruct((bsz, na, nb, 2 * B_WIDTH), BF16),
        scratch_shapes=[pltpu.VMEM((2 * B_WIDTH // LANES, na * DFT_PITCH, LANES), F32)] * 2,
        compiler_params=_params(2),
        name="dft_stage1",
    )(z.reshape(bsz, na, nb, 2 * B_WIDTH), m1, twr, twi)
    y = pl.pallas_call(
        functools.partial(_dft_stage2_kernel, scale=float((n * GROUP_W) ** -0.5)),
        grid=(bsz, na // DFT_ROWS),
        in_specs=[
            pl.BlockSpec((None, DFT_ROWS, nb, 2 * B_WIDTH), lambda b, j: (b, j, 0, 0)),
            _const_spec((nb, 2 * nb)),
        ],
        out_specs=pl.BlockSpec((None, nb, DFT_ROWS, B_WIDTH), lambda b, j: (b, 0, j, 0)),
        out_shape=jax.ShapeDtypeStruct((bsz, nb, na, B_WIDTH), BF16),
        scratch_shapes=[pltpu.VMEM((B_WIDTH // LANES, nb * DFT_PITCH, LANES), F32)],
        compiler_params=_params(2),
        name="dft_stage2",
    )(s1, m2)
    return y.reshape(bsz, n, B_WIDTH)


FF_CHUNK = 1024


def _out_mlp_kernel(*refs, steps_a, n_cast):
    (xa_ref, p1a_ref, p2a_ref, xb_ref, p1b_ref, p2b_ref, g1_ref, sh_ref, sc_ref, g2_ref,
     ng_ref, wout_ref, w1_ref, w2_ref) = refs[:14]
    oa_ref, ob_ref = refs[14 + n_cast:16 + n_cast]
    for w_ref, o_ref in zip(refs[14:14 + n_cast], refs[16 + n_cast:]):
        o_ref[...] = w_ref[...].astype(BF16)

    def body(x_ref, p1_ref, p2_ref, o_ref):
        half = p1_ref.shape[-1]
        mix = (_dot(p1_ref[...], wout_ref[:half, :].astype(BF16))
               + _dot(p2_ref[...], wout_ref[half:, :].astype(BF16)))
        x1 = x_ref[...] + g1_ref[...] * mix
        h = _rms_mod(x1, ng_ref[...], sc_ref[...], sh_ref[...]).astype(BF16)
        acc = None
        for c in range(D_FF // FF_CHUNK):
            a = _dot(h, w1_ref[:, c * FF_CHUNK:(c + 1) * FF_CHUNK])
            a = jnp.square(jnp.maximum(a, 0.0)).astype(BF16)
            part = _dot(a, w2_ref[c * FF_CHUNK:(c + 1) * FF_CHUNK, :])
            acc = part if acc is None else acc + part
        o_ref[...] = x1 + g2_ref[...] * acc

    @pl.when(pl.program_id(0) < steps_a)
    def _():
        body(xa_ref, p1a_ref, p2a_ref, oa_ref)

    @pl.when(pl.program_id(0) >= steps_a)
    def _():
        body(xb_ref, p1b_ref, p2b_ref, ob_ref)


def _out_mlp_call(set_a, set_b, mod, layer, ng, w_out, out_idx, w1, w2, side_cast=None):
    (xa, p1a, p2a, row_a), (xb, p1b, p2b, row_b) = set_a, set_b
    half = p1a.shape[-1]
    tm = min(xa.shape[1], xb.shape[1], MLP_TILE)
    tiles_a, tiles_b = xa.shape[1] // tm, xb.shape[1] // tm
    steps_a, steps_b = xa.shape[0] * tiles_a, xb.shape[0] * tiles_b

    def pos_a(s):
        sa = jnp.minimum(s, steps_a - 1)
        return sa // tiles_a, sa % tiles_a

    def pos_b(s):
        sb = jnp.maximum(s - steps_a, 0)
        return sb // tiles_b, sb % tiles_b

    tok_a = lambda w: pl.BlockSpec((None, tm, w), lambda s: pos_a(s) + (0,))
    tok_b = lambda w: pl.BlockSpec((None, tm, w), lambda s: pos_b(s) + (0,))
    row = lambda s: jnp.where(s < steps_a, row_a + pos_a(s)[0], row_b + pos_b(s)[0])
    mod_spec = lambda which: pl.BlockSpec((None, None, None, 1, D_MODEL),
                                          lambda s: (layer, row(s), which, 0, 0))
    stacks, cast_layer = side_cast if side_cast else ((), 0)
    cast_specs, cast_shapes = _side_cast_specs(stacks, cast_layer, steps_a + steps_b, lambda s: s)
    return pl.pallas_call(
        functools.partial(_out_mlp_kernel, steps_a=steps_a, n_cast=len(stacks)),
        grid=(steps_a + steps_b,),
        in_specs=[
            tok_a(D_MODEL), tok_a(half), tok_a(half), tok_b(D_MODEL), tok_b(half), tok_b(half),
            mod_spec(2), mod_spec(3), mod_spec(4), mod_spec(5),
            _const_spec((1, D_MODEL)),
            _stacked_spec((2 * half, D_MODEL), out_idx),
            _const_spec((D_MODEL, D_FF)),
            _const_spec((D_FF, D_MODEL)),
        ] + [i_spec for i_spec, _ in cast_specs],
        out_specs=[tok_a(D_MODEL), tok_b(D_MODEL)] + [o_spec for _, o_spec in cast_specs],
        out_shape=[jax.ShapeDtypeStruct(xa.shape, F32), jax.ShapeDtypeStruct(xb.shape, F32)] + cast_shapes,
        compiler_params=_params(1),
        name="out_mlp",
    )(xa, p1a, p2a, xb, p1b, p2b, mod, mod, mod, mod, ng, w_out, w1, w2, *stacks)


def _odd_in_kernel(*refs, rope, emit_cache, tm):
    x_ref, sh_ref, sc_ref, g_ref, win_ref, gain_ref, hm_ref = refs[:7]
    k = 7
    if rope:
        cos_ref, sin_ref = refs[k:k + 2]
        k += 2
    cq_ref, ck_ref, cvt_ref, dq_ref, dk_ref, dvt_ref = refs[k:k + 6]
    k += 6
    if emit_cache:
        nck_ref, ncv_ref, ndk_ref, ndv_ref = refs[k:k + 4]
    subs = x_ref.shape[0] // tm
    lane = lax.broadcasted_iota(jnp.int32, (tm, LANES), 1)
    first_half = (lane % (2 * ROPE_FREQS)) < ROPE_FREQS
    lo = lane < HEAD_DIM
    pad_row = lax.broadcasted_iota(jnp.int32, (V_ROWS - LANES, tm), 0)
    pad = jnp.where(pad_row == 0, 1.0, 0.0).astype(BF16)

    def cache_rows(ref, sub, r):
        rpt = ref.shape[0] // x_ref.shape[0]
        return ref.at[pl.ds(sub * tm * rpt + r, tm, stride=rpt), :]

    def cache_cols(ref, sub, lead, mat):
        n_req = ref.shape[-1]
        for r in range(tm // n_req):
            ref[(sub * (tm // n_req) + r,) + lead] = mat[:, r * n_req:(r + 1) * n_req]

    groups = ((0, ODD_CV0), (ODD_DQ0, ODD_W), (ODD_CV0, ODD_DQ0))
    gain_lo = (0, ODD_CV0)
    normed_w = (ODD_CV0, DQ_W + DKV_W)
    hidden = {}

    def project(item):
        sub, grp = item
        if sub not in hidden:
            x = x_ref[sub * tm:(sub + 1) * tm, :]
            hidden[sub] = _rms_mod(x, g_ref[...], sc_ref[...], sh_ref[...]).astype(BF16)
        lo, hi = groups[grp]
        return _dot(hidden[sub], win_ref[:, lo:hi].astype(BF16))

    def head_norms(p, lo):
        normed = []
        for c0 in range(0, p.shape[1], 2 * LANES):
            wd = min(2 * LANES, p.shape[1] - c0)
            v = p[:, c0:c0 + wd]
            ms = _dot((v * v).astype(BF16), hm_ref[:wd, :wd])
            y = v * lax.rsqrt(ms + EPS) * gain_ref[:, lo + c0:lo + c0 + wd]
            normed += [y[:, k:k + LANES] for k in range(0, wd, LANES)]
        return normed

    def finish(item, p, normed):
        sub, grp = item
        rows = slice(sub * tm, (sub + 1) * tm)

        def roped(y):
            if not rope:
                return y
            rot = jnp.where(first_half, -pltpu.roll(y, LANES - ROPE_FREQS, 1),
                            pltpu.roll(y, ROPE_FREQS, 1))
            return y * cos_ref[rows, :] + rot * sin_ref[rows, :]

        def put(ref, chunks):
            for c, y in enumerate(chunks):
                ref[rows, c * LANES:(c + 1) * LANES] = roped(y).astype(BF16)

        nq = C_W // LANES
        if grp == 0:
            put(cq_ref, normed[:nq])
            put(ck_ref, normed[nq:])
            if emit_cache:
                for c, y in enumerate(normed[nq:]):
                    yt = y.T
                    for half in range(2):
                        cache_cols(nck_ref, sub, (c, half), yt[half * HEAD_DIM:(half + 1) * HEAD_DIM])
        elif grp == 1:
            put(dq_ref, normed[:nq])
            dk = normed[nq]
            y = roped(dk)
            swapped = pltpu.roll(y, HEAD_DIM, 1)
            dk_ref[rows, :LANES] = jnp.where(lo, y, swapped).astype(BF16)
            dk_ref[rows, LANES:] = jnp.where(lo, swapped, y).astype(BF16)
            if emit_cache:
                dkt = dk.T
                for j in range(D_KV_HEADS):
                    cache_cols(ndk_ref, sub, (j,), dkt[j * HEAD_DIM:(j + 1) * HEAD_DIM])
            dvt = p[:, normed_w[1]:].T
            for j in range(D_KV_HEADS):
                vj = dvt[j * HEAD_DIM:(j + 1) * HEAD_DIM]
                dvt_ref[j, :LANES, rows] = jnp.concatenate([vj, vj], axis=0).astype(BF16)
                dvt_ref[j, LANES:, rows] = pad
                if emit_cache:
                    cache_cols(ndv_ref, sub, (j,), vj)
        else:
            cv = p
            for hh in range(C_HEADS):
                cvt_ref[hh, :LANES, rows] = cv[:, hh * LANES:(hh + 1) * LANES].T.astype(BF16)
                cvt_ref[hh, LANES:, rows] = pad
            if emit_cache:
                for hh in range(C_HEADS):
                    cache_rows(ncv_ref, sub, hh)[...] = cv[:, hh * LANES:(hh + 1) * LANES]

    items = [(sub, grp) for sub in range(subs) for grp in range(len(groups))]
    p = project(items[0])
    for idx, item in enumerate(items):
        normed = head_norms(p[:, :normed_w[item[1]]], gain_lo[item[1]]) if item[1] < 2 else None
        nxt = project(items[idx + 1]) if idx + 1 < len(items) else None
        finish(item, p, normed)
        p = nxt


def _odd_in_call(x, mod, layer, row_of_batch, g, w_in, odd_idx, gains, head_mean, rope_tabs, tm, cache_n=None):
    emit_cache = cache_n is not None
    bsz, n, _ = x.shape
    rope = rope_tabs is not None
    subs = 2 if n % (2 * tm) == 0 else 1
    ts = subs * tm
    tok = lambda w: pl.BlockSpec((None, ts, w), lambda b, i: (b, i, 0))
    in_specs = [
        tok(D_MODEL),
        _mod_spec(layer, 0, row_of_batch),
        _mod_spec(layer, 1, row_of_batch),
        _const_spec((1, D_MODEL)),
        _stacked_spec((D_MODEL, ODD_W), odd_idx),
        _const_spec((1, ODD_NORMED_W)),
        _const_spec((2 * LANES, 2 * LANES)),
    ]
    args = [x, mod, mod, g, w_in, gains, head_mean]
    if rope:
        in_specs += [pl.BlockSpec((ts, LANES), lambda b, i: (i, 0))] * 2
        args += list(rope_tabs)
    vt_spec = lambda heads: pl.BlockSpec((None, heads, V_ROWS, ts), lambda b, i: (b, 0, 0, i))
    vt_shape = lambda heads: jax.ShapeDtypeStruct((bsz, heads, V_ROWS, n), BF16)
    tok_shape = lambda w: jax.ShapeDtypeStruct((bsz, n, w), BF16)
    out_specs = [tok(C_W), tok(C_W), vt_spec(C_HEADS), tok(DQ_W), tok(DKV_DUP_W), vt_spec(D_KV_HEADS)]
    out_shape = [tok_shape(C_W), tok_shape(C_W), vt_shape(C_HEADS), tok_shape(DQ_W),
                 tok_shape(DKV_DUP_W), vt_shape(D_KV_HEADS)]
    if emit_cache:
        assert bsz == 1 and tm % cache_n == 0 and n % cache_n == 0
        reqs, rps = n // cache_n, ts // cache_n
        for lead in ((C_HEADS, 2), None, (D_KV_HEADS,), (D_KV_HEADS,)):
            if lead is None:
                out_specs.append(pl.BlockSpec((None, ts * C_HEADS, LANES), lambda b, i: (b, i, 0)))
                out_shape.append(jax.ShapeDtypeStruct((bsz, n * C_HEADS, LANES), F32))
            else:
                zeros = (0,) * (len(lead) + 2)
                out_specs.append(pl.BlockSpec((rps,) + lead + (HEAD_DIM, cache_n),
                                              lambda b, i, zeros=zeros: (i,) + zeros))
                out_shape.append(jax.ShapeDtypeStruct((reqs,) + lead + (HEAD_DIM, cache_n), F32))
    return pl.pallas_call(
        functools.partial(_odd_in_kernel, rope=rope, emit_cache=emit_cache, tm=tm),
        grid=(bsz, n // ts),
        in_specs=in_specs,
        out_specs=out_specs,
        out_shape=out_shape,
        compiler_params=_params(2),
        name="odd_in",
    )(*args)


QK_AHEAD = 3
SINK_AHEAD = 2


def _diff_attn_kernel(*refs, lam_init, tq, kc):
    n_src = (len(refs) - 4) // 2
    q_ref, (lamv_ref, g_ref, o_ref) = refs[0], refs[-3:]
    k_refs, vt_refs = refs[1:1 + n_src], refs[1 + n_src:1 + 2 * n_src]
    chunks = [(src, off) for src in range(n_src) for off in range(0, k_refs[src].shape[1], kc)]
    n_chunks = len(chunks)
    reqs = q_ref.shape[0]
    tiles = q_ref.shape[1] // tq
    heads = q_ref.shape[2] // LANES
    lane = lax.broadcasted_iota(jnp.int32, (tq, LANES), 1)
    lv = lamv_ref[...]
    lam = (jnp.exp(jnp.sum(lv[0:1] * lv[1:2], keepdims=True))
           - jnp.exp(jnp.sum(lv[2:3] * lv[3:4], keepdims=True)) + lam_init)

    qqs = {}

    def score(item):
        r, h, t, c = item
        cols = slice(h * LANES, (h + 1) * LANES)
        if (r, h, t) not in qqs:
            q = q_ref[r, t * tq:(t + 1) * tq, cols]
            zero = jnp.zeros_like(q)
            qqs[r, h, t] = jnp.concatenate([jnp.where(lane < HEAD_DIM, q, zero),
                                            jnp.where(lane >= HEAD_DIM, q, zero)], axis=0)
        src, off = chunks[c]
        return _dot_nt(k_refs[src][r, off:off + kc, cols], qqs[r, h, t])

    def finish(r, h, t, acc, l):
        inv = 1.0 / l
        ot = acc[:, :tq] * inv[:, :tq] - acc[:, tq:] * (inv[:, tq:] * lam)
        o = ot.T
        ms = jnp.mean(o * o, axis=-1, keepdims=True)
        y = o * lax.rsqrt(ms + EPS) * g_ref[...] * (1.0 - lam_init)
        o_ref[r, t * tq:(t + 1) * tq, h * LANES:(h + 1) * LANES] = y.astype(BF16)

    items = [(r, h, t, c) for r in range(reqs) for h in range(heads) for t in range(tiles)
             for c in range(n_chunks)]
    ahead = [score(it) for it in items[:QK_AHEAD]]
    m = acc = None
    for idx, (r, h, t, c) in enumerate(items):
        s = ahead.pop(0)
        if idx + QK_AHEAD < len(items):
            ahead.append(score(items[idx + QK_AHEAD]))
        cm = jnp.max(s, axis=0, keepdims=True)
        m_new = cm if c == 0 else jnp.maximum(m, cm)
        src, off = chunks[c]
        e = jnp.exp2(s - m_new)
        lsum = jnp.sum(e, axis=0, keepdims=True)
        lane0 = r * k_refs[src].shape[1] + off
        pv = _dot(vt_refs[src][h, :LANES, lane0:lane0 + kc], e.astype(BF16))
        if c == 0:
            acc, l = pv, lsum
        else:
            alpha = jnp.exp2(m - m_new)
            acc, l = acc * alpha + pv, l * alpha + lsum
        m = m_new
        if c == n_chunks - 1:
            finish(r, h, t, acc, l)


def _diff_attn_call(q, ks, vts, lamv, subln_g, lam_init, tq, tiles, heads, reqs=1, flat_values=False):
    bsz, n, _ = q.shape
    assert reqs == 1 or flat_values
    kc = next(c for c in (512, 256, 128) if all(k.shape[1] % c == 0 for k in ks))
    hw = heads * LANES
    k_specs = [pl.BlockSpec((reqs, k.shape[1], hw), lambda b, h, i: (b, 0, h)) for k in ks]
    vt_index = (lambda b, h, i: (0, h, 0, b)) if flat_values else (lambda b, h, i: (b, h, 0, 0))
    vt_specs = [pl.BlockSpec((None, heads, V_ROWS, reqs * k.shape[1]), vt_index) for k in ks]
    return pl.pallas_call(
        functools.partial(_diff_attn_kernel, lam_init=lam_init, tq=tq, kc=kc),
        grid=(bsz // reqs, C_HEADS // heads, n // (tq * tiles)),
        in_specs=[pl.BlockSpec((reqs, tiles * tq, hw), lambda b, h, i: (b, i, h))]
        + k_specs + vt_specs + [_const_spec((4, HEAD_DIM)), _const_spec((1, LANES))],
        out_specs=pl.BlockSpec((reqs, tiles * tq, hw), lambda b, h, i: (b, i, h)),
        out_shape=jax.ShapeDtypeStruct((bsz, n, C_W), BF16),
        compiler_params=_params(3),
        name="diff_attn",
    )(q, *ks, *vts, lamv, subln_g)


def _sink_attn_kernel(*refs, windowed, tq, nblk):
    if windowed:
        (q_ref, ck_ref, kp_ref, km_ref, kn_ref, cvt_ref, vp_ref, vm_ref, vn_ref,
         bias_ref, sink_ref, o_ref) = refs
        kband = jnp.concatenate([kp_ref[...], km_ref[...], kn_ref[...]], axis=0)
        vtband = jnp.concatenate([vp_ref[...], vm_ref[...], vn_ref[...]], axis=1)
    else:
        q_ref, k_ref, vt_ref, sink_ref, o_ref = refs
    reqs = q_ref.shape[0]
    subs = q_ref.shape[1] // tq
    groups = q_ref.shape[2] // (2 * LANES)
    lane = lax.broadcasted_iota(jnp.int32, (tq, LANES), 1)
    lo = lane < HEAD_DIM

    def scores_of(stream):
        r, g, sub = stream
        qd = q_ref[r, sub * tq:(sub + 1) * tq, g * 2 * LANES:(g + 1) * 2 * LANES]
        parts = []
        for c in range(2):
            ch = qd[:, c * LANES:(c + 1) * LANES]
            zero = jnp.zeros_like(ch)
            parts += [jnp.where(lo, ch, zero), jnp.where(lo, zero, ch)]
        qq = jnp.concatenate(parts, axis=0)
        if not windowed:
            return [_dot_nt(k_ref[r, :, g * LANES:(g + 1) * LANES], qq)]
        blk = pl.program_id(2) * subs + sub
        variant = jnp.where(blk == 0, 0, jnp.where(blk == nblk - 1, 2, 1))
        band = _dot_nt(kband[sub * tq:(sub + 3) * tq, :], qq) + bias_ref[variant]
        return [_dot_nt(ck_ref[...], qq), band]

    def finish(stream, scores):
        r, g, sub = stream
        values = ([cvt_ref[...], vtband[:, sub * tq:(sub + 3) * tq]] if windowed
                  else [vt_ref[g, :, r * tq:(r + 1) * tq]])
        sink = sink_ref[g]
        m = sink
        for s in scores:
            m = jnp.maximum(m, jnp.max(s, axis=0, keepdims=True))
        acc = None
        for s, vt in zip(scores, values):
            pv = _dot(vt, jnp.exp2(s - m).astype(BF16))
            acc = pv if acc is None else acc + pv
        inv = 1.0 / (acc[LANES:LANES + 1, :] + jnp.exp2(sink - m))
        ot = acc[:LANES, :] * inv
        for c in range(2):
            o0 = ot[:, (2 * c) * tq:(2 * c + 1) * tq].T
            o1 = ot[:, (2 * c + 1) * tq:(2 * c + 2) * tq].T
            cols = slice((2 * g + c) * LANES, (2 * g + c + 1) * LANES)
            o_ref[r, sub * tq:(sub + 1) * tq, cols] = jnp.where(lo, o0, o1).astype(BF16)

    streams = [(r, g, sub) for r in range(reqs) for g in range(groups) for sub in range(subs)]
    ahead = [scores_of(st) for st in streams[:SINK_AHEAD]]
    for idx, st in enumerate(streams):
        cur = ahead.pop(0)
        if idx + SINK_AHEAD < len(streams):
            ahead.append(scores_of(streams[idx + SINK_AHEAD]))
        finish(st, cur)


def _window_bias(n, tq):
    nblk = n // tq
    kk = np.arange(3 * tq)[:, None]
    qi = np.arange(tq)[None, :]
    band_ok = np.abs(qi + WINDOW - kk) <= WINDOW
    out = []
    for blk in (0, 1, nblk - 1):
        jpos = blk * tq - WINDOW + kk
        ok = band_ok & (jpos >= 0) & (jpos < n)
        out.append(np.tile(np.where(ok, 0.0, NEG_INF), (1, D_GROUP)))
    return jnp.asarray(np.stack(out), F32)


def _sink_attn_call(q, k, vt, sink_rows, ctx_k=None, ctx_vt=None, reqs=1, flat_values=False):
    bsz, n, _ = q.shape
    windowed = ctx_k is not None
    assert reqs == 1 or (flat_values and not windowed)
    tq = WINDOW if windowed else n
    nblk = n // tq
    subs = next(s for s in (16, 8, 4, 2, 1) if nblk % s == 0) if windowed else 1
    steps = nblk // subs
    groups = 1 if windowed else D_KV_HEADS
    q_spec = pl.BlockSpec((reqs, subs * tq, groups * 2 * LANES), lambda b, j, i: (b, i, j))
    sink_spec = pl.BlockSpec((groups, 1, D_GROUP * tq), lambda b, j, i: (j, 0, 0))
    if windowed:
        assert nblk >= 3, "first / interior / last mask variants need three query blocks"
        n_ctx = ctx_k.shape[1]
        prev = lambda i: jnp.maximum(i * subs - 1, 0)
        nxt = lambda i: jnp.minimum((i + 1) * subs, nblk - 1)
        in_specs = [
            q_spec,
            pl.BlockSpec((None, n_ctx, LANES), lambda b, j, i: (b, 0, j)),
            pl.BlockSpec((None, tq, LANES), lambda b, j, i: (b, prev(i), j)),
            pl.BlockSpec((None, subs * tq, LANES), lambda b, j, i: (b, i, j)),
            pl.BlockSpec((None, tq, LANES), lambda b, j, i: (b, nxt(i), j)),
            pl.BlockSpec((None, None, V_ROWS, n_ctx), lambda b, j, i: (b, j, 0, 0)),
            pl.BlockSpec((None, None, V_ROWS, tq), lambda b, j, i: (b, j, 0, prev(i))),
            pl.BlockSpec((None, None, V_ROWS, subs * tq), lambda b, j, i: (b, j, 0, i)),
            pl.BlockSpec((None, None, V_ROWS, tq), lambda b, j, i: (b, j, 0, nxt(i))),
            _const_spec((3, 3 * tq, D_GROUP * tq)),
            sink_spec,
        ]
        args = (q, ctx_k, k, k, k, ctx_vt, vt, vt, vt, _window_bias(n, tq), sink_rows)
    else:
        in_specs = [
            q_spec,
            pl.BlockSpec((reqs, n, groups * LANES), lambda b, j, i: (b, 0, j)),
            pl.BlockSpec((None, groups, V_ROWS, reqs * n),
                         (lambda b, j, i: (0, j, 0, b)) if flat_values else (lambda b, j, i: (b, j, 0, 0))),
            sink_spec,
        ]
        args = (q, k, vt, sink_rows)
    return pl.pallas_call(
        functools.partial(_sink_attn_kernel, windowed=windowed, tq=tq, nblk=nblk),
        grid=(bsz // reqs, D_KV_HEADS // groups, steps),
        in_specs=in_specs,
        out_specs=q_spec,
        out_shape=jax.ShapeDtypeStruct((bsz, n, DQ_W), BF16),
        compiler_params=_params(3),
        name="sink_attn",
    )(*args)


def _dup_heads(a):
    lead = a.shape[:-1]
    a = a.reshape(lead + (D_KV_HEADS, 1, HEAD_DIM))
    return jnp.broadcast_to(a, lead + (D_KV_HEADS, 2, HEAD_DIM)).reshape(lead + (DKV_DUP_W,))


def _odd_gains(c_qn_g, c_kn_g, d_qn_g, d_kn_g):
    return jnp.concatenate([
        jnp.tile(c_qn_g * (ATTN_SCALE * LOG2E), C_W // HEAD_DIM),
        jnp.tile(c_kn_g, C_W // HEAD_DIM),
        jnp.tile(d_qn_g * (ATTN_SCALE * LOG2E), DQ_W // HEAD_DIM),
        jnp.tile(d_kn_g, DKV_W // HEAD_DIM),
    ])[None, :]


def _value_rows(v):
    bsz, nk, w = v.shape
    vt = v.reshape(bsz, nk, w // LANES, LANES).transpose(0, 2, 3, 1)
    pad = jnp.zeros((bsz, w // LANES, V_ROWS - LANES, nk), v.dtype).at[:, :, 0, :].set(1)
    return jnp.concatenate([vt, pad], axis=2)


def _rope_tables(n):
    rows = n // GRID_W
    row = np.repeat(np.arange(rows), GRID_W).astype(np.float64)
    col = np.tile(np.arange(GRID_W), rows).astype(np.float64)
    freqs = ROPE_BASE ** (-np.arange(ROPE_FREQS, dtype=np.float64) / ROPE_FREQS)
    ang = np.stack([row[:, None] * freqs, col[:, None] * freqs], axis=1)
    ang = np.concatenate([ang, ang], axis=-1).reshape(n, HEAD_DIM)
    ang = np.concatenate([ang, ang], axis=-1)
    return jnp.asarray(np.cos(ang), F32), jnp.asarray(np.sin(ang), F32)


MLP_TILE = 512


def _token_tile(n):
    return min(n, 512)


def _mixer(x, l, mod, row_of_batch, wts, caches, new_cache, side_cast=None):
    bsz, n, _ = x.shape
    decode = caches is not None
    g1n = wts["norm1_g"][l][None, :]
    if l % 2 == 0:
        e = l // 2
        a, z, *copies = _even_in_call(x, mod, l, row_of_batch, g1n, e, wts["w_in_even"], wts["w_pool"],
                                      wts["pool_scale"][e][None, :], wts["dftc"], wts["w_fft"], min(n, 1024),
                                      1 if decode or n > 1024 else next(q for q in (4, 2, 1) if bsz % q == 0),
                                      side_cast)
        y = _dft_two_stage_call(z) if z.shape[-1] == 2 * B_WIDTH else z
        return (a, y) + tuple(copies)
    o = l // 2
    lam_init = _lam_init(l)
    rope_tabs = _rope_tables(n) if decode else None
    xo = x if decode else x.reshape(1, bsz * n, D_MODEL)
    outs = _odd_in_call(xo, mod, l, row_of_batch, g1n, wts["w_in_odd"], o, wts["odd_gains"][o],
                        wts["head_mean"], rope_tabs, _token_tile(xo.shape[1]),
                        None if decode else n)
    if not decode:
        outs = [a if a.ndim == 4 else a.reshape(bsz, n, a.shape[-1]) for a in outs[:6]] + list(outs[6:])
    cq, ck, cvt, dq, dkk, dvt = outs[:6]
    if decode:
        c_k, c_v, d_k, d_v = caches
        lc = c_k.shape[2]
        ks = [c_k[:, o].reshape(bsz, lc, C_W).astype(BF16), ck]
        vts = [_value_rows(c_v[:, o].reshape(bsz, lc, C_W).astype(BF16)), cvt]
        ctx_k = _dup_heads(d_k[:, o].reshape(bsz, lc, DKV_W)).astype(BF16)
        ctx_vt = _value_rows(_dup_heads(d_v[:, o].reshape(bsz, lc, DKV_W)).astype(BF16))
        tq = min(n, 256)
    else:
        new_cache.append(outs[6:])
        ks, vts = [ck], [cvt]
        ctx_k = ctx_vt = None
        tq = n
    tiles = next(t for t in (4, 2, 1) if n % (t * tq) == 0)
    heads = 1 if decode else C_HEADS
    c_out = _diff_attn_call(cq, ks, vts, wts["lamv"][o],
                            wts["c_subln_g"][o][None, :], lam_init, tq, tiles, heads,
                            1 if decode else next(r for r in (4, 2, 1) if bsz % r == 0), not decode)
    sink_tq = WINDOW if decode else n
    sink_rows = jnp.repeat(wts["d_sink"][o].reshape(D_KV_HEADS, D_GROUP) * LOG2E, sink_tq,
                           axis=-1)[:, None, :]
    d_out = _sink_attn_call(dq, dkk, dvt, sink_rows, ctx_k, ctx_vt,
                            1 if decode else next(r for r in (4, 2, 1) if bsz % r == 0), not decode)
    return c_out, d_out


def kernel(x_prompt, x_sample, c, cache_c_k, cache_c_v, cache_d_k, cache_d_v, c_ctx,
           norm1_g, norm2_g, w_ada, b_ada, w_in_even, w_pool, pool_scale, w_fft, w_out_even,
           w_in_odd, c_qn_g, c_kn_g, lam_q1, lam_k1, lam_q2, lam_k2, c_subln_g,
           d_qn_g, d_kn_g, d_sink, w_out_odd, w_mlp1, w_mlp2):
    depth = norm1_g.shape[0]
    n_odd = w_in_odd.shape[0]
    dec_b = c.shape[0]

    rows = -(-(1 + dec_b) // SUBLANES) * SUBLANES
    cond = jnp.concatenate([c_ctx[None, :], c, jnp.zeros((rows - 1 - dec_b, D_MODEL), F32)], axis=0)
    mod = _adaln_call(cond, w_ada, b_ada).reshape(depth, rows, 6, 1, D_MODEL)

    ic = np.arange(GROUP_W)
    ang = 2.0 * np.pi * np.outer(ic, ic) / GROUP_W
    dftc = _bf16_const(np.concatenate([np.cos(ang), -np.sin(ang)], axis=1))
    head_mean = _bf16_const(np.kron(np.eye(2 * LANES // HEAD_DIM), np.ones((HEAD_DIM, HEAD_DIM))) / HEAD_DIM)

    odd_gains = [_odd_gains(c_qn_g[o], c_kn_g[o], d_qn_g[o], d_kn_g[o]) for o in range(n_odd)]
    wts = {
        "norm1_g": norm1_g, "norm2_g": norm2_g,
        "w_in_even": w_in_even, "w_pool": w_pool, "pool_scale": pool_scale,
        "w_fft": w_fft, "w_out_even": w_out_even, "dftc": dftc,
        "w_in_odd": w_in_odd, "odd_gains": odd_gains, "head_mean": head_mean,
        "lamv": jnp.stack([lam_q1, lam_k1, lam_q2, lam_k2], axis=1),
        "c_subln_g": c_subln_g, "d_sink": d_sink, "w_out_odd": w_out_odd,
    }

    bsz, n, _ = x_prompt.shape
    caches = (cache_c_k, cache_c_v, cache_d_k, cache_d_v)
    new_cache = []
    xp, xs = x_prompt, x_sample
    mlp_stacks = (w_mlp1, w_mlp2)
    mlp_w = None
    for l in range(depth):
        pp = _mixer(xp, l, mod, lambda b: 0, wts, None, new_cache)
        ps = _mixer(xs, l, mod, lambda b: 1 + b, wts, caches, new_cache,
                    (mlp_stacks, 0) if l == 0 else None)
        if l == 0:
            mlp_w = ps[2:]
        w_out, out_idx = (wts["w_out_even"], l // 2) if l % 2 == 0 else (wts["w_out_odd"], l // 2)
        flat = lambda a: a.reshape(1, bsz * n, a.shape[-1])
        xp, xs, *mlp_next = _out_mlp_call((flat(xp), flat(pp[0]), flat(pp[1]), 0), (xs, ps[0], ps[1], 1),
                                          mod, l, wts["norm2_g"][l][None, :], w_out, out_idx, *mlp_w,
                                          (mlp_stacks, l + 1) if l + 1 < depth else None)
        mlp_w = mlp_next
        xp = xp.reshape(bsz, n, D_MODEL)
    y_prompt, y_sample = xp, xs
    new_c_k = jnp.stack([nc[0].transpose(0, 4, 1, 2, 3) for nc in new_cache], axis=1)
    new_c_v = jnp.stack([nc[1].reshape(bsz, n, C_HEADS, 2 * HEAD_DIM) for nc in new_cache], axis=1)
    new_d_k = jnp.stack([nc[2].transpose(0, 3, 1, 2) for nc in new_cache], axis=1)
    new_d_v = jnp.stack([nc[3].transpose(0, 3, 1, 2) for nc in new_cache], axis=1)
    return (y_prompt, y_sample, new_c_k, new_c_v, new_d_k, new_d_v)
```

```python
import functools
import math

import numpy as np
import jax
import jax.numpy as jnp
from jax import lax
from jax.experimental import pallas as pl
from jax.experimental.pallas import tpu as pltpu

F32 = jnp.float32
BF16 = jnp.bfloat16

D_MODEL = 1024
HEAD_DIM = 64
LANES = 128
SUBLANES = 8
GRID_W = 64
ROPE_FREQS = HEAD_DIM // 4
ROPE_BASE = 10000.0
EPS = 1e-6
NEG_INF = -1e30
WINDOW = 128
A_WIDTH = D_MODEL // 2
B_WIDTH = D_MODEL // 2
POOL_WINDOWS = (2, 4, 8, 16)
POOL_HALO = 8
N_GROUPS = 4
GROUP_W = A_WIDTH // N_GROUPS
C_HEADS = 4
D_HEADS = 8
D_KV_HEADS = 2
D_GROUP = D_HEADS // D_KV_HEADS
C_W = C_HEADS * 2 * HEAD_DIM
DQ_W = D_HEADS * HEAD_DIM
DKV_W = D_KV_HEADS * HEAD_DIM
DKV_DUP_W = 2 * DKV_W
D_FF = 4 * D_MODEL
ATTN_SCALE = HEAD_DIM ** -0.5
LOG2E = math.log2(math.e)
V_ROWS = LANES + 16
ODD_CV0 = 2 * C_W
ODD_DQ0 = 3 * C_W
ODD_DK0 = ODD_DQ0 + DQ_W
ODD_DV0 = ODD_DK0 + DKV_W
ODD_W = ODD_DV0 + DKV_W
ODD_NORMED_W = 2 * C_W + DQ_W + DKV_W
VMEM_LIMIT = 56 * 1024 * 1024


def _lam_init(layer):
    return 0.8 - 0.6 * math.exp(-0.3 * layer)


def _params(n_axes):
    return pltpu.CompilerParams(dimension_semantics=("arbitrary",) * n_axes,
                                vmem_limit_bytes=VMEM_LIMIT)


def _const_spec(shape):
    nd = len(shape)
    return pl.BlockSpec(shape, lambda *_: (0,) * nd, pipeline_mode=pl.Buffered(1))


def _stacked_spec(shape, idx):
    nd = len(shape)
    return pl.BlockSpec((None,) + tuple(shape), lambda *_: (idx,) + (0,) * nd, pipeline_mode=pl.Buffered(1))


def _side_cast_specs(stacks, layer, n_steps, step_of):
    specs, shapes = [], []
    for w in stacks:
        axis = 2 if w.shape[2] >= w.shape[1] else 1
        nb = 1
        while nb * 2 <= n_steps and w.shape[axis] % (nb * 2 * 2 * LANES) == 0:
            nb *= 2
        block = [None, w.shape[1], w.shape[2]]
        block[axis] //= nb

        def index(*grid, axis=axis, nb=nb):
            blk = jnp.minimum(step_of(*grid), nb - 1)
            return (layer, 0, blk) if axis == 2 else (layer, blk, 0)

        specs.append((pl.BlockSpec(tuple(block), index),
                      pl.BlockSpec(tuple(block[1:]), lambda *grid, index=index: index(*grid)[1:])))
        shapes.append(jax.ShapeDtypeStruct(w.shape[1:], BF16))
    return specs, shapes


def _mod_spec(layer, which, row_of_batch):
    return pl.BlockSpec((None, None, None, 1, D_MODEL),
                        lambda b, i: (layer, row_of_batch(b), which, 0, 0))


def _rms_mod(x, g, sc, sh):
    ms = jnp.mean(x * x, axis=-1, keepdims=True)
    return (x * lax.rsqrt(ms + EPS) * g) * (1.0 + sc) + sh


def _bf16_const(a):
    return jnp.asarray(a, F32).astype(BF16)


def _dot_nt(a, b):
    return lax.dot_general(a, b, (((1,), (1,)), ((), ())), preferred_element_type=F32)


def _dot(a, b):
    return jnp.dot(a, b, preferred_element_type=F32)


def _adaln_kernel(cond_ref, w_ref, b_ref, o_ref):
    cnd = cond_ref[...]
    s = (cnd * jax.nn.sigmoid(cnd)).astype(BF16)
    o_ref[...] = _dot(s, w_ref[...].astype(BF16)) + b_ref[...]


def _adaln_call(cond, w_ada, b_ada):
    depth = w_ada.shape[0]
    rows = cond.shape[0]
    tn = 3 * D_MODEL
    return pl.pallas_call(
        _adaln_kernel,
        grid=(depth, 6 * D_MODEL // tn),
        in_specs=[
            pl.BlockSpec((rows, D_MODEL), lambda l, j: (0, 0)),
            pl.BlockSpec((None, D_MODEL, tn), lambda l, j: (l, 0, j)),
            pl.BlockSpec((None, 1, tn), lambda l, j: (l, 0, j)),
        ],
        out_specs=pl.BlockSpec((None, rows, tn), lambda l, j: (l, 0, j)),
        out_shape=jax.ShapeDtypeStruct((depth, rows, 6 * D_MODEL), F32),
        compiler_params=_params(2),
        name="adaln",
    )(cond, w_ada, b_ada.reshape(depth, 1, 6 * D_MODEL))


def _even_in_kernel(*refs, tm, n, n_cast, dft_scale):
    n_in = 11 if dft_scale is None else 12
    (x_ref, xp_ref, xn_ref, sh_ref, sc_ref, g_ref, win_ref, wpool_ref,
     pscale_ref, dftc_ref, wfft_ref) = refs[:11]
    a_ref, z_ref = refs[n_in + n_cast:n_in + 2 + n_cast]
    for w_ref, o_ref in zip(refs[n_in:n_in + n_cast], refs[n_in + 2 + n_cast:]):
        o_ref[...] = w_ref[...].astype(BF16)
    i = pl.program_id(1)
    n_tiles = n // tm
    g, sc, sh = g_ref[...], sc_ref[...], sh_ref[...]
    seqs = x_ref.shape[0]
    h = _rms_mod(x_ref[...].reshape(seqs * tm, D_MODEL), g, sc, sh).astype(BF16)
    w_in = win_ref[...].astype(BF16)
    u = _dot(h, w_in)
    xh = jnp.concatenate([xp_ref[...], xn_ref[...]], axis=0)
    hh = _rms_mod(xh, g, sc, sh).astype(BF16)
    uh = _dot(hh, w_in[:, :A_WIDTH])
    has_prev = (i > 0).astype(F32)
    has_next = (i < n_tiles - 1).astype(F32)
    rows = tm + 2 * POOL_HALO

    planes = []
    for gi in range(N_GROUPS):
        ub = u[:, A_WIDTH + gi * GROUP_W:A_WIDTH + (gi + 1) * GROUP_W].astype(BF16)
        cs = _dot(ub, dftc_ref[...])
        w_fft = wfft_ref[gi].astype(BF16)
        zr = _dot(cs[:, :GROUP_W].astype(BF16), w_fft)
        zi = _dot(cs[:, GROUP_W:].astype(BF16), w_fft)
        planes.append((zr.astype(BF16), zi.astype(BF16)))
    for q in range(seqs):
        sl = slice(q * tm, (q + 1) * tm)
        zr = jnp.concatenate([p[0][sl] for p in planes], axis=1)
        zi = jnp.concatenate([p[1][sl] for p in planes], axis=1)
        if dft_scale is None:
            z_ref[q, :, :B_WIDTH] = zr
            z_ref[q, :, B_WIDTH:] = zi
        else:
            y = _dot(refs[11][...], jnp.concatenate([zr, zi], axis=0)) * dft_scale
            z_ref[q] = y.astype(BF16)

    t = i * tm + lax.broadcasted_iota(jnp.int32, (tm, 1), 0)
    for q, (gi, w) in ((q, gw) for q in range(seqs) for gw in enumerate(POOL_WINDOWS)):
        cols = slice(gi * GROUP_W, (gi + 1) * GROUP_W)
        uq = u[q * tm:(q + 1) * tm, cols]
        ext = jnp.concatenate([uh[0:POOL_HALO, cols] * has_prev, uq,
                               uh[POOL_HALO:, cols] * has_next], axis=0)
        s = ext
        span = 1
        while span < w:
            s = s + pltpu.roll(s, span, 0)
            span *= 2
        shift = w // 2 - 1
        s = pltpu.roll(s, rows - shift, 0) if shift else s
        acc = s[POOL_HALO:POOL_HALO + tm]
        lo = jnp.maximum(t - w // 2, 0)
        hi = jnp.minimum(t + w // 2, n)
        cnt = (hi - lo).astype(F32)
        pooled = acc / cnt - uq
        y = _dot(pooled.astype(BF16), wpool_ref[gi].astype(BF16)) * pscale_ref[:, cols]
        a_ref[q, :, cols] = y.astype(BF16)


def _even_in_call(x, mod, layer, row_of_batch, g, e, w_in, w_pool, pool_scale, dftc, w_fft, tm, seqs,
                  side_cast=None):
    bsz, n, _ = x.shape
    assert seqs == 1 or tm == n
    hb = tm // POOL_HALO
    tiles = n // tm
    stacks, cast_layer = side_cast if side_cast else ((), 0)
    cast_specs, cast_shapes = _side_cast_specs(stacks, cast_layer, (bsz // seqs) * tiles,
                                               lambda b, i: b * tiles + i)
    fused = tm == n
    if fused:
        k = np.arange(n)
        ang = 2.0 * np.pi * np.outer(k, k) / n
        dft_args = [_bf16_const(np.concatenate([np.cos(ang), np.sin(ang)], axis=1))]
        dft_specs = [_const_spec((n, 2 * n))]
    else:
        dft_args, dft_specs = [], []
    kern = functools.partial(_even_in_kernel, tm=tm, n=n, n_cast=len(stacks),
                             dft_scale=float((n * GROUP_W) ** -0.5) if fused else None)
    second_w = B_WIDTH if fused else 2 * B_WIDTH
    return pl.pallas_call(
        kern,
        grid=(bsz // seqs, n // tm),
        in_specs=[
            pl.BlockSpec((seqs, tm, D_MODEL), lambda b, i: (b, i, 0)),
            pl.BlockSpec((None, POOL_HALO, D_MODEL),
                         lambda b, i: (b * seqs, jnp.maximum(i * hb - 1, 0), 0)),
            pl.BlockSpec((None, POOL_HALO, D_MODEL),
                         lambda b, i: (b * seqs, jnp.minimum((i + 1) * hb, n // POOL_HALO - 1), 0)),
            _mod_spec(layer, 0, row_of_batch),
            _mod_spec(layer, 1, row_of_batch),
            _const_spec((1, D_MODEL)),
            _stacked_spec((D_MODEL, D_MODEL), e),
            _stacked_spec((N_GROUPS, GROUP_W, GROUP_W), e),
            _const_spec((1, A_WIDTH)),
            _const_spec((GROUP_W, 2 * GROUP_W)),
            _stacked_spec((N_GROUPS, GROUP_W, GROUP_W), e),
        ] + dft_specs + [i_spec for i_spec, _ in cast_specs],
        out_specs=[
            pl.BlockSpec((seqs, tm, A_WIDTH), lambda b, i: (b, i, 0)),
            pl.BlockSpec((seqs, tm, second_w), lambda b, i: (b, i, 0)),
        ] + [o_spec for _, o_spec in cast_specs],
        out_shape=[
            jax.ShapeDtypeStruct((bsz, n, A_WIDTH), BF16),
            jax.ShapeDtypeStruct((bsz, n, second_w), BF16),
        ] + cast_shapes,
        compiler_params=_params(2),
        name="even_in",
    )(x, x, x, mod, mod, g, w_in, w_pool, pool_scale, dftc, w_fft, *dft_args, *stacks)


DFT_ROWS = 16
DFT_PITCH = 24


def _pitch_rows(x):
    g, _, w = x.shape
    pad = jnp.zeros((g, DFT_PITCH - DFT_ROWS, w), x.dtype)
    return jnp.concatenate([x, pad], axis=1).reshape(g * DFT_PITCH, w)


def _dft_kernel(z_ref, m1_ref, twr_ref, twi_ref, m2_ref, y_ref, zs_ref, os_ref, ys_ref, bs_ref, *, na, nb, scale):
    j = pl.program_id(1)
    n_s1 = nb // DFT_ROWS
    nt = 2 * B_WIDTH // LANES

    @pl.when((pl.program_id(0) == 0) & (j == 0))
    def _():
        os_ref[...] = jnp.zeros_like(os_ref)
        ys_ref[...] = jnp.zeros_like(ys_ref)

    @pl.when(j < n_s1)
    def _():
        zf = _pitch_rows(z_ref[...].astype(F32))
        for k in range(nt):
            zs_ref[k] = zf[:, k * LANES:(k + 1) * LANES]
        for bi in range(DFT_ROWS):
            rows = pl.ds(bi, na, stride=DFT_PITCH)
            zb = [zs_ref[k, rows, :].astype(BF16) for k in range(nt)]
            rhs = jnp.concatenate([jnp.concatenate(zb[:nt // 2], axis=1),
                                   jnp.concatenate(zb[nt // 2:], axis=1)], axis=0)
            r = _dot(m1_ref[...], rhs)
            br, bim = r[:na], r[na:]
            tr, ti = twr_ref[bi], twi_ref[bi]
            out = (br * tr - bim * ti, br * ti + bim * tr)
            for k in range(nt):
                half, kk = divmod(k, nt // 2)
                os_ref[k, rows, :] = out[half][:, kk * LANES:(kk + 1) * LANES]
        of = jnp.concatenate([os_ref[k] for k in range(nt)], axis=1)
        bs_ref[j] = of.reshape(na, DFT_PITCH, 2 * B_WIDTH)[:, :DFT_ROWS].astype(BF16)

    @pl.when(j >= n_s1)
    def _():
        for di in range(DFT_ROWS):
            d = (j - n_s1) * DFT_ROWS + di
            bb = jnp.concatenate([bs_ref[jj, d] for jj in range(n_s1)], axis=0)
            rhs = jnp.concatenate([bb[:, :B_WIDTH], bb[:, B_WIDTH:]], axis=0)
            y = _dot(m2_ref[...], rhs) * scale
            for k in range(B_WIDTH // LANES):
                ys_ref[k, pl.ds(di, nb, stride=DFT_PITCH), :] = y[:, k * LANES:(k + 1) * LANES]
        yf = jnp.concatenate([ys_ref[k] for k in range(B_WIDTH // LANES)], axis=1)
        y_ref[...] = yf.reshape(nb, DFT_PITCH, B_WIDTH)[:, :DFT_ROWS].astype(BF16)


def _dft_two_stage_call(z):
    bsz, n, _ = z.shape
    na = 1 << (int(math.log2(n)) // 2)
    nb = n // na
    ia = np.arange(na)
    ib = np.arange(nb)
    ang_a = 2.0 * np.pi * np.outer(ia, ia) / na
    fr, fi = np.cos(ang_a), -np.sin(ang_a)
    m1 = _bf16_const(np.block([[fr, -fi], [fi, fr]]))
    ang_t = 2.0 * np.pi * np.outer(ib, ia) / n
    twr = jnp.asarray(np.cos(ang_t)[:, :, None], F32)
    twi = jnp.asarray(-np.sin(ang_t)[:, :, None], F32)
    ang_b = 2.0 * np.pi * np.outer(ib, ib) / nb
    m2 = _bf16_const(np.concatenate([np.cos(ang_b), np.sin(ang_b)], axis=1))

    n_s1, n_s2 = nb // DFT_ROWS, na // DFT_ROWS
    s1_blk = lambda j: jnp.minimum(j, n_s1 - 1)
    s2_blk = lambda j: jnp.maximum(j - n_s1, 0)
    y = pl.pallas_call(
        functools.partial(_dft_kernel, na=na, nb=nb, scale=float((n * GROUP_W) ** -0.5)),
        grid=(bsz, n_s1 + n_s2),
        in_specs=[
            pl.BlockSpec((None, na, DFT_ROWS, 2 * B_WIDTH), lambda b, j: (b, 0, s1_blk(j), 0)),
            _const_spec((2 * na, 2 * na)),
            pl.BlockSpec((DFT_ROWS, na, 1), lambda b, j: (s1_blk(j), 0, 0)),
            pl.BlockSpec((DFT_ROWS, na, 1), lambda b, j: (s1_blk(j), 0, 0)),
            _const_spec((nb, 2 * nb)),
        ],
        out_specs=pl.BlockSpec((None, nb, DFT_ROWS, B_WIDTH), lambda b, j: (b, 0, s2_blk(j), 0)),
        out_shape=jax.ShapeDtypeStruct((bsz, nb, na, B_WIDTH), BF16),
        scratch_shapes=[
            pltpu.VMEM((2 * B_WIDTH // LANES, na * DFT_PITCH, LANES), F32),
            pltpu.VMEM((2 * B_WIDTH // LANES, na * DFT_PITCH, LANES), F32),
            pltpu.VMEM((B_WIDTH // LANES, nb * DFT_PITCH, LANES), F32),
            pltpu.VMEM((n_s1, na, DFT_ROWS, 2 * B_WIDTH), BF16),
        ],
        compiler_params=_params(2),
        name="dft_two_stage",
    )(z.reshape(bsz, na, nb, 2 * B_WIDTH), m1, twr, twi, m2)
    return y.reshape(bsz, n, B_WIDTH)


FF_CHUNK = 1024


def _out_mlp_kernel(*refs, steps_a, n_cast):
    (xa_ref, p1a_ref, p2a_ref, xb_ref, p1b_ref, p2b_ref, g1_ref, sh_ref, sc_ref, g2_ref,
     ng_ref, wout_ref, w1_ref, w2_ref) = refs[:14]
    oa_ref, ob_ref = refs[14 + n_cast:16 + n_cast]
    for w_ref, o_ref in zip(refs[14:14 + n_cast], refs[16 + n_cast:]):
        o_ref[...] = w_ref[...].astype(BF16)

    def body(x_ref, p1_ref, p2_ref, o_ref):
        half = p1_ref.shape[-1]
        mix = (_dot(p1_ref[...], wout_ref[:half, :].astype(BF16))
               + _dot(p2_ref[...], wout_ref[half:, :].astype(BF16)))
        x1 = x_ref[...] + g1_ref[...] * mix
        h = _rms_mod(x1, ng_ref[...], sc_ref[...], sh_ref[...]).astype(BF16)
        acc = None
        for c in range(D_FF // FF_CHUNK):
            a = _dot(h, w1_ref[:, c * FF_CHUNK:(c + 1) * FF_CHUNK])
            a = jnp.square(jnp.maximum(a, 0.0)).astype(BF16)
            part = _dot(a, w2_ref[c * FF_CHUNK:(c + 1) * FF_CHUNK, :])
            acc = part if acc is None else acc + part
        o_ref[...] = x1 + g2_ref[...] * acc

    @pl.when(pl.program_id(0) < steps_a)
    def _():
        body(xa_ref, p1a_ref, p2a_ref, oa_ref)

    @pl.when(pl.program_id(0) >= steps_a)
    def _():
        body(xb_ref, p1b_ref, p2b_ref, ob_ref)


def _out_mlp_call(set_a, set_b, mod, layer, ng, w_out, out_idx, w1, w2, side_cast=None):
    (xa, p1a, p2a, row_a), (xb, p1b, p2b, row_b) = set_a, set_b
    half = p1a.shape[-1]
    tm = min(xa.shape[1], xb.shape[1], MLP_TILE)
    tiles_a, tiles_b = xa.shape[1] // tm, xb.shape[1] // tm
    steps_a, steps_b = xa.shape[0] * tiles_a, xb.shape[0] * tiles_b

    def pos_a(s):
        sa = jnp.minimum(s, steps_a - 1)
        return sa // tiles_a, sa % tiles_a

    def pos_b(s):
        sb = jnp.maximum(s - steps_a, 0)
        return sb // tiles_b, sb % tiles_b

    tok_a = lambda w: pl.BlockSpec((None, tm, w), lambda s: pos_a(s) + (0,))
    tok_b = lambda w: pl.BlockSpec((None, tm, w), lambda s: pos_b(s) + (0,))
    row = lambda s: jnp.where(s < steps_a, row_a + pos_a(s)[0], row_b + pos_b(s)[0])
    mod_spec = lambda which: pl.BlockSpec((None, None, None, 1, D_MODEL),
                                          lambda s: (layer, row(s), which, 0, 0))
    stacks, cast_layer = side_cast if side_cast else ((), 0)
    cast_specs, cast_shapes = _side_cast_specs(stacks, cast_layer, steps_a + steps_b, lambda s: s)
    return pl.pallas_call(
        functools.partial(_out_mlp_kernel, steps_a=steps_a, n_cast=len(stacks)),
        grid=(steps_a + steps_b,),
        in_specs=[
            tok_a(D_MODEL), tok_a(half), tok_a(half), tok_b(D_MODEL), tok_b(half), tok_b(half),
            mod_spec(2), mod_spec(3), mod_spec(4), mod_spec(5),
            _const_spec((1, D_MODEL)),
            _stacked_spec((2 * half, D_MODEL), out_idx),
            _const_spec((D_MODEL, D_FF)),
            _const_spec((D_FF, D_MODEL)),
        ] + [i_spec for i_spec, _ in cast_specs],
        out_specs=[tok_a(D_MODEL), tok_b(D_MODEL)] + [o_spec for _, o_spec in cast_specs],
        out_shape=[jax.ShapeDtypeStruct(xa.shape, F32), jax.ShapeDtypeStruct(xb.shape, F32)] + cast_shapes,
        compiler_params=_params(1),
        name="out_mlp",
    )(xa, p1a, p2a, xb, p1b, p2b, mod, mod, mod, mod, ng, w_out, w1, w2, *stacks)


def _odd_in_kernel(*refs, rope, emit_cache, tm):
    x_ref, sh_ref, sc_ref, g_ref, win_ref, gain_ref, hm_ref = refs[:7]
    k = 7
    if rope:
        cos_ref, sin_ref = refs[k:k + 2]
        k += 2
    cq_ref, ck_ref, cvt_ref, dq_ref, dk_ref, dvt_ref = refs[k:k + 6]
    k += 6
    if emit_cache:
        nck_ref, ncv_ref, ndk_ref, ndv_ref = refs[k:k + 4]
    subs = x_ref.shape[0] // tm
    lane = lax.broadcasted_iota(jnp.int32, (tm, LANES), 1)
    first_half = (lane % (2 * ROPE_FREQS)) < ROPE_FREQS
    lo = lane < HEAD_DIM
    pad_row = lax.broadcasted_iota(jnp.int32, (V_ROWS - LANES, tm), 0)
    pad = jnp.where(pad_row == 0, 1.0, 0.0).astype(BF16)

    def cache_rows(ref, sub, r):
        rpt = ref.shape[0] // x_ref.shape[0]
        return ref.at[pl.ds(sub * tm * rpt + r, tm, stride=rpt), :]

    def cache_cols(ref, sub, lead, mat):
        n_req = ref.shape[-1]
        for r in range(tm // n_req):
            ref[(sub * (tm // n_req) + r,) + lead] = mat[:, r * n_req:(r + 1) * n_req]

    groups = ((0, ODD_CV0), (ODD_DQ0, ODD_W), (ODD_CV0, ODD_DQ0))
    gain_lo = (0, ODD_CV0)
    normed_w = (ODD_CV0, DQ_W + DKV_W)
    hidden = {}

    def project(item):
        sub, grp = item
        if sub not in hidden:
            x = x_ref[sub * tm:(sub + 1) * tm, :]
            hidden[sub] = _rms_mod(x, g_ref[...], sc_ref[...], sh_ref[...]).astype(BF16)
        lo, hi = groups[grp]
        return _dot(hidden[sub], win_ref[:, lo:hi].astype(BF16))

    def head_norms(p, lo):
        normed = []
        for c0 in range(0, p.shape[1], 2 * LANES):
            wd = min(2 * LANES, p.shape[1] - c0)
            v = p[:, c0:c0 + wd]
            ms = _dot((v * v).astype(BF16), hm_ref[:wd, :wd])
            y = v * lax.rsqrt(ms + EPS) * gain_ref[:, lo + c0:lo + c0 + wd]
            normed += [y[:, k:k + LANES] for k in range(0, wd, LANES)]
        return normed

    def finish(item, p, normed):
        sub, grp = item
        rows = slice(sub * tm, (sub + 1) * tm)

        def roped(y):
            if not rope:
                return y
            rot = jnp.where(first_half, -pltpu.roll(y, LANES - ROPE_FREQS, 1),
                            pltpu.roll(y, ROPE_FREQS, 1))
            return y * cos_ref[rows, :] + rot * sin_ref[rows, :]

        def put(ref, chunks):
            for c, y in enumerate(chunks):
                ref[rows, c * LANES:(c + 1) * LANES] = roped(y).astype(BF16)

        nq = C_W // LANES
        if grp == 0:
            put(cq_ref, normed[:nq])
            put(ck_ref, normed[nq:])
            if emit_cache:
                for c, y in enumerate(normed[nq:]):
                    yt = y.T
                    for half in range(2):
                        cache_cols(nck_ref, sub, (c, half), yt[half * HEAD_DIM:(half + 1) * HEAD_DIM])
        elif grp == 1:
            put(dq_ref, normed[:nq])
            dk = normed[nq]
            y = roped(dk)
            swapped = pltpu.roll(y, HEAD_DIM, 1)
            dk_ref[rows, :LANES] = jnp.where(lo, y, swapped).astype(BF16)
            dk_ref[rows, LANES:] = jnp.where(lo, swapped, y).astype(BF16)
            if emit_cache:
                dkt = dk.T
                for j in range(D_KV_HEADS):
                    cache_cols(ndk_ref, sub, (j,), dkt[j * HEAD_DIM:(j + 1) * HEAD_DIM])
            dvt = p[:, normed_w[1]:].T
            for j in range(D_KV_HEADS):
                vj = dvt[j * HEAD_DIM:(j + 1) * HEAD_DIM]
                dvt_ref[j, :LANES, rows] = jnp.concatenate([vj, vj], axis=0).astype(BF16)
                dvt_ref[j, LANES:, rows] = pad
                if emit_cache:
                    cache_cols(ndv_ref, sub, (j,), vj)
        else:
            cv = p
            for hh in range(C_HEADS):
                cvt_ref[hh, :LANES, rows] = cv[:, hh * LANES:(hh + 1) * LANES].T.astype(BF16)
                cvt_ref[hh, LANES:, rows] = pad
            if emit_cache:
                for hh in range(C_HEADS):
                    cache_rows(ncv_ref, sub, hh)[...] = cv[:, hh * LANES:(hh + 1) * LANES]

    items = [(sub, grp) for sub in range(subs) for grp in range(len(groups))]
    p = project(items[0])
    for idx, item in enumerate(items):
        normed = head_norms(p[:, :normed_w[item[1]]], gain_lo[item[1]]) if item[1] < 2 else None
        nxt = project(items[idx + 1]) if idx + 1 < len(items) else None
        finish(item, p, normed)
        p = nxt


def _odd_in_call(x, mod, layer, row_of_batch, g, w_in, odd_idx, gains, head_mean, rope_tabs, tm, cache_n=None):
    emit_cache = cache_n is not None
    bsz, n, _ = x.shape
    rope = rope_tabs is not None
    subs = 2 if n % (2 * tm) == 0 else 1
    ts = subs * tm
    tok = lambda w: pl.BlockSpec((None, ts, w), lambda b, i: (b, i, 0))
    in_specs = [
        tok(D_MODEL),
        _mod_spec(layer, 0, row_of_batch),
        _mod_spec(layer, 1, row_of_batch),
        _const_spec((1, D_MODEL)),
        _stacked_spec((D_MODEL, ODD_W), odd_idx),
        _const_spec((1, ODD_NORMED_W)),
        _const_spec((2 * LANES, 2 * LANES)),
    ]
    args = [x, mod, mod, g, w_in, gains, head_mean]
    if rope:
        in_specs += [pl.BlockSpec((ts, LANES), lambda b, i: (i, 0))] * 2
        args += list(rope_tabs)
    vt_spec = lambda heads: pl.BlockSpec((None, heads, V_ROWS, ts), lambda b, i: (b, 0, 0, i))
    vt_shape = lambda heads: jax.ShapeDtypeStruct((bsz, heads, V_ROWS, n), BF16)
    tok_shape = lambda w: jax.ShapeDtypeStruct((bsz, n, w), BF16)
    out_specs = [tok(C_W), tok(C_W), vt_spec(C_HEADS), tok(DQ_W), tok(DKV_DUP_W), vt_spec(D_KV_HEADS)]
    out_shape = [tok_shape(C_W), tok_shape(C_W), vt_shape(C_HEADS), tok_shape(DQ_W),
                 tok_shape(DKV_DUP_W), vt_shape(D_KV_HEADS)]
    if emit_cache:
        assert bsz == 1 and tm % cache_n == 0 and n % cache_n == 0
        reqs, rps = n // cache_n, ts // cache_n
        for lead in ((C_HEADS, 2), None, (D_KV_HEADS,), (D_KV_HEADS,)):
            if lead is None:
                out_specs.append(pl.BlockSpec((None, ts * C_HEADS, LANES), lambda b, i: (b, i, 0)))
                out_shape.append(jax.ShapeDtypeStruct((bsz, n * C_HEADS, LANES), F32))
            else:
                zeros = (0,) * (len(lead) + 2)
                out_specs.append(pl.BlockSpec((rps,) + lead + (HEAD_DIM, cache_n),
                                              lambda b, i, zeros=zeros: (i,) + zeros))
                out_shape.append(jax.ShapeDtypeStruct((reqs,) + lead + (HEAD_DIM, cache_n), F32))
    return pl.pallas_call(
        functools.partial(_odd_in_kernel, rope=rope, emit_cache=emit_cache, tm=tm),
        grid=(bsz, n // ts),
        in_specs=in_specs,
        out_specs=out_specs,
        out_shape=out_shape,
        compiler_params=_params(2),
        name="odd_in",
    )(*args)


QK_AHEAD = 3
SINK_AHEAD = 2


def _diff_attn_kernel(*refs, lam_init, tq, kc):
    n_src = (len(refs) - 4) // 2
    q_ref, (lamv_ref, g_ref, o_ref) = refs[0], refs[-3:]
    k_refs, vt_refs = refs[1:1 + n_src], refs[1 + n_src:1 + 2 * n_src]
    chunks = [(src, off) for src in range(n_src) for off in range(0, k_refs[src].shape[1], kc)]
    n_chunks = len(chunks)
    reqs = q_ref.shape[0]
    tiles = q_ref.shape[1] // tq
    heads = q_ref.shape[2] // LANES
    lane = lax.broadcasted_iota(jnp.int32, (tq, LANES), 1)
    lv = lamv_ref[...]
    lam = (jnp.exp(jnp.sum(lv[0:1] * lv[1:2], keepdims=True))
           - jnp.exp(jnp.sum(lv[2:3] * lv[3:4], keepdims=True)) + lam_init)

    qqs = {}

    def score(item):
        r, h, t, c = item
        cols = slice(h * LANES, (h + 1) * LANES)
        if (r, h, t) not in qqs:
            q = q_ref[r, t * tq:(t + 1) * tq, cols]
            zero = jnp.zeros_like(q)
            qqs[r, h, t] = jnp.concatenate([jnp.where(lane < HEAD_DIM, q, zero),
                                            jnp.where(lane >= HEAD_DIM, q, zero)], axis=0)
        src, off = chunks[c]
        return _dot_nt(k_refs[src][r, off:off + kc, cols], qqs[r, h, t])

    def finish(r, h, t, acc, l):
        inv = 1.0 / l
        ot = acc[:, :tq] * inv[:, :tq] - acc[:, tq:] * (inv[:, tq:] * lam)
        o = ot.T
        ms = jnp.mean(o * o, axis=-1, keepdims=True)
        y = o * lax.rsqrt(ms + EPS) * g_ref[...] * (1.0 - lam_init)
        o_ref[r, t * tq:(t + 1) * tq, h * LANES:(h + 1) * LANES] = y.astype(BF16)

    items = [(r, h, t, c) for r in range(reqs) for h in range(heads) for t in range(tiles)
             for c in range(n_chunks)]
    ahead = [score(it) for it in items[:QK_AHEAD]]
    m = acc = None
    for idx, (r, h, t, c) in enumerate(items):
        s = ahead.pop(0)
        if idx + QK_AHEAD < len(items):
            ahead.append(score(items[idx + QK_AHEAD]))
        cm = jnp.max(s, axis=0, keepdims=True)
        m_new = cm if c == 0 else jnp.maximum(m, cm)
        src, off = chunks[c]
        e = jnp.exp2(s - m_new)
        lsum = jnp.sum(e, axis=0, keepdims=True)
        lane0 = r * k_refs[src].shape[1] + off
        pv = _dot(vt_refs[src][h, :LANES, lane0:lane0 + kc], e.astype(BF16))
        if c == 0:
            acc, l = pv, lsum
        else:
            alpha = jnp.exp2(m - m_new)
            acc, l = acc * alpha + pv, l * alpha + lsum
        m = m_new
        if c == n_chunks - 1:
            finish(r, h, t, acc, l)


def _diff_attn_call(q, ks, vts, lamv, subln_g, lam_init, tq, tiles, heads, reqs=1, flat_values=False):
    bsz, n, _ = q.shape
    assert reqs == 1 or flat_values
    kc = next(c for c in (512, 256, 128) if all(k.shape[1] % c == 0 for k in ks))
    hw = heads * LANES
    k_specs = [pl.BlockSpec((reqs, k.shape[1], hw), lambda b, h, i: (b, 0, h)) for k in ks]
    vt_index = (lambda b, h, i: (0, h, 0, b)) if flat_values else (lambda b, h, i: (b, h, 0, 0))
    vt_specs = [pl.BlockSpec((None, heads, V_ROWS, reqs * k.shape[1]), vt_index) for k in ks]
    return pl.pallas_call(
        functools.partial(_diff_attn_kernel, lam_init=lam_init, tq=tq, kc=kc),
        grid=(bsz // reqs, C_HEADS // heads, n // (tq * tiles)),
        in_specs=[pl.BlockSpec((reqs, tiles * tq, hw), lambda b, h, i: (b, i, h))]
        + k_specs + vt_specs + [_const_spec((4, HEAD_DIM)), _const_spec((1, LANES))],
        out_specs=pl.BlockSpec((reqs, tiles * tq, hw), lambda b, h, i: (b, i, h)),
        out_shape=jax.ShapeDtypeStruct((bsz, n, C_W), BF16),
        compiler_params=_params(3),
        name="diff_attn",
    )(q, *ks, *vts, lamv, subln_g)


def _sink_attn_kernel(*refs, windowed, tq, nblk):
    if windowed:
        (q_ref, ck_ref, kp_ref, km_ref, kn_ref, cvt_ref, vp_ref, vm_ref, vn_ref,
         bias_ref, sink_ref, o_ref) = refs
        kband = jnp.concatenate([kp_ref[...], km_ref[...], kn_ref[...]], axis=0)
        vtband = jnp.concatenate([vp_ref[...], vm_ref[...], vn_ref[...]], axis=1)
    else:
        q_ref, k_ref, vt_ref, sink_ref, o_ref = refs
    reqs = q_ref.shape[0]
    subs = q_ref.shape[1] // tq
    groups = q_ref.shape[2] // (2 * LANES)
    lane = lax.broadcasted_iota(jnp.int32, (tq, LANES), 1)
    lo = lane < HEAD_DIM

    def scores_of(stream):
        r, g, sub = stream
        qd = q_ref[r, sub * tq:(sub + 1) * tq, g * 2 * LANES:(g + 1) * 2 * LANES]
        parts = []
        for c in range(2):
            ch = qd[:, c * LANES:(c + 1) * LANES]
            zero = jnp.zeros_like(ch)
            parts += [jnp.where(lo, ch, zero), jnp.where(lo, zero, ch)]
        qq = jnp.concatenate(parts, axis=0)
        if not windowed:
            return [_dot_nt(k_ref[r, :, g * LANES:(g + 1) * LANES], qq)]
        blk = pl.program_id(2) * subs + sub
        variant = jnp.where(blk == 0, 0, jnp.where(blk == nblk - 1, 2, 1))
        band = _dot_nt(kband[sub * tq:(sub + 3) * tq, :], qq) + bias_ref[variant]
        return [_dot_nt(ck_ref[...], qq), band]

    def finish(stream, scores):
        r, g, sub = stream
        values = ([cvt_ref[...], vtband[:, sub * tq:(sub + 3) * tq]] if windowed
                  else [vt_ref[g, :, r * tq:(r + 1) * tq]])
        sink = sink_ref[g]
        m = sink
        for s in scores:
            m = jnp.maximum(m, jnp.max(s, axis=0, keepdims=True))
        acc = None
        for s, vt in zip(scores, values):
            pv = _dot(vt, jnp.exp2(s - m).astype(BF16))
            acc = pv if acc is None else acc + pv
        inv = 1.0 / (acc[LANES:LANES + 1, :] + jnp.exp2(sink - m))
        ot = acc[:LANES, :] * inv
        for c in range(2):
            o0 = ot[:, (2 * c) * tq:(2 * c + 1) * tq].T
            o1 = ot[:, (2 * c + 1) * tq:(2 * c + 2) * tq].T
            cols = slice((2 * g + c) * LANES, (2 * g + c + 1) * LANES)
            o_ref[r, sub * tq:(sub + 1) * tq, cols] = jnp.where(lo, o0, o1).astype(BF16)

    streams = [(r, g, sub) for r in range(reqs) for g in range(groups) for sub in range(subs)]
    ahead = [scores_of(st) for st in streams[:SINK_AHEAD]]
    for idx, st in enumerate(streams):
        cur = ahead.pop(0)
        if idx + SINK_AHEAD < len(streams):
            ahead.append(scores_of(streams[idx + SINK_AHEAD]))
        finish(st, cur)


def _window_bias(n, tq):
    nblk = n // tq
    kk = np.arange(3 * tq)[:, None]
    qi = np.arange(tq)[None, :]
    band_ok = np.abs(qi + WINDOW - kk) <= WINDOW
    out = []
    for blk in (0, 1, nblk - 1):
        jpos = blk * tq - WINDOW + kk
        ok = band_ok & (jpos >= 0) & (jpos < n)
        out.append(np.tile(np.where(ok, 0.0, NEG_INF), (1, D_GROUP)))
    return jnp.asarray(np.stack(out), F32)


def _sink_attn_call(q, k, vt, sink_rows, ctx_k=None, ctx_vt=None, reqs=1, flat_values=False):
    bsz, n, _ = q.shape
    windowed = ctx_k is not None
    assert reqs == 1 or (flat_values and not windowed)
    tq = WINDOW if windowed else n
    nblk = n // tq
    subs = next(s for s in (16, 8, 4, 2, 1) if nblk % s == 0) if windowed else 1
    steps = nblk // subs
    groups = 1 if windowed else D_KV_HEADS
    q_spec = pl.BlockSpec((reqs, subs * tq, groups * 2 * LANES), lambda b, j, i: (b, i, j))
    sink_spec = pl.BlockSpec((groups, 1, D_GROUP * tq), lambda b, j, i: (j, 0, 0))
    if windowed:
        assert nblk >= 3, "first / interior / last mask variants need three query blocks"
        n_ctx = ctx_k.shape[1]
        prev = lambda i: jnp.maximum(i * subs - 1, 0)
        nxt = lambda i: jnp.minimum((i + 1) * subs, nblk - 1)
        in_specs = [
            q_spec,
            pl.BlockSpec((None, n_ctx, LANES), lambda b, j, i: (b, 0, j)),
            pl.BlockSpec((None, tq, LANES), lambda b, j, i: (b, prev(i), j)),
            pl.BlockSpec((None, subs * tq, LANES), lambda b, j, i: (b, i, j)),
            pl.BlockSpec((None, tq, LANES), lambda b, j, i: (b, nxt(i), j)),
            pl.BlockSpec((None, None, V_ROWS, n_ctx), lambda b, j, i: (b, j, 0, 0)),
            pl.BlockSpec((None, None, V_ROWS, tq), lambda b, j, i: (b, j, 0, prev(i))),
            pl.BlockSpec((None, None, V_ROWS, subs * tq), lambda b, j, i: (b, j, 0, i)),
            pl.BlockSpec((None, None, V_ROWS, tq), lambda b, j, i: (b, j, 0, nxt(i))),
            _const_spec((3, 3 * tq, D_GROUP * tq)),
            sink_spec,
        ]
        args = (q, ctx_k, k, k, k, ctx_vt, vt, vt, vt, _window_bias(n, tq), sink_rows)
    else:
        in_specs = [
            q_spec,
            pl.BlockSpec((reqs, n, groups * LANES), lambda b, j, i: (b, 0, j)),
            pl.BlockSpec((None, groups, V_ROWS, reqs * n),
                         (lambda b, j, i: (0, j, 0, b)) if flat_values else (lambda b, j, i: (b, j, 0, 0))),
            sink_spec,
        ]
        args = (q, k, vt, sink_rows)
    return pl.pallas_call(
        functools.partial(_sink_attn_kernel, windowed=windowed, tq=tq, nblk=nblk),
        grid=(bsz // reqs, D_KV_HEADS // groups, steps),
        in_specs=in_specs,
        out_specs=q_spec,
        out_shape=jax.ShapeDtypeStruct((bsz, n, DQ_W), BF16),
        compiler_params=_params(3),
        name="sink_attn",
    )(*args)


def _dup_heads(a):
    lead = a.shape[:-1]
    a = a.reshape(lead + (D_KV_HEADS, 1, HEAD_DIM))
    return jnp.broadcast_to(a, lead + (D_KV_HEADS, 2, HEAD_DIM)).reshape(lead + (DKV_DUP_W,))


def _odd_gains(c_qn_g, c_kn_g, d_qn_g, d_kn_g):
    return jnp.concatenate([
        jnp.tile(c_qn_g * (ATTN_SCALE * LOG2E), C_W // HEAD_DIM),
        jnp.tile(c_kn_g, C_W // HEAD_DIM),
        jnp.tile(d_qn_g * (ATTN_SCALE * LOG2E), DQ_W // HEAD_DIM),
        jnp.tile(d_kn_g, DKV_W // HEAD_DIM),
    ])[None, :]


def _value_rows(v):
    bsz, nk, w = v.shape
    vt = v.reshape(bsz, nk, w // LANES, LANES).transpose(0, 2, 3, 1)
    pad = jnp.zeros((bsz, w // LANES, V_ROWS - LANES, nk), v.dtype).at[:, :, 0, :].set(1)
    return jnp.concatenate([vt, pad], axis=2)


def _rope_tables(n):
    rows = n // GRID_W
    row = np.repeat(np.arange(rows), GRID_W).astype(np.float64)
    col = np.tile(np.arange(GRID_W), rows).astype(np.float64)
    freqs = ROPE_BASE ** (-np.arange(ROPE_FREQS, dtype=np.float64) / ROPE_FREQS)
    ang = np.stack([row[:, None] * freqs, col[:, None] * freqs], axis=1)
    ang = np.concatenate([ang, ang], axis=-1).reshape(n, HEAD_DIM)
    ang = np.concatenate([ang, ang], axis=-1)
    return jnp.asarray(np.cos(ang), F32), jnp.asarray(np.sin(ang), F32)


MLP_TILE = 512


def _token_tile(n):
    return min(n, 512)


def _mixer(x, l, mod, row_of_batch, wts, caches, new_cache, side_cast=None):
    bsz, n, _ = x.shape
    decode = caches is not None
    g1n = wts["norm1_g"][l][None, :]
    if l % 2 == 0:
        e = l // 2
        a, z, *copies = _even_in_call(x, mod, l, row_of_batch, g1n, e, wts["w_in_even"], wts["w_pool"],
                                      wts["pool_scale"][e][None, :], wts["dftc"], wts["w_fft"], min(n, 1024),
                                      1 if decode or n > 1024 else next(q for q in (4, 2, 1) if bsz % q == 0),
                                      side_cast)
        y = _dft_two_stage_call(z) if z.shape[-1] == 2 * B_WIDTH else z
        return (a, y) + tuple(copies)
    o = l // 2
    lam_init = _lam_init(l)
    rope_tabs = _rope_tables(n) if decode else None
    xo = x if decode else x.reshape(1, bsz * n, D_MODEL)
    outs = _odd_in_call(xo, mod, l, row_of_batch, g1n, wts["w_in_odd"], o, wts["odd_gains"][o],
                        wts["head_mean"], rope_tabs, _token_tile(xo.shape[1]),
                        None if decode else n)
    if not decode:
        outs = [a if a.ndim == 4 else a.reshape(bsz, n, a.shape[-1]) for a in outs[:6]] + list(outs[6:])
    cq, ck, cvt, dq, dkk, dvt = outs[:6]
    if decode:
        c_k, c_v, d_k, d_v = caches
        lc = c_k.shape[2]
        ks = [c_k[:, o].reshape(bsz, lc, C_W).astype(BF16), ck]
        vts = [_value_rows(c_v[:, o].reshape(bsz, lc, C_W).astype(BF16)), cvt]
        ctx_k = _dup_heads(d_k[:, o].reshape(bsz, lc, DKV_W)).astype(BF16)
        ctx_vt = _value_rows(_dup_heads(d_v[:, o].reshape(bsz, lc, DKV_W)).astype(BF16))
        tq = min(n, 256)
    else:
        new_cache.append(outs[6:])
        ks, vts = [ck], [cvt]
        ctx_k = ctx_vt = None
        tq = n
    tiles = next(t for t in (4, 2, 1) if n % (t * tq) == 0)
    heads = 1 if decode else C_HEADS
    c_out = _diff_attn_call(cq, ks, vts, wts["lamv"][o],
                            wts["c_subln_g"][o][None, :], lam_init, tq, tiles, heads,
                            1 if decode else next(r for r in (4, 2, 1) if bsz % r == 0), not decode)
    sink_tq = WINDOW if decode else n
    sink_rows = jnp.repeat(wts["d_sink"][o].reshape(D_KV_HEADS, D_GROUP) * LOG2E, sink_tq,
                           axis=-1)[:, None, :]
    d_out = _sink_attn_call(dq, dkk, dvt, sink_rows, ctx_k, ctx_vt,
                            1 if decode else next(r for r in (4, 2, 1) if bsz % r == 0), not decode)
    return c_out, d_out


def kernel(x_prompt, x_sample, c, cache_c_k, cache_c_v, cache_d_k, cache_d_v, c_ctx,
           norm1_g, norm2_g, w_ada, b_ada, w_in_even, w_pool, pool_scale, w_fft, w_out_even,
           w_in_odd, c_qn_g, c_kn_g, lam_q1, lam_k1, lam_q2, lam_k2, c_subln_g,
           d_qn_g, d_kn_g, d_sink, w_out_odd, w_mlp1, w_mlp2):
    depth = norm1_g.shape[0]
    n_odd = w_in_odd.shape[0]
    dec_b = c.shape[0]

    rows = -(-(1 + dec_b) // SUBLANES) * SUBLANES
    cond = jnp.concatenate([c_ctx[None, :], c, jnp.zeros((rows - 1 - dec_b, D_MODEL), F32)], axis=0)
    mod = _adaln_call(cond, w_ada, b_ada).reshape(depth, rows, 6, 1, D_MODEL)

    ic = np.arange(GROUP_W)
    ang = 2.0 * np.pi * np.outer(ic, ic) / GROUP_W
    dftc = _bf16_const(np.concatenate([np.cos(ang), -np.sin(ang)], axis=1))
    head_mean = _bf16_const(np.kron(np.eye(2 * LANES // HEAD_DIM), np.ones((HEAD_DIM, HEAD_DIM))) / HEAD_DIM)

    odd_gains = [_odd_gains(c_qn_g[o], c_kn_g[o], d_qn_g[o], d_kn_g[o]) for o in range(n_odd)]
    wts = {
        "norm1_g": norm1_g, "norm2_g": norm2_g,
        "w_in_even": w_in_even, "w_pool": w_pool, "pool_scale": pool_scale,
        "w_fft": w_fft, "w_out_even": w_out_even, "dftc": dftc,
        "w_in_odd": w_in_odd, "odd_gains": odd_gains, "head_mean": head_mean,
        "lamv": jnp.stack([lam_q1, lam_k1, lam_q2, lam_k2], axis=1),
        "c_subln_g": c_subln_g, "d_sink": d_sink, "w_out_odd": w_out_odd,
    }

    bsz, n, _ = x_prompt.shape
    caches = (cache_c_k, cache_c_v, cache_d_k, cache_d_v)
    new_cache = []
    xp, xs = x_prompt, x_sample
    mlp_stacks = (w_mlp1, w_mlp2)
    mlp_w = None
    for l in range(depth):
        pp = _mixer(xp, l, mod, lambda b: 0, wts, None, new_cache)
        ps = _mixer(xs, l, mod, lambda b: 1 + b, wts, caches, new_cache,
                    (mlp_stacks, 0) if l == 0 else None)
        if l == 0:
            mlp_w = ps[2:]
        w_out, out_idx = (wts["w_out_even"], l // 2) if l % 2 == 0 else (wts["w_out_odd"], l // 2)
        flat = lambda a: a.reshape(1, bsz * n, a.shape[-1])
        xp, xs, *mlp_next = _out_mlp_call((flat(xp), flat(pp[0]), flat(pp[1]), 0), (xs, ps[0], ps[1], 1),
                                          mod, l, wts["norm2_g"][l][None, :], w_out, out_idx, *mlp_w,
                                          (mlp_stacks, l + 1) if l + 1 < depth else None)
        mlp_w = mlp_next
        xp = xp.reshape(bsz, n, D_MODEL)
    y_prompt, y_sample = xp, xs
    new_c_k = jnp.stack([nc[0].transpose(0, 4, 1, 2, 3) for nc in new_cache], axis=1)
    new_c_v = jnp.stack([nc[1].reshape(bsz, n, C_HEADS, 2 * HEAD_DIM) for nc in new_cache], axis=1)
    new_d_k = jnp.stack([nc[2].transpose(0, 3, 1, 2) for nc in new_cache], axis=1)
    new_d_v = jnp.stack([nc[3].transpose(0, 3, 1, 2) for nc in new_cache], axis=1)
    return (y_prompt, y_sample, new_c_k, new_c_v, new_d_k, new_d_v)
```

```python
import functools
import math

import numpy as np
import jax
import jax.numpy as jnp
from jax import lax
from jax.experimental import pallas as pl
from jax.experimental.pallas import tpu as pltpu

F32 = jnp.float32
BF16 = jnp.bfloat16

D_MODEL = 1024
HEAD_DIM = 64
LANES = 128
SUBLANES = 8
GRID_W = 64
ROPE_FREQS = HEAD_DIM // 4
ROPE_BASE = 10000.0
EPS = 1e-6
NEG_INF = -1e30
WINDOW = 128
A_WIDTH = D_MODEL // 2
B_WIDTH = D_MODEL // 2
POOL_WINDOWS = (2, 4, 8, 16)
POOL_HALO = 8
N_GROUPS = 4
GROUP_W = A_WIDTH // N_GROUPS
C_HEADS = 4
D_HEADS = 8
D_KV_HEADS = 2
D_GROUP = D_HEADS // D_KV_HEADS
C_W = C_HEADS * 2 * HEAD_DIM
DQ_W = D_HEADS * HEAD_DIM
DKV_W = D_KV_HEADS * HEAD_DIM
DKV_DUP_W = 2 * DKV_W
D_FF = 4 * D_MODEL
ATTN_SCALE = HEAD_DIM ** -0.5
LOG2E = math.log2(math.e)
V_ROWS = LANES + 16
ODD_CV0 = 2 * C_W
ODD_DQ0 = 3 * C_W
ODD_DK0 = ODD_DQ0 + DQ_W
ODD_DV0 = ODD_DK0 + DKV_W
ODD_W = ODD_DV0 + DKV_W
ODD_NORMED_W = 2 * C_W + DQ_W + DKV_W
VMEM_LIMIT = 56 * 1024 * 1024


def _lam_init(layer):
    return 0.8 - 0.6 * math.exp(-0.3 * layer)


def _params(n_axes):
    return pltpu.CompilerParams(dimension_semantics=("arbitrary",) * n_axes,
                                vmem_limit_bytes=VMEM_LIMIT)


def _const_spec(shape):
    nd = len(shape)
    return pl.BlockSpec(shape, lambda *_: (0,) * nd, pipeline_mode=pl.Buffered(1))


def _stacked_spec(shape, idx):
    nd = len(shape)
    return pl.BlockSpec((None,) + tuple(shape), lambda *_: (idx,) + (0,) * nd, pipeline_mode=pl.Buffered(1))


def _side_cast_specs(stacks, layer, n_steps, step_of):
    specs, shapes = [], []
    for w in stacks:
        axis = 2 if w.shape[2] >= w.shape[1] else 1
        nb = 1
        while nb * 2 <= n_steps and w.shape[axis] % (nb * 2 * 2 * LANES) == 0:
            nb *= 2
        block = [None, w.shape[1], w.shape[2]]
        block[axis] //= nb

        def index(*grid, axis=axis, nb=nb):
            blk = jnp.minimum(step_of(*grid), nb - 1)
            return (layer, 0, blk) if axis == 2 else (layer, blk, 0)

        specs.append((pl.BlockSpec(tuple(block), index),
                      pl.BlockSpec(tuple(block[1:]), lambda *grid, index=index: index(*grid)[1:])))
        shapes.append(jax.ShapeDtypeStruct(w.shape[1:], BF16))
    return specs, shapes


def _mod_spec(layer, which, row_of_batch):
    return pl.BlockSpec((None, None, None, 1, D_MODEL),
                        lambda b, i: (layer, row_of_batch(b), which, 0, 0))


def _rms_mod(x, g, sc, sh):
    ms = jnp.mean(x * x, axis=-1, keepdims=True)
    return (x * lax.rsqrt(ms + EPS) * g) * (1.0 + sc) + sh


def _bf16_const(a):
    return jnp.asarray(a, F32).astype(BF16)


def _dot_nt(a, b):
    return lax.dot_general(a, b, (((1,), (1,)), ((), ())), preferred_element_type=F32)


def _dot(a, b):
    return jnp.dot(a, b, preferred_element_type=F32)


def _adaln_kernel(cond_ref, w_ref, b_ref, o_ref):
    cnd = cond_ref[...]
    s = (cnd * jax.nn.sigmoid(cnd)).astype(BF16)
    o_ref[...] = _dot(s, w_ref[...].astype(BF16)) + b_ref[...]


def _adaln_call(cond, w_ada, b_ada):
    depth = w_ada.shape[0]
    rows = cond.shape[0]
    tn = 3 * D_MODEL
    return pl.pallas_call(
        _adaln_kernel,
        grid=(depth, 6 * D_MODEL // tn),
        in_specs=[
            pl.BlockSpec((rows, D_MODEL), lambda l, j: (0, 0)),
            pl.BlockSpec((None, D_MODEL, tn), lambda l, j: (l, 0, j)),
            pl.BlockSpec((None, 1, tn), lambda l, j: (l, 0, j)),
        ],
        out_specs=pl.BlockSpec((None, rows, tn), lambda l, j: (l, 0, j)),
        out_shape=jax.ShapeDtypeStruct((depth, rows, 6 * D_MODEL), F32),
        compiler_params=_params(2),
        name="adaln",
    )(cond, w_ada, b_ada.reshape(depth, 1, 6 * D_MODEL))


def _even_in_kernel(*refs, tm, n, n_cast, dft_scale):
    n_in = 11 if dft_scale is None else 12
    (x_ref, xp_ref, xn_ref, sh_ref, sc_ref, g_ref, win_ref, wpool_ref,
     pscale_ref, dftc_ref, wfft_ref) = refs[:11]
    a_ref, z_ref = refs[n_in + n_cast:n_in + 2 + n_cast]
    for w_ref, o_ref in zip(refs[n_in:n_in + n_cast], refs[n_in + 2 + n_cast:]):
        o_ref[...] = w_ref[...].astype(BF16)
    i = pl.program_id(1)
    n_tiles = n // tm
    g, sc, sh = g_ref[...], sc_ref[...], sh_ref[...]
    seqs = x_ref.shape[0]
    h = _rms_mod(x_ref[...].reshape(seqs * tm, D_MODEL), g, sc, sh).astype(BF16)
    w_in = win_ref[...].astype(BF16)
    u = _dot(h, w_in)
    xh = jnp.concatenate([xp_ref[...], xn_ref[...]], axis=0)
    hh = _rms_mod(xh, g, sc, sh).astype(BF16)
    uh = _dot(hh, w_in[:, :A_WIDTH])
    has_prev = (i > 0).astype(F32)
    has_next = (i < n_tiles - 1).astype(F32)
    rows = tm + 2 * POOL_HALO

    planes = []
    for gi in range(N_GROUPS):
        ub = u[:, A_WIDTH + gi * GROUP_W:A_WIDTH + (gi + 1) * GROUP_W].astype(BF16)
        cs = _dot(ub, dftc_ref[...])
        w_fft = wfft_ref[gi].astype(BF16)
        zr = _dot(cs[:, :GROUP_W].astype(BF16), w_fft)
        zi = _dot(cs[:, GROUP_W:].astype(BF16), w_fft)
        planes.append((zr.astype(BF16), zi.astype(BF16)))
    for q in range(seqs):
        sl = slice(q * tm, (q + 1) * tm)
        zr = jnp.concatenate([p[0][sl] for p in planes], axis=1)
        zi = jnp.concatenate([p[1][sl] for p in planes], axis=1)
        if dft_scale is None:
            z_ref[q, :, :B_WIDTH] = zr
            z_ref[q, :, B_WIDTH:] = zi
        else:
            y = _dot(refs[11][...], jnp.concatenate([zr, zi], axis=0)) * dft_scale
            z_ref[q] = y.astype(BF16)

    t = i * tm + lax.broadcasted_iota(jnp.int32, (tm, 1), 0)
    for q, (gi, w) in ((q, gw) for q in range(seqs) for gw in enumerate(POOL_WINDOWS)):
        cols = slice(gi * GROUP_W, (gi + 1) * GROUP_W)
        uq = u[q * tm:(q + 1) * tm, cols]
        ext = jnp.concatenate([uh[0:POOL_HALO, cols] * has_prev, uq,
                               uh[POOL_HALO:, cols] * has_next], axis=0)
        s = ext
        span = 1
        while span < w:
            s = s + pltpu.roll(s, span, 0)
            span *= 2
        shift = w // 2 - 1
        s = pltpu.roll(s, rows - shift, 0) if shift else s
        acc = s[POOL_HALO:POOL_HALO + tm]
        lo = jnp.maximum(t - w // 2, 0)
        hi = jnp.minimum(t + w // 2, n)
        cnt = (hi - lo).astype(F32)
        pooled = acc / cnt - uq
        y = _dot(pooled.astype(BF16), wpool_ref[gi].astype(BF16)) * pscale_ref[:, cols]
        a_ref[q, :, cols] = y.astype(BF16)


def _even_in_call(x, mod, layer, row_of_batch, g, e, w_in, w_pool, pool_scale, dftc, w_fft, tm, seqs,
                  side_cast=None):
    bsz, n, _ = x.shape
    assert seqs == 1 or tm == n
    hb = tm // POOL_HALO
    tiles = n // tm
    stacks, cast_layer = side_cast if side_cast else ((), 0)
    cast_specs, cast_shapes = _side_cast_specs(stacks, cast_layer, (bsz // seqs) * tiles,
                                               lambda b, i: b * tiles + i)
    fused = tm == n
    if fused:
        k = np.arange(n)
        ang = 2.0 * np.pi * np.outer(k, k) / n
        dft_args = [_bf16_const(np.concatenate([np.cos(ang), np.sin(ang)], axis=1))]
        dft_specs = [_const_spec((n, 2 * n))]
    else:
        dft_args, dft_specs = [], []
    kern = functools.partial(_even_in_kernel, tm=tm, n=n, n_cast=len(stacks),
                             dft_scale=float((n * GROUP_W) ** -0.5) if fused else None)
    second_w = B_WIDTH if fused else 2 * B_WIDTH
    return pl.pallas_call(
        kern,
        grid=(bsz // seqs, n // tm),
        in_specs=[
            pl.BlockSpec((seqs, tm, D_MODEL), lambda b, i: (b, i, 0)),
            pl.BlockSpec((None, POOL_HALO, D_MODEL),
                         lambda b, i: (b * seqs, jnp.maximum(i * hb - 1, 0), 0)),
            pl.BlockSpec((None, POOL_HALO, D_MODEL),
                         lambda b, i: (b * seqs, jnp.minimum((i + 1) * hb, n // POOL_HALO - 1), 0)),
            _mod_spec(layer, 0, row_of_batch),
            _mod_spec(layer, 1, row_of_batch),
            _const_spec((1, D_MODEL)),
            _stacked_spec((D_MODEL, D_MODEL), e),
            _stacked_spec((N_GROUPS, GROUP_W, GROUP_W), e),
            _const_spec((1, A_WIDTH)),
            _const_spec((GROUP_W, 2 * GROUP_W)),
            _stacked_spec((N_GROUPS, GROUP_W, GROUP_W), e),
        ] + dft_specs + [i_spec for i_spec, _ in cast_specs],
        out_specs=[
            pl.BlockSpec((seqs, tm, A_WIDTH), lambda b, i: (b, i, 0)),
            pl.BlockSpec((seqs, tm, second_w), lambda b, i: (b, i, 0)),
        ] + [o_spec for _, o_spec in cast_specs],
        out_shape=[
            jax.ShapeDtypeStruct((bsz, n, A_WIDTH), BF16),
            jax.ShapeDtypeStruct((bsz, n, second_w), BF16),
        ] + cast_shapes,
        compiler_params=_params(2),
        name="even_in",
    )(x, x, x, mod, mod, g, w_in, w_pool, pool_scale, dftc, w_fft, *dft_args, *stacks)


DFT_ROWS = 16
DFT_PITCH = 24


def _pitch_rows(x):
    g, _, w = x.shape
    pad = jnp.zeros((g, DFT_PITCH - DFT_ROWS, w), x.dtype)
    return jnp.concatenate([x, pad], axis=1).reshape(g * DFT_PITCH, w)


def _dft_kernel(z_ref, m1_ref, twr_ref, twi_ref, m2_ref, y_ref, zs_ref, os_ref, ys_ref, bs_ref, *, na, nb, scale):
    j = pl.program_id(1)
    n_s1 = nb // DFT_ROWS
    nt = 2 * B_WIDTH // LANES

    @pl.when((pl.program_id(0) == 0) & (j == 0))
    def _():
        os_ref[...] = jnp.zeros_like(os_ref)
        ys_ref[...] = jnp.zeros_like(ys_ref)

    @pl.when(j < n_s1)
    def _():
        zf = _pitch_rows(z_ref[...].astype(F32))
        for k in range(nt):
            zs_ref[k] = zf[:, k * LANES:(k + 1) * LANES]
        for bi in range(DFT_ROWS):
            rows = pl.ds(bi, na, stride=DFT_PITCH)
            zb = [zs_ref[k, rows, :].astype(BF16) for k in range(nt)]
            rhs = jnp.concatenate([jnp.concatenate(zb[:nt // 2], axis=1),
                                   jnp.concatenate(zb[nt // 2:], axis=1)], axis=0)
            r = _dot(m1_ref[...], rhs)
            br, bim = r[:na], r[na:]
            tr, ti = twr_ref[bi], twi_ref[bi]
            out = (br * tr - bim * ti, br * ti + bim * tr)
            for k in range(nt):
                half, kk = divmod(k, nt // 2)
                os_ref[k, rows, :] = out[half][:, kk * LANES:(kk + 1) * LANES]
        of = jnp.concatenate([os_ref[k] for k in range(nt)], axis=1)
        bs_ref[j] = of.reshape(na, DFT_PITCH, 2 * B_WIDTH)[:, :DFT_ROWS].astype(BF16)

    @pl.when(j >= n_s1)
    def _():
        for di in range(DFT_ROWS):
            d = (j - n_s1) * DFT_ROWS + di
            bb = jnp.concatenate([bs_ref[jj, d] for jj in range(n_s1)], axis=0)
            rhs = jnp.concatenate([bb[:, :B_WIDTH], bb[:, B_WIDTH:]], axis=0)
            y = _dot(m2_ref[...], rhs) * scale
            for k in range(B_WIDTH // LANES):
                ys_ref[k, pl.ds(di, nb, stride=DFT_PITCH), :] = y[:, k * LANES:(k + 1) * LANES]
        yf = jnp.concatenate([ys_ref[k] for k in range(B_WIDTH // LANES)], axis=1)
        y_ref[...] = yf.reshape(nb, DFT_PITCH, B_WIDTH)[:, :DFT_ROWS].astype(BF16)


def _dft_two_stage_call(z):
    bsz, n, _ = z.shape
    na = 1 << (int(math.log2(n)) // 2)
    nb = n // na
    ia = np.arange(na)
    ib = np.arange(nb)
    ang_a = 2.0 * np.pi * np.outer(ia, ia) / na
    fr, fi = np.cos(ang_a), -np.sin(ang_a)
    m1 = _bf16_const(np.block([[fr, -fi], [fi, fr]]))
    ang_t = 2.0 * np.pi * np.outer(ib, ia) / n
    twr = jnp.asarray(np.cos(ang_t)[:, :, None], F32)
    twi = jnp.asarray(-np.sin(ang_t)[:, :, None], F32)
    ang_b = 2.0 * np.pi * np.outer(ib, ib) / nb
    m2 = _bf16_const(np.concatenate([np.cos(ang_b), np.sin(ang_b)], axis=1))

    n_s1, n_s2 = nb // DFT_ROWS, na // DFT_ROWS
    s1_blk = lambda j: jnp.minimum(j, n_s1 - 1)
    s2_blk = lambda j: jnp.maximum(j - n_s1, 0)
    y = pl.pallas_call(
        functools.partial(_dft_kernel, na=na, nb=nb, scale=float((n * GROUP_W) ** -0.5)),
        grid=(bsz, n_s1 + n_s2),
        in_specs=[
            pl.BlockSpec((None, na, DFT_ROWS, 2 * B_WIDTH), lambda b, j: (b, 0, s1_blk(j), 0)),
            _const_spec((2 * na, 2 * na)),
            pl.BlockSpec((DFT_ROWS, na, 1), lambda b, j: (s1_blk(j), 0, 0)),
            pl.BlockSpec((DFT_ROWS, na, 1), lambda b, j: (s1_blk(j), 0, 0)),
            _const_spec((nb, 2 * nb)),
        ],
        out_specs=pl.BlockSpec((None, nb, DFT_ROWS, B_WIDTH), lambda b, j: (b, 0, s2_blk(j), 0)),
        out_shape=jax.ShapeDtypeStruct((bsz, nb, na, B_WIDTH), BF16),
        scratch_shapes=[
            pltpu.VMEM((2 * B_WIDTH // LANES, na * DFT_PITCH, LANES), F32),
            pltpu.VMEM((2 * B_WIDTH // LANES, na * DFT_PITCH, LANES), F32),
            pltpu.VMEM((B_WIDTH // LANES, nb * DFT_PITCH, LANES), F32),
            pltpu.VMEM((n_s1, na, DFT_ROWS, 2 * B_WIDTH), BF16),
        ],
        compiler_params=_params(2),
        name="dft_two_stage",
    )(z.reshape(bsz, na, nb, 2 * B_WIDTH), m1, twr, twi, m2)
    return y.reshape(bsz, n, B_WIDTH)


FF_CHUNK = 1024


def _out_mlp_kernel(*refs, steps_a, n_cast):
    (xa_ref, p1a_ref, p2a_ref, xb_ref, p1b_ref, p2b_ref, g1_ref, sh_ref, sc_ref, g2_ref,
     ng_ref, wout_ref, w1_ref, w2_ref) = refs[:14]
    oa_ref, ob_ref = refs[14 + n_cast:16 + n_cast]
    for w_ref, o_ref in zip(refs[14:14 + n_cast], refs[16 + n_cast:]):
        o_ref[...] = w_ref[...].astype(BF16)

    def body(x_ref, p1_ref, p2_ref, o_ref):
        half = p1_ref.shape[-1]
        mix = (_dot(p1_ref[...], wout_ref[:half, :].astype(BF16))
               + _dot(p2_ref[...], wout_ref[half:, :].astype(BF16)))
        x1 = x_ref[...] + g1_ref[...] * mix
        h = _rms_mod(x1, ng_ref[...], sc_ref[...], sh_ref[...]).astype(BF16)
        acc = None
        for c in range(D_FF // FF_CHUNK):
            a = _dot(h, w1_ref[:, c * FF_CHUNK:(c + 1) * FF_CHUNK])
            a = jnp.square(jnp.maximum(a, 0.0)).astype(BF16)
            part = _dot(a, w2_ref[c * FF_CHUNK:(c + 1) * FF_CHUNK, :])
            acc = part if acc is None else acc + part
        o_ref[...] = x1 + g2_ref[...] * acc

    @pl.when(pl.program_id(0) < steps_a)
    def _():
        body(xa_ref, p1a_ref, p2a_ref, oa_ref)

    @pl.when(pl.program_id(0) >= steps_a)
    def _():
        body(xb_ref, p1b_ref, p2b_ref, ob_ref)


def _out_mlp_call(set_a, set_b, mod, layer, ng, w_out, out_idx, w1, w2, side_cast=None):
    (xa, p1a, p2a, row_a), (xb, p1b, p2b, row_b) = set_a, set_b
    half = p1a.shape[-1]
    tm = min(xa.shape[1], xb.shape[1], MLP_TILE)
    tiles_a, tiles_b = xa.shape[1] // tm, xb.shape[1] // tm
    steps_a, steps_b = xa.shape[0] * tiles_a, xb.shape[0] * tiles_b

    def pos_a(s):
        sa = jnp.minimum(s, steps_a - 1)
        return sa // tiles_a, sa % tiles_a

    def pos_b(s):
        sb = jnp.maximum(s - steps_a, 0)
        return sb // tiles_b, sb % tiles_b

    tok_a = lambda w: pl.BlockSpec((None, tm, w), lambda s: pos_a(s) + (0,))
    tok_b = lambda w: pl.BlockSpec((None, tm, w), lambda s: pos_b(s) + (0,))
    row = lambda s: jnp.where(s < steps_a, row_a + pos_a(s)[0], row_b + pos_b(s)[0])
    mod_spec = lambda which: pl.BlockSpec((None, None, None, 1, D_MODEL),
                                          lambda s: (layer, row(s), which, 0, 0))
    stacks, cast_layer = side_cast if side_cast else ((), 0)
    cast_specs, cast_shapes = _side_cast_specs(stacks, cast_layer, steps_a + steps_b, lambda s: s)
    return pl.pallas_call(
        functools.partial(_out_mlp_kernel, steps_a=steps_a, n_cast=len(stacks)),
        grid=(steps_a + steps_b,),
        in_specs=[
            tok_a(D_MODEL), tok_a(half), tok_a(half), tok_b(D_MODEL), tok_b(half), tok_b(half),
            mod_spec(2), mod_spec(3), mod_spec(4), mod_spec(5),
            _const_spec((1, D_MODEL)),
            _stacked_spec((2 * half, D_MODEL), out_idx),
            _const_spec((D_MODEL, D_FF)),
            _const_spec((D_FF, D_MODEL)),
        ] + [i_spec for i_spec, _ in cast_specs],
        out_specs=[tok_a(D_MODEL), tok_b(D_MODEL)] + [o_spec for _, o_spec in cast_specs],
        out_shape=[jax.ShapeDtypeStruct(xa.shape, F32), jax.ShapeDtypeStruct(xb.shape, F32)] + cast_shapes,
        compiler_params=_params(1),
        name="out_mlp",
    )(xa, p1a, p2a, xb, p1b, p2b, mod, mod, mod, mod, ng, w_out, w1, w2, *stacks)


def _odd_in_kernel(*refs, rope, emit_cache, tm):
    x_ref, sh_ref, sc_ref, g_ref, win_ref, gain_ref, hm_ref = refs[:7]
    k = 7
    if rope:
        cos_ref, sin_ref = refs[k:k + 2]
        k += 2
    cq_ref, ck_ref, cvt_ref, dq_ref, dk_ref, dvt_ref = refs[k:k + 6]
    k += 6
    if emit_cache:
        nck_ref, ncv_ref, ndk_ref, ndv_ref = refs[k:k + 4]
    subs = x_ref.shape[0] // tm
    lane = lax.broadcasted_iota(jnp.int32, (tm, LANES), 1)
    first_half = (lane % (2 * ROPE_FREQS)) < ROPE_FREQS
    lo = lane < HEAD_DIM
    pad_row = lax.broadcasted_iota(jnp.int32, (V_ROWS - LANES, tm), 0)
    pad = jnp.where(pad_row == 0, 1.0, 0.0).astype(BF16)

    def cache_rows(ref, sub, r):
        rpt = ref.shape[0] // x_ref.shape[0]
        return ref.at[pl.ds(sub * tm * rpt + r, tm, stride=rpt), :]

    def cache_cols(ref, sub, lead, mat):
        n_req = ref.shape[-1]
        for r in range(tm // n_req):
            ref[(sub * (tm // n_req) + r,) + lead] = mat[:, r * n_req:(r + 1) * n_req]

    groups = ((0, ODD_CV0), (ODD_DQ0, ODD_W), (ODD_CV0, ODD_DQ0))
    gain_lo = (0, ODD_CV0)
    normed_w = (ODD_CV0, DQ_W + DKV_W)
    hidden = {}

    def project(item):
        sub, grp = item
        if sub not in hidden:
            x = x_ref[sub * tm:(sub + 1) * tm, :]
            hidden[sub] = _rms_mod(x, g_ref[...], sc_ref[...], sh_ref[...]).astype(BF16)
        lo, hi = groups[grp]
        return _dot(hidden[sub], win_ref[:, lo:hi].astype(BF16))

    def head_norms(p, lo):
        normed = []
        for c0 in range(0, p.shape[1], 2 * LANES):
            wd = min(2 * LANES, p.shape[1] - c0)
            v = p[:, c0:c0 + wd]
            ms = _dot((v * v).astype(BF16), hm_ref[:wd, :wd])
            y = v * lax.rsqrt(ms + EPS) * gain_ref[:, lo + c0:lo + c0 + wd]
            normed += [y[:, k:k + LANES] for k in range(0, wd, LANES)]
        return normed

    def finish(item, p, normed):
        sub, grp = item
        rows = slice(sub * tm, (sub + 1) * tm)

        def roped(y):
            if not rope:
                return y
            rot = jnp.where(first_half, -pltpu.roll(y, LANES - ROPE_FREQS, 1),
                            pltpu.roll(y, ROPE_FREQS, 1))
            return y * cos_ref[rows, :] + rot * sin_ref[rows, :]

        def put(ref, chunks):
            for c, y in enumerate(chunks):
                ref[rows, c * LANES:(c + 1) * LANES] = roped(y).astype(BF16)

        nq = C_W // LANES
        if grp == 0:
            put(cq_ref, normed[:nq])
            put(ck_ref, normed[nq:])
            if emit_cache:
                for c, y in enumerate(normed[nq:]):
                    yt = y.T
                    for half in range(2):
                        cache_cols(nck_ref, sub, (c, half), yt[half * HEAD_DIM:(half + 1) * HEAD_DIM])
        elif grp == 1:
            put(dq_ref, normed[:nq])
            dk = normed[nq]
            y = roped(dk)
            swapped = pltpu.roll(y, HEAD_DIM, 1)
            dk_ref[rows, :LANES] = jnp.where(lo, y, swapped).astype(BF16)
            dk_ref[rows, LANES:] = jnp.where(lo, swapped, y).astype(BF16)
            if emit_cache:
                dkt = dk.T
                for j in range(D_KV_HEADS):
                    cache_cols(ndk_ref, sub, (j,), dkt[j * HEAD_DIM:(j + 1) * HEAD_DIM])
            dvt = p[:, normed_w[1]:].T
            for j in range(D_KV_HEADS):
                vj = dvt[j * HEAD_DIM:(j + 1) * HEAD_DIM]
                dvt_ref[j, :LANES, rows] = jnp.concatenate([vj, vj], axis=0).astype(BF16)
                dvt_ref[j, LANES:, rows] = pad
                if emit_cache:
                    cache_cols(ndv_ref, sub, (j,), vj)
        else:
            cv = p
            for hh in range(C_HEADS):
                cvt_ref[hh, :LANES, rows] = cv[:, hh * LANES:(hh + 1) * LANES].T.astype(BF16)
                cvt_ref[hh, LANES:, rows] = pad
            if emit_cache:
                for hh in range(C_HEADS):
                    cache_rows(ncv_ref, sub, hh)[...] = cv[:, hh * LANES:(hh + 1) * LANES]

    items = [(sub, grp) for sub in range(subs) for grp in range(len(groups))]
    p = project(items[0])
    for idx, item in enumerate(items):
        normed = head_norms(p[:, :normed_w[item[1]]], gain_lo[item[1]]) if item[1] < 2 else None
        nxt = project(items[idx + 1]) if idx + 1 < len(items) else None
        finish(item, p, normed)
        p = nxt


def _odd_in_call(x, mod, layer, row_of_batch, g, w_in, odd_idx, gains, head_mean, rope_tabs, tm, cache_n=None):
    emit_cache = cache_n is not None
    bsz, n, _ = x.shape
    rope = rope_tabs is not None
    subs = 2 if n % (2 * tm) == 0 else 1
    ts = subs * tm
    tok = lambda w: pl.BlockSpec((None, ts, w), lambda b, i: (b, i, 0))
    in_specs = [
        tok(D_MODEL),
        _mod_spec(layer, 0, row_of_batch),
        _mod_spec(layer, 1, row_of_batch),
        _const_spec((1, D_MODEL)),
        _stacked_spec((D_MODEL, ODD_W), odd_idx),
        _const_spec((1, ODD_NORMED_W)),
        _const_spec((2 * LANES, 2 * LANES)),
    ]
    args = [x, mod, mod, g, w_in, gains, head_mean]
    if rope:
        in_specs += [pl.BlockSpec((ts, LANES), lambda b, i: (i, 0))] * 2
        args += list(rope_tabs)
    vt_spec = lambda heads: pl.BlockSpec((None, heads, V_ROWS, ts), lambda b, i: (b, 0, 0, i))
    vt_shape = lambda heads: jax.ShapeDtypeStruct((bsz, heads, V_ROWS, n), BF16)
    tok_shape = lambda w: jax.ShapeDtypeStruct((bsz, n, w), BF16)
    out_specs = [tok(C_W), tok(C_W), vt_spec(C_HEADS), tok(DQ_W), tok(DKV_DUP_W), vt_spec(D_KV_HEADS)]
    out_shape = [tok_shape(C_W), tok_shape(C_W), vt_shape(C_HEADS), tok_shape(DQ_W),
                 tok_shape(DKV_DUP_W), vt_shape(D_KV_HEADS)]
    if emit_cache:
        assert bsz == 1 and tm % cache_n == 0 and n % cache_n == 0
        reqs, rps = n // cache_n, ts // cache_n
        for lead in ((C_HEADS, 2), None, (D_KV_HEADS,), (D_KV_HEADS,)):
            if lead is None:
                out_specs.append(pl.BlockSpec((None, ts * C_HEADS, LANES), lambda b, i: (b, i, 0)))
                out_shape.append(jax.ShapeDtypeStruct((bsz, n * C_HEADS, LANES), F32))
            else:
                zeros = (0,) * (len(lead) + 2)
                out_specs.append(pl.BlockSpec((rps,) + lead + (HEAD_DIM, cache_n),
                                              lambda b, i, zeros=zeros: (i,) + zeros))
                out_shape.append(jax.ShapeDtypeStruct((reqs,) + lead + (HEAD_DIM, cache_n), F32))
    return pl.pallas_call(
        functools.partial(_odd_in_kernel, rope=rope, emit_cache=emit_cache, tm=tm),
        grid=(bsz, n // ts),
        in_specs=in_specs,
        out_specs=out_specs,
        out_shape=out_shape,
        compiler_params=_params(2),
        name="odd_in",
    )(*args)


QK_AHEAD = 3
SINK_AHEAD = 2


def _diff_attn_kernel(*refs, lam_init, tq, kc):
    n_src = (len(refs) - 4) // 2
    q_ref, (lamv_ref, g_ref, o_ref) = refs[0], refs[-3:]
    k_refs, vt_refs = refs[1:1 + n_src], refs[1 + n_src:1 + 2 * n_src]
    chunks = [(src, off) for src in range(n_src) for off in range(0, k_refs[src].shape[1], kc)]
    n_chunks = len(chunks)
    reqs = q_ref.shape[0]
    tiles = q_ref.shape[1] // tq
    heads = q_ref.shape[2] // LANES
    lane = lax.broadcasted_iota(jnp.int32, (tq, LANES), 1)
    lv = lamv_ref[...]
    lam = (jnp.exp(jnp.sum(lv[0:1] * lv[1:2], keepdims=True))
           - jnp.exp(jnp.sum(lv[2:3] * lv[3:4], keepdims=True)) + lam_init)

    qqs = {}

    def score(item):
        r, h, t, c = item
        cols = slice(h * LANES, (h + 1) * LANES)
        if (r, h, t) not in qqs:
            q = q_ref[r, t * tq:(t + 1) * tq, cols]
            zero = jnp.zeros_like(q)
            qqs[r, h, t] = jnp.concatenate([jnp.where(lane < HEAD_DIM, q, zero),
                                            jnp.where(lane >= HEAD_DIM, q, zero)], axis=0)
        src, off = chunks[c]
        return _dot_nt(k_refs[src][r, off:off + kc, cols], qqs[r, h, t])

    def finish(r, h, t, acc, l):
        inv = 1.0 / l
        ot = acc[:, :tq] * inv[:, :tq] - acc[:, tq:] * (inv[:, tq:] * lam)
        o = ot.T
        ms = jnp.mean(o * o, axis=-1, keepdims=True)
        y = o * lax.rsqrt(ms + EPS) * g_ref[...] * (1.0 - lam_init)
        o_ref[r, t * tq:(t + 1) * tq, h * LANES:(h + 1) * LANES] = y.astype(BF16)

    items = [(r, h, t, c) for r in range(reqs) for h in range(heads) for t in range(tiles)
             for c in range(n_chunks)]
    ahead = [score(it) for it in items[:QK_AHEAD]]
    m = acc = None
    for idx, (r, h, t, c) in enumerate(items):
        s = ahead.pop(0)
        if idx + QK_AHEAD < len(items):
            ahead.append(score(items[idx + QK_AHEAD]))
        cm = jnp.max(s, axis=0, keepdims=True)
        m_new = cm if c == 0 else jnp.maximum(m, cm)
        src, off = chunks[c]
        e = jnp.exp2(s - m_new)
        lsum = jnp.sum(e, axis=0, keepdims=True)
        lane0 = r * k_refs[src].shape[1] + off
        pv = _dot(vt_refs[src][h, :LANES, lane0:lane0 + kc], e.astype(BF16))
        if c == 0:
            acc, l = pv, lsum
        else:
            alpha = jnp.exp2(m - m_new)
            acc, l = acc * alpha + pv, l * alpha + lsum
        m = m_new
        if c == n_chunks - 1:
            finish(r, h, t, acc, l)


def _diff_attn_call(q, ks, vts, lamv, subln_g, lam_init, tq, tiles, heads, reqs=1, flat_values=False):
    bsz, n, _ = q.shape
    assert reqs == 1 or flat_values
    kc = next(c for c in (512, 256, 128) if all(k.shape[1] % c == 0 for k in ks))
    hw = heads * LANES
    k_specs = [pl.BlockSpec((reqs, k.shape[1], hw), lambda b, h, i: (b, 0, h)) for k in ks]
    vt_index = (lambda b, h, i: (0, h, 0, b)) if flat_values else (lambda b, h, i: (b, h, 0, 0))
    vt_specs = [pl.BlockSpec((None, heads, V_ROWS, reqs * k.shape[1]), vt_index) for k in ks]
    return pl.pallas_call(
        functools.partial(_diff_attn_kernel, lam_init=lam_init, tq=tq, kc=kc),
        grid=(bsz // reqs, C_HEADS // heads, n // (tq * tiles)),
        in_specs=[pl.BlockSpec((reqs, tiles * tq, hw), lambda b, h, i: (b, i, h))]
        + k_specs + vt_specs + [_const_spec((4, HEAD_DIM)), _const_spec((1, LANES))],
        out_specs=pl.BlockSpec((reqs, tiles * tq, hw), lambda b, h, i: (b, i, h)),
        out_shape=jax.ShapeDtypeStruct((bsz, n, C_W), BF16),
        compiler_params=_params(3),
        name="diff_attn",
    )(q, *ks, *vts, lamv, subln_g)


def _sink_attn_kernel(*refs, windowed, tq, nblk):
    if windowed:
        (q_ref, ck_ref, kp_ref, km_ref, kn_ref, cvt_ref, vp_ref, vm_ref, vn_ref,
         bias_ref, sink_ref, o_ref) = refs
        kband = jnp.concatenate([kp_ref[...], km_ref[...], kn_ref[...]], axis=0)
        vtband = jnp.concatenate([vp_ref[...], vm_ref[...], vn_ref[...]], axis=1)
    else:
        q_ref, k_ref, vt_ref, sink_ref, o_ref = refs
    reqs = q_ref.shape[0]
    subs = q_ref.shape[1] // tq
    groups = q_ref.shape[2] // (2 * LANES)
    lane = lax.broadcasted_iota(jnp.int32, (tq, LANES), 1)
    lo = lane < HEAD_DIM

    def scores_of(stream):
        r, g, sub = stream
        qd = q_ref[r, sub * tq:(sub + 1) * tq, g * 2 * LANES:(g + 1) * 2 * LANES]
        parts = []
        for c in range(2):
            ch = qd[:, c * LANES:(c + 1) * LANES]
            zero = jnp.zeros_like(ch)
            parts += [jnp.where(lo, ch, zero), jnp.where(lo, zero, ch)]
        qq = jnp.concatenate(parts, axis=0)
        if not windowed:
            return [_dot_nt(k_ref[r, :, g * LANES:(g + 1) * LANES], qq)]
        blk = pl.program_id(2) * subs + sub
        variant = jnp.where(blk == 0, 0, jnp.where(blk == nblk - 1, 2, 1))
        band = _dot_nt(kband[sub * tq:(sub + 3) * tq, :], qq) + bias_ref[variant]
        return [_dot_nt(ck_ref[...], qq), band]

    def finish(stream, scores):
        r, g, sub = stream
        values = ([cvt_ref[...], vtband[:, sub * tq:(sub + 3) * tq]] if windowed
                  else [vt_ref[g, :, r * tq:(r + 1) * tq]])
        sink = sink_ref[g]
        m = sink
        for s in scores:
            m = jnp.maximum(m, jnp.max(s, axis=0, keepdims=True))
        acc = None
        for s, vt in zip(scores, values):
            pv = _dot(vt, jnp.exp2(s - m).astype(BF16))
            acc = pv if acc is None else acc + pv
        inv = 1.0 / (acc[LANES:LANES + 1, :] + jnp.exp2(sink - m))
        ot = acc[:LANES, :] * inv
        for c in range(2):
            o0 = ot[:, (2 * c) * tq:(2 * c + 1) * tq].T
            o1 = ot[:, (2 * c + 1) * tq:(2 * c + 2) * tq].T
            cols = slice((2 * g + c) * LANES, (2 * g + c + 1) * LANES)
            o_ref[r, sub * tq:(sub + 1) * tq, cols] = jnp.where(lo, o0, o1).astype(BF16)

    streams = [(r, g, sub) for r in range(reqs) for g in range(groups) for sub in range(subs)]
    ahead = [scores_of(st) for st in streams[:SINK_AHEAD]]
    for idx, st in enumerate(streams):
        cur = ahead.pop(0)
        if idx + SINK_AHEAD < len(streams):
            ahead.append(scores_of(streams[idx + SINK_AHEAD]))
        finish(st, cur)


def _window_bias(n, tq):
    nblk = n // tq
    kk = np.arange(3 * tq)[:, None]
    qi = np.arange(tq)[None, :]
    band_ok = np.abs(qi + WINDOW - kk) <= WINDOW
    out = []
    for blk in (0, 1, nblk - 1):
        jpos = blk * tq - WINDOW + kk
        ok = band_ok & (jpos >= 0) & (jpos < n)
        out.append(np.tile(np.where(ok, 0.0, NEG_INF), (1, D_GROUP)))
    return jnp.asarray(np.stack(out), F32)


def _sink_attn_call(q, k, vt, sink_rows, ctx_k=None, ctx_vt=None, reqs=1, flat_values=False):
    bsz, n, _ = q.shape
    windowed = ctx_k is not None
    assert reqs == 1 or (flat_values and not windowed)
    tq = WINDOW if windowed else n
    nblk = n // tq
    subs = next(s for s in (16, 8, 4, 2, 1) if nblk % s == 0) if windowed else 1
    steps = nblk // subs
    groups = 1 if windowed else D_KV_HEADS
    q_spec = pl.BlockSpec((reqs, subs * tq, groups * 2 * LANES), lambda b, j, i: (b, i, j))
    sink_spec = pl.BlockSpec((groups, 1, D_GROUP * tq), lambda b, j, i: (j, 0, 0))
    if windowed:
        assert nblk >= 3, "first / interior / last mask variants need three query blocks"
        n_ctx = ctx_k.shape[1]
        prev = lambda i: jnp.maximum(i * subs - 1, 0)
        nxt = lambda i: jnp.minimum((i + 1) * subs, nblk - 1)
        in_specs = [
            q_spec,
            pl.BlockSpec((None, n_ctx, LANES), lambda b, j, i: (b, 0, j)),
            pl.BlockSpec((None, tq, LANES), lambda b, j, i: (b, prev(i), j)),
            pl.BlockSpec((None, subs * tq, LANES), lambda b, j, i: (b, i, j)),
            pl.BlockSpec((None, tq, LANES), lambda b, j, i: (b, nxt(i), j)),
            pl.BlockSpec((None, None, V_ROWS, n_ctx), lambda b, j, i: (b, j, 0, 0)),
            pl.BlockSpec((None, None, V_ROWS, tq), lambda b, j, i: (b, j, 0, prev(i))),
            pl.BlockSpec((None, None, V_ROWS, subs * tq), lambda b, j, i: (b, j, 0, i)),
            pl.BlockSpec((None, None, V_ROWS, tq), lambda b, j, i: (b, j, 0, nxt(i))),
            _const_spec((3, 3 * tq, D_GROUP * tq)),
            sink_spec,
        ]
        args = (q, ctx_k, k, k, k, ctx_vt, vt, vt, vt, _window_bias(n, tq), sink_rows)
    else:
        in_specs = [
            q_spec,
            pl.BlockSpec((reqs, n, groups * LANES), lambda b, j, i: (b, 0, j)),
            pl.BlockSpec((None, groups, V_ROWS, reqs * n),
                         (lambda b, j, i: (0, j, 0, b)) if flat_values else (lambda b, j, i: (b, j, 0, 0))),
            sink_spec,
        ]
        args = (q, k, vt, sink_rows)
    return pl.pallas_call(
        functools.partial(_sink_attn_kernel, windowed=windowed, tq=tq, nblk=nblk),
        grid=(bsz // reqs, D_KV_HEADS // groups, steps),
        in_specs=in_specs,
        out_specs=q_spec,
        out_shape=jax.ShapeDtypeStruct((bsz, n, DQ_W), BF16),
        compiler_params=_params(3),
        name="sink_attn",
    )(*args)


def _dup_heads(a):
    lead = a.shape[:-1]
    a = a.reshape(lead + (D_KV_HEADS, 1, HEAD_DIM))
    return jnp.broadcast_to(a, lead + (D_KV_HEADS, 2, HEAD_DIM)).reshape(lead + (DKV_DUP_W,))


def _odd_gains(c_qn_g, c_kn_g, d_qn_g, d_kn_g):
    return jnp.concatenate([
        jnp.tile(c_qn_g * (ATTN_SCALE * LOG2E), C_W // HEAD_DIM),
        jnp.tile(c_kn_g, C_W // HEAD_DIM),
        jnp.tile(d_qn_g * (ATTN_SCALE * LOG2E), DQ_W // HEAD_DIM),
        jnp.tile(d_kn_g, DKV_W // HEAD_DIM),
    ])[None, :]


def _value_rows(v):
    bsz, nk, w = v.shape
    vt = v.reshape(bsz, nk, w // LANES, LANES).transpose(0, 2, 3, 1)
    pad = jnp.zeros((bsz, w // LANES, V_ROWS - LANES, nk), v.dtype).at[:, :, 0, :].set(1)
    return jnp.concatenate([vt, pad], axis=2)


def _rope_tables(n):
    rows = n // GRID_W
    row = np.repeat(np.arange(rows), GRID_W).astype(np.float64)
    col = np.tile(np.arange(GRID_W), rows).astype(np.float64)
    freqs = ROPE_BASE ** (-np.arange(ROPE_FREQS, dtype=np.float64) / ROPE_FREQS)
    ang = np.stack([row[:, None] * freqs, col[:, None] * freqs], axis=1)
    ang = np.concatenate([ang, ang], axis=-1).reshape(n, HEAD_DIM)
    ang = np.concatenate([ang, ang], axis=-1)
    return jnp.asarray(np.cos(ang), F32), jnp.asarray(np.sin(ang), F32)


MLP_TILE = 512


def _token_tile(n):
    return min(n, 512)


def _mixer(x, l, mod, row_of_batch, wts, caches, new_cache, side_cast=None):
    bsz, n, _ = x.shape
    decode = caches is not None
    g1n = wts["norm1_g"][l][None, :]
    if l % 2 == 0:
        e = l // 2
        a, z, *copies = _even_in_call(x, mod, l, row_of_batch, g1n, e, wts["w_in_even"], wts["w_pool"],
                                      wts["pool_scale"][e][None, :], wts["dftc"], wts["w_fft"], min(n, 1024),
                                      1 if decode or n > 1024 else next(q for q in (4, 2, 1) if bsz % q == 0),
                                      side_cast)
        y = _dft_two_stage_call(z) if z.shape[-1] == 2 * B_WIDTH else z
        return (a, y) + tuple(copies)
    o = l // 2
    lam_init = _lam_init(l)
    rope_tabs = _rope_tables(n) if decode else None
    xo = x if decode else x.reshape(1, bsz * n, D_MODEL)
    outs = _odd_in_call(xo, mod, l, row_of_batch, g1n, wts["w_in_odd"], o, wts["odd_gains"][o],
                        wts["head_mean"], rope_tabs, _token_tile(xo.shape[1]),
                        None if decode else n)
    if not decode:
        outs = [a if a.ndim == 4 else a.reshape(bsz, n, a.shape[-1]) for a in outs[:6]] + list(outs[6:])
    cq, ck, cvt, dq, dkk, dvt = outs[:6]
    if decode:
        c_k, c_v, d_k, d_v = caches
        lc = c_k.shape[2]
        ks = [c_k[:, o].reshape(bsz, lc, C_W).astype(BF16), ck]
        vts = [_value_rows(c_v[:, o].reshape(bsz, lc, C_W).astype(BF16)), cvt]
        ctx_k = _dup_heads(d_k[:, o].reshape(bsz, lc, DKV_W)).astype(BF16)
        ctx_vt = _value_rows(_dup_heads(d_v[:, o].reshape(bsz, lc, DKV_W)).astype(BF16))
        tq = min(n, 256)
    else:
        new_cache.append(outs[6:])
        ks, vts = [ck], [cvt]
        ctx_k = ctx_vt = None
        tq = n
    tiles = next(t for t in (8, 4, 2, 1) if n % (t * tq) == 0)
    heads = 1 if decode else C_HEADS
    c_out = _diff_attn_call(cq, ks, vts, wts["lamv"][o],
                            wts["c_subln_g"][o][None, :], lam_init, tq, tiles, heads,
                            1 if decode else next(r for r in (4, 2, 1) if bsz % r == 0), not decode)
    sink_tq = WINDOW if decode else n
    sink_rows = jnp.repeat(wts["d_sink"][o].reshape(D_KV_HEADS, D_GROUP) * LOG2E, sink_tq,
                           axis=-1)[:, None, :]
    d_out = _sink_attn_call(dq, dkk, dvt, sink_rows, ctx_k, ctx_vt,
                            1 if decode else next(r for r in (4, 2, 1) if bsz % r == 0), not decode)
    return c_out, d_out


def kernel(x_prompt, x_sample, c, cache_c_k, cache_c_v, cache_d_k, cache_d_v, c_ctx,
           norm1_g, norm2_g, w_ada, b_ada, w_in_even, w_pool, pool_scale, w_fft, w_out_even,
           w_in_odd, c_qn_g, c_kn_g, lam_q1, lam_k1, lam_q2, lam_k2, c_subln_g,
           d_qn_g, d_kn_g, d_sink, w_out_odd, w_mlp1, w_mlp2):
    depth = norm1_g.shape[0]
    n_odd = w_in_odd.shape[0]
    dec_b = c.shape[0]

    rows = -(-(1 + dec_b) // SUBLANES) * SUBLANES
    cond = jnp.concatenate([c_ctx[None, :], c, jnp.zeros((rows - 1 - dec_b, D_MODEL), F32)], axis=0)
    mod = _adaln_call(cond, w_ada, b_ada).reshape(depth, rows, 6, 1, D_MODEL)

    ic = np.arange(GROUP_W)
    ang = 2.0 * np.pi * np.outer(ic, ic) / GROUP_W
    dftc = _bf16_const(np.concatenate([np.cos(ang), -np.sin(ang)], axis=1))
    head_mean = _bf16_const(np.kron(np.eye(2 * LANES // HEAD_DIM), np.ones((HEAD_DIM, HEAD_DIM))) / HEAD_DIM)

    odd_gains = [_odd_gains(c_qn_g[o], c_kn_g[o], d_qn_g[o], d_kn_g[o]) for o in range(n_odd)]
    wts = {
        "norm1_g": norm1_g, "norm2_g": norm2_g,
        "w_in_even": w_in_even, "w_pool": w_pool, "pool_scale": pool_scale,
        "w_fft": w_fft, "w_out_even": w_out_even, "dftc": dftc,
        "w_in_odd": w_in_odd, "odd_gains": odd_gains, "head_mean": head_mean,
        "lamv": jnp.stack([lam_q1, lam_k1, lam_q2, lam_k2], axis=1),
        "c_subln_g": c_subln_g, "d_sink": d_sink, "w_out_odd": w_out_odd,
    }

    bsz, n, _ = x_prompt.shape
    caches = (cache_c_k, cache_c_v, cache_d_k, cache_d_v)
    new_cache = []
    xp, xs = x_prompt, x_sample
    mlp_stacks = (w_mlp1, w_mlp2)
    mlp_w = None
    for l in range(depth):
        pp = _mixer(xp, l, mod, lambda b: 0, wts, None, new_cache)
        ps = _mixer(xs, l, mod, lambda b: 1 + b, wts, caches, new_cache,
                    (mlp_stacks, 0) if l == 0 else None)
        if l == 0:
            mlp_w = ps[2:]
        w_out, out_idx = (wts["w_out_even"], l // 2) if l % 2 == 0 else (wts["w_out_odd"], l // 2)
        flat = lambda a: a.reshape(1, bsz * n, a.shape[-1])
        xp, xs, *mlp_next = _out_mlp_call((flat(xp), flat(pp[0]), flat(pp[1]), 0), (xs, ps[0], ps[1], 1),
                                          mod, l, wts["norm2_g"][l][None, :], w_out, out_idx, *mlp_w,
                                          (mlp_stacks, l + 1) if l + 1 < depth else None)
        mlp_w = mlp_next
        xp = xp.reshape(bsz, n, D_MODEL)
    y_prompt, y_sample = xp, xs
    new_c_k = jnp.stack([nc[0].transpose(0, 4, 1, 2, 3) for nc in new_cache], axis=1)
    new_c_v = jnp.stack([nc[1].reshape(bsz, n, C_HEADS, 2 * HEAD_DIM) for nc in new_cache], axis=1)
    new_d_k = jnp.stack([nc[2].transpose(0, 3, 1, 2) for nc in new_cache], axis=1)
    new_d_v = jnp.stack([nc[3].transpose(0, 3, 1, 2) for nc in new_cache], axis=1)
    return (y_prompt, y_sample, new_c_k, new_c_v, new_d_k, new_d_v)
```

```python
import functools
import math

import numpy as np
import jax
import jax.numpy as jnp
from jax import lax
from jax.experimental import pallas as pl
from jax.experimental.pallas import tpu as pltpu

F32 = jnp.float32
BF16 = jnp.bfloat16

D_MODEL = 1024
HEAD_DIM = 64
LANES = 128
SUBLANES = 8
GRID_W = 64
ROPE_FREQS = HEAD_DIM // 4
ROPE_BASE = 10000.0
EPS = 1e-6
NEG_INF = -1e30
WINDOW = 128
A_WIDTH = D_MODEL // 2
B_WIDTH = D_MODEL // 2
POOL_WINDOWS = (2, 4, 8, 16)
POOL_HALO = 8
N_GROUPS = 4
GROUP_W = A_WIDTH // N_GROUPS
C_HEADS = 4
D_HEADS = 8
D_KV_HEADS = 2
D_GROUP = D_HEADS // D_KV_HEADS
C_W = C_HEADS * 2 * HEAD_DIM
DQ_W = D_HEADS * HEAD_DIM
DKV_W = D_KV_HEADS * HEAD_DIM
DKV_DUP_W = 2 * DKV_W
D_FF = 4 * D_MODEL
ATTN_SCALE = HEAD_DIM ** -0.5
LOG2E = math.log2(math.e)
V_ROWS = LANES + 16
ODD_CV0 = 2 * C_W
ODD_DQ0 = 3 * C_W
ODD_DK0 = ODD_DQ0 + DQ_W
ODD_DV0 = ODD_DK0 + DKV_W
ODD_W = ODD_DV0 + DKV_W
ODD_NORMED_W = 2 * C_W + DQ_W + DKV_W
VMEM_LIMIT = 56 * 1024 * 1024


def _lam_init(layer):
    return 0.8 - 0.6 * math.exp(-0.3 * layer)


def _params(n_axes):
    return pltpu.CompilerParams(dimension_semantics=("arbitrary",) * n_axes,
                                vmem_limit_bytes=VMEM_LIMIT)


def _const_spec(shape):
    nd = len(shape)
    return pl.BlockSpec(shape, lambda *_: (0,) * nd, pipeline_mode=pl.Buffered(1))


def _stacked_spec(shape, idx):
    nd = len(shape)
    return pl.BlockSpec((None,) + tuple(shape), lambda *_: (idx,) + (0,) * nd, pipeline_mode=pl.Buffered(1))


def _side_cast_specs(stacks, layer, n_steps, step_of):
    specs, shapes = [], []
    for w in stacks:
        axis = 2 if w.shape[2] >= w.shape[1] else 1
        nb = 1
        while nb * 2 <= n_steps and w.shape[axis] % (nb * 2 * 2 * LANES) == 0:
            nb *= 2
        block = [None, w.shape[1], w.shape[2]]
        block[axis] //= nb

        def index(*grid, axis=axis, nb=nb):
            blk = jnp.minimum(step_of(*grid), nb - 1)
            return (layer, 0, blk) if axis == 2 else (layer, blk, 0)

        specs.append((pl.BlockSpec(tuple(block), index),
                      pl.BlockSpec(tuple(block[1:]), lambda *grid, index=index: index(*grid)[1:])))
        shapes.append(jax.ShapeDtypeStruct(w.shape[1:], BF16))
    return specs, shapes


def _mod_spec(layer, which, row_of_batch):
    return pl.BlockSpec((None, None, None, 1, D_MODEL),
                        lambda b, i: (layer, row_of_batch(b), which, 0, 0))


def _rms_mod(x, g, sc, sh):
    ms = jnp.mean(x * x, axis=-1, keepdims=True)
    return (x * lax.rsqrt(ms + EPS) * g) * (1.0 + sc) + sh


def _bf16_const(a):
    return jnp.asarray(a, F32).astype(BF16)


def _dot_nt(a, b):
    return lax.dot_general(a, b, (((1,), (1,)), ((), ())), preferred_element_type=F32)


def _dot(a, b):
    return jnp.dot(a, b, preferred_element_type=F32)


def _adaln_kernel(cond_ref, w_ref, b_ref, o_ref):
    cnd = cond_ref[...]
    s = (cnd * jax.nn.sigmoid(cnd)).astype(BF16)
    o_ref[...] = _dot(s, w_ref[...].astype(BF16)) + b_ref[...]


def _adaln_call(cond, w_ada, b_ada):
    depth = w_ada.shape[0]
    rows = cond.shape[0]
    tn = 3 * D_MODEL
    return pl.pallas_call(
        _adaln_kernel,
        grid=(depth, 6 * D_MODEL // tn),
        in_specs=[
            pl.BlockSpec((rows, D_MODEL), lambda l, j: (0, 0)),
            pl.BlockSpec((None, D_MODEL, tn), lambda l, j: (l, 0, j)),
            pl.BlockSpec((None, 1, tn), lambda l, j: (l, 0, j)),
        ],
        out_specs=pl.BlockSpec((None, rows, tn), lambda l, j: (l, 0, j)),
        out_shape=jax.ShapeDtypeStruct((depth, rows, 6 * D_MODEL), F32),
        compiler_params=_params(2),
        name="adaln",
    )(cond, w_ada, b_ada.reshape(depth, 1, 6 * D_MODEL))


def _even_in_kernel(*refs, tm, n, n_cast, dft_scale):
    n_in = 11 if dft_scale is None else 12
    (x_ref, xp_ref, xn_ref, sh_ref, sc_ref, g_ref, win_ref, wpool_ref,
     pscale_ref, dftc_ref, wfft_ref) = refs[:11]
    a_ref, z_ref = refs[n_in + n_cast:n_in + 2 + n_cast]
    for w_ref, o_ref in zip(refs[n_in:n_in + n_cast], refs[n_in + 2 + n_cast:]):
        o_ref[...] = w_ref[...].astype(BF16)
    i = pl.program_id(1)
    n_tiles = n // tm
    g, sc, sh = g_ref[...], sc_ref[...], sh_ref[...]
    seqs = x_ref.shape[0]
    h = _rms_mod(x_ref[...].reshape(seqs * tm, D_MODEL), g, sc, sh).astype(BF16)
    w_in = win_ref[...].astype(BF16)
    u = _dot(h, w_in)
    xh = jnp.concatenate([xp_ref[...], xn_ref[...]], axis=0)
    hh = _rms_mod(xh, g, sc, sh).astype(BF16)
    uh = _dot(hh, w_in[:, :A_WIDTH])
    has_prev = (i > 0).astype(F32)
    has_next = (i < n_tiles - 1).astype(F32)
    rows = tm + 2 * POOL_HALO

    planes = []
    for gi in range(N_GROUPS):
        ub = u[:, A_WIDTH + gi * GROUP_W:A_WIDTH + (gi + 1) * GROUP_W].astype(BF16)
        cs = _dot(ub, dftc_ref[...])
        w_fft = wfft_ref[gi].astype(BF16)
        zr = _dot(cs[:, :GROUP_W].astype(BF16), w_fft)
        zi = _dot(cs[:, GROUP_W:].astype(BF16), w_fft)
        planes.append((zr.astype(BF16), zi.astype(BF16)))
    for q in range(seqs):
        sl = slice(q * tm, (q + 1) * tm)
        zr = jnp.concatenate([p[0][sl] for p in planes], axis=1)
        zi = jnp.concatenate([p[1][sl] for p in planes], axis=1)
        if dft_scale is None:
            z_ref[q, :, :B_WIDTH] = zr
            z_ref[q, :, B_WIDTH:] = zi
        else:
            y = _dot(refs[11][...], jnp.concatenate([zr, zi], axis=0)) * dft_scale
            z_ref[q] = y.astype(BF16)

    t = i * tm + lax.broadcasted_iota(jnp.int32, (tm, 1), 0)
    for q, (gi, w) in ((q, gw) for q in range(seqs) for gw in enumerate(POOL_WINDOWS)):
        cols = slice(gi * GROUP_W, (gi + 1) * GROUP_W)
        uq = u[q * tm:(q + 1) * tm, cols]
        ext = jnp.concatenate([uh[0:POOL_HALO, cols] * has_prev, uq,
                               uh[POOL_HALO:, cols] * has_next], axis=0)
        s = ext
        span = 1
        while span < w:
            s = s + pltpu.roll(s, span, 0)
            span *= 2
        shift = w // 2 - 1
        s = pltpu.roll(s, rows - shift, 0) if shift else s
        acc = s[POOL_HALO:POOL_HALO + tm]
        lo = jnp.maximum(t - w // 2, 0)
        hi = jnp.minimum(t + w // 2, n)
        cnt = (hi - lo).astype(F32)
        pooled = acc / cnt - uq
        y = _dot(pooled.astype(BF16), wpool_ref[gi].astype(BF16)) * pscale_ref[:, cols]
        a_ref[q, :, cols] = y.astype(BF16)


def _even_in_call(x, mod, layer, row_of_batch, g, e, w_in, w_pool, pool_scale, dftc, w_fft, tm, seqs,
                  side_cast=None):
    bsz, n, _ = x.shape
    assert seqs == 1 or tm == n
    hb = tm // POOL_HALO
    tiles = n // tm
    stacks, cast_layer = side_cast if side_cast else ((), 0)
    cast_specs, cast_shapes = _side_cast_specs(stacks, cast_layer, (bsz // seqs) * tiles,
                                               lambda b, i: b * tiles + i)
    fused = tm == n
    if fused:
        k = np.arange(n)
        ang = 2.0 * np.pi * np.outer(k, k) / n
        dft_args = [_bf16_const(np.concatenate([np.cos(ang), np.sin(ang)], axis=1))]
        dft_specs = [_const_spec((n, 2 * n))]
    else:
        dft_args, dft_specs = [], []
    kern = functools.partial(_even_in_kernel, tm=tm, n=n, n_cast=len(stacks),
                             dft_scale=float((n * GROUP_W) ** -0.5) if fused else None)
    second_w = B_WIDTH if fused else 2 * B_WIDTH
    return pl.pallas_call(
        kern,
        grid=(bsz // seqs, n // tm),
        in_specs=[
            pl.BlockSpec((seqs, tm, D_MODEL), lambda b, i: (b, i, 0)),
            pl.BlockSpec((None, POOL_HALO, D_MODEL),
                         lambda b, i: (b * seqs, jnp.maximum(i * hb - 1, 0), 0)),
            pl.BlockSpec((None, POOL_HALO, D_MODEL),
                         lambda b, i: (b * seqs, jnp.minimum((i + 1) * hb, n // POOL_HALO - 1), 0)),
            _mod_spec(layer, 0, row_of_batch),
            _mod_spec(layer, 1, row_of_batch),
            _const_spec((1, D_MODEL)),
            _stacked_spec((D_MODEL, D_MODEL), e),
            _stacked_spec((N_GROUPS, GROUP_W, GROUP_W), e),
            _const_spec((1, A_WIDTH)),
            _const_spec((GROUP_W, 2 * GROUP_W)),
            _stacked_spec((N_GROUPS, GROUP_W, GROUP_W), e),
        ] + dft_specs + [i_spec for i_spec, _ in cast_specs],
        out_specs=[
            pl.BlockSpec((seqs, tm, A_WIDTH), lambda b, i: (b, i, 0)),
            pl.BlockSpec((seqs, tm, second_w), lambda b, i: (b, i, 0)),
        ] + [o_spec for _, o_spec in cast_specs],
        out_shape=[
            jax.ShapeDtypeStruct((bsz, n, A_WIDTH), BF16),
            jax.ShapeDtypeStruct((bsz, n, second_w), BF16),
        ] + cast_shapes,
        compiler_params=_params(2),
        name="even_in",
    )(x, x, x, mod, mod, g, w_in, w_pool, pool_scale, dftc, w_fft, *dft_args, *stacks)


DFT_ROWS = 16
DFT_PITCH = 24


def _pitch_rows(x):
    g, _, w = x.shape
    pad = jnp.zeros((g, DFT_PITCH - DFT_ROWS, w), x.dtype)
    return jnp.concatenate([x, pad], axis=1).reshape(g * DFT_PITCH, w)


def _dft_kernel(z_ref, m1_ref, twr_ref, twi_ref, m2_ref, y_ref, zs_ref, os_ref, ys_ref, bs_ref, *, na, nb, scale):
    j = pl.program_id(1)
    n_s1 = nb // DFT_ROWS
    nt = 2 * B_WIDTH // LANES

    @pl.when((pl.program_id(0) == 0) & (j == 0))
    def _():
        os_ref[...] = jnp.zeros_like(os_ref)
        ys_ref[...] = jnp.zeros_like(ys_ref)

    @pl.when(j < n_s1)
    def _():
        zf = _pitch_rows(z_ref[...].astype(F32))
        for k in range(nt):
            zs_ref[k] = zf[:, k * LANES:(k + 1) * LANES]
        for bi in range(DFT_ROWS):
            rows = pl.ds(bi, na, stride=DFT_PITCH)
            zb = [zs_ref[k, rows, :].astype(BF16) for k in range(nt)]
            rhs = jnp.concatenate([jnp.concatenate(zb[:nt // 2], axis=1),
                                   jnp.concatenate(zb[nt // 2:], axis=1)], axis=0)
            r = _dot(m1_ref[...], rhs)
            br, bim = r[:na], r[na:]
            tr, ti = twr_ref[bi], twi_ref[bi]
            out = (br * tr - bim * ti, br * ti + bim * tr)
            for k in range(nt):
                half, kk = divmod(k, nt // 2)
                os_ref[k, rows, :] = out[half][:, kk * LANES:(kk + 1) * LANES]
        of = jnp.concatenate([os_ref[k] for k in range(nt)], axis=1)
        bs_ref[j] = of.reshape(na, DFT_PITCH, 2 * B_WIDTH)[:, :DFT_ROWS].astype(BF16)

    @pl.when(j >= n_s1)
    def _():
        for di in range(DFT_ROWS):
            d = (j - n_s1) * DFT_ROWS + di
            bb = jnp.concatenate([bs_ref[jj, d] for jj in range(n_s1)], axis=0)
            rhs = jnp.concatenate([bb[:, :B_WIDTH], bb[:, B_WIDTH:]], axis=0)
            y = _dot(m2_ref[...], rhs) * scale
            for k in range(B_WIDTH // LANES):
                ys_ref[k, pl.ds(di, nb, stride=DFT_PITCH), :] = y[:, k * LANES:(k + 1) * LANES]
        yf = jnp.concatenate([ys_ref[k] for k in range(B_WIDTH // LANES)], axis=1)
        y_ref[...] = yf.reshape(nb, DFT_PITCH, B_WIDTH)[:, :DFT_ROWS].astype(BF16)


def _dft_two_stage_call(z):
    bsz, n, _ = z.shape
    na = 1 << (int(math.log2(n)) // 2)
    nb = n // na
    ia = np.arange(na)
    ib = np.arange(nb)
    ang_a = 2.0 * np.pi * np.outer(ia, ia) / na
    fr, fi = np.cos(ang_a), -np.sin(ang_a)
    m1 = _bf16_const(np.block([[fr, -fi], [fi, fr]]))
    ang_t = 2.0 * np.pi * np.outer(ib, ia) / n
    twr = jnp.asarray(np.cos(ang_t)[:, :, None], F32)
    twi = jnp.asarray(-np.sin(ang_t)[:, :, None], F32)
    ang_b = 2.0 * np.pi * np.outer(ib, ib) / nb
    m2 = _bf16_const(np.concatenate([np.cos(ang_b), np.sin(ang_b)], axis=1))

    n_s1, n_s2 = nb // DFT_ROWS, na // DFT_ROWS
    s1_blk = lambda j: jnp.minimum(j, n_s1 - 1)
    s2_blk = lambda j: jnp.maximum(j - n_s1, 0)
    y = pl.pallas_call(
        functools.partial(_dft_kernel, na=na, nb=nb, scale=float((n * GROUP_W) ** -0.5)),
        grid=(bsz, n_s1 + n_s2),
        in_specs=[
            pl.BlockSpec((None, na, DFT_ROWS, 2 * B_WIDTH), lambda b, j: (b, 0, s1_blk(j), 0)),
            _const_spec((2 * na, 2 * na)),
            pl.BlockSpec((DFT_ROWS, na, 1), lambda b, j: (s1_blk(j), 0, 0)),
            pl.BlockSpec((DFT_ROWS, na, 1), lambda b, j: (s1_blk(j), 0, 0)),
            _const_spec((nb, 2 * nb)),
        ],
        out_specs=pl.BlockSpec((None, nb, DFT_ROWS, B_WIDTH), lambda b, j: (b, 0, s2_blk(j), 0)),
        out_shape=jax.ShapeDtypeStruct((bsz, nb, na, B_WIDTH), BF16),
        scratch_shapes=[
            pltpu.VMEM((2 * B_WIDTH // LANES, na * DFT_PITCH, LANES), F32),
            pltpu.VMEM((2 * B_WIDTH // LANES, na * DFT_PITCH, LANES), F32),
            pltpu.VMEM((B_WIDTH // LANES, nb * DFT_PITCH, LANES), F32),
            pltpu.VMEM((n_s1, na, DFT_ROWS, 2 * B_WIDTH), BF16),
        ],
        compiler_params=_params(2),
        name="dft_two_stage",
    )(z.reshape(bsz, na, nb, 2 * B_WIDTH), m1, twr, twi, m2)
    return y.reshape(bsz, n, B_WIDTH)


FF_CHUNK = 1024


def _out_mlp_kernel(*refs, steps_a, n_cast):
    (xa_ref, p1a_ref, p2a_ref, xb_ref, p1b_ref, p2b_ref, g1_ref, sh_ref, sc_ref, g2_ref,
     ng_ref, wout_ref, w1_ref, w2_ref) = refs[:14]
    oa_ref, ob_ref = refs[14 + n_cast:16 + n_cast]
    for w_ref, o_ref in zip(refs[14:14 + n_cast], refs[16 + n_cast:]):
        o_ref[...] = w_ref[...].astype(BF16)

    def body(x_ref, p1_ref, p2_ref, o_ref):
        half = p1_ref.shape[-1]
        mix = (_dot(p1_ref[...], wout_ref[:half, :].astype(BF16))
               + _dot(p2_ref[...], wout_ref[half:, :].astype(BF16)))
        x1 = x_ref[...] + g1_ref[...] * mix
        h = _rms_mod(x1, ng_ref[...], sc_ref[...], sh_ref[...]).astype(BF16)
        acc = None
        for c in range(D_FF // FF_CHUNK):
            a = _dot(h, w1_ref[:, c * FF_CHUNK:(c + 1) * FF_CHUNK])
            a = jnp.square(jnp.maximum(a, 0.0)).astype(BF16)
            part = _dot(a, w2_ref[c * FF_CHUNK:(c + 1) * FF_CHUNK, :])
            acc = part if acc is None else acc + part
        o_ref[...] = x1 + g2_ref[...] * acc

    @pl.when(pl.program_id(0) < steps_a)
    def _():
        body(xa_ref, p1a_ref, p2a_ref, oa_ref)

    @pl.when(pl.program_id(0) >= steps_a)
    def _():
        body(xb_ref, p1b_ref, p2b_ref, ob_ref)


def _out_mlp_call(set_a, set_b, mod, layer, ng, w_out, out_idx, w1, w2, side_cast=None):
    (xa, p1a, p2a, row_a), (xb, p1b, p2b, row_b) = set_a, set_b
    half = p1a.shape[-1]
    tm = min(xa.shape[1], xb.shape[1], MLP_TILE)
    tiles_a, tiles_b = xa.shape[1] // tm, xb.shape[1] // tm
    steps_a, steps_b = xa.shape[0] * tiles_a, xb.shape[0] * tiles_b

    def pos_a(s):
        sa = jnp.minimum(s, steps_a - 1)
        return sa // tiles_a, sa % tiles_a

    def pos_b(s):
        sb = jnp.maximum(s - steps_a, 0)
        return sb // tiles_b, sb % tiles_b

    tok_a = lambda w: pl.BlockSpec((None, tm, w), lambda s: pos_a(s) + (0,))
    tok_b = lambda w: pl.BlockSpec((None, tm, w), lambda s: pos_b(s) + (0,))
    row = lambda s: jnp.where(s < steps_a, row_a + pos_a(s)[0], row_b + pos_b(s)[0])
    mod_spec = lambda which: pl.BlockSpec((None, None, None, 1, D_MODEL),
                                          lambda s: (layer, row(s), which, 0, 0))
    stacks, cast_layer = side_cast if side_cast else ((), 0)
    cast_specs, cast_shapes = _side_cast_specs(stacks, cast_layer, steps_a + steps_b, lambda s: s)
    return pl.pallas_call(
        functools.partial(_out_mlp_kernel, steps_a=steps_a, n_cast=len(stacks)),
        grid=(steps_a + steps_b,),
        in_specs=[
            tok_a(D_MODEL), tok_a(half), tok_a(half), tok_b(D_MODEL), tok_b(half), tok_b(half),
            mod_spec(2), mod_spec(3), mod_spec(4), mod_spec(5),
            _const_spec((1, D_MODEL)),
            _stacked_spec((2 * half, D_MODEL), out_idx),
            _const_spec((D_MODEL, D_FF)),
            _const_spec((D_FF, D_MODEL)),
        ] + [i_spec for i_spec, _ in cast_specs],
        out_specs=[tok_a(D_MODEL), tok_b(D_MODEL)] + [o_spec for _, o_spec in cast_specs],
        out_shape=[jax.ShapeDtypeStruct(xa.shape, F32), jax.ShapeDtypeStruct(xb.shape, F32)] + cast_shapes,
        compiler_params=_params(1),
        name="out_mlp",
    )(xa, p1a, p2a, xb, p1b, p2b, mod, mod, mod, mod, ng, w_out, w1, w2, *stacks)


def _odd_in_kernel(*refs, rope, emit_cache, tm):
    x_ref, sh_ref, sc_ref, g_ref, win_ref, gain_ref, hm_ref = refs[:7]
    k = 7
    if rope:
        cos_ref, sin_ref = refs[k:k + 2]
        k += 2
    cq_ref, ck_ref, cvt_ref, dq_ref, dk_ref, dvt_ref = refs[k:k + 6]
    k += 6
    if emit_cache:
        nck_ref, ncv_ref, ndk_ref, ndv_ref = refs[k:k + 4]
    subs = x_ref.shape[0] // tm
    lane = lax.broadcasted_iota(jnp.int32, (tm, LANES), 1)
    first_half = (lane % (2 * ROPE_FREQS)) < ROPE_FREQS
    lo = lane < HEAD_DIM
    pad_row = lax.broadcasted_iota(jnp.int32, (V_ROWS - LANES, tm), 0)
    pad = jnp.where(pad_row == 0, 1.0, 0.0).astype(BF16)

    def cache_rows(ref, sub, r):
        rpt = ref.shape[0] // x_ref.shape[0]
        return ref.at[pl.ds(sub * tm * rpt + r, tm, stride=rpt), :]

    def cache_cols(ref, sub, lead, mat):
        n_req = ref.shape[-1]
        for r in range(tm // n_req):
            ref[(sub * (tm // n_req) + r,) + lead] = mat[:, r * n_req:(r + 1) * n_req]

    groups = ((0, ODD_CV0), (ODD_DQ0, ODD_W), (ODD_CV0, ODD_DQ0))
    gain_lo = (0, ODD_CV0)
    normed_w = (ODD_CV0, DQ_W + DKV_W)
    hidden = {}

    def project(item):
        sub, grp = item
        if sub not in hidden:
            x = x_ref[sub * tm:(sub + 1) * tm, :]
            hidden[sub] = _rms_mod(x, g_ref[...], sc_ref[...], sh_ref[...]).astype(BF16)
        lo, hi = groups[grp]
        return _dot(hidden[sub], win_ref[:, lo:hi].astype(BF16))

    def head_norms(p, lo):
        normed = []
        for c0 in range(0, p.shape[1], 2 * LANES):
            wd = min(2 * LANES, p.shape[1] - c0)
            v = p[:, c0:c0 + wd]
            ms = _dot((v * v).astype(BF16), hm_ref[:wd, :wd])
            y = v * lax.rsqrt(ms + EPS) * gain_ref[:, lo + c0:lo + c0 + wd]
            normed += [y[:, k:k + LANES] for k in range(0, wd, LANES)]
        return normed

    def finish(item, p, normed):
        sub, grp = item
        rows = slice(sub * tm, (sub + 1) * tm)

        def roped(y):
            if not rope:
                return y
            rot = jnp.where(first_half, -pltpu.roll(y, LANES - ROPE_FREQS, 1),
                            pltpu.roll(y, ROPE_FREQS, 1))
            return y * cos_ref[rows, :] + rot * sin_ref[rows, :]

        def put(ref, chunks):
            for c, y in enumerate(chunks):
                ref[rows, c * LANES:(c + 1) * LANES] = roped(y).astype(BF16)

        nq = C_W // LANES
        if grp == 0:
            put(cq_ref, normed[:nq])
            put(ck_ref, normed[nq:])
            if emit_cache:
                for c, y in enumerate(normed[nq:]):
                    yt = y.T
                    for half in range(2):
                        cache_cols(nck_ref, sub, (c, half), yt[half * HEAD_DIM:(half + 1) * HEAD_DIM])
        elif grp == 1:
            put(dq_ref, normed[:nq])
            dk = normed[nq]
            y = roped(dk)
            swapped = pltpu.roll(y, HEAD_DIM, 1)
            dk_ref[rows, :LANES] = jnp.where(lo, y, swapped).astype(BF16)
            dk_ref[rows, LANES:] = jnp.where(lo, swapped, y).astype(BF16)
            if emit_cache:
                dkt = dk.T
                for j in range(D_KV_HEADS):
                    cache_cols(ndk_ref, sub, (j,), dkt[j * HEAD_DIM:(j + 1) * HEAD_DIM])
            dvt = p[:, normed_w[1]:].T
            for j in range(D_KV_HEADS):
                vj = dvt[j * HEAD_DIM:(j + 1) * HEAD_DIM]
                dvt_ref[j, :LANES, rows] = jnp.concatenate([vj, vj], axis=0).astype(BF16)
                dvt_ref[j, LANES:, rows] = pad
                if emit_cache:
                    cache_cols(ndv_ref, sub, (j,), vj)
        else:
            cv = p
            for hh in range(C_HEADS):
                cvt_ref[hh, :LANES, rows] = cv[:, hh * LANES:(hh + 1) * LANES].T.astype(BF16)
                cvt_ref[hh, LANES:, rows] = pad
            if emit_cache:
                for hh in range(C_HEADS):
                    cache_rows(ncv_ref, sub, hh)[...] = cv[:, hh * LANES:(hh + 1) * LANES]

    items = [(sub, grp) for sub in range(subs) for grp in range(len(groups))]
    p = project(items[0])
    for idx, item in enumerate(items):
        normed = head_norms(p[:, :normed_w[item[1]]], gain_lo[item[1]]) if item[1] < 2 else None
        nxt = project(items[idx + 1]) if idx + 1 < len(items) else None
        finish(item, p, normed)
        p = nxt


def _odd_in_call(x, mod, layer, row_of_batch, g, w_in, odd_idx, gains, head_mean, rope_tabs, tm, cache_n=None):
    emit_cache = cache_n is not None
    bsz, n, _ = x.shape
    rope = rope_tabs is not None
    subs = 2 if n % (2 * tm) == 0 else 1
    ts = subs * tm
    tok = lambda w: pl.BlockSpec((None, ts, w), lambda b, i: (b, i, 0))
    in_specs = [
        tok(D_MODEL),
        _mod_spec(layer, 0, row_of_batch),
        _mod_spec(layer, 1, row_of_batch),
        _const_spec((1, D_MODEL)),
        _stacked_spec((D_MODEL, ODD_W), odd_idx),
        _const_spec((1, ODD_NORMED_W)),
        _const_spec((2 * LANES, 2 * LANES)),
    ]
    args = [x, mod, mod, g, w_in, gains, head_mean]
    if rope:
        in_specs += [pl.BlockSpec((ts, LANES), lambda b, i: (i, 0))] * 2
        args += list(rope_tabs)
    vt_spec = lambda heads: pl.BlockSpec((None, heads, V_ROWS, ts), lambda b, i: (b, 0, 0, i))
    vt_shape = lambda heads: jax.ShapeDtypeStruct((bsz, heads, V_ROWS, n), BF16)
    tok_shape = lambda w: jax.ShapeDtypeStruct((bsz, n, w), BF16)
    out_specs = [tok(C_W), tok(C_W), vt_spec(C_HEADS), tok(DQ_W), tok(DKV_DUP_W), vt_spec(D_KV_HEADS)]
    out_shape = [tok_shape(C_W), tok_shape(C_W), vt_shape(C_HEADS), tok_shape(DQ_W),
                 tok_shape(DKV_DUP_W), vt_shape(D_KV_HEADS)]
    if emit_cache:
        assert bsz == 1 and tm % cache_n == 0 and n % cache_n == 0
        reqs, rps = n // cache_n, ts // cache_n
        for lead in ((C_HEADS, 2), None, (D_KV_HEADS,), (D_KV_HEADS,)):
            if lead is None:
                out_specs.append(pl.BlockSpec((None, ts * C_HEADS, LANES), lambda b, i: (b, i, 0)))
                out_shape.append(jax.ShapeDtypeStruct((bsz, n * C_HEADS, LANES), F32))
            else:
                zeros = (0,) * (len(lead) + 2)
                out_specs.append(pl.BlockSpec((rps,) + lead + (HEAD_DIM, cache_n),
                                              lambda b, i, zeros=zeros: (i,) + zeros))
                out_shape.append(jax.ShapeDtypeStruct((reqs,) + lead + (HEAD_DIM, cache_n), F32))
    return pl.pallas_call(
        functools.partial(_odd_in_kernel, rope=rope, emit_cache=emit_cache, tm=tm),
        grid=(bsz, n // ts),
        in_specs=in_specs,
        out_specs=out_specs,
        out_shape=out_shape,
        compiler_params=_params(2),
        name="odd_in",
    )(*args)


QK_AHEAD = 3
SINK_AHEAD = 2


def _diff_attn_kernel(*refs, lam_init, tq, kc):
    n_src = (len(refs) - 4) // 2
    q_ref, (lamv_ref, g_ref, o_ref) = refs[0], refs[-3:]
    k_refs, vt_refs = refs[1:1 + n_src], refs[1 + n_src:1 + 2 * n_src]
    chunks = [(src, off) for src in range(n_src) for off in range(0, k_refs[src].shape[1], kc)]
    n_chunks = len(chunks)
    reqs = q_ref.shape[0]
    tiles = q_ref.shape[1] // tq
    heads = q_ref.shape[2] // LANES
    lane = lax.broadcasted_iota(jnp.int32, (tq, LANES), 1)
    lv = lamv_ref[...]
    lam = (jnp.exp(jnp.sum(lv[0:1] * lv[1:2], keepdims=True))
           - jnp.exp(jnp.sum(lv[2:3] * lv[3:4], keepdims=True)) + lam_init)

    qqs = {}

    def score(item):
        r, h, t, c = item
        cols = slice(h * LANES, (h + 1) * LANES)
        if (r, h, t) not in qqs:
            q = q_ref[r, t * tq:(t + 1) * tq, cols]
            zero = jnp.zeros_like(q)
            qqs[r, h, t] = jnp.concatenate([jnp.where(lane < HEAD_DIM, q, zero),
                                            jnp.where(lane >= HEAD_DIM, q, zero)], axis=0)
        src, off = chunks[c]
        return _dot_nt(k_refs[src][r, off:off + kc, cols], qqs[r, h, t])

    def finish(r, h, t, acc, l):
        inv = 1.0 / l
        ot = acc[:, :tq] * inv[:, :tq] - acc[:, tq:] * (inv[:, tq:] * lam)
        o = ot.T
        ms = jnp.mean(o * o, axis=-1, keepdims=True)
        y = o * lax.rsqrt(ms + EPS) * g_ref[...] * (1.0 - lam_init)
        o_ref[r, t * tq:(t + 1) * tq, h * LANES:(h + 1) * LANES] = y.astype(BF16)

    items = [(r, h, t, c) for r in range(reqs) for h in range(heads) for t in range(tiles)
             for c in range(n_chunks)]
    ahead = [score(it) for it in items[:QK_AHEAD]]
    m = acc = None
    for idx, (r, h, t, c) in enumerate(items):
        s = ahead.pop(0)
        if idx + QK_AHEAD < len(items):
            ahead.append(score(items[idx + QK_AHEAD]))
        cm = jnp.max(s, axis=0, keepdims=True)
        m_new = cm if c == 0 else jnp.maximum(m, cm)
        src, off = chunks[c]
        e = jnp.exp2(s - m_new)
        lsum = jnp.sum(e, axis=0, keepdims=True)
        lane0 = r * k_refs[src].shape[1] + off
        pv = _dot(vt_refs[src][h, :LANES, lane0:lane0 + kc], e.astype(BF16))
        if c == 0:
            acc, l = pv, lsum
        else:
            alpha = jnp.exp2(m - m_new)
            acc, l = acc * alpha + pv, l * alpha + lsum
        m = m_new
        if c == n_chunks - 1:
            finish(r, h, t, acc, l)


def _diff_attn_call(q, ks, vts, lamv, subln_g, lam_init, tq, tiles, heads, reqs=1, flat_values=False):
    bsz, n, _ = q.shape
    assert reqs == 1 or flat_values
    kc = next(c for c in (512, 256, 128) if all(k.shape[1] % c == 0 for k in ks))
    hw = heads * LANES
    k_specs = [pl.BlockSpec((reqs, k.shape[1], hw), lambda b, h, i: (b, 0, h)) for k in ks]
    vt_index = (lambda b, h, i: (0, h, 0, b)) if flat_values else (lambda b, h, i: (b, h, 0, 0))
    vt_specs = [pl.BlockSpec((None, heads, V_ROWS, reqs * k.shape[1]), vt_index) for k in ks]
    return pl.pallas_call(
        functools.partial(_diff_attn_kernel, lam_init=lam_init, tq=tq, kc=kc),
        grid=(bsz // reqs, C_HEADS // heads, n // (tq * tiles)),
        in_specs=[pl.BlockSpec((reqs, tiles * tq, hw), lambda b, h, i: (b, i, h))]
        + k_specs + vt_specs + [_const_spec((4, HEAD_DIM)), _const_spec((1, LANES))],
        out_specs=pl.BlockSpec((reqs, tiles * tq, hw), lambda b, h, i: (b, i, h)),
        out_shape=jax.ShapeDtypeStruct((bsz, n, C_W), BF16),
        compiler_params=_params(3),
        name="diff_attn",
    )(q, *ks, *vts, lamv, subln_g)


def _sink_attn_kernel(*refs, windowed, tq, nblk):
    if windowed:
        (q_ref, ck_ref, kp_ref, km_ref, kn_ref, cvt_ref, vp_ref, vm_ref, vn_ref,
         bias_ref, sink_ref, o_ref) = refs
        kband = jnp.concatenate([kp_ref[...], km_ref[...], kn_ref[...]], axis=0)
        vtband = jnp.concatenate([vp_ref[...], vm_ref[...], vn_ref[...]], axis=1)
    else:
        q_ref, k_ref, vt_ref, sink_ref, o_ref = refs
    reqs = q_ref.shape[0]
    subs = q_ref.shape[1] // tq
    groups = q_ref.shape[2] // (2 * LANES)
    lane = lax.broadcasted_iota(jnp.int32, (tq, LANES), 1)
    lo = lane < HEAD_DIM

    def scores_of(stream):
        r, g, sub = stream
        qd = q_ref[r, sub * tq:(sub + 1) * tq, g * 2 * LANES:(g + 1) * 2 * LANES]
        parts = []
        for c in range(2):
            ch = qd[:, c * LANES:(c + 1) * LANES]
            zero = jnp.zeros_like(ch)
            parts += [jnp.where(lo, ch, zero), jnp.where(lo, zero, ch)]
        qq = jnp.concatenate(parts, axis=0)
        if not windowed:
            return [_dot_nt(k_ref[r, :, g * LANES:(g + 1) * LANES], qq)]
        blk = pl.program_id(2) * subs + sub
        variant = jnp.where(blk == 0, 0, jnp.where(blk == nblk - 1, 2, 1))
        band = _dot_nt(kband[sub * tq:(sub + 3) * tq, :], qq) + bias_ref[variant]
        return [_dot_nt(ck_ref[...], qq), band]

    def finish(stream, scores):
        r, g, sub = stream
        values = ([cvt_ref[...], vtband[:, sub * tq:(sub + 3) * tq]] if windowed
                  else [vt_ref[g, :, r * tq:(r + 1) * tq]])
        sink = sink_ref[g]
        m = sink
        for s in scores:
            m = jnp.maximum(m, jnp.max(s, axis=0, keepdims=True))
        acc = None
        for s, vt in zip(scores, values):
            pv = _dot(vt, jnp.exp2(s - m).astype(BF16))
            acc = pv if acc is None else acc + pv
        inv = 1.0 / (acc[LANES:LANES + 1, :] + jnp.exp2(sink - m))
        ot = acc[:LANES, :] * inv
        for c in range(2):
            o0 = ot[:, (2 * c) * tq:(2 * c + 1) * tq].T
            o1 = ot[:, (2 * c + 1) * tq:(2 * c + 2) * tq].T
            cols = slice((2 * g + c) * LANES, (2 * g + c + 1) * LANES)
            o_ref[r, sub * tq:(sub + 1) * tq, cols] = jnp.where(lo, o0, o1).astype(BF16)

    streams = [(r, g, sub) for r in range(reqs) for g in range(groups) for sub in range(subs)]
    ahead = [scores_of(st) for st in streams[:SINK_AHEAD]]
    for idx, st in enumerate(streams):
        cur = ahead.pop(0)
        if idx + SINK_AHEAD < len(streams):
            ahead.append(scores_of(streams[idx + SINK_AHEAD]))
        finish(st, cur)


def _window_bias(n, tq):
    nblk = n // tq
    kk = np.arange(3 * tq)[:, None]
    qi = np.arange(tq)[None, :]
    band_ok = np.abs(qi + WINDOW - kk) <= WINDOW
    out = []
    for blk in (0, 1, nblk - 1):
        jpos = blk * tq - WINDOW + kk
        ok = band_ok & (jpos >= 0) & (jpos < n)
        out.append(np.tile(np.where(ok, 0.0, NEG_INF), (1, D_GROUP)))
    return jnp.asarray(np.stack(out), F32)


def _sink_attn_call(q, k, vt, sink_rows, ctx_k=None, ctx_vt=None, reqs=1, flat_values=False):
    bsz, n, _ = q.shape
    windowed = ctx_k is not None
    assert reqs == 1 or (flat_values and not windowed)
    tq = WINDOW if windowed else n
    nblk = n // tq
    subs = next(s for s in (32, 16, 8, 4, 2, 1) if nblk % s == 0) if windowed else 1
    steps = nblk // subs
    groups = 1 if windowed else D_KV_HEADS
    q_spec = pl.BlockSpec((reqs, subs * tq, groups * 2 * LANES), lambda b, j, i: (b, i, j))
    sink_spec = pl.BlockSpec((groups, 1, D_GROUP * tq), lambda b, j, i: (j, 0, 0))
    if windowed:
        assert nblk >= 3, "first / interior / last mask variants need three query blocks"
        n_ctx = ctx_k.shape[1]
        prev = lambda i: jnp.maximum(i * subs - 1, 0)
        nxt = lambda i: jnp.minimum((i + 1) * subs, nblk - 1)
        in_specs = [
            q_spec,
            pl.BlockSpec((None, n_ctx, LANES), lambda b, j, i: (b, 0, j)),
            pl.BlockSpec((None, tq, LANES), lambda b, j, i: (b, prev(i), j)),
            pl.BlockSpec((None, subs * tq, LANES), lambda b, j, i: (b, i, j)),
            pl.BlockSpec((None, tq, LANES), lambda b, j, i: (b, nxt(i), j)),
            pl.BlockSpec((None, None, V_ROWS, n_ctx), lambda b, j, i: (b, j, 0, 0)),
            pl.BlockSpec((None, None, V_ROWS, tq), lambda b, j, i: (b, j, 0, prev(i))),
            pl.BlockSpec((None, None, V_ROWS, subs * tq), lambda b, j, i: (b, j, 0, i)),
            pl.BlockSpec((None, None, V_ROWS, tq), lambda b, j, i: (b, j, 0, nxt(i))),
            _const_spec((3, 3 * tq, D_GROUP * tq)),
            sink_spec,
        ]
        args = (q, ctx_k, k, k, k, ctx_vt, vt, vt, vt, _window_bias(n, tq), sink_rows)
    else:
        in_specs = [
            q_spec,
            pl.BlockSpec((reqs, n, groups * LANES), lambda b, j, i: (b, 0, j)),
            pl.BlockSpec((None, groups, V_ROWS, reqs * n),
                         (lambda b, j, i: (0, j, 0, b)) if flat_values else (lambda b, j, i: (b, j, 0, 0))),
            sink_spec,
        ]
        args = (q, k, vt, sink_rows)
    return pl.pallas_call(
        functools.partial(_sink_attn_kernel, windowed=windowed, tq=tq, nblk=nblk),
        grid=(bsz // reqs, D_KV_HEADS // groups, steps),
        in_specs=in_specs,
        out_specs=q_spec,
        out_shape=jax.ShapeDtypeStruct((bsz, n, DQ_W), BF16),
        compiler_params=_params(3),
        name="sink_attn",
    )(*args)


def _dup_heads(a):
    lead = a.shape[:-1]
    a = a.reshape(lead + (D_KV_HEADS, 1, HEAD_DIM))
    return jnp.broadcast_to(a, lead + (D_KV_HEADS, 2, HEAD_DIM)).reshape(lead + (DKV_DUP_W,))


def _odd_gains(c_qn_g, c_kn_g, d_qn_g, d_kn_g):
    return jnp.concatenate([
        jnp.tile(c_qn_g * (ATTN_SCALE * LOG2E), C_W // HEAD_DIM),
        jnp.tile(c_kn_g, C_W // HEAD_DIM),
        jnp.tile(d_qn_g * (ATTN_SCALE * LOG2E), DQ_W // HEAD_DIM),
        jnp.tile(d_kn_g, DKV_W // HEAD_DIM),
    ])[None, :]


def _value_rows(v):
    bsz, nk, w = v.shape
    vt = v.reshape(bsz, nk, w // LANES, LANES).transpose(0, 2, 3, 1)
    pad = jnp.zeros((bsz, w // LANES, V_ROWS - LANES, nk), v.dtype).at[:, :, 0, :].set(1)
    return jnp.concatenate([vt, pad], axis=2)


def _rope_tables(n):
    rows = n // GRID_W
    row = np.repeat(np.arange(rows), GRID_W).astype(np.float64)
    col = np.tile(np.arange(GRID_W), rows).astype(np.float64)
    freqs = ROPE_BASE ** (-np.arange(ROPE_FREQS, dtype=np.float64) / ROPE_FREQS)
    ang = np.stack([row[:, None] * freqs, col[:, None] * freqs], axis=1)
    ang = np.concatenate([ang, ang], axis=-1).reshape(n, HEAD_DIM)
    ang = np.concatenate([ang, ang], axis=-1)
    return jnp.asarray(np.cos(ang), F32), jnp.asarray(np.sin(ang), F32)


MLP_TILE = 512


def _token_tile(n):
    return min(n, 512)


def _mixer(x, l, mod, row_of_batch, wts, caches, new_cache, side_cast=None):
    bsz, n, _ = x.shape
    decode = caches is not None
    g1n = wts["norm1_g"][l][None, :]
    if l % 2 == 0:
        e = l // 2
        a, z, *copies = _even_in_call(x, mod, l, row_of_batch, g1n, e, wts["w_in_even"], wts["w_pool"],
                                      wts["pool_scale"][e][None, :], wts["dftc"], wts["w_fft"], min(n, 1024),
                                      1 if decode or n > 1024 else next(q for q in (4, 2, 1) if bsz % q == 0),
                                      side_cast)
        y = _dft_two_stage_call(z) if z.shape[-1] == 2 * B_WIDTH else z
        return (a, y) + tuple(copies)
    o = l // 2
    lam_init = _lam_init(l)
    rope_tabs = _rope_tables(n) if decode else None
    xo = x if decode else x.reshape(1, bsz * n, D_MODEL)
    outs = _odd_in_call(xo, mod, l, row_of_batch, g1n, wts["w_in_odd"], o, wts["odd_gains"][o],
                        wts["head_mean"], rope_tabs, _token_tile(xo.shape[1]),
                        None if decode else n)
    if not decode:
        outs = [a if a.ndim == 4 else a.reshape(bsz, n, a.shape[-1]) for a in outs[:6]] + list(outs[6:])
    cq, ck, cvt, dq, dkk, dvt = outs[:6]
    if decode:
        c_k, c_v, d_k, d_v = caches
        lc = c_k.shape[2]
        ks = [c_k[:, o].reshape(bsz, lc, C_W).astype(BF16), ck]
        vts = [_value_rows(c_v[:, o].reshape(bsz, lc, C_W).astype(BF16)), cvt]
        ctx_k = _dup_heads(d_k[:, o].reshape(bsz, lc, DKV_W)).astype(BF16)
        ctx_vt = _value_rows(_dup_heads(d_v[:, o].reshape(bsz, lc, DKV_W)).astype(BF16))
        tq = min(n, 256)
    else:
        new_cache.append(outs[6:])
        ks, vts = [ck], [cvt]
        ctx_k = ctx_vt = None
        tq = n
    tiles = next(t for t in (8, 4, 2, 1) if n % (t * tq) == 0)
    heads = 1 if decode else C_HEADS
    c_out = _diff_attn_call(cq, ks, vts, wts["lamv"][o],
                            wts["c_subln_g"][o][None, :], lam_init, tq, tiles, heads,
                            1 if decode else next(r for r in (4, 2, 1) if bsz % r == 0), not decode)
    sink_tq = WINDOW if decode else n
    sink_rows = jnp.repeat(wts["d_sink"][o].reshape(D_KV_HEADS, D_GROUP) * LOG2E, sink_tq,
                           axis=-1)[:, None, :]
    d_out = _sink_attn_call(dq, dkk, dvt, sink_rows, ctx_k, ctx_vt,
                            1 if decode else next(r for r in (4, 2, 1) if bsz % r == 0), not decode)
    return c_out, d_out


def kernel(x_prompt, x_sample, c, cache_c_k, cache_c_v, cache_d_k, cache_d_v, c_ctx,
           norm1_g, norm2_g, w_ada, b_ada, w_in_even, w_pool, pool_scale, w_fft, w_out_even,
           w_in_odd, c_qn_g, c_kn_g, lam_q1, lam_k1, lam_q2, lam_k2, c_subln_g,
           d_qn_g, d_kn_g, d_sink, w_out_odd, w_mlp1, w_mlp2):
    depth = norm1_g.shape[0]
    n_odd = w_in_odd.shape[0]
    dec_b = c.shape[0]

    rows = -(-(1 + dec_b) // SUBLANES) * SUBLANES
    cond = jnp.concatenate([c_ctx[None, :], c, jnp.zeros((rows - 1 - dec_b, D_MODEL), F32)], axis=0)
    mod = _adaln_call(cond, w_ada, b_ada).reshape(depth, rows, 6, 1, D_MODEL)

    ic = np.arange(GROUP_W)
    ang = 2.0 * np.pi * np.outer(ic, ic) / GROUP_W
    dftc = _bf16_const(np.concatenate([np.cos(ang), -np.sin(ang)], axis=1))
    head_mean = _bf16_const(np.kron(np.eye(2 * LANES // HEAD_DIM), np.ones((HEAD_DIM, HEAD_DIM))) / HEAD_DIM)

    odd_gains = [_odd_gains(c_qn_g[o], c_kn_g[o], d_qn_g[o], d_kn_g[o]) for o in range(n_odd)]
    wts = {
        "norm1_g": norm1_g, "norm2_g": norm2_g,
        "w_in_even": w_in_even, "w_pool": w_pool, "pool_scale": pool_scale,
        "w_fft": w_fft, "w_out_even": w_out_even, "dftc": dftc,
        "w_in_odd": w_in_odd, "odd_gains": odd_gains, "head_mean": head_mean,
        "lamv": jnp.stack([lam_q1, lam_k1, lam_q2, lam_k2], axis=1),
        "c_subln_g": c_subln_g, "d_sink": d_sink, "w_out_odd": w_out_odd,
    }

    bsz, n, _ = x_prompt.shape
    caches = (cache_c_k, cache_c_v, cache_d_k, cache_d_v)
    new_cache = []
    xp, xs = x_prompt, x_sample
    mlp_stacks = (w_mlp1, w_mlp2)
    mlp_w = None
    for l in range(depth):
        pp = _mixer(xp, l, mod, lambda b: 0, wts, None, new_cache)
        ps = _mixer(xs, l, mod, lambda b: 1 + b, wts, caches, new_cache,
                    (mlp_stacks, 0) if l == 0 else None)
        if l == 0:
            mlp_w = ps[2:]
        w_out, out_idx = (wts["w_out_even"], l // 2) if l % 2 == 0 else (wts["w_out_odd"], l // 2)
        flat = lambda a: a.reshape(1, bsz * n, a.shape[-1])
        xp, xs, *mlp_next = _out_mlp_call((flat(xp), flat(pp[0]), flat(pp[1]), 0), (xs, ps[0], ps[1], 1),
                                          mod, l, wts["norm2_g"][l][None, :], w_out, out_idx, *mlp_w,
                                          (mlp_stacks, l + 1) if l + 1 < depth else None)
        mlp_w = mlp_next
        xp = xp.reshape(bsz, n, D_MODEL)
    y_prompt, y_sample = xp, xs
    new_c_k = jnp.stack([nc[0].transpose(0, 4, 1, 2, 3) for nc in new_cache], axis=1)
    new_c_v = jnp.stack([nc[1].reshape(bsz, n, C_HEADS, 2 * HEAD_DIM) for nc in new_cache], axis=1)
    new_d_k = jnp.stack([nc[2].transpose(0, 3, 1, 2) for nc in new_cache], axis=1)
    new_d_v = jnp.stack([nc[3].transpose(0, 3, 1, 2) for nc in new_cache], axis=1)
    return (y_prompt, y_sample, new_c_k, new_c_v, new_d_k, new_d_v)
```

```python
import functools
import math

import numpy as np
import jax
import jax.numpy as jnp
from jax import lax
from jax.experimental import pallas as pl
from jax.experimental.pallas import tpu as pltpu

F32 = jnp.float32
BF16 = jnp.bfloat16

D_MODEL = 1024
HEAD_DIM = 64
LANES = 128
SUBLANES = 8
GRID_W = 64
ROPE_FREQS = HEAD_DIM // 4
ROPE_BASE = 10000.0
EPS = 1e-6
NEG_INF = -1e30
WINDOW = 128
A_WIDTH = D_MODEL // 2
B_WIDTH = D_MODEL // 2
POOL_WINDOWS = (2, 4, 8, 16)
POOL_HALO = 8
N_GROUPS = 4
GROUP_W = A_WIDTH // N_GROUPS
C_HEADS = 4
D_HEADS = 8
D_KV_HEADS = 2
D_GROUP = D_HEADS // D_KV_HEADS
C_W = C_HEADS * 2 * HEAD_DIM
DQ_W = D_HEADS * HEAD_DIM
DKV_W = D_KV_HEADS * HEAD_DIM
DKV_DUP_W = 2 * DKV_W
D_FF = 4 * D_MODEL
ATTN_SCALE = HEAD_DIM ** -0.5
LOG2E = math.log2(math.e)
V_ROWS = LANES + 16
ODD_CV0 = 2 * C_W
ODD_DQ0 = 3 * C_W
ODD_DK0 = ODD_DQ0 + DQ_W
ODD_DV0 = ODD_DK0 + DKV_W
ODD_W = ODD_DV0 + DKV_W
ODD_NORMED_W = 2 * C_W + DQ_W + DKV_W
VMEM_LIMIT = 56 * 1024 * 1024


def _lam_init(layer):
    return 0.8 - 0.6 * math.exp(-0.3 * layer)


def _params(n_axes):
    return pltpu.CompilerParams(dimension_semantics=("arbitrary",) * n_axes,
                                vmem_limit_bytes=VMEM_LIMIT)


def _const_spec(shape):
    nd = len(shape)
    return pl.BlockSpec(shape, lambda *_: (0,) * nd, pipeline_mode=pl.Buffered(1))


def _stacked_spec(shape, idx):
    nd = len(shape)
    return pl.BlockSpec((None,) + tuple(shape), lambda *_: (idx,) + (0,) * nd, pipeline_mode=pl.Buffered(1))


def _side_cast_specs(stacks, layer, n_steps, step_of):
    specs, shapes = [], []
    for w in stacks:
        axis = 2 if w.shape[2] >= w.shape[1] else 1
        nb = 1
        while nb * 2 <= n_steps and w.shape[axis] % (nb * 2 * 2 * LANES) == 0:
            nb *= 2
        block = [None, w.shape[1], w.shape[2]]
        block[axis] //= nb

        def index(*grid, axis=axis, nb=nb):
            blk = jnp.minimum(step_of(*grid), nb - 1)
            return (layer, 0, blk) if axis == 2 else (layer, blk, 0)

        specs.append((pl.BlockSpec(tuple(block), index),
                      pl.BlockSpec(tuple(block[1:]), lambda *grid, index=index: index(*grid)[1:])))
        shapes.append(jax.ShapeDtypeStruct(w.shape[1:], BF16))
    return specs, shapes


def _mod_spec(layer, which, row_of_batch):
    return pl.BlockSpec((None, None, None, 1, D_MODEL),
                        lambda b, i: (layer, row_of_batch(b), which, 0, 0))


def _rms_mod(x, g, sc, sh):
    ms = jnp.mean(x * x, axis=-1, keepdims=True)
    return (x * lax.rsqrt(ms + EPS) * g) * (1.0 + sc) + sh


def _bf16_const(a):
    return jnp.asarray(a, F32).astype(BF16)


def _dot_nt(a, b):
    return lax.dot_general(a, b, (((1,), (1,)), ((), ())), preferred_element_type=F32)


def _dot(a, b):
    return jnp.dot(a, b, preferred_element_type=F32)


def _adaln_kernel(cond_ref, w_ref, b_ref, o_ref):
    cnd = cond_ref[...]
    s = (cnd * jax.nn.sigmoid(cnd)).astype(BF16)
    o_ref[...] = _dot(s, w_ref[...].astype(BF16)) + b_ref[...]


def _adaln_call(cond, w_ada, b_ada):
    depth = w_ada.shape[0]
    rows = cond.shape[0]
    tn = 3 * D_MODEL
    return pl.pallas_call(
        _adaln_kernel,
        grid=(depth, 6 * D_MODEL // tn),
        in_specs=[
            pl.BlockSpec((rows, D_MODEL), lambda l, j: (0, 0)),
            pl.BlockSpec((None, D_MODEL, tn), lambda l, j: (l, 0, j)),
            pl.BlockSpec((None, 1, tn), lambda l, j: (l, 0, j)),
        ],
        out_specs=pl.BlockSpec((None, rows, tn), lambda l, j: (l, 0, j)),
        out_shape=jax.ShapeDtypeStruct((depth, rows, 6 * D_MODEL), F32),
        compiler_params=_params(2),
        name="adaln",
    )(cond, w_ada, b_ada.reshape(depth, 1, 6 * D_MODEL))


def _even_in_kernel(*refs, tm, n, n_cast, dft_scale):
    n_in = 11 if dft_scale is None else 12
    (x_ref, xp_ref, xn_ref, sh_ref, sc_ref, g_ref, win_ref, wpool_ref,
     pscale_ref, dftc_ref, wfft_ref) = refs[:11]
    a_ref, z_ref = refs[n_in + n_cast:n_in + 2 + n_cast]
    for w_ref, o_ref in zip(refs[n_in:n_in + n_cast], refs[n_in + 2 + n_cast:]):
        o_ref[...] = w_ref[...].astype(BF16)
    i = pl.program_id(1)
    n_tiles = n // tm
    g, sc, sh = g_ref[...], sc_ref[...], sh_ref[...]
    seqs = x_ref.shape[0]
    h = _rms_mod(x_ref[...].reshape(seqs * tm, D_MODEL), g, sc, sh).astype(BF16)
    w_in = win_ref[...].astype(BF16)
    u = _dot(h, w_in)
    xh = jnp.concatenate([xp_ref[...], xn_ref[...]], axis=0)
    hh = _rms_mod(xh, g, sc, sh).astype(BF16)
    uh = _dot(hh, w_in[:, :A_WIDTH])
    has_prev = (i > 0).astype(F32)
    has_next = (i < n_tiles - 1).astype(F32)
    rows = tm + 2 * POOL_HALO

    planes = []
    for gi in range(N_GROUPS):
        ub = u[:, A_WIDTH + gi * GROUP_W:A_WIDTH + (gi + 1) * GROUP_W].astype(BF16)
        cs = _dot(ub, dftc_ref[...])
        w_fft = wfft_ref[gi].astype(BF16)
        zr = _dot(cs[:, :GROUP_W].astype(BF16), w_fft)
        zi = _dot(cs[:, GROUP_W:].astype(BF16), w_fft)
        planes.append((zr.astype(BF16), zi.astype(BF16)))
    for q in range(seqs):
        sl = slice(q * tm, (q + 1) * tm)
        zr = jnp.concatenate([p[0][sl] for p in planes], axis=1)
        zi = jnp.concatenate([p[1][sl] for p in planes], axis=1)
        if dft_scale is None:
            z_ref[q, :, :B_WIDTH] = zr
            z_ref[q, :, B_WIDTH:] = zi
        else:
            y = _dot(refs[11][...], jnp.concatenate([zr, zi], axis=0)) * dft_scale
            z_ref[q] = y.astype(BF16)

    t = i * tm + lax.broadcasted_iota(jnp.int32, (tm, 1), 0)
    for q, (gi, w) in ((q, gw) for q in range(seqs) for gw in enumerate(POOL_WINDOWS)):
        cols = slice(gi * GROUP_W, (gi + 1) * GROUP_W)
        uq = u[q * tm:(q + 1) * tm, cols]
        ext = jnp.concatenate([uh[0:POOL_HALO, cols] * has_prev, uq,
                               uh[POOL_HALO:, cols] * has_next], axis=0)
        s = ext
        span = 1
        while span < w:
            s = s + pltpu.roll(s, span, 0)
            span *= 2
        shift = w // 2 - 1
        s = pltpu.roll(s, rows - shift, 0) if shift else s
        acc = s[POOL_HALO:POOL_HALO + tm]
        lo = jnp.maximum(t - w // 2, 0)
        hi = jnp.minimum(t + w // 2, n)
        cnt = (hi - lo).astype(F32)
        pooled = acc / cnt - uq
        y = _dot(pooled.astype(BF16), wpool_ref[gi].astype(BF16)) * pscale_ref[:, cols]
        a_ref[q, :, cols] = y.astype(BF16)


def _even_in_call(x, mod, layer, row_of_batch, g, e, w_in, w_pool, pool_scale, dftc, w_fft, tm, seqs,
                  side_cast=None):
    bsz, n, _ = x.shape
    assert seqs == 1 or tm == n
    hb = tm // POOL_HALO
    tiles = n // tm
    stacks, cast_layer = side_cast if side_cast else ((), 0)
    cast_specs, cast_shapes = _side_cast_specs(stacks, cast_layer, (bsz // seqs) * tiles,
                                               lambda b, i: b * tiles + i)
    fused = tm == n
    if fused:
        k = np.arange(n)
        ang = 2.0 * np.pi * np.outer(k, k) / n
        dft_args = [_bf16_const(np.concatenate([np.cos(ang), np.sin(ang)], axis=1))]
        dft_specs = [_const_spec((n, 2 * n))]
    else:
        dft_args, dft_specs = [], []
    kern = functools.partial(_even_in_kernel, tm=tm, n=n, n_cast=len(stacks),
                             dft_scale=float((n * GROUP_W) ** -0.5) if fused else None)
    second_w = B_WIDTH if fused else 2 * B_WIDTH
    return pl.pallas_call(
        kern,
        grid=(bsz // seqs, n // tm),
        in_specs=[
            pl.BlockSpec((seqs, tm, D_MODEL), lambda b, i: (b, i, 0)),
            pl.BlockSpec((None, POOL_HALO, D_MODEL),
                         lambda b, i: (b * seqs, jnp.maximum(i * hb - 1, 0), 0)),
            pl.BlockSpec((None, POOL_HALO, D_MODEL),
                         lambda b, i: (b * seqs, jnp.minimum((i + 1) * hb, n // POOL_HALO - 1), 0)),
            _mod_spec(layer, 0, row_of_batch),
            _mod_spec(layer, 1, row_of_batch),
            _const_spec((1, D_MODEL)),
            _stacked_spec((D_MODEL, D_MODEL), e),
            _stacked_spec((N_GROUPS, GROUP_W, GROUP_W), e),
            _const_spec((1, A_WIDTH)),
            _const_spec((GROUP_W, 2 * GROUP_W)),
            _stacked_spec((N_GROUPS, GROUP_W, GROUP_W), e),
        ] + dft_specs + [i_spec for i_spec, _ in cast_specs],
        out_specs=[
            pl.BlockSpec((seqs, tm, A_WIDTH), lambda b, i: (b, i, 0)),
            pl.BlockSpec((seqs, tm, second_w), lambda b, i: (b, i, 0)),
        ] + [o_spec for _, o_spec in cast_specs],
        out_shape=[
            jax.ShapeDtypeStruct((bsz, n, A_WIDTH), BF16),
            jax.ShapeDtypeStruct((bsz, n, second_w), BF16),
        ] + cast_shapes,
        compiler_params=_params(2),
        name="even_in",
    )(x, x, x, mod, mod, g, w_in, w_pool, pool_scale, dftc, w_fft, *dft_args, *stacks)


DFT_ROWS = 16
DFT_PITCH = 24


def _pitch_rows(x):
    g, _, w = x.shape
    pad = jnp.zeros((g, DFT_PITCH - DFT_ROWS, w), x.dtype)
    return jnp.concatenate([x, pad], axis=1).reshape(g * DFT_PITCH, w)


def _dft_kernel(z_ref, m1_ref, twr_ref, twi_ref, m2_ref, y_ref, zs_ref, os_ref, ys_ref, bs_ref, *, na, nb, scale):
    j = pl.program_id(1)
    n_s1 = nb // DFT_ROWS
    nt = 2 * B_WIDTH // LANES

    @pl.when((pl.program_id(0) == 0) & (j == 0))
    def _():
        os_ref[...] = jnp.zeros_like(os_ref)
        ys_ref[...] = jnp.zeros_like(ys_ref)

    @pl.when(j < n_s1)
    def _():
        zf = _pitch_rows(z_ref[...].astype(F32))
        for k in range(nt):
            zs_ref[k] = zf[:, k * LANES:(k + 1) * LANES]
        for bi in range(DFT_ROWS):
            rows = pl.ds(bi, na, stride=DFT_PITCH)
            zb = [zs_ref[k, rows, :].astype(BF16) for k in range(nt)]
            rhs = jnp.concatenate([jnp.concatenate(zb[:nt // 2], axis=1),
                                   jnp.concatenate(zb[nt // 2:], axis=1)], axis=0)
            r = _dot(m1_ref[...], rhs)
            br, bim = r[:na], r[na:]
            tr, ti = twr_ref[bi], twi_ref[bi]
            out = (br * tr - bim * ti, br * ti + bim * tr)
            for k in range(nt):
                half, kk = divmod(k, nt // 2)
                os_ref[k, rows, :] = out[half][:, kk * LANES:(kk + 1) * LANES]
        of = jnp.concatenate([os_ref[k] for k in range(nt)], axis=1)
        bs_ref[j] = of.reshape(na, DFT_PITCH, 2 * B_WIDTH)[:, :DFT_ROWS].astype(BF16)

    @pl.when(j >= n_s1)
    def _():
        for di in range(DFT_ROWS):
            d = (j - n_s1) * DFT_ROWS + di
            bb = jnp.concatenate([bs_ref[jj, d] for jj in range(n_s1)], axis=0)
            rhs = jnp.concatenate([bb[:, :B_WIDTH], bb[:, B_WIDTH:]], axis=0)
            y = _dot(m2_ref[...], rhs) * scale
            for k in range(B_WIDTH // LANES):
                ys_ref[k, pl.ds(di, nb, stride=DFT_PITCH), :] = y[:, k * LANES:(k + 1) * LANES]
        yf = jnp.concatenate([ys_ref[k] for k in range(B_WIDTH // LANES)], axis=1)
        y_ref[...] = yf.reshape(nb, DFT_PITCH, B_WIDTH)[:, :DFT_ROWS].astype(BF16)


def _dft_two_stage_call(z):
    bsz, n, _ = z.shape
    na = 1 << (int(math.log2(n)) // 2)
    nb = n // na
    ia = np.arange(na)
    ib = np.arange(nb)
    ang_a = 2.0 * np.pi * np.outer(ia, ia) / na
    fr, fi = np.cos(ang_a), -np.sin(ang_a)
    m1 = _bf16_const(np.block([[fr, -fi], [fi, fr]]))
    ang_t = 2.0 * np.pi * np.outer(ib, ia) / n
    twr = jnp.asarray(np.cos(ang_t)[:, :, None], F32)
    twi = jnp.asarray(-np.sin(ang_t)[:, :, None], F32)
    ang_b = 2.0 * np.pi * np.outer(ib, ib) / nb
    m2 = _bf16_const(np.concatenate([np.cos(ang_b), np.sin(ang_b)], axis=1))

    n_s1, n_s2 = nb // DFT_ROWS, na // DFT_ROWS
    s1_blk = lambda j: jnp.minimum(j, n_s1 - 1)
    s2_blk = lambda j: jnp.maximum(j - n_s1, 0)
    y = pl.pallas_call(
        functools.partial(_dft_kernel, na=na, nb=nb, scale=float((n * GROUP_W) ** -0.5)),
        grid=(bsz, n_s1 + n_s2),
        in_specs=[
            pl.BlockSpec((None, na, DFT_ROWS, 2 * B_WIDTH), lambda b, j: (b, 0, s1_blk(j), 0)),
            _const_spec((2 * na, 2 * na)),
            pl.BlockSpec((DFT_ROWS, na, 1), lambda b, j: (s1_blk(j), 0, 0)),
            pl.BlockSpec((DFT_ROWS, na, 1), lambda b, j: (s1_blk(j), 0, 0)),
            _const_spec((nb, 2 * nb)),
        ],
        out_specs=pl.BlockSpec((None, nb, DFT_ROWS, B_WIDTH), lambda b, j: (b, 0, s2_blk(j), 0)),
        out_shape=jax.ShapeDtypeStruct((bsz, nb, na, B_WIDTH), BF16),
        scratch_shapes=[
            pltpu.VMEM((2 * B_WIDTH // LANES, na * DFT_PITCH, LANES), F32),
            pltpu.VMEM((2 * B_WIDTH // LANES, na * DFT_PITCH, LANES), F32),
            pltpu.VMEM((B_WIDTH // LANES, nb * DFT_PITCH, LANES), F32),
            pltpu.VMEM((n_s1, na, DFT_ROWS, 2 * B_WIDTH), BF16),
        ],
        compiler_params=_params(2),
        name="dft_two_stage",
    )(z.reshape(bsz, na, nb, 2 * B_WIDTH), m1, twr, twi, m2)
    return y.reshape(bsz, n, B_WIDTH)


FF_CHUNK = 1024


def _out_mlp_kernel(*refs, steps_a, n_cast):
    (xa_ref, p1a_ref, p2a_ref, xb_ref, p1b_ref, p2b_ref, g1_ref, sh_ref, sc_ref, g2_ref,
     ng_ref, wout_ref, w1_ref, w2_ref) = refs[:14]
    oa_ref, ob_ref = refs[14 + n_cast:16 + n_cast]
    for w_ref, o_ref in zip(refs[14:14 + n_cast], refs[16 + n_cast:]):
        o_ref[...] = w_ref[...].astype(BF16)

    def body(x_ref, p1_ref, p2_ref, o_ref):
        half = p1_ref.shape[-1]
        mix = (_dot(p1_ref[...], wout_ref[:half, :].astype(BF16))
               + _dot(p2_ref[...], wout_ref[half:, :].astype(BF16)))
        x1 = x_ref[...] + g1_ref[...] * mix
        h = _rms_mod(x1, ng_ref[...], sc_ref[...], sh_ref[...]).astype(BF16)
        acc = None
        for c in range(D_FF // FF_CHUNK):
            a = _dot(h, w1_ref[:, c * FF_CHUNK:(c + 1) * FF_CHUNK])
            a = jnp.square(jnp.maximum(a, 0.0)).astype(BF16)
            part = _dot(a, w2_ref[c * FF_CHUNK:(c + 1) * FF_CHUNK, :])
            acc = part if acc is None else acc + part
        o_ref[...] = x1 + g2_ref[...] * acc

    @pl.when(pl.program_id(0) < steps_a)
    def _():
        body(xa_ref, p1a_ref, p2a_ref, oa_ref)

    @pl.when(pl.program_id(0) >= steps_a)
    def _():
        body(xb_ref, p1b_ref, p2b_ref, ob_ref)


def _out_mlp_call(set_a, set_b, mod, layer, ng, w_out, out_idx, w1, w2, side_cast=None):
    (xa, p1a, p2a, row_a), (xb, p1b, p2b, row_b) = set_a, set_b
    half = p1a.shape[-1]
    tm = min(xa.shape[1], xb.shape[1], MLP_TILE)
    tiles_a, tiles_b = xa.shape[1] // tm, xb.shape[1] // tm
    steps_a, steps_b = xa.shape[0] * tiles_a, xb.shape[0] * tiles_b

    def pos_a(s):
        sa = jnp.minimum(s, steps_a - 1)
        return sa // tiles_a, sa % tiles_a

    def pos_b(s):
        sb = jnp.maximum(s - steps_a, 0)
        return sb // tiles_b, sb % tiles_b

    tok_a = lambda w: pl.BlockSpec((None, tm, w), lambda s: pos_a(s) + (0,))
    tok_b = lambda w: pl.BlockSpec((None, tm, w), lambda s: pos_b(s) + (0,))
    row = lambda s: jnp.where(s < steps_a, row_a + pos_a(s)[0], row_b + pos_b(s)[0])
    mod_spec = lambda which: pl.BlockSpec((None, None, None, 1, D_MODEL),
                                          lambda s: (layer, row(s), which, 0, 0))
    stacks, cast_layer = side_cast if side_cast else ((), 0)
    cast_specs, cast_shapes = _side_cast_specs(stacks, cast_layer, steps_a + steps_b, lambda s: s)
    return pl.pallas_call(
        functools.partial(_out_mlp_kernel, steps_a=steps_a, n_cast=len(stacks)),
        grid=(steps_a + steps_b,),
        in_specs=[
            tok_a(D_MODEL), tok_a(half), tok_a(half), tok_b(D_MODEL), tok_b(half), tok_b(half),
            mod_spec(2), mod_spec(3), mod_spec(4), mod_spec(5),
            _const_spec((1, D_MODEL)),
            _stacked_spec((2 * half, D_MODEL), out_idx),
            _const_spec((D_MODEL, D_FF)),
            _const_spec((D_FF, D_MODEL)),
        ] + [i_spec for i_spec, _ in cast_specs],
        out_specs=[tok_a(D_MODEL), tok_b(D_MODEL)] + [o_spec for _, o_spec in cast_specs],
        out_shape=[jax.ShapeDtypeStruct(xa.shape, F32), jax.ShapeDtypeStruct(xb.shape, F32)] + cast_shapes,
        compiler_params=_params(1),
        name="out_mlp",
    )(xa, p1a, p2a, xb, p1b, p2b, mod, mod, mod, mod, ng, w_out, w1, w2, *stacks)


def _odd_in_kernel(*refs, rope, emit_cache, tm):
    x_ref, sh_ref, sc_ref, g_ref, win_ref, gain_ref, hm_ref = refs[:7]
    k = 7
    if rope:
        cos_ref, sin_ref = refs[k:k + 2]
        k += 2
    cq_ref, ck_ref, cvt_ref, dq_ref, dk_ref, dvt_ref = refs[k:k + 6]
    k += 6
    if emit_cache:
        nck_ref, ncv_ref, ndk_ref, ndv_ref = refs[k:k + 4]
    subs = x_ref.shape[0] // tm
    lane = lax.broadcasted_iota(jnp.int32, (tm, LANES), 1)
    first_half = (lane % (2 * ROPE_FREQS)) < ROPE_FREQS
    lo = lane < HEAD_DIM
    pad_row = lax.broadcasted_iota(jnp.int32, (V_ROWS - LANES, tm), 0)
    pad = jnp.where(pad_row == 0, 1.0, 0.0).astype(BF16)

    def cache_rows(ref, sub, r):
        rpt = ref.shape[0] // x_ref.shape[0]
        return ref.at[pl.ds(sub * tm * rpt + r, tm, stride=rpt), :]

    def cache_cols(ref, sub, lead, mat):
        n_req = ref.shape[-1]
        for r in range(tm // n_req):
            ref[(sub * (tm // n_req) + r,) + lead] = mat[:, r * n_req:(r + 1) * n_req]

    groups = ((0, ODD_CV0), (ODD_DQ0, ODD_W), (ODD_CV0, ODD_DQ0))
    gain_lo = (0, ODD_CV0)
    normed_w = (ODD_CV0, DQ_W + DKV_W)
    hidden = {}

    def project(item):
        sub, grp = item
        if sub not in hidden:
            x = x_ref[sub * tm:(sub + 1) * tm, :]
            hidden[sub] = _rms_mod(x, g_ref[...], sc_ref[...], sh_ref[...]).astype(BF16)
        lo, hi = groups[grp]
        return _dot(hidden[sub], win_ref[:, lo:hi].astype(BF16))

    def head_norms(p, lo):
        normed = []
        for c0 in range(0, p.shape[1], 2 * LANES):
            wd = min(2 * LANES, p.shape[1] - c0)
            v = p[:, c0:c0 + wd]
            ms = _dot((v * v).astype(BF16), hm_ref[:wd, :wd])
            y = v * lax.rsqrt(ms + EPS) * gain_ref[:, lo + c0:lo + c0 + wd]
            normed += [y[:, k:k + LANES] for k in range(0, wd, LANES)]
        return normed

    def finish(item, p, normed):
        sub, grp = item
        rows = slice(sub * tm, (sub + 1) * tm)

        def roped(y):
            if not rope:
                return y
            rot = jnp.where(first_half, -pltpu.roll(y, LANES - ROPE_FREQS, 1),
                            pltpu.roll(y, ROPE_FREQS, 1))
            return y * cos_ref[rows, :] + rot * sin_ref[rows, :]

        def put(ref, chunks):
            for c, y in enumerate(chunks):
                ref[rows, c * LANES:(c + 1) * LANES] = roped(y).astype(BF16)

        nq = C_W // LANES
        if grp == 0:
            put(cq_ref, normed[:nq])
            put(ck_ref, normed[nq:])
            if emit_cache:
                for c, y in enumerate(normed[nq:]):
                    yt = y.T
                    for half in range(2):
                        cache_cols(nck_ref, sub, (c, half), yt[half * HEAD_DIM:(half + 1) * HEAD_DIM])
        elif grp == 1:
            put(dq_ref, normed[:nq])
            dk = normed[nq]
            y = roped(dk)
            swapped = pltpu.roll(y, HEAD_DIM, 1)
            dk_ref[rows, :LANES] = jnp.where(lo, y, swapped).astype(BF16)
            dk_ref[rows, LANES:] = jnp.where(lo, swapped, y).astype(BF16)
            if emit_cache:
                dkt = dk.T
                for j in range(D_KV_HEADS):
                    cache_cols(ndk_ref, sub, (j,), dkt[j * HEAD_DIM:(j + 1) * HEAD_DIM])
            dvt = p[:, normed_w[1]:].T
            for j in range(D_KV_HEADS):
                vj = dvt[j * HEAD_DIM:(j + 1) * HEAD_DIM]
                dvt_ref[j, :LANES, rows] = jnp.concatenate([vj, vj], axis=0).astype(BF16)
                dvt_ref[j, LANES:, rows] = pad
                if emit_cache:
                    cache_cols(ndv_ref, sub, (j,), vj)
        else:
            cv = p
            for hh in range(C_HEADS):
                cvt_ref[hh, :LANES, rows] = cv[:, hh * LANES:(hh + 1) * LANES].T.astype(BF16)
                cvt_ref[hh, LANES:, rows] = pad
            if emit_cache:
                for hh in range(C_HEADS):
                    cache_rows(ncv_ref, sub, hh)[...] = cv[:, hh * LANES:(hh + 1) * LANES]

    items = [(sub, grp) for sub in range(subs) for grp in range(len(groups))]
    p = project(items[0])
    for idx, item in enumerate(items):
        normed = head_norms(p[:, :normed_w[item[1]]], gain_lo[item[1]]) if item[1] < 2 else None
        nxt = project(items[idx + 1]) if idx + 1 < len(items) else None
        finish(item, p, normed)
        p = nxt


def _odd_in_call(x, mod, layer, row_of_batch, g, w_in, odd_idx, gains, head_mean, rope_tabs, tm, cache_n=None):
    emit_cache = cache_n is not None
    bsz, n, _ = x.shape
    rope = rope_tabs is not None
    subs = 2 if n % (2 * tm) == 0 else 1
    ts = subs * tm
    tok = lambda w: pl.BlockSpec((None, ts, w), lambda b, i: (b, i, 0))
    in_specs = [
        tok(D_MODEL),
        _mod_spec(layer, 0, row_of_batch),
        _mod_spec(layer, 1, row_of_batch),
        _const_spec((1, D_MODEL)),
        _stacked_spec((D_MODEL, ODD_W), odd_idx),
        _const_spec((1, ODD_NORMED_W)),
        _const_spec((2 * LANES, 2 * LANES)),
    ]
    args = [x, mod, mod, g, w_in, gains, head_mean]
    if rope:
        in_specs += [pl.BlockSpec((ts, LANES), lambda b, i: (i, 0))] * 2
        args += list(rope_tabs)
    vt_spec = lambda heads: pl.BlockSpec((None, heads, V_ROWS, ts), lambda b, i: (b, 0, 0, i))
    vt_shape = lambda heads: jax.ShapeDtypeStruct((bsz, heads, V_ROWS, n), BF16)
    tok_shape = lambda w: jax.ShapeDtypeStruct((bsz, n, w), BF16)
    out_specs = [tok(C_W), tok(C_W), vt_spec(C_HEADS), tok(DQ_W), tok(DKV_DUP_W), vt_spec(D_KV_HEADS)]
    out_shape = [tok_shape(C_W), tok_shape(C_W), vt_shape(C_HEADS), tok_shape(DQ_W),
                 tok_shape(DKV_DUP_W), vt_shape(D_KV_HEADS)]
    if emit_cache:
        assert bsz == 1 and tm % cache_n == 0 and n % cache_n == 0
        reqs, rps = n // cache_n, ts // cache_n
        for lead in ((C_HEADS, 2), None, (D_KV_HEADS,), (D_KV_HEADS,)):
            if lead is None:
                out_specs.append(pl.BlockSpec((None, ts * C_HEADS, LANES), lambda b, i: (b, i, 0)))
                out_shape.append(jax.ShapeDtypeStruct((bsz, n * C_HEADS, LANES), F32))
            else:
                zeros = (0,) * (len(lead) + 2)
                out_specs.append(pl.BlockSpec((rps,) + lead + (HEAD_DIM, cache_n),
                                              lambda b, i, zeros=zeros: (i,) + zeros))
                out_shape.append(jax.ShapeDtypeStruct((reqs,) + lead + (HEAD_DIM, cache_n), F32))
    return pl.pallas_call(
        functools.partial(_odd_in_kernel, rope=rope, emit_cache=emit_cache, tm=tm),
        grid=(bsz, n // ts),
        in_specs=in_specs,
        out_specs=out_specs,
        out_shape=out_shape,
        compiler_params=_params(2),
        name="odd_in",
    )(*args)


QK_AHEAD = 3
SINK_AHEAD = 2


def _diff_attn_kernel(*refs, lam_init, tq, kc):
    n_src = (len(refs) - 4) // 2
    q_ref, (lamv_ref, g_ref, o_ref) = refs[0], refs[-3:]
    k_refs, vt_refs = refs[1:1 + n_src], refs[1 + n_src:1 + 2 * n_src]
    chunks = [(src, off) for src in range(n_src) for off in range(0, k_refs[src].shape[1], kc)]
    n_chunks = len(chunks)
    reqs = q_ref.shape[0]
    tiles = q_ref.shape[1] // tq
    heads = q_ref.shape[2] // LANES
    lane = lax.broadcasted_iota(jnp.int32, (tq, LANES), 1)
    lv = lamv_ref[...]
    lam = (jnp.exp(jnp.sum(lv[0:1] * lv[1:2], keepdims=True))
           - jnp.exp(jnp.sum(lv[2:3] * lv[3:4], keepdims=True)) + lam_init)

    qqs = {}

    def score(item):
        r, h, t, c = item
        cols = slice(h * LANES, (h + 1) * LANES)
        if (r, h, t) not in qqs:
            q = q_ref[r, t * tq:(t + 1) * tq, cols]
            zero = jnp.zeros_like(q)
            qqs[r, h, t] = jnp.concatenate([jnp.where(lane < HEAD_DIM, q, zero),
                                            jnp.where(lane >= HEAD_DIM, q, zero)], axis=0)
        src, off = chunks[c]
        return _dot_nt(k_refs[src][r, off:off + kc, cols], qqs[r, h, t])

    def finish(r, h, t, acc, l):
        inv = 1.0 / l
        ot = acc[:, :tq] * inv[:, :tq] - acc[:, tq:] * (inv[:, tq:] * lam)
        o = ot.T
        ms = jnp.mean(o * o, axis=-1, keepdims=True)
        y = o * lax.rsqrt(ms + EPS) * g_ref[...] * (1.0 - lam_init)
        o_ref[r, t * tq:(t + 1) * tq, h * LANES:(h + 1) * LANES] = y.astype(BF16)

    items = [(r, h, t, c) for r in range(reqs) for h in range(heads) for t in range(tiles)
             for c in range(n_chunks)]
    ahead = [score(it) for it in items[:QK_AHEAD]]
    m = acc = None
    for idx, (r, h, t, c) in enumerate(items):
        s = ahead.pop(0)
        if idx + QK_AHEAD < len(items):
            ahead.append(score(items[idx + QK_AHEAD]))
        cm = jnp.max(s, axis=0, keepdims=True)
        m_new = cm if c == 0 else jnp.maximum(m, cm)
        src, off = chunks[c]
        e = jnp.exp2(s - m_new)
        lsum = jnp.sum(e, axis=0, keepdims=True)
        lane0 = r * k_refs[src].shape[1] + off
        pv = _dot(vt_refs[src][h, :LANES, lane0:lane0 + kc], e.astype(BF16))
        if c == 0:
            acc, l = pv, lsum
        else:
            alpha = jnp.exp2(m - m_new)
            acc, l = acc * alpha + pv, l * alpha + lsum
        m = m_new
        if c == n_chunks - 1:
            finish(r, h, t, acc, l)


def _diff_attn_call(q, ks, vts, lamv, subln_g, lam_init, tq, tiles, heads, reqs=1, flat_values=False):
    bsz, n, _ = q.shape
    assert reqs == 1 or flat_values
    kc = next(c for c in (512, 256, 128) if all(k.shape[1] % c == 0 for k in ks))
    hw = heads * LANES
    k_specs = [pl.BlockSpec((reqs, k.shape[1], hw), lambda b, h, i: (b, 0, h)) for k in ks]
    vt_index = (lambda b, h, i: (0, h, 0, b)) if flat_values else (lambda b, h, i: (b, h, 0, 0))
    vt_specs = [pl.BlockSpec((None, heads, V_ROWS, reqs * k.shape[1]), vt_index) for k in ks]
    return pl.pallas_call(
        functools.partial(_diff_attn_kernel, lam_init=lam_init, tq=tq, kc=kc),
        grid=(bsz // reqs, C_HEADS // heads, n // (tq * tiles)),
        in_specs=[pl.BlockSpec((reqs, tiles * tq, hw), lambda b, h, i: (b, i, h))]
        + k_specs + vt_specs + [_const_spec((4, HEAD_DIM)), _const_spec((1, LANES))],
        out_specs=pl.BlockSpec((reqs, tiles * tq, hw), lambda b, h, i: (b, i, h)),
        out_shape=jax.ShapeDtypeStruct((bsz, n, C_W), BF16),
        compiler_params=_params(3),
        name="diff_attn",
    )(q, *ks, *vts, lamv, subln_g)


def _sink_attn_kernel(*refs, windowed, tq, nblk):
    if windowed:
        (q_ref, ck_ref, kp_ref, km_ref, kn_ref, cvt_ref, vp_ref, vm_ref, vn_ref,
         bias_ref, sink_ref, o_ref) = refs
        kband = jnp.concatenate([kp_ref[...], km_ref[...], kn_ref[...]], axis=0)
        vtband = jnp.concatenate([vp_ref[...], vm_ref[...], vn_ref[...]], axis=1)
    else:
        q_ref, k_ref, vt_ref, sink_ref, o_ref = refs
    reqs = q_ref.shape[0]
    subs = q_ref.shape[1] // tq
    groups = q_ref.shape[2] // (2 * LANES)
    lane = lax.broadcasted_iota(jnp.int32, (tq, LANES), 1)
    lo = lane < HEAD_DIM

    def scores_of(stream):
        r, g, sub = stream
        qd = q_ref[r, sub * tq:(sub + 1) * tq, g * 2 * LANES:(g + 1) * 2 * LANES]
        parts = []
        for c in range(2):
            ch = qd[:, c * LANES:(c + 1) * LANES]
            zero = jnp.zeros_like(ch)
            parts += [jnp.where(lo, ch, zero), jnp.where(lo, zero, ch)]
        qq = jnp.concatenate(parts, axis=0)
        if not windowed:
            return [_dot_nt(k_ref[r, :, g * LANES:(g + 1) * LANES], qq)]
        blk = pl.program_id(2) * subs + sub
        variant = jnp.where(blk == 0, 0, jnp.where(blk == nblk - 1, 2, 1))
        band = _dot_nt(kband[sub * tq:(sub + 3) * tq, :], qq) + bias_ref[variant]
        return [_dot_nt(ck_ref[...], qq), band]

    def finish(stream, scores):
        r, g, sub = stream
        values = ([cvt_ref[...], vtband[:, sub * tq:(sub + 3) * tq]] if windowed
                  else [vt_ref[g, :, r * tq:(r + 1) * tq]])
        sink = sink_ref[g]
        m = sink
        for s in scores:
            m = jnp.maximum(m, jnp.max(s, axis=0, keepdims=True))
        acc = None
        for s, vt in zip(scores, values):
            pv = _dot(vt, jnp.exp2(s - m).astype(BF16))
            acc = pv if acc is None else acc + pv
        inv = 1.0 / (acc[LANES:LANES + 1, :] + jnp.exp2(sink - m))
        ot = acc[:LANES, :] * inv
        for c in range(2):
            o0 = ot[:, (2 * c) * tq:(2 * c + 1) * tq].T
            o1 = ot[:, (2 * c + 1) * tq:(2 * c + 2) * tq].T
            cols = slice((2 * g + c) * LANES, (2 * g + c + 1) * LANES)
            o_ref[r, sub * tq:(sub + 1) * tq, cols] = jnp.where(lo, o0, o1).astype(BF16)

    streams = [(r, g, sub) for r in range(reqs) for g in range(groups) for sub in range(subs)]
    ahead = [scores_of(st) for st in streams[:SINK_AHEAD]]
    for idx, st in enumerate(streams):
        cur = ahead.pop(0)
        if idx + SINK_AHEAD < len(streams):
            ahead.append(scores_of(streams[idx + SINK_AHEAD]))
        finish(st, cur)


def _window_bias(n, tq):
    nblk = n // tq
    kk = np.arange(3 * tq)[:, None]
    qi = np.arange(tq)[None, :]
    band_ok = np.abs(qi + WINDOW - kk) <= WINDOW
    out = []
    for blk in (0, 1, nblk - 1):
        jpos = blk * tq - WINDOW + kk
        ok = band_ok & (jpos >= 0) & (jpos < n)
        out.append(np.tile(np.where(ok, 0.0, NEG_INF), (1, D_GROUP)))
    return jnp.asarray(np.stack(out), F32)


def _sink_attn_call(q, k, vt, sink_rows, ctx_k=None, ctx_vt=None, reqs=1, flat_values=False):
    bsz, n, _ = q.shape
    windowed = ctx_k is not None
    assert reqs == 1 or (flat_values and not windowed)
    tq = WINDOW if windowed else n
    nblk = n // tq
    subs = next(s for s in (16, 8, 4, 2, 1) if nblk % s == 0) if windowed else 1
    steps = nblk // subs
    groups = 1 if windowed else D_KV_HEADS
    q_spec = pl.BlockSpec((reqs, subs * tq, groups * 2 * LANES), lambda b, j, i: (b, i, j))
    sink_spec = pl.BlockSpec((groups, 1, D_GROUP * tq), lambda b, j, i: (j, 0, 0))
    if windowed:
        assert nblk >= 3, "first / interior / last mask variants need three query blocks"
        n_ctx = ctx_k.shape[1]
        prev = lambda i: jnp.maximum(i * subs - 1, 0)
        nxt = lambda i: jnp.minimum((i + 1) * subs, nblk - 1)
        in_specs = [
            q_spec,
            pl.BlockSpec((None, n_ctx, LANES), lambda b, j, i: (b, 0, j)),
            pl.BlockSpec((None, tq, LANES), lambda b, j, i: (b, prev(i), j)),
            pl.BlockSpec((None, subs * tq, LANES), lambda b, j, i: (b, i, j)),
            pl.BlockSpec((None, tq, LANES), lambda b, j, i: (b, nxt(i), j)),
            pl.BlockSpec((None, None, V_ROWS, n_ctx), lambda b, j, i: (b, j, 0, 0)),
            pl.BlockSpec((None, None, V_ROWS, tq), lambda b, j, i: (b, j, 0, prev(i))),
            pl.BlockSpec((None, None, V_ROWS, subs * tq), lambda b, j, i: (b, j, 0, i)),
            pl.BlockSpec((None, None, V_ROWS, tq), lambda b, j, i: (b, j, 0, nxt(i))),
            _const_spec((3, 3 * tq, D_GROUP * tq)),
            sink_spec,
        ]
        args = (q, ctx_k, k, k, k, ctx_vt, vt, vt, vt, _window_bias(n, tq), sink_rows)
    else:
        in_specs = [
            q_spec,
            pl.BlockSpec((reqs, n, groups * LANES), lambda b, j, i: (b, 0, j)),
            pl.BlockSpec((None, groups, V_ROWS, reqs * n),
                         (lambda b, j, i: (0, j, 0, b)) if flat_values else (lambda b, j, i: (b, j, 0, 0))),
            sink_spec,
        ]
        args = (q, k, vt, sink_rows)
    return pl.pallas_call(
        functools.partial(_sink_attn_kernel, windowed=windowed, tq=tq, nblk=nblk),
        grid=(bsz // reqs, D_KV_HEADS // groups, steps),
        in_specs=in_specs,
        out_specs=q_spec,
        out_shape=jax.ShapeDtypeStruct((bsz, n, DQ_W), BF16),
        compiler_params=_params(3),
        name="sink_attn",
    )(*args)


def _dup_heads(a):
    lead = a.shape[:-1]
    a = a.reshape(lead + (D_KV_HEADS, 1, HEAD_DIM))
    return jnp.broadcast_to(a, lead + (D_KV_HEADS, 2, HEAD_DIM)).reshape(lead + (DKV_DUP_W,))


def _odd_gains(c_qn_g, c_kn_g, d_qn_g, d_kn_g):
    return jnp.concatenate([
        jnp.tile(c_qn_g * (ATTN_SCALE * LOG2E), C_W // HEAD_DIM),
        jnp.tile(c_kn_g, C_W // HEAD_DIM),
        jnp.tile(d_qn_g * (ATTN_SCALE * LOG2E), DQ_W // HEAD_DIM),
        jnp.tile(d_kn_g, DKV_W // HEAD_DIM),
    ])[None, :]


def _value_rows(v):
    bsz, nk, w = v.shape
    vt = v.reshape(bsz, nk, w // LANES, LANES).transpose(0, 2, 3, 1)
    pad = jnp.zeros((bsz, w // LANES, V_ROWS - LANES, nk), v.dtype).at[:, :, 0, :].set(1)
    return jnp.concatenate([vt, pad], axis=2)


def _rope_tables(n):
    rows = n // GRID_W
    row = np.repeat(np.arange(rows), GRID_W).astype(np.float64)
    col = np.tile(np.arange(GRID_W), rows).astype(np.float64)
    freqs = ROPE_BASE ** (-np.arange(ROPE_FREQS, dtype=np.float64) / ROPE_FREQS)
    ang = np.stack([row[:, None] * freqs, col[:, None] * freqs], axis=1)
    ang = np.concatenate([ang, ang], axis=-1).reshape(n, HEAD_DIM)
    ang = np.concatenate([ang, ang], axis=-1)
    return jnp.asarray(np.cos(ang), F32), jnp.asarray(np.sin(ang), F32)


MLP_TILE = 512


def _token_tile(n):
    return min(n, 512)


def _mixer(x, l, mod, row_of_batch, wts, caches, new_cache, side_cast=None):
    bsz, n, _ = x.shape
    decode = caches is not None
    g1n = wts["norm1_g"][l][None, :]
    if l % 2 == 0:
        e = l // 2
        a, z, *copies = _even_in_call(x, mod, l, row_of_batch, g1n, e, wts["w_in_even"], wts["w_pool"],
                                      wts["pool_scale"][e][None, :], wts["dftc"], wts["w_fft"], min(n, 1024),
                                      1 if decode or n > 1024 else next(q for q in (4, 2, 1) if bsz % q == 0),
                                      side_cast)
        y = _dft_two_stage_call(z) if z.shape[-1] == 2 * B_WIDTH else z
        return (a, y) + tuple(copies)
    o = l // 2
    lam_init = _lam_init(l)
    rope_tabs = _rope_tables(n) if decode else None
    xo = x if decode else x.reshape(1, bsz * n, D_MODEL)
    outs = _odd_in_call(xo, mod, l, row_of_batch, g1n, wts["w_in_odd"], o, wts["odd_gains"][o],
                        wts["head_mean"], rope_tabs, _token_tile(xo.shape[1]),
                        None if decode else n)
    if not decode:
        outs = [a if a.ndim == 4 else a.reshape(bsz, n, a.shape[-1]) for a in outs[:6]] + list(outs[6:])
    cq, ck, cvt, dq, dkk, dvt = outs[:6]
    if decode:
        c_k, c_v, d_k, d_v = caches
        lc = c_k.shape[2]
        ks = [c_k[:, o].reshape(bsz, lc, C_W).astype(BF16), ck]
        vts = [_value_rows(c_v[:, o].reshape(bsz, lc, C_W).astype(BF16)), cvt]
        ctx_k = _dup_heads(d_k[:, o].reshape(bsz, lc, DKV_W)).astype(BF16)
        ctx_vt = _value_rows(_dup_heads(d_v[:, o].reshape(bsz, lc, DKV_W)).astype(BF16))
        tq = min(n, 128)
    else:
        new_cache.append(outs[6:])
        ks, vts = [ck], [cvt]
        ctx_k = ctx_vt = None
        tq = n
    tiles = next(t for t in (16, 8, 4, 2, 1) if n % (t * tq) == 0)
    heads = 1 if decode else C_HEADS
    c_out = _diff_attn_call(cq, ks, vts, wts["lamv"][o],
                            wts["c_subln_g"][o][None, :], lam_init, tq, tiles, heads,
                            1 if decode else next(r for r in (4, 2, 1) if bsz % r == 0), not decode)
    sink_tq = WINDOW if decode else n
    sink_rows = jnp.repeat(wts["d_sink"][o].reshape(D_KV_HEADS, D_GROUP) * LOG2E, sink_tq,
                           axis=-1)[:, None, :]
    d_out = _sink_attn_call(dq, dkk, dvt, sink_rows, ctx_k, ctx_vt,
                            1 if decode else next(r for r in (4, 2, 1) if bsz % r == 0), not decode)
    return c_out, d_out


def kernel(x_prompt, x_sample, c, cache_c_k, cache_c_v, cache_d_k, cache_d_v, c_ctx,
           norm1_g, norm2_g, w_ada, b_ada, w_in_even, w_pool, pool_scale, w_fft, w_out_even,
           w_in_odd, c_qn_g, c_kn_g, lam_q1, lam_k1, lam_q2, lam_k2, c_subln_g,
           d_qn_g, d_kn_g, d_sink, w_out_odd, w_mlp1, w_mlp2):
    depth = norm1_g.shape[0]
    n_odd = w_in_odd.shape[0]
    dec_b = c.shape[0]

    rows = -(-(1 + dec_b) // SUBLANES) * SUBLANES
    cond = jnp.concatenate([c_ctx[None, :], c, jnp.zeros((rows - 1 - dec_b, D_MODEL), F32)], axis=0)
    mod = _adaln_call(cond, w_ada, b_ada).reshape(depth, rows, 6, 1, D_MODEL)

    ic = np.arange(GROUP_W)
    ang = 2.0 * np.pi * np.outer(ic, ic) / GROUP_W
    dftc = _bf16_const(np.concatenate([np.cos(ang), -np.sin(ang)], axis=1))
    head_mean = _bf16_const(np.kron(np.eye(2 * LANES // HEAD_DIM), np.ones((HEAD_DIM, HEAD_DIM))) / HEAD_DIM)

    odd_gains = [_odd_gains(c_qn_g[o], c_kn_g[o], d_qn_g[o], d_kn_g[o]) for o in range(n_odd)]
    wts = {
        "norm1_g": norm1_g, "norm2_g": norm2_g,
        "w_in_even": w_in_even, "w_pool": w_pool, "pool_scale": pool_scale,
        "w_fft": w_fft, "w_out_even": w_out_even, "dftc": dftc,
        "w_in_odd": w_in_odd, "odd_gains": odd_gains, "head_mean": head_mean,
        "lamv": jnp.stack([lam_q1, lam_k1, lam_q2, lam_k2], axis=1),
        "c_subln_g": c_subln_g, "d_sink": d_sink, "w_out_odd": w_out_odd,
    }

    bsz, n, _ = x_prompt.shape
    caches = (cache_c_k, cache_c_v, cache_d_k, cache_d_v)
    new_cache = []
    xp, xs = x_prompt, x_sample
    mlp_stacks = (w_mlp1, w_mlp2)
    mlp_w = None
    for l in range(depth):
        pp = _mixer(xp, l, mod, lambda b: 0, wts, None, new_cache)
        ps = _mixer(xs, l, mod, lambda b: 1 + b, wts, caches, new_cache,
                    (mlp_stacks, 0) if l == 0 else None)
        if l == 0:
            mlp_w = ps[2:]
        w_out, out_idx = (wts["w_out_even"], l // 2) if l % 2 == 0 else (wts["w_out_odd"], l // 2)
        flat = lambda a: a.reshape(1, bsz * n, a.shape[-1])
        xp, xs, *mlp_next = _out_mlp_call((flat(xp), flat(pp[0]), flat(pp[1]), 0), (xs, ps[0], ps[1], 1),
                                          mod, l, wts["norm2_g"][l][None, :], w_out, out_idx, *mlp_w,
                                          (mlp_stacks, l + 1) if l + 1 < depth else None)
        mlp_w = mlp_next
        xp = xp.reshape(bsz, n, D_MODEL)
    y_prompt, y_sample = xp, xs
    new_c_k = jnp.stack([nc[0].transpose(0, 4, 1, 2, 3) for nc in new_cache], axis=1)
    new_c_v = jnp.stack([nc[1].reshape(bsz, n, C_HEADS, 2 * HEAD_DIM) for nc in new_cache], axis=1)
    new_d_k = jnp.stack([nc[2].transpose(0, 3, 1, 2) for nc in new_cache], axis=1)
    new_d_v = jnp.stack([nc[3].transpose(0, 3, 1, 2) for nc in new_cache], axis=1)
    return (y_prompt, y_sample, new_c_k, new_c_v, new_d_k, new_d_v)
```
